```python
import math
import jax, jax.numpy as jnp
from jax import lax
import numpy as np

D_MODEL = 1024
BATCH = 32
SEQ = 2048
DEPTH = 4

N_MIXERS = 2
N_CONV_LAYERS = (DEPTH + 1) // 2
N_LRU_LAYERS = DEPTH // 2
SC_WIDTH = 3
LRU_WIDTH = 1280
LRU_HEADS = 10
LRU_BLOCK = LRU_WIDTH // LRU_HEADS
LRU_CONV_WIDTH = 4
LRU_C = 8.0
FFN_HIDDEN = 2816
FFN_CONV_WIDTH = 3
LN_EPS = 1e-5
DEEPNORM_ALPHA = (2.0 * DEPTH) ** 0.25
DEEPNORM_BETA = (8.0 * DEPTH) ** -0.25

kernel_name = "hybrid_shortconv_rglru_convffn_deepnorm"


def causal_dwconv(x, w, b):
    k_width = w.shape[0]
    s = x.shape[1]
    xp = jnp.pad(x, ((0, 0), (k_width - 1, 0), (0, 0)))
    y = xp[:, 0:s] * w[0] + b
    for k in range(1, k_width):
        y = y + xp[:, k:k + s] * w[k]
    return y


def layer_norm(x, g, b):
    xf = x.astype(jnp.float32)
    mu = jnp.mean(xf, axis=-1, keepdims=True)
    var = jnp.mean(jnp.square(xf - mu), axis=-1, keepdims=True)
    y = (xf - mu) * lax.rsqrt(var + LN_EPS)
    return y.astype(x.dtype) * g + b


def short_conv_mixer(x, w_in, conv_w, conv_b, w_out):
    h = jnp.einsum('bsd,de->bse', x, w_in)
    gate_b, gate_c, v = jnp.split(h, 3, axis=-1)
    u = causal_dwconv(gate_c * v, conv_w, conv_b)
    return jnp.einsum('bsd,de->bse', gate_b * u, w_out)


def _lru_combine(left, right):
    a_l, b_l = left
    a_r, b_r = right
    return a_l * a_r, a_r * b_l + b_r


def rglru_block(x, w_in, b_in, conv_w, conv_b, w_gate, b_gate, lam, w_out):
    bsz, s, _ = x.shape
    h = jnp.einsum('bsd,de->bse', x, w_in) + b_in
    g_branch, r_branch = jnp.split(h, 2, axis=-1)
    xr = causal_dwconv(r_branch, conv_w, conv_b)
    xh = xr.reshape(bsz, s, LRU_HEADS, LRU_BLOCK)
    gates = jnp.einsum('bshi,hio->bsho', xh, w_gate) + b_gate
    r_gate, i_gate = jnp.split(gates.astype(jnp.float32), 2, axis=-1)
    r_gate = jax.nn.sigmoid(r_gate).reshape(bsz, s, LRU_WIDTH)
    i_gate = jax.nn.sigmoid(i_gate).reshape(bsz, s, LRU_WIDTH)
    log_a = -LRU_C * r_gate * jax.nn.softplus(-lam.astype(jnp.float32))
    a = jnp.exp(log_a)
    mult = jnp.sqrt(-jnp.expm1(2.0 * log_a))
    b = mult * (i_gate * xr.astype(jnp.float32))
    _, hs = lax.associative_scan(_lru_combine, (a, b), axis=1)
    y = hs.astype(x.dtype) * jax.nn.gelu(g_branch, approximate=True)
    return jnp.einsum('bsr,rd->bsd', y, w_out)


def conv_ffn(x, w_up, conv_w, conv_b, w_down):
    h = jnp.einsum('bsd,df->bsf', x, w_up)
    h = causal_dwconv(h, conv_w, conv_b)
    g, v = jnp.split(h, 2, axis=-1)
    return jnp.einsum('bsf,fd->bsd', jax.nn.silu(g) * v, w_down)


def _fwd_setup_inputs(seed: int = 0) -> dict:
    key = jax.random.key(seed)
    ks = jax.random.split(key, 24)
    d, r, f = D_MODEL, LRU_WIDTH, FFN_HIDDEN
    nA, nB, L = N_CONV_LAYERS, N_LRU_LAYERS, DEPTH
    nrm = jax.random.normal
    x = nrm(ks[0], (BATCH, SEQ, d), jnp.float32)
    sc_w_in = nrm(ks[1], (nA, d, 3 * d), jnp.float32) * d ** -0.5
    sc_conv_w = nrm(ks[2], (nA, SC_WIDTH, d), jnp.float32) * SC_WIDTH ** -0.5
    sc_conv_b = nrm(ks[3], (nA, d), jnp.float32) * 0.01
    sc_w_out = nrm(ks[4], (nA, d, d), jnp.float32) * d ** -0.5 * DEEPNORM_BETA
    lru_w_in = nrm(ks[5], (nB, d, 2 * r), jnp.float32) * d ** -0.5
    lru_b_in = nrm(ks[6], (nB, 2 * r), jnp.float32) * 0.01
    lru_conv_w = nrm(ks[7], (nB, LRU_CONV_WIDTH, r), jnp.float32) * LRU_CONV_WIDTH ** -0.5
    lru_conv_b = nrm(ks[8], (nB, r), jnp.float32) * 0.01
    lru_w_gate = nrm(ks[9], (nB, LRU_HEADS, LRU_BLOCK, 2 * LRU_BLOCK), jnp.float32) * LRU_BLOCK ** -0.5
    lru_b_gate = nrm(ks[10], (nB, LRU_HEADS, 2 * LRU_BLOCK), jnp.float32) * 0.01
    u = jax.random.uniform(ks[11], (nB, r), jnp.float32, 0.9, 0.999)
    p = u ** (1.0 / LRU_C)
    lru_lambda = jnp.log(p) - jnp.log1p(-p)
    lru_w_out = nrm(ks[12], (nB, r, d), jnp.float32) * r ** -0.5 * DEEPNORM_BETA
    ffn_w_up = nrm(ks[13], (L, d, 2 * f), jnp.float32) * d ** -0.5
    ffn_conv_w = nrm(ks[14], (L, FFN_CONV_WIDTH, 2 * f), jnp.float32) * FFN_CONV_WIDTH ** -0.5
    ffn_conv_b = nrm(ks[15], (L, 2 * f), jnp.float32) * 0.01
    ffn_w_down = nrm(ks[16], (L, f, d), jnp.float32) * f ** -0.5 * DEEPNORM_BETA
    ln_g = 1.0 + 0.02 * nrm(ks[17], (L, 2, d), jnp.float32)
    ln_b = 0.02 * nrm(ks[18], (L, 2, d), jnp.float32)
    return {"x": x,
            "sc_w_in": sc_w_in, "sc_conv_w": sc_conv_w, "sc_conv_b": sc_conv_b, "sc_w_out": sc_w_out,
            "lru_w_in": lru_w_in, "lru_b_in": lru_b_in, "lru_conv_w": lru_conv_w, "lru_conv_b": lru_conv_b,
            "lru_w_gate": lru_w_gate, "lru_b_gate": lru_b_gate, "lru_lambda": lru_lambda, "lru_w_out": lru_w_out,
            "ffn_w_up": ffn_w_up, "ffn_conv_w": ffn_conv_w, "ffn_conv_b": ffn_conv_b, "ffn_w_down": ffn_w_down,
            "ln_g": ln_g, "ln_b": ln_b}


def _fwd_reference(x, sc_w_in, sc_conv_w, sc_conv_b, sc_w_out,
              lru_w_in, lru_b_in, lru_conv_w, lru_conv_b, lru_w_gate, lru_b_gate, lru_lambda, lru_w_out,
              ffn_w_up, ffn_conv_w, ffn_conv_b, ffn_w_down, ln_g, ln_b):
    for i in range(DEPTH):
        j = i // N_MIXERS
        if i % N_MIXERS == 0:
            y = short_conv_mixer(x, sc_w_in[j], sc_conv_w[j], sc_conv_b[j], sc_w_out[j])
        else:
            y = rglru_block(x, lru_w_in[j], lru_b_in[j], lru_conv_w[j], lru_conv_b[j],
                            lru_w_gate[j], lru_b_gate[j], lru_lambda[j], lru_w_out[j])
        x = layer_norm(DEEPNORM_ALPHA * x + y, ln_g[i, 0], ln_b[i, 0])
        y = conv_ffn(x, ffn_w_up[i], ffn_conv_w[i], ffn_conv_b[i], ffn_w_down[i])
        x = layer_norm(DEEPNORM_ALPHA * x + y, ln_g[i, 1], ln_b[i, 1])
    return x


import jax as _jax
import jax.numpy as _jnp

TWIN_FORMAT = 'train_step'
FWD_PARAMS = ['x', 'sc_w_in', 'sc_conv_w', 'sc_conv_b', 'sc_w_out', 'lru_w_in', 'lru_b_in', 'lru_conv_w', 'lru_conv_b', 'lru_w_gate', 'lru_b_gate', 'lru_lambda', 'lru_w_out', 'ffn_w_up', 'ffn_conv_w', 'ffn_conv_b', 'ffn_w_down', 'ln_g', 'ln_b']
TWIN_WEIGHTS = ['sc_w_in', 'sc_conv_w', 'sc_conv_b', 'sc_w_out', 'lru_w_in', 'lru_b_in', 'lru_conv_w', 'lru_conv_b', 'lru_w_gate', 'lru_b_gate', 'lru_lambda', 'lru_w_out', 'ffn_w_up', 'ffn_conv_w', 'ffn_conv_b', 'ffn_w_down', 'ln_g', 'ln_b']
TWIN_DIFF_INPUT = 'x'
TWIN_INPUTS = ['x', 'sc_w_in', 'sc_conv_w', 'sc_conv_b', 'sc_w_out', 'lru_w_in', 'lru_b_in', 'lru_conv_w', 'lru_conv_b', 'lru_w_gate', 'lru_b_gate', 'lru_lambda', 'lru_w_out', 'ffn_w_up', 'ffn_conv_w', 'ffn_conv_b', 'ffn_w_down', 'ln_g', 'ln_b', 'loss_target', 'm_sc_w_in', 'm_sc_conv_w', 'm_sc_conv_b', 'm_sc_w_out', 'm_lru_w_in', 'm_lru_b_in', 'm_lru_conv_w', 'm_lru_conv_b', 'm_lru_w_gate', 'm_lru_b_gate', 'm_lru_lambda', 'm_lru_w_out', 'm_ffn_w_up', 'm_ffn_conv_w', 'm_ffn_conv_b', 'm_ffn_w_down', 'm_ln_g', 'm_ln_b', 'v_sc_w_in', 'v_sc_conv_w', 'v_sc_conv_b', 'v_sc_w_out', 'v_lru_w_in', 'v_lru_b_in', 'v_lru_conv_w', 'v_lru_conv_b', 'v_lru_w_gate', 'v_lru_b_gate', 'v_lru_lambda', 'v_lru_w_out', 'v_ffn_w_up', 'v_ffn_conv_w', 'v_ffn_conv_b', 'v_ffn_w_down', 'v_ln_g', 'v_ln_b']
TWIN_OUTPUTS = ['loss', 'grad_x', 'grad_sc_w_in', 'grad_sc_conv_w', 'grad_sc_conv_b', 'grad_sc_w_out', 'grad_lru_w_in', 'grad_lru_b_in', 'grad_lru_conv_w', 'grad_lru_conv_b', 'grad_lru_w_gate', 'grad_lru_b_gate', 'grad_lru_lambda', 'grad_lru_w_out', 'grad_ffn_w_up', 'grad_ffn_conv_w', 'grad_ffn_conv_b', 'grad_ffn_w_down', 'grad_ln_g', 'grad_ln_b', 'delta_sc_w_in', 'delta_sc_conv_w', 'delta_sc_conv_b', 'delta_sc_w_out', 'delta_lru_w_in', 'delta_lru_b_in', 'delta_lru_conv_w', 'delta_lru_conv_b', 'delta_lru_w_gate', 'delta_lru_b_gate', 'delta_lru_lambda', 'delta_lru_w_out', 'delta_ffn_w_up', 'delta_ffn_conv_w', 'delta_ffn_conv_b', 'delta_ffn_w_down', 'delta_ln_g', 'delta_ln_b', 'new_m_sc_w_in', 'new_m_sc_conv_w', 'new_m_sc_conv_b', 'new_m_sc_w_out', 'new_m_lru_w_in', 'new_m_lru_b_in', 'new_m_lru_conv_w', 'new_m_lru_conv_b', 'new_m_lru_w_gate', 'new_m_lru_b_gate', 'new_m_lru_lambda', 'new_m_lru_w_out', 'new_m_ffn_w_up', 'new_m_ffn_conv_w', 'new_m_ffn_conv_b', 'new_m_ffn_w_down', 'new_m_ln_g', 'new_m_ln_b', 'new_v_sc_w_in', 'new_v_sc_conv_w', 'new_v_sc_conv_b', 'new_v_sc_w_out', 'new_v_lru_w_in', 'new_v_lru_b_in', 'new_v_lru_conv_w', 'new_v_lru_conv_b', 'new_v_lru_w_gate', 'new_v_lru_b_gate', 'new_v_lru_lambda', 'new_v_lru_w_out', 'new_v_ffn_w_up', 'new_v_ffn_conv_w', 'new_v_ffn_conv_b', 'new_v_ffn_w_down', 'new_v_ln_g', 'new_v_ln_b']
TWIN_LEAF_KINDS = {'loss': 'loss', 'grad_x': 'grad_x', 'grad_sc_w_in': 'grad_w', 'grad_sc_conv_w': 'grad_w', 'grad_sc_conv_b': 'grad_w', 'grad_sc_w_out': 'grad_w', 'grad_lru_w_in': 'grad_w', 'grad_lru_b_in': 'grad_w', 'grad_lru_conv_w': 'grad_w', 'grad_lru_conv_b': 'grad_w', 'grad_lru_w_gate': 'grad_w', 'grad_lru_b_gate': 'grad_w', 'grad_lru_lambda': 'grad_w', 'grad_lru_w_out': 'grad_w', 'grad_ffn_w_up': 'grad_w', 'grad_ffn_conv_w': 'grad_w', 'grad_ffn_conv_b': 'grad_w', 'grad_ffn_w_down': 'grad_w', 'grad_ln_g': 'grad_w', 'grad_ln_b': 'grad_w', 'delta_sc_w_in': 'delta_w', 'delta_sc_conv_w': 'delta_w', 'delta_sc_conv_b': 'delta_w', 'delta_sc_w_out': 'delta_w', 'delta_lru_w_in': 'delta_w', 'delta_lru_b_in': 'delta_w', 'delta_lru_conv_w': 'delta_w', 'delta_lru_conv_b': 'delta_w', 'delta_lru_w_gate': 'delta_w', 'delta_lru_b_gate': 'delta_w', 'delta_lru_lambda': 'delta_w', 'delta_lru_w_out': 'delta_w', 'delta_ffn_w_up': 'delta_w', 'delta_ffn_conv_w': 'delta_w', 'delta_ffn_conv_b': 'delta_w', 'delta_ffn_w_down': 'delta_w', 'delta_ln_g': 'delta_w', 'delta_ln_b': 'delta_w', 'new_m_sc_w_in': 'new_m', 'new_m_sc_conv_w': 'new_m', 'new_m_sc_conv_b': 'new_m', 'new_m_sc_w_out': 'new_m', 'new_m_lru_w_in': 'new_m', 'new_m_lru_b_in': 'new_m', 'new_m_lru_conv_w': 'new_m', 'new_m_lru_conv_b': 'new_m', 'new_m_lru_w_gate': 'new_m', 'new_m_lru_b_gate': 'new_m', 'new_m_lru_lambda': 'new_m', 'new_m_lru_w_out': 'new_m', 'new_m_ffn_w_up': 'new_m', 'new_m_ffn_conv_w': 'new_m', 'new_m_ffn_conv_b': 'new_m', 'new_m_ffn_w_down': 'new_m', 'new_m_ln_g': 'new_m', 'new_m_ln_b': 'new_m', 'new_v_sc_w_in': 'new_v', 'new_v_sc_conv_w': 'new_v', 'new_v_sc_conv_b': 'new_v', 'new_v_sc_w_out': 'new_v', 'new_v_lru_w_in': 'new_v', 'new_v_lru_b_in': 'new_v', 'new_v_lru_conv_w': 'new_v', 'new_v_lru_conv_b': 'new_v', 'new_v_lru_w_gate': 'new_v', 'new_v_lru_b_gate': 'new_v', 'new_v_lru_lambda': 'new_v', 'new_v_lru_w_out': 'new_v', 'new_v_ffn_w_up': 'new_v', 'new_v_ffn_conv_w': 'new_v', 'new_v_ffn_conv_b': 'new_v', 'new_v_ffn_w_down': 'new_v', 'new_v_ln_g': 'new_v', 'new_v_ln_b': 'new_v'}


def _forward(args):
    return _fwd_reference(*[args[k] for k in FWD_PARAMS])


def _output_shape():
    out = _jax.eval_shape(lambda: _forward(_fwd_setup_inputs(0)))
    return out.shape, out.dtype

N_MICROBATCH = 1
ADAM_LR = 0.001
ADAM_B1 = 0.9
ADAM_B2 = 0.999
ADAM_EPS = 1e-08
ADAM_WD = 0.01
ADAM_STEP = 10
PER_EXAMPLE_BATCH_AXIS = {'x': 0, 'loss_target': 0}
SHARED_INPUTS = []
_WEIGHT_DTYPES = {'sc_w_in': _jnp.float32, 'sc_conv_w': _jnp.float32, 'sc_conv_b': _jnp.float32, 'sc_w_out': _jnp.float32, 'lru_w_in': _jnp.float32, 'lru_b_in': _jnp.float32, 'lru_conv_w': _jnp.float32, 'lru_conv_b': _jnp.float32, 'lru_w_gate': _jnp.float32, 'lru_b_gate': _jnp.float32, 'lru_lambda': _jnp.float32, 'lru_w_out': _jnp.float32, 'ffn_w_up': _jnp.float32, 'ffn_conv_w': _jnp.float32, 'ffn_conv_b': _jnp.float32, 'ffn_w_down': _jnp.float32, 'ln_g': _jnp.float32, 'ln_b': _jnp.float32}
MOMENT_SCALE = {'sc_w_in': 6.601351e-02, 'sc_conv_w': 6.648357e-02, 'sc_conv_b': 7.029321e-02, 'sc_w_out': 1.570499e-01, 'lru_w_in': 3.101806e-02, 'lru_b_in': 3.993949e-01, 'lru_conv_w': 3.552181e-02, 'lru_conv_b': 5.169463e-01, 'lru_w_gate': 1.741352e-02, 'lru_b_gate': 1.120752e-02, 'lru_lambda': 1.938120e-02, 'lru_w_out': 9.099412e-02, 'ffn_w_up': 2.433454e-02, 'ffn_conv_w': 2.475954e-02, 'ffn_conv_b': 2.898979e-02, 'ffn_w_down': 9.446507e-02, 'ln_g': 2.267112e+01, 'ln_b': 1.392146e+00}


def _to_microbatches(a, axis):
    t = _jnp.moveaxis(a, axis, 0)
    t = t.reshape((N_MICROBATCH, t.shape[0] // N_MICROBATCH) + t.shape[1:])
    return _jnp.moveaxis(t, 1, axis + 1)


def setup_inputs(seed: int = 0) -> dict:
    inp = _fwd_setup_inputs(seed)
    key = _jax.random.fold_in(_jax.random.key(seed), 7919)
    shape, _ = _output_shape()
    out = dict(inp)
    out["loss_target"] = _jax.random.normal(_jax.random.fold_in(key, 0), shape, _jnp.float32)
    for i, name in enumerate(TWIN_WEIGHTS):
        w = inp[name].astype(_jnp.float32)
        if MOMENT_SCALE is None:
            s = _jnp.sqrt(_jnp.mean(_jnp.square(w)) + 1e-30)
        else:
            s = MOMENT_SCALE[name]
        km, kv = _jax.random.split(_jax.random.fold_in(key, i + 1))
        out[name] = w
        out["m_" + name] = s * _jax.random.normal(km, w.shape, _jnp.float32)
        out["v_" + name] = (s * s) * _jax.random.uniform(kv, w.shape, _jnp.float32, 0.5, 1.5)
    if N_MICROBATCH > 1:
        for name, axis in PER_EXAMPLE_BATCH_AXIS.items():
            out[name] = _to_microbatches(out[name], axis)
    return {'x': out['x'], 'sc_w_in': out['sc_w_in'], 'sc_conv_w': out['sc_conv_w'], 'sc_conv_b': out['sc_conv_b'], 'sc_w_out': out['sc_w_out'], 'lru_w_in': out['lru_w_in'], 'lru_b_in': out['lru_b_in'], 'lru_conv_w': out['lru_conv_w'], 'lru_conv_b': out['lru_conv_b'], 'lru_w_gate': out['lru_w_gate'], 'lru_b_gate': out['lru_b_gate'], 'lru_lambda': out['lru_lambda'], 'lru_w_out': out['lru_w_out'], 'ffn_w_up': out['ffn_w_up'], 'ffn_conv_w': out['ffn_conv_w'], 'ffn_conv_b': out['ffn_conv_b'], 'ffn_w_down': out['ffn_w_down'], 'ln_g': out['ln_g'], 'ln_b': out['ln_b'], 'loss_target': out['loss_target'], 'm_sc_w_in': out['m_sc_w_in'], 'm_sc_conv_w': out['m_sc_conv_w'], 'm_sc_conv_b': out['m_sc_conv_b'], 'm_sc_w_out': out['m_sc_w_out'], 'm_lru_w_in': out['m_lru_w_in'], 'm_lru_b_in': out['m_lru_b_in'], 'm_lru_conv_w': out['m_lru_conv_w'], 'm_lru_conv_b': out['m_lru_conv_b'], 'm_lru_w_gate': out['m_lru_w_gate'], 'm_lru_b_gate': out['m_lru_b_gate'], 'm_lru_lambda': out['m_lru_lambda'], 'm_lru_w_out': out['m_lru_w_out'], 'm_ffn_w_up': out['m_ffn_w_up'], 'm_ffn_conv_w': out['m_ffn_conv_w'], 'm_ffn_conv_b': out['m_ffn_conv_b'], 'm_ffn_w_down': out['m_ffn_w_down'], 'm_ln_g': out['m_ln_g'], 'm_ln_b': out['m_ln_b'], 'v_sc_w_in': out['v_sc_w_in'], 'v_sc_conv_w': out['v_sc_conv_w'], 'v_sc_conv_b': out['v_sc_conv_b'], 'v_sc_w_out': out['v_sc_w_out'], 'v_lru_w_in': out['v_lru_w_in'], 'v_lru_b_in': out['v_lru_b_in'], 'v_lru_conv_w': out['v_lru_conv_w'], 'v_lru_conv_b': out['v_lru_conv_b'], 'v_lru_w_gate': out['v_lru_w_gate'], 'v_lru_b_gate': out['v_lru_b_gate'], 'v_lru_lambda': out['v_lru_lambda'], 'v_lru_w_out': out['v_lru_w_out'], 'v_ffn_w_up': out['v_ffn_w_up'], 'v_ffn_conv_w': out['v_ffn_conv_w'], 'v_ffn_conv_b': out['v_ffn_conv_b'], 'v_ffn_w_down': out['v_ffn_w_down'], 'v_ln_g': out['v_ln_g'], 'v_ln_b': out['v_ln_b']}


def _loss(weights, diff, rest, loss_target):
    with _jax.named_scope("forward"):
        args = {**rest, TWIN_DIFF_INPUT: diff, **{k: w.astype(_WEIGHT_DTYPES[k]) for k, w in weights.items()}}
        y = _forward(args)
    with _jax.named_scope("loss_head"):
        err = _jnp.square(y.astype(_jnp.float32) - loss_target)
        return 0.5 * _jnp.sum(_jnp.mean(err, axis=-1)) if err.ndim else 0.5 * err


def _adamw(w, g, m, v):
    m = ADAM_B1 * m + (1.0 - ADAM_B1) * g
    v = ADAM_B2 * v + (1.0 - ADAM_B2) * _jnp.square(g)
    m_hat = m / (1.0 - ADAM_B1 ** ADAM_STEP)
    v_hat = v / (1.0 - ADAM_B2 ** ADAM_STEP)
    delta = -ADAM_LR * (m_hat / (_jnp.sqrt(v_hat) + ADAM_EPS) + ADAM_WD * w)
    return delta, m, v


def reference(x, sc_w_in, sc_conv_w, sc_conv_b, sc_w_out, lru_w_in, lru_b_in, lru_conv_w, lru_conv_b, lru_w_gate, lru_b_gate, lru_lambda, lru_w_out, ffn_w_up, ffn_conv_w, ffn_conv_b, ffn_w_down, ln_g, ln_b, loss_target, m_sc_w_in, m_sc_conv_w, m_sc_conv_b, m_sc_w_out, m_lru_w_in, m_lru_b_in, m_lru_conv_w, m_lru_conv_b, m_lru_w_gate, m_lru_b_gate, m_lru_lambda, m_lru_w_out, m_ffn_w_up, m_ffn_conv_w, m_ffn_conv_b, m_ffn_w_down, m_ln_g, m_ln_b, v_sc_w_in, v_sc_conv_w, v_sc_conv_b, v_sc_w_out, v_lru_w_in, v_lru_b_in, v_lru_conv_w, v_lru_conv_b, v_lru_w_gate, v_lru_b_gate, v_lru_lambda, v_lru_w_out, v_ffn_w_up, v_ffn_conv_w, v_ffn_conv_b, v_ffn_w_down, v_ln_g, v_ln_b):
    given = dict(x=x, sc_w_in=sc_w_in, sc_conv_w=sc_conv_w, sc_conv_b=sc_conv_b, sc_w_out=sc_w_out, lru_w_in=lru_w_in, lru_b_in=lru_b_in, lru_conv_w=lru_conv_w, lru_conv_b=lru_conv_b, lru_w_gate=lru_w_gate, lru_b_gate=lru_b_gate, lru_lambda=lru_lambda, lru_w_out=lru_w_out, ffn_w_up=ffn_w_up, ffn_conv_w=ffn_conv_w, ffn_conv_b=ffn_conv_b, ffn_w_down=ffn_w_down, ln_g=ln_g, ln_b=ln_b, loss_target=loss_target, m_sc_w_in=m_sc_w_in, m_sc_conv_w=m_sc_conv_w, m_sc_conv_b=m_sc_conv_b, m_sc_w_out=m_sc_w_out, m_lru_w_in=m_lru_w_in, m_lru_b_in=m_lru_b_in, m_lru_conv_w=m_lru_conv_w, m_lru_conv_b=m_lru_conv_b, m_lru_w_gate=m_lru_w_gate, m_lru_b_gate=m_lru_b_gate, m_lru_lambda=m_lru_lambda, m_lru_w_out=m_lru_w_out, m_ffn_w_up=m_ffn_w_up, m_ffn_conv_w=m_ffn_conv_w, m_ffn_conv_b=m_ffn_conv_b, m_ffn_w_down=m_ffn_w_down, m_ln_g=m_ln_g, m_ln_b=m_ln_b, v_sc_w_in=v_sc_w_in, v_sc_conv_w=v_sc_conv_w, v_sc_conv_b=v_sc_conv_b, v_sc_w_out=v_sc_w_out, v_lru_w_in=v_lru_w_in, v_lru_b_in=v_lru_b_in, v_lru_conv_w=v_lru_conv_w, v_lru_conv_b=v_lru_conv_b, v_lru_w_gate=v_lru_w_gate, v_lru_b_gate=v_lru_b_gate, v_lru_lambda=v_lru_lambda, v_lru_w_out=v_lru_w_out, v_ffn_w_up=v_ffn_w_up, v_ffn_conv_w=v_ffn_conv_w, v_ffn_conv_b=v_ffn_conv_b, v_ffn_w_down=v_ffn_w_down, v_ln_g=v_ln_g, v_ln_b=v_ln_b)
    weights = {n: given[n] for n in TWIN_WEIGHTS}
    shared = {n: given[n] for n in SHARED_INPUTS}
    per_example = {n: given[n] for n in ['x']}
    grad_fn = _jax.value_and_grad(_loss, argnums=(0, 1))

    def one_microbatch(ex, loss_target):
        ex = dict(ex)
        diff = ex.pop(TWIN_DIFF_INPUT)
        return grad_fn(weights, diff, {**shared, **ex}, loss_target)

    if N_MICROBATCH == 1:
        loss, (grad_w, grad_x) = one_microbatch(per_example, given["loss_target"])
    else:
        def body(carry, xs):
            loss_sum, grad_sum = carry
            l_k, (gw_k, gx_k) = one_microbatch(xs[0], xs[1])
            with _jax.named_scope("update"):
                return (loss_sum + l_k, _jax.tree.map(_jnp.add, grad_sum, gw_k)), gx_k

        init = (_jnp.zeros((), _jnp.float32), _jax.tree.map(_jnp.zeros_like, weights))
        (loss, grad_w), grad_x = _jax.lax.scan(body, init, (per_example, given["loss_target"]))
    with _jax.named_scope("update"):
        delta_w, new_m, new_v = {}, {}, {}
        for n in TWIN_WEIGHTS:
            delta_w[n], new_m[n], new_v[n] = _adamw(weights[n], grad_w[n], given["m_" + n], given["v_" + n])
    return (loss, grad_x, *[grad_w[n] for n in TWIN_WEIGHTS], *[delta_w[n] for n in TWIN_WEIGHTS],
            *[new_m[n] for n in TWIN_WEIGHTS], *[new_v[n] for n in TWIN_WEIGHTS])
```

```python
import math

import jax
import jax.numpy as jnp
from jax import lax
from jax.experimental import pallas as pl
from jax.experimental.pallas import tpu as pltpu

F32 = jnp.float32
BF16 = jnp.bfloat16

DEPTH = 4
LRU_HEADS = 10
LRU_BLOCK = 128
LRU_C = 8.0
LN_EPS = 1e-5
ALPHA = (2.0 * DEPTH) ** 0.25
ADAM_LR, ADAM_B1, ADAM_B2, ADAM_EPS, ADAM_WD, ADAM_STEP = 0.001, 0.9, 0.999, 1e-08, 0.01, 10
N_CHIPS = 4
MESH_AXES = ("x", "y", "c")

VMEM_LIMIT_BYTES = 48 * 1024 * 1024
TM_MM = 512
TT_MM = 512
TM_SC = 256
TM_FFN = 256
TS_LRU = 128
TM_LN = 512
ELEMS_PER_BLOCK = 256 * 1024
SUB = 8
SUB16 = 16


def _cp(*sem):
    return pltpu.CompilerParams(dimension_semantics=sem, vmem_limit_bytes=VMEM_LIMIT_BYTES)


def _sigmoid(v):
    return 1.0 / (1.0 + jnp.exp(-v))


def _softplus(v):
    e = jnp.exp(-jnp.abs(v))
    log1p = jnp.where(e < 1e-3, e * (1.0 - e * (0.5 - e * (1.0 / 3.0))), jnp.log(1.0 + e))
    return jnp.maximum(v, 0.0) + log1p


def _one_minus_exp(v):
    series = -v * (1.0 + v * (0.5 + v * (1.0 / 6.0 + v * (1.0 / 24.0))))
    return jnp.where(v > -0.02, series, 1.0 - jnp.exp(v))


def _gelu_and_grad(v):
    k = math.sqrt(2.0 / math.pi)
    t = jnp.tanh(k * (v + 0.044715 * v * v * v))
    val = 0.5 * v * (1.0 + t)
    grad = 0.5 * (1.0 + t) + 0.5 * v * (1.0 - t * t) * k * (1.0 + 3.0 * 0.044715 * v * v)
    return val, grad


def _down(ext, k, n_head):
    if k:
        ext = pltpu.roll(ext, k, 0)
    return ext[n_head:]


def _up(ext, k, n):
    if k:
        ext = pltpu.roll(ext, ext.shape[0] - k, 0)
    return ext[:n]


def _row(ref, k):
    return ref[k:k + 1, :]


def _colsum(v):
    return jnp.sum(v, axis=0, keepdims=True)


def _mm_nn(a, w3, l, bias, tn, name):
    m, k = a.shape
    n = w3.shape[2]
    tm = min(TM_MM, m)
    has_bias = bias is not None

    def body(*refs):
        if has_bias:
            a_ref, w_ref, b_ref, o_ref = refs
        else:
            a_ref, w_ref, o_ref = refs
        acc = jnp.dot(a_ref[...], w_ref[...], preferred_element_type=F32)
        if has_bias:
            acc = acc + b_ref[...]
        o_ref[...] = acc.astype(o_ref.dtype)

    in_specs = [pl.BlockSpec((tm, k), lambda i, j: (i, 0)),
                pl.BlockSpec((None, k, tn), lambda i, j: (l, 0, j))]
    args = [a, w3]
    if has_bias:
        in_specs.append(pl.BlockSpec((1, tn), lambda i, j: (0, j)))
        args.append(bias)
    return pl.pallas_call(
        body, name=name, grid=(m // tm, n // tn), in_specs=in_specs,
        out_specs=pl.BlockSpec((tm, tn), lambda i, j: (i, j)),
        out_shape=jax.ShapeDtypeStruct((m, n), BF16),
        compiler_params=_cp("parallel", "arbitrary"))(*args)


def _mm_nn_ln(a, w3, l, xres, g, b, name):
    m, k = a.shape
    n = w3.shape[2]
    tm = min(TM_MM, m)

    def body(a_ref, w_ref, x_ref, g_ref, b_ref, z_ref, xn_ref, xb_ref):
        y = jnp.dot(a_ref[...], w_ref[...], preferred_element_type=F32)
        z = ALPHA * x_ref[...] + y
        mu = jnp.mean(z, axis=-1, keepdims=True)
        zc = z - mu
        var = jnp.mean(zc * zc, axis=-1, keepdims=True)
        xn = zc * lax.rsqrt(var + LN_EPS) * g_ref[...] + b_ref[...]
        z_ref[...] = z
        xn_ref[...] = xn
        xb_ref[...] = xn.astype(BF16)

    row = pl.BlockSpec((tm, n), lambda i: (i, 0))
    vec = pl.BlockSpec((1, n), lambda i: (0, 0))
    return pl.pallas_call(
        body, name=name, grid=(m // tm,),
        in_specs=[pl.BlockSpec((tm, k), lambda i: (i, 0)),
                  pl.BlockSpec((None, k, n), lambda i: (l, 0, 0)), row, vec, vec],
        out_specs=[row, row, row],
        out_shape=[jax.ShapeDtypeStruct((m, n), F32), jax.ShapeDtypeStruct((m, n), F32),
                   jax.ShapeDtypeStruct((m, n), BF16)],
        compiler_params=_cp("parallel"))(a, w3, xres, g, b)


def _mm_nt(a, w3, l, tk, name):
    m, n = a.shape
    kd = w3.shape[1]
    tm = min(TM_MM, m)

    def body(a_ref, w_ref, o_ref):
        o_ref[...] = lax.dot_general(a_ref[...], w_ref[...], (((1,), (1,)), ((), ())),
                                     preferred_element_type=F32).astype(o_ref.dtype)

    return pl.pallas_call(
        body, name=name, grid=(m // tm, kd // tk),
        in_specs=[pl.BlockSpec((tm, n), lambda i, j: (i, 0)),
                  pl.BlockSpec((None, tk, n), lambda i, j: (l, j, 0))],
        out_specs=pl.BlockSpec((tm, tk), lambda i, j: (i, j)),
        out_shape=jax.ShapeDtypeStruct((m, kd), BF16),
        compiler_params=_cp("parallel", "arbitrary"))(a, w3)


def _mm_nt_res(dh3, w3, l, dz, tc, name):
    g, m, cg = dh3.shape
    kd = w3.shape[1]
    ncg = cg // tc
    nk = g * ncg
    tm = min(TM_MM, m)

    def body(a_ref, w_ref, dz_ref, o_ref, acc):
        k = pl.program_id(1)

        @pl.when(k == 0)
        def _():
            acc[...] = ALPHA * dz_ref[...]

        acc[...] += lax.dot_general(a_ref[...], w_ref[...], (((1,), (1,)), ((), ())),
                                    preferred_element_type=F32)

        @pl.when(k == nk - 1)
        def _():
            o_ref[...] = acc[...]

    return pl.pallas_call(
        body, name=name, grid=(m // tm, nk),
        in_specs=[pl.BlockSpec((None, tm, tc), lambda i, k: (k // ncg, i, k % ncg)),
                  pl.BlockSpec((None, kd, tc), lambda i, k: (l, 0, k)),
                  pl.BlockSpec((tm, kd), lambda i, k: (i, 0))],
        out_specs=pl.BlockSpec((tm, kd), lambda i, k: (i, 0)),
        out_shape=jax.ShapeDtypeStruct((m, kd), F32),
        scratch_shapes=[pltpu.VMEM((tm, kd), F32)],
        compiler_params=_cp("parallel", "arbitrary"))(dh3, w3, dz)


def _mm_tn(a, b3, tka, tnb, name):
    t, ka = a.shape
    g, _, cg = b3.shape
    ncg = cg // tnb
    tt = min(TT_MM, t)
    nt = t // tt

    def body(a_ref, b_ref, o_ref, acc):
        s = pl.program_id(2)

        @pl.when(s == 0)
        def _():
            acc[...] = jnp.zeros_like(acc)

        acc[...] += lax.dot_general(a_ref[...], b_ref[...], (((0,), (0,)), ((), ())),
                                    preferred_element_type=F32)

        @pl.when(s == nt - 1)
        def _():
            o_ref[...] = acc[...].astype(o_ref.dtype)

    return pl.pallas_call(
        body, name=name, grid=(ka // tka, g * ncg, nt),
        in_specs=[pl.BlockSpec((tt, tka), lambda i, j, s: (s, i)),
                  pl.BlockSpec((None, tt, tnb), lambda i, j, s: (j // ncg, s, j % ncg))],
        out_specs=pl.BlockSpec((tka, tnb), lambda i, j, s: (i, j)),
        out_shape=jax.ShapeDtypeStruct((ka, g * cg), BF16),
        scratch_shapes=[pltpu.VMEM((tka, tnb), F32)],
        compiler_params=_cp("parallel", "parallel", "arbitrary"))(a, b3)


def _ln_bwd(dxn, z, g, name):
    m, d = z.shape
    tm = min(TM_LN, m)

    def body(dx_ref, z_ref, g_ref, dz_ref, dzb_ref, dg_ref, db_ref):
        @pl.when(pl.program_id(0) == 0)
        def _():
            dg_ref[...] = jnp.zeros_like(dg_ref)
            db_ref[...] = jnp.zeros_like(db_ref)

        zz = z_ref[...]
        dx = dx_ref[...]
        mu = jnp.mean(zz, axis=-1, keepdims=True)
        zc = zz - mu
        var = jnp.mean(zc * zc, axis=-1, keepdims=True)
        rstd = lax.rsqrt(var + LN_EPS)
        xh = zc * rstd
        dg_ref[...] += _colsum(dx * xh)
        db_ref[...] += _colsum(dx)
        dxh = dx * g_ref[...]
        m1 = jnp.mean(dxh, axis=-1, keepdims=True)
        m2 = jnp.mean(dxh * xh, axis=-1, keepdims=True)
        dz = rstd * (dxh - m1 - xh * m2)
        dz_ref[...] = dz
        dzb_ref[...] = dz.astype(BF16)

    row = pl.BlockSpec((tm, d), lambda i: (i, 0))
    vec = pl.BlockSpec((1, d), lambda i: (0, 0))
    return pl.pallas_call(
        body, name=name, grid=(m // tm,), in_specs=[row, row, vec],
        out_specs=[row, row, vec, vec],
        out_shape=[jax.ShapeDtypeStruct((m, d), F32), jax.ShapeDtypeStruct((m, d), BF16),
                   jax.ShapeDtypeStruct((1, d), F32), jax.ShapeDtypeStruct((1, d), F32)],
        compiler_params=_cp("arbitrary"))(dxn, z, g)


def _loss_bwd(y, target):
    m, d = y.shape
    tm = min(TM_LN, m)

    def body(y_ref, t_ref, dy_ref, ls_ref):
        @pl.when(pl.program_id(0) == 0)
        def _():
            ls_ref[...] = jnp.zeros_like(ls_ref)

        e = y_ref[...] - t_ref[...]
        dy_ref[...] = e * (1.0 / d)
        ls_ref[...] += _colsum(e * e) * (0.5 / d)

    row = pl.BlockSpec((tm, d), lambda i: (i, 0))
    return pl.pallas_call(
        body, name="loss_bwd", grid=(m // tm,), in_specs=[row, row],
        out_specs=[row, pl.BlockSpec((1, d), lambda i: (0, 0))],
        out_shape=[jax.ShapeDtypeStruct((m, d), F32), jax.ShapeDtypeStruct((1, d), F32)],
        compiler_params=_cp("arbitrary"))(y, target)


def _sc_fwd(h, cw, cb, seq, name):
    t, d3 = h.shape
    d = d3 // 3
    tm = min(TM_SC, seq)

    def body(hb_ref, hc_ref, hv_ref, cw_ref, cb_ref, q_ref, carry):
        i = pl.program_id(0)

        @pl.when(lax.rem(i * tm, seq) == 0)
        def _():
            carry[...] = jnp.zeros_like(carry)

        p = hc_ref[...].astype(F32) * hv_ref[...].astype(F32)
        ext = jnp.concatenate([carry[...], p], axis=0)
        u = cb_ref[...] + _row(cw_ref, 0) * _down(ext, 2, SUB) + _row(cw_ref, 1) * _down(ext, 1, SUB) \
            + _row(cw_ref, 2) * p
        q_ref[...] = (hb_ref[...].astype(F32) * u).astype(BF16)
        carry[...] = p[tm - SUB:, :]

    blk = lambda c: pl.BlockSpec((tm, d), lambda i: (i, c))
    return pl.pallas_call(
        body, name=name, grid=(t // tm,),
        in_specs=[blk(0), blk(1), blk(2), pl.BlockSpec((3, d), lambda i: (0, 0)),
                  pl.BlockSpec((1, d), lambda i: (0, 0))],
        out_specs=pl.BlockSpec((tm, d), lambda i: (i, 0)),
        out_shape=jax.ShapeDtypeStruct((t, d), BF16),
        scratch_shapes=[pltpu.VMEM((SUB, d), F32)],
        compiler_params=_cp("arbitrary"))(h, h, h, cw, cb)


def _sc_bwd(h, dq, cw, cb, seq, name):
    t, d3 = h.shape
    d = d3 // 3
    tm = min(TM_SC, seq)
    nt = t // tm
    hpt = tm // SUB16

    def body(hb_ref, hc_ref, hv_ref, hch_ref, hvh_ref, dq_ref, cw_ref, cb_ref,
             dh_ref, dcw_ref, dcb_ref, carry):
        i = pl.program_id(0)
        ri = nt - 1 - i

        @pl.when(i == 0)
        def _():
            dcw_ref[...] = jnp.zeros_like(dcw_ref)
            dcb_ref[...] = jnp.zeros_like(dcb_ref)

        @pl.when(lax.rem((ri + 1) * tm, seq) == 0)
        def _():
            carry[...] = jnp.zeros_like(carry)

        keep = jnp.where(lax.rem(ri * tm, seq) == 0, 0.0, 1.0)
        gb = hb_ref[...].astype(F32)
        gc = hc_ref[...].astype(F32)
        v = hv_ref[...].astype(F32)
        p = gc * v
        p_head = hch_ref[...].astype(F32) * hvh_ref[...].astype(F32) * keep
        ext = jnp.concatenate([p_head, p], axis=0)
        pm2 = _down(ext, 2, SUB16)
        pm1 = _down(ext, 1, SUB16)
        u = cb_ref[...] + _row(cw_ref, 0) * pm2 + _row(cw_ref, 1) * pm1 + _row(cw_ref, 2) * p
        dqf = dq_ref[...].astype(F32)
        du = dqf * gb
        dcb_ref[...] += _colsum(du)
        dcw_ref[0:1, :] += _colsum(du * pm2)
        dcw_ref[1:2, :] += _colsum(du * pm1)
        dcw_ref[2:3, :] += _colsum(du * p)
        ext2 = jnp.concatenate([du, carry[...]], axis=0)
        dp = _row(cw_ref, 2) * du + _row(cw_ref, 1) * _up(ext2, 1, tm) + _row(cw_ref, 0) * _up(ext2, 2, tm)
        carry[...] = du[0:SUB, :]
        dh_ref[0] = (dqf * u).astype(BF16)
        dh_ref[1] = (dp * v).astype(BF16)
        dh_ref[2] = (dp * gc).astype(BF16)

    blk = lambda c: pl.BlockSpec((tm, d), lambda i: (nt - 1 - i, c))
    head = lambda c: pl.BlockSpec((SUB16, d), lambda i: (jnp.maximum((nt - 1 - i) * hpt - 1, 0), c))
    vec = lambda r: pl.BlockSpec((r, d), lambda i: (0, 0))
    return pl.pallas_call(
        body, name=name, grid=(nt,),
        in_specs=[blk(0), blk(1), blk(2), head(1), head(2),
                  pl.BlockSpec((tm, d), lambda i: (nt - 1 - i, 0)), vec(3), vec(1)],
        out_specs=[pl.BlockSpec((3, tm, d), lambda i: (0, nt - 1 - i, 0)), vec(3), vec(1)],
        out_shape=[jax.ShapeDtypeStruct((3, t, d), BF16), jax.ShapeDtypeStruct((3, d), F32),
                   jax.ShapeDtypeStruct((1, d), F32)],
        scratch_shapes=[pltpu.VMEM((SUB, d), F32)],
        compiler_params=_cp("arbitrary"))(h, h, h, h, h, dq, cw, cb)


def _conv3(ext, cur, cw_ref, cb_ref, n_head):
    return cb_ref[...] + _row(cw_ref, 0) * _down(ext, 2, n_head) + _row(cw_ref, 1) * _down(ext, 1, n_head) \
        + _row(cw_ref, 2) * cur


def _ffn_fwd(h, cw, cb, seq, tc, name):
    t, f2 = h.shape
    f = f2 // 2
    nc = f // tc
    tm = min(TM_FFN, seq)

    def body(hg_ref, hv_ref, cwg_ref, cwv_ref, cbg_ref, cbv_ref, act_ref, carry):
        i = pl.program_id(1)

        @pl.when(lax.rem(i * tm, seq) == 0)
        def _():
            carry[...] = jnp.zeros_like(carry)

        hg = hg_ref[...].astype(F32)
        hv = hv_ref[...].astype(F32)
        gp = _conv3(jnp.concatenate([carry[0], hg], axis=0), hg, cwg_ref, cbg_ref, SUB)
        vp = _conv3(jnp.concatenate([carry[1], hv], axis=0), hv, cwv_ref, cbv_ref, SUB)
        carry[0] = hg[tm - SUB:, :]
        carry[1] = hv[tm - SUB:, :]
        act_ref[...] = (gp * _sigmoid(gp) * vp).astype(BF16)

    blk = lambda off: pl.BlockSpec((tm, tc), lambda j, i: (i, j + off))
    vec = lambda r, off: pl.BlockSpec((r, tc), lambda j, i: (0, j + off))
    return pl.pallas_call(
        body, name=name, grid=(nc, t // tm),
        in_specs=[blk(0), blk(nc), vec(3, 0), vec(3, nc), vec(1, 0), vec(1, nc)],
        out_specs=pl.BlockSpec((tm, tc), lambda j, i: (i, j)),
        out_shape=jax.ShapeDtypeStruct((t, f), BF16),
        scratch_shapes=[pltpu.VMEM((2, SUB, tc), F32)],
        compiler_params=_cp("arbitrary", "arbitrary"))(h, h, cw, cw, cb, cb)


def _ffn_bwd(h, dact, cw, cb, seq, tc, name):
    t, f2 = h.shape
    f = f2 // 2
    nc = f // tc
    tm = min(TM_FFN, seq)
    nt = t // tm
    hpt = tm // SUB16

    def body(hg_ref, hv_ref, hgh_ref, hvh_ref, da_ref, cwg_ref, cwv_ref, cbg_ref, cbv_ref,
             dh_ref, dcwg_ref, dcwv_ref, dcbg_ref, dcbv_ref, carry):
        i = pl.program_id(1)
        ri = nt - 1 - i

        @pl.when(i == 0)
        def _():
            for r in (dcwg_ref, dcwv_ref, dcbg_ref, dcbv_ref):
                r[...] = jnp.zeros_like(r)

        @pl.when(lax.rem((ri + 1) * tm, seq) == 0)
        def _():
            carry[...] = jnp.zeros_like(carry)

        keep = jnp.where(lax.rem(ri * tm, seq) == 0, 0.0, 1.0)
        da = da_ref[...].astype(F32)

        def half(h_ref, hh_ref, cw_ref, cb_ref):
            cur = h_ref[...].astype(F32)
            ext = jnp.concatenate([hh_ref[...].astype(F32) * keep, cur], axis=0)
            m2 = _down(ext, 2, SUB16)
            m1 = _down(ext, 1, SUB16)
            pre = cb_ref[...] + _row(cw_ref, 0) * m2 + _row(cw_ref, 1) * m1 + _row(cw_ref, 2) * cur
            return cur, m1, m2, pre

        hg, g1, g2, gp = half(hg_ref, hgh_ref, cwg_ref, cbg_ref)
        hv, v1, v2, vp = half(hv_ref, hvh_ref, cwv_ref, cbv_ref)
        sg = _sigmoid(gp)
        dg = da * vp * (sg * (1.0 + gp * (1.0 - sg)))
        dv = da * (gp * sg)

        def back(dpre, cur, m1, m2, cw_ref, dcw_ref, dcb_ref, slot, out_slot):
            dcb_ref[...] += _colsum(dpre)
            dcw_ref[0:1, :] += _colsum(dpre * m2)
            dcw_ref[1:2, :] += _colsum(dpre * m1)
            dcw_ref[2:3, :] += _colsum(dpre * cur)
            ext2 = jnp.concatenate([dpre, carry[slot]], axis=0)
            dh = _row(cw_ref, 2) * dpre + _row(cw_ref, 1) * _up(ext2, 1, tm) + _row(cw_ref, 0) * _up(ext2, 2, tm)
            carry[slot] = dpre[0:SUB, :]
            dh_ref[out_slot] = dh.astype(BF16)

        back(dg, hg, g1, g2, cwg_ref, dcwg_ref, dcbg_ref, 0, 0)
        back(dv, hv, v1, v2, cwv_ref, dcwv_ref, dcbv_ref, 1, 1)

    blk = lambda off: pl.BlockSpec((tm, tc), lambda j, i: (nt - 1 - i, j + off))
    head = lambda off: pl.BlockSpec((SUB16, tc), lambda j, i: (jnp.maximum((nt - 1 - i) * hpt - 1, 0), j + off))
    vec = lambda r, off: pl.BlockSpec((r, tc), lambda j, i: (0, j + off))
    acc = lambda r: pl.BlockSpec((r, tc), lambda j, i: (0, j))
    return pl.pallas_call(
        body, name=name, grid=(nc, nt),
        in_specs=[blk(0), blk(nc), head(0), head(nc), pl.BlockSpec((tm, tc), lambda j, i: (nt - 1 - i, j)),
                  vec(3, 0), vec(3, nc), vec(1, 0), vec(1, nc)],
        out_specs=[pl.BlockSpec((2, tm, tc), lambda j, i: (0, nt - 1 - i, j)), acc(3), acc(3), acc(1), acc(1)],
        out_shape=[jax.ShapeDtypeStruct((2, t, f), BF16), jax.ShapeDtypeStruct((3, f), F32),
                   jax.ShapeDtypeStruct((3, f), F32), jax.ShapeDtypeStruct((1, f), F32),
                   jax.ShapeDtypeStruct((1, f), F32)],
        scratch_shapes=[pltpu.VMEM((2, SUB, tc), F32)],
        compiler_params=_cp("arbitrary", "arbitrary"))(h, h, h, h, dact, cw, cw, cb, cb)


def _lru_gates(xr, wg_ref, bg_ref):
    rs, gs = [], []
    for hd in range(LRU_HEADS):
        xh = xr[:, hd * LRU_BLOCK:(hd + 1) * LRU_BLOCK].astype(BF16)
        gt = jnp.dot(xh, wg_ref[hd], preferred_element_type=F32) + _row(bg_ref, hd)
        rs.append(gt[:, :LRU_BLOCK])
        gs.append(gt[:, LRU_BLOCK:])
    return jnp.concatenate(rs, axis=1), jnp.concatenate(gs, axis=1)


def _lru_coeffs(xr, wg_ref, bg_ref, lam_ref):
    gr, gi = _lru_gates(xr, wg_ref, bg_ref)
    r = _sigmoid(gr)
    ig = _sigmoid(gi)
    sp = _softplus(-lam_ref[...])
    log_a = -LRU_C * r * sp
    a = jnp.exp(log_a)
    mult = jnp.sqrt(_one_minus_exp(2.0 * log_a))
    return r, ig, sp, a, mult


def _lru_fwd(h, cw, cb, wg, bg, lam, seq, name):
    t, r2 = h.shape
    rw = r2 // 2
    ts = min(TS_LRU, seq)
    n8 = ts // SUB

    def body(hg_ref, hr_ref, cw_ref, cb_ref, wg_ref, bg_ref, lam_ref, hs_ref, y_ref,
             a_s, b_s, cconv, cstate):
        i = pl.program_id(0)

        @pl.when(lax.rem(i * ts, seq) == 0)
        def _():
            cconv[...] = jnp.zeros_like(cconv)
            cstate[...] = jnp.zeros_like(cstate)

        rin = hr_ref[...].astype(F32)
        ext = jnp.concatenate([cconv[...], rin], axis=0)
        xr = cb_ref[...]
        for k in range(4):
            xr = xr + _row(cw_ref, k) * _down(ext, 3 - k, SUB)
        cconv[...] = rin[ts - SUB:, :]
        _, ig, _, a, mult = _lru_coeffs(xr, wg_ref, bg_ref, lam_ref)
        a_s[...] = a
        b_s[...] = mult * (ig * xr)
        row = lax.broadcasted_iota(jnp.int32, (SUB, rw), 0)

        def step(j, carry):
            off = pl.multiple_of(j * SUB, SUB)
            a8 = a_s[pl.ds(off, SUB), :]
            b8 = b_s[pl.ds(off, SUB), :]
            for d in (1, 2, 4):
                m = row >= d
                b8 = jnp.where(m, a8 * pltpu.roll(b8, d, 0) + b8, b8)
                a8 = jnp.where(m, a8 * pltpu.roll(a8, d, 0), a8)
            h8 = a8 * carry + b8
            hs_ref[pl.ds(off, SUB), :] = h8
            return _colsum(jnp.where(row == SUB - 1, h8, 0.0))

        cstate[...] = lax.fori_loop(0, n8, step, cstate[...])
        gel, _ = _gelu_and_grad(hg_ref[...].astype(F32))
        y_ref[...] = (hs_ref[...] * gel).astype(BF16)

    full = lambda shp: pl.BlockSpec(shp, lambda i: (0,) * len(shp))
    return pl.pallas_call(
        body, name=name, grid=(t // ts,),
        in_specs=[pl.BlockSpec((ts, rw), lambda i: (i, 0)), pl.BlockSpec((ts, rw), lambda i: (i, 1)),
                  full((4, rw)), full((1, rw)), full(wg.shape), full(bg.shape), full((1, rw))],
        out_specs=[pl.BlockSpec((ts, rw), lambda i: (i, 0)), pl.BlockSpec((ts, rw), lambda i: (i, 0))],
        out_shape=[jax.ShapeDtypeStruct((t, rw), F32), jax.ShapeDtypeStruct((t, rw), BF16)],
        scratch_shapes=[pltpu.VMEM((ts, rw), F32), pltpu.VMEM((ts, rw), F32),
                        pltpu.VMEM((SUB, rw), F32), pltpu.VMEM((1, rw), F32)],
        compiler_params=_cp("arbitrary"))(h, h, cw, cb, wg, bg, lam)


def _lru_bwd(h, hs, dy, cw, cb, wg, bg, lam, seq, name):
    t, r2 = h.shape
    rw = r2 // 2
    ts = min(TS_LRU, seq)
    nt = t // ts
    n8 = ts // SUB
    hp16 = ts // SUB16
    hp8 = ts // SUB

    def body(hg_ref, hr_ref, hrh_ref, hs_ref, hsh_ref, dy_ref, cw_ref, cb_ref, wg_ref, bg_ref, lam_ref,
             dh_ref, dbin_ref, dcw_ref, dcb_ref, dwg_ref, dbg_ref, dlam_ref,
             a_s, g_s, l_s, c_lam, c_a, c_dxr):
        i = pl.program_id(0)
        ri = nt - 1 - i

        @pl.when(i == 0)
        def _():
            for r in (dbin_ref, dcw_ref, dcb_ref, dwg_ref, dbg_ref, dlam_ref):
                r[...] = jnp.zeros_like(r)

        @pl.when(lax.rem((ri + 1) * ts, seq) == 0)
        def _():
            c_lam[...] = jnp.zeros_like(c_lam)
            c_a[...] = jnp.zeros_like(c_a)
            c_dxr[...] = jnp.zeros_like(c_dxr)

        keep = jnp.where(lax.rem(ri * ts, seq) == 0, 0.0, 1.0)
        rin = hr_ref[...].astype(F32)
        ext = jnp.concatenate([hrh_ref[...].astype(F32) * keep, rin], axis=0)
        shifted = [_down(ext, 3 - k, SUB16) for k in range(4)]
        xr = cb_ref[...]
        for k in range(4):
            xr = xr + _row(cw_ref, k) * shifted[k]
        r, ig, sp, a, mult = _lru_coeffs(xr, wg_ref, bg_ref, lam_ref)
        gel, dgel = _gelu_and_grad(hg_ref[...].astype(F32))
        dyf = dy_ref[...].astype(F32)
        hsv = hs_ref[...]
        dg = dyf * hsv * dgel

        a_s[...] = _up(jnp.concatenate([a, c_a[...]], axis=0), 1, ts)
        g_s[...] = dyf * gel
        c_a[...] = a[0:SUB, :]
        row = lax.broadcasted_iota(jnp.int32, (SUB, rw), 0)

        def step(j, carry):
            off = pl.multiple_of((n8 - 1 - j) * SUB, SUB)
            a8 = a_s[pl.ds(off, SUB), :]
            b8 = g_s[pl.ds(off, SUB), :]
            for d in (1, 2, 4):
                m = row < SUB - d
                b8 = jnp.where(m, a8 * pltpu.roll(b8, SUB - d, 0) + b8, b8)
                a8 = jnp.where(m, a8 * pltpu.roll(a8, SUB - d, 0), a8)
            l8 = a8 * carry + b8
            l_s[pl.ds(off, SUB), :] = l8
            return _colsum(jnp.where(row == 0, l8, 0.0))

        c_lam[...] = lax.fori_loop(0, n8, step, c_lam[...])
        lamv = l_s[...]
        hs_prev = _down(jnp.concatenate([hsh_ref[...] * keep, hsv], axis=0), 1, SUB)
        da = lamv * hs_prev
        t1 = lamv * xr
        dmult = t1 * ig
        dig = t1 * mult
        dxr = lamv * mult * ig
        dla = da * a - dmult * (a * a) / mult
        dr = dla * (-LRU_C * sp)
        dlam_ref[...] += _colsum(dla * (-LRU_C) * r) * (-_sigmoid(-lam_ref[...]))
        dgr = dr * r * (1.0 - r)
        dgi = dig * ig * (1.0 - ig)
        parts = []
        for hd in range(LRU_HEADS):
            sl = slice(hd * LRU_BLOCK, (hd + 1) * LRU_BLOCK)
            dgt = jnp.concatenate([dgr[:, sl], dgi[:, sl]], axis=1)
            dbg_ref[hd:hd + 1, :] += _colsum(dgt)
            dgt16 = dgt.astype(BF16)
            parts.append(lax.dot_general(dgt16, wg_ref[hd], (((1,), (1,)), ((), ())),
                                         preferred_element_type=F32))
            dwg_ref[hd] += lax.dot_general(xr[:, sl].astype(BF16), dgt16, (((0,), (0,)), ((), ())),
                                           preferred_element_type=F32)
        dxr = dxr + jnp.concatenate(parts, axis=1)

        dcb_ref[...] += _colsum(dxr)
        for k in range(4):
            dcw_ref[k:k + 1, :] += _colsum(dxr * shifted[k])
        ext2 = jnp.concatenate([dxr, c_dxr[...]], axis=0)
        drb = _row(cw_ref, 3) * dxr
        for k in range(3):
            drb = drb + _row(cw_ref, k) * _up(ext2, 3 - k, ts)
        c_dxr[...] = dxr[0:SUB, :]
        dh_ref[0] = dg.astype(BF16)
        dh_ref[1] = drb.astype(BF16)
        dbin_ref[:, 0:rw] += _colsum(dg)
        dbin_ref[:, rw:] += _colsum(drb)

    rev = lambda c: pl.BlockSpec((ts, rw), lambda i: (nt - 1 - i, c))
    full = lambda shp: pl.BlockSpec(shp, lambda i: (0,) * len(shp))
    nh = LRU_HEADS
    return pl.pallas_call(
        body, name=name, grid=(nt,),
        in_specs=[rev(0), rev(1),
                  pl.BlockSpec((SUB16, rw), lambda i: (jnp.maximum((nt - 1 - i) * hp16 - 1, 0), 1)),
                  rev(0),
                  pl.BlockSpec((SUB, rw), lambda i: (jnp.maximum((nt - 1 - i) * hp8 - 1, 0), 0)),
                  rev(0), full((4, rw)), full((1, rw)), full(wg.shape), full(bg.shape), full((1, rw))],
        out_specs=[pl.BlockSpec((2, ts, rw), lambda i: (0, nt - 1 - i, 0)), full((1, r2)), full((4, rw)),
                   full((1, rw)), full((nh, LRU_BLOCK, 2 * LRU_BLOCK)), full((nh, 2 * LRU_BLOCK)), full((1, rw))],
        out_shape=[jax.ShapeDtypeStruct((2, t, rw), BF16), jax.ShapeDtypeStruct((1, r2), F32),
                   jax.ShapeDtypeStruct((4, rw), F32), jax.ShapeDtypeStruct((1, rw), F32),
                   jax.ShapeDtypeStruct((nh, LRU_BLOCK, 2 * LRU_BLOCK), F32),
                   jax.ShapeDtypeStruct((nh, 2 * LRU_BLOCK), F32), jax.ShapeDtypeStruct((1, rw), F32)],
        scratch_shapes=[pltpu.VMEM((ts, rw), F32), pltpu.VMEM((ts, rw), F32), pltpu.VMEM((ts, rw), F32),
                        pltpu.VMEM((1, rw), F32), pltpu.VMEM((SUB, rw), F32), pltpu.VMEM((SUB, rw), F32)],
        compiler_params=_cp("arbitrary"))(h, h, h, hs, hs, dy, cw, cb, wg, bg, lam)


def _row_tile(rows, cols, mult):
    cap = max(mult, ELEMS_PER_BLOCK // cols)
    best = None
    for cand in range(mult, min(rows, cap) + 1, mult):
        if rows % cand == 0:
            best = cand
    return best if best is not None else rows


def _add_pair(a, b, name):
    shape = a.shape
    a2 = a.reshape(-1, shape[-1])
    b2 = b.reshape(-1, shape[-1])
    rows, cols = a2.shape
    tr = _row_tile(rows, cols, SUB16)

    def body(a_ref, b_ref, o_ref):
        o_ref[...] = (a_ref[...].astype(F32) + b_ref[...].astype(F32)).astype(o_ref.dtype)

    blk = pl.BlockSpec((tr, cols), lambda i: (i, 0))
    out = pl.pallas_call(body, name=name, grid=(rows // tr,), in_specs=[blk, blk], out_specs=blk,
                         out_shape=jax.ShapeDtypeStruct((rows, cols), a.dtype),
                         compiler_params=_cp("parallel"))(a2, b2)
    return out.reshape(shape)


def _add_chips(r, name):
    shape = r.shape[1:]
    r3 = r.reshape(N_CHIPS, -1, shape[-1])
    _, rows, cols = r3.shape
    tr = _row_tile(rows, cols, SUB16)

    def body(r_ref, o_ref):
        s = r_ref[0].astype(F32) + r_ref[1].astype(F32)
        s = s + r_ref[2].astype(F32)
        o_ref[...] = s + r_ref[3].astype(F32)

    out = pl.pallas_call(body, name=name, grid=(rows // tr,),
                         in_specs=[pl.BlockSpec((N_CHIPS, tr, cols), lambda i: (0, i, 0))],
                         out_specs=pl.BlockSpec((tr, cols), lambda i: (i, 0)),
                         out_shape=jax.ShapeDtypeStruct((rows, cols), F32),
                         compiler_params=_cp("parallel"))(r3)
    return out.reshape(shape)


def _adamw(w, g, m, v, name):
    shape = w.shape
    flat = [arr.reshape(-1, shape[-1]) for arr in (w, g, m, v)]
    rows, cols = flat[0].shape
    tr = _row_tile(rows, cols, SUB)

    def body(w_ref, g_ref, m_ref, v_ref, d_ref, mo_ref, vo_ref):
        gg = g_ref[...]
        m2 = ADAM_B1 * m_ref[...] + (1.0 - ADAM_B1) * gg
        v2 = ADAM_B2 * v_ref[...] + (1.0 - ADAM_B2) * (gg * gg)
        m_hat = m2 / (1.0 - ADAM_B1 ** ADAM_STEP)
        v_hat = v2 / (1.0 - ADAM_B2 ** ADAM_STEP)
        d_ref[...] = -ADAM_LR * (m_hat / (jnp.sqrt(v_hat) + ADAM_EPS) + ADAM_WD * w_ref[...])
        mo_ref[...] = m2
        vo_ref[...] = v2

    blk = pl.BlockSpec((tr, cols), lambda i: (i, 0))
    outs = pl.pallas_call(body, name=name, grid=(rows // tr,), in_specs=[blk] * 4, out_specs=[blk] * 3,
                          out_shape=[jax.ShapeDtypeStruct((rows, cols), F32)] * 3,
                          compiler_params=_cp("parallel"))(*flat)
    return tuple(o.reshape(shape) for o in outs)


def _full_shape(kind, shard_shape):
    s = tuple(shard_shape)
    if kind == "col":
        return s[:-1] + (N_CHIPS * s[-1],)
    if kind == "row":
        return (s[0], N_CHIPS * s[1]) + s[2:]
    return (N_CHIPS,) + s


def _slot(kind, ref, k, shard_shape):
    if kind == "col":
        n = shard_shape[-1]
        return ref.at[:, :, pl.ds(pl.multiple_of(k * n, 128), n)]
    if kind == "row":
        n = shard_shape[1]
        return ref.at[:, pl.ds(pl.multiple_of(k * n, SUB16), n), :]
    return ref.at[k]


def _half(ref, c, h):
    return ref.at[pl.ds(c * h, h)]


def _half_of_full(kind, ref, c, h):
    if kind == "lead":
        return ref.at[:, pl.ds(c * h, h)]
    return ref.at[pl.ds(c * h, h)]


def _position():
    x = lax.axis_index("x")
    y = lax.axis_index("y")
    c = lax.axis_index("c")
    return x, y, c


def _peer_chip(x, y, j):
    tx = 1 - x if j & 2 else x
    ty = 1 - y if j & 1 else y
    return tx, ty


def _remote(src, dst, ssem, rsem, dev):
    return pltpu.make_async_remote_copy(src_ref=src, dst_ref=dst, send_sem=ssem, recv_sem=rsem,
                                        device_id=dev, device_id_type=pl.DeviceIdType.MESH)


_ANY = pl.BlockSpec(memory_space=pl.ANY)


def _all_gather(shards, kinds):
    nt = len(shards)
    shapes = [s.shape for s in shards]

    def body(*refs):
        ins, outs = refs[:nt], refs[nt:2 * nt]
        ssem, rsem, lsem = refs[2 * nt:]
        x, y, c = _position()
        my = 2 * x + y
        sib = (x, y, 1 - c)
        local, sends, fwds = [], [], []
        for t in range(nt):
            h = shapes[t][0] // 2
            own = _slot(kinds[t], outs[t], my, shapes[t])
            cp = pltpu.make_async_copy(ins[t], own, lsem.at[t])
            cp.start()
            local.append(cp)
            for j in (1, 2, 3):
                tx, ty = _peer_chip(x, y, j)
                cp = _remote(_half(ins[t], c, h), _half(own, c, h), ssem.at[6 * t + j - 1],
                             rsem.at[6 * t + j - 1], (tx, ty, c))
                cp.start()
                sends.append(cp)
        for t in range(nt):
            h = shapes[t][0] // 2
            for j in (1, 2, 3):
                tx, ty = _peer_chip(x, y, j)
                got = _half(_slot(kinds[t], outs[t], 2 * tx + ty, shapes[t]), c, h)
                _remote(got, got, ssem.at[6 * t + j - 1], rsem.at[6 * t + j - 1], sib).wait_recv()
                cp = _remote(got, got, ssem.at[6 * t + 2 + j], rsem.at[6 * t + 2 + j], sib)
                cp.start()
                fwds.append(cp)
        for t in range(nt):
            h = shapes[t][0] // 2
            for j in (1, 2, 3):
                tx, ty = _peer_chip(x, y, j)
                other = _half(_slot(kinds[t], outs[t], 2 * tx + ty, shapes[t]), 1 - c, h)
                _remote(other, other, ssem.at[6 * t + 2 + j], rsem.at[6 * t + 2 + j], sib).wait_recv()
        for cp in sends + fwds:
            cp.wait_send()
        for cp in local:
            cp.wait()

    return pl.pallas_call(
        body, name="all_gather", in_specs=[_ANY] * nt, out_specs=[_ANY] * nt,
        out_shape=[jax.ShapeDtypeStruct(_full_shape(k, s.shape), s.dtype) for k, s in zip(kinds, shards)],
        scratch_shapes=[pltpu.SemaphoreType.DMA((6 * nt,)), pltpu.SemaphoreType.DMA((6 * nt,)),
                        pltpu.SemaphoreType.DMA((nt,))],
    )(*shards)


def _rs_swap(partials, kinds, shard_shapes):
    nt = len(partials)

    def half_shape(kind, shape):
        s = list(shape)
        s[1 if kind == "lead" else 0] //= 2
        return tuple(s)

    def body(*refs):
        ins, r0, r1 = refs[:nt], refs[nt:2 * nt], refs[2 * nt:3 * nt]
        ssem, rsem, lsem = refs[3 * nt:]
        x, y, c = _position()
        cps = []
        for t in range(nt):
            h = shard_shapes[t][0] // 2
            keep = pltpu.make_async_copy(_half_of_full(kinds[t], ins[t], c, h), r0[t], lsem.at[t])
            give = _remote(_half_of_full(kinds[t], ins[t], 1 - c, h), r1[t], ssem.at[t], rsem.at[t],
                           (x, y, 1 - c))
            keep.start()
            give.start()
            cps += [keep, give]
        for cp in cps:
            cp.wait()

    outs = [jax.ShapeDtypeStruct(half_shape(k, p.shape), p.dtype) for k, p in zip(kinds, partials)]
    res = pl.pallas_call(
        body, name="rs_swap", in_specs=[_ANY] * nt, out_specs=[_ANY] * (2 * nt), out_shape=outs + outs,
        scratch_shapes=[pltpu.SemaphoreType.DMA((nt,)), pltpu.SemaphoreType.DMA((nt,)),
                        pltpu.SemaphoreType.DMA((nt,))],
    )(*partials)
    return res[:nt], res[nt:]


def _rs_all_to_all(chip_sums, kinds, shard_shapes):
    nt = len(chip_sums)

    def half_shard(shape):
        return (shape[0] // 2,) + tuple(shape[1:])

    def body(*refs):
        ins, outs = refs[:nt], refs[nt:2 * nt]
        ssem, rsem, lsem = refs[2 * nt:]
        x, y, c = _position()
        my = 2 * x + y
        cps = []
        for t in range(nt):
            hs = half_shard(shard_shapes[t])
            cp = pltpu.make_async_copy(_slot(kinds[t], ins[t], my, hs), outs[t].at[my], lsem.at[t])
            cp.start()
            cps.append(cp)
            for j in (1, 2, 3):
                tx, ty = _peer_chip(x, y, j)
                cp = _remote(_slot(kinds[t], ins[t], 2 * tx + ty, hs), outs[t].at[my], ssem.at[3 * t + j - 1],
                             rsem.at[3 * t + j - 1], (tx, ty, c))
                cp.start()
                cps.append(cp)
        for cp in cps:
            cp.wait()

    return pl.pallas_call(
        body, name="rs_all_to_all", in_specs=[_ANY] * nt, out_specs=[_ANY] * nt,
        out_shape=[jax.ShapeDtypeStruct((N_CHIPS,) + half_shard(s), a.dtype)
                   for s, a in zip(shard_shapes, chip_sums)],
        scratch_shapes=[pltpu.SemaphoreType.DMA((3 * nt,)), pltpu.SemaphoreType.DMA((3 * nt,)),
                        pltpu.SemaphoreType.DMA((nt,))],
    )(*chip_sums)


def _rs_share(halves, shard_shapes):
    nt = len(halves)

    def body(*refs):
        ins, outs = refs[:nt], refs[nt:2 * nt]
        ssem, rsem, lsem = refs[2 * nt:]
        x, y, c = _position()
        cps = []
        for t in range(nt):
            h = shard_shapes[t][0] // 2
            dst = _half(outs[t], c, h)
            keep = pltpu.make_async_copy(ins[t], dst, lsem.at[t])
            give = _remote(ins[t], dst, ssem.at[t], rsem.at[t], (x, y, 1 - c))
            keep.start()
            give.start()
            cps += [keep, give]
        for cp in cps:
            cp.wait()

    return pl.pallas_call(
        body, name="rs_share", in_specs=[_ANY] * nt, out_specs=[_ANY] * nt,
        out_shape=[jax.ShapeDtypeStruct(tuple(s), F32) for s in shard_shapes],
        scratch_shapes=[pltpu.SemaphoreType.DMA((nt,)), pltpu.SemaphoreType.DMA((nt,)),
                        pltpu.SemaphoreType.DMA((nt,))],
    )(*halves)


SMALL = (("sc_conv_w", True), ("sc_conv_b", False), ("lru_b_in", True), ("lru_conv_w", True),
         ("lru_conv_b", True), ("lru_b_gate", True), ("lru_lambda", True), ("ffn_conv_w", True),
         ("ffn_conv_b", False), ("ln_g", True), ("ln_b", True))
PACK_ROW_MULT = 2 * SUB16


def _pack_rows(shapes):
    n = sum(math.prod(shapes[name]) for name, _ in SMALL)
    rows = -(-n // 128)
    return -(-rows // PACK_ROW_MULT) * PACK_ROW_MULT


def _pack_local(vals, shapes):
    flat = jnp.concatenate([vals[name].reshape(-1) for name, _ in SMALL])
    rows = _pack_rows(shapes)
    return jnp.pad(flat, (0, rows * 128 - flat.shape[0])).reshape(rows, 128)


def _unpack_local(pack, shapes):
    flat = pack.reshape(-1)
    out, off = {}, 0
    for name, _ in SMALL:
        n = math.prod(shapes[name])
        out[name] = flat[off:off + n].reshape(shapes[name])
        off += n
    return out


def _pack_slots(fulls, shapes):
    parts = []
    for name, sharded in SMALL:
        v = fulls[name]
        if sharded:
            ns = shapes[name][-1]
            v = jnp.moveaxis(v.reshape(v.shape[:-1] + (N_CHIPS, ns)), -2, 0).reshape(N_CHIPS, -1)
        else:
            v = jnp.broadcast_to(v.reshape(1, -1), (N_CHIPS, v.size))
        parts.append(v)
    flat = jnp.concatenate(parts, axis=1)
    rows = _pack_rows(shapes)
    return jnp.pad(flat, ((0, 0), (0, rows * 128 - flat.shape[1]))).reshape(N_CHIPS, rows, 128)


def _unpack_slots(packs, shapes):
    flat = packs.reshape(N_CHIPS, -1)
    out, off = {}, 0
    for name, sharded in SMALL:
        n = math.prod(shapes[name])
        if sharded:
            seg = flat[:, off:off + n].reshape((N_CHIPS,) + tuple(shapes[name]))
            seg = jnp.moveaxis(seg, 0, -2)
            out[name] = seg.reshape(seg.shape[:-2] + (N_CHIPS * shapes[name][-1],))
        off += n
    return out


WEIGHTS = ("sc_w_in", "sc_conv_w", "sc_conv_b", "sc_w_out", "lru_w_in", "lru_b_in", "lru_conv_w", "lru_conv_b",
           "lru_w_gate", "lru_b_gate", "lru_lambda", "lru_w_out", "ffn_w_up", "ffn_conv_w", "ffn_conv_b",
           "ffn_w_down", "ln_g", "ln_b")
BIG = (("sc_w_in", "col"), ("sc_w_out", "row"), ("lru_w_in", "col"), ("lru_w_gate", "lead"),
       ("lru_w_out", "row"), ("ffn_w_up", "col"), ("ffn_w_down", "row"))


def kernel(x, sc_w_in, sc_conv_w, sc_conv_b, sc_w_out, lru_w_in, lru_b_in, lru_conv_w, lru_conv_b, lru_w_gate, lru_b_gate, lru_lambda, lru_w_out, ffn_w_up, ffn_conv_w, ffn_conv_b, ffn_w_down, ln_g, ln_b, loss_target, m_sc_w_in, m_sc_conv_w, m_sc_conv_b, m_sc_w_out, m_lru_w_in, m_lru_b_in, m_lru_conv_w, m_lru_conv_b, m_lru_w_gate, m_lru_b_gate, m_lru_lambda, m_lru_w_out, m_ffn_w_up, m_ffn_conv_w, m_ffn_conv_b, m_ffn_w_down, m_ln_g, m_ln_b, v_sc_w_in, v_sc_conv_w, v_sc_conv_b, v_sc_w_out, v_lru_w_in, v_lru_b_in, v_lru_conv_w, v_lru_conv_b, v_lru_w_gate, v_lru_b_gate, v_lru_lambda, v_lru_w_out, v_ffn_w_up, v_ffn_conv_w, v_ffn_conv_b, v_ffn_w_down, v_ln_g, v_ln_b):
    w = dict(zip(WEIGHTS, (sc_w_in, sc_conv_w, sc_conv_b, sc_w_out, lru_w_in, lru_b_in, lru_conv_w, lru_conv_b,
                           lru_w_gate, lru_b_gate, lru_lambda, lru_w_out, ffn_w_up, ffn_conv_w, ffn_conv_b,
                           ffn_w_down, ln_g, ln_b)))
    mom = dict(zip(WEIGHTS, (m_sc_w_in, m_sc_conv_w, m_sc_conv_b, m_sc_w_out, m_lru_w_in, m_lru_b_in, m_lru_conv_w,
                             m_lru_conv_b, m_lru_w_gate, m_lru_b_gate, m_lru_lambda, m_lru_w_out, m_ffn_w_up,
                             m_ffn_conv_w, m_ffn_conv_b, m_ffn_w_down, m_ln_g, m_ln_b)))
    vel = dict(zip(WEIGHTS, (v_sc_w_in, v_sc_conv_w, v_sc_conv_b, v_sc_w_out, v_lru_w_in, v_lru_b_in, v_lru_conv_w,
                             v_lru_conv_b, v_lru_w_gate, v_lru_b_gate, v_lru_lambda, v_lru_w_out, v_ffn_w_up,
                             v_ffn_conv_w, v_ffn_conv_b, v_ffn_w_down, v_ln_g, v_ln_b)))
    bd, seq, d = x.shape
    t = bd * seq
    small_shapes = {name: w[name].shape for name, _ in SMALL}
    big_names = [n for n, _ in BIG]
    kinds = [k for _, k in BIG] + ["lead"]
    shard_shapes = [w[n].shape for n in big_names]

    w_pack = _pack_local(w, small_shapes)
    gathered = _all_gather([w[n].astype(BF16) for n in big_names] + [w_pack], kinds)
    full = dict(zip(big_names, gathered[:-1]))
    full.update(_unpack_slots(gathered[-1], small_shapes))
    full["sc_conv_b"] = sc_conv_b
    full["ffn_conv_b"] = ffn_conv_b
    wg_full = jnp.moveaxis(full["lru_w_gate"], 0, -2)
    wg_full = wg_full.reshape(wg_full.shape[:-2] + (2 * LRU_BLOCK,))
    f = full["ffn_w_down"].shape[1]
    rw = full["lru_w_out"].shape[1]

    x0 = x.reshape(t, d)
    xb = x0.astype(BF16)
    cur, cur_b = x0, xb
    saved = []

    for i in range(DEPTH):
        j = i // 2
        s = {"xb": cur_b}
        if i % 2 == 0:
            h = _mm_nn(cur_b, full["sc_w_in"], j, None, 768, "sc_in")
            q = _sc_fwd(h, full["sc_conv_w"][j], full["sc_conv_b"][j][None], seq, "sc_fwd")
            z1, x1, x1b = _mm_nn_ln(q, full["sc_w_out"], j, cur, full["ln_g"][i, 0][None], full["ln_b"][i, 0][None],
                                    "sc_out_ln")
        else:
            h = _mm_nn(cur_b, full["lru_w_in"], j, full["lru_b_in"][j][None], 1280, "lru_in")
            hs, q = _lru_fwd(h, full["lru_conv_w"][j], full["lru_conv_b"][j][None], wg_full[j],
                             full["lru_b_gate"][j], full["lru_lambda"][j][None], seq, "lru_fwd")
            s["hs"] = hs
            z1, x1, x1b = _mm_nn_ln(q, full["lru_w_out"], j, cur, full["ln_g"][i, 0][None],
                                    full["ln_b"][i, 0][None], "lru_out_ln")
        s.update(h=h, q=q, z1=z1, x1b=x1b)
        h2 = _mm_nn(x1b, full["ffn_w_up"], i, None, f // 2, "ffn_up")
        act = _ffn_fwd(h2, full["ffn_conv_w"][i], full["ffn_conv_b"][i][None], seq, f // 2, "ffn_fwd")
        z2, x2, x2b = _mm_nn_ln(act, full["ffn_w_down"], i, x1, full["ln_g"][i, 1][None], full["ln_b"][i, 1][None],
                                "ffn_down_ln")
        s.update(h2=h2, act=act, z2=z2)
        saved.append(s)
        cur, cur_b = x2, x2b

    dcur, loss_parts = _loss_bwd(cur, loss_target.reshape(t, d))
    loss = lax.psum(jnp.sum(loss_parts), MESH_AXES)

    gp = {n: [None] * w[n].shape[0] for n in WEIGHTS}
    for i in reversed(range(DEPTH)):
        j = i // 2
        s = saved[i]
        dz2, dz2b, dg, db = _ln_bwd(dcur, s["z2"], full["ln_g"][i, 1][None], "ln_bwd")
        gp["ln_g"][i] = [None, dg[0]]
        gp["ln_b"][i] = [None, db[0]]
        dact = _mm_nt(dz2b, full["ffn_w_down"], i, f // 2, "ffn_down_dx")
        gp["ffn_w_down"][i] = _mm_tn(s["act"], dz2b[None], f // 2, d, "ffn_down_dw")
        dh3, dcwg, dcwv, dcbg, dcbv = _ffn_bwd(s["h2"], dact, full["ffn_conv_w"][i], full["ffn_conv_b"][i][None],
                                               seq, f // 2, "ffn_bwd")
        gp["ffn_conv_w"][i] = jnp.concatenate([dcwg, dcwv], axis=1)
        gp["ffn_conv_b"][i] = jnp.concatenate([dcbg[0], dcbv[0]])
        dx1 = _mm_nt_res(dh3, full["ffn_w_up"], i, dz2, f // 2, "ffn_up_dx")
        gp["ffn_w_up"][i] = _mm_tn(s["x1b"], dh3, d, f // 2, "ffn_up_dw")
        dz1, dz1b, dg, db = _ln_bwd(dx1, s["z1"], full["ln_g"][i, 0][None], "ln_bwd")
        gp["ln_g"][i][0] = dg[0]
        gp["ln_b"][i][0] = db[0]
        gp["ln_g"][i] = jnp.stack(gp["ln_g"][i])
        gp["ln_b"][i] = jnp.stack(gp["ln_b"][i])
        if i % 2 == 0:
            dq = _mm_nt(dz1b, full["sc_w_out"], j, d, "sc_out_dx")
            gp["sc_w_out"][j] = _mm_tn(s["q"], dz1b[None], d, d, "sc_out_dw")
            dh3, dcw, dcb = _sc_bwd(s["h"], dq, full["sc_conv_w"][j], full["sc_conv_b"][j][None], seq, "sc_bwd")
            gp["sc_conv_w"][j] = dcw
            gp["sc_conv_b"][j] = dcb[0]
            dcur = _mm_nt_res(dh3, full["sc_w_in"], j, dz1, d, "sc_in_dx")
            gp["sc_w_in"][j] = _mm_tn(s["xb"], dh3, d, d, "sc_in_dw")
        else:
            dq = _mm_nt(dz1b, full["lru_w_out"], j, rw, "lru_out_dx")
            gp["lru_w_out"][j] = _mm_tn(s["q"], dz1b[None], rw, d, "lru_out_dw")
            dh3, dbin, dcw, dcb, dwg, dbg, dlam = _lru_bwd(
                s["h"], s["hs"], dq, full["lru_conv_w"][j], full["lru_conv_b"][j][None], wg_full[j],
                full["lru_b_gate"][j], full["lru_lambda"][j][None], seq, "lru_bwd")
            gp["lru_b_in"][j] = dbin[0]
            gp["lru_conv_w"][j] = dcw
            gp["lru_conv_b"][j] = dcb[0]
            gp["lru_w_gate"][j] = dwg
            gp["lru_b_gate"][j] = dbg
            gp["lru_lambda"][j] = dlam[0]
            dcur = _mm_nt_res(dh3, full["lru_w_in"], j, dz1, rw, "lru_in_dx")
            gp["lru_w_in"][j] = _mm_tn(s["xb"], dh3, d, rw, "lru_in_dw")
    grad_x = dcur.reshape(bd, seq, d)
    gp = {n: jnp.stack(v) for n, v in gp.items()}

    ns_gate = w["lru_w_gate"].shape[-1]
    gate = gp["lru_w_gate"]
    gate = jnp.moveaxis(gate.reshape(gate.shape[:-1] + (N_CHIPS, ns_gate)), -2, 0).astype(BF16)
    partials = [gate if n == "lru_w_gate" else gp[n] for n in big_names]
    partials.append(_pack_slots(gp, small_shapes))
    rs_shapes = shard_shapes + [w_pack.shape]
    r0, r1 = _rs_swap(partials, kinds, rs_shapes)
    chip_sums = [_add_pair(a, b, "rs_add_pair") for a, b in zip(r0, r1)]
    arrived = _rs_all_to_all(chip_sums, kinds, rs_shapes)
    halves = [_add_chips(r, "rs_add_chips") for r in arrived]
    grads = _rs_share(halves, rs_shapes)

    g_out, d_out, m_out, v_out = {}, {}, {}, {}
    for n, g in zip(big_names, grads[:-1]):
        g_out[n] = g
        d_out[n], m_out[n], v_out[n] = _adamw(w[n], g, mom[n], vel[n], "adamw")
    dp, mp, vp = _adamw(w_pack, grads[-1], _pack_local(mom, small_shapes), _pack_local(vel, small_shapes), "adamw")
    for dst, pack in ((g_out, grads[-1]), (d_out, dp), (m_out, mp), (v_out, vp)):
        dst.update(_unpack_local(pack, small_shapes))

    return (loss, grad_x, *[g_out[n] for n in WEIGHTS], *[d_out[n] for n in WEIGHTS],
            *[m_out[n] for n in WEIGHTS], *[v_out[n] for n in WEIGHTS])
```

```python
import math

import jax
import jax.numpy as jnp
from jax import lax
from jax.experimental import pallas as pl
from jax.experimental.pallas import tpu as pltpu

F32 = jnp.float32
BF16 = jnp.bfloat16

DEPTH = 4
LRU_HEADS = 10
LRU_BLOCK = 128
LRU_C = 8.0
LN_EPS = 1e-5
ALPHA = (2.0 * DEPTH) ** 0.25
ADAM_LR, ADAM_B1, ADAM_B2, ADAM_EPS, ADAM_WD, ADAM_STEP = 0.001, 0.9, 0.999, 1e-08, 0.01, 10
N_CHIPS = 4
MESH_AXES = ("x", "y", "c")

VMEM_LIMIT_BYTES = 48 * 1024 * 1024
TM_MM = 512
TT_MM = 512
TM_SC = 256
TM_FFN = 256
TS_LRU = 128
TM_LN = 512
ELEMS_PER_BLOCK = 256 * 1024
STREAM_ELEMS_PER_BLOCK = 1024 * 1024
SUB = 8
SUB16 = 16


def _cp(*sem):
    return pltpu.CompilerParams(dimension_semantics=sem, vmem_limit_bytes=VMEM_LIMIT_BYTES)


def _sigmoid(v):
    return 1.0 / (1.0 + jnp.exp(-v))


def _softplus(v):
    e = jnp.exp(-jnp.abs(v))
    log1p = jnp.where(e < 1e-3, e * (1.0 - e * (0.5 - e * (1.0 / 3.0))), jnp.log(1.0 + e))
    return jnp.maximum(v, 0.0) + log1p


def _one_minus_exp(v):
    series = -v * (1.0 + v * (0.5 + v * (1.0 / 6.0 + v * (1.0 / 24.0))))
    return jnp.where(v > -0.02, series, 1.0 - jnp.exp(v))


def _gelu_and_grad(v):
    k = math.sqrt(2.0 / math.pi)
    t = jnp.tanh(k * (v + 0.044715 * v * v * v))
    val = 0.5 * v * (1.0 + t)
    grad = 0.5 * (1.0 + t) + 0.5 * v * (1.0 - t * t) * k * (1.0 + 3.0 * 0.044715 * v * v)
    return val, grad


def _down(ext, k, n_head):
    if k:
        ext = pltpu.roll(ext, k, 0)
    return ext[n_head:]


def _up(ext, k, n):
    if k:
        ext = pltpu.roll(ext, ext.shape[0] - k, 0)
    return ext[:n]


def _row(ref, k):
    return ref[k:k + 1, :]


def _colsum(v):
    return jnp.sum(v, axis=0, keepdims=True)


def _mm_nn(a, w3, l, bias, tn, name):
    m, k = a.shape
    n = w3.shape[2]
    tm = min(TM_MM, m)
    has_bias = bias is not None

    def body(*refs):
        if has_bias:
            a_ref, w_ref, b_ref, o_ref = refs
        else:
            a_ref, w_ref, o_ref = refs
        acc = jnp.dot(a_ref[...], w_ref[...], preferred_element_type=F32)
        if has_bias:
            acc = acc + b_ref[...]
        o_ref[...] = acc.astype(o_ref.dtype)

    in_specs = [pl.BlockSpec((tm, k), lambda i, j: (i, 0)),
                pl.BlockSpec((None, k, tn), lambda i, j: (l, 0, j))]
    args = [a, w3]
    if has_bias:
        in_specs.append(pl.BlockSpec((1, tn), lambda i, j: (0, j)))
        args.append(bias)
    return pl.pallas_call(
        body, name=name, grid=(m // tm, n // tn), in_specs=in_specs,
        out_specs=pl.BlockSpec((tm, tn), lambda i, j: (i, j)),
        out_shape=jax.ShapeDtypeStruct((m, n), BF16),
        compiler_params=_cp("parallel", "arbitrary"))(*args)


def _mm_nn_ln(a, w3, l, xres, g, b, name):
    m, k = a.shape
    n = w3.shape[2]
    tm = min(TM_MM, m)

    def body(a_ref, w_ref, x_ref, g_ref, b_ref, z_ref, xn_ref, xb_ref):
        y = jnp.dot(a_ref[...], w_ref[...], preferred_element_type=F32)
        z = ALPHA * x_ref[...] + y
        mu = jnp.mean(z, axis=-1, keepdims=True)
        zc = z - mu
        var = jnp.mean(zc * zc, axis=-1, keepdims=True)
        xn = zc * lax.rsqrt(var + LN_EPS) * g_ref[...] + b_ref[...]
        z_ref[...] = z
        xn_ref[...] = xn
        xb_ref[...] = xn.astype(BF16)

    row = pl.BlockSpec((tm, n), lambda i: (i, 0))
    vec = pl.BlockSpec((1, n), lambda i: (0, 0))
    return pl.pallas_call(
        body, name=name, grid=(m // tm,),
        in_specs=[pl.BlockSpec((tm, k), lambda i: (i, 0)),
                  pl.BlockSpec((None, k, n), lambda i: (l, 0, 0)), row, vec, vec],
        out_specs=[row, row, row],
        out_shape=[jax.ShapeDtypeStruct((m, n), F32), jax.ShapeDtypeStruct((m, n), F32),
                   jax.ShapeDtypeStruct((m, n), BF16)],
        compiler_params=_cp("parallel"))(a, w3, xres, g, b)


def _mm_nt(a, w3, l, tk, name):
    m, n = a.shape
    kd = w3.shape[1]
    tm = min(TM_MM, m)

    def body(a_ref, w_ref, o_ref):
        o_ref[...] = lax.dot_general(a_ref[...], w_ref[...], (((1,), (1,)), ((), ())),
                                     preferred_element_type=F32).astype(o_ref.dtype)

    return pl.pallas_call(
        body, name=name, grid=(m // tm, kd // tk),
        in_specs=[pl.BlockSpec((tm, n), lambda i, j: (i, 0)),
                  pl.BlockSpec((None, tk, n), lambda i, j: (l, j, 0))],
        out_specs=pl.BlockSpec((tm, tk), lambda i, j: (i, j)),
        out_shape=jax.ShapeDtypeStruct((m, kd), BF16),
        compiler_params=_cp("parallel", "arbitrary"))(a, w3)


def _mm_nt_res(dh3, w3, l, dz, tc, name):
    g, m, cg = dh3.shape
    kd = w3.shape[1]
    ncg = cg // tc
    nk = g * ncg
    tm = min(TM_MM, m)

    def body(a_ref, w_ref, dz_ref, o_ref, acc):
        k = pl.program_id(1)

        @pl.when(k == 0)
        def _():
            acc[...] = ALPHA * dz_ref[...]

        acc[...] += lax.dot_general(a_ref[...], w_ref[...], (((1,), (1,)), ((), ())),
                                    preferred_element_type=F32)

        @pl.when(k == nk - 1)
        def _():
            o_ref[...] = acc[...]

    return pl.pallas_call(
        body, name=name, grid=(m // tm, nk),
        in_specs=[pl.BlockSpec((None, tm, tc), lambda i, k: (k // ncg, i, k % ncg)),
                  pl.BlockSpec((None, kd, tc), lambda i, k: (l, 0, k)),
                  pl.BlockSpec((tm, kd), lambda i, k: (i, 0))],
        out_specs=pl.BlockSpec((tm, kd), lambda i, k: (i, 0)),
        out_shape=jax.ShapeDtypeStruct((m, kd), F32),
        scratch_shapes=[pltpu.VMEM((tm, kd), F32)],
        compiler_params=_cp("parallel", "arbitrary"))(dh3, w3, dz)


def _mm_tn(a, b3, tka, tnb, name):
    t, ka = a.shape
    g, _, cg = b3.shape
    ncg = cg // tnb
    tt = min(TT_MM, t)
    nt = t // tt

    def body(a_ref, b_ref, o_ref, acc):
        s = pl.program_id(2)

        @pl.when(s == 0)
        def _():
            acc[...] = jnp.zeros_like(acc)

        acc[...] += lax.dot_general(a_ref[...], b_ref[...], (((0,), (0,)), ((), ())),
                                    preferred_element_type=F32)

        @pl.when(s == nt - 1)
        def _():
            o_ref[...] = acc[...].astype(o_ref.dtype)

    return pl.pallas_call(
        body, name=name, grid=(ka // tka, g * ncg, nt),
        in_specs=[pl.BlockSpec((tt, tka), lambda i, j, s: (s, i)),
                  pl.BlockSpec((None, tt, tnb), lambda i, j, s: (j // ncg, s, j % ncg))],
        out_specs=pl.BlockSpec((tka, tnb), lambda i, j, s: (i, j)),
        out_shape=jax.ShapeDtypeStruct((ka, g * cg), BF16),
        scratch_shapes=[pltpu.VMEM((tka, tnb), F32)],
        compiler_params=_cp("parallel", "parallel", "arbitrary"))(a, b3)


def _ln_bwd(dxn, z, g, name):
    m, d = z.shape
    tm = min(TM_LN, m)

    def body(dx_ref, z_ref, g_ref, dz_ref, dzb_ref, dg_ref, db_ref):
        @pl.when(pl.program_id(0) == 0)
        def _():
            dg_ref[...] = jnp.zeros_like(dg_ref)
            db_ref[...] = jnp.zeros_like(db_ref)

        zz = z_ref[...]
        dx = dx_ref[...]
        mu = jnp.mean(zz, axis=-1, keepdims=True)
        zc = zz - mu
        var = jnp.mean(zc * zc, axis=-1, keepdims=True)
        rstd = lax.rsqrt(var + LN_EPS)
        xh = zc * rstd
        dg_ref[...] += _colsum(dx * xh)
        db_ref[...] += _colsum(dx)
        dxh = dx * g_ref[...]
        m1 = jnp.mean(dxh, axis=-1, keepdims=True)
        m2 = jnp.mean(dxh * xh, axis=-1, keepdims=True)
        dz = rstd * (dxh - m1 - xh * m2)
        dz_ref[...] = dz
        dzb_ref[...] = dz.astype(BF16)

    row = pl.BlockSpec((tm, d), lambda i: (i, 0))
    vec = pl.BlockSpec((1, d), lambda i: (0, 0))
    return pl.pallas_call(
        body, name=name, grid=(m // tm,), in_specs=[row, row, vec],
        out_specs=[row, row, vec, vec],
        out_shape=[jax.ShapeDtypeStruct((m, d), F32), jax.ShapeDtypeStruct((m, d), BF16),
                   jax.ShapeDtypeStruct((1, d), F32), jax.ShapeDtypeStruct((1, d), F32)],
        compiler_params=_cp("arbitrary"))(dxn, z, g)


def _loss_bwd(y, target):
    m, d = y.shape
    tm = min(TM_LN, m)

    def body(y_ref, t_ref, dy_ref, ls_ref):
        @pl.when(pl.program_id(0) == 0)
        def _():
            ls_ref[...] = jnp.zeros_like(ls_ref)

        e = y_ref[...] - t_ref[...]
        dy_ref[...] = e * (1.0 / d)
        ls_ref[...] += _colsum(e * e) * (0.5 / d)

    row = pl.BlockSpec((tm, d), lambda i: (i, 0))
    return pl.pallas_call(
        body, name="loss_bwd", grid=(m // tm,), in_specs=[row, row],
        out_specs=[row, pl.BlockSpec((1, d), lambda i: (0, 0))],
        out_shape=[jax.ShapeDtypeStruct((m, d), F32), jax.ShapeDtypeStruct((1, d), F32)],
        compiler_params=_cp("arbitrary"))(y, target)


def _sc_fwd(h, cw, cb, seq, name):
    t, d3 = h.shape
    d = d3 // 3
    tm = min(TM_SC, seq)

    def body(hb_ref, hc_ref, hv_ref, cw_ref, cb_ref, q_ref, carry):
        i = pl.program_id(0)

        @pl.when(lax.rem(i * tm, seq) == 0)
        def _():
            carry[...] = jnp.zeros_like(carry)

        p = hc_ref[...].astype(F32) * hv_ref[...].astype(F32)
        ext = jnp.concatenate([carry[...], p], axis=0)
        u = cb_ref[...] + _row(cw_ref, 0) * _down(ext, 2, SUB) + _row(cw_ref, 1) * _down(ext, 1, SUB) \
            + _row(cw_ref, 2) * p
        q_ref[...] = (hb_ref[...].astype(F32) * u).astype(BF16)
        carry[...] = p[tm - SUB:, :]

    blk = lambda c: pl.BlockSpec((tm, d), lambda i: (i, c))
    return pl.pallas_call(
        body, name=name, grid=(t // tm,),
        in_specs=[blk(0), blk(1), blk(2), pl.BlockSpec((3, d), lambda i: (0, 0)),
                  pl.BlockSpec((1, d), lambda i: (0, 0))],
        out_specs=pl.BlockSpec((tm, d), lambda i: (i, 0)),
        out_shape=jax.ShapeDtypeStruct((t, d), BF16),
        scratch_shapes=[pltpu.VMEM((SUB, d), F32)],
        compiler_params=_cp("arbitrary"))(h, h, h, cw, cb)


def _sc_bwd(h, dq, cw, cb, seq, name):
    t, d3 = h.shape
    d = d3 // 3
    tm = min(TM_SC, seq)
    nt = t // tm
    hpt = tm // SUB16

    def body(hb_ref, hc_ref, hv_ref, hch_ref, hvh_ref, dq_ref, cw_ref, cb_ref,
             dh_ref, dcw_ref, dcb_ref, carry):
        i = pl.program_id(0)
        ri = nt - 1 - i

        @pl.when(i == 0)
        def _():
            dcw_ref[...] = jnp.zeros_like(dcw_ref)
            dcb_ref[...] = jnp.zeros_like(dcb_ref)

        @pl.when(lax.rem((ri + 1) * tm, seq) == 0)
        def _():
            carry[...] = jnp.zeros_like(carry)

        keep = jnp.where(lax.rem(ri * tm, seq) == 0, 0.0, 1.0)
        gb = hb_ref[...].astype(F32)
        gc = hc_ref[...].astype(F32)
        v = hv_ref[...].astype(F32)
        p = gc * v
        p_head = hch_ref[...].astype(F32) * hvh_ref[...].astype(F32) * keep
        ext = jnp.concatenate([p_head, p], axis=0)
        pm2 = _down(ext, 2, SUB16)
        pm1 = _down(ext, 1, SUB16)
        u = cb_ref[...] + _row(cw_ref, 0) * pm2 + _row(cw_ref, 1) * pm1 + _row(cw_ref, 2) * p
        dqf = dq_ref[...].astype(F32)
        du = dqf * gb
        dcb_ref[...] += _colsum(du)
        dcw_ref[0:1, :] += _colsum(du * pm2)
        dcw_ref[1:2, :] += _colsum(du * pm1)
        dcw_ref[2:3, :] += _colsum(du * p)
        ext2 = jnp.concatenate([du, carry[...]], axis=0)
        dp = _row(cw_ref, 2) * du + _row(cw_ref, 1) * _up(ext2, 1, tm) + _row(cw_ref, 0) * _up(ext2, 2, tm)
        carry[...] = du[0:SUB, :]
        dh_ref[0] = (dqf * u).astype(BF16)
        dh_ref[1] = (dp * v).astype(BF16)
        dh_ref[2] = (dp * gc).astype(BF16)

    blk = lambda c: pl.BlockSpec((tm, d), lambda i: (nt - 1 - i, c))
    head = lambda c: pl.BlockSpec((SUB16, d), lambda i: (jnp.maximum((nt - 1 - i) * hpt - 1, 0), c))
    vec = lambda r: pl.BlockSpec((r, d), lambda i: (0, 0))
    return pl.pallas_call(
        body, name=name, grid=(nt,),
        in_specs=[blk(0), blk(1), blk(2), head(1), head(2),
                  pl.BlockSpec((tm, d), lambda i: (nt - 1 - i, 0)), vec(3), vec(1)],
        out_specs=[pl.BlockSpec((3, tm, d), lambda i: (0, nt - 1 - i, 0)), vec(3), vec(1)],
        out_shape=[jax.ShapeDtypeStruct((3, t, d), BF16), jax.ShapeDtypeStruct((3, d), F32),
                   jax.ShapeDtypeStruct((1, d), F32)],
        scratch_shapes=[pltpu.VMEM((SUB, d), F32)],
        compiler_params=_cp("arbitrary"))(h, h, h, h, h, dq, cw, cb)


def _conv3(ext, cur, cw_ref, cb_ref, n_head):
    return cb_ref[...] + _row(cw_ref, 0) * _down(ext, 2, n_head) + _row(cw_ref, 1) * _down(ext, 1, n_head) \
        + _row(cw_ref, 2) * cur


def _ffn_fwd(h, cw, cb, seq, tc, name):
    t, f2 = h.shape
    f = f2 // 2
    nc = f // tc
    tm = min(TM_FFN, seq)

    def body(hg_ref, hv_ref, cwg_ref, cwv_ref, cbg_ref, cbv_ref, act_ref, carry):
        i = pl.program_id(1)

        @pl.when(lax.rem(i * tm, seq) == 0)
        def _():
            carry[...] = jnp.zeros_like(carry)

        hg = hg_ref[...].astype(F32)
        hv = hv_ref[...].astype(F32)
        gp = _conv3(jnp.concatenate([carry[0], hg], axis=0), hg, cwg_ref, cbg_ref, SUB)
        vp = _conv3(jnp.concatenate([carry[1], hv], axis=0), hv, cwv_ref, cbv_ref, SUB)
        carry[0] = hg[tm - SUB:, :]
        carry[1] = hv[tm - SUB:, :]
        act_ref[...] = (gp * _sigmoid(gp) * vp).astype(BF16)

    blk = lambda off: pl.BlockSpec((tm, tc), lambda j, i: (i, j + off))
    vec = lambda r, off: pl.BlockSpec((r, tc), lambda j, i: (0, j + off))
    return pl.pallas_call(
        body, name=name, grid=(nc, t // tm),
        in_specs=[blk(0), blk(nc), vec(3, 0), vec(3, nc), vec(1, 0), vec(1, nc)],
        out_specs=pl.BlockSpec((tm, tc), lambda j, i: (i, j)),
        out_shape=jax.ShapeDtypeStruct((t, f), BF16),
        scratch_shapes=[pltpu.VMEM((2, SUB, tc), F32)],
        compiler_params=_cp("arbitrary", "arbitrary"))(h, h, cw, cw, cb, cb)


def _ffn_bwd(h, dact, cw, cb, seq, tc, name):
    t, f2 = h.shape
    f = f2 // 2
    nc = f // tc
    tm = min(TM_FFN, seq)
    nt = t // tm
    hpt = tm // SUB16

    def body(hg_ref, hv_ref, hgh_ref, hvh_ref, da_ref, cwg_ref, cwv_ref, cbg_ref, cbv_ref,
             dh_ref, dcwg_ref, dcwv_ref, dcbg_ref, dcbv_ref, carry):
        i = pl.program_id(1)
        ri = nt - 1 - i

        @pl.when(i == 0)
        def _():
            for r in (dcwg_ref, dcwv_ref, dcbg_ref, dcbv_ref):
                r[...] = jnp.zeros_like(r)

        @pl.when(lax.rem((ri + 1) * tm, seq) == 0)
        def _():
            carry[...] = jnp.zeros_like(carry)

        keep = jnp.where(lax.rem(ri * tm, seq) == 0, 0.0, 1.0)
        da = da_ref[...].astype(F32)

        def half(h_ref, hh_ref, cw_ref, cb_ref):
            cur = h_ref[...].astype(F32)
            ext = jnp.concatenate([hh_ref[...].astype(F32) * keep, cur], axis=0)
            m2 = _down(ext, 2, SUB16)
            m1 = _down(ext, 1, SUB16)
            pre = cb_ref[...] + _row(cw_ref, 0) * m2 + _row(cw_ref, 1) * m1 + _row(cw_ref, 2) * cur
            return cur, m1, m2, pre

        hg, g1, g2, gp = half(hg_ref, hgh_ref, cwg_ref, cbg_ref)
        hv, v1, v2, vp = half(hv_ref, hvh_ref, cwv_ref, cbv_ref)
        sg = _sigmoid(gp)
        dg = da * vp * (sg * (1.0 + gp * (1.0 - sg)))
        dv = da * (gp * sg)

        def back(dpre, cur, m1, m2, cw_ref, dcw_ref, dcb_ref, slot, out_slot):
            dcb_ref[...] += _colsum(dpre)
            dcw_ref[0:1, :] += _colsum(dpre * m2)
            dcw_ref[1:2, :] += _colsum(dpre * m1)
            dcw_ref[2:3, :] += _colsum(dpre * cur)
            ext2 = jnp.concatenate([dpre, carry[slot]], axis=0)
            dh = _row(cw_ref, 2) * dpre + _row(cw_ref, 1) * _up(ext2, 1, tm) + _row(cw_ref, 0) * _up(ext2, 2, tm)
            carry[slot] = dpre[0:SUB, :]
            dh_ref[out_slot] = dh.astype(BF16)

        back(dg, hg, g1, g2, cwg_ref, dcwg_ref, dcbg_ref, 0, 0)
        back(dv, hv, v1, v2, cwv_ref, dcwv_ref, dcbv_ref, 1, 1)

    blk = lambda off: pl.BlockSpec((tm, tc), lambda j, i: (nt - 1 - i, j + off))
    head = lambda off: pl.BlockSpec((SUB16, tc), lambda j, i: (jnp.maximum((nt - 1 - i) * hpt - 1, 0), j + off))
    vec = lambda r, off: pl.BlockSpec((r, tc), lambda j, i: (0, j + off))
    acc = lambda r: pl.BlockSpec((r, tc), lambda j, i: (0, j))
    return pl.pallas_call(
        body, name=name, grid=(nc, nt),
        in_specs=[blk(0), blk(nc), head(0), head(nc), pl.BlockSpec((tm, tc), lambda j, i: (nt - 1 - i, j)),
                  vec(3, 0), vec(3, nc), vec(1, 0), vec(1, nc)],
        out_specs=[pl.BlockSpec((2, tm, tc), lambda j, i: (0, nt - 1 - i, j)), acc(3), acc(3), acc(1), acc(1)],
        out_shape=[jax.ShapeDtypeStruct((2, t, f), BF16), jax.ShapeDtypeStruct((3, f), F32),
                   jax.ShapeDtypeStruct((3, f), F32), jax.ShapeDtypeStruct((1, f), F32),
                   jax.ShapeDtypeStruct((1, f), F32)],
        scratch_shapes=[pltpu.VMEM((2, SUB, tc), F32)],
        compiler_params=_cp("arbitrary", "arbitrary"))(h, h, h, h, dact, cw, cw, cb, cb)


def _lru_gates(xr, wg_ref, bg_ref):
    rs, gs = [], []
    for hd in range(LRU_HEADS):
        xh = xr[:, hd * LRU_BLOCK:(hd + 1) * LRU_BLOCK].astype(BF16)
        gt = jnp.dot(xh, wg_ref[hd], preferred_element_type=F32) + _row(bg_ref, hd)
        rs.append(gt[:, :LRU_BLOCK])
        gs.append(gt[:, LRU_BLOCK:])
    return jnp.concatenate(rs, axis=1), jnp.concatenate(gs, axis=1)


def _lru_coeffs(xr, wg_ref, bg_ref, lam_ref):
    gr, gi = _lru_gates(xr, wg_ref, bg_ref)
    r = _sigmoid(gr)
    ig = _sigmoid(gi)
    sp = _softplus(-lam_ref[...])
    log_a = -LRU_C * r * sp
    a = jnp.exp(log_a)
    mult = jnp.sqrt(_one_minus_exp(2.0 * log_a))
    return r, ig, sp, a, mult


def _lru_fwd(h, cw, cb, wg, bg, lam, seq, name):
    t, r2 = h.shape
    rw = r2 // 2
    ts = min(TS_LRU, seq)
    n8 = ts // SUB

    def body(hg_ref, hr_ref, cw_ref, cb_ref, wg_ref, bg_ref, lam_ref, hs_ref, y_ref,
             a_s, b_s, cconv, cstate):
        i = pl.program_id(0)

        @pl.when(lax.rem(i * ts, seq) == 0)
        def _():
            cconv[...] = jnp.zeros_like(cconv)
            cstate[...] = jnp.zeros_like(cstate)

        rin = hr_ref[...].astype(F32)
        ext = jnp.concatenate([cconv[...], rin], axis=0)
        xr = cb_ref[...]
        for k in range(4):
            xr = xr + _row(cw_ref, k) * _down(ext, 3 - k, SUB)
        cconv[...] = rin[ts - SUB:, :]
        _, ig, _, a, mult = _lru_coeffs(xr, wg_ref, bg_ref, lam_ref)
        a_s[...] = a
        b_s[...] = mult * (ig * xr)
        row = lax.broadcasted_iota(jnp.int32, (SUB, rw), 0)

        def step(j, carry):
            off = pl.multiple_of(j * SUB, SUB)
            a8 = a_s[pl.ds(off, SUB), :]
            b8 = b_s[pl.ds(off, SUB), :]
            for d in (1, 2, 4):
                m = row >= d
                b8 = jnp.where(m, a8 * pltpu.roll(b8, d, 0) + b8, b8)
                a8 = jnp.where(m, a8 * pltpu.roll(a8, d, 0), a8)
            h8 = a8 * carry + b8
            hs_ref[pl.ds(off, SUB), :] = h8
            return _colsum(jnp.where(row == SUB - 1, h8, 0.0))

        cstate[...] = lax.fori_loop(0, n8, step, cstate[...])
        gel, _ = _gelu_and_grad(hg_ref[...].astype(F32))
        y_ref[...] = (hs_ref[...] * gel).astype(BF16)

    full = lambda shp: pl.BlockSpec(shp, lambda i: (0,) * len(shp))
    return pl.pallas_call(
        body, name=name, grid=(t // ts,),
        in_specs=[pl.BlockSpec((ts, rw), lambda i: (i, 0)), pl.BlockSpec((ts, rw), lambda i: (i, 1)),
                  full((4, rw)), full((1, rw)), full(wg.shape), full(bg.shape), full((1, rw))],
        out_specs=[pl.BlockSpec((ts, rw), lambda i: (i, 0)), pl.BlockSpec((ts, rw), lambda i: (i, 0))],
        out_shape=[jax.ShapeDtypeStruct((t, rw), F32), jax.ShapeDtypeStruct((t, rw), BF16)],
        scratch_shapes=[pltpu.VMEM((ts, rw), F32), pltpu.VMEM((ts, rw), F32),
                        pltpu.VMEM((SUB, rw), F32), pltpu.VMEM((1, rw), F32)],
        compiler_params=_cp("arbitrary"))(h, h, cw, cb, wg, bg, lam)


def _lru_bwd(h, hs, dy, cw, cb, wg, bg, lam, seq, name):
    t, r2 = h.shape
    rw = r2 // 2
    ts = min(TS_LRU, seq)
    nt = t // ts
    n8 = ts // SUB
    hp16 = ts // SUB16
    hp8 = ts // SUB

    def body(hg_ref, hr_ref, hrh_ref, hs_ref, hsh_ref, dy_ref, cw_ref, cb_ref, wg_ref, bg_ref, lam_ref,
             dh_ref, dbin_ref, dcw_ref, dcb_ref, dwg_ref, dbg_ref, dlam_ref,
             a_s, g_s, l_s, c_lam, c_a, c_dxr):
        i = pl.program_id(0)
        ri = nt - 1 - i

        @pl.when(i == 0)
        def _():
            for r in (dbin_ref, dcw_ref, dcb_ref, dwg_ref, dbg_ref, dlam_ref):
                r[...] = jnp.zeros_like(r)

        @pl.when(lax.rem((ri + 1) * ts, seq) == 0)
        def _():
            c_lam[...] = jnp.zeros_like(c_lam)
            c_a[...] = jnp.zeros_like(c_a)
            c_dxr[...] = jnp.zeros_like(c_dxr)

        keep = jnp.where(lax.rem(ri * ts, seq) == 0, 0.0, 1.0)
        rin = hr_ref[...].astype(F32)
        ext = jnp.concatenate([hrh_ref[...].astype(F32) * keep, rin], axis=0)
        shifted = [_down(ext, 3 - k, SUB16) for k in range(4)]
        xr = cb_ref[...]
        for k in range(4):
            xr = xr + _row(cw_ref, k) * shifted[k]
        r, ig, sp, a, mult = _lru_coeffs(xr, wg_ref, bg_ref, lam_ref)
        gel, dgel = _gelu_and_grad(hg_ref[...].astype(F32))
        dyf = dy_ref[...].astype(F32)
        hsv = hs_ref[...]
        dg = dyf * hsv * dgel

        a_s[...] = _up(jnp.concatenate([a, c_a[...]], axis=0), 1, ts)
        g_s[...] = dyf * gel
        c_a[...] = a[0:SUB, :]
        row = lax.broadcasted_iota(jnp.int32, (SUB, rw), 0)

        def step(j, carry):
            off = pl.multiple_of((n8 - 1 - j) * SUB, SUB)
            a8 = a_s[pl.ds(off, SUB), :]
            b8 = g_s[pl.ds(off, SUB), :]
            for d in (1, 2, 4):
                m = row < SUB - d
                b8 = jnp.where(m, a8 * pltpu.roll(b8, SUB - d, 0) + b8, b8)
                a8 = jnp.where(m, a8 * pltpu.roll(a8, SUB - d, 0), a8)
            l8 = a8 * carry + b8
            l_s[pl.ds(off, SUB), :] = l8
            return _colsum(jnp.where(row == 0, l8, 0.0))

        c_lam[...] = lax.fori_loop(0, n8, step, c_lam[...])
        lamv = l_s[...]
        hs_prev = _down(jnp.concatenate([hsh_ref[...] * keep, hsv], axis=0), 1, SUB)
        da = lamv * hs_prev
        t1 = lamv * xr
        dmult = t1 * ig
        dig = t1 * mult
        dxr = lamv * mult * ig
        dla = da * a - dmult * (a * a) / mult
        dr = dla * (-LRU_C * sp)
        dlam_ref[...] += _colsum(dla * (-LRU_C) * r) * (-_sigmoid(-lam_ref[...]))
        dgr = dr * r * (1.0 - r)
        dgi = dig * ig * (1.0 - ig)
        parts = []
        for hd in range(LRU_HEADS):
            sl = slice(hd * LRU_BLOCK, (hd + 1) * LRU_BLOCK)
            dgt = jnp.concatenate([dgr[:, sl], dgi[:, sl]], axis=1)
            dbg_ref[hd:hd + 1, :] += _colsum(dgt)
            dgt16 = dgt.astype(BF16)
            parts.append(lax.dot_general(dgt16, wg_ref[hd], (((1,), (1,)), ((), ())),
                                         preferred_element_type=F32))
            dwg_ref[hd] += lax.dot_general(xr[:, sl].astype(BF16), dgt16, (((0,), (0,)), ((), ())),
                                           preferred_element_type=F32)
        dxr = dxr + jnp.concatenate(parts, axis=1)

        dcb_ref[...] += _colsum(dxr)
        for k in range(4):
            dcw_ref[k:k + 1, :] += _colsum(dxr * shifted[k])
        ext2 = jnp.concatenate([dxr, c_dxr[...]], axis=0)
        drb = _row(cw_ref, 3) * dxr
        for k in range(3):
            drb = drb + _row(cw_ref, k) * _up(ext2, 3 - k, ts)
        c_dxr[...] = dxr[0:SUB, :]
        dh_ref[0] = dg.astype(BF16)
        dh_ref[1] = drb.astype(BF16)
        dbin_ref[:, 0:rw] += _colsum(dg)
        dbin_ref[:, rw:] += _colsum(drb)

    rev = lambda c: pl.BlockSpec((ts, rw), lambda i: (nt - 1 - i, c))
    full = lambda shp: pl.BlockSpec(shp, lambda i: (0,) * len(shp))
    nh = LRU_HEADS
    return pl.pallas_call(
        body, name=name, grid=(nt,),
        in_specs=[rev(0), rev(1),
                  pl.BlockSpec((SUB16, rw), lambda i: (jnp.maximum((nt - 1 - i) * hp16 - 1, 0), 1)),
                  rev(0),
                  pl.BlockSpec((SUB, rw), lambda i: (jnp.maximum((nt - 1 - i) * hp8 - 1, 0), 0)),
                  rev(0), full((4, rw)), full((1, rw)), full(wg.shape), full(bg.shape), full((1, rw))],
        out_specs=[pl.BlockSpec((2, ts, rw), lambda i: (0, nt - 1 - i, 0)), full((1, r2)), full((4, rw)),
                   full((1, rw)), full((nh, LRU_BLOCK, 2 * LRU_BLOCK)), full((nh, 2 * LRU_BLOCK)), full((1, rw))],
        out_shape=[jax.ShapeDtypeStruct((2, t, rw), BF16), jax.ShapeDtypeStruct((1, r2), F32),
                   jax.ShapeDtypeStruct((4, rw), F32), jax.ShapeDtypeStruct((1, rw), F32),
                   jax.ShapeDtypeStruct((nh, LRU_BLOCK, 2 * LRU_BLOCK), F32),
                   jax.ShapeDtypeStruct((nh, 2 * LRU_BLOCK), F32), jax.ShapeDtypeStruct((1, rw), F32)],
        scratch_shapes=[pltpu.VMEM((ts, rw), F32), pltpu.VMEM((ts, rw), F32), pltpu.VMEM((ts, rw), F32),
                        pltpu.VMEM((1, rw), F32), pltpu.VMEM((SUB, rw), F32), pltpu.VMEM((SUB, rw), F32)],
        compiler_params=_cp("arbitrary"))(h, h, h, hs, hs, dy, cw, cb, wg, bg, lam)


def _row_tile(rows, cols, mult, elems=ELEMS_PER_BLOCK):
    cap = max(mult, elems // cols)
    best = None
    for cand in range(mult, min(rows, cap) + 1, mult):
        if rows % cand == 0:
            best = cand
    return best if best is not None else rows


def _core_index():
    return lax.axis_index("c").astype(jnp.int32).reshape(1)


def _chip_index():
    return (2 * lax.axis_index("x") + lax.axis_index("y")).astype(jnp.int32).reshape(1)


def _add_pair(p4, r3, name):
    s, _, rows, cols = p4.shape
    tr = _row_tile(rows, cols, SUB16)

    def body(c_ref, a_ref, b_ref, o_ref):
        o_ref[...] = (a_ref[...].astype(F32) + b_ref[...].astype(F32)).astype(o_ref.dtype)

    blk = pl.BlockSpec((None, tr, cols), lambda k, i, c_ref: (k, i, 0))
    return pl.pallas_call(
        body, name=name,
        grid_spec=pltpu.PrefetchScalarGridSpec(
            num_scalar_prefetch=1, grid=(s, rows // tr),
            in_specs=[pl.BlockSpec((None, None, tr, cols), lambda k, i, c_ref: (k, c_ref[0], i, 0)), blk],
            out_specs=blk),
        out_shape=jax.ShapeDtypeStruct((s, rows, cols), p4.dtype),
        compiler_params=_cp("parallel", "parallel"))(_core_index(), p4, r3)


def _add_chips(r, name):
    shape = r.shape[1:]
    r3 = r.reshape(N_CHIPS, -1, shape[-1])
    _, rows, cols = r3.shape
    tr = _row_tile(rows, cols, SUB16)

    def body(r_ref, o_ref):
        s = r_ref[0].astype(F32) + r_ref[1].astype(F32)
        s = s + r_ref[2].astype(F32)
        o_ref[...] = s + r_ref[3].astype(F32)

    out = pl.pallas_call(body, name=name, grid=(rows // tr,),
                         in_specs=[pl.BlockSpec((N_CHIPS, tr, cols), lambda i: (0, i, 0))],
                         out_specs=pl.BlockSpec((tr, cols), lambda i: (i, 0)),
                         out_shape=jax.ShapeDtypeStruct((rows, cols), F32),
                         compiler_params=_cp("parallel"))(r3)
    return out.reshape(shape)


def _adamw(w, g_mine, g_sib, m, v, name):
    shape = w.shape
    rows, cols = g_mine.shape
    flat = [arr.reshape(2, rows, cols) for arr in (w, m, v)]
    tr = _row_tile(rows, cols, SUB)

    def body(c_ref, w_ref, gm_ref, gs_ref, m_ref, v_ref, g_ref, d_ref, mo_ref, vo_ref):
        gg = jnp.where(pl.program_id(0) == c_ref[0], gm_ref[...], gs_ref[...])
        m2 = ADAM_B1 * m_ref[...] + (1.0 - ADAM_B1) * gg
        v2 = ADAM_B2 * v_ref[...] + (1.0 - ADAM_B2) * (gg * gg)
        m_hat = m2 / (1.0 - ADAM_B1 ** ADAM_STEP)
        v_hat = v2 / (1.0 - ADAM_B2 ** ADAM_STEP)
        g_ref[...] = gg
        d_ref[...] = -ADAM_LR * (m_hat / (jnp.sqrt(v_hat) + ADAM_EPS) + ADAM_WD * w_ref[...])
        mo_ref[...] = m2
        vo_ref[...] = v2

    blk = pl.BlockSpec((None, tr, cols), lambda hh, i, c_ref: (hh, i, 0))
    gblk = pl.BlockSpec((tr, cols), lambda hh, i, c_ref: (i, 0))
    outs = pl.pallas_call(
        body, name=name,
        grid_spec=pltpu.PrefetchScalarGridSpec(
            num_scalar_prefetch=1, grid=(2, rows // tr),
            in_specs=[blk, gblk, gblk, blk, blk], out_specs=[blk] * 4),
        out_shape=[jax.ShapeDtypeStruct((2, rows, cols), F32)] * 4,
        compiler_params=_cp("parallel", "parallel"))(_core_index(), flat[0], g_mine, g_sib, flat[1], flat[2])
    return tuple(o.reshape(shape) for o in outs)


def _cast_place(shard, kind, dtype, name):
    my = _chip_index()
    s = shard.shape
    if kind == "col":
        rows, ns = math.prod(s[:-1]), s[-1]
        tr = _row_tile(rows, ns, SUB16)
        src = shard.reshape(rows, ns)
        grid = (rows // tr,)
        in_spec = pl.BlockSpec((tr, ns), lambda i, my_ref: (i, 0))
        out_spec = pl.BlockSpec((tr, ns), lambda i, my_ref: (i, my_ref[0]))
        out_shape = (rows, N_CHIPS * ns)
    else:
        a, b, c = (s[0], s[1], math.prod(s[2:])) if kind == "row" else (1, math.prod(s[:-1]), s[-1])
        tr = _row_tile(b, c, SUB16)
        src = shard.reshape(a, b, c)
        grid = (a, b // tr)
        in_spec = pl.BlockSpec((None, tr, c), lambda l, i, my_ref: (l, i, 0))
        out_spec = pl.BlockSpec((None, None, tr, c), lambda l, i, my_ref: (l, my_ref[0], i, 0))
        out_shape = (a, N_CHIPS, b, c)

    def body(my_ref, i_ref, o_ref):
        o_ref[...] = i_ref[...].astype(o_ref.dtype)

    out = pl.pallas_call(
        body, name=name,
        grid_spec=pltpu.PrefetchScalarGridSpec(num_scalar_prefetch=1, grid=grid, in_specs=[in_spec],
                                               out_specs=out_spec),
        out_shape=jax.ShapeDtypeStruct(out_shape, dtype),
        compiler_params=_cp(*["parallel"] * len(grid)))(my, src)
    return out.reshape(_full_shape(kind, s))


def _full_shape(kind, shard_shape):
    s = tuple(shard_shape)
    if kind == "col":
        return s[:-1] + (N_CHIPS * s[-1],)
    if kind == "row":
        return (s[0], N_CHIPS * s[1]) + s[2:]
    return (N_CHIPS,) + s


def _slot(kind, ref, k, shard_shape):
    if kind == "col":
        n = shard_shape[-1]
        return ref.at[:, :, pl.ds(pl.multiple_of(k * n, 128), n)]
    if kind == "row":
        n = shard_shape[1]
        return ref.at[:, pl.ds(pl.multiple_of(k * n, SUB16), n), :]
    return ref.at[k]


def _half(ref, c, h):
    return ref.at[pl.ds(c * h, h)]


def _position():
    x = lax.axis_index("x")
    y = lax.axis_index("y")
    c = lax.axis_index("c")
    return x, y, c


def _peer_chip(x, y, j):
    tx = 1 - x if j & 2 else x
    ty = 1 - y if j & 1 else y
    return tx, ty


def _remote(src, dst, ssem, rsem, dev):
    return pltpu.make_async_remote_copy(src_ref=src, dst_ref=dst, send_sem=ssem, recv_sem=rsem,
                                        device_id=dev, device_id_type=pl.DeviceIdType.MESH)


_ANY = pl.BlockSpec(memory_space=pl.ANY)


def _all_gather(placed, kinds, shapes):
    nt = len(placed)

    def body(*refs):
        outs = refs[nt:2 * nt]
        ssem, rsem = refs[2 * nt:]
        x, y, c = _position()
        my = 2 * x + y
        sib = (x, y, 1 - c)
        sends, fwds = [], []
        for t in range(nt):
            h = shapes[t][0] // 2
            own = _half(_slot(kinds[t], outs[t], my, shapes[t]), c, h)
            for j in (1, 2, 3):
                tx, ty = _peer_chip(x, y, j)
                cp = _remote(own, own, ssem.at[6 * t + j - 1], rsem.at[6 * t + j - 1], (tx, ty, c))
                cp.start()
                sends.append(cp)
        for t in range(nt):
            h = shapes[t][0] // 2
            for j in (1, 2, 3):
                tx, ty = _peer_chip(x, y, j)
                got = _half(_slot(kinds[t], outs[t], 2 * tx + ty, shapes[t]), c, h)
                _remote(got, got, ssem.at[6 * t + j - 1], rsem.at[6 * t + j - 1], sib).wait_recv()
                cp = _remote(got, got, ssem.at[6 * t + 2 + j], rsem.at[6 * t + 2 + j], sib)
                cp.start()
                fwds.append(cp)
        for t in range(nt):
            h = shapes[t][0] // 2
            for j in (1, 2, 3):
                tx, ty = _peer_chip(x, y, j)
                other = _half(_slot(kinds[t], outs[t], 2 * tx + ty, shapes[t]), 1 - c, h)
                _remote(other, other, ssem.at[6 * t + 2 + j], rsem.at[6 * t + 2 + j], sib).wait_recv()
        for cp in sends + fwds:
            cp.wait_send()

    return pl.pallas_call(
        body, name="all_gather", in_specs=[_ANY] * nt, out_specs=[_ANY] * nt,
        out_shape=[jax.ShapeDtypeStruct(p.shape, p.dtype) for p in placed],
        input_output_aliases={t: t for t in range(nt)},
        scratch_shapes=[pltpu.SemaphoreType.DMA((6 * nt,)), pltpu.SemaphoreType.DMA((6 * nt,))],
    )(*placed)


def _d2d_stream(src4, other_half, name):
    s, _, rows, cols = src4.shape
    tr = _row_tile(rows, cols, SUB16, STREAM_ELEMS_PER_BLOCK)
    nblk = rows // tr

    nh = src4.shape[1]

    def body(c_ref, src_ref, dst_ref, ssem, rsem):
        k = pl.program_id(0)
        i = pl.program_id(1)
        x, y, c = _position()
        sib = (x, y, 1 - c)
        blk = dst_ref.at[pl.ds(pl.multiple_of((k * nblk + i) * tr, SUB16), tr)]
        cp = _remote(src_ref, blk, ssem, rsem, sib)
        cp.start()
        cp.wait_send()

        @pl.when(jnp.logical_and(k == s - 1, i == nblk - 1))
        def _():
            _remote(dst_ref, dst_ref, ssem, rsem, sib).wait_recv()

    if other_half:
        src_map = lambda k, i, c_ref: ((k * nh + 1 - c_ref[0]) * nblk + i, 0)
    else:
        src_map = lambda k, i, c_ref: (k * nh * nblk + i, 0)
    out = pl.pallas_call(
        body, name=name,
        grid_spec=pltpu.PrefetchScalarGridSpec(
            num_scalar_prefetch=1, grid=(s, nblk),
            in_specs=[pl.BlockSpec((tr, cols), src_map)], out_specs=_ANY,
            scratch_shapes=[pltpu.SemaphoreType.DMA, pltpu.SemaphoreType.DMA]),
        out_shape=jax.ShapeDtypeStruct((s * rows, cols), src4.dtype),
        compiler_params=_cp("arbitrary", "arbitrary"))(_core_index(), src4.reshape(s * nh * rows, cols))
    return out.reshape(s, rows, cols)


def _rs_all_to_all(chip_sums, kinds, shard_shapes):
    nt = len(chip_sums)

    def half_shard(shape):
        return (shape[0] // 2,) + tuple(shape[1:])

    def body(*refs):
        ins, outs = refs[:nt], refs[nt:2 * nt]
        ssem, rsem, lsem = refs[2 * nt:]
        x, y, c = _position()
        my = 2 * x + y
        cps = []
        for t in range(nt):
            hs = half_shard(shard_shapes[t])
            cp = pltpu.make_async_copy(_slot(kinds[t], ins[t], my, hs), outs[t].at[my], lsem.at[t])
            cp.start()
            cps.append(cp)
            for j in (1, 2, 3):
                tx, ty = _peer_chip(x, y, j)
                cp = _remote(_slot(kinds[t], ins[t], 2 * tx + ty, hs), outs[t].at[my], ssem.at[3 * t + j - 1],
                             rsem.at[3 * t + j - 1], (tx, ty, c))
                cp.start()
                cps.append(cp)
        for cp in cps:
            cp.wait()

    return pl.pallas_call(
        body, name="rs_all_to_all", in_specs=[_ANY] * nt, out_specs=[_ANY] * nt,
        out_shape=[jax.ShapeDtypeStruct((N_CHIPS,) + half_shard(s), a.dtype)
                   for s, a in zip(shard_shapes, chip_sums)],
        scratch_shapes=[pltpu.SemaphoreType.DMA((3 * nt,)), pltpu.SemaphoreType.DMA((3 * nt,)),
                        pltpu.SemaphoreType.DMA((nt,))],
    )(*chip_sums)


SMALL = (("sc_conv_w", True), ("sc_conv_b", False), ("lru_b_in", True), ("lru_conv_w", True),
         ("lru_conv_b", True), ("lru_b_gate", True), ("lru_lambda", True), ("ffn_conv_w", True),
         ("ffn_conv_b", False), ("ln_g", True), ("ln_b", True))
PACK_ROW_MULT = 2 * SUB16


def _pack_rows(shapes):
    n = sum(math.prod(shapes[name]) for name, _ in SMALL)
    rows = -(-n // 128)
    return -(-rows // PACK_ROW_MULT) * PACK_ROW_MULT


def _pack_local(vals, shapes):
    flat = jnp.concatenate([vals[name].reshape(-1) for name, _ in SMALL])
    rows = _pack_rows(shapes)
    return jnp.pad(flat, (0, rows * 128 - flat.shape[0])).reshape(rows, 128)


def _unpack_local(pack, shapes):
    flat = pack.reshape(-1)
    out, off = {}, 0
    for name, _ in SMALL:
        n = math.prod(shapes[name])
        out[name] = flat[off:off + n].reshape(shapes[name])
        off += n
    return out


def _pack_slots(fulls, shapes):
    parts = []
    for name, sharded in SMALL:
        v = fulls[name]
        if sharded:
            ns = shapes[name][-1]
            v = jnp.moveaxis(v.reshape(v.shape[:-1] + (N_CHIPS, ns)), -2, 0).reshape(N_CHIPS, -1)
        else:
            v = jnp.broadcast_to(v.reshape(1, -1), (N_CHIPS, v.size))
        parts.append(v)
    flat = jnp.concatenate(parts, axis=1)
    rows = _pack_rows(shapes)
    return jnp.pad(flat, ((0, 0), (0, rows * 128 - flat.shape[1]))).reshape(N_CHIPS, rows, 128)


def _unpack_slots(packs, shapes):
    flat = packs.reshape(N_CHIPS, -1)
    out, off = {}, 0
    for name, sharded in SMALL:
        n = math.prod(shapes[name])
        if sharded:
            seg = flat[:, off:off + n].reshape((N_CHIPS,) + tuple(shapes[name]))
            seg = jnp.moveaxis(seg, 0, -2)
            out[name] = seg.reshape(seg.shape[:-2] + (N_CHIPS * shapes[name][-1],))
        off += n
    return out


WEIGHTS = ("sc_w_in", "sc_conv_w", "sc_conv_b", "sc_w_out", "lru_w_in", "lru_b_in", "lru_conv_w", "lru_conv_b",
           "lru_w_gate", "lru_b_gate", "lru_lambda", "lru_w_out", "ffn_w_up", "ffn_conv_w", "ffn_conv_b",
           "ffn_w_down", "ln_g", "ln_b")
BIG = (("sc_w_in", "col"), ("sc_w_out", "row"), ("lru_w_in", "col"), ("lru_w_gate", "lead"),
       ("lru_w_out", "row"), ("ffn_w_up", "col"), ("ffn_w_down", "row"))


def kernel(x, sc_w_in, sc_conv_w, sc_conv_b, sc_w_out, lru_w_in, lru_b_in, lru_conv_w, lru_conv_b, lru_w_gate, lru_b_gate, lru_lambda, lru_w_out, ffn_w_up, ffn_conv_w, ffn_conv_b, ffn_w_down, ln_g, ln_b, loss_target, m_sc_w_in, m_sc_conv_w, m_sc_conv_b, m_sc_w_out, m_lru_w_in, m_lru_b_in, m_lru_conv_w, m_lru_conv_b, m_lru_w_gate, m_lru_b_gate, m_lru_lambda, m_lru_w_out, m_ffn_w_up, m_ffn_conv_w, m_ffn_conv_b, m_ffn_w_down, m_ln_g, m_ln_b, v_sc_w_in, v_sc_conv_w, v_sc_conv_b, v_sc_w_out, v_lru_w_in, v_lru_b_in, v_lru_conv_w, v_lru_conv_b, v_lru_w_gate, v_lru_b_gate, v_lru_lambda, v_lru_w_out, v_ffn_w_up, v_ffn_conv_w, v_ffn_conv_b, v_ffn_w_down, v_ln_g, v_ln_b):
    w = dict(zip(WEIGHTS, (sc_w_in, sc_conv_w, sc_conv_b, sc_w_out, lru_w_in, lru_b_in, lru_conv_w, lru_conv_b,
                           lru_w_gate, lru_b_gate, lru_lambda, lru_w_out, ffn_w_up, ffn_conv_w, ffn_conv_b,
                           ffn_w_down, ln_g, ln_b)))
    mom = dict(zip(WEIGHTS, (m_sc_w_in, m_sc_conv_w, m_sc_conv_b, m_sc_w_out, m_lru_w_in, m_lru_b_in, m_lru_conv_w,
                             m_lru_conv_b, m_lru_w_gate, m_lru_b_gate, m_lru_lambda, m_lru_w_out, m_ffn_w_up,
                             m_ffn_conv_w, m_ffn_conv_b, m_ffn_w_down, m_ln_g, m_ln_b)))
    vel = dict(zip(WEIGHTS, (v_sc_w_in, v_sc_conv_w, v_sc_conv_b, v_sc_w_out, v_lru_w_in, v_lru_b_in, v_lru_conv_w,
                             v_lru_conv_b, v_lru_w_gate, v_lru_b_gate, v_lru_lambda, v_lru_w_out, v_ffn_w_up,
                             v_ffn_conv_w, v_ffn_conv_b, v_ffn_w_down, v_ln_g, v_ln_b)))
    bd, seq, d = x.shape
    t = bd * seq
    small_shapes = {name: w[name].shape for name, _ in SMALL}
    big_names = [n for n, _ in BIG]
    kinds = [k for _, k in BIG] + ["lead"]
    shard_shapes = [w[n].shape for n in big_names]

    w_pack = _pack_local(w, small_shapes)
    rs_shapes = shard_shapes + [w_pack.shape]
    placed = [_cast_place(w[n], k, BF16, "place_w") for n, k in BIG] + [_cast_place(w_pack, "lead", F32, "place_w")]
    gathered = _all_gather(placed, kinds, rs_shapes)
    full = dict(zip(big_names, gathered[:-1]))
    full.update(_unpack_slots(gathered[-1], small_shapes))
    full["sc_conv_b"] = sc_conv_b
    full["ffn_conv_b"] = ffn_conv_b
    wg_full = jnp.moveaxis(full["lru_w_gate"], 0, -2)
    wg_full = wg_full.reshape(wg_full.shape[:-2] + (2 * LRU_BLOCK,))
    f = full["ffn_w_down"].shape[1]
    rw = full["lru_w_out"].shape[1]

    x0 = x.reshape(t, d)
    xb = x0.astype(BF16)
    cur, cur_b = x0, xb
    saved = []

    for i in range(DEPTH):
        j = i // 2
        s = {"xb": cur_b}
        if i % 2 == 0:
            h = _mm_nn(cur_b, full["sc_w_in"], j, None, 768, "sc_in")
            q = _sc_fwd(h, full["sc_conv_w"][j], full["sc_conv_b"][j][None], seq, "sc_fwd")
            z1, x1, x1b = _mm_nn_ln(q, full["sc_w_out"], j, cur, full["ln_g"][i, 0][None], full["ln_b"][i, 0][None],
                                    "sc_out_ln")
        else:
            h = _mm_nn(cur_b, full["lru_w_in"], j, full["lru_b_in"][j][None], 1280, "lru_in")
            hs, q = _lru_fwd(h, full["lru_conv_w"][j], full["lru_conv_b"][j][None], wg_full[j],
                             full["lru_b_gate"][j], full["lru_lambda"][j][None], seq, "lru_fwd")
            s["hs"] = hs
            z1, x1, x1b = _mm_nn_ln(q, full["lru_w_out"], j, cur, full["ln_g"][i, 0][None],
                                    full["ln_b"][i, 0][None], "lru_out_ln")
        s.update(h=h, q=q, z1=z1, x1b=x1b)
        h2 = _mm_nn(x1b, full["ffn_w_up"], i, None, f // 2, "ffn_up")
        act = _ffn_fwd(h2, full["ffn_conv_w"][i], full["ffn_conv_b"][i][None], seq, f // 2, "ffn_fwd")
        z2, x2, x2b = _mm_nn_ln(act, full["ffn_w_down"], i, x1, full["ln_g"][i, 1][None], full["ln_b"][i, 1][None],
                                "ffn_down_ln")
        s.update(h2=h2, act=act, z2=z2)
        saved.append(s)
        cur, cur_b = x2, x2b

    dcur, loss_parts = _loss_bwd(cur, loss_target.reshape(t, d))
    loss = lax.psum(jnp.sum(loss_parts), MESH_AXES)

    gp = {n: [None] * w[n].shape[0] for n in WEIGHTS}
    for i in reversed(range(DEPTH)):
        j = i // 2
        s = saved[i]
        dz2, dz2b, dg, db = _ln_bwd(dcur, s["z2"], full["ln_g"][i, 1][None], "ln_bwd")
        gp["ln_g"][i] = [None, dg[0]]
        gp["ln_b"][i] = [None, db[0]]
        dact = _mm_nt(dz2b, full["ffn_w_down"], i, f // 2, "ffn_down_dx")
        gp["ffn_w_down"][i] = _mm_tn(s["act"], dz2b[None], f // 2, d, "ffn_down_dw")
        dh3, dcwg, dcwv, dcbg, dcbv = _ffn_bwd(s["h2"], dact, full["ffn_conv_w"][i], full["ffn_conv_b"][i][None],
                                               seq, f // 2, "ffn_bwd")
        gp["ffn_conv_w"][i] = jnp.concatenate([dcwg, dcwv], axis=1)
        gp["ffn_conv_b"][i] = jnp.concatenate([dcbg[0], dcbv[0]])
        dx1 = _mm_nt_res(dh3, full["ffn_w_up"], i, dz2, f // 2, "ffn_up_dx")
        gp["ffn_w_up"][i] = _mm_tn(s["x1b"], dh3, d, f // 2, "ffn_up_dw")
        dz1, dz1b, dg, db = _ln_bwd(dx1, s["z1"], full["ln_g"][i, 0][None], "ln_bwd")
        gp["ln_g"][i][0] = dg[0]
        gp["ln_b"][i][0] = db[0]
        gp["ln_g"][i] = jnp.stack(gp["ln_g"][i])
        gp["ln_b"][i] = jnp.stack(gp["ln_b"][i])
        if i % 2 == 0:
            dq = _mm_nt(dz1b, full["sc_w_out"], j, d, "sc_out_dx")
            gp["sc_w_out"][j] = _mm_tn(s["q"], dz1b[None], d, d, "sc_out_dw")
            dh3, dcw, dcb = _sc_bwd(s["h"], dq, full["sc_conv_w"][j], full["sc_conv_b"][j][None], seq, "sc_bwd")
            gp["sc_conv_w"][j] = dcw
            gp["sc_conv_b"][j] = dcb[0]
            dcur = _mm_nt_res(dh3, full["sc_w_in"], j, dz1, d, "sc_in_dx")
            gp["sc_w_in"][j] = _mm_tn(s["xb"], dh3, d, d, "sc_in_dw")
        else:
            dq = _mm_nt(dz1b, full["lru_w_out"], j, rw, "lru_out_dx")
            gp["lru_w_out"][j] = _mm_tn(s["q"], dz1b[None], rw, d, "lru_out_dw")
            dh3, dbin, dcw, dcb, dwg, dbg, dlam = _lru_bwd(
                s["h"], s["hs"], dq, full["lru_conv_w"][j], full["lru_conv_b"][j][None], wg_full[j],
                full["lru_b_gate"][j], full["lru_lambda"][j][None], seq, "lru_bwd")
            gp["lru_b_in"][j] = dbin[0]
            gp["lru_conv_w"][j] = dcw
            gp["lru_conv_b"][j] = dcb[0]
            gp["lru_w_gate"][j] = dwg
            gp["lru_b_gate"][j] = dbg
            gp["lru_lambda"][j] = dlam[0]
            dcur = _mm_nt_res(dh3, full["lru_w_in"], j, dz1, rw, "lru_in_dx")
            gp["lru_w_in"][j] = _mm_tn(s["xb"], dh3, d, rw, "lru_in_dw")
    grad_x = dcur.reshape(bd, seq, d)
    gp = {n: jnp.stack(v) for n, v in gp.items()}

    ns_gate = w["lru_w_gate"].shape[-1]
    gate = gp["lru_w_gate"]
    gate = jnp.moveaxis(gate.reshape(gate.shape[:-1] + (N_CHIPS, ns_gate)), -2, 0).astype(BF16)
    partials = [gate if n == "lru_w_gate" else gp[n] for n in big_names]
    partials.append(_pack_slots(gp, small_shapes))
    chip_sums = []
    for p, kind in zip(partials, kinds):
        lead = kind == "lead"
        p4 = p.reshape(N_CHIPS if lead else 1, 2, -1, p.shape[-1])
        from_sibling = _d2d_stream(p4, True, "rs_swap")
        half_full = list(p.shape)
        half_full[1 if lead else 0] //= 2
        chip_sums.append(_add_pair(p4, from_sibling, "rs_add_pair").reshape(half_full))
    arrived = _rs_all_to_all(chip_sums, kinds, rs_shapes)

    g_out, d_out, m_out, v_out = {}, {}, {}, {}
    packs = {}
    for n, r in zip(big_names + ["pack"], arrived):
        g_mine = _add_chips(r, "rs_add_chips").reshape(-1, r.shape[-1])
        g_sib = _d2d_stream(g_mine[None, None], False, "rs_share")[0]
        if n == "pack":
            packs = _adamw(w_pack, g_mine, g_sib, _pack_local(mom, small_shapes), _pack_local(vel, small_shapes),
                           "adamw")
        else:
            g_out[n], d_out[n], m_out[n], v_out[n] = _adamw(w[n], g_mine, g_sib, mom[n], vel[n], "adamw")
    for dst, pack in zip((g_out, d_out, m_out, v_out), packs):
        dst.update(_unpack_local(pack, small_shapes))

    return (loss, grad_x, *[g_out[n] for n in WEIGHTS], *[d_out[n] for n in WEIGHTS],
            *[m_out[n] for n in WEIGHTS], *[v_out[n] for n in WEIGHTS])
```

```python
import math

import jax
import jax.numpy as jnp
from jax import lax
from jax.experimental import pallas as pl
from jax.experimental.pallas import tpu as pltpu

F32 = jnp.float32
BF16 = jnp.bfloat16

DEPTH = 4
LRU_HEADS = 10
LRU_BLOCK = 128
LRU_C = 8.0
LN_EPS = 1e-5
ALPHA = (2.0 * DEPTH) ** 0.25
ADAM_LR, ADAM_B1, ADAM_B2, ADAM_EPS, ADAM_WD, ADAM_STEP = 0.001, 0.9, 0.999, 1e-08, 0.01, 10
N_CHIPS = 4
MESH_AXES = ("x", "y", "c")

VMEM_LIMIT_BYTES = 48 * 1024 * 1024
TM_MM = 512
TT_MM = 512
TM_SC = 256
FFN_COL_BLOCKS = 2
RC = 64
LANE = 128
TS_LRU = 128
TM_LN = 512
ELEMS_PER_BLOCK = 256 * 1024
STREAM_ELEMS_PER_BLOCK = 1024 * 1024
SUB = 8
SUB16 = 16


def _cp(*sem):
    return pltpu.CompilerParams(dimension_semantics=sem, vmem_limit_bytes=VMEM_LIMIT_BYTES)


def _sigmoid(v):
    return 1.0 / (1.0 + jnp.exp(-v))


def _softplus(v):
    e = jnp.exp(-jnp.abs(v))
    log1p = jnp.where(e < 1e-3, e * (1.0 - e * (0.5 - e * (1.0 / 3.0))), jnp.log(1.0 + e))
    return jnp.maximum(v, 0.0) + log1p


def _one_minus_exp(v):
    series = -v * (1.0 + v * (0.5 + v * (1.0 / 6.0 + v * (1.0 / 24.0))))
    return jnp.where(v > -0.02, series, 1.0 - jnp.exp(v))


def _gelu_and_grad(v):
    k = math.sqrt(2.0 / math.pi)
    t = jnp.tanh(k * (v + 0.044715 * v * v * v))
    val = 0.5 * v * (1.0 + t)
    grad = 0.5 * (1.0 + t) + 0.5 * v * (1.0 - t * t) * k * (1.0 + 3.0 * 0.044715 * v * v)
    return val, grad


def _down(ext, k, n_head):
    if k:
        ext = pltpu.roll(ext, k, 0)
    return ext[n_head:]


def _up(ext, k, n):
    if k:
        ext = pltpu.roll(ext, ext.shape[0] - k, 0)
    return ext[:n]


def _row(ref, k):
    return ref[k:k + 1, :]


def _colsum(v):
    return jnp.sum(v, axis=0, keepdims=True)


def _mm_nn(a, w3, l, bias, tn, name):
    m, k = a.shape
    n = w3.shape[2]
    tm = min(TM_MM, m)
    has_bias = bias is not None

    def body(*refs):
        if has_bias:
            a_ref, w_ref, b_ref, o_ref = refs
        else:
            a_ref, w_ref, o_ref = refs
        acc = jnp.dot(a_ref[...], w_ref[...], preferred_element_type=F32)
        if has_bias:
            acc = acc + b_ref[...]
        o_ref[...] = acc.astype(o_ref.dtype)

    in_specs = [pl.BlockSpec((tm, k), lambda i, j: (i, 0)),
                pl.BlockSpec((None, k, tn), lambda i, j: (l, 0, j))]
    args = [a, w3]
    if has_bias:
        in_specs.append(pl.BlockSpec((1, tn), lambda i, j: (0, j)))
        args.append(bias)
    return pl.pallas_call(
        body, name=name, grid=(m // tm, n // tn), in_specs=in_specs,
        out_specs=pl.BlockSpec((tm, tn), lambda i, j: (i, j)),
        out_shape=jax.ShapeDtypeStruct((m, n), BF16),
        compiler_params=_cp("parallel", "arbitrary"))(*args)


def _mm_nn_ln(a, w3, l, xres, g, b, name):
    m, k = a.shape
    n = w3.shape[2]
    tm = min(TM_MM, m)

    def body(a_ref, w_ref, x_ref, g_ref, b_ref, z_ref, xn_ref, xb_ref):
        y = jnp.dot(a_ref[...], w_ref[...], preferred_element_type=F32)
        z = ALPHA * x_ref[...] + y
        mu = jnp.mean(z, axis=-1, keepdims=True)
        zc = z - mu
        var = jnp.mean(zc * zc, axis=-1, keepdims=True)
        xn = zc * lax.rsqrt(var + LN_EPS) * g_ref[...] + b_ref[...]
        z_ref[...] = z
        xn_ref[...] = xn
        xb_ref[...] = xn.astype(BF16)

    row = pl.BlockSpec((tm, n), lambda i: (i, 0))
    vec = pl.BlockSpec((1, n), lambda i: (0, 0))
    return pl.pallas_call(
        body, name=name, grid=(m // tm,),
        in_specs=[pl.BlockSpec((tm, k), lambda i: (i, 0)),
                  pl.BlockSpec((None, k, n), lambda i: (l, 0, 0)), row, vec, vec],
        out_specs=[row, row, row],
        out_shape=[jax.ShapeDtypeStruct((m, n), F32), jax.ShapeDtypeStruct((m, n), F32),
                   jax.ShapeDtypeStruct((m, n), BF16)],
        compiler_params=_cp("parallel"))(a, w3, xres, g, b)


def _mm_nt(a, w3, l, tk, name):
    m, n = a.shape
    kd = w3.shape[1]
    tm = min(TM_MM, m)

    def body(a_ref, w_ref, o_ref):
        o_ref[...] = lax.dot_general(a_ref[...], w_ref[...], (((1,), (1,)), ((), ())),
                                     preferred_element_type=F32).astype(o_ref.dtype)

    return pl.pallas_call(
        body, name=name, grid=(m // tm, kd // tk),
        in_specs=[pl.BlockSpec((tm, n), lambda i, j: (i, 0)),
                  pl.BlockSpec((None, tk, n), lambda i, j: (l, j, 0))],
        out_specs=pl.BlockSpec((tm, tk), lambda i, j: (i, j)),
        out_shape=jax.ShapeDtypeStruct((m, kd), BF16),
        compiler_params=_cp("parallel", "arbitrary"))(a, w3)


def _mm_nt_res(dh3, w3, l, dz, tc, name):
    g, m, cg = dh3.shape
    kd = w3.shape[1]
    ncg = cg // tc
    nk = g * ncg
    tm = min(TM_MM, m)

    def body(a_ref, w_ref, dz_ref, o_ref, acc):
        k = pl.program_id(1)

        @pl.when(k == 0)
        def _():
            acc[...] = ALPHA * dz_ref[...]

        acc[...] += lax.dot_general(a_ref[...], w_ref[...], (((1,), (1,)), ((), ())),
                                    preferred_element_type=F32)

        @pl.when(k == nk - 1)
        def _():
            o_ref[...] = acc[...]

    return pl.pallas_call(
        body, name=name, grid=(m // tm, nk),
        in_specs=[pl.BlockSpec((None, tm, tc), lambda i, k: (k // ncg, i, k % ncg)),
                  pl.BlockSpec((None, kd, tc), lambda i, k: (l, 0, k)),
                  pl.BlockSpec((tm, kd), lambda i, k: (i, 0))],
        out_specs=pl.BlockSpec((tm, kd), lambda i, k: (i, 0)),
        out_shape=jax.ShapeDtypeStruct((m, kd), F32),
        scratch_shapes=[pltpu.VMEM((tm, kd), F32)],
        compiler_params=_cp("parallel", "arbitrary"))(dh3, w3, dz)


def _mm_tn(a, b3, tka, tnb, name):
    t, ka = a.shape
    g, _, cg = b3.shape
    ncg = cg // tnb
    tt = min(TT_MM, t)
    nt = t // tt

    def body(a_ref, b_ref, o_ref, acc):
        s = pl.program_id(2)

        @pl.when(s == 0)
        def _():
            acc[...] = jnp.zeros_like(acc)

        acc[...] += lax.dot_general(a_ref[...], b_ref[...], (((0,), (0,)), ((), ())),
                                    preferred_element_type=F32)

        @pl.when(s == nt - 1)
        def _():
            o_ref[...] = acc[...].astype(o_ref.dtype)

    return pl.pallas_call(
        body, name=name, grid=(ka // tka, g * ncg, nt),
        in_specs=[pl.BlockSpec((tt, tka), lambda i, j, s: (s, i)),
                  pl.BlockSpec((None, tt, tnb), lambda i, j, s: (j // ncg, s, j % ncg))],
        out_specs=pl.BlockSpec((tka, tnb), lambda i, j, s: (i, j)),
        out_shape=jax.ShapeDtypeStruct((ka, g * cg), BF16),
        scratch_shapes=[pltpu.VMEM((tka, tnb), F32)],
        compiler_params=_cp("parallel", "parallel", "arbitrary"))(a, b3)


def _ln_bwd(dxn, z, g, name):
    m, d = z.shape
    tm = min(TM_LN, m)

    def body(dx_ref, z_ref, g_ref, dz_ref, dzb_ref, dg_ref, db_ref):
        @pl.when(pl.program_id(0) == 0)
        def _():
            dg_ref[...] = jnp.zeros_like(dg_ref)
            db_ref[...] = jnp.zeros_like(db_ref)

        zz = z_ref[...]
        dx = dx_ref[...]
        mu = jnp.mean(zz, axis=-1, keepdims=True)
        zc = zz - mu
        var = jnp.mean(zc * zc, axis=-1, keepdims=True)
        rstd = lax.rsqrt(var + LN_EPS)
        xh = zc * rstd
        dg_ref[...] += _colsum(dx * xh)
        db_ref[...] += _colsum(dx)
        dxh = dx * g_ref[...]
        m1 = jnp.mean(dxh, axis=-1, keepdims=True)
        m2 = jnp.mean(dxh * xh, axis=-1, keepdims=True)
        dz = rstd * (dxh - m1 - xh * m2)
        dz_ref[...] = dz
        dzb_ref[...] = dz.astype(BF16)

    row = pl.BlockSpec((tm, d), lambda i: (i, 0))
    vec = pl.BlockSpec((1, d), lambda i: (0, 0))
    return pl.pallas_call(
        body, name=name, grid=(m // tm,), in_specs=[row, row, vec],
        out_specs=[row, row, vec, vec],
        out_shape=[jax.ShapeDtypeStruct((m, d), F32), jax.ShapeDtypeStruct((m, d), BF16),
                   jax.ShapeDtypeStruct((1, d), F32), jax.ShapeDtypeStruct((1, d), F32)],
        compiler_params=_cp("arbitrary"))(dxn, z, g)


def _loss_bwd(y, target):
    m, d = y.shape
    tm = min(TM_LN, m)

    def body(y_ref, t_ref, dy_ref, ls_ref):
        @pl.when(pl.program_id(0) == 0)
        def _():
            ls_ref[...] = jnp.zeros_like(ls_ref)

        e = y_ref[...] - t_ref[...]
        dy_ref[...] = e * (1.0 / d)
        ls_ref[...] += _colsum(e * e) * (0.5 / d)

    row = pl.BlockSpec((tm, d), lambda i: (i, 0))
    return pl.pallas_call(
        body, name="loss_bwd", grid=(m // tm,), in_specs=[row, row],
        out_specs=[row, pl.BlockSpec((1, d), lambda i: (0, 0))],
        out_shape=[jax.ShapeDtypeStruct((m, d), F32), jax.ShapeDtypeStruct((1, d), F32)],
        compiler_params=_cp("arbitrary"))(y, target)


def _sc_fwd(h, cw, cb, seq, name):
    t, d3 = h.shape
    d = d3 // 3
    tm = min(TM_SC, seq)

    def body(hb_ref, hc_ref, hv_ref, cw_ref, cb_ref, q_ref, carry):
        i = pl.program_id(0)

        @pl.when(lax.rem(i * tm, seq) == 0)
        def _():
            carry[...] = jnp.zeros_like(carry)

        p = hc_ref[...].astype(F32) * hv_ref[...].astype(F32)
        ext = jnp.concatenate([carry[...], p], axis=0)
        u = cb_ref[...] + _row(cw_ref, 0) * _down(ext, 2, SUB) + _row(cw_ref, 1) * _down(ext, 1, SUB) \
            + _row(cw_ref, 2) * p
        q_ref[...] = (hb_ref[...].astype(F32) * u).astype(BF16)
        carry[...] = p[tm - SUB:, :]

    blk = lambda c: pl.BlockSpec((tm, d), lambda i: (i, c))
    return pl.pallas_call(
        body, name=name, grid=(t // tm,),
        in_specs=[blk(0), blk(1), blk(2), pl.BlockSpec((3, d), lambda i: (0, 0)),
                  pl.BlockSpec((1, d), lambda i: (0, 0))],
        out_specs=pl.BlockSpec((tm, d), lambda i: (i, 0)),
        out_shape=jax.ShapeDtypeStruct((t, d), BF16),
        scratch_shapes=[pltpu.VMEM((SUB, d), F32)],
        compiler_params=_cp("arbitrary"))(h, h, h, cw, cb)


def _sc_bwd(h, dq, cw, cb, seq, name):
    t, d3 = h.shape
    d = d3 // 3
    tm = min(TM_SC, seq)
    nt = t // tm
    hpt = tm // SUB16

    def body(hb_ref, hc_ref, hv_ref, hch_ref, hvh_ref, dq_ref, cw_ref, cb_ref,
             dh_ref, dcw_ref, dcb_ref, carry):
        i = pl.program_id(0)
        ri = nt - 1 - i

        @pl.when(i == 0)
        def _():
            dcw_ref[...] = jnp.zeros_like(dcw_ref)
            dcb_ref[...] = jnp.zeros_like(dcb_ref)

        @pl.when(lax.rem((ri + 1) * tm, seq) == 0)
        def _():
            carry[...] = jnp.zeros_like(carry)

        keep = jnp.where(lax.rem(ri * tm, seq) == 0, 0.0, 1.0)
        gb = hb_ref[...].astype(F32)
        gc = hc_ref[...].astype(F32)
        v = hv_ref[...].astype(F32)
        p = gc * v
        p_head = hch_ref[...].astype(F32) * hvh_ref[...].astype(F32) * keep
        ext = jnp.concatenate([p_head, p], axis=0)
        pm2 = _down(ext, 2, SUB16)
        pm1 = _down(ext, 1, SUB16)
        u = cb_ref[...] + _row(cw_ref, 0) * pm2 + _row(cw_ref, 1) * pm1 + _row(cw_ref, 2) * p
        dqf = dq_ref[...].astype(F32)
        du = dqf * gb
        dcb_ref[...] += _colsum(du)
        dcw_ref[0:1, :] += _colsum(du * pm2)
        dcw_ref[1:2, :] += _colsum(du * pm1)
        dcw_ref[2:3, :] += _colsum(du * p)
        ext2 = jnp.concatenate([du, carry[...]], axis=0)
        dp = _row(cw_ref, 2) * du + _row(cw_ref, 1) * _up(ext2, 1, tm) + _row(cw_ref, 0) * _up(ext2, 2, tm)
        carry[...] = du[0:SUB, :]
        dh_ref[0] = (dqf * u).astype(BF16)
        dh_ref[1] = (dp * v).astype(BF16)
        dh_ref[2] = (dp * gc).astype(BF16)

    blk = lambda c: pl.BlockSpec((tm, d), lambda i: (nt - 1 - i, c))
    head = lambda c: pl.BlockSpec((SUB16, d), lambda i: (jnp.maximum((nt - 1 - i) * hpt - 1, 0), c))
    vec = lambda r: pl.BlockSpec((r, d), lambda i: (0, 0))
    return pl.pallas_call(
        body, name=name, grid=(nt,),
        in_specs=[blk(0), blk(1), blk(2), head(1), head(2),
                  pl.BlockSpec((tm, d), lambda i: (nt - 1 - i, 0)), vec(3), vec(1)],
        out_specs=[pl.BlockSpec((3, tm, d), lambda i: (0, nt - 1 - i, 0)), vec(3), vec(1)],
        out_shape=[jax.ShapeDtypeStruct((3, t, d), BF16), jax.ShapeDtypeStruct((3, d), F32),
                   jax.ShapeDtypeStruct((1, d), F32)],
        scratch_shapes=[pltpu.VMEM((SUB, d), F32)],
        compiler_params=_cp("arbitrary"))(h, h, h, h, h, dq, cw, cb)


def _fold(v):
    return jnp.sum(v.reshape(v.shape[0] // SUB, SUB, v.shape[1]), axis=0)


def _ffn_up(xb, w3, l, cw, cb, seq, name):
    t, d = xb.shape
    f = w3.shape[2] // 2
    tc = f // FFN_COL_BLOCKS
    tm = min(TM_MM, seq)
    ncc = tc // LANE
    nrc = tm // RC

    def body(x_ref, wg_ref, wv_ref, cwg_ref, cwv_ref, cbg_ref, cbv_ref, h_ref, pre_ref, act_ref, eg, ev):
        i = pl.program_id(1)

        @pl.when(lax.rem(i * tm, seq) == 0)
        def _():
            eg[0:SUB, :] = jnp.zeros((SUB, tc), F32)
            ev[0:SUB, :] = jnp.zeros((SUB, tc), F32)

        xx = x_ref[...]
        eg[SUB:, :] = jnp.dot(xx, wg_ref[...], preferred_element_type=F32)
        ev[SUB:, :] = jnp.dot(xx, wv_ref[...], preferred_element_type=F32)
        for cc in range(ncc):
            cols = slice(cc * LANE, (cc + 1) * LANE)
            taps = [[ref[k:k + 1, cols] for k in range(3)] + [bref[:, cols]]
                    for ref, bref in ((cwg_ref, cbg_ref), (cwv_ref, cbv_ref))]

            def chunk(r, carry, cols=cols, taps=taps):
                r0 = pl.multiple_of(r * RC, RC)
                rows = pl.ds(r0, RC)
                pres = []
                for half, e_ref in enumerate((eg, ev)):
                    w0, w1, w2, bias = taps[half]
                    e = e_ref[pl.ds(r0, RC + SUB), cols]
                    cur = e[SUB:]
                    pre = bias + w0 * _down(e, 2, SUB) + w1 * _down(e, 1, SUB) + w2 * cur
                    h_ref[half, rows, cols] = cur.astype(BF16)
                    pre_ref[half, rows, cols] = pre.astype(BF16)
                    pres.append(pre)
                act_ref[rows, cols] = (pres[0] * _sigmoid(pres[0]) * pres[1]).astype(BF16)
                return carry

            lax.fori_loop(0, nrc, chunk, 0)
        eg[0:SUB, :] = eg[tm:tm + SUB, :]
        ev[0:SUB, :] = ev[tm:tm + SUB, :]

    nc = FFN_COL_BLOCKS
    wspec = lambda off: pl.BlockSpec((None, d, tc), lambda j, i: (l, 0, j + off))
    vec = lambda r, off: pl.BlockSpec((r, tc), lambda j, i: (0, j + off))
    pair = pl.BlockSpec((2, tm, tc), lambda j, i: (0, i, j))
    return pl.pallas_call(
        body, name=name, grid=(nc, t // tm),
        in_specs=[pl.BlockSpec((tm, d), lambda j, i: (i, 0)), wspec(0), wspec(nc),
                  vec(3, 0), vec(3, nc), vec(1, 0), vec(1, nc)],
        out_specs=[pair, pair, pl.BlockSpec((tm, tc), lambda j, i: (i, j))],
        out_shape=[jax.ShapeDtypeStruct((2, t, f), BF16), jax.ShapeDtypeStruct((2, t, f), BF16),
                   jax.ShapeDtypeStruct((t, f), BF16)],
        scratch_shapes=[pltpu.VMEM((tm + SUB, tc), F32), pltpu.VMEM((tm + SUB, tc), F32)],
        compiler_params=_cp("arbitrary", "arbitrary"))(xb, w3, w3, cw, cw, cb, cb)


def _ffn_down_bwd(dzb, wd3, l, h3, pre3, cw, seq, name):
    t, d = dzb.shape
    f = wd3.shape[1]
    tc = f // FFN_COL_BLOCKS
    tm = min(TM_MM, seq)
    nt = t // tm
    ncc = tc // LANE
    nrc = tm // RC

    def body(dz_ref, wd_ref, h_ref, pre_ref, cwg_ref, cwv_ref,
             dh_ref, dcwg_ref, dcwv_ref, dcbg_ref, dcbv_ref, da_s, carry):
        i = pl.program_id(1)
        ri = nt - 1 - i

        @pl.when(i == 0)
        def _():
            for r in (dcwg_ref, dcwv_ref, dcbg_ref, dcbv_ref):
                r[...] = jnp.zeros_like(r)

        @pl.when(lax.rem((ri + 1) * tm, seq) == 0)
        def _():
            carry[...] = jnp.zeros_like(carry)

        da_s[...] = lax.dot_general(dz_ref[...], wd_ref[...], (((1,), (1,)), ((), ())),
                                    preferred_element_type=F32)
        for cc in range(ncc):
            cols = slice(cc * LANE, (cc + 1) * LANE)
            taps = [[ref[k:k + 1, cols] for k in range(3)] for ref in (cwg_ref, cwv_ref)]

            def chunk(it, acc, cols=cols, taps=taps):
                r0 = pl.multiple_of((nrc - 1 - it) * RC, RC)
                rows = pl.ds(r0, RC)
                da = da_s[rows, cols]
                gp = pre_ref[0, rows, cols].astype(F32)
                vp = pre_ref[1, rows, cols].astype(F32)
                sg = _sigmoid(gp)
                dpres = (da * vp * (sg * (1.0 + gp * (1.0 - sg))), da * (gp * sg))
                out = []
                for half in range(2):
                    w0, w1, w2 = taps[half]
                    dpre = dpres[half]
                    ext = jnp.concatenate([dpre, carry[half, :, cols]], axis=0)
                    u1 = _up(ext, 1, RC)
                    u2 = _up(ext, 2, RC)
                    carry[half, :, cols] = dpre[0:SUB]
                    dh_ref[half, rows, cols] = (w2 * dpre + w1 * u1 + w0 * u2).astype(BF16)
                    hh = h_ref[half, rows, cols].astype(F32)
                    out += [_fold(hh * u2), _fold(hh * u1), _fold(hh * dpre), _fold(dpre)]
                return tuple(a + o for a, o in zip(acc, out))

            zero = jnp.zeros((SUB, LANE), F32)
            acc = lax.fori_loop(0, nrc, chunk, (zero,) * 8)
            for half, (dcw_ref, dcb_ref) in enumerate(((dcwg_ref, dcbg_ref), (dcwv_ref, dcbv_ref))):
                for k in range(3):
                    dcw_ref[k:k + 1, cols] += _colsum(acc[4 * half + k])
                dcb_ref[:, cols] += _colsum(acc[4 * half + 3])

    nc = FFN_COL_BLOCKS
    pair = pl.BlockSpec((2, tm, tc), lambda j, i: (0, nt - 1 - i, j))
    vec = lambda off: pl.BlockSpec((3, tc), lambda j, i: (0, j + off))
    acc_spec = lambda r: pl.BlockSpec((r, tc), lambda j, i: (0, j))
    return pl.pallas_call(
        body, name=name, grid=(nc, nt),
        in_specs=[pl.BlockSpec((tm, d), lambda j, i: (nt - 1 - i, 0)),
                  pl.BlockSpec((None, tc, d), lambda j, i: (l, j, 0)), pair, pair, vec(0), vec(nc)],
        out_specs=[pair, acc_spec(3), acc_spec(3), acc_spec(1), acc_spec(1)],
        out_shape=[jax.ShapeDtypeStruct((2, t, f), BF16), jax.ShapeDtypeStruct((3, f), F32),
                   jax.ShapeDtypeStruct((3, f), F32), jax.ShapeDtypeStruct((1, f), F32),
                   jax.ShapeDtypeStruct((1, f), F32)],
        scratch_shapes=[pltpu.VMEM((tm, tc), F32), pltpu.VMEM((2, SUB, tc), F32)],
        compiler_params=_cp("arbitrary", "arbitrary"))(dzb, wd3, h3, pre3, cw, cw)


def _lru_gates(xr, wg_ref, bg_ref):
    rs, gs = [], []
    for hd in range(LRU_HEADS):
        xh = xr[:, hd * LRU_BLOCK:(hd + 1) * LRU_BLOCK].astype(BF16)
        gt = jnp.dot(xh, wg_ref[hd], preferred_element_type=F32) + _row(bg_ref, hd)
        rs.append(gt[:, :LRU_BLOCK])
        gs.append(gt[:, LRU_BLOCK:])
    return jnp.concatenate(rs, axis=1), jnp.concatenate(gs, axis=1)


def _lru_coeffs(xr, wg_ref, bg_ref, lam_ref):
    gr, gi = _lru_gates(xr, wg_ref, bg_ref)
    r = _sigmoid(gr)
    ig = _sigmoid(gi)
    sp = _softplus(-lam_ref[...])
    log_a = -LRU_C * r * sp
    a = jnp.exp(log_a)
    mult = jnp.sqrt(_one_minus_exp(2.0 * log_a))
    return r, ig, sp, a, mult


def _lru_fwd(h, cw, cb, wg, bg, lam, seq, name):
    t, r2 = h.shape
    rw = r2 // 2
    ts = min(TS_LRU, seq)
    n8 = ts // SUB

    def body(hg_ref, hr_ref, cw_ref, cb_ref, wg_ref, bg_ref, lam_ref, hs_ref, y_ref,
             a_s, b_s, cconv, cstate):
        i = pl.program_id(0)

        @pl.when(lax.rem(i * ts, seq) == 0)
        def _():
            cconv[...] = jnp.zeros_like(cconv)
            cstate[...] = jnp.zeros_like(cstate)

        rin = hr_ref[...].astype(F32)
        ext = jnp.concatenate([cconv[...], rin], axis=0)
        xr = cb_ref[...]
        for k in range(4):
            xr = xr + _row(cw_ref, k) * _down(ext, 3 - k, SUB)
        cconv[...] = rin[ts - SUB:, :]
        _, ig, _, a, mult = _lru_coeffs(xr, wg_ref, bg_ref, lam_ref)
        a_s[...] = a
        b_s[...] = mult * (ig * xr)
        row = lax.broadcasted_iota(jnp.int32, (SUB, rw), 0)

        def step(j, carry):
            off = pl.multiple_of(j * SUB, SUB)
            a8 = a_s[pl.ds(off, SUB), :]
            b8 = b_s[pl.ds(off, SUB), :]
            for d in (1, 2, 4):
                m = row >= d
                b8 = jnp.where(m, a8 * pltpu.roll(b8, d, 0) + b8, b8)
                a8 = jnp.where(m, a8 * pltpu.roll(a8, d, 0), a8)
            h8 = a8 * carry + b8
            hs_ref[pl.ds(off, SUB), :] = h8
            return _colsum(jnp.where(row == SUB - 1, h8, 0.0))

        cstate[...] = lax.fori_loop(0, n8, step, cstate[...])
        gel, _ = _gelu_and_grad(hg_ref[...].astype(F32))
        y_ref[...] = (hs_ref[...] * gel).astype(BF16)

    full = lambda shp: pl.BlockSpec(shp, lambda i: (0,) * len(shp))
    return pl.pallas_call(
        body, name=name, grid=(t // ts,),
        in_specs=[pl.BlockSpec((ts, rw), lambda i: (i, 0)), pl.BlockSpec((ts, rw), lambda i: (i, 1)),
                  full((4, rw)), full((1, rw)), full(wg.shape), full(bg.shape), full((1, rw))],
        out_specs=[pl.BlockSpec((ts, rw), lambda i: (i, 0)), pl.BlockSpec((ts, rw), lambda i: (i, 0))],
        out_shape=[jax.ShapeDtypeStruct((t, rw), F32), jax.ShapeDtypeStruct((t, rw), BF16)],
        scratch_shapes=[pltpu.VMEM((ts, rw), F32), pltpu.VMEM((ts, rw), F32),
                        pltpu.VMEM((SUB, rw), F32), pltpu.VMEM((1, rw), F32)],
        compiler_params=_cp("arbitrary"))(h, h, cw, cb, wg, bg, lam)


def _lru_bwd(h, hs, dy, cw, cb, wg, bg, lam, seq, name):
    t, r2 = h.shape
    rw = r2 // 2
    ts = min(TS_LRU, seq)
    nt = t // ts
    n8 = ts // SUB
    hp16 = ts // SUB16
    hp8 = ts // SUB

    def body(hg_ref, hr_ref, hrh_ref, hs_ref, hsh_ref, dy_ref, cw_ref, cb_ref, wg_ref, bg_ref, lam_ref,
             dh_ref, dbin_ref, dcw_ref, dcb_ref, dwg_ref, dbg_ref, dlam_ref,
             a_s, g_s, l_s, c_lam, c_a, c_dxr):
        i = pl.program_id(0)
        ri = nt - 1 - i

        @pl.when(i == 0)
        def _():
            for r in (dbin_ref, dcw_ref, dcb_ref, dwg_ref, dbg_ref, dlam_ref):
                r[...] = jnp.zeros_like(r)

        @pl.when(lax.rem((ri + 1) * ts, seq) == 0)
        def _():
            c_lam[...] = jnp.zeros_like(c_lam)
            c_a[...] = jnp.zeros_like(c_a)
            c_dxr[...] = jnp.zeros_like(c_dxr)

        keep = jnp.where(lax.rem(ri * ts, seq) == 0, 0.0, 1.0)
        rin = hr_ref[...].astype(F32)
        ext = jnp.concatenate([hrh_ref[...].astype(F32) * keep, rin], axis=0)
        shifted = [_down(ext, 3 - k, SUB16) for k in range(4)]
        xr = cb_ref[...]
        for k in range(4):
            xr = xr + _row(cw_ref, k) * shifted[k]
        r, ig, sp, a, mult = _lru_coeffs(xr, wg_ref, bg_ref, lam_ref)
        gel, dgel = _gelu_and_grad(hg_ref[...].astype(F32))
        dyf = dy_ref[...].astype(F32)
        hsv = hs_ref[...]
        dg = dyf * hsv * dgel

        a_s[...] = _up(jnp.concatenate([a, c_a[...]], axis=0), 1, ts)
        g_s[...] = dyf * gel
        c_a[...] = a[0:SUB, :]
        row = lax.broadcasted_iota(jnp.int32, (SUB, rw), 0)

        def step(j, carry):
            off = pl.multiple_of((n8 - 1 - j) * SUB, SUB)
            a8 = a_s[pl.ds(off, SUB), :]
            b8 = g_s[pl.ds(off, SUB), :]
            for d in (1, 2, 4):
                m = row < SUB - d
                b8 = jnp.where(m, a8 * pltpu.roll(b8, SUB - d, 0) + b8, b8)
                a8 = jnp.where(m, a8 * pltpu.roll(a8, SUB - d, 0), a8)
            l8 = a8 * carry + b8
            l_s[pl.ds(off, SUB), :] = l8
            return _colsum(jnp.where(row == 0, l8, 0.0))

        c_lam[...] = lax.fori_loop(0, n8, step, c_lam[...])
        lamv = l_s[...]
        hs_prev = _down(jnp.concatenate([hsh_ref[...] * keep, hsv], axis=0), 1, SUB)
        da = lamv * hs_prev
        t1 = lamv * xr
        dmult = t1 * ig
        dig = t1 * mult
        dxr = lamv * mult * ig
        dla = da * a - dmult * (a * a) / mult
        dr = dla * (-LRU_C * sp)
        dlam_ref[...] += _colsum(dla * (-LRU_C) * r) * (-_sigmoid(-lam_ref[...]))
        dgr = dr * r * (1.0 - r)
        dgi = dig * ig * (1.0 - ig)
        parts = []
        for hd in range(LRU_HEADS):
            sl = slice(hd * LRU_BLOCK, (hd + 1) * LRU_BLOCK)
            dgt = jnp.concatenate([dgr[:, sl], dgi[:, sl]], axis=1)
            dbg_ref[hd:hd + 1, :] += _colsum(dgt)
            dgt16 = dgt.astype(BF16)
            parts.append(lax.dot_general(dgt16, wg_ref[hd], (((1,), (1,)), ((), ())),
                                         preferred_element_type=F32))
            dwg_ref[hd] += lax.dot_general(xr[:, sl].astype(BF16), dgt16, (((0,), (0,)), ((), ())),
                                           preferred_element_type=F32)
        dxr = dxr + jnp.concatenate(parts, axis=1)

        dcb_ref[...] += _colsum(dxr)
        for k in range(4):
            dcw_ref[k:k + 1, :] += _colsum(dxr * shifted[k])
        ext2 = jnp.concatenate([dxr, c_dxr[...]], axis=0)
        drb = _row(cw_ref, 3) * dxr
        for k in range(3):
            drb = drb + _row(cw_ref, k) * _up(ext2, 3 - k, ts)
        c_dxr[...] = dxr[0:SUB, :]
        dh_ref[0] = dg.astype(BF16)
        dh_ref[1] = drb.astype(BF16)
        dbin_ref[:, 0:rw] += _colsum(dg)
        dbin_ref[:, rw:] += _colsum(drb)

    rev = lambda c: pl.BlockSpec((ts, rw), lambda i: (nt - 1 - i, c))
    full = lambda shp: pl.BlockSpec(shp, lambda i: (0,) * len(shp))
    nh = LRU_HEADS
    return pl.pallas_call(
        body, name=name, grid=(nt,),
        in_specs=[rev(0), rev(1),
                  pl.BlockSpec((SUB16, rw), lambda i: (jnp.maximum((nt - 1 - i) * hp16 - 1, 0), 1)),
                  rev(0),
                  pl.BlockSpec((SUB, rw), lambda i: (jnp.maximum((nt - 1 - i) * hp8 - 1, 0), 0)),
                  rev(0), full((4, rw)), full((1, rw)), full(wg.shape), full(bg.shape), full((1, rw))],
        out_specs=[pl.BlockSpec((2, ts, rw), lambda i: (0, nt - 1 - i, 0)), full((1, r2)), full((4, rw)),
                   full((1, rw)), full((nh, LRU_BLOCK, 2 * LRU_BLOCK)), full((nh, 2 * LRU_BLOCK)), full((1, rw))],
        out_shape=[jax.ShapeDtypeStruct((2, t, rw), BF16), jax.ShapeDtypeStruct((1, r2), F32),
                   jax.ShapeDtypeStruct((4, rw), F32), jax.ShapeDtypeStruct((1, rw), F32),
                   jax.ShapeDtypeStruct((nh, LRU_BLOCK, 2 * LRU_BLOCK), F32),
                   jax.ShapeDtypeStruct((nh, 2 * LRU_BLOCK), F32), jax.ShapeDtypeStruct((1, rw), F32)],
        scratch_shapes=[pltpu.VMEM((ts, rw), F32), pltpu.VMEM((ts, rw), F32), pltpu.VMEM((ts, rw), F32),
                        pltpu.VMEM((1, rw), F32), pltpu.VMEM((SUB, rw), F32), pltpu.VMEM((SUB, rw), F32)],
        compiler_params=_cp("arbitrary"))(h, h, h, hs, hs, dy, cw, cb, wg, bg, lam)


def _row_tile(rows, cols, mult, elems=ELEMS_PER_BLOCK):
    cap = max(mult, elems // cols)
    best = None
    for cand in range(mult, min(rows, cap) + 1, mult):
        if rows % cand == 0:
            best = cand
    return best if best is not None else rows


def _core_index():
    return lax.axis_index("c").astype(jnp.int32).reshape(1)


def _chip_index():
    return (2 * lax.axis_index("x") + lax.axis_index("y")).astype(jnp.int32).reshape(1)


def _add_pair(p4, r3, name):
    s, _, rows, cols = p4.shape
    tr = _row_tile(rows, cols, SUB16)

    def body(c_ref, a_ref, b_ref, o_ref):
        o_ref[...] = (a_ref[...].astype(F32) + b_ref[...].astype(F32)).astype(o_ref.dtype)

    blk = pl.BlockSpec((None, tr, cols), lambda k, i, c_ref: (k, i, 0))
    return pl.pallas_call(
        body, name=name,
        grid_spec=pltpu.PrefetchScalarGridSpec(
            num_scalar_prefetch=1, grid=(s, rows // tr),
            in_specs=[pl.BlockSpec((None, None, tr, cols), lambda k, i, c_ref: (k, c_ref[0], i, 0)), blk],
            out_specs=blk),
        out_shape=jax.ShapeDtypeStruct((s, rows, cols), p4.dtype),
        compiler_params=_cp("parallel", "parallel"))(_core_index(), p4, r3)


def _add_chips(r, name):
    shape = r.shape[1:]
    r3 = r.reshape(N_CHIPS, -1, shape[-1])
    _, rows, cols = r3.shape
    tr = _row_tile(rows, cols, SUB16)

    def body(r_ref, o_ref):
        s = r_ref[0].astype(F32) + r_ref[1].astype(F32)
        s = s + r_ref[2].astype(F32)
        o_ref[...] = s + r_ref[3].astype(F32)

    out = pl.pallas_call(body, name=name, grid=(rows // tr,),
                         in_specs=[pl.BlockSpec((N_CHIPS, tr, cols), lambda i: (0, i, 0))],
                         out_specs=pl.BlockSpec((tr, cols), lambda i: (i, 0)),
                         out_shape=jax.ShapeDtypeStruct((rows, cols), F32),
                         compiler_params=_cp("parallel"))(r3)
    return out.reshape(shape)


def _adamw(w, g_mine, g_sib, m, v, name):
    shape = w.shape
    rows, cols = g_mine.shape
    flat = [arr.reshape(2, rows, cols) for arr in (w, m, v)]
    tr = _row_tile(rows, cols, SUB)

    def body(c_ref, w_ref, gm_ref, gs_ref, m_ref, v_ref, g_ref, d_ref, mo_ref, vo_ref):
        gg = jnp.where(pl.program_id(0) == c_ref[0], gm_ref[...], gs_ref[...])
        m2 = ADAM_B1 * m_ref[...] + (1.0 - ADAM_B1) * gg
        v2 = ADAM_B2 * v_ref[...] + (1.0 - ADAM_B2) * (gg * gg)
        m_hat = m2 / (1.0 - ADAM_B1 ** ADAM_STEP)
        v_hat = v2 / (1.0 - ADAM_B2 ** ADAM_STEP)
        g_ref[...] = gg
        d_ref[...] = -ADAM_LR * (m_hat / (jnp.sqrt(v_hat) + ADAM_EPS) + ADAM_WD * w_ref[...])
        mo_ref[...] = m2
        vo_ref[...] = v2

    blk = pl.BlockSpec((None, tr, cols), lambda hh, i, c_ref: (hh, i, 0))
    gblk = pl.BlockSpec((tr, cols), lambda hh, i, c_ref: (i, 0))
    outs = pl.pallas_call(
        body, name=name,
        grid_spec=pltpu.PrefetchScalarGridSpec(
            num_scalar_prefetch=1, grid=(2, rows // tr),
            in_specs=[blk, gblk, gblk, blk, blk], out_specs=[blk] * 4),
        out_shape=[jax.ShapeDtypeStruct((2, rows, cols), F32)] * 4,
        compiler_params=_cp("parallel", "parallel"))(_core_index(), flat[0], g_mine, g_sib, flat[1], flat[2])
    return tuple(o.reshape(shape) for o in outs)


def _cast_place(shard, kind, dtype, name):
    my = _chip_index()
    s = shard.shape
    if kind == "col":
        rows, ns = math.prod(s[:-1]), s[-1]
        tr = _row_tile(rows, ns, SUB16)
        src = shard.reshape(rows, ns)
        grid = (rows // tr,)
        in_spec = pl.BlockSpec((tr, ns), lambda i, my_ref: (i, 0))
        out_spec = pl.BlockSpec((tr, ns), lambda i, my_ref: (i, my_ref[0]))
        out_shape = (rows, N_CHIPS * ns)
    else:
        a, b, c = (s[0], s[1], math.prod(s[2:])) if kind == "row" else (1, math.prod(s[:-1]), s[-1])
        tr = _row_tile(b, c, SUB16)
        src = shard.reshape(a, b, c)
        grid = (a, b // tr)
        in_spec = pl.BlockSpec((None, tr, c), lambda l, i, my_ref: (l, i, 0))
        out_spec = pl.BlockSpec((None, None, tr, c), lambda l, i, my_ref: (l, my_ref[0], i, 0))
        out_shape = (a, N_CHIPS, b, c)

    def body(my_ref, i_ref, o_ref):
        o_ref[...] = i_ref[...].astype(o_ref.dtype)

    out = pl.pallas_call(
        body, name=name,
        grid_spec=pltpu.PrefetchScalarGridSpec(num_scalar_prefetch=1, grid=grid, in_specs=[in_spec],
                                               out_specs=out_spec),
        out_shape=jax.ShapeDtypeStruct(out_shape, dtype),
        compiler_params=_cp(*["parallel"] * len(grid)))(my, src)
    return out.reshape(_full_shape(kind, s))


def _full_shape(kind, shard_shape):
    s = tuple(shard_shape)
    if kind == "col":
        return s[:-1] + (N_CHIPS * s[-1],)
    if kind == "row":
        return (s[0], N_CHIPS * s[1]) + s[2:]
    return (N_CHIPS,) + s


def _slot(kind, ref, k, shard_shape):
    if kind == "col":
        n = shard_shape[-1]
        return ref.at[:, :, pl.ds(pl.multiple_of(k * n, 128), n)]
    if kind == "row":
        n = shard_shape[1]
        return ref.at[:, pl.ds(pl.multiple_of(k * n, SUB16), n), :]
    return ref.at[k]


def _half(ref, c, h):
    return ref.at[pl.ds(c * h, h)]


def _position():
    x = lax.axis_index("x")
    y = lax.axis_index("y")
    c = lax.axis_index("c")
    return x, y, c


def _peer_chip(x, y, j):
    tx = 1 - x if j & 2 else x
    ty = 1 - y if j & 1 else y
    return tx, ty


def _remote(src, dst, ssem, rsem, dev):
    return pltpu.make_async_remote_copy(src_ref=src, dst_ref=dst, send_sem=ssem, recv_sem=rsem,
                                        device_id=dev, device_id_type=pl.DeviceIdType.MESH)


_ANY = pl.BlockSpec(memory_space=pl.ANY)


def _all_gather(placed, kinds, shapes):
    nt = len(placed)

    def body(*refs):
        outs = refs[nt:2 * nt]
        ssem, rsem = refs[2 * nt:]
        x, y, c = _position()
        my = 2 * x + y
        sib = (x, y, 1 - c)
        sends, fwds = [], []
        for t in range(nt):
            h = shapes[t][0] // 2
            own = _half(_slot(kinds[t], outs[t], my, shapes[t]), c, h)
            for j in (1, 2, 3):
                tx, ty = _peer_chip(x, y, j)
                cp = _remote(own, own, ssem.at[6 * t + j - 1], rsem.at[6 * t + j - 1], (tx, ty, c))
                cp.start()
                sends.append(cp)
        for t in range(nt):
            h = shapes[t][0] // 2
            for j in (1, 2, 3):
                tx, ty = _peer_chip(x, y, j)
                got = _half(_slot(kinds[t], outs[t], 2 * tx + ty, shapes[t]), c, h)
                _remote(got, got, ssem.at[6 * t + j - 1], rsem.at[6 * t + j - 1], sib).wait_recv()
                cp = _remote(got, got, ssem.at[6 * t + 2 + j], rsem.at[6 * t + 2 + j], sib)
                cp.start()
                fwds.append(cp)
        for t in range(nt):
            h = shapes[t][0] // 2
            for j in (1, 2, 3):
                tx, ty = _peer_chip(x, y, j)
                other = _half(_slot(kinds[t], outs[t], 2 * tx + ty, shapes[t]), 1 - c, h)
                _remote(other, other, ssem.at[6 * t + 2 + j], rsem.at[6 * t + 2 + j], sib).wait_recv()
        for cp in sends + fwds:
            cp.wait_send()

    return pl.pallas_call(
        body, name="all_gather", in_specs=[_ANY] * nt, out_specs=[_ANY] * nt,
        out_shape=[jax.ShapeDtypeStruct(p.shape, p.dtype) for p in placed],
        input_output_aliases={t: t for t in range(nt)},
        scratch_shapes=[pltpu.SemaphoreType.DMA((6 * nt,)), pltpu.SemaphoreType.DMA((6 * nt,))],
    )(*placed)


def _d2d_stream(src4, other_half, name):
    s, _, rows, cols = src4.shape
    tr = _row_tile(rows, cols, SUB16, STREAM_ELEMS_PER_BLOCK)
    nblk = rows // tr

    nh = src4.shape[1]

    def body(c_ref, src_ref, dst_ref, ssem, rsem):
        k = pl.program_id(0)
        i = pl.program_id(1)
        x, y, c = _position()
        sib = (x, y, 1 - c)
        blk = dst_ref.at[pl.ds(pl.multiple_of((k * nblk + i) * tr, SUB16), tr)]
        cp = _remote(src_ref, blk, ssem, rsem, sib)
        cp.start()
        cp.wait_send()

        @pl.when(jnp.logical_and(k == s - 1, i == nblk - 1))
        def _():
            _remote(dst_ref, dst_ref, ssem, rsem, sib).wait_recv()

    if other_half:
        src_map = lambda k, i, c_ref: ((k * nh + 1 - c_ref[0]) * nblk + i, 0)
    else:
        src_map = lambda k, i, c_ref: (k * nh * nblk + i, 0)
    out = pl.pallas_call(
        body, name=name,
        grid_spec=pltpu.PrefetchScalarGridSpec(
            num_scalar_prefetch=1, grid=(s, nblk),
            in_specs=[pl.BlockSpec((tr, cols), src_map)], out_specs=_ANY,
            scratch_shapes=[pltpu.SemaphoreType.DMA, pltpu.SemaphoreType.DMA]),
        out_shape=jax.ShapeDtypeStruct((s * rows, cols), src4.dtype),
        compiler_params=_cp("arbitrary", "arbitrary"))(_core_index(), src4.reshape(s * nh * rows, cols))
    return out.reshape(s, rows, cols)


def _rs_all_to_all(chip_sums, kinds, shard_shapes):
    nt = len(chip_sums)

    def half_shard(shape):
        return (shape[0] // 2,) + tuple(shape[1:])

    def body(*refs):
        ins, outs = refs[:nt], refs[nt:2 * nt]
        ssem, rsem, lsem = refs[2 * nt:]
        x, y, c = _position()
        my = 2 * x + y
        cps = []
        for t in range(nt):
            hs = half_shard(shard_shapes[t])
            cp = pltpu.make_async_copy(_slot(kinds[t], ins[t], my, hs), outs[t].at[my], lsem.at[t])
            cp.start()
            cps.append(cp)
            for j in (1, 2, 3):
                tx, ty = _peer_chip(x, y, j)
                cp = _remote(_slot(kinds[t], ins[t], 2 * tx + ty, hs), outs[t].at[my], ssem.at[3 * t + j - 1],
                             rsem.at[3 * t + j - 1], (tx, ty, c))
                cp.start()
                cps.append(cp)
        for cp in cps:
            cp.wait()

    return pl.pallas_call(
        body, name="rs_all_to_all", in_specs=[_ANY] * nt, out_specs=[_ANY] * nt,
        out_shape=[jax.ShapeDtypeStruct((N_CHIPS,) + half_shard(s), a.dtype)
                   for s, a in zip(shard_shapes, chip_sums)],
        scratch_shapes=[pltpu.SemaphoreType.DMA((3 * nt,)), pltpu.SemaphoreType.DMA((3 * nt,)),
                        pltpu.SemaphoreType.DMA((nt,))],
    )(*chip_sums)


SMALL = (("sc_conv_w", True), ("sc_conv_b", False), ("lru_b_in", True), ("lru_conv_w", True),
         ("lru_conv_b", True), ("lru_b_gate", True), ("lru_lambda", True), ("ffn_conv_w", True),
         ("ffn_conv_b", False), ("ln_g", True), ("ln_b", True))
PACK_ROW_MULT = 2 * SUB16


def _pack_rows(shapes):
    n = sum(math.prod(shapes[name]) for name, _ in SMALL)
    rows = -(-n // 128)
    return -(-rows // PACK_ROW_MULT) * PACK_ROW_MULT


def _pack_local(vals, shapes):
    flat = jnp.concatenate([vals[name].reshape(-1) for name, _ in SMALL])
    rows = _pack_rows(shapes)
    return jnp.pad(flat, (0, rows * 128 - flat.shape[0])).reshape(rows, 128)


def _unpack_local(pack, shapes):
    flat = pack.reshape(-1)
    out, off = {}, 0
    for name, _ in SMALL:
        n = math.prod(shapes[name])
        out[name] = flat[off:off + n].reshape(shapes[name])
        off += n
    return out


def _pack_slots(fulls, shapes):
    parts = []
    for name, sharded in SMALL:
        v = fulls[name]
        if sharded:
            ns = shapes[name][-1]
            v = jnp.moveaxis(v.reshape(v.shape[:-1] + (N_CHIPS, ns)), -2, 0).reshape(N_CHIPS, -1)
        else:
            v = jnp.broadcast_to(v.reshape(1, -1), (N_CHIPS, v.size))
        parts.append(v)
    flat = jnp.concatenate(parts, axis=1)
    rows = _pack_rows(shapes)
    return jnp.pad(flat, ((0, 0), (0, rows * 128 - flat.shape[1]))).reshape(N_CHIPS, rows, 128)


def _unpack_slots(packs, shapes):
    flat = packs.reshape(N_CHIPS, -1)
    out, off = {}, 0
    for name, sharded in SMALL:
        n = math.prod(shapes[name])
        if sharded:
            seg = flat[:, off:off + n].reshape((N_CHIPS,) + tuple(shapes[name]))
            seg = jnp.moveaxis(seg, 0, -2)
            out[name] = seg.reshape(seg.shape[:-2] + (N_CHIPS * shapes[name][-1],))
        off += n
    return out


WEIGHTS = ("sc_w_in", "sc_conv_w", "sc_conv_b", "sc_w_out", "lru_w_in", "lru_b_in", "lru_conv_w", "lru_conv_b",
           "lru_w_gate", "lru_b_gate", "lru_lambda", "lru_w_out", "ffn_w_up", "ffn_conv_w", "ffn_conv_b",
           "ffn_w_down", "ln_g", "ln_b")
BIG = (("sc_w_in", "col"), ("sc_w_out", "row"), ("lru_w_in", "col"), ("lru_w_gate", "lead"),
       ("lru_w_out", "row"), ("ffn_w_up", "col"), ("ffn_w_down", "row"))


def kernel(x, sc_w_in, sc_conv_w, sc_conv_b, sc_w_out, lru_w_in, lru_b_in, lru_conv_w, lru_conv_b, lru_w_gate, lru_b_gate, lru_lambda, lru_w_out, ffn_w_up, ffn_conv_w, ffn_conv_b, ffn_w_down, ln_g, ln_b, loss_target, m_sc_w_in, m_sc_conv_w, m_sc_conv_b, m_sc_w_out, m_lru_w_in, m_lru_b_in, m_lru_conv_w, m_lru_conv_b, m_lru_w_gate, m_lru_b_gate, m_lru_lambda, m_lru_w_out, m_ffn_w_up, m_ffn_conv_w, m_ffn_conv_b, m_ffn_w_down, m_ln_g, m_ln_b, v_sc_w_in, v_sc_conv_w, v_sc_conv_b, v_sc_w_out, v_lru_w_in, v_lru_b_in, v_lru_conv_w, v_lru_conv_b, v_lru_w_gate, v_lru_b_gate, v_lru_lambda, v_lru_w_out, v_ffn_w_up, v_ffn_conv_w, v_ffn_conv_b, v_ffn_w_down, v_ln_g, v_ln_b):
    w = dict(zip(WEIGHTS, (sc_w_in, sc_conv_w, sc_conv_b, sc_w_out, lru_w_in, lru_b_in, lru_conv_w, lru_conv_b,
                           lru_w_gate, lru_b_gate, lru_lambda, lru_w_out, ffn_w_up, ffn_conv_w, ffn_conv_b,
                           ffn_w_down, ln_g, ln_b)))
    mom = dict(zip(WEIGHTS, (m_sc_w_in, m_sc_conv_w, m_sc_conv_b, m_sc_w_out, m_lru_w_in, m_lru_b_in, m_lru_conv_w,
                             m_lru_conv_b, m_lru_w_gate, m_lru_b_gate, m_lru_lambda, m_lru_w_out, m_ffn_w_up,
                             m_ffn_conv_w, m_ffn_conv_b, m_ffn_w_down, m_ln_g, m_ln_b)))
    vel = dict(zip(WEIGHTS, (v_sc_w_in, v_sc_conv_w, v_sc_conv_b, v_sc_w_out, v_lru_w_in, v_lru_b_in, v_lru_conv_w,
                             v_lru_conv_b, v_lru_w_gate, v_lru_b_gate, v_lru_lambda, v_lru_w_out, v_ffn_w_up,
                             v_ffn_conv_w, v_ffn_conv_b, v_ffn_w_down, v_ln_g, v_ln_b)))
    bd, seq, d = x.shape
    t = bd * seq
    small_shapes = {name: w[name].shape for name, _ in SMALL}
    big_names = [n for n, _ in BIG]
    kinds = [k for _, k in BIG] + ["lead"]
    shard_shapes = [w[n].shape for n in big_names]

    w_pack = _pack_local(w, small_shapes)
    rs_shapes = shard_shapes + [w_pack.shape]
    placed = [_cast_place(w[n], k, BF16, "place_w") for n, k in BIG] + [_cast_place(w_pack, "lead", F32, "place_w")]
    gathered = _all_gather(placed, kinds, rs_shapes)
    full = dict(zip(big_names, gathered[:-1]))
    full.update(_unpack_slots(gathered[-1], small_shapes))
    full["sc_conv_b"] = sc_conv_b
    full["ffn_conv_b"] = ffn_conv_b
    wg_full = jnp.moveaxis(full["lru_w_gate"], 0, -2)
    wg_full = wg_full.reshape(wg_full.shape[:-2] + (2 * LRU_BLOCK,))
    f = full["ffn_w_down"].shape[1]
    rw = full["lru_w_out"].shape[1]

    x0 = x.reshape(t, d)
    xb = x0.astype(BF16)
    cur, cur_b = x0, xb
    saved = []

    for i in range(DEPTH):
        j = i // 2
        s = {"xb": cur_b}
        if i % 2 == 0:
            h = _mm_nn(cur_b, full["sc_w_in"], j, None, 768, "sc_in")
            q = _sc_fwd(h, full["sc_conv_w"][j], full["sc_conv_b"][j][None], seq, "sc_fwd")
            z1, x1, x1b = _mm_nn_ln(q, full["sc_w_out"], j, cur, full["ln_g"][i, 0][None], full["ln_b"][i, 0][None],
                                    "sc_out_ln")
        else:
            h = _mm_nn(cur_b, full["lru_w_in"], j, full["lru_b_in"][j][None], 1280, "lru_in")
            hs, q = _lru_fwd(h, full["lru_conv_w"][j], full["lru_conv_b"][j][None], wg_full[j],
                             full["lru_b_gate"][j], full["lru_lambda"][j][None], seq, "lru_fwd")
            s["hs"] = hs
            z1, x1, x1b = _mm_nn_ln(q, full["lru_w_out"], j, cur, full["ln_g"][i, 0][None],
                                    full["ln_b"][i, 0][None], "lru_out_ln")
        s.update(h=h, q=q, z1=z1, x1b=x1b)
        h3, pre3, act = _ffn_up(x1b, full["ffn_w_up"], i, full["ffn_conv_w"][i], full["ffn_conv_b"][i][None], seq,
                                "ffn_up")
        z2, x2, x2b = _mm_nn_ln(act, full["ffn_w_down"], i, x1, full["ln_g"][i, 1][None], full["ln_b"][i, 1][None],
                                "ffn_down_ln")
        s.update(h3=h3, pre3=pre3, act=act, z2=z2)
        saved.append(s)
        cur, cur_b = x2, x2b

    dcur, loss_parts = _loss_bwd(cur, loss_target.reshape(t, d))
    loss = lax.psum(jnp.sum(loss_parts), MESH_AXES)

    gp = {n: [None] * w[n].shape[0] for n in WEIGHTS}
    for i in reversed(range(DEPTH)):
        j = i // 2
        s = saved[i]
        dz2, dz2b, dg, db = _ln_bwd(dcur, s["z2"], full["ln_g"][i, 1][None], "ln_bwd")
        gp["ln_g"][i] = [None, dg[0]]
        gp["ln_b"][i] = [None, db[0]]
        gp["ffn_w_down"][i] = _mm_tn(s["act"], dz2b[None], f // 2, d, "ffn_down_dw")
        dh3, dcwg, dcwv, dcbg, dcbv = _ffn_down_bwd(dz2b, full["ffn_w_down"], i, s["h3"], s["pre3"],
                                                    full["ffn_conv_w"][i], seq, "ffn_down_bwd")
        gp["ffn_conv_w"][i] = jnp.concatenate([dcwg, dcwv], axis=1)
        gp["ffn_conv_b"][i] = jnp.concatenate([dcbg[0], dcbv[0]])
        dx1 = _mm_nt_res(dh3, full["ffn_w_up"], i, dz2, f // 2, "ffn_up_dx")
        gp["ffn_w_up"][i] = _mm_tn(s["x1b"], dh3, d, f // 2, "ffn_up_dw")
        dz1, dz1b, dg, db = _ln_bwd(dx1, s["z1"], full["ln_g"][i, 0][None], "ln_bwd")
        gp["ln_g"][i][0] = dg[0]
        gp["ln_b"][i][0] = db[0]
        gp["ln_g"][i] = jnp.stack(gp["ln_g"][i])
        gp["ln_b"][i] = jnp.stack(gp["ln_b"][i])
        if i % 2 == 0:
            dq = _mm_nt(dz1b, full["sc_w_out"], j, d, "sc_out_dx")
            gp["sc_w_out"][j] = _mm_tn(s["q"], dz1b[None], d, d, "sc_out_dw")
            dh3, dcw, dcb = _sc_bwd(s["h"], dq, full["sc_conv_w"][j], full["sc_conv_b"][j][None], seq, "sc_bwd")
            gp["sc_conv_w"][j] = dcw
            gp["sc_conv_b"][j] = dcb[0]
            dcur = _mm_nt_res(dh3, full["sc_w_in"], j, dz1, d, "sc_in_dx")
            gp["sc_w_in"][j] = _mm_tn(s["xb"], dh3, d, d, "sc_in_dw")
        else:
            dq = _mm_nt(dz1b, full["lru_w_out"], j, rw, "lru_out_dx")
            gp["lru_w_out"][j] = _mm_tn(s["q"], dz1b[None], rw, d, "lru_out_dw")
            dh3, dbin, dcw, dcb, dwg, dbg, dlam = _lru_bwd(
                s["h"], s["hs"], dq, full["lru_conv_w"][j], full["lru_conv_b"][j][None], wg_full[j],
                full["lru_b_gate"][j], full["lru_lambda"][j][None], seq, "lru_bwd")
            gp["lru_b_in"][j] = dbin[0]
            gp["lru_conv_w"][j] = dcw
            gp["lru_conv_b"][j] = dcb[0]
            gp["lru_w_gate"][j] = dwg
            gp["lru_b_gate"][j] = dbg
            gp["lru_lambda"][j] = dlam[0]
            dcur = _mm_nt_res(dh3, full["lru_w_in"], j, dz1, rw, "lru_in_dx")
            gp["lru_w_in"][j] = _mm_tn(s["xb"], dh3, d, rw, "lru_in_dw")
    grad_x = dcur.reshape(bd, seq, d)
    gp = {n: jnp.stack(v) for n, v in gp.items()}

    ns_gate = w["lru_w_gate"].shape[-1]
    gate = gp["lru_w_gate"]
    gate = jnp.moveaxis(gate.reshape(gate.shape[:-1] + (N_CHIPS, ns_gate)), -2, 0).astype(BF16)
    partials = [gate if n == "lru_w_gate" else gp[n] for n in big_names]
    partials.append(_pack_slots(gp, small_shapes))
    chip_sums = []
    for p, kind in zip(partials, kinds):
        lead = kind == "lead"
        p4 = p.reshape(N_CHIPS if lead else 1, 2, -1, p.shape[-1])
        from_sibling = _d2d_stream(p4, True, "rs_swap")
        half_full = list(p.shape)
        half_full[1 if lead else 0] //= 2
        chip_sums.append(_add_pair(p4, from_sibling, "rs_add_pair").reshape(half_full))
    arrived = _rs_all_to_all(chip_sums, kinds, rs_shapes)

    g_out, d_out, m_out, v_out = {}, {}, {}, {}
    packs = {}
    for n, r in zip(big_names + ["pack"], arrived):
        g_mine = _add_chips(r, "rs_add_chips").reshape(-1, r.shape[-1])
        g_sib = _d2d_stream(g_mine[None, None], False, "rs_share")[0]
        if n == "pack":
            packs = _adamw(w_pack, g_mine, g_sib, _pack_local(mom, small_shapes), _pack_local(vel, small_shapes),
                           "adamw")
        else:
            g_out[n], d_out[n], m_out[n], v_out[n] = _adamw(w[n], g_mine, g_sib, mom[n], vel[n], "adamw")
    for dst, pack in zip((g_out, d_out, m_out, v_out), packs):
        dst.update(_unpack_local(pack, small_shapes))

    return (loss, grad_x, *[g_out[n] for n in WEIGHTS], *[d_out[n] for n in WEIGHTS],
            *[m_out[n] for n in WEIGHTS], *[v_out[n] for n in WEIGHTS])
```

```python
import math

import jax
import jax.numpy as jnp
from jax import lax
from jax.experimental import pallas as pl
from jax.experimental.pallas import tpu as pltpu

F32 = jnp.float32
BF16 = jnp.bfloat16

DEPTH = 4
LRU_HEADS = 10
LRU_BLOCK = 128
LRU_C = 8.0
LN_EPS = 1e-5
ALPHA = (2.0 * DEPTH) ** 0.25
ADAM_LR, ADAM_B1, ADAM_B2, ADAM_EPS, ADAM_WD, ADAM_STEP = 0.001, 0.9, 0.999, 1e-08, 0.01, 10
N_CHIPS = 4
MESH_AXES = ("x", "y", "c")

VMEM_LIMIT_BYTES = 48 * 1024 * 1024
TM_MM = 512
TM_RES = 1024
TT_MM = 512
TM_SC = 256
FFN_COL_BLOCKS = 2
RC = 64
LANE = 128
MXU_COLS = 256
TS_LRU = 128
TM_LN = 512
ELEMS_PER_BLOCK = 256 * 1024
STREAM_ELEMS_PER_BLOCK = 1024 * 1024
SUB = 8
SUB16 = 16


def _cp(*sem):
    return pltpu.CompilerParams(dimension_semantics=sem, vmem_limit_bytes=VMEM_LIMIT_BYTES)


def _sigmoid(v):
    return 1.0 / (1.0 + jnp.exp(-v))


def _softplus(v):
    e = jnp.exp(-jnp.abs(v))
    log1p = jnp.where(e < 1e-3, e * (1.0 - e * (0.5 - e * (1.0 / 3.0))), jnp.log(1.0 + e))
    return jnp.maximum(v, 0.0) + log1p


def _one_minus_exp(v):
    series = -v * (1.0 + v * (0.5 + v * (1.0 / 6.0 + v * (1.0 / 24.0))))
    return jnp.where(v > -0.02, series, 1.0 - jnp.exp(v))


def _gelu_and_grad(v):
    k = math.sqrt(2.0 / math.pi)
    t = jnp.tanh(k * (v + 0.044715 * v * v * v))
    val = 0.5 * v * (1.0 + t)
    grad = 0.5 * (1.0 + t) + 0.5 * v * (1.0 - t * t) * k * (1.0 + 3.0 * 0.044715 * v * v)
    return val, grad


def _down(ext, k, n_head):
    if k:
        ext = pltpu.roll(ext, k, 0)
    return ext[n_head:]


def _up(ext, k, n):
    if k:
        ext = pltpu.roll(ext, ext.shape[0] - k, 0)
    return ext[:n]


def _row(ref, k):
    return ref[k:k + 1, :]


def _colsum(v):
    return jnp.sum(v, axis=0, keepdims=True)


def _mm_nn(a, w3, l, bias, tn, name):
    m, k = a.shape
    n = w3.shape[2]
    tm = min(TM_MM, m)
    has_bias = bias is not None

    def body(*refs):
        if has_bias:
            a_ref, w_ref, b_ref, o_ref = refs
        else:
            a_ref, w_ref, o_ref = refs
        acc = jnp.dot(a_ref[...], w_ref[...], preferred_element_type=F32)
        if has_bias:
            acc = acc + b_ref[...]
        o_ref[...] = acc.astype(o_ref.dtype)

    in_specs = [pl.BlockSpec((tm, k), lambda i, j: (i, 0)),
                pl.BlockSpec((None, k, tn), lambda i, j: (l, 0, j))]
    args = [a, w3]
    if has_bias:
        in_specs.append(pl.BlockSpec((1, tn), lambda i, j: (0, j)))
        args.append(bias)
    return pl.pallas_call(
        body, name=name, grid=(m // tm, n // tn), in_specs=in_specs,
        out_specs=pl.BlockSpec((tm, tn), lambda i, j: (i, j)),
        out_shape=jax.ShapeDtypeStruct((m, n), BF16),
        compiler_params=_cp("parallel", "arbitrary"))(*args)


def _mm_nn_ln(a, w3, l, xres, g, b, name):
    m, k = a.shape
    n = w3.shape[2]
    tm = min(TM_MM, m)

    def body(a_ref, w_ref, x_ref, g_ref, b_ref, z_ref, xn_ref, xb_ref):
        y = jnp.dot(a_ref[...], w_ref[...], preferred_element_type=F32)
        z = ALPHA * x_ref[...] + y
        mu = jnp.mean(z, axis=-1, keepdims=True)
        zc = z - mu
        var = jnp.mean(zc * zc, axis=-1, keepdims=True)
        xn = zc * lax.rsqrt(var + LN_EPS) * g_ref[...] + b_ref[...]
        z_ref[...] = z
        xn_ref[...] = xn
        xb_ref[...] = xn.astype(BF16)

    row = pl.BlockSpec((tm, n), lambda i: (i, 0))
    vec = pl.BlockSpec((1, n), lambda i: (0, 0))
    return pl.pallas_call(
        body, name=name, grid=(m // tm,),
        in_specs=[pl.BlockSpec((tm, k), lambda i: (i, 0)),
                  pl.BlockSpec((None, k, n), lambda i: (l, 0, 0)), row, vec, vec],
        out_specs=[row, row, row],
        out_shape=[jax.ShapeDtypeStruct((m, n), F32), jax.ShapeDtypeStruct((m, n), F32),
                   jax.ShapeDtypeStruct((m, n), BF16)],
        compiler_params=_cp("parallel"))(a, w3, xres, g, b)


def _mm_nt(a, w3, l, tk, name):
    m, n = a.shape
    kd = w3.shape[1]
    tm = min(TM_MM, m)

    def body(a_ref, w_ref, o_ref):
        o_ref[...] = lax.dot_general(a_ref[...], w_ref[...], (((1,), (1,)), ((), ())),
                                     preferred_element_type=F32).astype(o_ref.dtype)

    return pl.pallas_call(
        body, name=name, grid=(m // tm, kd // tk),
        in_specs=[pl.BlockSpec((tm, n), lambda i, j: (i, 0)),
                  pl.BlockSpec((None, tk, n), lambda i, j: (l, j, 0))],
        out_specs=pl.BlockSpec((tm, tk), lambda i, j: (i, j)),
        out_shape=jax.ShapeDtypeStruct((m, kd), BF16),
        compiler_params=_cp("parallel", "arbitrary"))(a, w3)


def _mm_nt_res(dh3, w3, l, dz, tc, name):
    g, m, cg = dh3.shape
    kd = w3.shape[1]
    ncg = cg // tc
    nk = g * ncg
    tm = min(TM_RES, m)

    def body(a_ref, w_ref, dz_ref, o_ref, acc):
        k = pl.program_id(1)

        @pl.when(k == 0)
        def _():
            acc[...] = ALPHA * dz_ref[...]

        acc[...] += lax.dot_general(a_ref[...], w_ref[...], (((1,), (1,)), ((), ())),
                                    preferred_element_type=F32)

        @pl.when(k == nk - 1)
        def _():
            o_ref[...] = acc[...]

    return pl.pallas_call(
        body, name=name, grid=(m // tm, nk),
        in_specs=[pl.BlockSpec((None, tm, tc), lambda i, k: (k // ncg, i, k % ncg)),
                  pl.BlockSpec((None, kd, tc), lambda i, k: (l, 0, k)),
                  pl.BlockSpec((tm, kd), lambda i, k: (i, 0))],
        out_specs=pl.BlockSpec((tm, kd), lambda i, k: (i, 0)),
        out_shape=jax.ShapeDtypeStruct((m, kd), F32),
        scratch_shapes=[pltpu.VMEM((tm, kd), F32)],
        compiler_params=_cp("parallel", "arbitrary"))(dh3, w3, dz)


def _mm_tn(a, b3, tka, tnb, name):
    t, ka = a.shape
    g, _, cg = b3.shape
    ncg = cg // tnb
    tt = min(TT_MM, t)
    nt = t // tt

    def body(a_ref, b_ref, o_ref, acc):
        s = pl.program_id(2)

        @pl.when(s == 0)
        def _():
            acc[...] = jnp.zeros_like(acc)

        acc[...] += lax.dot_general(a_ref[...], b_ref[...], (((0,), (0,)), ((), ())),
                                    preferred_element_type=F32)

        @pl.when(s == nt - 1)
        def _():
            o_ref[...] = acc[...].astype(o_ref.dtype)

    return pl.pallas_call(
        body, name=name, grid=(ka // tka, g * ncg, nt),
        in_specs=[pl.BlockSpec((tt, tka), lambda i, j, s: (s, i)),
                  pl.BlockSpec((None, tt, tnb), lambda i, j, s: (j // ncg, s, j % ncg))],
        out_specs=pl.BlockSpec((tka, tnb), lambda i, j, s: (i, j)),
        out_shape=jax.ShapeDtypeStruct((ka, g * cg), BF16),
        scratch_shapes=[pltpu.VMEM((tka, tnb), F32)],
        compiler_params=_cp("parallel", "parallel", "arbitrary"))(a, b3)


def _ln_bwd(dxn, z, g, name):
    m, d = z.shape
    tm = min(TM_LN, m)

    def body(dx_ref, z_ref, g_ref, dz_ref, dzb_ref, dg_ref, db_ref):
        @pl.when(pl.program_id(0) == 0)
        def _():
            dg_ref[...] = jnp.zeros_like(dg_ref)
            db_ref[...] = jnp.zeros_like(db_ref)

        zz = z_ref[...]
        dx = dx_ref[...]
        mu = jnp.mean(zz, axis=-1, keepdims=True)
        zc = zz - mu
        var = jnp.mean(zc * zc, axis=-1, keepdims=True)
        rstd = lax.rsqrt(var + LN_EPS)
        xh = zc * rstd
        dg_ref[...] += _colsum(dx * xh)
        db_ref[...] += _colsum(dx)
        dxh = dx * g_ref[...]
        m1 = jnp.mean(dxh, axis=-1, keepdims=True)
        m2 = jnp.mean(dxh * xh, axis=-1, keepdims=True)
        dz = rstd * (dxh - m1 - xh * m2)
        dz_ref[...] = dz
        dzb_ref[...] = dz.astype(BF16)

    row = pl.BlockSpec((tm, d), lambda i: (i, 0))
    vec = pl.BlockSpec((1, d), lambda i: (0, 0))
    return pl.pallas_call(
        body, name=name, grid=(m // tm,), in_specs=[row, row, vec],
        out_specs=[row, row, vec, vec],
        out_shape=[jax.ShapeDtypeStruct((m, d), F32), jax.ShapeDtypeStruct((m, d), BF16),
                   jax.ShapeDtypeStruct((1, d), F32), jax.ShapeDtypeStruct((1, d), F32)],
        compiler_params=_cp("arbitrary"))(dxn, z, g)


def _loss_bwd(y, target):
    m, d = y.shape
    tm = min(TM_LN, m)

    def body(y_ref, t_ref, dy_ref, ls_ref):
        @pl.when(pl.program_id(0) == 0)
        def _():
            ls_ref[...] = jnp.zeros_like(ls_ref)

        e = y_ref[...] - t_ref[...]
        dy_ref[...] = e * (1.0 / d)
        ls_ref[...] += _colsum(e * e) * (0.5 / d)

    row = pl.BlockSpec((tm, d), lambda i: (i, 0))
    return pl.pallas_call(
        body, name="loss_bwd", grid=(m // tm,), in_specs=[row, row],
        out_specs=[row, pl.BlockSpec((1, d), lambda i: (0, 0))],
        out_shape=[jax.ShapeDtypeStruct((m, d), F32), jax.ShapeDtypeStruct((1, d), F32)],
        compiler_params=_cp("arbitrary"))(y, target)


def _sc_fwd(h, cw, cb, seq, name):
    t, d3 = h.shape
    d = d3 // 3
    tm = min(TM_SC, seq)

    def body(hb_ref, hc_ref, hv_ref, cw_ref, cb_ref, q_ref, carry):
        i = pl.program_id(0)

        @pl.when(lax.rem(i * tm, seq) == 0)
        def _():
            carry[...] = jnp.zeros_like(carry)

        p = hc_ref[...].astype(F32) * hv_ref[...].astype(F32)
        ext = jnp.concatenate([carry[...], p], axis=0)
        u = cb_ref[...] + _row(cw_ref, 0) * _down(ext, 2, SUB) + _row(cw_ref, 1) * _down(ext, 1, SUB) \
            + _row(cw_ref, 2) * p
        q_ref[...] = (hb_ref[...].astype(F32) * u).astype(BF16)
        carry[...] = p[tm - SUB:, :]

    blk = lambda c: pl.BlockSpec((tm, d), lambda i: (i, c))
    return pl.pallas_call(
        body, name=name, grid=(t // tm,),
        in_specs=[blk(0), blk(1), blk(2), pl.BlockSpec((3, d), lambda i: (0, 0)),
                  pl.BlockSpec((1, d), lambda i: (0, 0))],
        out_specs=pl.BlockSpec((tm, d), lambda i: (i, 0)),
        out_shape=jax.ShapeDtypeStruct((t, d), BF16),
        scratch_shapes=[pltpu.VMEM((SUB, d), F32)],
        compiler_params=_cp("arbitrary"))(h, h, h, cw, cb)


def _sc_bwd(h, dq, cw, cb, seq, name):
    t, d3 = h.shape
    d = d3 // 3
    tm = min(TM_SC, seq)
    nt = t // tm
    hpt = tm // SUB16

    def body(hb_ref, hc_ref, hv_ref, hch_ref, hvh_ref, dq_ref, cw_ref, cb_ref,
             dh_ref, dcw_ref, dcb_ref, carry):
        i = pl.program_id(0)
        ri = nt - 1 - i

        @pl.when(i == 0)
        def _():
            dcw_ref[...] = jnp.zeros_like(dcw_ref)
            dcb_ref[...] = jnp.zeros_like(dcb_ref)

        @pl.when(lax.rem((ri + 1) * tm, seq) == 0)
        def _():
            carry[...] = jnp.zeros_like(carry)

        keep = jnp.where(lax.rem(ri * tm, seq) == 0, 0.0, 1.0)
        gb = hb_ref[...].astype(F32)
        gc = hc_ref[...].astype(F32)
        v = hv_ref[...].astype(F32)
        p = gc * v
        p_head = hch_ref[...].astype(F32) * hvh_ref[...].astype(F32) * keep
        ext = jnp.concatenate([p_head, p], axis=0)
        pm2 = _down(ext, 2, SUB16)
        pm1 = _down(ext, 1, SUB16)
        u = cb_ref[...] + _row(cw_ref, 0) * pm2 + _row(cw_ref, 1) * pm1 + _row(cw_ref, 2) * p
        dqf = dq_ref[...].astype(F32)
        du = dqf * gb
        dcb_ref[...] += _colsum(du)
        dcw_ref[0:1, :] += _colsum(du * pm2)
        dcw_ref[1:2, :] += _colsum(du * pm1)
        dcw_ref[2:3, :] += _colsum(du * p)
        ext2 = jnp.concatenate([du, carry[...]], axis=0)
        dp = _row(cw_ref, 2) * du + _row(cw_ref, 1) * _up(ext2, 1, tm) + _row(cw_ref, 0) * _up(ext2, 2, tm)
        carry[...] = du[0:SUB, :]
        dh_ref[0] = (dqf * u).astype(BF16)
        dh_ref[1] = (dp * v).astype(BF16)
        dh_ref[2] = (dp * gc).astype(BF16)

    blk = lambda c: pl.BlockSpec((tm, d), lambda i: (nt - 1 - i, c))
    head = lambda c: pl.BlockSpec((SUB16, d), lambda i: (jnp.maximum((nt - 1 - i) * hpt - 1, 0), c))
    vec = lambda r: pl.BlockSpec((r, d), lambda i: (0, 0))
    return pl.pallas_call(
        body, name=name, grid=(nt,),
        in_specs=[blk(0), blk(1), blk(2), head(1), head(2),
                  pl.BlockSpec((tm, d), lambda i: (nt - 1 - i, 0)), vec(3), vec(1)],
        out_specs=[pl.BlockSpec((3, tm, d), lambda i: (0, nt - 1 - i, 0)), vec(3), vec(1)],
        out_shape=[jax.ShapeDtypeStruct((3, t, d), BF16), jax.ShapeDtypeStruct((3, d), F32),
                   jax.ShapeDtypeStruct((1, d), F32)],
        scratch_shapes=[pltpu.VMEM((SUB, d), F32)],
        compiler_params=_cp("arbitrary"))(h, h, h, h, h, dq, cw, cb)


def _fold(v):
    return jnp.sum(v.reshape(v.shape[0] // SUB, SUB, v.shape[1]), axis=0)


def _ffn_up(xb, w3, l, cw, cb, seq, name):
    t, d = xb.shape
    f = w3.shape[2] // 2
    tc = f // FFN_COL_BLOCKS
    tm = min(TM_MM, seq)

    def body(x_ref, wg_ref, wv_ref, cwg_ref, cwv_ref, cbg_ref, cbv_ref, h_ref, pre_ref, act_ref, eg, ev):
        i = pl.program_id(1)

        @pl.when(lax.rem(i * tm, seq) == 0)
        def _():
            eg[0:SUB, :] = jnp.zeros((SUB, tc), F32)
            ev[0:SUB, :] = jnp.zeros((SUB, tc), F32)

        xx = x_ref[...]

        def matmul(lo, hi):
            eg[SUB:, lo:hi] = jnp.dot(xx, wg_ref[:, lo:hi], preferred_element_type=F32)
            ev[SUB:, lo:hi] = jnp.dot(xx, wv_ref[:, lo:hi], preferred_element_type=F32)

        def epilogue(lo, hi):
            for c0 in range(lo, hi, LANE):
                cols = slice(c0, c0 + LANE)
                taps = [[ref[k:k + 1, cols] for k in range(3)] + [bref[:, cols]]
                        for ref, bref in ((cwg_ref, cbg_ref), (cwv_ref, cbv_ref))]
                for r0 in range(0, tm, RC):
                    rows = slice(r0, r0 + RC)
                    pres = []
                    for half, e_ref in enumerate((eg, ev)):
                        w0, w1, w2, bias = taps[half]
                        e = e_ref[r0:r0 + RC + SUB, cols]
                        cur = e[SUB:]
                        pre = bias + w0 * _down(e, 2, SUB) + w1 * _down(e, 1, SUB) + w2 * cur
                        h_ref[half, rows, cols] = cur.astype(BF16)
                        pre_ref[half, rows, cols] = pre.astype(BF16)
                        pres.append(pre)
                    act_ref[rows, cols] = (pres[0] * _sigmoid(pres[0]) * pres[1]).astype(BF16)

        blocks = [(lo, min(lo + MXU_COLS, tc)) for lo in range(0, tc, MXU_COLS)]
        matmul(*blocks[0])
        for b, blk in enumerate(blocks):
            if b + 1 < len(blocks):
                matmul(*blocks[b + 1])
            epilogue(*blk)
        eg[0:SUB, :] = eg[tm:tm + SUB, :]
        ev[0:SUB, :] = ev[tm:tm + SUB, :]

    nc = FFN_COL_BLOCKS
    wspec = lambda off: pl.BlockSpec((None, d, tc), lambda j, i: (l, 0, j + off))
    vec = lambda r, off: pl.BlockSpec((r, tc), lambda j, i: (0, j + off))
    pair = pl.BlockSpec((2, tm, tc), lambda j, i: (0, i, j))
    return pl.pallas_call(
        body, name=name, grid=(nc, t // tm),
        in_specs=[pl.BlockSpec((tm, d), lambda j, i: (i, 0)), wspec(0), wspec(nc),
                  vec(3, 0), vec(3, nc), vec(1, 0), vec(1, nc)],
        out_specs=[pair, pair, pl.BlockSpec((tm, tc), lambda j, i: (i, j))],
        out_shape=[jax.ShapeDtypeStruct((2, t, f), BF16), jax.ShapeDtypeStruct((2, t, f), BF16),
                   jax.ShapeDtypeStruct((t, f), BF16)],
        scratch_shapes=[pltpu.VMEM((tm + SUB, tc), F32), pltpu.VMEM((tm + SUB, tc), F32)],
        compiler_params=_cp("arbitrary", "arbitrary"))(xb, w3, w3, cw, cw, cb, cb)


def _ffn_down_bwd(dzb, wd3, l, h3, pre3, cw, seq, name):
    t, d = dzb.shape
    f = wd3.shape[1]
    tc = f // FFN_COL_BLOCKS
    tm = min(TM_MM, seq)
    nt = t // tm

    def body(dz_ref, wd_ref, h_ref, pre_ref, cwg_ref, cwv_ref,
             dh_ref, dcwg_ref, dcwv_ref, dcbg_ref, dcbv_ref, da_s, carry):
        i = pl.program_id(1)
        ri = nt - 1 - i

        @pl.when(i == 0)
        def _():
            for r in (dcwg_ref, dcwv_ref, dcbg_ref, dcbv_ref):
                r[...] = jnp.zeros_like(r)

        @pl.when(lax.rem((ri + 1) * tm, seq) == 0)
        def _():
            carry[...] = jnp.zeros_like(carry)

        dz = dz_ref[...]

        def matmul(lo, hi):
            da_s[:, lo:hi] = lax.dot_general(dz, wd_ref[lo:hi, :], (((1,), (1,)), ((), ())),
                                             preferred_element_type=F32)

        def epilogue(lo, hi):
            for c0 in range(lo, hi, LANE):
                cols = slice(c0, c0 + LANE)
                taps = [[ref[k:k + 1, cols] for k in range(3)] for ref in (cwg_ref, cwv_ref)]
                acc = [jnp.zeros((SUB, LANE), F32)] * 8
                for r0 in range(tm - RC, -1, -RC):
                    rows = slice(r0, r0 + RC)
                    da = da_s[rows, cols]
                    gp = pre_ref[0, rows, cols].astype(F32)
                    vp = pre_ref[1, rows, cols].astype(F32)
                    sg = _sigmoid(gp)
                    dpres = (da * vp * (sg * (1.0 + gp * (1.0 - sg))), da * (gp * sg))
                    for half in range(2):
                        w0, w1, w2 = taps[half]
                        dpre = dpres[half]
                        ext = jnp.concatenate([dpre, carry[half, :, cols]], axis=0)
                        u1 = _up(ext, 1, RC)
                        u2 = _up(ext, 2, RC)
                        carry[half, :, cols] = dpre[0:SUB]
                        dh_ref[half, rows, cols] = (w2 * dpre + w1 * u1 + w0 * u2).astype(BF16)
                        hh = h_ref[half, rows, cols].astype(F32)
                        for k, term in enumerate((hh * u2, hh * u1, hh * dpre, dpre)):
                            acc[4 * half + k] = acc[4 * half + k] + _fold(term)
                for half, (dcw_ref, dcb_ref) in enumerate(((dcwg_ref, dcbg_ref), (dcwv_ref, dcbv_ref))):
                    for k in range(3):
                        dcw_ref[k:k + 1, cols] += _colsum(acc[4 * half + k])
                    dcb_ref[:, cols] += _colsum(acc[4 * half + 3])

        blocks = [(lo, min(lo + MXU_COLS, tc)) for lo in range(0, tc, MXU_COLS)]
        matmul(*blocks[0])
        for b, blk in enumerate(blocks):
            if b + 1 < len(blocks):
                matmul(*blocks[b + 1])
            epilogue(*blk)

    nc = FFN_COL_BLOCKS
    pair = pl.BlockSpec((2, tm, tc), lambda j, i: (0, nt - 1 - i, j))
    vec = lambda off: pl.BlockSpec((3, tc), lambda j, i: (0, j + off))
    acc_spec = lambda r: pl.BlockSpec((r, tc), lambda j, i: (0, j))
    return pl.pallas_call(
        body, name=name, grid=(nc, nt),
        in_specs=[pl.BlockSpec((tm, d), lambda j, i: (nt - 1 - i, 0)),
                  pl.BlockSpec((None, tc, d), lambda j, i: (l, j, 0)), pair, pair, vec(0), vec(nc)],
        out_specs=[pair, acc_spec(3), acc_spec(3), acc_spec(1), acc_spec(1)],
        out_shape=[jax.ShapeDtypeStruct((2, t, f), BF16), jax.ShapeDtypeStruct((3, f), F32),
                   jax.ShapeDtypeStruct((3, f), F32), jax.ShapeDtypeStruct((1, f), F32),
                   jax.ShapeDtypeStruct((1, f), F32)],
        scratch_shapes=[pltpu.VMEM((tm, tc), F32), pltpu.VMEM((2, SUB, tc), F32)],
        compiler_params=_cp("arbitrary", "arbitrary"))(dzb, wd3, h3, pre3, cw, cw)


def _lru_gates(xr, wg_ref, bg_ref):
    rs, gs = [], []
    for hd in range(LRU_HEADS):
        xh = xr[:, hd * LRU_BLOCK:(hd + 1) * LRU_BLOCK].astype(BF16)
        gt = jnp.dot(xh, wg_ref[hd], preferred_element_type=F32) + _row(bg_ref, hd)
        rs.append(gt[:, :LRU_BLOCK])
        gs.append(gt[:, LRU_BLOCK:])
    return jnp.concatenate(rs, axis=1), jnp.concatenate(gs, axis=1)


def _lru_coeffs(xr, wg_ref, bg_ref, lam_ref):
    gr, gi = _lru_gates(xr, wg_ref, bg_ref)
    r = _sigmoid(gr)
    ig = _sigmoid(gi)
    sp = _softplus(-lam_ref[...])
    log_a = -LRU_C * r * sp
    a = jnp.exp(log_a)
    mult = jnp.sqrt(_one_minus_exp(2.0 * log_a))
    return r, ig, sp, a, mult


def _lru_fwd(h, cw, cb, wg, bg, lam, seq, name):
    t, r2 = h.shape
    rw = r2 // 2
    ts = min(TS_LRU, seq)
    n8 = ts // SUB

    def body(hg_ref, hr_ref, cw_ref, cb_ref, wg_ref, bg_ref, lam_ref, hs_ref, y_ref,
             a_s, b_s, cconv, cstate):
        i = pl.program_id(0)

        @pl.when(lax.rem(i * ts, seq) == 0)
        def _():
            cconv[...] = jnp.zeros_like(cconv)
            cstate[...] = jnp.zeros_like(cstate)

        rin = hr_ref[...].astype(F32)
        ext = jnp.concatenate([cconv[...], rin], axis=0)
        xr = cb_ref[...]
        for k in range(4):
            xr = xr + _row(cw_ref, k) * _down(ext, 3 - k, SUB)
        cconv[...] = rin[ts - SUB:, :]
        _, ig, _, a, mult = _lru_coeffs(xr, wg_ref, bg_ref, lam_ref)
        a_s[...] = a
        b_s[...] = mult * (ig * xr)
        row = lax.broadcasted_iota(jnp.int32, (SUB, rw), 0)

        def step(j, carry):
            off = pl.multiple_of(j * SUB, SUB)
            a8 = a_s[pl.ds(off, SUB), :]
            b8 = b_s[pl.ds(off, SUB), :]
            for d in (1, 2, 4):
                m = row >= d
                b8 = jnp.where(m, a8 * pltpu.roll(b8, d, 0) + b8, b8)
                a8 = jnp.where(m, a8 * pltpu.roll(a8, d, 0), a8)
            h8 = a8 * carry + b8
            hs_ref[pl.ds(off, SUB), :] = h8
            return _colsum(jnp.where(row == SUB - 1, h8, 0.0))

        cstate[...] = lax.fori_loop(0, n8, step, cstate[...])
        gel, _ = _gelu_and_grad(hg_ref[...].astype(F32))
        y_ref[...] = (hs_ref[...] * gel).astype(BF16)

    full = lambda shp: pl.BlockSpec(shp, lambda i: (0,) * len(shp))
    return pl.pallas_call(
        body, name=name, grid=(t // ts,),
        in_specs=[pl.BlockSpec((ts, rw), lambda i: (i, 0)), pl.BlockSpec((ts, rw), lambda i: (i, 1)),
                  full((4, rw)), full((1, rw)), full(wg.shape), full(bg.shape), full((1, rw))],
        out_specs=[pl.BlockSpec((ts, rw), lambda i: (i, 0)), pl.BlockSpec((ts, rw), lambda i: (i, 0))],
        out_shape=[jax.ShapeDtypeStruct((t, rw), F32), jax.ShapeDtypeStruct((t, rw), BF16)],
        scratch_shapes=[pltpu.VMEM((ts, rw), F32), pltpu.VMEM((ts, rw), F32),
                        pltpu.VMEM((SUB, rw), F32), pltpu.VMEM((1, rw), F32)],
        compiler_params=_cp("arbitrary"))(h, h, cw, cb, wg, bg, lam)


def _lru_bwd(h, hs, dy, cw, cb, wg, bg, lam, seq, name):
    t, r2 = h.shape
    rw = r2 // 2
    ts = min(TS_LRU, seq)
    nt = t // ts
    n8 = ts // SUB
    hp16 = ts // SUB16
    hp8 = ts // SUB

    def body(hg_ref, hr_ref, hrh_ref, hs_ref, hsh_ref, dy_ref, cw_ref, cb_ref, wg_ref, bg_ref, lam_ref,
             dh_ref, dbin_ref, dcw_ref, dcb_ref, dwg_ref, dbg_ref, dlam_ref,
             a_s, g_s, l_s, c_lam, c_a, c_dxr):
        i = pl.program_id(0)
        ri = nt - 1 - i

        @pl.when(i == 0)
        def _():
            for r in (dbin_ref, dcw_ref, dcb_ref, dwg_ref, dbg_ref, dlam_ref):
                r[...] = jnp.zeros_like(r)

        @pl.when(lax.rem((ri + 1) * ts, seq) == 0)
        def _():
            c_lam[...] = jnp.zeros_like(c_lam)
            c_a[...] = jnp.zeros_like(c_a)
            c_dxr[...] = jnp.zeros_like(c_dxr)

        keep = jnp.where(lax.rem(ri * ts, seq) == 0, 0.0, 1.0)
        rin = hr_ref[...].astype(F32)
        ext = jnp.concatenate([hrh_ref[...].astype(F32) * keep, rin], axis=0)
        shifted = [_down(ext, 3 - k, SUB16) for k in range(4)]
        xr = cb_ref[...]
        for k in range(4):
            xr = xr + _row(cw_ref, k) * shifted[k]
        r, ig, sp, a, mult = _lru_coeffs(xr, wg_ref, bg_ref, lam_ref)
        gel, dgel = _gelu_and_grad(hg_ref[...].astype(F32))
        dyf = dy_ref[...].astype(F32)
        hsv = hs_ref[...]
        dg = dyf * hsv * dgel

        a_s[...] = _up(jnp.concatenate([a, c_a[...]], axis=0), 1, ts)
        g_s[...] = dyf * gel
        c_a[...] = a[0:SUB, :]
        row = lax.broadcasted_iota(jnp.int32, (SUB, rw), 0)

        def step(j, carry):
            off = pl.multiple_of((n8 - 1 - j) * SUB, SUB)
            a8 = a_s[pl.ds(off, SUB), :]
            b8 = g_s[pl.ds(off, SUB), :]
            for d in (1, 2, 4):
                m = row < SUB - d
                b8 = jnp.where(m, a8 * pltpu.roll(b8, SUB - d, 0) + b8, b8)
                a8 = jnp.where(m, a8 * pltpu.roll(a8, SUB - d, 0), a8)
            l8 = a8 * carry + b8
            l_s[pl.ds(off, SUB), :] = l8
            return _colsum(jnp.where(row == 0, l8, 0.0))

        c_lam[...] = lax.fori_loop(0, n8, step, c_lam[...])
        lamv = l_s[...]
        hs_prev = _down(jnp.concatenate([hsh_ref[...] * keep, hsv], axis=0), 1, SUB)
        da = lamv * hs_prev
        t1 = lamv * xr
        dmult = t1 * ig
        dig = t1 * mult
        dxr = lamv * mult * ig
        dla = da * a - dmult * (a * a) / mult
        dr = dla * (-LRU_C * sp)
        dlam_ref[...] += _colsum(dla * (-LRU_C) * r) * (-_sigmoid(-lam_ref[...]))
        dgr = dr * r * (1.0 - r)
        dgi = dig * ig * (1.0 - ig)
        parts = []
        for hd in range(LRU_HEADS):
            sl = slice(hd * LRU_BLOCK, (hd + 1) * LRU_BLOCK)
            dgt = jnp.concatenate([dgr[:, sl], dgi[:, sl]], axis=1)
            dbg_ref[hd:hd + 1, :] += _colsum(dgt)
            dgt16 = dgt.astype(BF16)
            parts.append(lax.dot_general(dgt16, wg_ref[hd], (((1,), (1,)), ((), ())),
                                         preferred_element_type=F32))
            dwg_ref[hd] += lax.dot_general(xr[:, sl].astype(BF16), dgt16, (((0,), (0,)), ((), ())),
                                           preferred_element_type=F32)
        dxr = dxr + jnp.concatenate(parts, axis=1)

        dcb_ref[...] += _colsum(dxr)
        for k in range(4):
            dcw_ref[k:k + 1, :] += _colsum(dxr * shifted[k])
        ext2 = jnp.concatenate([dxr, c_dxr[...]], axis=0)
        drb = _row(cw_ref, 3) * dxr
        for k in range(3):
            drb = drb + _row(cw_ref, k) * _up(ext2, 3 - k, ts)
        c_dxr[...] = dxr[0:SUB, :]
        dh_ref[0] = dg.astype(BF16)
        dh_ref[1] = drb.astype(BF16)
        dbin_ref[:, 0:rw] += _colsum(dg)
        dbin_ref[:, rw:] += _colsum(drb)

    rev = lambda c: pl.BlockSpec((ts, rw), lambda i: (nt - 1 - i, c))
    full = lambda shp: pl.BlockSpec(shp, lambda i: (0,) * len(shp))
    nh = LRU_HEADS
    return pl.pallas_call(
        body, name=name, grid=(nt,),
        in_specs=[rev(0), rev(1),
                  pl.BlockSpec((SUB16, rw), lambda i: (jnp.maximum((nt - 1 - i) * hp16 - 1, 0), 1)),
                  rev(0),
                  pl.BlockSpec((SUB, rw), lambda i: (jnp.maximum((nt - 1 - i) * hp8 - 1, 0), 0)),
                  rev(0), full((4, rw)), full((1, rw)), full(wg.shape), full(bg.shape), full((1, rw))],
        out_specs=[pl.BlockSpec((2, ts, rw), lambda i: (0, nt - 1 - i, 0)), full((1, r2)), full((4, rw)),
                   full((1, rw)), full((nh, LRU_BLOCK, 2 * LRU_BLOCK)), full((nh, 2 * LRU_BLOCK)), full((1, rw))],
        out_shape=[jax.ShapeDtypeStruct((2, t, rw), BF16), jax.ShapeDtypeStruct((1, r2), F32),
                   jax.ShapeDtypeStruct((4, rw), F32), jax.ShapeDtypeStruct((1, rw), F32),
                   jax.ShapeDtypeStruct((nh, LRU_BLOCK, 2 * LRU_BLOCK), F32),
                   jax.ShapeDtypeStruct((nh, 2 * LRU_BLOCK), F32), jax.ShapeDtypeStruct((1, rw), F32)],
        scratch_shapes=[pltpu.VMEM((ts, rw), F32), pltpu.VMEM((ts, rw), F32), pltpu.VMEM((ts, rw), F32),
                        pltpu.VMEM((1, rw), F32), pltpu.VMEM((SUB, rw), F32), pltpu.VMEM((SUB, rw), F32)],
        compiler_params=_cp("arbitrary"))(h, h, h, hs, hs, dy, cw, cb, wg, bg, lam)


def _row_tile(rows, cols, mult, elems=ELEMS_PER_BLOCK):
    cap = max(mult, elems // cols)
    best = None
    for cand in range(mult, min(rows, cap) + 1, mult):
        if rows % cand == 0:
            best = cand
    return best if best is not None else rows


def _core_index():
    return lax.axis_index("c").astype(jnp.int32).reshape(1)


def _chip_index():
    return (2 * lax.axis_index("x") + lax.axis_index("y")).astype(jnp.int32).reshape(1)


def _add_pair(p4, r3, name):
    s, _, rows, cols = p4.shape
    tr = _row_tile(rows, cols, SUB16)

    def body(c_ref, a_ref, b_ref, o_ref):
        o_ref[...] = (a_ref[...].astype(F32) + b_ref[...].astype(F32)).astype(o_ref.dtype)

    blk = pl.BlockSpec((None, tr, cols), lambda k, i, c_ref: (k, i, 0))
    return pl.pallas_call(
        body, name=name,
        grid_spec=pltpu.PrefetchScalarGridSpec(
            num_scalar_prefetch=1, grid=(s, rows // tr),
            in_specs=[pl.BlockSpec((None, None, tr, cols), lambda k, i, c_ref: (k, c_ref[0], i, 0)), blk],
            out_specs=blk),
        out_shape=jax.ShapeDtypeStruct((s, rows, cols), p4.dtype),
        compiler_params=_cp("parallel", "parallel"))(_core_index(), p4, r3)


def _add_chips(r, name):
    shape = r.shape[1:]
    r3 = r.reshape(N_CHIPS, -1, shape[-1])
    _, rows, cols = r3.shape
    tr = _row_tile(rows, cols, SUB16)

    def body(r_ref, o_ref):
        s = r_ref[0].astype(F32) + r_ref[1].astype(F32)
        s = s + r_ref[2].astype(F32)
        o_ref[...] = s + r_ref[3].astype(F32)

    out = pl.pallas_call(body, name=name, grid=(rows // tr,),
                         in_specs=[pl.BlockSpec((N_CHIPS, tr, cols), lambda i: (0, i, 0))],
                         out_specs=pl.BlockSpec((tr, cols), lambda i: (i, 0)),
                         out_shape=jax.ShapeDtypeStruct((rows, cols), F32),
                         compiler_params=_cp("parallel"))(r3)
    return out.reshape(shape)


def _adamw(w, g_mine, g_sib, m, v, name):
    shape = w.shape
    rows, cols = g_mine.shape
    flat = [arr.reshape(2, rows, cols) for arr in (w, m, v)]
    tr = _row_tile(rows, cols, SUB)

    def body(c_ref, w_ref, gm_ref, gs_ref, m_ref, v_ref, g_ref, d_ref, mo_ref, vo_ref):
        gg = jnp.where(pl.program_id(0) == c_ref[0], gm_ref[...], gs_ref[...])
        m2 = ADAM_B1 * m_ref[...] + (1.0 - ADAM_B1) * gg
        v2 = ADAM_B2 * v_ref[...] + (1.0 - ADAM_B2) * (gg * gg)
        m_hat = m2 / (1.0 - ADAM_B1 ** ADAM_STEP)
        v_hat = v2 / (1.0 - ADAM_B2 ** ADAM_STEP)
        g_ref[...] = gg
        d_ref[...] = -ADAM_LR * (m_hat / (jnp.sqrt(v_hat) + ADAM_EPS) + ADAM_WD * w_ref[...])
        mo_ref[...] = m2
        vo_ref[...] = v2

    blk = pl.BlockSpec((None, tr, cols), lambda hh, i, c_ref: (hh, i, 0))
    gblk = pl.BlockSpec((tr, cols), lambda hh, i, c_ref: (i, 0))
    outs = pl.pallas_call(
        body, name=name,
        grid_spec=pltpu.PrefetchScalarGridSpec(
            num_scalar_prefetch=1, grid=(2, rows // tr),
            in_specs=[blk, gblk, gblk, blk, blk], out_specs=[blk] * 4),
        out_shape=[jax.ShapeDtypeStruct((2, rows, cols), F32)] * 4,
        compiler_params=_cp("parallel", "parallel"))(_core_index(), flat[0], g_mine, g_sib, flat[1], flat[2])
    return tuple(o.reshape(shape) for o in outs)


def _cast_place(shard, kind, dtype, name):
    my = _chip_index()
    s = shard.shape
    if kind == "col":
        rows, ns = math.prod(s[:-1]), s[-1]
        tr = _row_tile(rows, ns, SUB16)
        src = shard.reshape(rows, ns)
        grid = (rows // tr,)
        in_spec = pl.BlockSpec((tr, ns), lambda i, my_ref: (i, 0))
        out_spec = pl.BlockSpec((tr, ns), lambda i, my_ref: (i, my_ref[0]))
        out_shape = (rows, N_CHIPS * ns)
    else:
        a, b, c = (s[0], s[1], math.prod(s[2:])) if kind == "row" else (1, math.prod(s[:-1]), s[-1])
        tr = _row_tile(b, c, SUB16)
        src = shard.reshape(a, b, c)
        grid = (a, b // tr)
        in_spec = pl.BlockSpec((None, tr, c), lambda l, i, my_ref: (l, i, 0))
        out_spec = pl.BlockSpec((None, None, tr, c), lambda l, i, my_ref: (l, my_ref[0], i, 0))
        out_shape = (a, N_CHIPS, b, c)

    def body(my_ref, i_ref, o_ref):
        o_ref[...] = i_ref[...].astype(o_ref.dtype)

    out = pl.pallas_call(
        body, name=name,
        grid_spec=pltpu.PrefetchScalarGridSpec(num_scalar_prefetch=1, grid=grid, in_specs=[in_spec],
                                               out_specs=out_spec),
        out_shape=jax.ShapeDtypeStruct(out_shape, dtype),
        compiler_params=_cp(*["parallel"] * len(grid)))(my, src)
    return out.reshape(_full_shape(kind, s))


def _full_shape(kind, shard_shape):
    s = tuple(shard_shape)
    if kind == "col":
        return s[:-1] + (N_CHIPS * s[-1],)
    if kind == "row":
        return (s[0], N_CHIPS * s[1]) + s[2:]
    return (N_CHIPS,) + s


def _slot(kind, ref, k, shard_shape):
    if kind == "col":
        n = shard_shape[-1]
        return ref.at[:, :, pl.ds(pl.multiple_of(k * n, 128), n)]
    if kind == "row":
        n = shard_shape[1]
        return ref.at[:, pl.ds(pl.multiple_of(k * n, SUB16), n), :]
    return ref.at[k]


def _half(ref, c, h):
    return ref.at[pl.ds(c * h, h)]


def _position():
    x = lax.axis_index("x")
    y = lax.axis_index("y")
    c = lax.axis_index("c")
    return x, y, c


def _peer_chip(x, y, j):
    tx = 1 - x if j & 2 else x
    ty = 1 - y if j & 1 else y
    return tx, ty


def _remote(src, dst, ssem, rsem, dev):
    return pltpu.make_async_remote_copy(src_ref=src, dst_ref=dst, send_sem=ssem, recv_sem=rsem,
                                        device_id=dev, device_id_type=pl.DeviceIdType.MESH)


_ANY = pl.BlockSpec(memory_space=pl.ANY)


def _all_gather(placed, kinds, shapes):
    nt = len(placed)

    def body(*refs):
        outs = refs[nt:2 * nt]
        ssem, rsem = refs[2 * nt:]
        x, y, c = _position()
        my = 2 * x + y
        sib = (x, y, 1 - c)
        sends, fwds = [], []
        for t in range(nt):
            h = shapes[t][0] // 2
            own = _half(_slot(kinds[t], outs[t], my, shapes[t]), c, h)
            for j in (1, 2, 3):
                tx, ty = _peer_chip(x, y, j)
                cp = _remote(own, own, ssem.at[6 * t + j - 1], rsem.at[6 * t + j - 1], (tx, ty, c))
                cp.start()
                sends.append(cp)
        for t in range(nt):
            h = shapes[t][0] // 2
            for j in (1, 2, 3):
                tx, ty = _peer_chip(x, y, j)
                got = _half(_slot(kinds[t], outs[t], 2 * tx + ty, shapes[t]), c, h)
                _remote(got, got, ssem.at[6 * t + j - 1], rsem.at[6 * t + j - 1], sib).wait_recv()
                cp = _remote(got, got, ssem.at[6 * t + 2 + j], rsem.at[6 * t + 2 + j], sib)
                cp.start()
                fwds.append(cp)
        for t in range(nt):
            h = shapes[t][0] // 2
            for j in (1, 2, 3):
                tx, ty = _peer_chip(x, y, j)
                other = _half(_slot(kinds[t], outs[t], 2 * tx + ty, shapes[t]), 1 - c, h)
                _remote(other, other, ssem.at[6 * t + 2 + j], rsem.at[6 * t + 2 + j], sib).wait_recv()
        for cp in sends + fwds:
            cp.wait_send()

    return pl.pallas_call(
        body, name="all_gather", in_specs=[_ANY] * nt, out_specs=[_ANY] * nt,
        out_shape=[jax.ShapeDtypeStruct(p.shape, p.dtype) for p in placed],
        input_output_aliases={t: t for t in range(nt)},
        scratch_shapes=[pltpu.SemaphoreType.DMA((6 * nt,)), pltpu.SemaphoreType.DMA((6 * nt,))],
    )(*placed)


def _d2d_stream(src4, other_half, name):
    s, _, rows, cols = src4.shape
    tr = _row_tile(rows, cols, SUB16, STREAM_ELEMS_PER_BLOCK)
    nblk = rows // tr

    nh = src4.shape[1]

    def body(c_ref, src_ref, dst_ref, ssem, rsem):
        k = pl.program_id(0)
        i = pl.program_id(1)
        x, y, c = _position()
        sib = (x, y, 1 - c)
        blk = dst_ref.at[pl.ds(pl.multiple_of((k * nblk + i) * tr, SUB16), tr)]
        cp = _remote(src_ref, blk, ssem, rsem, sib)
        cp.start()
        cp.wait_send()

        @pl.when(jnp.logical_and(k == s - 1, i == nblk - 1))
        def _():
            _remote(dst_ref, dst_ref, ssem, rsem, sib).wait_recv()

    if other_half:
        src_map = lambda k, i, c_ref: ((k * nh + 1 - c_ref[0]) * nblk + i, 0)
    else:
        src_map = lambda k, i, c_ref: (k * nh * nblk + i, 0)
    out = pl.pallas_call(
        body, name=name,
        grid_spec=pltpu.PrefetchScalarGridSpec(
            num_scalar_prefetch=1, grid=(s, nblk),
            in_specs=[pl.BlockSpec((tr, cols), src_map)], out_specs=_ANY,
            scratch_shapes=[pltpu.SemaphoreType.DMA, pltpu.SemaphoreType.DMA]),
        out_shape=jax.ShapeDtypeStruct((s * rows, cols), src4.dtype),
        compiler_params=_cp("arbitrary", "arbitrary"))(_core_index(), src4.reshape(s * nh * rows, cols))
    return out.reshape(s, rows, cols)


def _rs_all_to_all(chip_sums, kinds, shard_shapes):
    nt = len(chip_sums)

    def half_shard(shape):
        return (shape[0] // 2,) + tuple(shape[1:])

    def body(*refs):
        ins, outs = refs[:nt], refs[nt:2 * nt]
        ssem, rsem, lsem = refs[2 * nt:]
        x, y, c = _position()
        my = 2 * x + y
        cps = []
        for t in range(nt):
            hs = half_shard(shard_shapes[t])
            cp = pltpu.make_async_copy(_slot(kinds[t], ins[t], my, hs), outs[t].at[my], lsem.at[t])
            cp.start()
            cps.append(cp)
            for j in (1, 2, 3):
                tx, ty = _peer_chip(x, y, j)
                cp = _remote(_slot(kinds[t], ins[t], 2 * tx + ty, hs), outs[t].at[my], ssem.at[3 * t + j - 1],
                             rsem.at[3 * t + j - 1], (tx, ty, c))
                cp.start()
                cps.append(cp)
        for cp in cps:
            cp.wait()

    return pl.pallas_call(
        body, name="rs_all_to_all", in_specs=[_ANY] * nt, out_specs=[_ANY] * nt,
        out_shape=[jax.ShapeDtypeStruct((N_CHIPS,) + half_shard(s), a.dtype)
                   for s, a in zip(shard_shapes, chip_sums)],
        scratch_shapes=[pltpu.SemaphoreType.DMA((3 * nt,)), pltpu.SemaphoreType.DMA((3 * nt,)),
                        pltpu.SemaphoreType.DMA((nt,))],
    )(*chip_sums)


SMALL = (("sc_conv_w", True), ("sc_conv_b", False), ("lru_b_in", True), ("lru_conv_w", True),
         ("lru_conv_b", True), ("lru_b_gate", True), ("lru_lambda", True), ("ffn_conv_w", True),
         ("ffn_conv_b", False), ("ln_g", True), ("ln_b", True))
PACK_ROW_MULT = 2 * SUB16


def _pack_rows(shapes):
    n = sum(math.prod(shapes[name]) for name, _ in SMALL)
    rows = -(-n // 128)
    return -(-rows // PACK_ROW_MULT) * PACK_ROW_MULT


def _pack_local(vals, shapes):
    flat = jnp.concatenate([vals[name].reshape(-1) for name, _ in SMALL])
    rows = _pack_rows(shapes)
    return jnp.pad(flat, (0, rows * 128 - flat.shape[0])).reshape(rows, 128)


def _unpack_local(pack, shapes):
    flat = pack.reshape(-1)
    out, off = {}, 0
    for name, _ in SMALL:
        n = math.prod(shapes[name])
        out[name] = flat[off:off + n].reshape(shapes[name])
        off += n
    return out


def _pack_slots(fulls, shapes):
    parts = []
    for name, sharded in SMALL:
        v = fulls[name]
        if sharded:
            ns = shapes[name][-1]
            v = jnp.moveaxis(v.reshape(v.shape[:-1] + (N_CHIPS, ns)), -2, 0).reshape(N_CHIPS, -1)
        else:
            v = jnp.broadcast_to(v.reshape(1, -1), (N_CHIPS, v.size))
        parts.append(v)
    flat = jnp.concatenate(parts, axis=1)
    rows = _pack_rows(shapes)
    return jnp.pad(flat, ((0, 0), (0, rows * 128 - flat.shape[1]))).reshape(N_CHIPS, rows, 128)


def _unpack_slots(packs, shapes):
    flat = packs.reshape(N_CHIPS, -1)
    out, off = {}, 0
    for name, sharded in SMALL:
        n = math.prod(shapes[name])
        if sharded:
            seg = flat[:, off:off + n].reshape((N_CHIPS,) + tuple(shapes[name]))
            seg = jnp.moveaxis(seg, 0, -2)
            out[name] = seg.reshape(seg.shape[:-2] + (N_CHIPS * shapes[name][-1],))
        off += n
    return out


WEIGHTS = ("sc_w_in", "sc_conv_w", "sc_conv_b", "sc_w_out", "lru_w_in", "lru_b_in", "lru_conv_w", "lru_conv_b",
           "lru_w_gate", "lru_b_gate", "lru_lambda", "lru_w_out", "ffn_w_up", "ffn_conv_w", "ffn_conv_b",
           "ffn_w_down", "ln_g", "ln_b")
BIG = (("sc_w_in", "col"), ("sc_w_out", "row"), ("lru_w_in", "col"), ("lru_w_gate", "lead"),
       ("lru_w_out", "row"), ("ffn_w_up", "col"), ("ffn_w_down", "row"))


def kernel(x, sc_w_in, sc_conv_w, sc_conv_b, sc_w_out, lru_w_in, lru_b_in, lru_conv_w, lru_conv_b, lru_w_gate, lru_b_gate, lru_lambda, lru_w_out, ffn_w_up, ffn_conv_w, ffn_conv_b, ffn_w_down, ln_g, ln_b, loss_target, m_sc_w_in, m_sc_conv_w, m_sc_conv_b, m_sc_w_out, m_lru_w_in, m_lru_b_in, m_lru_conv_w, m_lru_conv_b, m_lru_w_gate, m_lru_b_gate, m_lru_lambda, m_lru_w_out, m_ffn_w_up, m_ffn_conv_w, m_ffn_conv_b, m_ffn_w_down, m_ln_g, m_ln_b, v_sc_w_in, v_sc_conv_w, v_sc_conv_b, v_sc_w_out, v_lru_w_in, v_lru_b_in, v_lru_conv_w, v_lru_conv_b, v_lru_w_gate, v_lru_b_gate, v_lru_lambda, v_lru_w_out, v_ffn_w_up, v_ffn_conv_w, v_ffn_conv_b, v_ffn_w_down, v_ln_g, v_ln_b):
    w = dict(zip(WEIGHTS, (sc_w_in, sc_conv_w, sc_conv_b, sc_w_out, lru_w_in, lru_b_in, lru_conv_w, lru_conv_b,
                           lru_w_gate, lru_b_gate, lru_lambda, lru_w_out, ffn_w_up, ffn_conv_w, ffn_conv_b,
                           ffn_w_down, ln_g, ln_b)))
    mom = dict(zip(WEIGHTS, (m_sc_w_in, m_sc_conv_w, m_sc_conv_b, m_sc_w_out, m_lru_w_in, m_lru_b_in, m_lru_conv_w,
                             m_lru_conv_b, m_lru_w_gate, m_lru_b_gate, m_lru_lambda, m_lru_w_out, m_ffn_w_up,
                             m_ffn_conv_w, m_ffn_conv_b, m_ffn_w_down, m_ln_g, m_ln_b)))
    vel = dict(zip(WEIGHTS, (v_sc_w_in, v_sc_conv_w, v_sc_conv_b, v_sc_w_out, v_lru_w_in, v_lru_b_in, v_lru_conv_w,
                             v_lru_conv_b, v_lru_w_gate, v_lru_b_gate, v_lru_lambda, v_lru_w_out, v_ffn_w_up,
                             v_ffn_conv_w, v_ffn_conv_b, v_ffn_w_down, v_ln_g, v_ln_b)))
    bd, seq, d = x.shape
    t = bd * seq
    small_shapes = {name: w[name].shape for name, _ in SMALL}
    big_names = [n for n, _ in BIG]
    kinds = [k for _, k in BIG] + ["lead"]
    shard_shapes = [w[n].shape for n in big_names]

    w_pack = _pack_local(w, small_shapes)
    rs_shapes = shard_shapes + [w_pack.shape]
    placed = [_cast_place(w[n], k, BF16, "place_w") for n, k in BIG] + [_cast_place(w_pack, "lead", F32, "place_w")]
    gathered = _all_gather(placed, kinds, rs_shapes)
    full = dict(zip(big_names, gathered[:-1]))
    full.update(_unpack_slots(gathered[-1], small_shapes))
    full["sc_conv_b"] = sc_conv_b
    full["ffn_conv_b"] = ffn_conv_b
    wg_full = jnp.moveaxis(full["lru_w_gate"], 0, -2)
    wg_full = wg_full.reshape(wg_full.shape[:-2] + (2 * LRU_BLOCK,))
    f = full["ffn_w_down"].shape[1]
    rw = full["lru_w_out"].shape[1]

    x0 = x.reshape(t, d)
    xb = x0.astype(BF16)
    cur, cur_b = x0, xb
    saved = []

    for i in range(DEPTH):
        j = i // 2
        s = {"xb": cur_b}
        if i % 2 == 0:
            h = _mm_nn(cur_b, full["sc_w_in"], j, None, 3 * d, "sc_in")
            q = _sc_fwd(h, full["sc_conv_w"][j], full["sc_conv_b"][j][None], seq, "sc_fwd")
            z1, x1, x1b = _mm_nn_ln(q, full["sc_w_out"], j, cur, full["ln_g"][i, 0][None], full["ln_b"][i, 0][None],
                                    "sc_out_ln")
        else:
            h = _mm_nn(cur_b, full["lru_w_in"], j, full["lru_b_in"][j][None], 2 * rw, "lru_in")
            hs, q = _lru_fwd(h, full["lru_conv_w"][j], full["lru_conv_b"][j][None], wg_full[j],
                             full["lru_b_gate"][j], full["lru_lambda"][j][None], seq, "lru_fwd")
            s["hs"] = hs
            z1, x1, x1b = _mm_nn_ln(q, full["lru_w_out"], j, cur, full["ln_g"][i, 0][None],
                                    full["ln_b"][i, 0][None], "lru_out_ln")
        s.update(h=h, q=q, z1=z1, x1b=x1b)
        h3, pre3, act = _ffn_up(x1b, full["ffn_w_up"], i, full["ffn_conv_w"][i], full["ffn_conv_b"][i][None], seq,
                                "ffn_up")
        z2, x2, x2b = _mm_nn_ln(act, full["ffn_w_down"], i, x1, full["ln_g"][i, 1][None], full["ln_b"][i, 1][None],
                                "ffn_down_ln")
        s.update(h3=h3, pre3=pre3, act=act, z2=z2)
        saved.append(s)
        cur, cur_b = x2, x2b

    dcur, loss_parts = _loss_bwd(cur, loss_target.reshape(t, d))
    loss = lax.psum(jnp.sum(loss_parts), MESH_AXES)

    gp = {n: [None] * w[n].shape[0] for n in WEIGHTS}
    for i in reversed(range(DEPTH)):
        j = i // 2
        s = saved[i]
        dz2, dz2b, dg, db = _ln_bwd(dcur, s["z2"], full["ln_g"][i, 1][None], "ln_bwd")
        gp["ln_g"][i] = [None, dg[0]]
        gp["ln_b"][i] = [None, db[0]]
        gp["ffn_w_down"][i] = _mm_tn(s["act"], dz2b[None], f // 2, d, "ffn_down_dw")
        dh3, dcwg, dcwv, dcbg, dcbv = _ffn_down_bwd(dz2b, full["ffn_w_down"], i, s["h3"], s["pre3"],
                                                    full["ffn_conv_w"][i], seq, "ffn_down_bwd")
        gp["ffn_conv_w"][i] = jnp.concatenate([dcwg, dcwv], axis=1)
        gp["ffn_conv_b"][i] = jnp.concatenate([dcbg[0], dcbv[0]])
        dx1 = _mm_nt_res(dh3, full["ffn_w_up"], i, dz2, f // 2, "ffn_up_dx")
        gp["ffn_w_up"][i] = _mm_tn(s["x1b"], dh3, d, f // 2, "ffn_up_dw")
        dz1, dz1b, dg, db = _ln_bwd(dx1, s["z1"], full["ln_g"][i, 0][None], "ln_bwd")
        gp["ln_g"][i][0] = dg[0]
        gp["ln_b"][i][0] = db[0]
        gp["ln_g"][i] = jnp.stack(gp["ln_g"][i])
        gp["ln_b"][i] = jnp.stack(gp["ln_b"][i])
        if i % 2 == 0:
            dq = _mm_nt(dz1b, full["sc_w_out"], j, d, "sc_out_dx")
            gp["sc_w_out"][j] = _mm_tn(s["q"], dz1b[None], d, d, "sc_out_dw")
            dh3, dcw, dcb = _sc_bwd(s["h"], dq, full["sc_conv_w"][j], full["sc_conv_b"][j][None], seq, "sc_bwd")
            gp["sc_conv_w"][j] = dcw
            gp["sc_conv_b"][j] = dcb[0]
            dcur = _mm_nt_res(dh3, full["sc_w_in"], j, dz1, d, "sc_in_dx")
            gp["sc_w_in"][j] = _mm_tn(s["xb"], dh3, d, d, "sc_in_dw")
        else:
            dq = _mm_nt(dz1b, full["lru_w_out"], j, rw, "lru_out_dx")
            gp["lru_w_out"][j] = _mm_tn(s["q"], dz1b[None], rw, d, "lru_out_dw")
            dh3, dbin, dcw, dcb, dwg, dbg, dlam = _lru_bwd(
                s["h"], s["hs"], dq, full["lru_conv_w"][j], full["lru_conv_b"][j][None], wg_full[j],
                full["lru_b_gate"][j], full["lru_lambda"][j][None], seq, "lru_bwd")
            gp["lru_b_in"][j] = dbin[0]
            gp["lru_conv_w"][j] = dcw
            gp["lru_conv_b"][j] = dcb[0]
            gp["lru_w_gate"][j] = dwg
            gp["lru_b_gate"][j] = dbg
            gp["lru_lambda"][j] = dlam[0]
            dcur = _mm_nt_res(dh3, full["lru_w_in"], j, dz1, rw, "lru_in_dx")
            gp["lru_w_in"][j] = _mm_tn(s["xb"], dh3, d, rw, "lru_in_dw")
    grad_x = dcur.reshape(bd, seq, d)
    gp = {n: jnp.stack(v) for n, v in gp.items()}

    ns_gate = w["lru_w_gate"].shape[-1]
    gate = gp["lru_w_gate"]
    gate = jnp.moveaxis(gate.reshape(gate.shape[:-1] + (N_CHIPS, ns_gate)), -2, 0).astype(BF16)
    partials = [gate if n == "lru_w_gate" else gp[n] for n in big_names]
    partials.append(_pack_slots(gp, small_shapes))
    chip_sums = []
    for p, kind in zip(partials, kinds):
        lead = kind == "lead"
        p4 = p.reshape(N_CHIPS if lead else 1, 2, -1, p.shape[-1])
        from_sibling = _d2d_stream(p4, True, "rs_swap")
        half_full = list(p.shape)
        half_full[1 if lead else 0] //= 2
        chip_sums.append(_add_pair(p4, from_sibling, "rs_add_pair").reshape(half_full))
    arrived = _rs_all_to_all(chip_sums, kinds, rs_shapes)

    g_out, d_out, m_out, v_out = {}, {}, {}, {}
    packs = {}
    for n, r in zip(big_names + ["pack"], arrived):
        g_mine = _add_chips(r, "rs_add_chips").reshape(-1, r.shape[-1])
        g_sib = _d2d_stream(g_mine[None, None], False, "rs_share")[0]
        if n == "pack":
            packs = _adamw(w_pack, g_mine, g_sib, _pack_local(mom, small_shapes), _pack_local(vel, small_shapes),
                           "adamw")
        else:
            g_out[n], d_out[n], m_out[n], v_out[n] = _adamw(w[n], g_mine, g_sib, mom[n], vel[n], "adamw")
    for dst, pack in zip((g_out, d_out, m_out, v_out), packs):
        dst.update(_unpack_local(pack, small_shapes))

    return (loss, grad_x, *[g_out[n] for n in WEIGHTS], *[d_out[n] for n in WEIGHTS],
            *[m_out[n] for n in WEIGHTS], *[v_out[n] for n in WEIGHTS])
```

```python
import math

import jax
import jax.numpy as jnp
from jax import lax
from jax.experimental import pallas as pl
from jax.experimental.pallas import tpu as pltpu

F32 = jnp.float32
BF16 = jnp.bfloat16

DEPTH = 4
LRU_HEADS = 10
LRU_BLOCK = 128
LRU_C = 8.0
LN_EPS = 1e-5
ALPHA = (2.0 * DEPTH) ** 0.25
ADAM_LR, ADAM_B1, ADAM_B2, ADAM_EPS, ADAM_WD, ADAM_STEP = 0.001, 0.9, 0.999, 1e-08, 0.01, 10
N_CHIPS = 4
MESH_AXES = ("x", "y", "c")

VMEM_LIMIT_BYTES = 48 * 1024 * 1024
TM_MM = 512
TM_RES = 1024
TT_MM = 512
TM_SC = 256
FFN_COL_BLOCKS = 2
RC = 64
LANE = 128
MXU_COLS = 256
TS_LRU = 128
TM_LN = 512
ELEMS_PER_BLOCK = 256 * 1024
STREAM_ELEMS_PER_BLOCK = 1024 * 1024
SUB = 8
SUB16 = 16


def _cp(*sem):
    return pltpu.CompilerParams(dimension_semantics=sem, vmem_limit_bytes=VMEM_LIMIT_BYTES)


def _sigmoid(v):
    return 1.0 / (1.0 + jnp.exp(-v))


def _softplus(v):
    e = jnp.exp(-jnp.abs(v))
    log1p = jnp.where(e < 1e-3, e * (1.0 - e * (0.5 - e * (1.0 / 3.0))), jnp.log(1.0 + e))
    return jnp.maximum(v, 0.0) + log1p


def _one_minus_exp(v):
    series = -v * (1.0 + v * (0.5 + v * (1.0 / 6.0 + v * (1.0 / 24.0))))
    return jnp.where(v > -0.02, series, 1.0 - jnp.exp(v))


def _gelu_and_grad(v):
    k = math.sqrt(2.0 / math.pi)
    t = jnp.tanh(k * (v + 0.044715 * v * v * v))
    val = 0.5 * v * (1.0 + t)
    grad = 0.5 * (1.0 + t) + 0.5 * v * (1.0 - t * t) * k * (1.0 + 3.0 * 0.044715 * v * v)
    return val, grad


def _down(ext, k, n_head):
    if k:
        ext = pltpu.roll(ext, k, 0)
    return ext[n_head:]


def _up(ext, k, n):
    if k:
        ext = pltpu.roll(ext, ext.shape[0] - k, 0)
    return ext[:n]


def _row(ref, k):
    return ref[k:k + 1, :]


def _colsum(v):
    return jnp.sum(v, axis=0, keepdims=True)


def _mm_nn(a, w3, l, bias, tn, name):
    m, k = a.shape
    n = w3.shape[2]
    tm = min(TM_MM, m)
    has_bias = bias is not None

    def body(*refs):
        if has_bias:
            a_ref, w_ref, b_ref, o_ref = refs
        else:
            a_ref, w_ref, o_ref = refs
        acc = jnp.dot(a_ref[...], w_ref[...], preferred_element_type=F32)
        if has_bias:
            acc = acc + b_ref[...]
        o_ref[...] = acc.astype(o_ref.dtype)

    in_specs = [pl.BlockSpec((tm, k), lambda i, j: (i, 0)),
                pl.BlockSpec((None, k, tn), lambda i, j: (l, 0, j))]
    args = [a, w3]
    if has_bias:
        in_specs.append(pl.BlockSpec((1, tn), lambda i, j: (0, j)))
        args.append(bias)
    return pl.pallas_call(
        body, name=name, grid=(m // tm, n // tn), in_specs=in_specs,
        out_specs=pl.BlockSpec((tm, tn), lambda i, j: (i, j)),
        out_shape=jax.ShapeDtypeStruct((m, n), BF16),
        compiler_params=_cp("parallel", "arbitrary"))(*args)


def _mm_nn_ln(a, w3, l, xres, g, b, name, rider=()):
    m, k = a.shape
    n = w3.shape[2]
    tm = min(TM_MM, m)
    bufs, rkinds, rspecs, rshapes, rsems = _rider_args(rider)
    nr = len(bufs)

    def body(*refs):
        a_ref, w_ref, x_ref, g_ref, b_ref = refs[:5]
        z_ref, xn_ref, xb_ref = refs[5 + nr:8 + nr]
        rout, sems = refs[8 + nr:8 + 2 * nr], refs[8 + 2 * nr:]
        if nr:
            @pl.when(pl.program_id(0) == 0)
            def _():
                _rider_start(rout, rkinds, *sems)

        y = jnp.dot(a_ref[...], w_ref[...], preferred_element_type=F32)
        z = ALPHA * x_ref[...] + y
        mu = jnp.mean(z, axis=-1, keepdims=True)
        zc = z - mu
        var = jnp.mean(zc * zc, axis=-1, keepdims=True)
        xn = zc * lax.rsqrt(var + LN_EPS) * g_ref[...] + b_ref[...]
        z_ref[...] = z
        xn_ref[...] = xn
        xb_ref[...] = xn.astype(BF16)
        if nr:
            @pl.when(pl.program_id(0) == m // tm - 1)
            def _():
                _rider_wait(rout, rkinds, *sems)

    row = pl.BlockSpec((tm, n), lambda i: (i, 0))
    vec = pl.BlockSpec((1, n), lambda i: (0, 0))
    return pl.pallas_call(
        body, name=name, grid=(m // tm,),
        in_specs=[pl.BlockSpec((tm, k), lambda i: (i, 0)),
                  pl.BlockSpec((None, k, n), lambda i: (l, 0, 0)), row, vec, vec] + rspecs,
        out_specs=[row, row, row] + rspecs,
        out_shape=[jax.ShapeDtypeStruct((m, n), F32), jax.ShapeDtypeStruct((m, n), F32),
                   jax.ShapeDtypeStruct((m, n), BF16)] + rshapes,
        input_output_aliases={5 + u: 3 + u for u in range(nr)},
        scratch_shapes=rsems,
        compiler_params=_cp("arbitrary"))(a, w3, xres, g, b, *bufs)


def _mm_nt(a, w3, l, tk, name):
    m, n = a.shape
    kd = w3.shape[1]
    tm = min(TM_MM, m)

    def body(a_ref, w_ref, o_ref):
        o_ref[...] = lax.dot_general(a_ref[...], w_ref[...], (((1,), (1,)), ((), ())),
                                     preferred_element_type=F32).astype(o_ref.dtype)

    return pl.pallas_call(
        body, name=name, grid=(m // tm, kd // tk),
        in_specs=[pl.BlockSpec((tm, n), lambda i, j: (i, 0)),
                  pl.BlockSpec((None, tk, n), lambda i, j: (l, j, 0))],
        out_specs=pl.BlockSpec((tm, tk), lambda i, j: (i, j)),
        out_shape=jax.ShapeDtypeStruct((m, kd), BF16),
        compiler_params=_cp("parallel", "arbitrary"))(a, w3)


def _mm_nt_res(dh3, w3, l, dz, tc, name):
    g, m, cg = dh3.shape
    kd = w3.shape[1]
    ncg = cg // tc
    nk = g * ncg
    tm = min(TM_RES, m)

    def body(a_ref, w_ref, dz_ref, o_ref, acc):
        k = pl.program_id(1)

        @pl.when(k == 0)
        def _():
            acc[...] = ALPHA * dz_ref[...]

        acc[...] += lax.dot_general(a_ref[...], w_ref[...], (((1,), (1,)), ((), ())),
                                    preferred_element_type=F32)

        @pl.when(k == nk - 1)
        def _():
            o_ref[...] = acc[...]

    return pl.pallas_call(
        body, name=name, grid=(m // tm, nk),
        in_specs=[pl.BlockSpec((None, tm, tc), lambda i, k: (k // ncg, i, k % ncg)),
                  pl.BlockSpec((None, kd, tc), lambda i, k: (l, 0, k)),
                  pl.BlockSpec((tm, kd), lambda i, k: (i, 0))],
        out_specs=pl.BlockSpec((tm, kd), lambda i, k: (i, 0)),
        out_shape=jax.ShapeDtypeStruct((m, kd), F32),
        scratch_shapes=[pltpu.VMEM((tm, kd), F32)],
        compiler_params=_cp("parallel", "arbitrary"))(dh3, w3, dz)


def _mm_tn(a, b3, tka, tnb, name):
    t, ka = a.shape
    g, _, cg = b3.shape
    ncg = cg // tnb
    tt = min(TT_MM, t)
    nt = t // tt

    def body(a_ref, b_ref, o_ref, acc):
        s = pl.program_id(2)

        @pl.when(s == 0)
        def _():
            acc[...] = jnp.zeros_like(acc)

        acc[...] += lax.dot_general(a_ref[...], b_ref[...], (((0,), (0,)), ((), ())),
                                    preferred_element_type=F32)

        @pl.when(s == nt - 1)
        def _():
            o_ref[...] = acc[...].astype(o_ref.dtype)

    return pl.pallas_call(
        body, name=name, grid=(ka // tka, g * ncg, nt),
        in_specs=[pl.BlockSpec((tt, tka), lambda i, j, s: (s, i)),
                  pl.BlockSpec((None, tt, tnb), lambda i, j, s: (j // ncg, s, j % ncg))],
        out_specs=pl.BlockSpec((tka, tnb), lambda i, j, s: (i, j)),
        out_shape=jax.ShapeDtypeStruct((ka, g * cg), BF16),
        scratch_shapes=[pltpu.VMEM((tka, tnb), F32)],
        compiler_params=_cp("parallel", "parallel", "arbitrary"))(a, b3)


def _ln_bwd(dxn, z, g, name):
    m, d = z.shape
    tm = min(TM_LN, m)

    def body(dx_ref, z_ref, g_ref, dz_ref, dzb_ref, dg_ref, db_ref):
        @pl.when(pl.program_id(0) == 0)
        def _():
            dg_ref[...] = jnp.zeros_like(dg_ref)
            db_ref[...] = jnp.zeros_like(db_ref)

        zz = z_ref[...]
        dx = dx_ref[...]
        mu = jnp.mean(zz, axis=-1, keepdims=True)
        zc = zz - mu
        var = jnp.mean(zc * zc, axis=-1, keepdims=True)
        rstd = lax.rsqrt(var + LN_EPS)
        xh = zc * rstd
        dg_ref[...] += _colsum(dx * xh)
        db_ref[...] += _colsum(dx)
        dxh = dx * g_ref[...]
        m1 = jnp.mean(dxh, axis=-1, keepdims=True)
        m2 = jnp.mean(dxh * xh, axis=-1, keepdims=True)
        dz = rstd * (dxh - m1 - xh * m2)
        dz_ref[...] = dz
        dzb_ref[...] = dz.astype(BF16)

    row = pl.BlockSpec((tm, d), lambda i: (i, 0))
    vec = pl.BlockSpec((1, d), lambda i: (0, 0))
    return pl.pallas_call(
        body, name=name, grid=(m // tm,), in_specs=[row, row, vec],
        out_specs=[row, row, vec, vec],
        out_shape=[jax.ShapeDtypeStruct((m, d), F32), jax.ShapeDtypeStruct((m, d), BF16),
                   jax.ShapeDtypeStruct((1, d), F32), jax.ShapeDtypeStruct((1, d), F32)],
        compiler_params=_cp("arbitrary"))(dxn, z, g)


def _loss_bwd(y, target):
    m, d = y.shape
    tm = min(TM_LN, m)

    def body(y_ref, t_ref, dy_ref, ls_ref):
        @pl.when(pl.program_id(0) == 0)
        def _():
            ls_ref[...] = jnp.zeros_like(ls_ref)

        e = y_ref[...] - t_ref[...]
        dy_ref[...] = e * (1.0 / d)
        ls_ref[...] += _colsum(e * e) * (0.5 / d)

    row = pl.BlockSpec((tm, d), lambda i: (i, 0))
    return pl.pallas_call(
        body, name="loss_bwd", grid=(m // tm,), in_specs=[row, row],
        out_specs=[row, pl.BlockSpec((1, d), lambda i: (0, 0))],
        out_shape=[jax.ShapeDtypeStruct((m, d), F32), jax.ShapeDtypeStruct((1, d), F32)],
        compiler_params=_cp("arbitrary"))(y, target)


def _sc_fwd(h, cw, cb, seq, name):
    t, d3 = h.shape
    d = d3 // 3
    tm = min(TM_SC, seq)

    def body(hb_ref, hc_ref, hv_ref, cw_ref, cb_ref, q_ref, carry):
        i = pl.program_id(0)

        @pl.when(lax.rem(i * tm, seq) == 0)
        def _():
            carry[...] = jnp.zeros_like(carry)

        p = hc_ref[...].astype(F32) * hv_ref[...].astype(F32)
        ext = jnp.concatenate([carry[...], p], axis=0)
        u = cb_ref[...] + _row(cw_ref, 0) * _down(ext, 2, SUB) + _row(cw_ref, 1) * _down(ext, 1, SUB) \
            + _row(cw_ref, 2) * p
        q_ref[...] = (hb_ref[...].astype(F32) * u).astype(BF16)
        carry[...] = p[tm - SUB:, :]

    blk = lambda c: pl.BlockSpec((tm, d), lambda i: (i, c))
    return pl.pallas_call(
        body, name=name, grid=(t // tm,),
        in_specs=[blk(0), blk(1), blk(2), pl.BlockSpec((3, d), lambda i: (0, 0)),
                  pl.BlockSpec((1, d), lambda i: (0, 0))],
        out_specs=pl.BlockSpec((tm, d), lambda i: (i, 0)),
        out_shape=jax.ShapeDtypeStruct((t, d), BF16),
        scratch_shapes=[pltpu.VMEM((SUB, d), F32)],
        compiler_params=_cp("arbitrary"))(h, h, h, cw, cb)


def _sc_bwd(h, dq, cw, cb, seq, name):
    t, d3 = h.shape
    d = d3 // 3
    tm = min(TM_SC, seq)
    nt = t // tm
    hpt = tm // SUB16

    def body(hb_ref, hc_ref, hv_ref, hch_ref, hvh_ref, dq_ref, cw_ref, cb_ref,
             dh_ref, dcw_ref, dcb_ref, carry):
        i = pl.program_id(0)
        ri = nt - 1 - i

        @pl.when(i == 0)
        def _():
            dcw_ref[...] = jnp.zeros_like(dcw_ref)
            dcb_ref[...] = jnp.zeros_like(dcb_ref)

        @pl.when(lax.rem((ri + 1) * tm, seq) == 0)
        def _():
            carry[...] = jnp.zeros_like(carry)

        keep = jnp.where(lax.rem(ri * tm, seq) == 0, 0.0, 1.0)
        gb = hb_ref[...].astype(F32)
        gc = hc_ref[...].astype(F32)
        v = hv_ref[...].astype(F32)
        p = gc * v
        p_head = hch_ref[...].astype(F32) * hvh_ref[...].astype(F32) * keep
        ext = jnp.concatenate([p_head, p], axis=0)
        pm2 = _down(ext, 2, SUB16)
        pm1 = _down(ext, 1, SUB16)
        u = cb_ref[...] + _row(cw_ref, 0) * pm2 + _row(cw_ref, 1) * pm1 + _row(cw_ref, 2) * p
        dqf = dq_ref[...].astype(F32)
        du = dqf * gb
        dcb_ref[...] += _colsum(du)
        dcw_ref[0:1, :] += _colsum(du * pm2)
        dcw_ref[1:2, :] += _colsum(du * pm1)
        dcw_ref[2:3, :] += _colsum(du * p)
        ext2 = jnp.concatenate([du, carry[...]], axis=0)
        dp = _row(cw_ref, 2) * du + _row(cw_ref, 1) * _up(ext2, 1, tm) + _row(cw_ref, 0) * _up(ext2, 2, tm)
        carry[...] = du[0:SUB, :]
        dh_ref[0] = (dqf * u).astype(BF16)
        dh_ref[1] = (dp * v).astype(BF16)
        dh_ref[2] = (dp * gc).astype(BF16)

    blk = lambda c: pl.BlockSpec((tm, d), lambda i: (nt - 1 - i, c))
    head = lambda c: pl.BlockSpec((SUB16, d), lambda i: (jnp.maximum((nt - 1 - i) * hpt - 1, 0), c))
    vec = lambda r: pl.BlockSpec((r, d), lambda i: (0, 0))
    return pl.pallas_call(
        body, name=name, grid=(nt,),
        in_specs=[blk(0), blk(1), blk(2), head(1), head(2),
                  pl.BlockSpec((tm, d), lambda i: (nt - 1 - i, 0)), vec(3), vec(1)],
        out_specs=[pl.BlockSpec((3, tm, d), lambda i: (0, nt - 1 - i, 0)), vec(3), vec(1)],
        out_shape=[jax.ShapeDtypeStruct((3, t, d), BF16), jax.ShapeDtypeStruct((3, d), F32),
                   jax.ShapeDtypeStruct((1, d), F32)],
        scratch_shapes=[pltpu.VMEM((SUB, d), F32)],
        compiler_params=_cp("arbitrary"))(h, h, h, h, h, dq, cw, cb)


def _fold(v):
    return jnp.sum(v.reshape(v.shape[0] // SUB, SUB, v.shape[1]), axis=0)


def _ffn_up(xb, w3, l, cw, cb, seq, name, rider=()):
    t, d = xb.shape
    f = w3.shape[2] // 2
    tc = f // FFN_COL_BLOCKS
    tm = min(TM_MM, seq)
    nc = FFN_COL_BLOCKS
    bufs, rkinds, rspecs, rshapes, rsems = _rider_args(rider)
    nr = len(bufs)

    def body(*refs):
        x_ref, wg_ref, wv_ref, cwg_ref, cwv_ref, cbg_ref, cbv_ref = refs[:7]
        h_ref, pre_ref, act_ref = refs[7 + nr:10 + nr]
        rout = refs[10 + nr:10 + 2 * nr]
        eg, ev = refs[10 + 2 * nr:12 + 2 * nr]
        sems = refs[12 + 2 * nr:]
        i = pl.program_id(1)
        if nr:
            @pl.when(jnp.logical_and(pl.program_id(0) == 0, i == 0))
            def _():
                _rider_start(rout, rkinds, *sems)

        @pl.when(lax.rem(i * tm, seq) == 0)
        def _():
            eg[0:SUB, :] = jnp.zeros((SUB, tc), F32)
            ev[0:SUB, :] = jnp.zeros((SUB, tc), F32)

        xx = x_ref[...]

        def matmul(lo, hi):
            eg[SUB:, lo:hi] = jnp.dot(xx, wg_ref[:, lo:hi], preferred_element_type=F32)
            ev[SUB:, lo:hi] = jnp.dot(xx, wv_ref[:, lo:hi], preferred_element_type=F32)

        def epilogue(lo, hi):
            for c0 in range(lo, hi, LANE):
                cols = slice(c0, c0 + LANE)
                taps = [[ref[k:k + 1, cols] for k in range(3)] + [bref[:, cols]]
                        for ref, bref in ((cwg_ref, cbg_ref), (cwv_ref, cbv_ref))]
                for r0 in range(0, tm, RC):
                    rows = slice(r0, r0 + RC)
                    pres = []
                    for half, e_ref in enumerate((eg, ev)):
                        w0, w1, w2, bias = taps[half]
                        e = e_ref[r0:r0 + RC + SUB, cols]
                        cur = e[SUB:]
                        pre = bias + w0 * _down(e, 2, SUB) + w1 * _down(e, 1, SUB) + w2 * cur
                        h_ref[half, rows, cols] = cur.astype(BF16)
                        pre_ref[half, rows, cols] = pre.astype(BF16)
                        pres.append(pre)
                    act_ref[rows, cols] = (pres[0] * _sigmoid(pres[0]) * pres[1]).astype(BF16)

        blocks = [(lo, min(lo + MXU_COLS, tc)) for lo in range(0, tc, MXU_COLS)]
        matmul(*blocks[0])
        for b, blk in enumerate(blocks):
            if b + 1 < len(blocks):
                matmul(*blocks[b + 1])
            epilogue(*blk)
        eg[0:SUB, :] = eg[tm:tm + SUB, :]
        ev[0:SUB, :] = ev[tm:tm + SUB, :]
        if nr:
            @pl.when(jnp.logical_and(pl.program_id(0) == nc - 1, i == t // tm - 1))
            def _():
                _rider_wait(rout, rkinds, *sems)

    wspec = lambda off: pl.BlockSpec((None, d, tc), lambda j, i: (l, 0, j + off))
    vec = lambda r, off: pl.BlockSpec((r, tc), lambda j, i: (0, j + off))
    pair = pl.BlockSpec((2, tm, tc), lambda j, i: (0, i, j))
    return pl.pallas_call(
        body, name=name, grid=(nc, t // tm),
        in_specs=[pl.BlockSpec((tm, d), lambda j, i: (i, 0)), wspec(0), wspec(nc),
                  vec(3, 0), vec(3, nc), vec(1, 0), vec(1, nc)] + rspecs,
        out_specs=[pair, pair, pl.BlockSpec((tm, tc), lambda j, i: (i, j))] + rspecs,
        out_shape=[jax.ShapeDtypeStruct((2, t, f), BF16), jax.ShapeDtypeStruct((2, t, f), BF16),
                   jax.ShapeDtypeStruct((t, f), BF16)] + rshapes,
        input_output_aliases={7 + u: 3 + u for u in range(nr)},
        scratch_shapes=[pltpu.VMEM((tm + SUB, tc), F32), pltpu.VMEM((tm + SUB, tc), F32)] + rsems,
        compiler_params=_cp("arbitrary", "arbitrary"))(xb, w3, w3, cw, cw, cb, cb, *bufs)


def _ffn_down_bwd(dzb, wd3, l, h3, pre3, cw, seq, name):
    t, d = dzb.shape
    f = wd3.shape[1]
    tc = f // FFN_COL_BLOCKS
    tm = min(TM_MM, seq)
    nt = t // tm

    def body(dz_ref, wd_ref, h_ref, pre_ref, cwg_ref, cwv_ref,
             dh_ref, dcwg_ref, dcwv_ref, dcbg_ref, dcbv_ref, da_s, carry):
        i = pl.program_id(1)
        ri = nt - 1 - i

        @pl.when(i == 0)
        def _():
            for r in (dcwg_ref, dcwv_ref, dcbg_ref, dcbv_ref):
                r[...] = jnp.zeros_like(r)

        @pl.when(lax.rem((ri + 1) * tm, seq) == 0)
        def _():
            carry[...] = jnp.zeros_like(carry)

        dz = dz_ref[...]

        def matmul(lo, hi):
            da_s[:, lo:hi] = lax.dot_general(dz, wd_ref[lo:hi, :], (((1,), (1,)), ((), ())),
                                             preferred_element_type=F32)

        def epilogue(lo, hi):
            for c0 in range(lo, hi, LANE):
                cols = slice(c0, c0 + LANE)
                taps = [[ref[k:k + 1, cols] for k in range(3)] for ref in (cwg_ref, cwv_ref)]
                acc = [jnp.zeros((SUB, LANE), F32)] * 8
                for r0 in range(tm - RC, -1, -RC):
                    rows = slice(r0, r0 + RC)
                    da = da_s[rows, cols]
                    gp = pre_ref[0, rows, cols].astype(F32)
                    vp = pre_ref[1, rows, cols].astype(F32)
                    sg = _sigmoid(gp)
                    dpres = (da * vp * (sg * (1.0 + gp * (1.0 - sg))), da * (gp * sg))
                    for half in range(2):
                        w0, w1, w2 = taps[half]
                        dpre = dpres[half]
                        ext = jnp.concatenate([dpre, carry[half, :, cols]], axis=0)
                        u1 = _up(ext, 1, RC)
                        u2 = _up(ext, 2, RC)
                        carry[half, :, cols] = dpre[0:SUB]
                        dh_ref[half, rows, cols] = (w2 * dpre + w1 * u1 + w0 * u2).astype(BF16)
                        hh = h_ref[half, rows, cols].astype(F32)
                        for k, term in enumerate((hh * u2, hh * u1, hh * dpre, dpre)):
                            acc[4 * half + k] = acc[4 * half + k] + _fold(term)
                for half, (dcw_ref, dcb_ref) in enumerate(((dcwg_ref, dcbg_ref), (dcwv_ref, dcbv_ref))):
                    for k in range(3):
                        dcw_ref[k:k + 1, cols] += _colsum(acc[4 * half + k])
                    dcb_ref[:, cols] += _colsum(acc[4 * half + 3])

        blocks = [(lo, min(lo + MXU_COLS, tc)) for lo in range(0, tc, MXU_COLS)]
        matmul(*blocks[0])
        for b, blk in enumerate(blocks):
            if b + 1 < len(blocks):
                matmul(*blocks[b + 1])
            epilogue(*blk)

    nc = FFN_COL_BLOCKS
    pair = pl.BlockSpec((2, tm, tc), lambda j, i: (0, nt - 1 - i, j))
    vec = lambda off: pl.BlockSpec((3, tc), lambda j, i: (0, j + off))
    acc_spec = lambda r: pl.BlockSpec((r, tc), lambda j, i: (0, j))
    return pl.pallas_call(
        body, name=name, grid=(nc, nt),
        in_specs=[pl.BlockSpec((tm, d), lambda j, i: (nt - 1 - i, 0)),
                  pl.BlockSpec((None, tc, d), lambda j, i: (l, j, 0)), pair, pair, vec(0), vec(nc)],
        out_specs=[pair, acc_spec(3), acc_spec(3), acc_spec(1), acc_spec(1)],
        out_shape=[jax.ShapeDtypeStruct((2, t, f), BF16), jax.ShapeDtypeStruct((3, f), F32),
                   jax.ShapeDtypeStruct((3, f), F32), jax.ShapeDtypeStruct((1, f), F32),
                   jax.ShapeDtypeStruct((1, f), F32)],
        scratch_shapes=[pltpu.VMEM((tm, tc), F32), pltpu.VMEM((2, SUB, tc), F32)],
        compiler_params=_cp("arbitrary", "arbitrary"))(dzb, wd3, h3, pre3, cw, cw)


def _lru_gates(xr, wg_ref, bg_ref):
    rs, gs = [], []
    for hd in range(LRU_HEADS):
        xh = xr[:, hd * LRU_BLOCK:(hd + 1) * LRU_BLOCK].astype(BF16)
        gt = jnp.dot(xh, wg_ref[hd], preferred_element_type=F32) + _row(bg_ref, hd)
        rs.append(gt[:, :LRU_BLOCK])
        gs.append(gt[:, LRU_BLOCK:])
    return jnp.concatenate(rs, axis=1), jnp.concatenate(gs, axis=1)


def _lru_coeffs(xr, wg_ref, bg_ref, lam_ref):
    gr, gi = _lru_gates(xr, wg_ref, bg_ref)
    r = _sigmoid(gr)
    ig = _sigmoid(gi)
    sp = _softplus(-lam_ref[...])
    log_a = -LRU_C * r * sp
    a = jnp.exp(log_a)
    mult = jnp.sqrt(_one_minus_exp(2.0 * log_a))
    return r, ig, sp, a, mult


def _lru_fwd(h, cw, cb, wg, bg, lam, seq, name):
    t, r2 = h.shape
    rw = r2 // 2
    ts = min(TS_LRU, seq)
    n8 = ts // SUB

    def body(hg_ref, hr_ref, cw_ref, cb_ref, wg_ref, bg_ref, lam_ref, hs_ref, y_ref,
             a_s, b_s, cconv, cstate):
        i = pl.program_id(0)

        @pl.when(lax.rem(i * ts, seq) == 0)
        def _():
            cconv[...] = jnp.zeros_like(cconv)
            cstate[...] = jnp.zeros_like(cstate)

        rin = hr_ref[...].astype(F32)
        ext = jnp.concatenate([cconv[...], rin], axis=0)
        xr = cb_ref[...]
        for k in range(4):
            xr = xr + _row(cw_ref, k) * _down(ext, 3 - k, SUB)
        cconv[...] = rin[ts - SUB:, :]
        _, ig, _, a, mult = _lru_coeffs(xr, wg_ref, bg_ref, lam_ref)
        a_s[...] = a
        b_s[...] = mult * (ig * xr)
        row = lax.broadcasted_iota(jnp.int32, (SUB, rw), 0)

        def step(j, carry):
            off = pl.multiple_of(j * SUB, SUB)
            a8 = a_s[pl.ds(off, SUB), :]
            b8 = b_s[pl.ds(off, SUB), :]
            for d in (1, 2, 4):
                m = row >= d
                b8 = jnp.where(m, a8 * pltpu.roll(b8, d, 0) + b8, b8)
                a8 = jnp.where(m, a8 * pltpu.roll(a8, d, 0), a8)
            h8 = a8 * carry + b8
            hs_ref[pl.ds(off, SUB), :] = h8
            return _colsum(jnp.where(row == SUB - 1, h8, 0.0))

        cstate[...] = lax.fori_loop(0, n8, step, cstate[...])
        gel, _ = _gelu_and_grad(hg_ref[...].astype(F32))
        y_ref[...] = (hs_ref[...] * gel).astype(BF16)

    full = lambda shp: pl.BlockSpec(shp, lambda i: (0,) * len(shp))
    return pl.pallas_call(
        body, name=name, grid=(t // ts,),
        in_specs=[pl.BlockSpec((ts, rw), lambda i: (i, 0)), pl.BlockSpec((ts, rw), lambda i: (i, 1)),
                  full((4, rw)), full((1, rw)), full(wg.shape), full(bg.shape), full((1, rw))],
        out_specs=[pl.BlockSpec((ts, rw), lambda i: (i, 0)), pl.BlockSpec((ts, rw), lambda i: (i, 0))],
        out_shape=[jax.ShapeDtypeStruct((t, rw), F32), jax.ShapeDtypeStruct((t, rw), BF16)],
        scratch_shapes=[pltpu.VMEM((ts, rw), F32), pltpu.VMEM((ts, rw), F32),
                        pltpu.VMEM((SUB, rw), F32), pltpu.VMEM((1, rw), F32)],
        compiler_params=_cp("arbitrary"))(h, h, cw, cb, wg, bg, lam)


def _lru_bwd(h, hs, dy, cw, cb, wg, bg, lam, seq, name):
    t, r2 = h.shape
    rw = r2 // 2
    ts = min(TS_LRU, seq)
    nt = t // ts
    n8 = ts // SUB
    hp16 = ts // SUB16
    hp8 = ts // SUB

    def body(hg_ref, hr_ref, hrh_ref, hs_ref, hsh_ref, dy_ref, cw_ref, cb_ref, wg_ref, bg_ref, lam_ref,
             dh_ref, dbin_ref, dcw_ref, dcb_ref, dwg_ref, dbg_ref, dlam_ref,
             a_s, g_s, l_s, c_lam, c_a, c_dxr):
        i = pl.program_id(0)
        ri = nt - 1 - i

        @pl.when(i == 0)
        def _():
            for r in (dbin_ref, dcw_ref, dcb_ref, dwg_ref, dbg_ref, dlam_ref):
                r[...] = jnp.zeros_like(r)

        @pl.when(lax.rem((ri + 1) * ts, seq) == 0)
        def _():
            c_lam[...] = jnp.zeros_like(c_lam)
            c_a[...] = jnp.zeros_like(c_a)
            c_dxr[...] = jnp.zeros_like(c_dxr)

        keep = jnp.where(lax.rem(ri * ts, seq) == 0, 0.0, 1.0)
        rin = hr_ref[...].astype(F32)
        ext = jnp.concatenate([hrh_ref[...].astype(F32) * keep, rin], axis=0)
        shifted = [_down(ext, 3 - k, SUB16) for k in range(4)]
        xr = cb_ref[...]
        for k in range(4):
            xr = xr + _row(cw_ref, k) * shifted[k]
        r, ig, sp, a, mult = _lru_coeffs(xr, wg_ref, bg_ref, lam_ref)
        gel, dgel = _gelu_and_grad(hg_ref[...].astype(F32))
        dyf = dy_ref[...].astype(F32)
        hsv = hs_ref[...]
        dg = dyf * hsv * dgel

        a_s[...] = _up(jnp.concatenate([a, c_a[...]], axis=0), 1, ts)
        g_s[...] = dyf * gel
        c_a[...] = a[0:SUB, :]
        row = lax.broadcasted_iota(jnp.int32, (SUB, rw), 0)

        def step(j, carry):
            off = pl.multiple_of((n8 - 1 - j) * SUB, SUB)
            a8 = a_s[pl.ds(off, SUB), :]
            b8 = g_s[pl.ds(off, SUB), :]
            for d in (1, 2, 4):
                m = row < SUB - d
                b8 = jnp.where(m, a8 * pltpu.roll(b8, SUB - d, 0) + b8, b8)
                a8 = jnp.where(m, a8 * pltpu.roll(a8, SUB - d, 0), a8)
            l8 = a8 * carry + b8
            l_s[pl.ds(off, SUB), :] = l8
            return _colsum(jnp.where(row == 0, l8, 0.0))

        c_lam[...] = lax.fori_loop(0, n8, step, c_lam[...])
        lamv = l_s[...]
        hs_prev = _down(jnp.concatenate([hsh_ref[...] * keep, hsv], axis=0), 1, SUB)
        da = lamv * hs_prev
        t1 = lamv * xr
        dmult = t1 * ig
        dig = t1 * mult
        dxr = lamv * mult * ig
        dla = da * a - dmult * (a * a) / mult
        dr = dla * (-LRU_C * sp)
        dlam_ref[...] += _colsum(dla * (-LRU_C) * r) * (-_sigmoid(-lam_ref[...]))
        dgr = dr * r * (1.0 - r)
        dgi = dig * ig * (1.0 - ig)
        parts = []
        for hd in range(LRU_HEADS):
            sl = slice(hd * LRU_BLOCK, (hd + 1) * LRU_BLOCK)
            dgt = jnp.concatenate([dgr[:, sl], dgi[:, sl]], axis=1)
            dbg_ref[hd:hd + 1, :] += _colsum(dgt)
            dgt16 = dgt.astype(BF16)
            parts.append(lax.dot_general(dgt16, wg_ref[hd], (((1,), (1,)), ((), ())),
                                         preferred_element_type=F32))
            dwg_ref[hd] += lax.dot_general(xr[:, sl].astype(BF16), dgt16, (((0,), (0,)), ((), ())),
                                           preferred_element_type=F32)
        dxr = dxr + jnp.concatenate(parts, axis=1)

        dcb_ref[...] += _colsum(dxr)
        for k in range(4):
            dcw_ref[k:k + 1, :] += _colsum(dxr * shifted[k])
        ext2 = jnp.concatenate([dxr, c_dxr[...]], axis=0)
        drb = _row(cw_ref, 3) * dxr
        for k in range(3):
            drb = drb + _row(cw_ref, k) * _up(ext2, 3 - k, ts)
        c_dxr[...] = dxr[0:SUB, :]
        dh_ref[0] = dg.astype(BF16)
        dh_ref[1] = drb.astype(BF16)
        dbin_ref[:, 0:rw] += _colsum(dg)
        dbin_ref[:, rw:] += _colsum(drb)

    rev = lambda c: pl.BlockSpec((ts, rw), lambda i: (nt - 1 - i, c))
    full = lambda shp: pl.BlockSpec(shp, lambda i: (0,) * len(shp))
    nh = LRU_HEADS
    return pl.pallas_call(
        body, name=name, grid=(nt,),
        in_specs=[rev(0), rev(1),
                  pl.BlockSpec((SUB16, rw), lambda i: (jnp.maximum((nt - 1 - i) * hp16 - 1, 0), 1)),
                  rev(0),
                  pl.BlockSpec((SUB, rw), lambda i: (jnp.maximum((nt - 1 - i) * hp8 - 1, 0), 0)),
                  rev(0), full((4, rw)), full((1, rw)), full(wg.shape), full(bg.shape), full((1, rw))],
        out_specs=[pl.BlockSpec((2, ts, rw), lambda i: (0, nt - 1 - i, 0)), full((1, r2)), full((4, rw)),
                   full((1, rw)), full((nh, LRU_BLOCK, 2 * LRU_BLOCK)), full((nh, 2 * LRU_BLOCK)), full((1, rw))],
        out_shape=[jax.ShapeDtypeStruct((2, t, rw), BF16), jax.ShapeDtypeStruct((1, r2), F32),
                   jax.ShapeDtypeStruct((4, rw), F32), jax.ShapeDtypeStruct((1, rw), F32),
                   jax.ShapeDtypeStruct((nh, LRU_BLOCK, 2 * LRU_BLOCK), F32),
                   jax.ShapeDtypeStruct((nh, 2 * LRU_BLOCK), F32), jax.ShapeDtypeStruct((1, rw), F32)],
        scratch_shapes=[pltpu.VMEM((ts, rw), F32), pltpu.VMEM((ts, rw), F32), pltpu.VMEM((ts, rw), F32),
                        pltpu.VMEM((1, rw), F32), pltpu.VMEM((SUB, rw), F32), pltpu.VMEM((SUB, rw), F32)],
        compiler_params=_cp("arbitrary"))(h, h, h, hs, hs, dy, cw, cb, wg, bg, lam)


def _row_tile(rows, cols, mult, elems=ELEMS_PER_BLOCK):
    cap = max(mult, elems // cols)
    best = None
    for cand in range(mult, min(rows, cap) + 1, mult):
        if rows % cand == 0:
            best = cand
    return best if best is not None else rows


def _core_index():
    return lax.axis_index("c").astype(jnp.int32).reshape(1)


def _chip_index():
    return (2 * lax.axis_index("x") + lax.axis_index("y")).astype(jnp.int32).reshape(1)


def _add_pair(p4, r3, name):
    s, _, rows, cols = p4.shape
    tr = _row_tile(rows, cols, SUB16)

    def body(c_ref, a_ref, b_ref, o_ref):
        o_ref[...] = (a_ref[...].astype(F32) + b_ref[...].astype(F32)).astype(o_ref.dtype)

    blk = pl.BlockSpec((None, tr, cols), lambda k, i, c_ref: (k, i, 0))
    return pl.pallas_call(
        body, name=name,
        grid_spec=pltpu.PrefetchScalarGridSpec(
            num_scalar_prefetch=1, grid=(s, rows // tr),
            in_specs=[pl.BlockSpec((None, None, tr, cols), lambda k, i, c_ref: (k, c_ref[0], i, 0)), blk],
            out_specs=blk),
        out_shape=jax.ShapeDtypeStruct((s, rows, cols), p4.dtype),
        compiler_params=_cp("parallel", "parallel"))(_core_index(), p4, r3)


def _add_chips(r, name):
    shape = r.shape[1:]
    r3 = r.reshape(N_CHIPS, -1, shape[-1])
    _, rows, cols = r3.shape
    tr = _row_tile(rows, cols, SUB16)

    def body(r_ref, o_ref):
        s = r_ref[0].astype(F32) + r_ref[1].astype(F32)
        s = s + r_ref[2].astype(F32)
        o_ref[...] = s + r_ref[3].astype(F32)

    out = pl.pallas_call(body, name=name, grid=(rows // tr,),
                         in_specs=[pl.BlockSpec((N_CHIPS, tr, cols), lambda i: (0, i, 0))],
                         out_specs=pl.BlockSpec((tr, cols), lambda i: (i, 0)),
                         out_shape=jax.ShapeDtypeStruct((rows, cols), F32),
                         compiler_params=_cp("parallel"))(r3)
    return out.reshape(shape)


def _adamw(w, g_mine, g_sib, m, v, name):
    shape = w.shape
    rows, cols = g_mine.shape
    flat = [arr.reshape(2, rows, cols) for arr in (w, m, v)]
    tr = _row_tile(rows, cols, SUB)

    def body(c_ref, w_ref, gm_ref, gs_ref, m_ref, v_ref, g_ref, d_ref, mo_ref, vo_ref):
        gg = jnp.where(pl.program_id(0) == c_ref[0], gm_ref[...], gs_ref[...])
        m2 = ADAM_B1 * m_ref[...] + (1.0 - ADAM_B1) * gg
        v2 = ADAM_B2 * v_ref[...] + (1.0 - ADAM_B2) * (gg * gg)
        m_hat = m2 / (1.0 - ADAM_B1 ** ADAM_STEP)
        v_hat = v2 / (1.0 - ADAM_B2 ** ADAM_STEP)
        g_ref[...] = gg
        d_ref[...] = -ADAM_LR * (m_hat / (jnp.sqrt(v_hat) + ADAM_EPS) + ADAM_WD * w_ref[...])
        mo_ref[...] = m2
        vo_ref[...] = v2

    blk = pl.BlockSpec((None, tr, cols), lambda hh, i, c_ref: (hh, i, 0))
    gblk = pl.BlockSpec((tr, cols), lambda hh, i, c_ref: (i, 0))
    outs = pl.pallas_call(
        body, name=name,
        grid_spec=pltpu.PrefetchScalarGridSpec(
            num_scalar_prefetch=1, grid=(2, rows // tr),
            in_specs=[blk, gblk, gblk, blk, blk], out_specs=[blk] * 4),
        out_shape=[jax.ShapeDtypeStruct((2, rows, cols), F32)] * 4,
        compiler_params=_cp("parallel", "parallel"))(_core_index(), flat[0], g_mine, g_sib, flat[1], flat[2])
    return tuple(o.reshape(shape) for o in outs)


def _place(src3, layer, kind, dtype, name):
    _, r, c = src3.shape
    tr = _row_tile(r, c, SUB16)
    in_spec = pl.BlockSpec((None, tr, c), lambda i, my_ref: (layer, i, 0))
    if kind == "col":
        out_spec = pl.BlockSpec((tr, c), lambda i, my_ref: (i, my_ref[0]))
        out_shape = (r, N_CHIPS * c)
    else:
        out_spec = pl.BlockSpec((None, tr, c), lambda i, my_ref: (my_ref[0], i, 0))
        out_shape = (N_CHIPS, r, c)

    def body(my_ref, i_ref, o_ref):
        o_ref[...] = i_ref[...].astype(o_ref.dtype)

    return pl.pallas_call(
        body, name=name,
        grid_spec=pltpu.PrefetchScalarGridSpec(num_scalar_prefetch=1, grid=(r // tr,), in_specs=[in_spec],
                                               out_specs=out_spec),
        out_shape=jax.ShapeDtypeStruct(out_shape, dtype),
        compiler_params=_cp("parallel"))(_chip_index(), src3)


def _full_shape(kind, shard_shape):
    s = tuple(shard_shape)
    if kind == "col":
        return s[:-1] + (N_CHIPS * s[-1],)
    if kind == "row":
        return (s[0], N_CHIPS * s[1]) + s[2:]
    return (N_CHIPS,) + s


def _slot(kind, ref, k, shard_shape):
    if kind == "col":
        n = shard_shape[-1]
        return ref.at[:, :, pl.ds(pl.multiple_of(k * n, 128), n)]
    if kind == "row":
        n = shard_shape[1]
        return ref.at[:, pl.ds(pl.multiple_of(k * n, SUB16), n), :]
    return ref.at[k]


def _half(ref, c, h):
    return ref.at[pl.ds(c * h, h)]


def _position():
    x = lax.axis_index("x")
    y = lax.axis_index("y")
    c = lax.axis_index("c")
    return x, y, c


def _peer_chip(x, y, j):
    tx = 1 - x if j & 2 else x
    ty = 1 - y if j & 1 else y
    return tx, ty


def _remote(src, dst, ssem, rsem, dev):
    return pltpu.make_async_remote_copy(src_ref=src, dst_ref=dst, send_sem=ssem, recv_sem=rsem,
                                        device_id=dev, device_id_type=pl.DeviceIdType.MESH)


_ANY = pl.BlockSpec(memory_space=pl.ANY)


def _unit_view(kind, ref, k):
    if kind == "col":
        n = ref.shape[1] // N_CHIPS
        return ref.at[:, pl.ds(pl.multiple_of(k * n, LANE), n)]
    return ref.at[k]


def _all_gather(placed, kinds):
    nt = len(placed)

    def body(*refs):
        outs = refs[nt:2 * nt]
        ssem, rsem = refs[2 * nt:]
        x, y, c = _position()
        my = 2 * x + y
        sib = (x, y, 1 - c)

        def part(t, k, core):
            view = _unit_view(kinds[t], outs[t], k)
            h = view.shape[0] // 2
            return _half(view, core, h)

        sends, fwds = [], []
        for t in range(nt):
            own = part(t, my, c)
            for j in (1, 2, 3):
                tx, ty = _peer_chip(x, y, j)
                cp = _remote(own, own, ssem.at[6 * t + j - 1], rsem.at[6 * t + j - 1], (tx, ty, c))
                cp.start()
                sends.append(cp)
        for t in range(nt):
            for j in (1, 2, 3):
                tx, ty = _peer_chip(x, y, j)
                got = part(t, 2 * tx + ty, c)
                _remote(got, got, ssem.at[6 * t + j - 1], rsem.at[6 * t + j - 1], sib).wait_recv()
                cp = _remote(got, got, ssem.at[6 * t + 2 + j], rsem.at[6 * t + 2 + j], sib)
                cp.start()
                fwds.append(cp)
        for t in range(nt):
            for j in (1, 2, 3):
                tx, ty = _peer_chip(x, y, j)
                other = part(t, 2 * tx + ty, 1 - c)
                _remote(other, other, ssem.at[6 * t + 2 + j], rsem.at[6 * t + 2 + j], sib).wait_recv()
        for cp in sends + fwds:
            cp.wait_send()

    return pl.pallas_call(
        body, name="all_gather", in_specs=[_ANY] * nt, out_specs=[_ANY] * nt,
        out_shape=[jax.ShapeDtypeStruct(p.shape, p.dtype) for p in placed],
        input_output_aliases={t: t for t in range(nt)},
        scratch_shapes=[pltpu.SemaphoreType.DMA((6 * nt,)), pltpu.SemaphoreType.DMA((6 * nt,))],
    )(*placed)


def _rider_start(refs, kinds, ssem, rsem):
    x, y, c = _position()
    my = 2 * x + y
    for u, (ref, kind) in enumerate(zip(refs, kinds)):
        own = _unit_view(kind, ref, my)
        for j in (1, 2, 3):
            tx, ty = _peer_chip(x, y, j)
            _remote(own, own, ssem.at[3 * u + j - 1], rsem.at[3 * u + j - 1], (tx, ty, c)).start()


def _rider_wait(refs, kinds, ssem, rsem):
    x, y, c = _position()
    for u, (ref, kind) in enumerate(zip(refs, kinds)):
        for j in (1, 2, 3):
            tx, ty = _peer_chip(x, y, j)
            got = _unit_view(kind, ref, 2 * tx + ty)
            _remote(got, got, ssem.at[3 * u + j - 1], rsem.at[3 * u + j - 1], (tx, ty, c)).wait()


def _rider_args(rider):
    bufs = [b for b, _ in rider]
    kinds = [k for _, k in rider]
    n = len(bufs)
    sems = [pltpu.SemaphoreType.DMA((3 * n,)), pltpu.SemaphoreType.DMA((3 * n,))] if n else []
    return bufs, kinds, [_ANY] * n, [jax.ShapeDtypeStruct(b.shape, b.dtype) for b in bufs], sems


def _d2d_stream(src4, other_half, name):
    s, _, rows, cols = src4.shape
    tr = _row_tile(rows, cols, SUB16, STREAM_ELEMS_PER_BLOCK)
    nblk = rows // tr

    nh = src4.shape[1]

    def body(c_ref, src_ref, dst_ref, ssem, rsem):
        k = pl.program_id(0)
        i = pl.program_id(1)
        x, y, c = _position()
        sib = (x, y, 1 - c)
        blk = dst_ref.at[pl.ds(pl.multiple_of((k * nblk + i) * tr, SUB16), tr)]
        cp = _remote(src_ref, blk, ssem, rsem, sib)
        cp.start()
        cp.wait_send()

        @pl.when(jnp.logical_and(k == s - 1, i == nblk - 1))
        def _():
            _remote(dst_ref, dst_ref, ssem, rsem, sib).wait_recv()

    if other_half:
        src_map = lambda k, i, c_ref: ((k * nh + 1 - c_ref[0]) * nblk + i, 0)
    else:
        src_map = lambda k, i, c_ref: (k * nh * nblk + i, 0)
    out = pl.pallas_call(
        body, name=name,
        grid_spec=pltpu.PrefetchScalarGridSpec(
            num_scalar_prefetch=1, grid=(s, nblk),
            in_specs=[pl.BlockSpec((tr, cols), src_map)], out_specs=_ANY,
            scratch_shapes=[pltpu.SemaphoreType.DMA, pltpu.SemaphoreType.DMA]),
        out_shape=jax.ShapeDtypeStruct((s * rows, cols), src4.dtype),
        compiler_params=_cp("arbitrary", "arbitrary"))(_core_index(), src4.reshape(s * nh * rows, cols))
    return out.reshape(s, rows, cols)


def _rs_all_to_all(chip_sums, kinds, shard_shapes):
    nt = len(chip_sums)

    def half_shard(shape):
        return (shape[0] // 2,) + tuple(shape[1:])

    def body(*refs):
        ins, outs = refs[:nt], refs[nt:2 * nt]
        ssem, rsem, lsem = refs[2 * nt:]
        x, y, c = _position()
        my = 2 * x + y
        cps = []
        for t in range(nt):
            hs = half_shard(shard_shapes[t])
            cp = pltpu.make_async_copy(_slot(kinds[t], ins[t], my, hs), outs[t].at[my], lsem.at[t])
            cp.start()
            cps.append(cp)
            for j in (1, 2, 3):
                tx, ty = _peer_chip(x, y, j)
                cp = _remote(_slot(kinds[t], ins[t], 2 * tx + ty, hs), outs[t].at[my], ssem.at[3 * t + j - 1],
                             rsem.at[3 * t + j - 1], (tx, ty, c))
                cp.start()
                cps.append(cp)
        for cp in cps:
            cp.wait()

    return pl.pallas_call(
        body, name="rs_all_to_all", in_specs=[_ANY] * nt, out_specs=[_ANY] * nt,
        out_shape=[jax.ShapeDtypeStruct((N_CHIPS,) + half_shard(s), a.dtype)
                   for s, a in zip(shard_shapes, chip_sums)],
        scratch_shapes=[pltpu.SemaphoreType.DMA((3 * nt,)), pltpu.SemaphoreType.DMA((3 * nt,)),
                        pltpu.SemaphoreType.DMA((nt,))],
    )(*chip_sums)


SMALL = (("sc_conv_w", True), ("sc_conv_b", False), ("lru_b_in", True), ("lru_conv_w", True),
         ("lru_conv_b", True), ("lru_b_gate", True), ("lru_lambda", True), ("ffn_conv_w", True),
         ("ffn_conv_b", False), ("ln_g", True), ("ln_b", True))
PACK_ROW_MULT = 2 * SUB16


def _pack_rows(shapes):
    n = sum(math.prod(shapes[name]) for name, _ in SMALL)
    rows = -(-n // 128)
    return -(-rows // PACK_ROW_MULT) * PACK_ROW_MULT


def _pack_local(vals, shapes):
    flat = jnp.concatenate([vals[name].reshape(-1) for name, _ in SMALL])
    rows = _pack_rows(shapes)
    return jnp.pad(flat, (0, rows * 128 - flat.shape[0])).reshape(rows, 128)


def _unpack_local(pack, shapes):
    flat = pack.reshape(-1)
    out, off = {}, 0
    for name, _ in SMALL:
        n = math.prod(shapes[name])
        out[name] = flat[off:off + n].reshape(shapes[name])
        off += n
    return out


def _pack_slots(fulls, shapes):
    parts = []
    for name, sharded in SMALL:
        v = fulls[name]
        if sharded:
            ns = shapes[name][-1]
            v = jnp.moveaxis(v.reshape(v.shape[:-1] + (N_CHIPS, ns)), -2, 0).reshape(N_CHIPS, -1)
        else:
            v = jnp.broadcast_to(v.reshape(1, -1), (N_CHIPS, v.size))
        parts.append(v)
    flat = jnp.concatenate(parts, axis=1)
    rows = _pack_rows(shapes)
    return jnp.pad(flat, ((0, 0), (0, rows * 128 - flat.shape[1]))).reshape(N_CHIPS, rows, 128)


def _unpack_slots(packs, shapes):
    flat = packs.reshape(N_CHIPS, -1)
    out, off = {}, 0
    for name, sharded in SMALL:
        n = math.prod(shapes[name])
        if sharded:
            seg = flat[:, off:off + n].reshape((N_CHIPS,) + tuple(shapes[name]))
            seg = jnp.moveaxis(seg, 0, -2)
            out[name] = seg.reshape(seg.shape[:-2] + (N_CHIPS * shapes[name][-1],))
        off += n
    return out


WEIGHTS = ("sc_w_in", "sc_conv_w", "sc_conv_b", "sc_w_out", "lru_w_in", "lru_b_in", "lru_conv_w", "lru_conv_b",
           "lru_w_gate", "lru_b_gate", "lru_lambda", "lru_w_out", "ffn_w_up", "ffn_conv_w", "ffn_conv_b",
           "ffn_w_down", "ln_g", "ln_b")
GATHER_KIND = {"sc_w_in": "col", "sc_w_out": "lead", "lru_w_in": "col", "lru_w_out": "lead", "ffn_w_up": "col",
               "ffn_w_down": "lead"}
BIG = (("sc_w_in", "col"), ("sc_w_out", "row"), ("lru_w_in", "col"), ("lru_w_gate", "lead"),
       ("lru_w_out", "row"), ("ffn_w_up", "col"), ("ffn_w_down", "row"))


def kernel(x, sc_w_in, sc_conv_w, sc_conv_b, sc_w_out, lru_w_in, lru_b_in, lru_conv_w, lru_conv_b, lru_w_gate, lru_b_gate, lru_lambda, lru_w_out, ffn_w_up, ffn_conv_w, ffn_conv_b, ffn_w_down, ln_g, ln_b, loss_target, m_sc_w_in, m_sc_conv_w, m_sc_conv_b, m_sc_w_out, m_lru_w_in, m_lru_b_in, m_lru_conv_w, m_lru_conv_b, m_lru_w_gate, m_lru_b_gate, m_lru_lambda, m_lru_w_out, m_ffn_w_up, m_ffn_conv_w, m_ffn_conv_b, m_ffn_w_down, m_ln_g, m_ln_b, v_sc_w_in, v_sc_conv_w, v_sc_conv_b, v_sc_w_out, v_lru_w_in, v_lru_b_in, v_lru_conv_w, v_lru_conv_b, v_lru_w_gate, v_lru_b_gate, v_lru_lambda, v_lru_w_out, v_ffn_w_up, v_ffn_conv_w, v_ffn_conv_b, v_ffn_w_down, v_ln_g, v_ln_b):
    w = dict(zip(WEIGHTS, (sc_w_in, sc_conv_w, sc_conv_b, sc_w_out, lru_w_in, lru_b_in, lru_conv_w, lru_conv_b,
                           lru_w_gate, lru_b_gate, lru_lambda, lru_w_out, ffn_w_up, ffn_conv_w, ffn_conv_b,
                           ffn_w_down, ln_g, ln_b)))
    mom = dict(zip(WEIGHTS, (m_sc_w_in, m_sc_conv_w, m_sc_conv_b, m_sc_w_out, m_lru_w_in, m_lru_b_in, m_lru_conv_w,
                             m_lru_conv_b, m_lru_w_gate, m_lru_b_gate, m_lru_lambda, m_lru_w_out, m_ffn_w_up,
                             m_ffn_conv_w, m_ffn_conv_b, m_ffn_w_down, m_ln_g, m_ln_b)))
    vel = dict(zip(WEIGHTS, (v_sc_w_in, v_sc_conv_w, v_sc_conv_b, v_sc_w_out, v_lru_w_in, v_lru_b_in, v_lru_conv_w,
                             v_lru_conv_b, v_lru_w_gate, v_lru_b_gate, v_lru_lambda, v_lru_w_out, v_ffn_w_up,
                             v_ffn_conv_w, v_ffn_conv_b, v_ffn_w_down, v_ln_g, v_ln_b)))
    bd, seq, d = x.shape
    t = bd * seq
    small_shapes = {name: w[name].shape for name, _ in SMALL}
    big_names = [n for n, _ in BIG]
    kinds = [k for _, k in BIG] + ["lead"]
    shard_shapes = [w[n].shape for n in big_names]

    w_pack = _pack_local(w, small_shapes)
    rs_shapes = shard_shapes + [w_pack.shape]
    gate_shape = w["lru_w_gate"].shape
    bufs = {(n, l): (_place(w[n], l, k, BF16, "place_w"), k)
            for n, k in GATHER_KIND.items() for l in range(w[n].shape[0])}
    bufs["gate"] = (_place(w["lru_w_gate"].reshape(1, -1, gate_shape[-1]), 0, "lead", BF16, "place_w"), "lead")
    bufs["pack"] = (_place(w_pack[None], 0, "lead", F32, "place_w"), "lead")

    def layer_keys(i):
        mixer = ("sc_w_in", "sc_w_out") if i % 2 == 0 else ("lru_w_in", "lru_w_out")
        return [(mixer[0], i // 2), (mixer[1], i // 2)], [("ffn_w_up", i), ("ffn_w_down", i)]

    def gathered(keys, arrays):
        for key, arr in zip(keys, arrays):
            bufs[key] = (arr, bufs[key][1])

    def wt(name, l):
        arr = bufs[(name, l)][0]
        return arr.reshape(1, -1, arr.shape[-1])

    first = layer_keys(0)[0] + layer_keys(0)[1] + ["gate", "pack"]
    gathered(first, _all_gather([bufs[k][0] for k in first], [bufs[k][1] for k in first]))
    full = _unpack_slots(bufs["pack"][0], small_shapes)
    full["sc_conv_b"] = sc_conv_b
    full["ffn_conv_b"] = ffn_conv_b
    wg_full = jnp.moveaxis(bufs["gate"][0].reshape((N_CHIPS,) + gate_shape), 0, -2)
    wg_full = wg_full.reshape(wg_full.shape[:-2] + (2 * LRU_BLOCK,))
    f = N_CHIPS * w["ffn_w_down"].shape[1]
    rw = N_CHIPS * w["lru_w_out"].shape[1]

    x0 = x.reshape(t, d)
    xb = x0.astype(BF16)
    cur, cur_b = x0, xb
    saved = []

    for i in range(DEPTH):
        j = i // 2
        s = {"xb": cur_b}
        if i % 2 == 0:
            h = _mm_nn(cur_b, wt("sc_w_in", j), 0, None, 3 * d, "sc_in")
            q = _sc_fwd(h, full["sc_conv_w"][j], full["sc_conv_b"][j][None], seq, "sc_fwd")
            z1, x1, x1b = _mm_nn_ln(q, wt("sc_w_out", j), 0, cur, full["ln_g"][i, 0][None], full["ln_b"][i, 0][None],
                                    "sc_out_ln")
        else:
            h = _mm_nn(cur_b, wt("lru_w_in", j), 0, full["lru_b_in"][j][None], 2 * rw, "lru_in")
            hs, q = _lru_fwd(h, full["lru_conv_w"][j], full["lru_conv_b"][j][None], wg_full[j],
                             full["lru_b_gate"][j], full["lru_lambda"][j][None], seq, "lru_fwd")
            s["hs"] = hs
            z1, x1, x1b = _mm_nn_ln(q, wt("lru_w_out", j), 0, cur, full["ln_g"][i, 0][None],
                                    full["ln_b"][i, 0][None], "lru_out_ln")
        s.update(h=h, q=q, z1=z1, x1b=x1b)
        mixer_next, ffn_next = layer_keys(i + 1) if i + 1 < DEPTH else ([], [])
        h3, pre3, act, *arrived = _ffn_up(x1b, wt("ffn_w_up", i), 0, full["ffn_conv_w"][i],
                                          full["ffn_conv_b"][i][None], seq, "ffn_up",
                                          rider=[bufs[k] for k in ffn_next])
        gathered(ffn_next, arrived)
        z2, x2, x2b, *arrived = _mm_nn_ln(act, wt("ffn_w_down", i), 0, x1, full["ln_g"][i, 1][None],
                                          full["ln_b"][i, 1][None], "ffn_down_ln",
                                          rider=[bufs[k] for k in mixer_next])
        gathered(mixer_next, arrived)
        s.update(h3=h3, pre3=pre3, act=act, z2=z2)
        saved.append(s)
        cur, cur_b = x2, x2b

    dcur, loss_parts = _loss_bwd(cur, loss_target.reshape(t, d))
    loss = lax.psum(jnp.sum(loss_parts), MESH_AXES)

    gp = {n: [None] * w[n].shape[0] for n in WEIGHTS}
    for i in reversed(range(DEPTH)):
        j = i // 2
        s = saved[i]
        dz2, dz2b, dg, db = _ln_bwd(dcur, s["z2"], full["ln_g"][i, 1][None], "ln_bwd")
        gp["ln_g"][i] = [None, dg[0]]
        gp["ln_b"][i] = [None, db[0]]
        gp["ffn_w_down"][i] = _mm_tn(s["act"], dz2b[None], f // 2, d, "ffn_down_dw")
        dh3, dcwg, dcwv, dcbg, dcbv = _ffn_down_bwd(dz2b, wt("ffn_w_down", i), 0, s["h3"], s["pre3"],
                                                    full["ffn_conv_w"][i], seq, "ffn_down_bwd")
        gp["ffn_conv_w"][i] = jnp.concatenate([dcwg, dcwv], axis=1)
        gp["ffn_conv_b"][i] = jnp.concatenate([dcbg[0], dcbv[0]])
        dx1 = _mm_nt_res(dh3, wt("ffn_w_up", i), 0, dz2, f // 2, "ffn_up_dx")
        gp["ffn_w_up"][i] = _mm_tn(s["x1b"], dh3, d, f // 2, "ffn_up_dw")
        dz1, dz1b, dg, db = _ln_bwd(dx1, s["z1"], full["ln_g"][i, 0][None], "ln_bwd")
        gp["ln_g"][i][0] = dg[0]
        gp["ln_b"][i][0] = db[0]
        gp["ln_g"][i] = jnp.stack(gp["ln_g"][i])
        gp["ln_b"][i] = jnp.stack(gp["ln_b"][i])
        if i % 2 == 0:
            dq = _mm_nt(dz1b, wt("sc_w_out", j), 0, d, "sc_out_dx")
            gp["sc_w_out"][j] = _mm_tn(s["q"], dz1b[None], d, d, "sc_out_dw")
            dh3, dcw, dcb = _sc_bwd(s["h"], dq, full["sc_conv_w"][j], full["sc_conv_b"][j][None], seq, "sc_bwd")
            gp["sc_conv_w"][j] = dcw
            gp["sc_conv_b"][j] = dcb[0]
            dcur = _mm_nt_res(dh3, wt("sc_w_in", j), 0, dz1, d, "sc_in_dx")
            gp["sc_w_in"][j] = _mm_tn(s["xb"], dh3, d, d, "sc_in_dw")
        else:
            dq = _mm_nt(dz1b, wt("lru_w_out", j), 0, rw, "lru_out_dx")
            gp["lru_w_out"][j] = _mm_tn(s["q"], dz1b[None], rw, d, "lru_out_dw")
            dh3, dbin, dcw, dcb, dwg, dbg, dlam = _lru_bwd(
                s["h"], s["hs"], dq, full["lru_conv_w"][j], full["lru_conv_b"][j][None], wg_full[j],
                full["lru_b_gate"][j], full["lru_lambda"][j][None], seq, "lru_bwd")
            gp["lru_b_in"][j] = dbin[0]
            gp["lru_conv_w"][j] = dcw
            gp["lru_conv_b"][j] = dcb[0]
            gp["lru_w_gate"][j] = dwg
            gp["lru_b_gate"][j] = dbg
            gp["lru_lambda"][j] = dlam[0]
            dcur = _mm_nt_res(dh3, wt("lru_w_in", j), 0, dz1, rw, "lru_in_dx")
            gp["lru_w_in"][j] = _mm_tn(s["xb"], dh3, d, rw, "lru_in_dw")
    grad_x = dcur.reshape(bd, seq, d)
    gp = {n: jnp.stack(v) for n, v in gp.items()}

    ns_gate = w["lru_w_gate"].shape[-1]
    gate = gp["lru_w_gate"]
    gate = jnp.moveaxis(gate.reshape(gate.shape[:-1] + (N_CHIPS, ns_gate)), -2, 0).astype(BF16)
    partials = [gate if n == "lru_w_gate" else gp[n] for n in big_names]
    partials.append(_pack_slots(gp, small_shapes))
    chip_sums = []
    for p, kind in zip(partials, kinds):
        lead = kind == "lead"
        p4 = p.reshape(N_CHIPS if lead else 1, 2, -1, p.shape[-1])
        from_sibling = _d2d_stream(p4, True, "rs_swap")
        half_full = list(p.shape)
        half_full[1 if lead else 0] //= 2
        chip_sums.append(_add_pair(p4, from_sibling, "rs_add_pair").reshape(half_full))
    arrived = _rs_all_to_all(chip_sums, kinds, rs_shapes)

    g_out, d_out, m_out, v_out = {}, {}, {}, {}
    packs = {}
    for n, r in zip(big_names + ["pack"], arrived):
        g_mine = _add_chips(r, "rs_add_chips").reshape(-1, r.shape[-1])
        g_sib = _d2d_stream(g_mine[None, None], False, "rs_share")[0]
        if n == "pack":
            packs = _adamw(w_pack, g_mine, g_sib, _pack_local(mom, small_shapes), _pack_local(vel, small_shapes),
                           "adamw")
        else:
            g_out[n], d_out[n], m_out[n], v_out[n] = _adamw(w[n], g_mine, g_sib, mom[n], vel[n], "adamw")
    for dst, pack in zip((g_out, d_out, m_out, v_out), packs):
        dst.update(_unpack_local(pack, small_shapes))

    return (loss, grad_x, *[g_out[n] for n in WEIGHTS], *[d_out[n] for n in WEIGHTS],
            *[m_out[n] for n in WEIGHTS], *[v_out[n] for n in WEIGHTS])
```

```python
import math

import jax
import jax.numpy as jnp
from jax import lax
from jax.experimental import pallas as pl
from jax.experimental.pallas import tpu as pltpu

F32 = jnp.float32
BF16 = jnp.bfloat16

DEPTH = 4
LRU_HEADS = 10
LRU_BLOCK = 128
LRU_C = 8.0
LN_EPS = 1e-5
ALPHA = (2.0 * DEPTH) ** 0.25
ADAM_LR, ADAM_B1, ADAM_B2, ADAM_EPS, ADAM_WD, ADAM_STEP = 0.001, 0.9, 0.999, 1e-08, 0.01, 10
N_CHIPS = 4
MESH_AXES = ("x", "y", "c")

VMEM_LIMIT_BYTES = 48 * 1024 * 1024
TM_MM = 512
TM_RES = 1024
TT_MM = 512
TM_SC = 256
FFN_COL_BLOCKS = 2
RC = 64
LANE = 128
MXU_COLS = 256
TS_LRU = 128
TM_LN = 512
ELEMS_PER_BLOCK = 256 * 1024
STREAM_ELEMS_PER_BLOCK = 1024 * 1024
SUB = 8
SUB16 = 16


def _cp(*sem):
    return pltpu.CompilerParams(dimension_semantics=sem, vmem_limit_bytes=VMEM_LIMIT_BYTES)


def _sigmoid(v):
    return 1.0 / (1.0 + jnp.exp(-v))


def _softplus(v):
    e = jnp.exp(-jnp.abs(v))
    log1p = jnp.where(e < 1e-3, e * (1.0 - e * (0.5 - e * (1.0 / 3.0))), jnp.log(1.0 + e))
    return jnp.maximum(v, 0.0) + log1p


def _one_minus_exp(v):
    series = -v * (1.0 + v * (0.5 + v * (1.0 / 6.0 + v * (1.0 / 24.0))))
    return jnp.where(v > -0.02, series, 1.0 - jnp.exp(v))


def _gelu_and_grad(v):
    k = math.sqrt(2.0 / math.pi)
    t = jnp.tanh(k * (v + 0.044715 * v * v * v))
    val = 0.5 * v * (1.0 + t)
    grad = 0.5 * (1.0 + t) + 0.5 * v * (1.0 - t * t) * k * (1.0 + 3.0 * 0.044715 * v * v)
    return val, grad


def _down(ext, k, n_head):
    if k:
        ext = pltpu.roll(ext, k, 0)
    return ext[n_head:]


def _up(ext, k, n):
    if k:
        ext = pltpu.roll(ext, ext.shape[0] - k, 0)
    return ext[:n]


def _row(ref, k):
    return ref[k:k + 1, :]


def _colsum(v):
    return jnp.sum(v, axis=0, keepdims=True)


def _mm_nn(a, w3, l, bias, tn, name):
    m, k = a.shape
    n = w3.shape[2]
    tm = min(TM_MM, m)
    has_bias = bias is not None

    def body(*refs):
        if has_bias:
            a_ref, w_ref, b_ref, o_ref = refs
        else:
            a_ref, w_ref, o_ref = refs
        acc = jnp.dot(a_ref[...], w_ref[...], preferred_element_type=F32)
        if has_bias:
            acc = acc + b_ref[...]
        o_ref[...] = acc.astype(o_ref.dtype)

    in_specs = [pl.BlockSpec((tm, k), lambda i, j: (i, 0)),
                pl.BlockSpec((None, k, tn), lambda i, j: (l, 0, j))]
    args = [a, w3]
    if has_bias:
        in_specs.append(pl.BlockSpec((1, tn), lambda i, j: (0, j)))
        args.append(bias)
    return pl.pallas_call(
        body, name=name, grid=(m // tm, n // tn), in_specs=in_specs,
        out_specs=pl.BlockSpec((tm, tn), lambda i, j: (i, j)),
        out_shape=jax.ShapeDtypeStruct((m, n), BF16),
        compiler_params=_cp("parallel", "arbitrary"))(*args)


def _mm_nn_ln(a, w3, l, xres, g, b, name, rider=()):
    m, k = a.shape
    n = w3.shape[2]
    tm = min(TM_MM, m)
    bufs, rkinds, rspecs, rshapes, rsems = _rider_args(rider)
    nr = len(bufs)

    def body(*refs):
        a_ref, w_ref, x_ref, g_ref, b_ref = refs[:5]
        z_ref, xn_ref, xb_ref = refs[5 + nr:8 + nr]
        rout, sems = refs[8 + nr:8 + 2 * nr], refs[8 + 2 * nr:]
        if nr:
            @pl.when(pl.program_id(0) == 0)
            def _():
                _rider_start(rout, rkinds, *sems)

        y = jnp.dot(a_ref[...], w_ref[...], preferred_element_type=F32)
        z = ALPHA * x_ref[...] + y
        mu = jnp.mean(z, axis=-1, keepdims=True)
        zc = z - mu
        var = jnp.mean(zc * zc, axis=-1, keepdims=True)
        xn = zc * lax.rsqrt(var + LN_EPS) * g_ref[...] + b_ref[...]
        z_ref[...] = z
        xn_ref[...] = xn
        xb_ref[...] = xn.astype(BF16)
        if nr:
            @pl.when(pl.program_id(0) == m // tm - 1)
            def _():
                _rider_wait(rout, rkinds, *sems)

    row = pl.BlockSpec((tm, n), lambda i: (i, 0))
    vec = pl.BlockSpec((1, n), lambda i: (0, 0))
    return pl.pallas_call(
        body, name=name, grid=(m // tm,),
        in_specs=[pl.BlockSpec((tm, k), lambda i: (i, 0)),
                  pl.BlockSpec((None, k, n), lambda i: (l, 0, 0)), row, vec, vec] + rspecs,
        out_specs=[row, row, row] + rspecs,
        out_shape=[jax.ShapeDtypeStruct((m, n), F32), jax.ShapeDtypeStruct((m, n), F32),
                   jax.ShapeDtypeStruct((m, n), BF16)] + rshapes,
        input_output_aliases={5 + u: 3 + u for u in range(nr)},
        scratch_shapes=rsems,
        compiler_params=_cp("arbitrary"))(a, w3, xres, g, b, *bufs)


def _mm_nt(a, w3, l, tk, name):
    m, n = a.shape
    kd = w3.shape[1]
    tm = min(TM_MM, m)

    def body(a_ref, w_ref, o_ref):
        o_ref[...] = lax.dot_general(a_ref[...], w_ref[...], (((1,), (1,)), ((), ())),
                                     preferred_element_type=F32).astype(o_ref.dtype)

    return pl.pallas_call(
        body, name=name, grid=(m // tm, kd // tk),
        in_specs=[pl.BlockSpec((tm, n), lambda i, j: (i, 0)),
                  pl.BlockSpec((None, tk, n), lambda i, j: (l, j, 0))],
        out_specs=pl.BlockSpec((tm, tk), lambda i, j: (i, j)),
        out_shape=jax.ShapeDtypeStruct((m, kd), BF16),
        compiler_params=_cp("parallel", "arbitrary"))(a, w3)


def _mm_nt_res(dh3, w3, l, dz, tc, name):
    g, m, cg = dh3.shape
    kd = w3.shape[1]
    ncg = cg // tc
    nk = g * ncg
    tm = min(TM_RES, m)

    def body(a_ref, w_ref, dz_ref, o_ref, acc):
        k = pl.program_id(1)

        @pl.when(k == 0)
        def _():
            acc[...] = ALPHA * dz_ref[...]

        acc[...] += lax.dot_general(a_ref[...], w_ref[...], (((1,), (1,)), ((), ())),
                                    preferred_element_type=F32)

        @pl.when(k == nk - 1)
        def _():
            o_ref[...] = acc[...]

    return pl.pallas_call(
        body, name=name, grid=(m // tm, nk),
        in_specs=[pl.BlockSpec((None, tm, tc), lambda i, k: (k // ncg, i, k % ncg)),
                  pl.BlockSpec((None, kd, tc), lambda i, k: (l, 0, k)),
                  pl.BlockSpec((tm, kd), lambda i, k: (i, 0))],
        out_specs=pl.BlockSpec((tm, kd), lambda i, k: (i, 0)),
        out_shape=jax.ShapeDtypeStruct((m, kd), F32),
        scratch_shapes=[pltpu.VMEM((tm, kd), F32)],
        compiler_params=_cp("parallel", "arbitrary"))(dh3, w3, dz)


def _mm_tn(a, b3, tka, tnb, name):
    t, ka = a.shape
    g, _, cg = b3.shape
    ncg = cg // tnb
    tt = min(TT_MM, t)
    nt = t // tt

    def body(a_ref, b_ref, o_ref, acc):
        s = pl.program_id(2)

        @pl.when(s == 0)
        def _():
            acc[...] = jnp.zeros_like(acc)

        acc[...] += lax.dot_general(a_ref[...], b_ref[...], (((0,), (0,)), ((), ())),
                                    preferred_element_type=F32)

        @pl.when(s == nt - 1)
        def _():
            o_ref[...] = acc[...].astype(o_ref.dtype)

    return pl.pallas_call(
        body, name=name, grid=(ka // tka, g * ncg, nt),
        in_specs=[pl.BlockSpec((tt, tka), lambda i, j, s: (s, i)),
                  pl.BlockSpec((None, tt, tnb), lambda i, j, s: (j // ncg, s, j % ncg))],
        out_specs=pl.BlockSpec((tka, tnb), lambda i, j, s: (i, j)),
        out_shape=jax.ShapeDtypeStruct((ka, g * cg), BF16),
        scratch_shapes=[pltpu.VMEM((tka, tnb), F32)],
        compiler_params=_cp("parallel", "parallel", "arbitrary"))(a, b3)


def _ln_bwd(dxn, z, g, name):
    m, d = z.shape
    tm = min(TM_LN, m)

    def body(dx_ref, z_ref, g_ref, dz_ref, dzb_ref, dg_ref, db_ref):
        @pl.when(pl.program_id(0) == 0)
        def _():
            dg_ref[...] = jnp.zeros_like(dg_ref)
            db_ref[...] = jnp.zeros_like(db_ref)

        zz = z_ref[...]
        dx = dx_ref[...]
        mu = jnp.mean(zz, axis=-1, keepdims=True)
        zc = zz - mu
        var = jnp.mean(zc * zc, axis=-1, keepdims=True)
        rstd = lax.rsqrt(var + LN_EPS)
        xh = zc * rstd
        dg_ref[...] += _colsum(dx * xh)
        db_ref[...] += _colsum(dx)
        dxh = dx * g_ref[...]
        m1 = jnp.mean(dxh, axis=-1, keepdims=True)
        m2 = jnp.mean(dxh * xh, axis=-1, keepdims=True)
        dz = rstd * (dxh - m1 - xh * m2)
        dz_ref[...] = dz
        dzb_ref[...] = dz.astype(BF16)

    row = pl.BlockSpec((tm, d), lambda i: (i, 0))
    vec = pl.BlockSpec((1, d), lambda i: (0, 0))
    return pl.pallas_call(
        body, name=name, grid=(m // tm,), in_specs=[row, row, vec],
        out_specs=[row, row, vec, vec],
        out_shape=[jax.ShapeDtypeStruct((m, d), F32), jax.ShapeDtypeStruct((m, d), BF16),
                   jax.ShapeDtypeStruct((1, d), F32), jax.ShapeDtypeStruct((1, d), F32)],
        compiler_params=_cp("arbitrary"))(dxn, z, g)


def _loss_bwd(y, target):
    m, d = y.shape
    tm = min(TM_LN, m)

    def body(y_ref, t_ref, dy_ref, ls_ref):
        @pl.when(pl.program_id(0) == 0)
        def _():
            ls_ref[...] = jnp.zeros_like(ls_ref)

        e = y_ref[...] - t_ref[...]
        dy_ref[...] = e * (1.0 / d)
        ls_ref[...] += _colsum(e * e) * (0.5 / d)

    row = pl.BlockSpec((tm, d), lambda i: (i, 0))
    return pl.pallas_call(
        body, name="loss_bwd", grid=(m // tm,), in_specs=[row, row],
        out_specs=[row, pl.BlockSpec((1, d), lambda i: (0, 0))],
        out_shape=[jax.ShapeDtypeStruct((m, d), F32), jax.ShapeDtypeStruct((1, d), F32)],
        compiler_params=_cp("arbitrary"))(y, target)


def _sc_fwd(h, cw, cb, seq, name):
    t, d3 = h.shape
    d = d3 // 3
    tm = min(TM_SC, seq)

    def body(hb_ref, hc_ref, hv_ref, cw_ref, cb_ref, q_ref, carry):
        i = pl.program_id(0)

        @pl.when(lax.rem(i * tm, seq) == 0)
        def _():
            carry[...] = jnp.zeros_like(carry)

        p = hc_ref[...].astype(F32) * hv_ref[...].astype(F32)
        ext = jnp.concatenate([carry[...], p], axis=0)
        u = cb_ref[...] + _row(cw_ref, 0) * _down(ext, 2, SUB) + _row(cw_ref, 1) * _down(ext, 1, SUB) \
            + _row(cw_ref, 2) * p
        q_ref[...] = (hb_ref[...].astype(F32) * u).astype(BF16)
        carry[...] = p[tm - SUB:, :]

    blk = lambda c: pl.BlockSpec((tm, d), lambda i: (i, c))
    return pl.pallas_call(
        body, name=name, grid=(t // tm,),
        in_specs=[blk(0), blk(1), blk(2), pl.BlockSpec((3, d), lambda i: (0, 0)),
                  pl.BlockSpec((1, d), lambda i: (0, 0))],
        out_specs=pl.BlockSpec((tm, d), lambda i: (i, 0)),
        out_shape=jax.ShapeDtypeStruct((t, d), BF16),
        scratch_shapes=[pltpu.VMEM((SUB, d), F32)],
        compiler_params=_cp("arbitrary"))(h, h, h, cw, cb)


def _sc_bwd(h, dq, cw, cb, seq, name):
    t, d3 = h.shape
    d = d3 // 3
    tm = min(TM_SC, seq)
    nt = t // tm
    hpt = tm // SUB16

    def body(hb_ref, hc_ref, hv_ref, hch_ref, hvh_ref, dq_ref, cw_ref, cb_ref,
             dh_ref, dcw_ref, dcb_ref, carry):
        i = pl.program_id(0)
        ri = nt - 1 - i

        @pl.when(i == 0)
        def _():
            dcw_ref[...] = jnp.zeros_like(dcw_ref)
            dcb_ref[...] = jnp.zeros_like(dcb_ref)

        @pl.when(lax.rem((ri + 1) * tm, seq) == 0)
        def _():
            carry[...] = jnp.zeros_like(carry)

        keep = jnp.where(lax.rem(ri * tm, seq) == 0, 0.0, 1.0)
        gb = hb_ref[...].astype(F32)
        gc = hc_ref[...].astype(F32)
        v = hv_ref[...].astype(F32)
        p = gc * v
        p_head = hch_ref[...].astype(F32) * hvh_ref[...].astype(F32) * keep
        ext = jnp.concatenate([p_head, p], axis=0)
        pm2 = _down(ext, 2, SUB16)
        pm1 = _down(ext, 1, SUB16)
        u = cb_ref[...] + _row(cw_ref, 0) * pm2 + _row(cw_ref, 1) * pm1 + _row(cw_ref, 2) * p
        dqf = dq_ref[...].astype(F32)
        du = dqf * gb
        dcb_ref[...] += _colsum(du)
        dcw_ref[0:1, :] += _colsum(du * pm2)
        dcw_ref[1:2, :] += _colsum(du * pm1)
        dcw_ref[2:3, :] += _colsum(du * p)
        ext2 = jnp.concatenate([du, carry[...]], axis=0)
        dp = _row(cw_ref, 2) * du + _row(cw_ref, 1) * _up(ext2, 1, tm) + _row(cw_ref, 0) * _up(ext2, 2, tm)
        carry[...] = du[0:SUB, :]
        dh_ref[0] = (dqf * u).astype(BF16)
        dh_ref[1] = (dp * v).astype(BF16)
        dh_ref[2] = (dp * gc).astype(BF16)

    blk = lambda c: pl.BlockSpec((tm, d), lambda i: (nt - 1 - i, c))
    head = lambda c: pl.BlockSpec((SUB16, d), lambda i: (jnp.maximum((nt - 1 - i) * hpt - 1, 0), c))
    vec = lambda r: pl.BlockSpec((r, d), lambda i: (0, 0))
    return pl.pallas_call(
        body, name=name, grid=(nt,),
        in_specs=[blk(0), blk(1), blk(2), head(1), head(2),
                  pl.BlockSpec((tm, d), lambda i: (nt - 1 - i, 0)), vec(3), vec(1)],
        out_specs=[pl.BlockSpec((3, tm, d), lambda i: (0, nt - 1 - i, 0)), vec(3), vec(1)],
        out_shape=[jax.ShapeDtypeStruct((3, t, d), BF16), jax.ShapeDtypeStruct((3, d), F32),
                   jax.ShapeDtypeStruct((1, d), F32)],
        scratch_shapes=[pltpu.VMEM((SUB, d), F32)],
        compiler_params=_cp("arbitrary"))(h, h, h, h, h, dq, cw, cb)


def _fold(v):
    return jnp.sum(v.reshape(v.shape[0] // SUB, SUB, v.shape[1]), axis=0)


def _ffn_up(xb, w3, l, cw, cb, seq, name, rider=()):
    t, d = xb.shape
    f = w3.shape[2] // 2
    tc = f // FFN_COL_BLOCKS
    tm = min(TM_MM, seq)
    nc = FFN_COL_BLOCKS
    bufs, rkinds, rspecs, rshapes, rsems = _rider_args(rider)
    nr = len(bufs)

    def body(*refs):
        x_ref, wg_ref, wv_ref, cwg_ref, cwv_ref, cbg_ref, cbv_ref = refs[:7]
        h_ref, pre_ref, act_ref = refs[7 + nr:10 + nr]
        rout = refs[10 + nr:10 + 2 * nr]
        eg, ev = refs[10 + 2 * nr:12 + 2 * nr]
        sems = refs[12 + 2 * nr:]
        i = pl.program_id(1)
        if nr:
            @pl.when(jnp.logical_and(pl.program_id(0) == 0, i == 0))
            def _():
                _rider_start(rout, rkinds, *sems)

        @pl.when(lax.rem(i * tm, seq) == 0)
        def _():
            eg[0:SUB, :] = jnp.zeros((SUB, tc), F32)
            ev[0:SUB, :] = jnp.zeros((SUB, tc), F32)

        xx = x_ref[...]

        def matmul(lo, hi):
            eg[SUB:, lo:hi] = jnp.dot(xx, wg_ref[:, lo:hi], preferred_element_type=F32)
            ev[SUB:, lo:hi] = jnp.dot(xx, wv_ref[:, lo:hi], preferred_element_type=F32)

        def epilogue(lo, hi):
            for c0 in range(lo, hi, LANE):
                cols = slice(c0, c0 + LANE)
                taps = [[ref[k:k + 1, cols] for k in range(3)] + [bref[:, cols]]
                        for ref, bref in ((cwg_ref, cbg_ref), (cwv_ref, cbv_ref))]
                for r0 in range(0, tm, RC):
                    rows = slice(r0, r0 + RC)
                    pres = []
                    for half, e_ref in enumerate((eg, ev)):
                        w0, w1, w2, bias = taps[half]
                        e = e_ref[r0:r0 + RC + SUB, cols]
                        cur = e[SUB:]
                        pre = bias + w0 * _down(e, 2, SUB) + w1 * _down(e, 1, SUB) + w2 * cur
                        h_ref[half, rows, cols] = cur.astype(BF16)
                        pre_ref[half, rows, cols] = pre.astype(BF16)
                        pres.append(pre)
                    act_ref[rows, cols] = (pres[0] * _sigmoid(pres[0]) * pres[1]).astype(BF16)

        blocks = [(lo, min(lo + MXU_COLS, tc)) for lo in range(0, tc, MXU_COLS)]
        matmul(*blocks[0])
        for b, blk in enumerate(blocks):
            if b + 1 < len(blocks):
                matmul(*blocks[b + 1])
            epilogue(*blk)
        eg[0:SUB, :] = eg[tm:tm + SUB, :]
        ev[0:SUB, :] = ev[tm:tm + SUB, :]
        if nr:
            @pl.when(jnp.logical_and(pl.program_id(0) == nc - 1, i == t // tm - 1))
            def _():
                _rider_wait(rout, rkinds, *sems)

    wspec = lambda off: pl.BlockSpec((None, d, tc), lambda j, i: (l, 0, j + off))
    vec = lambda r, off: pl.BlockSpec((r, tc), lambda j, i: (0, j + off))
    pair = pl.BlockSpec((2, tm, tc), lambda j, i: (0, i, j))
    return pl.pallas_call(
        body, name=name, grid=(nc, t // tm),
        in_specs=[pl.BlockSpec((tm, d), lambda j, i: (i, 0)), wspec(0), wspec(nc),
                  vec(3, 0), vec(3, nc), vec(1, 0), vec(1, nc)] + rspecs,
        out_specs=[pair, pair, pl.BlockSpec((tm, tc), lambda j, i: (i, j))] + rspecs,
        out_shape=[jax.ShapeDtypeStruct((2, t, f), BF16), jax.ShapeDtypeStruct((2, t, f), BF16),
                   jax.ShapeDtypeStruct((t, f), BF16)] + rshapes,
        input_output_aliases={7 + u: 3 + u for u in range(nr)},
        scratch_shapes=[pltpu.VMEM((tm + SUB, tc), F32), pltpu.VMEM((tm + SUB, tc), F32)] + rsems,
        compiler_params=_cp("arbitrary", "arbitrary"))(xb, w3, w3, cw, cw, cb, cb, *bufs)


def _ffn_down_bwd(dzb, wd3, l, h3, pre3, cw, seq, name, scatter=()):
    t, d = dzb.shape
    f = wd3.shape[1]
    tc = f // FFN_COL_BLOCKS
    tm = min(TM_MM, seq)
    nt = t // tm
    nc = FFN_COL_BLOCKS
    srcs, skinds, sspecs, sshapes, ssems = _scatter_args(scatter)
    ns = len(srcs)

    def body(*refs):
        dz_ref, wd_ref, h_ref, pre_ref, cwg_ref, cwv_ref = refs[:6]
        sin = refs[6:6 + ns]
        dh_ref, dcwg_ref, dcwv_ref, dcbg_ref, dcbv_ref = refs[6 + ns:11 + ns]
        lands = refs[11 + ns:11 + 2 * ns]
        da_s, carry = refs[11 + 2 * ns:13 + 2 * ns]
        sems = refs[13 + 2 * ns:]
        i = pl.program_id(1)
        ri = nt - 1 - i
        if ns:
            @pl.when(jnp.logical_and(pl.program_id(0) == 0, i == 0))
            def _():
                for cp in _scatter_copies(sin, lands, skinds, *sems):
                    cp.start()

        @pl.when(i == 0)
        def _():
            for r in (dcwg_ref, dcwv_ref, dcbg_ref, dcbv_ref):
                r[...] = jnp.zeros_like(r)

        @pl.when(lax.rem((ri + 1) * tm, seq) == 0)
        def _():
            carry[...] = jnp.zeros_like(carry)

        dz = dz_ref[...]

        def matmul(lo, hi):
            da_s[:, lo:hi] = lax.dot_general(dz, wd_ref[lo:hi, :], (((1,), (1,)), ((), ())),
                                             preferred_element_type=F32)

        def epilogue(lo, hi):
            for c0 in range(lo, hi, LANE):
                cols = slice(c0, c0 + LANE)
                taps = [[ref[k:k + 1, cols] for k in range(3)] for ref in (cwg_ref, cwv_ref)]
                acc = [jnp.zeros((SUB, LANE), F32)] * 8
                for r0 in range(tm - RC, -1, -RC):
                    rows = slice(r0, r0 + RC)
                    da = da_s[rows, cols]
                    gp = pre_ref[0, rows, cols].astype(F32)
                    vp = pre_ref[1, rows, cols].astype(F32)
                    sg = _sigmoid(gp)
                    dpres = (da * vp * (sg * (1.0 + gp * (1.0 - sg))), da * (gp * sg))
                    for half in range(2):
                        w0, w1, w2 = taps[half]
                        dpre = dpres[half]
                        ext = jnp.concatenate([dpre, carry[half, :, cols]], axis=0)
                        u1 = _up(ext, 1, RC)
                        u2 = _up(ext, 2, RC)
                        carry[half, :, cols] = dpre[0:SUB]
                        dh_ref[half, rows, cols] = (w2 * dpre + w1 * u1 + w0 * u2).astype(BF16)
                        hh = h_ref[half, rows, cols].astype(F32)
                        for k, term in enumerate((hh * u2, hh * u1, hh * dpre, dpre)):
                            acc[4 * half + k] = acc[4 * half + k] + _fold(term)
                for half, (dcw_ref, dcb_ref) in enumerate(((dcwg_ref, dcbg_ref), (dcwv_ref, dcbv_ref))):
                    for k in range(3):
                        dcw_ref[k:k + 1, cols] += _colsum(acc[4 * half + k])
                    dcb_ref[:, cols] += _colsum(acc[4 * half + 3])

        blocks = [(lo, min(lo + MXU_COLS, tc)) for lo in range(0, tc, MXU_COLS)]
        matmul(*blocks[0])
        for b, blk in enumerate(blocks):
            if b + 1 < len(blocks):
                matmul(*blocks[b + 1])
            epilogue(*blk)
        if ns:
            @pl.when(jnp.logical_and(pl.program_id(0) == nc - 1, i == nt - 1))
            def _():
                for cp in _scatter_copies(sin, lands, skinds, *sems):
                    cp.wait()

    pair = pl.BlockSpec((2, tm, tc), lambda j, i: (0, nt - 1 - i, j))
    vec = lambda off: pl.BlockSpec((3, tc), lambda j, i: (0, j + off))
    acc_spec = lambda r: pl.BlockSpec((r, tc), lambda j, i: (0, j))
    return pl.pallas_call(
        body, name=name, grid=(nc, nt),
        in_specs=[pl.BlockSpec((tm, d), lambda j, i: (nt - 1 - i, 0)),
                  pl.BlockSpec((None, tc, d), lambda j, i: (l, j, 0)), pair, pair, vec(0), vec(nc)] + sspecs,
        out_specs=[pair, acc_spec(3), acc_spec(3), acc_spec(1), acc_spec(1)] + sspecs,
        out_shape=[jax.ShapeDtypeStruct((2, t, f), BF16), jax.ShapeDtypeStruct((3, f), F32),
                   jax.ShapeDtypeStruct((3, f), F32), jax.ShapeDtypeStruct((1, f), F32),
                   jax.ShapeDtypeStruct((1, f), F32)] + sshapes,
        scratch_shapes=[pltpu.VMEM((tm, tc), F32), pltpu.VMEM((2, SUB, tc), F32)] + ssems,
        compiler_params=_cp("arbitrary", "arbitrary"))(dzb, wd3, h3, pre3, cw, cw, *srcs)


def _lru_gates(xr, wg_ref, bg_ref):
    rs, gs = [], []
    for hd in range(LRU_HEADS):
        xh = xr[:, hd * LRU_BLOCK:(hd + 1) * LRU_BLOCK].astype(BF16)
        gt = jnp.dot(xh, wg_ref[hd], preferred_element_type=F32) + _row(bg_ref, hd)
        rs.append(gt[:, :LRU_BLOCK])
        gs.append(gt[:, LRU_BLOCK:])
    return jnp.concatenate(rs, axis=1), jnp.concatenate(gs, axis=1)


def _lru_coeffs(xr, wg_ref, bg_ref, lam_ref):
    gr, gi = _lru_gates(xr, wg_ref, bg_ref)
    r = _sigmoid(gr)
    ig = _sigmoid(gi)
    sp = _softplus(-lam_ref[...])
    log_a = -LRU_C * r * sp
    a = jnp.exp(log_a)
    mult = jnp.sqrt(_one_minus_exp(2.0 * log_a))
    return r, ig, sp, a, mult


def _lru_fwd(h, cw, cb, wg, bg, lam, seq, name):
    t, r2 = h.shape
    rw = r2 // 2
    ts = min(TS_LRU, seq)
    n8 = ts // SUB

    def body(hg_ref, hr_ref, cw_ref, cb_ref, wg_ref, bg_ref, lam_ref, hs_ref, y_ref,
             a_s, b_s, cconv, cstate):
        i = pl.program_id(0)

        @pl.when(lax.rem(i * ts, seq) == 0)
        def _():
            cconv[...] = jnp.zeros_like(cconv)
            cstate[...] = jnp.zeros_like(cstate)

        rin = hr_ref[...].astype(F32)
        ext = jnp.concatenate([cconv[...], rin], axis=0)
        xr = cb_ref[...]
        for k in range(4):
            xr = xr + _row(cw_ref, k) * _down(ext, 3 - k, SUB)
        cconv[...] = rin[ts - SUB:, :]
        _, ig, _, a, mult = _lru_coeffs(xr, wg_ref, bg_ref, lam_ref)
        a_s[...] = a
        b_s[...] = mult * (ig * xr)
        row = lax.broadcasted_iota(jnp.int32, (SUB, rw), 0)

        def step(j, carry):
            off = pl.multiple_of(j * SUB, SUB)
            a8 = a_s[pl.ds(off, SUB), :]
            b8 = b_s[pl.ds(off, SUB), :]
            for d in (1, 2, 4):
                m = row >= d
                b8 = jnp.where(m, a8 * pltpu.roll(b8, d, 0) + b8, b8)
                a8 = jnp.where(m, a8 * pltpu.roll(a8, d, 0), a8)
            h8 = a8 * carry + b8
            hs_ref[pl.ds(off, SUB), :] = h8
            return _colsum(jnp.where(row == SUB - 1, h8, 0.0))

        cstate[...] = lax.fori_loop(0, n8, step, cstate[...])
        gel, _ = _gelu_and_grad(hg_ref[...].astype(F32))
        y_ref[...] = (hs_ref[...] * gel).astype(BF16)

    full = lambda shp: pl.BlockSpec(shp, lambda i: (0,) * len(shp))
    return pl.pallas_call(
        body, name=name, grid=(t // ts,),
        in_specs=[pl.BlockSpec((ts, rw), lambda i: (i, 0)), pl.BlockSpec((ts, rw), lambda i: (i, 1)),
                  full((4, rw)), full((1, rw)), full(wg.shape), full(bg.shape), full((1, rw))],
        out_specs=[pl.BlockSpec((ts, rw), lambda i: (i, 0)), pl.BlockSpec((ts, rw), lambda i: (i, 0))],
        out_shape=[jax.ShapeDtypeStruct((t, rw), F32), jax.ShapeDtypeStruct((t, rw), BF16)],
        scratch_shapes=[pltpu.VMEM((ts, rw), F32), pltpu.VMEM((ts, rw), F32),
                        pltpu.VMEM((SUB, rw), F32), pltpu.VMEM((1, rw), F32)],
        compiler_params=_cp("arbitrary"))(h, h, cw, cb, wg, bg, lam)


def _lru_bwd(h, hs, dy, cw, cb, wg, bg, lam, seq, name):
    t, r2 = h.shape
    rw = r2 // 2
    ts = min(TS_LRU, seq)
    nt = t // ts
    n8 = ts // SUB
    hp16 = ts // SUB16
    hp8 = ts // SUB

    def body(hg_ref, hr_ref, hrh_ref, hs_ref, hsh_ref, dy_ref, cw_ref, cb_ref, wg_ref, bg_ref, lam_ref,
             dh_ref, dbin_ref, dcw_ref, dcb_ref, dwg_ref, dbg_ref, dlam_ref,
             a_s, g_s, l_s, c_lam, c_a, c_dxr):
        i = pl.program_id(0)
        ri = nt - 1 - i

        @pl.when(i == 0)
        def _():
            for r in (dbin_ref, dcw_ref, dcb_ref, dwg_ref, dbg_ref, dlam_ref):
                r[...] = jnp.zeros_like(r)

        @pl.when(lax.rem((ri + 1) * ts, seq) == 0)
        def _():
            c_lam[...] = jnp.zeros_like(c_lam)
            c_a[...] = jnp.zeros_like(c_a)
            c_dxr[...] = jnp.zeros_like(c_dxr)

        keep = jnp.where(lax.rem(ri * ts, seq) == 0, 0.0, 1.0)
        rin = hr_ref[...].astype(F32)
        ext = jnp.concatenate([hrh_ref[...].astype(F32) * keep, rin], axis=0)
        shifted = [_down(ext, 3 - k, SUB16) for k in range(4)]
        xr = cb_ref[...]
        for k in range(4):
            xr = xr + _row(cw_ref, k) * shifted[k]
        r, ig, sp, a, mult = _lru_coeffs(xr, wg_ref, bg_ref, lam_ref)
        gel, dgel = _gelu_and_grad(hg_ref[...].astype(F32))
        dyf = dy_ref[...].astype(F32)
        hsv = hs_ref[...]
        dg = dyf * hsv * dgel

        a_s[...] = _up(jnp.concatenate([a, c_a[...]], axis=0), 1, ts)
        g_s[...] = dyf * gel
        c_a[...] = a[0:SUB, :]
        row = lax.broadcasted_iota(jnp.int32, (SUB, rw), 0)

        def step(j, carry):
            off = pl.multiple_of((n8 - 1 - j) * SUB, SUB)
            a8 = a_s[pl.ds(off, SUB), :]
            b8 = g_s[pl.ds(off, SUB), :]
            for d in (1, 2, 4):
                m = row < SUB - d
                b8 = jnp.where(m, a8 * pltpu.roll(b8, SUB - d, 0) + b8, b8)
                a8 = jnp.where(m, a8 * pltpu.roll(a8, SUB - d, 0), a8)
            l8 = a8 * carry + b8
            l_s[pl.ds(off, SUB), :] = l8
            return _colsum(jnp.where(row == 0, l8, 0.0))

        c_lam[...] = lax.fori_loop(0, n8, step, c_lam[...])
        lamv = l_s[...]
        hs_prev = _down(jnp.concatenate([hsh_ref[...] * keep, hsv], axis=0), 1, SUB)
        da = lamv * hs_prev
        t1 = lamv * xr
        dmult = t1 * ig
        dig = t1 * mult
        dxr = lamv * mult * ig
        dla = da * a - dmult * (a * a) / mult
        dr = dla * (-LRU_C * sp)
        dlam_ref[...] += _colsum(dla * (-LRU_C) * r) * (-_sigmoid(-lam_ref[...]))
        dgr = dr * r * (1.0 - r)
        dgi = dig * ig * (1.0 - ig)
        parts = []
        for hd in range(LRU_HEADS):
            sl = slice(hd * LRU_BLOCK, (hd + 1) * LRU_BLOCK)
            dgt = jnp.concatenate([dgr[:, sl], dgi[:, sl]], axis=1)
            dbg_ref[hd:hd + 1, :] += _colsum(dgt)
            dgt16 = dgt.astype(BF16)
            parts.append(lax.dot_general(dgt16, wg_ref[hd], (((1,), (1,)), ((), ())),
                                         preferred_element_type=F32))
            dwg_ref[hd] += lax.dot_general(xr[:, sl].astype(BF16), dgt16, (((0,), (0,)), ((), ())),
                                           preferred_element_type=F32)
        dxr = dxr + jnp.concatenate(parts, axis=1)

        dcb_ref[...] += _colsum(dxr)
        for k in range(4):
            dcw_ref[k:k + 1, :] += _colsum(dxr * shifted[k])
        ext2 = jnp.concatenate([dxr, c_dxr[...]], axis=0)
        drb = _row(cw_ref, 3) * dxr
        for k in range(3):
            drb = drb + _row(cw_ref, k) * _up(ext2, 3 - k, ts)
        c_dxr[...] = dxr[0:SUB, :]
        dh_ref[0] = dg.astype(BF16)
        dh_ref[1] = drb.astype(BF16)
        dbin_ref[:, 0:rw] += _colsum(dg)
        dbin_ref[:, rw:] += _colsum(drb)

    rev = lambda c: pl.BlockSpec((ts, rw), lambda i: (nt - 1 - i, c))
    full = lambda shp: pl.BlockSpec(shp, lambda i: (0,) * len(shp))
    nh = LRU_HEADS
    return pl.pallas_call(
        body, name=name, grid=(nt,),
        in_specs=[rev(0), rev(1),
                  pl.BlockSpec((SUB16, rw), lambda i: (jnp.maximum((nt - 1 - i) * hp16 - 1, 0), 1)),
                  rev(0),
                  pl.BlockSpec((SUB, rw), lambda i: (jnp.maximum((nt - 1 - i) * hp8 - 1, 0), 0)),
                  rev(0), full((4, rw)), full((1, rw)), full(wg.shape), full(bg.shape), full((1, rw))],
        out_specs=[pl.BlockSpec((2, ts, rw), lambda i: (0, nt - 1 - i, 0)), full((1, r2)), full((4, rw)),
                   full((1, rw)), full((nh, LRU_BLOCK, 2 * LRU_BLOCK)), full((nh, 2 * LRU_BLOCK)), full((1, rw))],
        out_shape=[jax.ShapeDtypeStruct((2, t, rw), BF16), jax.ShapeDtypeStruct((1, r2), F32),
                   jax.ShapeDtypeStruct((4, rw), F32), jax.ShapeDtypeStruct((1, rw), F32),
                   jax.ShapeDtypeStruct((nh, LRU_BLOCK, 2 * LRU_BLOCK), F32),
                   jax.ShapeDtypeStruct((nh, 2 * LRU_BLOCK), F32), jax.ShapeDtypeStruct((1, rw), F32)],
        scratch_shapes=[pltpu.VMEM((ts, rw), F32), pltpu.VMEM((ts, rw), F32), pltpu.VMEM((ts, rw), F32),
                        pltpu.VMEM((1, rw), F32), pltpu.VMEM((SUB, rw), F32), pltpu.VMEM((SUB, rw), F32)],
        compiler_params=_cp("arbitrary"))(h, h, h, hs, hs, dy, cw, cb, wg, bg, lam)


def _row_tile(rows, cols, mult, elems=ELEMS_PER_BLOCK):
    cap = max(mult, elems // cols)
    best = None
    for cand in range(mult, min(rows, cap) + 1, mult):
        if rows % cand == 0:
            best = cand
    return best if best is not None else rows


def _core_index():
    return lax.axis_index("c").astype(jnp.int32).reshape(1)


def _chip_index():
    return (2 * lax.axis_index("x") + lax.axis_index("y")).astype(jnp.int32).reshape(1)


def _add_pair(p4, r3, name):
    s, _, rows, cols = p4.shape
    tr = _row_tile(rows, cols, SUB16)

    def body(c_ref, a_ref, b_ref, o_ref):
        o_ref[...] = (a_ref[...].astype(F32) + b_ref[...].astype(F32)).astype(o_ref.dtype)

    blk = pl.BlockSpec((None, tr, cols), lambda k, i, c_ref: (k, i, 0))
    return pl.pallas_call(
        body, name=name,
        grid_spec=pltpu.PrefetchScalarGridSpec(
            num_scalar_prefetch=1, grid=(s, rows // tr),
            in_specs=[pl.BlockSpec((None, None, tr, cols), lambda k, i, c_ref: (k, c_ref[0], i, 0)), blk],
            out_specs=blk),
        out_shape=jax.ShapeDtypeStruct((s, rows, cols), p4.dtype),
        compiler_params=_cp("parallel", "parallel"))(_core_index(), p4, r3)


def _add_chips(r, name):
    shape = r.shape[1:]
    r3 = r.reshape(N_CHIPS, -1, shape[-1])
    _, rows, cols = r3.shape
    tr = _row_tile(rows, cols, SUB16)

    def body(r_ref, o_ref):
        s = r_ref[0].astype(F32) + r_ref[1].astype(F32)
        s = s + r_ref[2].astype(F32)
        o_ref[...] = s + r_ref[3].astype(F32)

    out = pl.pallas_call(body, name=name, grid=(rows // tr,),
                         in_specs=[pl.BlockSpec((N_CHIPS, tr, cols), lambda i: (0, i, 0))],
                         out_specs=pl.BlockSpec((tr, cols), lambda i: (i, 0)),
                         out_shape=jax.ShapeDtypeStruct((rows, cols), F32),
                         compiler_params=_cp("parallel"))(r3)
    return out.reshape(shape)


def _adamw(w3, g_mine, g_sib, m3, v3, name):
    nl, r, cols = w3.shape
    rows = r // 2
    flat = [arr.reshape(nl, 2, rows, cols) for arr in (w3, m3, v3)]
    tr = _row_tile(rows, cols, SUB)

    def body(c_ref, w_ref, gm_ref, gs_ref, m_ref, v_ref, g_ref, d_ref, mo_ref, vo_ref):
        gg = jnp.where(pl.program_id(1) == c_ref[0], gm_ref[...], gs_ref[...])
        m2 = ADAM_B1 * m_ref[...] + (1.0 - ADAM_B1) * gg
        v2 = ADAM_B2 * v_ref[...] + (1.0 - ADAM_B2) * (gg * gg)
        m_hat = m2 / (1.0 - ADAM_B1 ** ADAM_STEP)
        v_hat = v2 / (1.0 - ADAM_B2 ** ADAM_STEP)
        g_ref[...] = gg
        d_ref[...] = -ADAM_LR * (m_hat / (jnp.sqrt(v_hat) + ADAM_EPS) + ADAM_WD * w_ref[...])
        mo_ref[...] = m2
        vo_ref[...] = v2

    blk = pl.BlockSpec((None, None, tr, cols), lambda l, hh, i, c_ref: (l, hh, i, 0))
    gblk = pl.BlockSpec((None, tr, cols), lambda l, hh, i, c_ref: (l, i, 0))
    outs = pl.pallas_call(
        body, name=name,
        grid_spec=pltpu.PrefetchScalarGridSpec(
            num_scalar_prefetch=1, grid=(nl, 2, rows // tr),
            in_specs=[blk, gblk, gblk, blk, blk], out_specs=[blk] * 4),
        out_shape=[jax.ShapeDtypeStruct((nl, 2, rows, cols), F32)] * 4,
        compiler_params=_cp("parallel", "parallel", "parallel"))(_core_index(), flat[0], g_mine, g_sib, flat[1],
                                                                 flat[2])
    return tuple(o.reshape(nl, r, cols) for o in outs)


def _place(src3, layer, kind, dtype, name):
    _, r, c = src3.shape
    tr = _row_tile(r, c, SUB16)
    in_spec = pl.BlockSpec((None, tr, c), lambda i, my_ref: (layer, i, 0))
    if kind == "col":
        out_spec = pl.BlockSpec((tr, c), lambda i, my_ref: (i, my_ref[0]))
        out_shape = (r, N_CHIPS * c)
    else:
        out_spec = pl.BlockSpec((None, tr, c), lambda i, my_ref: (my_ref[0], i, 0))
        out_shape = (N_CHIPS, r, c)

    def body(my_ref, i_ref, o_ref):
        o_ref[...] = i_ref[...].astype(o_ref.dtype)

    return pl.pallas_call(
        body, name=name,
        grid_spec=pltpu.PrefetchScalarGridSpec(num_scalar_prefetch=1, grid=(r // tr,), in_specs=[in_spec],
                                               out_specs=out_spec),
        out_shape=jax.ShapeDtypeStruct(out_shape, dtype),
        compiler_params=_cp("parallel"))(_chip_index(), src3)


def _half(ref, c, h):
    return ref.at[pl.ds(c * h, h)]


def _position():
    x = lax.axis_index("x")
    y = lax.axis_index("y")
    c = lax.axis_index("c")
    return x, y, c


def _peer_chip(x, y, j):
    tx = 1 - x if j & 2 else x
    ty = 1 - y if j & 1 else y
    return tx, ty


def _remote(src, dst, ssem, rsem, dev):
    return pltpu.make_async_remote_copy(src_ref=src, dst_ref=dst, send_sem=ssem, recv_sem=rsem,
                                        device_id=dev, device_id_type=pl.DeviceIdType.MESH)


_ANY = pl.BlockSpec(memory_space=pl.ANY)


def _unit_view(kind, ref, k):
    if kind == "col":
        n = ref.shape[1] // N_CHIPS
        return ref.at[:, pl.ds(pl.multiple_of(k * n, LANE), n)]
    return ref.at[k]


def _all_gather(placed, kinds):
    nt = len(placed)

    def body(*refs):
        outs = refs[nt:2 * nt]
        ssem, rsem = refs[2 * nt:]
        x, y, c = _position()
        my = 2 * x + y
        sib = (x, y, 1 - c)

        def part(t, k, core):
            view = _unit_view(kinds[t], outs[t], k)
            h = view.shape[0] // 2
            return _half(view, core, h)

        sends, fwds = [], []
        for t in range(nt):
            own = part(t, my, c)
            for j in (1, 2, 3):
                tx, ty = _peer_chip(x, y, j)
                cp = _remote(own, own, ssem.at[6 * t + j - 1], rsem.at[6 * t + j - 1], (tx, ty, c))
                cp.start()
                sends.append(cp)
        for t in range(nt):
            for j in (1, 2, 3):
                tx, ty = _peer_chip(x, y, j)
                got = part(t, 2 * tx + ty, c)
                _remote(got, got, ssem.at[6 * t + j - 1], rsem.at[6 * t + j - 1], sib).wait_recv()
                cp = _remote(got, got, ssem.at[6 * t + 2 + j], rsem.at[6 * t + 2 + j], sib)
                cp.start()
                fwds.append(cp)
        for t in range(nt):
            for j in (1, 2, 3):
                tx, ty = _peer_chip(x, y, j)
                other = part(t, 2 * tx + ty, 1 - c)
                _remote(other, other, ssem.at[6 * t + 2 + j], rsem.at[6 * t + 2 + j], sib).wait_recv()
        for cp in sends + fwds:
            cp.wait_send()

    return pl.pallas_call(
        body, name="all_gather", in_specs=[_ANY] * nt, out_specs=[_ANY] * nt,
        out_shape=[jax.ShapeDtypeStruct(p.shape, p.dtype) for p in placed],
        input_output_aliases={t: t for t in range(nt)},
        scratch_shapes=[pltpu.SemaphoreType.DMA((6 * nt,)), pltpu.SemaphoreType.DMA((6 * nt,))],
    )(*placed)


def _rider_start(refs, kinds, ssem, rsem):
    x, y, c = _position()
    my = 2 * x + y
    for u, (ref, kind) in enumerate(zip(refs, kinds)):
        own = _unit_view(kind, ref, my)
        for j in (1, 2, 3):
            tx, ty = _peer_chip(x, y, j)
            _remote(own, own, ssem.at[3 * u + j - 1], rsem.at[3 * u + j - 1], (tx, ty, c)).start()


def _rider_wait(refs, kinds, ssem, rsem):
    x, y, c = _position()
    for u, (ref, kind) in enumerate(zip(refs, kinds)):
        for j in (1, 2, 3):
            tx, ty = _peer_chip(x, y, j)
            got = _unit_view(kind, ref, 2 * tx + ty)
            _remote(got, got, ssem.at[3 * u + j - 1], rsem.at[3 * u + j - 1], (tx, ty, c)).wait()


def _rider_args(rider):
    bufs = [b for b, _ in rider]
    kinds = [k for _, k in rider]
    n = len(bufs)
    sems = [pltpu.SemaphoreType.DMA((3 * n,)), pltpu.SemaphoreType.DMA((3 * n,))] if n else []
    return bufs, kinds, [_ANY] * n, [jax.ShapeDtypeStruct(b.shape, b.dtype) for b in bufs], sems


def _d2d_stream(src4, other_half, name):
    s, _, rows, cols = src4.shape
    tr = _row_tile(rows, cols, SUB16, STREAM_ELEMS_PER_BLOCK)
    nblk = rows // tr

    nh = src4.shape[1]

    def body(c_ref, src_ref, dst_ref, ssem, rsem):
        k = pl.program_id(0)
        i = pl.program_id(1)
        x, y, c = _position()
        sib = (x, y, 1 - c)
        blk = dst_ref.at[pl.ds(pl.multiple_of((k * nblk + i) * tr, SUB16), tr)]
        cp = _remote(src_ref, blk, ssem, rsem, sib)
        cp.start()
        cp.wait_send()

        @pl.when(jnp.logical_and(k == s - 1, i == nblk - 1))
        def _():
            _remote(dst_ref, dst_ref, ssem, rsem, sib).wait_recv()

    if other_half:
        src_map = lambda k, i, c_ref: ((k * nh + 1 - c_ref[0]) * nblk + i, 0)
    else:
        src_map = lambda k, i, c_ref: (k * nh * nblk + i, 0)
    out = pl.pallas_call(
        body, name=name,
        grid_spec=pltpu.PrefetchScalarGridSpec(
            num_scalar_prefetch=1, grid=(s, nblk),
            in_specs=[pl.BlockSpec((tr, cols), src_map)], out_specs=_ANY,
            scratch_shapes=[pltpu.SemaphoreType.DMA, pltpu.SemaphoreType.DMA]),
        out_shape=jax.ShapeDtypeStruct((s * rows, cols), src4.dtype),
        compiler_params=_cp("arbitrary", "arbitrary"))(_core_index(), src4.reshape(s * nh * rows, cols))
    return out.reshape(s, rows, cols)


def _scatter_copies(srcs, lands, kinds, ssem, rsem, lsem):
    x, y, c = _position()
    my = 2 * x + y
    cps = []
    for u, (src, land, kind) in enumerate(zip(srcs, lands, kinds)):
        cps.append(pltpu.make_async_copy(_unit_view(kind, src, my), land.at[my], lsem.at[u]))
        for j in (1, 2, 3):
            tx, ty = _peer_chip(x, y, j)
            cps.append(_remote(_unit_view(kind, src, 2 * tx + ty), land.at[my], ssem.at[3 * u + j - 1],
                               rsem.at[3 * u + j - 1], (tx, ty, c)))
    return cps


def _scatter_args(scatter):
    srcs = [s for s, _ in scatter]
    kinds = [k for _, k in scatter]
    n = len(srcs)
    shapes = [jax.ShapeDtypeStruct((N_CHIPS, s.shape[0], s.shape[1] // N_CHIPS) if k == "col" else s.shape, s.dtype)
              for s, k in scatter]
    sems = [pltpu.SemaphoreType.DMA((3 * n,)), pltpu.SemaphoreType.DMA((3 * n,)),
            pltpu.SemaphoreType.DMA((n,))] if n else []
    return srcs, kinds, [_ANY] * n, shapes, sems


def _rs_scatter(scatter):
    srcs, kinds, specs, shapes, sems = _scatter_args(scatter)
    n = len(srcs)

    def body(*refs):
        cps = _scatter_copies(refs[:n], refs[n:2 * n], kinds, *refs[2 * n:])
        for cp in cps:
            cp.start()
        for cp in cps:
            cp.wait()

    return pl.pallas_call(body, name="rs_scatter", in_specs=specs, out_specs=specs, out_shape=shapes,
                          scratch_shapes=sems)(*srcs)


SMALL = (("sc_conv_w", True), ("sc_conv_b", False), ("lru_b_in", True), ("lru_conv_w", True),
         ("lru_conv_b", True), ("lru_b_gate", True), ("lru_lambda", True), ("ffn_conv_w", True),
         ("ffn_conv_b", False), ("ln_g", True), ("ln_b", True))
PACK_ROW_MULT = 2 * SUB16


def _pack_rows(shapes):
    n = sum(math.prod(shapes[name]) for name, _ in SMALL)
    rows = -(-n // 128)
    return -(-rows // PACK_ROW_MULT) * PACK_ROW_MULT


def _pack_local(vals, shapes):
    flat = jnp.concatenate([vals[name].reshape(-1) for name, _ in SMALL])
    rows = _pack_rows(shapes)
    return jnp.pad(flat, (0, rows * 128 - flat.shape[0])).reshape(rows, 128)


def _unpack_local(pack, shapes):
    flat = pack.reshape(-1)
    out, off = {}, 0
    for name, _ in SMALL:
        n = math.prod(shapes[name])
        out[name] = flat[off:off + n].reshape(shapes[name])
        off += n
    return out


def _pack_slots(fulls, shapes):
    parts = []
    for name, sharded in SMALL:
        v = fulls[name]
        if sharded:
            ns = shapes[name][-1]
            v = jnp.moveaxis(v.reshape(v.shape[:-1] + (N_CHIPS, ns)), -2, 0).reshape(N_CHIPS, -1)
        else:
            v = jnp.broadcast_to(v.reshape(1, -1), (N_CHIPS, v.size))
        parts.append(v)
    flat = jnp.concatenate(parts, axis=1)
    rows = _pack_rows(shapes)
    return jnp.pad(flat, ((0, 0), (0, rows * 128 - flat.shape[1]))).reshape(N_CHIPS, rows, 128)


def _unpack_slots(packs, shapes):
    flat = packs.reshape(N_CHIPS, -1)
    out, off = {}, 0
    for name, sharded in SMALL:
        n = math.prod(shapes[name])
        if sharded:
            seg = flat[:, off:off + n].reshape((N_CHIPS,) + tuple(shapes[name]))
            seg = jnp.moveaxis(seg, 0, -2)
            out[name] = seg.reshape(seg.shape[:-2] + (N_CHIPS * shapes[name][-1],))
        off += n
    return out


WEIGHTS = ("sc_w_in", "sc_conv_w", "sc_conv_b", "sc_w_out", "lru_w_in", "lru_b_in", "lru_conv_w", "lru_conv_b",
           "lru_w_gate", "lru_b_gate", "lru_lambda", "lru_w_out", "ffn_w_up", "ffn_conv_w", "ffn_conv_b",
           "ffn_w_down", "ln_g", "ln_b")
GATHER_KIND = {"sc_w_in": "col", "sc_w_out": "lead", "lru_w_in": "col", "lru_w_out": "lead", "ffn_w_up": "col",
               "ffn_w_down": "lead"}


def kernel(x, sc_w_in, sc_conv_w, sc_conv_b, sc_w_out, lru_w_in, lru_b_in, lru_conv_w, lru_conv_b, lru_w_gate, lru_b_gate, lru_lambda, lru_w_out, ffn_w_up, ffn_conv_w, ffn_conv_b, ffn_w_down, ln_g, ln_b, loss_target, m_sc_w_in, m_sc_conv_w, m_sc_conv_b, m_sc_w_out, m_lru_w_in, m_lru_b_in, m_lru_conv_w, m_lru_conv_b, m_lru_w_gate, m_lru_b_gate, m_lru_lambda, m_lru_w_out, m_ffn_w_up, m_ffn_conv_w, m_ffn_conv_b, m_ffn_w_down, m_ln_g, m_ln_b, v_sc_w_in, v_sc_conv_w, v_sc_conv_b, v_sc_w_out, v_lru_w_in, v_lru_b_in, v_lru_conv_w, v_lru_conv_b, v_lru_w_gate, v_lru_b_gate, v_lru_lambda, v_lru_w_out, v_ffn_w_up, v_ffn_conv_w, v_ffn_conv_b, v_ffn_w_down, v_ln_g, v_ln_b):
    w = dict(zip(WEIGHTS, (sc_w_in, sc_conv_w, sc_conv_b, sc_w_out, lru_w_in, lru_b_in, lru_conv_w, lru_conv_b,
                           lru_w_gate, lru_b_gate, lru_lambda, lru_w_out, ffn_w_up, ffn_conv_w, ffn_conv_b,
                           ffn_w_down, ln_g, ln_b)))
    mom = dict(zip(WEIGHTS, (m_sc_w_in, m_sc_conv_w, m_sc_conv_b, m_sc_w_out, m_lru_w_in, m_lru_b_in, m_lru_conv_w,
                             m_lru_conv_b, m_lru_w_gate, m_lru_b_gate, m_lru_lambda, m_lru_w_out, m_ffn_w_up,
                             m_ffn_conv_w, m_ffn_conv_b, m_ffn_w_down, m_ln_g, m_ln_b)))
    vel = dict(zip(WEIGHTS, (v_sc_w_in, v_sc_conv_w, v_sc_conv_b, v_sc_w_out, v_lru_w_in, v_lru_b_in, v_lru_conv_w,
                             v_lru_conv_b, v_lru_w_gate, v_lru_b_gate, v_lru_lambda, v_lru_w_out, v_ffn_w_up,
                             v_ffn_conv_w, v_ffn_conv_b, v_ffn_w_down, v_ln_g, v_ln_b)))
    bd, seq, d = x.shape
    t = bd * seq
    small_shapes = {name: w[name].shape for name, _ in SMALL}

    w_pack = _pack_local(w, small_shapes)
    gate_shape = w["lru_w_gate"].shape
    bufs = {(n, l): (_place(w[n], l, k, BF16, "place_w"), k)
            for n, k in GATHER_KIND.items() for l in range(w[n].shape[0])}
    bufs["gate"] = (_place(w["lru_w_gate"].reshape(1, -1, gate_shape[-1]), 0, "lead", BF16, "place_w"), "lead")
    bufs["pack"] = (_place(w_pack[None], 0, "lead", F32, "place_w"), "lead")

    def layer_keys(i):
        mixer = ("sc_w_in", "sc_w_out") if i % 2 == 0 else ("lru_w_in", "lru_w_out")
        return [(mixer[0], i // 2), (mixer[1], i // 2)], [("ffn_w_up", i), ("ffn_w_down", i)]

    def gathered(keys, arrays):
        for key, arr in zip(keys, arrays):
            bufs[key] = (arr, bufs[key][1])

    def wt(name, l):
        arr = bufs[(name, l)][0]
        return arr.reshape(1, -1, arr.shape[-1])

    first = layer_keys(0)[0] + layer_keys(0)[1] + ["gate", "pack"]
    gathered(first, _all_gather([bufs[k][0] for k in first], [bufs[k][1] for k in first]))
    full = _unpack_slots(bufs["pack"][0], small_shapes)
    full["sc_conv_b"] = sc_conv_b
    full["ffn_conv_b"] = ffn_conv_b
    wg_full = jnp.moveaxis(bufs["gate"][0].reshape((N_CHIPS,) + gate_shape), 0, -2)
    wg_full = wg_full.reshape(wg_full.shape[:-2] + (2 * LRU_BLOCK,))
    f = N_CHIPS * w["ffn_w_down"].shape[1]
    rw = N_CHIPS * w["lru_w_out"].shape[1]

    x0 = x.reshape(t, d)
    xb = x0.astype(BF16)
    cur, cur_b = x0, xb
    saved = []

    for i in range(DEPTH):
        j = i // 2
        s = {"xb": cur_b}
        if i % 2 == 0:
            h = _mm_nn(cur_b, wt("sc_w_in", j), 0, None, 3 * d, "sc_in")
            q = _sc_fwd(h, full["sc_conv_w"][j], full["sc_conv_b"][j][None], seq, "sc_fwd")
            z1, x1, x1b = _mm_nn_ln(q, wt("sc_w_out", j), 0, cur, full["ln_g"][i, 0][None], full["ln_b"][i, 0][None],
                                    "sc_out_ln")
        else:
            h = _mm_nn(cur_b, wt("lru_w_in", j), 0, full["lru_b_in"][j][None], 2 * rw, "lru_in")
            hs, q = _lru_fwd(h, full["lru_conv_w"][j], full["lru_conv_b"][j][None], wg_full[j],
                             full["lru_b_gate"][j], full["lru_lambda"][j][None], seq, "lru_fwd")
            s["hs"] = hs
            z1, x1, x1b = _mm_nn_ln(q, wt("lru_w_out", j), 0, cur, full["ln_g"][i, 0][None],
                                    full["ln_b"][i, 0][None], "lru_out_ln")
        s.update(h=h, q=q, z1=z1, x1b=x1b)
        mixer_next, ffn_next = layer_keys(i + 1) if i + 1 < DEPTH else ([], [])
        h3, pre3, act, *arrived = _ffn_up(x1b, wt("ffn_w_up", i), 0, full["ffn_conv_w"][i],
                                          full["ffn_conv_b"][i][None], seq, "ffn_up",
                                          rider=[bufs[k] for k in ffn_next])
        gathered(ffn_next, arrived)
        z2, x2, x2b, *arrived = _mm_nn_ln(act, wt("ffn_w_down", i), 0, x1, full["ln_g"][i, 1][None],
                                          full["ln_b"][i, 1][None], "ffn_down_ln",
                                          rider=[bufs[k] for k in mixer_next])
        gathered(mixer_next, arrived)
        s.update(h3=h3, pre3=pre3, act=act, z2=z2)
        saved.append(s)
        cur, cur_b = x2, x2b

    dcur, loss_parts = _loss_bwd(cur, loss_target.reshape(t, d))
    loss = lax.psum(jnp.sum(loss_parts), MESH_AXES)

    def pair_sum(p, kind):
        lead = kind == "lead"
        p4 = p.reshape(N_CHIPS if lead else 1, 2, -1, p.shape[-1])
        chip_sum = _add_pair(p4, _d2d_stream(p4, True, "rs_swap"), "rs_add_pair")
        return chip_sum if lead else chip_sum[0]

    gp = {n: [None] * w[n].shape[0] for n, _ in SMALL}
    gp["lru_w_gate"] = [None] * gate_shape[0]
    landed, pending = {}, []
    for i in reversed(range(DEPTH)):
        j = i // 2
        s = saved[i]
        mixer_keys, ffn_keys = layer_keys(i)
        dz2, dz2b, dg, db = _ln_bwd(dcur, s["z2"], full["ln_g"][i, 1][None], "ln_bwd")
        gp["ln_g"][i] = [None, dg[0]]
        gp["ln_b"][i] = [None, db[0]]
        p_down = _mm_tn(s["act"], dz2b[None], f // 2, d, "ffn_down_dw").reshape(N_CHIPS, -1, d)
        dh3, dcwg, dcwv, dcbg, dcbv, *lands = _ffn_down_bwd(
            dz2b, wt("ffn_w_down", i), 0, s["h3"], s["pre3"], full["ffn_conv_w"][i], seq, "ffn_down_bwd",
            scatter=[(chip_sum, kind) for _, chip_sum, kind in pending])
        landed.update({key: land for (key, _, _), land in zip(pending, lands)})
        gp["ffn_conv_w"][i] = jnp.concatenate([dcwg, dcwv], axis=1)
        gp["ffn_conv_b"][i] = jnp.concatenate([dcbg[0], dcbv[0]])
        dx1 = _mm_nt_res(dh3, wt("ffn_w_up", i), 0, dz2, f // 2, "ffn_up_dx")
        p_up = _mm_tn(s["x1b"], dh3, d, f // 2, "ffn_up_dw")
        dz1, dz1b, dg, db = _ln_bwd(dx1, s["z1"], full["ln_g"][i, 0][None], "ln_bwd")
        gp["ln_g"][i][0] = dg[0]
        gp["ln_b"][i][0] = db[0]
        gp["ln_g"][i] = jnp.stack(gp["ln_g"][i])
        gp["ln_b"][i] = jnp.stack(gp["ln_b"][i])
        if i % 2 == 0:
            dq = _mm_nt(dz1b, wt("sc_w_out", j), 0, d, "sc_out_dx")
            p_out = _mm_tn(s["q"], dz1b[None], d, d, "sc_out_dw")
            dh3, dcw, dcb = _sc_bwd(s["h"], dq, full["sc_conv_w"][j], full["sc_conv_b"][j][None], seq, "sc_bwd")
            gp["sc_conv_w"][j] = dcw
            gp["sc_conv_b"][j] = dcb[0]
            dcur = _mm_nt_res(dh3, wt("sc_w_in", j), 0, dz1, d, "sc_in_dx")
            p_in = _mm_tn(s["xb"], dh3, d, d, "sc_in_dw")
        else:
            dq = _mm_nt(dz1b, wt("lru_w_out", j), 0, rw, "lru_out_dx")
            p_out = _mm_tn(s["q"], dz1b[None], rw, d, "lru_out_dw")
            dh3, dbin, dcw, dcb, dwg, dbg, dlam = _lru_bwd(
                s["h"], s["hs"], dq, full["lru_conv_w"][j], full["lru_conv_b"][j][None], wg_full[j],
                full["lru_b_gate"][j], full["lru_lambda"][j][None], seq, "lru_bwd")
            gp["lru_b_in"][j] = dbin[0]
            gp["lru_conv_w"][j] = dcw
            gp["lru_conv_b"][j] = dcb[0]
            gp["lru_w_gate"][j] = dwg
            gp["lru_b_gate"][j] = dbg
            gp["lru_lambda"][j] = dlam[0]
            dcur = _mm_nt_res(dh3, wt("lru_w_in", j), 0, dz1, rw, "lru_in_dx")
            p_in = _mm_tn(s["xb"], dh3, d, rw, "lru_in_dw")
        layer_partials = ((mixer_keys[0], p_in, "col"), (mixer_keys[1], p_out.reshape(N_CHIPS, -1, d), "lead"),
                          (ffn_keys[0], p_up, "col"), (ffn_keys[1], p_down, "lead"))
        pending = [(key, pair_sum(p, kind), kind) for key, p, kind in layer_partials]
    grad_x = dcur.reshape(bd, seq, d)
    gp = {n: jnp.stack(v) for n, v in gp.items()}

    gate = gp["lru_w_gate"]
    gate = jnp.moveaxis(gate.reshape(gate.shape[:-1] + (N_CHIPS, gate_shape[-1])), -2, 0)
    gate = gate.astype(BF16).reshape(N_CHIPS, -1, gate_shape[-1])
    pending += [("gate", pair_sum(gate, "lead"), "lead"),
                ("pack", pair_sum(_pack_slots(gp, small_shapes), "lead"), "lead")]
    lands = _rs_scatter([(chip_sum, kind) for _, chip_sum, kind in pending])
    landed.update({key: land for (key, _, _), land in zip(pending, lands)})

    def update(keys, w3, m3, v3):
        g_mine = jnp.stack([_add_chips(landed[k], "rs_add_chips") for k in keys])
        g_sib = _d2d_stream(g_mine.reshape(1, 1, -1, g_mine.shape[-1]), False, "rs_share").reshape(g_mine.shape)
        return _adamw(w3, g_mine, g_sib, m3, v3, "adamw")

    g_out, d_out, m_out, v_out = {}, {}, {}, {}
    for n in GATHER_KIND:
        outs = update([(n, l) for l in range(w[n].shape[0])], w[n], mom[n], vel[n])
        g_out[n], d_out[n], m_out[n], v_out[n] = outs
    as_rows = lambda a: a.reshape(1, -1, a.shape[-1])
    outs = update(["gate"], as_rows(w["lru_w_gate"]), as_rows(mom["lru_w_gate"]), as_rows(vel["lru_w_gate"]))
    g_out["lru_w_gate"], d_out["lru_w_gate"], m_out["lru_w_gate"], v_out["lru_w_gate"] = (
        o.reshape(gate_shape) for o in outs)
    packs = update(["pack"], w_pack[None], _pack_local(mom, small_shapes)[None], _pack_local(vel, small_shapes)[None])
    for dst, pack in zip((g_out, d_out, m_out, v_out), packs):
        dst.update(_unpack_local(pack[0], small_shapes))

    return (loss, grad_x, *[g_out[n] for n in WEIGHTS], *[d_out[n] for n in WEIGHTS],
            *[m_out[n] for n in WEIGHTS], *[v_out[n] for n in WEIGHTS])
```

```python
import math

import jax
import jax.numpy as jnp
from jax import lax
from jax.experimental import pallas as pl
from jax.experimental.pallas import tpu as pltpu

F32 = jnp.float32
BF16 = jnp.bfloat16

DEPTH = 4
LRU_HEADS = 10
LRU_BLOCK = 128
LRU_C = 8.0
LN_EPS = 1e-5
ALPHA = (2.0 * DEPTH) ** 0.25
ADAM_LR, ADAM_B1, ADAM_B2, ADAM_EPS, ADAM_WD, ADAM_STEP = 0.001, 0.9, 0.999, 1e-08, 0.01, 10
N_CHIPS = 4
MESH_AXES = ("x", "y", "c")

VMEM_LIMIT_BYTES = 48 * 1024 * 1024
TM_MM = 512
TM_RES = 1024
TT_MM = 1024
TM_SC = 256
FFN_COL_BLOCKS = 2
RC = 128
LANE = 128
MXU_COLS = 256
TS_LRU = 128
TM_LN = 512
ELEMS_PER_BLOCK = 256 * 1024
STREAM_ELEMS_PER_BLOCK = 1024 * 1024
SUB = 8
SUB16 = 16


def _cp(*sem):
    return pltpu.CompilerParams(dimension_semantics=sem, vmem_limit_bytes=VMEM_LIMIT_BYTES)


def _sigmoid(v):
    return 1.0 / (1.0 + jnp.exp(-v))


def _softplus(v):
    e = jnp.exp(-jnp.abs(v))
    log1p = jnp.where(e < 1e-3, e * (1.0 - e * (0.5 - e * (1.0 / 3.0))), jnp.log(1.0 + e))
    return jnp.maximum(v, 0.0) + log1p


def _one_minus_exp(v):
    series = -v * (1.0 + v * (0.5 + v * (1.0 / 6.0 + v * (1.0 / 24.0))))
    return jnp.where(v > -0.02, series, 1.0 - jnp.exp(v))


def _gelu_and_grad(v):
    k = math.sqrt(2.0 / math.pi)
    t = jnp.tanh(k * (v + 0.044715 * v * v * v))
    val = 0.5 * v * (1.0 + t)
    grad = 0.5 * (1.0 + t) + 0.5 * v * (1.0 - t * t) * k * (1.0 + 3.0 * 0.044715 * v * v)
    return val, grad


def _down(ext, k, n_head):
    if k:
        ext = pltpu.roll(ext, k, 0)
    return ext[n_head:]


def _up(ext, k, n):
    if k:
        ext = pltpu.roll(ext, ext.shape[0] - k, 0)
    return ext[:n]


def _row(ref, k):
    return ref[k:k + 1, :]


def _colsum(v):
    return jnp.sum(v, axis=0, keepdims=True)


def _mm_nn(a, w3, l, bias, tn, name):
    m, k = a.shape
    n = w3.shape[2]
    tm = min(TM_MM, m)
    has_bias = bias is not None

    def body(*refs):
        if has_bias:
            a_ref, w_ref, b_ref, o_ref = refs
        else:
            a_ref, w_ref, o_ref = refs
        acc = jnp.dot(a_ref[...], w_ref[...], preferred_element_type=F32)
        if has_bias:
            acc = acc + b_ref[...]
        o_ref[...] = acc.astype(o_ref.dtype)

    in_specs = [pl.BlockSpec((tm, k), lambda i, j: (i, 0)),
                pl.BlockSpec((None, k, tn), lambda i, j: (l, 0, j))]
    args = [a, w3]
    if has_bias:
        in_specs.append(pl.BlockSpec((1, tn), lambda i, j: (0, j)))
        args.append(bias)
    return pl.pallas_call(
        body, name=name, grid=(m // tm, n // tn), in_specs=in_specs,
        out_specs=pl.BlockSpec((tm, tn), lambda i, j: (i, j)),
        out_shape=jax.ShapeDtypeStruct((m, n), BF16),
        compiler_params=_cp("parallel", "arbitrary"))(*args)


def _mm_nn_ln(a, w3, l, xres, g, b, name, rider=()):
    m, k = a.shape
    n = w3.shape[2]
    tm = min(TM_MM, m)
    bufs, rkinds, rspecs, rshapes, rsems = _rider_args(rider)
    nr = len(bufs)

    def body(*refs):
        a_ref, w_ref, x_ref, g_ref, b_ref = refs[:5]
        z_ref, xn_ref, xb_ref = refs[5 + nr:8 + nr]
        rout, sems = refs[8 + nr:8 + 2 * nr], refs[8 + 2 * nr:]
        if nr:
            @pl.when(pl.program_id(0) == 0)
            def _():
                _rider_start(rout, rkinds, *sems)

        y = jnp.dot(a_ref[...], w_ref[...], preferred_element_type=F32)
        z = ALPHA * x_ref[...] + y
        mu = jnp.mean(z, axis=-1, keepdims=True)
        zc = z - mu
        var = jnp.mean(zc * zc, axis=-1, keepdims=True)
        xn = zc * lax.rsqrt(var + LN_EPS) * g_ref[...] + b_ref[...]
        z_ref[...] = z
        xn_ref[...] = xn
        xb_ref[...] = xn.astype(BF16)
        if nr:
            @pl.when(pl.program_id(0) == m // tm - 1)
            def _():
                _rider_wait(rout, rkinds, *sems)

    row = pl.BlockSpec((tm, n), lambda i: (i, 0))
    vec = pl.BlockSpec((1, n), lambda i: (0, 0))
    return pl.pallas_call(
        body, name=name, grid=(m // tm,),
        in_specs=[pl.BlockSpec((tm, k), lambda i: (i, 0)),
                  pl.BlockSpec((None, k, n), lambda i: (l, 0, 0)), row, vec, vec] + rspecs,
        out_specs=[row, row, row] + rspecs,
        out_shape=[jax.ShapeDtypeStruct((m, n), F32), jax.ShapeDtypeStruct((m, n), F32),
                   jax.ShapeDtypeStruct((m, n), BF16)] + rshapes,
        input_output_aliases={5 + u: 3 + u for u in range(nr)},
        scratch_shapes=rsems,
        compiler_params=_cp("arbitrary"))(a, w3, xres, g, b, *bufs)


def _mm_nt(a, w3, l, tk, name):
    m, n = a.shape
    kd = w3.shape[1]
    tm = min(TM_MM, m)

    def body(a_ref, w_ref, o_ref):
        o_ref[...] = lax.dot_general(a_ref[...], w_ref[...], (((1,), (1,)), ((), ())),
                                     preferred_element_type=F32).astype(o_ref.dtype)

    return pl.pallas_call(
        body, name=name, grid=(m // tm, kd // tk),
        in_specs=[pl.BlockSpec((tm, n), lambda i, j: (i, 0)),
                  pl.BlockSpec((None, tk, n), lambda i, j: (l, j, 0))],
        out_specs=pl.BlockSpec((tm, tk), lambda i, j: (i, j)),
        out_shape=jax.ShapeDtypeStruct((m, kd), BF16),
        compiler_params=_cp("parallel", "arbitrary"))(a, w3)


def _mm_nt_res(dh3, w3, l, dz, tc, name):
    g, m, cg = dh3.shape
    kd = w3.shape[1]
    ncg = cg // tc
    nk = g * ncg
    tm = min(TM_RES, m)

    def body(a_ref, w_ref, dz_ref, o_ref, acc):
        k = pl.program_id(1)

        @pl.when(k == 0)
        def _():
            acc[...] = ALPHA * dz_ref[...]

        acc[...] += lax.dot_general(a_ref[...], w_ref[...], (((1,), (1,)), ((), ())),
                                    preferred_element_type=F32)

        @pl.when(k == nk - 1)
        def _():
            o_ref[...] = acc[...]

    return pl.pallas_call(
        body, name=name, grid=(m // tm, nk),
        in_specs=[pl.BlockSpec((None, tm, tc), lambda i, k: (k // ncg, i, k % ncg)),
                  pl.BlockSpec((None, kd, tc), lambda i, k: (l, 0, k)),
                  pl.BlockSpec((tm, kd), lambda i, k: (i, 0))],
        out_specs=pl.BlockSpec((tm, kd), lambda i, k: (i, 0)),
        out_shape=jax.ShapeDtypeStruct((m, kd), F32),
        scratch_shapes=[pltpu.VMEM((tm, kd), F32)],
        compiler_params=_cp("parallel", "arbitrary"))(dh3, w3, dz)


def _mm_tn(a, b3, tka, tnb, name):
    t, ka = a.shape
    g, _, cg = b3.shape
    ncg = cg // tnb
    tt = min(TT_MM, t)
    nt = t // tt

    def body(a_ref, b_ref, o_ref, acc):
        s = pl.program_id(2)

        @pl.when(s == 0)
        def _():
            acc[...] = jnp.zeros_like(acc)

        acc[...] += lax.dot_general(a_ref[...], b_ref[...], (((0,), (0,)), ((), ())),
                                    preferred_element_type=F32)

        @pl.when(s == nt - 1)
        def _():
            o_ref[...] = acc[...].astype(o_ref.dtype)

    return pl.pallas_call(
        body, name=name, grid=(ka // tka, g * ncg, nt),
        in_specs=[pl.BlockSpec((tt, tka), lambda i, j, s: (s, i)),
                  pl.BlockSpec((None, tt, tnb), lambda i, j, s: (j // ncg, s, j % ncg))],
        out_specs=pl.BlockSpec((tka, tnb), lambda i, j, s: (i, j)),
        out_shape=jax.ShapeDtypeStruct((ka, g * cg), BF16),
        scratch_shapes=[pltpu.VMEM((tka, tnb), F32)],
        compiler_params=_cp("parallel", "parallel", "arbitrary"))(a, b3)


def _ln_bwd(dxn, z, g, name):
    m, d = z.shape
    tm = min(TM_LN, m)

    def body(dx_ref, z_ref, g_ref, dz_ref, dzb_ref, dg_ref, db_ref):
        @pl.when(pl.program_id(0) == 0)
        def _():
            dg_ref[...] = jnp.zeros_like(dg_ref)
            db_ref[...] = jnp.zeros_like(db_ref)

        zz = z_ref[...]
        dx = dx_ref[...]
        mu = jnp.mean(zz, axis=-1, keepdims=True)
        zc = zz - mu
        var = jnp.mean(zc * zc, axis=-1, keepdims=True)
        rstd = lax.rsqrt(var + LN_EPS)
        xh = zc * rstd
        dg_ref[...] += _colsum(dx * xh)
        db_ref[...] += _colsum(dx)
        dxh = dx * g_ref[...]
        m1 = jnp.mean(dxh, axis=-1, keepdims=True)
        m2 = jnp.mean(dxh * xh, axis=-1, keepdims=True)
        dz = rstd * (dxh - m1 - xh * m2)
        dz_ref[...] = dz
        dzb_ref[...] = dz.astype(BF16)

    row = pl.BlockSpec((tm, d), lambda i: (i, 0))
    vec = pl.BlockSpec((1, d), lambda i: (0, 0))
    return pl.pallas_call(
        body, name=name, grid=(m // tm,), in_specs=[row, row, vec],
        out_specs=[row, row, vec, vec],
        out_shape=[jax.ShapeDtypeStruct((m, d), F32), jax.ShapeDtypeStruct((m, d), BF16),
                   jax.ShapeDtypeStruct((1, d), F32), jax.ShapeDtypeStruct((1, d), F32)],
        compiler_params=_cp("arbitrary"))(dxn, z, g)


def _loss_bwd(y, target):
    m, d = y.shape
    tm = min(TM_LN, m)

    def body(y_ref, t_ref, dy_ref, ls_ref):
        @pl.when(pl.program_id(0) == 0)
        def _():
            ls_ref[...] = jnp.zeros_like(ls_ref)

        e = y_ref[...] - t_ref[...]
        dy_ref[...] = e * (1.0 / d)
        ls_ref[...] += _colsum(e * e) * (0.5 / d)

    row = pl.BlockSpec((tm, d), lambda i: (i, 0))
    return pl.pallas_call(
        body, name="loss_bwd", grid=(m // tm,), in_specs=[row, row],
        out_specs=[row, pl.BlockSpec((1, d), lambda i: (0, 0))],
        out_shape=[jax.ShapeDtypeStruct((m, d), F32), jax.ShapeDtypeStruct((1, d), F32)],
        compiler_params=_cp("arbitrary"))(y, target)


def _sc_fwd(h, cw, cb, seq, name):
    t, d3 = h.shape
    d = d3 // 3
    tm = min(TM_SC, seq)

    def body(hb_ref, hc_ref, hv_ref, cw_ref, cb_ref, q_ref, carry):
        i = pl.program_id(0)

        @pl.when(lax.rem(i * tm, seq) == 0)
        def _():
            carry[...] = jnp.zeros_like(carry)

        p = hc_ref[...].astype(F32) * hv_ref[...].astype(F32)
        ext = jnp.concatenate([carry[...], p], axis=0)
        u = cb_ref[...] + _row(cw_ref, 0) * _down(ext, 2, SUB) + _row(cw_ref, 1) * _down(ext, 1, SUB) \
            + _row(cw_ref, 2) * p
        q_ref[...] = (hb_ref[...].astype(F32) * u).astype(BF16)
        carry[...] = p[tm - SUB:, :]

    blk = lambda c: pl.BlockSpec((tm, d), lambda i: (i, c))
    return pl.pallas_call(
        body, name=name, grid=(t // tm,),
        in_specs=[blk(0), blk(1), blk(2), pl.BlockSpec((3, d), lambda i: (0, 0)),
                  pl.BlockSpec((1, d), lambda i: (0, 0))],
        out_specs=pl.BlockSpec((tm, d), lambda i: (i, 0)),
        out_shape=jax.ShapeDtypeStruct((t, d), BF16),
        scratch_shapes=[pltpu.VMEM((SUB, d), F32)],
        compiler_params=_cp("arbitrary"))(h, h, h, cw, cb)


def _sc_bwd(h, dq, cw, cb, seq, name):
    t, d3 = h.shape
    d = d3 // 3
    tm = min(TM_SC, seq)
    nt = t // tm
    hpt = tm // SUB16

    def body(hb_ref, hc_ref, hv_ref, hch_ref, hvh_ref, dq_ref, cw_ref, cb_ref,
             dh_ref, dcw_ref, dcb_ref, carry):
        i = pl.program_id(0)
        ri = nt - 1 - i

        @pl.when(i == 0)
        def _():
            dcw_ref[...] = jnp.zeros_like(dcw_ref)
            dcb_ref[...] = jnp.zeros_like(dcb_ref)

        @pl.when(lax.rem((ri + 1) * tm, seq) == 0)
        def _():
            carry[...] = jnp.zeros_like(carry)

        keep = jnp.where(lax.rem(ri * tm, seq) == 0, 0.0, 1.0)
        gb = hb_ref[...].astype(F32)
        gc = hc_ref[...].astype(F32)
        v = hv_ref[...].astype(F32)
        p = gc * v
        p_head = hch_ref[...].astype(F32) * hvh_ref[...].astype(F32) * keep
        ext = jnp.concatenate([p_head, p], axis=0)
        pm2 = _down(ext, 2, SUB16)
        pm1 = _down(ext, 1, SUB16)
        u = cb_ref[...] + _row(cw_ref, 0) * pm2 + _row(cw_ref, 1) * pm1 + _row(cw_ref, 2) * p
        dqf = dq_ref[...].astype(F32)
        du = dqf * gb
        dcb_ref[...] += _colsum(du)
        dcw_ref[0:1, :] += _colsum(du * pm2)
        dcw_ref[1:2, :] += _colsum(du * pm1)
        dcw_ref[2:3, :] += _colsum(du * p)
        ext2 = jnp.concatenate([du, carry[...]], axis=0)
        dp = _row(cw_ref, 2) * du + _row(cw_ref, 1) * _up(ext2, 1, tm) + _row(cw_ref, 0) * _up(ext2, 2, tm)
        carry[...] = du[0:SUB, :]
        dh_ref[0] = (dqf * u).astype(BF16)
        dh_ref[1] = (dp * v).astype(BF16)
        dh_ref[2] = (dp * gc).astype(BF16)

    blk = lambda c: pl.BlockSpec((tm, d), lambda i: (nt - 1 - i, c))
    head = lambda c: pl.BlockSpec((SUB16, d), lambda i: (jnp.maximum((nt - 1 - i) * hpt - 1, 0), c))
    vec = lambda r: pl.BlockSpec((r, d), lambda i: (0, 0))
    return pl.pallas_call(
        body, name=name, grid=(nt,),
        in_specs=[blk(0), blk(1), blk(2), head(1), head(2),
                  pl.BlockSpec((tm, d), lambda i: (nt - 1 - i, 0)), vec(3), vec(1)],
        out_specs=[pl.BlockSpec((3, tm, d), lambda i: (0, nt - 1 - i, 0)), vec(3), vec(1)],
        out_shape=[jax.ShapeDtypeStruct((3, t, d), BF16), jax.ShapeDtypeStruct((3, d), F32),
                   jax.ShapeDtypeStruct((1, d), F32)],
        scratch_shapes=[pltpu.VMEM((SUB, d), F32)],
        compiler_params=_cp("arbitrary"))(h, h, h, h, h, dq, cw, cb)


def _fold(v):
    return jnp.sum(v.reshape(v.shape[0] // SUB, SUB, v.shape[1]), axis=0)


def _ffn_up(xb, w3, l, cw, cb, seq, name, rider=()):
    t, d = xb.shape
    f = w3.shape[2] // 2
    tc = f // FFN_COL_BLOCKS
    tm = min(TM_MM, seq)
    nc = FFN_COL_BLOCKS
    bufs, rkinds, rspecs, rshapes, rsems = _rider_args(rider)
    nr = len(bufs)

    def body(*refs):
        x_ref, wg_ref, wv_ref, cwg_ref, cwv_ref, cbg_ref, cbv_ref = refs[:7]
        h_ref, pre_ref, act_ref = refs[7 + nr:10 + nr]
        rout = refs[10 + nr:10 + 2 * nr]
        eg, ev = refs[10 + 2 * nr:12 + 2 * nr]
        sems = refs[12 + 2 * nr:]
        i = pl.program_id(1)
        if nr:
            @pl.when(jnp.logical_and(pl.program_id(0) == 0, i == 0))
            def _():
                _rider_start(rout, rkinds, *sems)

        @pl.when(lax.rem(i * tm, seq) == 0)
        def _():
            eg[0:SUB, :] = jnp.zeros((SUB, tc), F32)
            ev[0:SUB, :] = jnp.zeros((SUB, tc), F32)

        xx = x_ref[...]

        def matmul(lo, hi):
            eg[SUB:, lo:hi] = jnp.dot(xx, wg_ref[:, lo:hi], preferred_element_type=F32)
            ev[SUB:, lo:hi] = jnp.dot(xx, wv_ref[:, lo:hi], preferred_element_type=F32)

        def epilogue(lo, hi):
            for c0 in range(lo, hi, LANE):
                cols = slice(c0, c0 + LANE)
                taps = [[ref[k:k + 1, cols] for k in range(3)] + [bref[:, cols]]
                        for ref, bref in ((cwg_ref, cbg_ref), (cwv_ref, cbv_ref))]
                for r0 in range(0, tm, RC):
                    rows = slice(r0, r0 + RC)
                    pres = []
                    for half, e_ref in enumerate((eg, ev)):
                        w0, w1, w2, bias = taps[half]
                        e = e_ref[r0:r0 + RC + SUB, cols]
                        cur = e[SUB:]
                        pre = bias + w0 * _down(e, 2, SUB) + w1 * _down(e, 1, SUB) + w2 * cur
                        h_ref[half, rows, cols] = cur.astype(BF16)
                        pre_ref[half, rows, cols] = pre.astype(BF16)
                        pres.append(pre)
                    act_ref[rows, cols] = (pres[0] * _sigmoid(pres[0]) * pres[1]).astype(BF16)

        blocks = [(lo, min(lo + MXU_COLS, tc)) for lo in range(0, tc, MXU_COLS)]
        matmul(*blocks[0])
        for b, blk in enumerate(blocks):
            if b + 1 < len(blocks):
                matmul(*blocks[b + 1])
            epilogue(*blk)
        eg[0:SUB, :] = eg[tm:tm + SUB, :]
        ev[0:SUB, :] = ev[tm:tm + SUB, :]
        if nr:
            @pl.when(jnp.logical_and(pl.program_id(0) == nc - 1, i == t // tm - 1))
            def _():
                _rider_wait(rout, rkinds, *sems)

    wspec = lambda off: pl.BlockSpec((None, d, tc), lambda j, i: (l, 0, j + off))
    vec = lambda r, off: pl.BlockSpec((r, tc), lambda j, i: (0, j + off))
    pair = pl.BlockSpec((2, tm, tc), lambda j, i: (0, i, j))
    return pl.pallas_call(
        body, name=name, grid=(nc, t // tm),
        in_specs=[pl.BlockSpec((tm, d), lambda j, i: (i, 0)), wspec(0), wspec(nc),
                  vec(3, 0), vec(3, nc), vec(1, 0), vec(1, nc)] + rspecs,
        out_specs=[pair, pair, pl.BlockSpec((tm, tc), lambda j, i: (i, j))] + rspecs,
        out_shape=[jax.ShapeDtypeStruct((2, t, f), BF16), jax.ShapeDtypeStruct((2, t, f), BF16),
                   jax.ShapeDtypeStruct((t, f), BF16)] + rshapes,
        input_output_aliases={7 + u: 3 + u for u in range(nr)},
        scratch_shapes=[pltpu.VMEM((tm + SUB, tc), F32), pltpu.VMEM((tm + SUB, tc), F32)] + rsems,
        compiler_params=_cp("arbitrary", "arbitrary"))(xb, w3, w3, cw, cw, cb, cb, *bufs)


def _ffn_down_bwd(dzb, wd3, l, h3, pre3, cw, seq, name, scatter=()):
    t, d = dzb.shape
    f = wd3.shape[1]
    tc = f // FFN_COL_BLOCKS
    tm = min(TM_MM, seq)
    nt = t // tm
    nc = FFN_COL_BLOCKS
    srcs, skinds, sspecs, sshapes, ssems = _scatter_args(scatter)
    ns = len(srcs)

    def body(*refs):
        dz_ref, wd_ref, h_ref, pre_ref, cwg_ref, cwv_ref = refs[:6]
        sin = refs[6:6 + ns]
        dh_ref, dcwg_ref, dcwv_ref, dcbg_ref, dcbv_ref = refs[6 + ns:11 + ns]
        lands = refs[11 + ns:11 + 2 * ns]
        da_s, carry = refs[11 + 2 * ns:13 + 2 * ns]
        sems = refs[13 + 2 * ns:]
        i = pl.program_id(1)
        ri = nt - 1 - i
        if ns:
            @pl.when(jnp.logical_and(pl.program_id(0) == 0, i == 0))
            def _():
                for cp in _scatter_copies(sin, lands, skinds, *sems):
                    cp.start()

        @pl.when(i == 0)
        def _():
            for r in (dcwg_ref, dcwv_ref, dcbg_ref, dcbv_ref):
                r[...] = jnp.zeros_like(r)

        @pl.when(lax.rem((ri + 1) * tm, seq) == 0)
        def _():
            carry[...] = jnp.zeros_like(carry)

        dz = dz_ref[...]

        def matmul(lo, hi):
            da_s[:, lo:hi] = lax.dot_general(dz, wd_ref[lo:hi, :], (((1,), (1,)), ((), ())),
                                             preferred_element_type=F32)

        def epilogue(lo, hi):
            for c0 in range(lo, hi, LANE):
                cols = slice(c0, c0 + LANE)
                taps = [[ref[k:k + 1, cols] for k in range(3)] for ref in (cwg_ref, cwv_ref)]
                acc = [jnp.zeros((SUB, LANE), F32)] * 8
                for r0 in range(tm - RC, -1, -RC):
                    rows = slice(r0, r0 + RC)
                    da = da_s[rows, cols]
                    gp = pre_ref[0, rows, cols].astype(F32)
                    vp = pre_ref[1, rows, cols].astype(F32)
                    sg = _sigmoid(gp)
                    dpres = (da * vp * (sg * (1.0 + gp * (1.0 - sg))), da * (gp * sg))
                    for half in range(2):
                        w0, w1, w2 = taps[half]
                        dpre = dpres[half]
                        ext = jnp.concatenate([dpre, carry[half, :, cols]], axis=0)
                        u1 = _up(ext, 1, RC)
                        u2 = _up(ext, 2, RC)
                        carry[half, :, cols] = dpre[0:SUB]
                        dh_ref[half, rows, cols] = (w2 * dpre + w1 * u1 + w0 * u2).astype(BF16)
                        hh = h_ref[half, rows, cols].astype(F32)
                        for k, term in enumerate((hh * u2, hh * u1, hh * dpre, dpre)):
                            acc[4 * half + k] = acc[4 * half + k] + _fold(term)
                for half, (dcw_ref, dcb_ref) in enumerate(((dcwg_ref, dcbg_ref), (dcwv_ref, dcbv_ref))):
                    for k in range(3):
                        dcw_ref[k:k + 1, cols] += _colsum(acc[4 * half + k])
                    dcb_ref[:, cols] += _colsum(acc[4 * half + 3])

        blocks = [(lo, min(lo + MXU_COLS, tc)) for lo in range(0, tc, MXU_COLS)]
        matmul(*blocks[0])
        for b, blk in enumerate(blocks):
            if b + 1 < len(blocks):
                matmul(*blocks[b + 1])
            epilogue(*blk)
        if ns:
            @pl.when(jnp.logical_and(pl.program_id(0) == nc - 1, i == nt - 1))
            def _():
                for cp in _scatter_copies(sin, lands, skinds, *sems):
                    cp.wait()

    pair = pl.BlockSpec((2, tm, tc), lambda j, i: (0, nt - 1 - i, j))
    vec = lambda off: pl.BlockSpec((3, tc), lambda j, i: (0, j + off))
    acc_spec = lambda r: pl.BlockSpec((r, tc), lambda j, i: (0, j))
    return pl.pallas_call(
        body, name=name, grid=(nc, nt),
        in_specs=[pl.BlockSpec((tm, d), lambda j, i: (nt - 1 - i, 0)),
                  pl.BlockSpec((None, tc, d), lambda j, i: (l, j, 0)), pair, pair, vec(0), vec(nc)] + sspecs,
        out_specs=[pair, acc_spec(3), acc_spec(3), acc_spec(1), acc_spec(1)] + sspecs,
        out_shape=[jax.ShapeDtypeStruct((2, t, f), BF16), jax.ShapeDtypeStruct((3, f), F32),
                   jax.ShapeDtypeStruct((3, f), F32), jax.ShapeDtypeStruct((1, f), F32),
                   jax.ShapeDtypeStruct((1, f), F32)] + sshapes,
        scratch_shapes=[pltpu.VMEM((tm, tc), F32), pltpu.VMEM((2, SUB, tc), F32)] + ssems,
        compiler_params=_cp("arbitrary", "arbitrary"))(dzb, wd3, h3, pre3, cw, cw, *srcs)


def _lru_gates(xr, wg_ref, bg_ref):
    rs, gs = [], []
    for hd in range(LRU_HEADS):
        xh = xr[:, hd * LRU_BLOCK:(hd + 1) * LRU_BLOCK].astype(BF16)
        gt = jnp.dot(xh, wg_ref[hd], preferred_element_type=F32) + _row(bg_ref, hd)
        rs.append(gt[:, :LRU_BLOCK])
        gs.append(gt[:, LRU_BLOCK:])
    return jnp.concatenate(rs, axis=1), jnp.concatenate(gs, axis=1)


def _lru_coeffs(xr, wg_ref, bg_ref, lam_ref):
    gr, gi = _lru_gates(xr, wg_ref, bg_ref)
    r = _sigmoid(gr)
    ig = _sigmoid(gi)
    sp = _softplus(-lam_ref[...])
    log_a = -LRU_C * r * sp
    a = jnp.exp(log_a)
    mult = jnp.sqrt(_one_minus_exp(2.0 * log_a))
    return r, ig, sp, a, mult


def _lru_fwd(h, cw, cb, wg, bg, lam, seq, name):
    t, r2 = h.shape
    rw = r2 // 2
    ts = min(TS_LRU, seq)
    n8 = ts // SUB

    def body(hg_ref, hr_ref, cw_ref, cb_ref, wg_ref, bg_ref, lam_ref, hs_ref, y_ref,
             a_s, b_s, cconv, cstate):
        i = pl.program_id(0)

        @pl.when(lax.rem(i * ts, seq) == 0)
        def _():
            cconv[...] = jnp.zeros_like(cconv)
            cstate[...] = jnp.zeros_like(cstate)

        rin = hr_ref[...].astype(F32)
        ext = jnp.concatenate([cconv[...], rin], axis=0)
        xr = cb_ref[...]
        for k in range(4):
            xr = xr + _row(cw_ref, k) * _down(ext, 3 - k, SUB)
        cconv[...] = rin[ts - SUB:, :]
        _, ig, _, a, mult = _lru_coeffs(xr, wg_ref, bg_ref, lam_ref)
        a_s[...] = a
        b_s[...] = mult * (ig * xr)
        row = lax.broadcasted_iota(jnp.int32, (SUB, rw), 0)

        def step(j, carry):
            off = pl.multiple_of(j * SUB, SUB)
            a8 = a_s[pl.ds(off, SUB), :]
            b8 = b_s[pl.ds(off, SUB), :]
            for d in (1, 2, 4):
                m = row >= d
                b8 = jnp.where(m, a8 * pltpu.roll(b8, d, 0) + b8, b8)
                a8 = jnp.where(m, a8 * pltpu.roll(a8, d, 0), a8)
            h8 = a8 * carry + b8
            hs_ref[pl.ds(off, SUB), :] = h8
            return _colsum(jnp.where(row == SUB - 1, h8, 0.0))

        cstate[...] = lax.fori_loop(0, n8, step, cstate[...])
        gel, _ = _gelu_and_grad(hg_ref[...].astype(F32))
        y_ref[...] = (hs_ref[...] * gel).astype(BF16)

    full = lambda shp: pl.BlockSpec(shp, lambda i: (0,) * len(shp))
    return pl.pallas_call(
        body, name=name, grid=(t // ts,),
        in_specs=[pl.BlockSpec((ts, rw), lambda i: (i, 0)), pl.BlockSpec((ts, rw), lambda i: (i, 1)),
                  full((4, rw)), full((1, rw)), full(wg.shape), full(bg.shape), full((1, rw))],
        out_specs=[pl.BlockSpec((ts, rw), lambda i: (i, 0)), pl.BlockSpec((ts, rw), lambda i: (i, 0))],
        out_shape=[jax.ShapeDtypeStruct((t, rw), F32), jax.ShapeDtypeStruct((t, rw), BF16)],
        scratch_shapes=[pltpu.VMEM((ts, rw), F32), pltpu.VMEM((ts, rw), F32),
                        pltpu.VMEM((SUB, rw), F32), pltpu.VMEM((1, rw), F32)],
        compiler_params=_cp("arbitrary"))(h, h, cw, cb, wg, bg, lam)


def _lru_bwd(h, hs, dy, cw, cb, wg, bg, lam, seq, name):
    t, r2 = h.shape
    rw = r2 // 2
    ts = min(TS_LRU, seq)
    nt = t // ts
    n8 = ts // SUB
    hp16 = ts // SUB16
    hp8 = ts // SUB

    def body(hg_ref, hr_ref, hrh_ref, hs_ref, hsh_ref, dy_ref, cw_ref, cb_ref, wg_ref, bg_ref, lam_ref,
             dh_ref, dbin_ref, dcw_ref, dcb_ref, dwg_ref, dbg_ref, dlam_ref,
             a_s, g_s, l_s, c_lam, c_a, c_dxr):
        i = pl.program_id(0)
        ri = nt - 1 - i

        @pl.when(i == 0)
        def _():
            for r in (dbin_ref, dcw_ref, dcb_ref, dwg_ref, dbg_ref, dlam_ref):
                r[...] = jnp.zeros_like(r)

        @pl.when(lax.rem((ri + 1) * ts, seq) == 0)
        def _():
            c_lam[...] = jnp.zeros_like(c_lam)
            c_a[...] = jnp.zeros_like(c_a)
            c_dxr[...] = jnp.zeros_like(c_dxr)

        keep = jnp.where(lax.rem(ri * ts, seq) == 0, 0.0, 1.0)
        rin = hr_ref[...].astype(F32)
        ext = jnp.concatenate([hrh_ref[...].astype(F32) * keep, rin], axis=0)
        shifted = [_down(ext, 3 - k, SUB16) for k in range(4)]
        xr = cb_ref[...]
        for k in range(4):
            xr = xr + _row(cw_ref, k) * shifted[k]
        r, ig, sp, a, mult = _lru_coeffs(xr, wg_ref, bg_ref, lam_ref)
        gel, dgel = _gelu_and_grad(hg_ref[...].astype(F32))
        dyf = dy_ref[...].astype(F32)
        hsv = hs_ref[...]
        dg = dyf * hsv * dgel

        a_s[...] = _up(jnp.concatenate([a, c_a[...]], axis=0), 1, ts)
        g_s[...] = dyf * gel
        c_a[...] = a[0:SUB, :]
        row = lax.broadcasted_iota(jnp.int32, (SUB, rw), 0)

        def step(j, carry):
            off = pl.multiple_of((n8 - 1 - j) * SUB, SUB)
            a8 = a_s[pl.ds(off, SUB), :]
            b8 = g_s[pl.ds(off, SUB), :]
            for d in (1, 2, 4):
                m = row < SUB - d
                b8 = jnp.where(m, a8 * pltpu.roll(b8, SUB - d, 0) + b8, b8)
                a8 = jnp.where(m, a8 * pltpu.roll(a8, SUB - d, 0), a8)
            l8 = a8 * carry + b8
            l_s[pl.ds(off, SUB), :] = l8
            return _colsum(jnp.where(row == 0, l8, 0.0))

        c_lam[...] = lax.fori_loop(0, n8, step, c_lam[...])
        lamv = l_s[...]
        hs_prev = _down(jnp.concatenate([hsh_ref[...] * keep, hsv], axis=0), 1, SUB)
        da = lamv * hs_prev
        t1 = lamv * xr
        dmult = t1 * ig
        dig = t1 * mult
        dxr = lamv * mult * ig
        dla = da * a - dmult * (a * a) / mult
        dr = dla * (-LRU_C * sp)
        dlam_ref[...] += _colsum(dla * (-LRU_C) * r) * (-_sigmoid(-lam_ref[...]))
        dgr = dr * r * (1.0 - r)
        dgi = dig * ig * (1.0 - ig)
        parts = []
        for hd in range(LRU_HEADS):
            sl = slice(hd * LRU_BLOCK, (hd + 1) * LRU_BLOCK)
            dgt = jnp.concatenate([dgr[:, sl], dgi[:, sl]], axis=1)
            dbg_ref[hd:hd + 1, :] += _colsum(dgt)
            dgt16 = dgt.astype(BF16)
            parts.append(lax.dot_general(dgt16, wg_ref[hd], (((1,), (1,)), ((), ())),
                                         preferred_element_type=F32))
            dwg_ref[hd] += lax.dot_general(xr[:, sl].astype(BF16), dgt16, (((0,), (0,)), ((), ())),
                                           preferred_element_type=F32)
        dxr = dxr + jnp.concatenate(parts, axis=1)

        dcb_ref[...] += _colsum(dxr)
        for k in range(4):
            dcw_ref[k:k + 1, :] += _colsum(dxr * shifted[k])
        ext2 = jnp.concatenate([dxr, c_dxr[...]], axis=0)
        drb = _row(cw_ref, 3) * dxr
        for k in range(3):
            drb = drb + _row(cw_ref, k) * _up(ext2, 3 - k, ts)
        c_dxr[...] = dxr[0:SUB, :]
        dh_ref[0] = dg.astype(BF16)
        dh_ref[1] = drb.astype(BF16)
        dbin_ref[:, 0:rw] += _colsum(dg)
        dbin_ref[:, rw:] += _colsum(drb)

    rev = lambda c: pl.BlockSpec((ts, rw), lambda i: (nt - 1 - i, c))
    full = lambda shp: pl.BlockSpec(shp, lambda i: (0,) * len(shp))
    nh = LRU_HEADS
    return pl.pallas_call(
        body, name=name, grid=(nt,),
        in_specs=[rev(0), rev(1),
                  pl.BlockSpec((SUB16, rw), lambda i: (jnp.maximum((nt - 1 - i) * hp16 - 1, 0), 1)),
                  rev(0),
                  pl.BlockSpec((SUB, rw), lambda i: (jnp.maximum((nt - 1 - i) * hp8 - 1, 0), 0)),
                  rev(0), full((4, rw)), full((1, rw)), full(wg.shape), full(bg.shape), full((1, rw))],
        out_specs=[pl.BlockSpec((2, ts, rw), lambda i: (0, nt - 1 - i, 0)), full((1, r2)), full((4, rw)),
                   full((1, rw)), full((nh, LRU_BLOCK, 2 * LRU_BLOCK)), full((nh, 2 * LRU_BLOCK)), full((1, rw))],
        out_shape=[jax.ShapeDtypeStruct((2, t, rw), BF16), jax.ShapeDtypeStruct((1, r2), F32),
                   jax.ShapeDtypeStruct((4, rw), F32), jax.ShapeDtypeStruct((1, rw), F32),
                   jax.ShapeDtypeStruct((nh, LRU_BLOCK, 2 * LRU_BLOCK), F32),
                   jax.ShapeDtypeStruct((nh, 2 * LRU_BLOCK), F32), jax.ShapeDtypeStruct((1, rw), F32)],
        scratch_shapes=[pltpu.VMEM((ts, rw), F32), pltpu.VMEM((ts, rw), F32), pltpu.VMEM((ts, rw), F32),
                        pltpu.VMEM((1, rw), F32), pltpu.VMEM((SUB, rw), F32), pltpu.VMEM((SUB, rw), F32)],
        compiler_params=_cp("arbitrary"))(h, h, h, hs, hs, dy, cw, cb, wg, bg, lam)


def _row_tile(rows, cols, mult, elems=ELEMS_PER_BLOCK):
    cap = max(mult, elems // cols)
    best = None
    for cand in range(mult, min(rows, cap) + 1, mult):
        if rows % cand == 0:
            best = cand
    return best if best is not None else rows


def _core_index():
    return lax.axis_index("c").astype(jnp.int32).reshape(1)


def _chip_index():
    return (2 * lax.axis_index("x") + lax.axis_index("y")).astype(jnp.int32).reshape(1)


def _add_pair(p4, r3, name):
    s, _, rows, cols = p4.shape
    tr = _row_tile(rows, cols, SUB16)

    def body(c_ref, a_ref, b_ref, o_ref):
        o_ref[...] = (a_ref[...].astype(F32) + b_ref[...].astype(F32)).astype(o_ref.dtype)

    blk = pl.BlockSpec((None, tr, cols), lambda k, i, c_ref: (k, i, 0))
    return pl.pallas_call(
        body, name=name,
        grid_spec=pltpu.PrefetchScalarGridSpec(
            num_scalar_prefetch=1, grid=(s, rows // tr),
            in_specs=[pl.BlockSpec((None, None, tr, cols), lambda k, i, c_ref: (k, c_ref[0], i, 0)), blk],
            out_specs=blk),
        out_shape=jax.ShapeDtypeStruct((s, rows, cols), p4.dtype),
        compiler_params=_cp("parallel", "parallel"))(_core_index(), p4, r3)


def _add_chips(r, name):
    shape = r.shape[1:]
    r3 = r.reshape(N_CHIPS, -1, shape[-1])
    _, rows, cols = r3.shape
    tr = _row_tile(rows, cols, SUB16)

    def body(r_ref, o_ref):
        s = r_ref[0].astype(F32) + r_ref[1].astype(F32)
        s = s + r_ref[2].astype(F32)
        o_ref[...] = s + r_ref[3].astype(F32)

    out = pl.pallas_call(body, name=name, grid=(rows // tr,),
                         in_specs=[pl.BlockSpec((N_CHIPS, tr, cols), lambda i: (0, i, 0))],
                         out_specs=pl.BlockSpec((tr, cols), lambda i: (i, 0)),
                         out_shape=jax.ShapeDtypeStruct((rows, cols), F32),
                         compiler_params=_cp("parallel"))(r3)
    return out.reshape(shape)


def _adamw(w3, g_mine, g_sib, m3, v3, name):
    nl, r, cols = w3.shape
    rows = r // 2
    flat = [arr.reshape(nl, 2, rows, cols) for arr in (w3, m3, v3)]
    tr = _row_tile(rows, cols, SUB)

    def body(c_ref, w_ref, gm_ref, gs_ref, m_ref, v_ref, g_ref, d_ref, mo_ref, vo_ref):
        gg = jnp.where(pl.program_id(1) == c_ref[0], gm_ref[...], gs_ref[...])
        m2 = ADAM_B1 * m_ref[...] + (1.0 - ADAM_B1) * gg
        v2 = ADAM_B2 * v_ref[...] + (1.0 - ADAM_B2) * (gg * gg)
        m_hat = m2 / (1.0 - ADAM_B1 ** ADAM_STEP)
        v_hat = v2 / (1.0 - ADAM_B2 ** ADAM_STEP)
        g_ref[...] = gg
        d_ref[...] = -ADAM_LR * (m_hat / (jnp.sqrt(v_hat) + ADAM_EPS) + ADAM_WD * w_ref[...])
        mo_ref[...] = m2
        vo_ref[...] = v2

    blk = pl.BlockSpec((None, None, tr, cols), lambda l, hh, i, c_ref: (l, hh, i, 0))
    gblk = pl.BlockSpec((None, tr, cols), lambda l, hh, i, c_ref: (l, i, 0))
    outs = pl.pallas_call(
        body, name=name,
        grid_spec=pltpu.PrefetchScalarGridSpec(
            num_scalar_prefetch=1, grid=(nl, 2, rows // tr),
            in_specs=[blk, gblk, gblk, blk, blk], out_specs=[blk] * 4),
        out_shape=[jax.ShapeDtypeStruct((nl, 2, rows, cols), F32)] * 4,
        compiler_params=_cp("parallel", "parallel", "parallel"))(_core_index(), flat[0], g_mine, g_sib, flat[1],
                                                                 flat[2])
    return tuple(o.reshape(nl, r, cols) for o in outs)


def _place(src3, layer, kind, dtype, name):
    _, r, c = src3.shape
    tr = _row_tile(r, c, SUB16)
    in_spec = pl.BlockSpec((None, tr, c), lambda i, my_ref: (layer, i, 0))
    if kind == "col":
        out_spec = pl.BlockSpec((tr, c), lambda i, my_ref: (i, my_ref[0]))
        out_shape = (r, N_CHIPS * c)
    else:
        out_spec = pl.BlockSpec((None, tr, c), lambda i, my_ref: (my_ref[0], i, 0))
        out_shape = (N_CHIPS, r, c)

    def body(my_ref, i_ref, o_ref):
        o_ref[...] = i_ref[...].astype(o_ref.dtype)

    return pl.pallas_call(
        body, name=name,
        grid_spec=pltpu.PrefetchScalarGridSpec(num_scalar_prefetch=1, grid=(r // tr,), in_specs=[in_spec],
                                               out_specs=out_spec),
        out_shape=jax.ShapeDtypeStruct(out_shape, dtype),
        compiler_params=_cp("parallel"))(_chip_index(), src3)


def _half(ref, c, h):
    return ref.at[pl.ds(c * h, h)]


def _position():
    x = lax.axis_index("x")
    y = lax.axis_index("y")
    c = lax.axis_index("c")
    return x, y, c


def _peer_chip(x, y, j):
    tx = 1 - x if j & 2 else x
    ty = 1 - y if j & 1 else y
    return tx, ty


def _remote(src, dst, ssem, rsem, dev):
    return pltpu.make_async_remote_copy(src_ref=src, dst_ref=dst, send_sem=ssem, recv_sem=rsem,
                                        device_id=dev, device_id_type=pl.DeviceIdType.MESH)


_ANY = pl.BlockSpec(memory_space=pl.ANY)


def _unit_view(kind, ref, k):
    if kind == "col":
        n = ref.shape[1] // N_CHIPS
        return ref.at[:, pl.ds(pl.multiple_of(k * n, LANE), n)]
    return ref.at[k]


def _all_gather(placed, kinds):
    nt = len(placed)

    def body(*refs):
        outs = refs[nt:2 * nt]
        ssem, rsem = refs[2 * nt:]
        x, y, c = _position()
        my = 2 * x + y
        sib = (x, y, 1 - c)

        def part(t, k, core):
            view = _unit_view(kinds[t], outs[t], k)
            h = view.shape[0] // 2
            return _half(view, core, h)

        sends, fwds = [], []
        for t in range(nt):
            own = part(t, my, c)
            for j in (1, 2, 3):
                tx, ty = _peer_chip(x, y, j)
                cp = _remote(own, own, ssem.at[6 * t + j - 1], rsem.at[6 * t + j - 1], (tx, ty, c))
                cp.start()
                sends.append(cp)
        for t in range(nt):
            for j in (1, 2, 3):
                tx, ty = _peer_chip(x, y, j)
                got = part(t, 2 * tx + ty, c)
                _remote(got, got, ssem.at[6 * t + j - 1], rsem.at[6 * t + j - 1], sib).wait_recv()
                cp = _remote(got, got, ssem.at[6 * t + 2 + j], rsem.at[6 * t + 2 + j], sib)
                cp.start()
                fwds.append(cp)
        for t in range(nt):
            for j in (1, 2, 3):
                tx, ty = _peer_chip(x, y, j)
                other = part(t, 2 * tx + ty, 1 - c)
                _remote(other, other, ssem.at[6 * t + 2 + j], rsem.at[6 * t + 2 + j], sib).wait_recv()
        for cp in sends + fwds:
            cp.wait_send()

    return pl.pallas_call(
        body, name="all_gather", in_specs=[_ANY] * nt, out_specs=[_ANY] * nt,
        out_shape=[jax.ShapeDtypeStruct(p.shape, p.dtype) for p in placed],
        input_output_aliases={t: t for t in range(nt)},
        scratch_shapes=[pltpu.SemaphoreType.DMA((6 * nt,)), pltpu.SemaphoreType.DMA((6 * nt,))],
    )(*placed)


def _rider_start(refs, kinds, ssem, rsem):
    x, y, c = _position()
    my = 2 * x + y
    for u, (ref, kind) in enumerate(zip(refs, kinds)):
        own = _unit_view(kind, ref, my)
        for j in (1, 2, 3):
            tx, ty = _peer_chip(x, y, j)
            _remote(own, own, ssem.at[3 * u + j - 1], rsem.at[3 * u + j - 1], (tx, ty, c)).start()


def _rider_wait(refs, kinds, ssem, rsem):
    x, y, c = _position()
    for u, (ref, kind) in enumerate(zip(refs, kinds)):
        for j in (1, 2, 3):
            tx, ty = _peer_chip(x, y, j)
            got = _unit_view(kind, ref, 2 * tx + ty)
            _remote(got, got, ssem.at[3 * u + j - 1], rsem.at[3 * u + j - 1], (tx, ty, c)).wait()


def _rider_args(rider):
    bufs = [b for b, _ in rider]
    kinds = [k for _, k in rider]
    n = len(bufs)
    sems = [pltpu.SemaphoreType.DMA((3 * n,)), pltpu.SemaphoreType.DMA((3 * n,))] if n else []
    return bufs, kinds, [_ANY] * n, [jax.ShapeDtypeStruct(b.shape, b.dtype) for b in bufs], sems


def _d2d_stream(src4, other_half, name):
    s, _, rows, cols = src4.shape
    tr = _row_tile(rows, cols, SUB16, STREAM_ELEMS_PER_BLOCK)
    nblk = rows // tr

    nh = src4.shape[1]

    def body(c_ref, src_ref, dst_ref, ssem, rsem):
        k = pl.program_id(0)
        i = pl.program_id(1)
        x, y, c = _position()
        sib = (x, y, 1 - c)
        blk = dst_ref.at[pl.ds(pl.multiple_of((k * nblk + i) * tr, SUB16), tr)]
        cp = _remote(src_ref, blk, ssem, rsem, sib)
        cp.start()
        cp.wait_send()

        @pl.when(jnp.logical_and(k == s - 1, i == nblk - 1))
        def _():
            _remote(dst_ref, dst_ref, ssem, rsem, sib).wait_recv()

    if other_half:
        src_map = lambda k, i, c_ref: ((k * nh + 1 - c_ref[0]) * nblk + i, 0)
    else:
        src_map = lambda k, i, c_ref: (k * nh * nblk + i, 0)
    out = pl.pallas_call(
        body, name=name,
        grid_spec=pltpu.PrefetchScalarGridSpec(
            num_scalar_prefetch=1, grid=(s, nblk),
            in_specs=[pl.BlockSpec((tr, cols), src_map)], out_specs=_ANY,
            scratch_shapes=[pltpu.SemaphoreType.DMA, pltpu.SemaphoreType.DMA]),
        out_shape=jax.ShapeDtypeStruct((s * rows, cols), src4.dtype),
        compiler_params=_cp("arbitrary", "arbitrary"))(_core_index(), src4.reshape(s * nh * rows, cols))
    return out.reshape(s, rows, cols)


def _scatter_copies(srcs, lands, kinds, ssem, rsem, lsem):
    x, y, c = _position()
    my = 2 * x + y
    cps = []
    for u, (src, land, kind) in enumerate(zip(srcs, lands, kinds)):
        cps.append(pltpu.make_async_copy(_unit_view(kind, src, my), land.at[my], lsem.at[u]))
        for j in (1, 2, 3):
            tx, ty = _peer_chip(x, y, j)
            cps.append(_remote(_unit_view(kind, src, 2 * tx + ty), land.at[my], ssem.at[3 * u + j - 1],
                               rsem.at[3 * u + j - 1], (tx, ty, c)))
    return cps


def _scatter_args(scatter):
    srcs = [s for s, _ in scatter]
    kinds = [k for _, k in scatter]
    n = len(srcs)
    shapes = [jax.ShapeDtypeStruct((N_CHIPS, s.shape[0], s.shape[1] // N_CHIPS) if k == "col" else s.shape, s.dtype)
              for s, k in scatter]
    sems = [pltpu.SemaphoreType.DMA((3 * n,)), pltpu.SemaphoreType.DMA((3 * n,)),
            pltpu.SemaphoreType.DMA((n,))] if n else []
    return srcs, kinds, [_ANY] * n, shapes, sems


def _rs_scatter(scatter):
    srcs, kinds, specs, shapes, sems = _scatter_args(scatter)
    n = len(srcs)

    def body(*refs):
        cps = _scatter_copies(refs[:n], refs[n:2 * n], kinds, *refs[2 * n:])
        for cp in cps:
            cp.start()
        for cp in cps:
            cp.wait()

    return pl.pallas_call(body, name="rs_scatter", in_specs=specs, out_specs=specs, out_shape=shapes,
                          scratch_shapes=sems)(*srcs)


SMALL = (("sc_conv_w", True), ("sc_conv_b", False), ("lru_b_in", True), ("lru_conv_w", True),
         ("lru_conv_b", True), ("lru_b_gate", True), ("lru_lambda", True), ("ffn_conv_w", True),
         ("ffn_conv_b", False), ("ln_g", True), ("ln_b", True))
PACK_ROW_MULT = 2 * SUB16


def _pack_rows(shapes):
    n = sum(math.prod(shapes[name]) for name, _ in SMALL)
    rows = -(-n // 128)
    return -(-rows // PACK_ROW_MULT) * PACK_ROW_MULT


def _pack_local(vals, shapes):
    flat = jnp.concatenate([vals[name].reshape(-1) for name, _ in SMALL])
    rows = _pack_rows(shapes)
    return jnp.pad(flat, (0, rows * 128 - flat.shape[0])).reshape(rows, 128)


def _unpack_local(pack, shapes):
    flat = pack.reshape(-1)
    out, off = {}, 0
    for name, _ in SMALL:
        n = math.prod(shapes[name])
        out[name] = flat[off:off + n].reshape(shapes[name])
        off += n
    return out


def _pack_slots(fulls, shapes):
    parts = []
    for name, sharded in SMALL:
        v = fulls[name]
        if sharded:
            ns = shapes[name][-1]
            v = jnp.moveaxis(v.reshape(v.shape[:-1] + (N_CHIPS, ns)), -2, 0).reshape(N_CHIPS, -1)
        else:
            v = jnp.broadcast_to(v.reshape(1, -1), (N_CHIPS, v.size))
        parts.append(v)
    flat = jnp.concatenate(parts, axis=1)
    rows = _pack_rows(shapes)
    return jnp.pad(flat, ((0, 0), (0, rows * 128 - flat.shape[1]))).reshape(N_CHIPS, rows, 128)


def _unpack_slots(packs, shapes):
    flat = packs.reshape(N_CHIPS, -1)
    out, off = {}, 0
    for name, sharded in SMALL:
        n = math.prod(shapes[name])
        if sharded:
            seg = flat[:, off:off + n].reshape((N_CHIPS,) + tuple(shapes[name]))
            seg = jnp.moveaxis(seg, 0, -2)
            out[name] = seg.reshape(seg.shape[:-2] + (N_CHIPS * shapes[name][-1],))
        off += n
    return out


WEIGHTS = ("sc_w_in", "sc_conv_w", "sc_conv_b", "sc_w_out", "lru_w_in", "lru_b_in", "lru_conv_w", "lru_conv_b",
           "lru_w_gate", "lru_b_gate", "lru_lambda", "lru_w_out", "ffn_w_up", "ffn_conv_w", "ffn_conv_b",
           "ffn_w_down", "ln_g", "ln_b")
GATHER_KIND = {"sc_w_in": "col", "sc_w_out": "lead", "lru_w_in": "col", "lru_w_out": "lead", "ffn_w_up": "col",
               "ffn_w_down": "lead"}


def kernel(x, sc_w_in, sc_conv_w, sc_conv_b, sc_w_out, lru_w_in, lru_b_in, lru_conv_w, lru_conv_b, lru_w_gate, lru_b_gate, lru_lambda, lru_w_out, ffn_w_up, ffn_conv_w, ffn_conv_b, ffn_w_down, ln_g, ln_b, loss_target, m_sc_w_in, m_sc_conv_w, m_sc_conv_b, m_sc_w_out, m_lru_w_in, m_lru_b_in, m_lru_conv_w, m_lru_conv_b, m_lru_w_gate, m_lru_b_gate, m_lru_lambda, m_lru_w_out, m_ffn_w_up, m_ffn_conv_w, m_ffn_conv_b, m_ffn_w_down, m_ln_g, m_ln_b, v_sc_w_in, v_sc_conv_w, v_sc_conv_b, v_sc_w_out, v_lru_w_in, v_lru_b_in, v_lru_conv_w, v_lru_conv_b, v_lru_w_gate, v_lru_b_gate, v_lru_lambda, v_lru_w_out, v_ffn_w_up, v_ffn_conv_w, v_ffn_conv_b, v_ffn_w_down, v_ln_g, v_ln_b):
    w = dict(zip(WEIGHTS, (sc_w_in, sc_conv_w, sc_conv_b, sc_w_out, lru_w_in, lru_b_in, lru_conv_w, lru_conv_b,
                           lru_w_gate, lru_b_gate, lru_lambda, lru_w_out, ffn_w_up, ffn_conv_w, ffn_conv_b,
                           ffn_w_down, ln_g, ln_b)))
    mom = dict(zip(WEIGHTS, (m_sc_w_in, m_sc_conv_w, m_sc_conv_b, m_sc_w_out, m_lru_w_in, m_lru_b_in, m_lru_conv_w,
                             m_lru_conv_b, m_lru_w_gate, m_lru_b_gate, m_lru_lambda, m_lru_w_out, m_ffn_w_up,
                             m_ffn_conv_w, m_ffn_conv_b, m_ffn_w_down, m_ln_g, m_ln_b)))
    vel = dict(zip(WEIGHTS, (v_sc_w_in, v_sc_conv_w, v_sc_conv_b, v_sc_w_out, v_lru_w_in, v_lru_b_in, v_lru_conv_w,
                             v_lru_conv_b, v_lru_w_gate, v_lru_b_gate, v_lru_lambda, v_lru_w_out, v_ffn_w_up,
                             v_ffn_conv_w, v_ffn_conv_b, v_ffn_w_down, v_ln_g, v_ln_b)))
    bd, seq, d = x.shape
    t = bd * seq
    small_shapes = {name: w[name].shape for name, _ in SMALL}

    w_pack = _pack_local(w, small_shapes)
    gate_shape = w["lru_w_gate"].shape
    bufs = {(n, l): (_place(w[n], l, k, BF16, "place_w"), k)
            for n, k in GATHER_KIND.items() for l in range(w[n].shape[0])}
    bufs["gate"] = (_place(w["lru_w_gate"].reshape(1, -1, gate_shape[-1]), 0, "lead", BF16, "place_w"), "lead")
    bufs["pack"] = (_place(w_pack[None], 0, "lead", F32, "place_w"), "lead")

    def layer_keys(i):
        mixer = ("sc_w_in", "sc_w_out") if i % 2 == 0 else ("lru_w_in", "lru_w_out")
        return [(mixer[0], i // 2), (mixer[1], i // 2)], [("ffn_w_up", i), ("ffn_w_down", i)]

    def gathered(keys, arrays):
        for key, arr in zip(keys, arrays):
            bufs[key] = (arr, bufs[key][1])

    def wt(name, l):
        arr = bufs[(name, l)][0]
        return arr.reshape(1, -1, arr.shape[-1])

    first = layer_keys(0)[0] + layer_keys(0)[1] + ["gate", "pack"]
    gathered(first, _all_gather([bufs[k][0] for k in first], [bufs[k][1] for k in first]))
    full = _unpack_slots(bufs["pack"][0], small_shapes)
    full["sc_conv_b"] = sc_conv_b
    full["ffn_conv_b"] = ffn_conv_b
    wg_full = jnp.moveaxis(bufs["gate"][0].reshape((N_CHIPS,) + gate_shape), 0, -2)
    wg_full = wg_full.reshape(wg_full.shape[:-2] + (2 * LRU_BLOCK,))
    f = N_CHIPS * w["ffn_w_down"].shape[1]
    rw = N_CHIPS * w["lru_w_out"].shape[1]

    x0 = x.reshape(t, d)
    xb = x0.astype(BF16)
    cur, cur_b = x0, xb
    saved = []

    for i in range(DEPTH):
        j = i // 2
        s = {"xb": cur_b}
        if i % 2 == 0:
            h = _mm_nn(cur_b, wt("sc_w_in", j), 0, None, 3 * d, "sc_in")
            q = _sc_fwd(h, full["sc_conv_w"][j], full["sc_conv_b"][j][None], seq, "sc_fwd")
            z1, x1, x1b = _mm_nn_ln(q, wt("sc_w_out", j), 0, cur, full["ln_g"][i, 0][None], full["ln_b"][i, 0][None],
                                    "sc_out_ln")
        else:
            h = _mm_nn(cur_b, wt("lru_w_in", j), 0, full["lru_b_in"][j][None], 2 * rw, "lru_in")
            hs, q = _lru_fwd(h, full["lru_conv_w"][j], full["lru_conv_b"][j][None], wg_full[j],
                             full["lru_b_gate"][j], full["lru_lambda"][j][None], seq, "lru_fwd")
            s["hs"] = hs
            z1, x1, x1b = _mm_nn_ln(q, wt("lru_w_out", j), 0, cur, full["ln_g"][i, 0][None],
                                    full["ln_b"][i, 0][None], "lru_out_ln")
        s.update(h=h, q=q, z1=z1, x1b=x1b)
        mixer_next, ffn_next = layer_keys(i + 1) if i + 1 < DEPTH else ([], [])
        h3, pre3, act, *arrived = _ffn_up(x1b, wt("ffn_w_up", i), 0, full["ffn_conv_w"][i],
                                          full["ffn_conv_b"][i][None], seq, "ffn_up",
                                          rider=[bufs[k] for k in ffn_next])
        gathered(ffn_next, arrived)
        z2, x2, x2b, *arrived = _mm_nn_ln(act, wt("ffn_w_down", i), 0, x1, full["ln_g"][i, 1][None],
                                          full["ln_b"][i, 1][None], "ffn_down_ln",
                                          rider=[bufs[k] for k in mixer_next])
        gathered(mixer_next, arrived)
        s.update(h3=h3, pre3=pre3, act=act, z2=z2)
        saved.append(s)
        cur, cur_b = x2, x2b

    dcur, loss_parts = _loss_bwd(cur, loss_target.reshape(t, d))
    loss = lax.psum(jnp.sum(loss_parts), MESH_AXES)

    def pair_sum(p, kind):
        lead = kind == "lead"
        p4 = p.reshape(N_CHIPS if lead else 1, 2, -1, p.shape[-1])
        chip_sum = _add_pair(p4, _d2d_stream(p4, True, "rs_swap"), "rs_add_pair")
        return chip_sum if lead else chip_sum[0]

    gp = {n: [None] * w[n].shape[0] for n, _ in SMALL}
    gp["lru_w_gate"] = [None] * gate_shape[0]
    landed, pending = {}, []
    for i in reversed(range(DEPTH)):
        j = i // 2
        s = saved[i]
        mixer_keys, ffn_keys = layer_keys(i)
        dz2, dz2b, dg, db = _ln_bwd(dcur, s["z2"], full["ln_g"][i, 1][None], "ln_bwd")
        gp["ln_g"][i] = [None, dg[0]]
        gp["ln_b"][i] = [None, db[0]]
        p_down = _mm_tn(s["act"], dz2b[None], f // 2, d, "ffn_down_dw").reshape(N_CHIPS, -1, d)
        dh3, dcwg, dcwv, dcbg, dcbv, *lands = _ffn_down_bwd(
            dz2b, wt("ffn_w_down", i), 0, s["h3"], s["pre3"], full["ffn_conv_w"][i], seq, "ffn_down_bwd",
            scatter=[(chip_sum, kind) for _, chip_sum, kind in pending])
        landed.update({key: land for (key, _, _), land in zip(pending, lands)})
        gp["ffn_conv_w"][i] = jnp.concatenate([dcwg, dcwv], axis=1)
        gp["ffn_conv_b"][i] = jnp.concatenate([dcbg[0], dcbv[0]])
        dx1 = _mm_nt_res(dh3, wt("ffn_w_up", i), 0, dz2, f // 2, "ffn_up_dx")
        p_up = _mm_tn(s["x1b"], dh3, d, f // 2, "ffn_up_dw")
        dz1, dz1b, dg, db = _ln_bwd(dx1, s["z1"], full["ln_g"][i, 0][None], "ln_bwd")
        gp["ln_g"][i][0] = dg[0]
        gp["ln_b"][i][0] = db[0]
        gp["ln_g"][i] = jnp.stack(gp["ln_g"][i])
        gp["ln_b"][i] = jnp.stack(gp["ln_b"][i])
        if i % 2 == 0:
            dq = _mm_nt(dz1b, wt("sc_w_out", j), 0, d, "sc_out_dx")
            p_out = _mm_tn(s["q"], dz1b[None], d, d, "sc_out_dw")
            dh3, dcw, dcb = _sc_bwd(s["h"], dq, full["sc_conv_w"][j], full["sc_conv_b"][j][None], seq, "sc_bwd")
            gp["sc_conv_w"][j] = dcw
            gp["sc_conv_b"][j] = dcb[0]
            dcur = _mm_nt_res(dh3, wt("sc_w_in", j), 0, dz1, d, "sc_in_dx")
            p_in = _mm_tn(s["xb"], dh3, d, d, "sc_in_dw")
        else:
            dq = _mm_nt(dz1b, wt("lru_w_out", j), 0, rw, "lru_out_dx")
            p_out = _mm_tn(s["q"], dz1b[None], rw, d, "lru_out_dw")
            dh3, dbin, dcw, dcb, dwg, dbg, dlam = _lru_bwd(
                s["h"], s["hs"], dq, full["lru_conv_w"][j], full["lru_conv_b"][j][None], wg_full[j],
                full["lru_b_gate"][j], full["lru_lambda"][j][None], seq, "lru_bwd")
            gp["lru_b_in"][j] = dbin[0]
            gp["lru_conv_w"][j] = dcw
            gp["lru_conv_b"][j] = dcb[0]
            gp["lru_w_gate"][j] = dwg
            gp["lru_b_gate"][j] = dbg
            gp["lru_lambda"][j] = dlam[0]
            dcur = _mm_nt_res(dh3, wt("lru_w_in", j), 0, dz1, rw, "lru_in_dx")
            p_in = _mm_tn(s["xb"], dh3, d, rw, "lru_in_dw")
        layer_partials = ((mixer_keys[0], p_in, "col"), (mixer_keys[1], p_out.reshape(N_CHIPS, -1, d), "lead"),
                          (ffn_keys[0], p_up, "col"), (ffn_keys[1], p_down, "lead"))
        pending = [(key, pair_sum(p, kind), kind) for key, p, kind in layer_partials]
    grad_x = dcur.reshape(bd, seq, d)
    gp = {n: jnp.stack(v) for n, v in gp.items()}

    gate = gp["lru_w_gate"]
    gate = jnp.moveaxis(gate.reshape(gate.shape[:-1] + (N_CHIPS, gate_shape[-1])), -2, 0)
    gate = gate.astype(BF16).reshape(N_CHIPS, -1, gate_shape[-1])
    pending += [("gate", pair_sum(gate, "lead"), "lead"),
                ("pack", pair_sum(_pack_slots(gp, small_shapes), "lead"), "lead")]
    lands = _rs_scatter([(chip_sum, kind) for _, chip_sum, kind in pending])
    landed.update({key: land for (key, _, _), land in zip(pending, lands)})

    def update(keys, w3, m3, v3):
        g_mine = jnp.stack([_add_chips(landed[k], "rs_add_chips") for k in keys])
        g_sib = _d2d_stream(g_mine.reshape(1, 1, -1, g_mine.shape[-1]), False, "rs_share").reshape(g_mine.shape)
        return _adamw(w3, g_mine, g_sib, m3, v3, "adamw")

    g_out, d_out, m_out, v_out = {}, {}, {}, {}
    for n in GATHER_KIND:
        outs = update([(n, l) for l in range(w[n].shape[0])], w[n], mom[n], vel[n])
        g_out[n], d_out[n], m_out[n], v_out[n] = outs
    as_rows = lambda a: a.reshape(1, -1, a.shape[-1])
    outs = update(["gate"], as_rows(w["lru_w_gate"]), as_rows(mom["lru_w_gate"]), as_rows(vel["lru_w_gate"]))
    g_out["lru_w_gate"], d_out["lru_w_gate"], m_out["lru_w_gate"], v_out["lru_w_gate"] = (
        o.reshape(gate_shape) for o in outs)
    packs = update(["pack"], w_pack[None], _pack_local(mom, small_shapes)[None], _pack_local(vel, small_shapes)[None])
    for dst, pack in zip((g_out, d_out, m_out, v_out), packs):
        dst.update(_unpack_local(pack[0], small_shapes))

    return (loss, grad_x, *[g_out[n] for n in WEIGHTS], *[d_out[n] for n in WEIGHTS],
            *[m_out[n] for n in WEIGHTS], *[v_out[n] for n in WEIGHTS])
```

```python
import math

import jax
import jax.numpy as jnp
from jax import lax
from jax.experimental import pallas as pl
from jax.experimental.pallas import tpu as pltpu

F32 = jnp.float32
BF16 = jnp.bfloat16

DEPTH = 4
LRU_HEADS = 10
LRU_BLOCK = 128
LRU_C = 8.0
LN_EPS = 1e-5
ALPHA = (2.0 * DEPTH) ** 0.25
ADAM_LR, ADAM_B1, ADAM_B2, ADAM_EPS, ADAM_WD, ADAM_STEP = 0.001, 0.9, 0.999, 1e-08, 0.01, 10
N_CHIPS = 4
MESH_AXES = ("x", "y", "c")

VMEM_LIMIT_BYTES = 48 * 1024 * 1024
TM_MM = 512
TM_RES = 1024
TT_MM = 1024
TM_SC = 256
FFN_COL_BLOCKS = 2
RC = 128
LANE = 128
MXU_COLS = 256
TS_LRU = 128
TM_LN = 512
ELEMS_PER_BLOCK = 256 * 1024
STREAM_ELEMS_PER_BLOCK = 1024 * 1024
SUB = 8
SUB16 = 16


def _cp(*sem):
    return pltpu.CompilerParams(dimension_semantics=sem, vmem_limit_bytes=VMEM_LIMIT_BYTES)


def _sigmoid(v):
    return 1.0 / (1.0 + jnp.exp(-v))


def _softplus(v):
    e = jnp.exp(-jnp.abs(v))
    log1p = jnp.where(e < 1e-3, e * (1.0 - e * (0.5 - e * (1.0 / 3.0))), jnp.log(1.0 + e))
    return jnp.maximum(v, 0.0) + log1p


def _one_minus_exp(v):
    series = -v * (1.0 + v * (0.5 + v * (1.0 / 6.0 + v * (1.0 / 24.0))))
    return jnp.where(v > -0.02, series, 1.0 - jnp.exp(v))


def _gelu_and_grad(v):
    k = math.sqrt(2.0 / math.pi)
    t = jnp.tanh(k * (v + 0.044715 * v * v * v))
    val = 0.5 * v * (1.0 + t)
    grad = 0.5 * (1.0 + t) + 0.5 * v * (1.0 - t * t) * k * (1.0 + 3.0 * 0.044715 * v * v)
    return val, grad


def _down(ext, k, n_head):
    if k:
        ext = pltpu.roll(ext, k, 0)
    return ext[n_head:]


def _up(ext, k, n):
    if k:
        ext = pltpu.roll(ext, ext.shape[0] - k, 0)
    return ext[:n]


def _row(ref, k):
    return ref[k:k + 1, :]


def _colsum(v):
    return jnp.sum(v, axis=0, keepdims=True)


def _mm_nn(a, w3, l, bias, tn, name):
    m, k = a.shape
    n = w3.shape[2]
    tm = min(TM_MM, m)
    has_bias = bias is not None

    def body(*refs):
        if has_bias:
            a_ref, w_ref, b_ref, o_ref = refs
        else:
            a_ref, w_ref, o_ref = refs
        acc = jnp.dot(a_ref[...], w_ref[...], preferred_element_type=F32)
        if has_bias:
            acc = acc + b_ref[...]
        o_ref[...] = acc.astype(o_ref.dtype)

    in_specs = [pl.BlockSpec((tm, k), lambda i, j: (i, 0)),
                pl.BlockSpec((None, k, tn), lambda i, j: (l, 0, j))]
    args = [a, w3]
    if has_bias:
        in_specs.append(pl.BlockSpec((1, tn), lambda i, j: (0, j)))
        args.append(bias)
    return pl.pallas_call(
        body, name=name, grid=(m // tm, n // tn), in_specs=in_specs,
        out_specs=pl.BlockSpec((tm, tn), lambda i, j: (i, j)),
        out_shape=jax.ShapeDtypeStruct((m, n), BF16),
        compiler_params=_cp("parallel", "arbitrary"))(*args)


def _mm_nn_ln(a, w3, l, xres, g, b, name, rider=()):
    m, k = a.shape
    n = w3.shape[2]
    tm = min(TM_MM, m)
    bufs, rkinds, rspecs, rshapes, rsems = _rider_args(rider)
    nr = len(bufs)

    def body(*refs):
        a_ref, w_ref, x_ref, g_ref, b_ref = refs[:5]
        z_ref, xn_ref, xb_ref = refs[5 + nr:8 + nr]
        rout, sems = refs[8 + nr:8 + 2 * nr], refs[8 + 2 * nr:]
        if nr:
            @pl.when(pl.program_id(0) == 0)
            def _():
                _rider_start(rout, rkinds, *sems)

        y = jnp.dot(a_ref[...], w_ref[...], preferred_element_type=F32)
        z = ALPHA * x_ref[...] + y
        mu = jnp.mean(z, axis=-1, keepdims=True)
        zc = z - mu
        var = jnp.mean(zc * zc, axis=-1, keepdims=True)
        xn = zc * lax.rsqrt(var + LN_EPS) * g_ref[...] + b_ref[...]
        z_ref[...] = z
        xn_ref[...] = xn
        xb_ref[...] = xn.astype(BF16)
        if nr:
            @pl.when(pl.program_id(0) == m // tm - 1)
            def _():
                _rider_wait(rout, rkinds, *sems)

    row = pl.BlockSpec((tm, n), lambda i: (i, 0))
    vec = pl.BlockSpec((1, n), lambda i: (0, 0))
    return pl.pallas_call(
        body, name=name, grid=(m // tm,),
        in_specs=[pl.BlockSpec((tm, k), lambda i: (i, 0)),
                  pl.BlockSpec((None, k, n), lambda i: (l, 0, 0)), row, vec, vec] + rspecs,
        out_specs=[row, row, row] + rspecs,
        out_shape=[jax.ShapeDtypeStruct((m, n), F32), jax.ShapeDtypeStruct((m, n), F32),
                   jax.ShapeDtypeStruct((m, n), BF16)] + rshapes,
        input_output_aliases={5 + u: 3 + u for u in range(nr)},
        scratch_shapes=rsems,
        compiler_params=_cp("arbitrary"))(a, w3, xres, g, b, *bufs)


def _mm_nt(a, w3, l, tk, name):
    m, n = a.shape
    kd = w3.shape[1]
    tm = min(TM_MM, m)

    def body(a_ref, w_ref, o_ref):
        o_ref[...] = lax.dot_general(a_ref[...], w_ref[...], (((1,), (1,)), ((), ())),
                                     preferred_element_type=F32).astype(o_ref.dtype)

    return pl.pallas_call(
        body, name=name, grid=(m // tm, kd // tk),
        in_specs=[pl.BlockSpec((tm, n), lambda i, j: (i, 0)),
                  pl.BlockSpec((None, tk, n), lambda i, j: (l, j, 0))],
        out_specs=pl.BlockSpec((tm, tk), lambda i, j: (i, j)),
        out_shape=jax.ShapeDtypeStruct((m, kd), BF16),
        compiler_params=_cp("parallel", "arbitrary"))(a, w3)


def _mm_nt_res(dh3, w3, l, dz, tc, name, scatter=()):
    g, m, cg = dh3.shape
    kd = w3.shape[1]
    ncg = cg // tc
    nk = g * ncg
    tm = min(TM_RES, m)
    srcs, skinds, sspecs, sshapes, ssems = _scatter_args(scatter)
    ns = len(srcs)

    def body(*refs):
        a_ref, w_ref, dz_ref = refs[:3]
        sin, o_ref, lands = refs[3:3 + ns], refs[3 + ns], refs[4 + ns:4 + 2 * ns]
        acc, sems = refs[4 + 2 * ns], refs[5 + 2 * ns:]
        i = pl.program_id(0)
        k = pl.program_id(1)
        if ns:
            @pl.when(jnp.logical_and(i == 0, k == 0))
            def _():
                for cp in _scatter_copies(sin, lands, skinds, *sems):
                    cp.start()

        @pl.when(k == 0)
        def _():
            acc[...] = ALPHA * dz_ref[...]

        acc[...] += lax.dot_general(a_ref[...], w_ref[...], (((1,), (1,)), ((), ())),
                                    preferred_element_type=F32)

        @pl.when(k == nk - 1)
        def _():
            o_ref[...] = acc[...]

        if ns:
            @pl.when(jnp.logical_and(i == m // tm - 1, k == nk - 1))
            def _():
                for cp in _scatter_copies(sin, lands, skinds, *sems):
                    cp.wait()

    outs = pl.pallas_call(
        body, name=name, grid=(m // tm, nk),
        in_specs=[pl.BlockSpec((None, tm, tc), lambda i, k: (k // ncg, i, k % ncg)),
                  pl.BlockSpec((None, kd, tc), lambda i, k: (l, 0, k)),
                  pl.BlockSpec((tm, kd), lambda i, k: (i, 0))] + sspecs,
        out_specs=[pl.BlockSpec((tm, kd), lambda i, k: (i, 0))] + sspecs,
        out_shape=[jax.ShapeDtypeStruct((m, kd), F32)] + sshapes,
        scratch_shapes=[pltpu.VMEM((tm, kd), F32)] + ssems,
        compiler_params=_cp("arbitrary", "arbitrary"))(dh3, w3, dz, *srcs)
    return outs if ns else outs[0]


def _mm_tn(a, b3, tka, tnb, name):
    t, ka = a.shape
    g, _, cg = b3.shape
    ncg = cg // tnb
    tt = min(TT_MM, t)
    nt = t // tt

    def body(a_ref, b_ref, o_ref, acc):
        s = pl.program_id(2)

        @pl.when(s == 0)
        def _():
            acc[...] = jnp.zeros_like(acc)

        acc[...] += lax.dot_general(a_ref[...], b_ref[...], (((0,), (0,)), ((), ())),
                                    preferred_element_type=F32)

        @pl.when(s == nt - 1)
        def _():
            o_ref[...] = acc[...].astype(o_ref.dtype)

    return pl.pallas_call(
        body, name=name, grid=(ka // tka, g * ncg, nt),
        in_specs=[pl.BlockSpec((tt, tka), lambda i, j, s: (s, i)),
                  pl.BlockSpec((None, tt, tnb), lambda i, j, s: (j // ncg, s, j % ncg))],
        out_specs=pl.BlockSpec((tka, tnb), lambda i, j, s: (i, j)),
        out_shape=jax.ShapeDtypeStruct((ka, g * cg), BF16),
        scratch_shapes=[pltpu.VMEM((tka, tnb), F32)],
        compiler_params=_cp("parallel", "parallel", "arbitrary"))(a, b3)


def _ln_bwd(dxn, z, g, name):
    m, d = z.shape
    tm = min(TM_LN, m)

    def body(dx_ref, z_ref, g_ref, dz_ref, dzb_ref, dg_ref, db_ref):
        @pl.when(pl.program_id(0) == 0)
        def _():
            dg_ref[...] = jnp.zeros_like(dg_ref)
            db_ref[...] = jnp.zeros_like(db_ref)

        zz = z_ref[...]
        dx = dx_ref[...]
        mu = jnp.mean(zz, axis=-1, keepdims=True)
        zc = zz - mu
        var = jnp.mean(zc * zc, axis=-1, keepdims=True)
        rstd = lax.rsqrt(var + LN_EPS)
        xh = zc * rstd
        dg_ref[...] += _colsum(dx * xh)
        db_ref[...] += _colsum(dx)
        dxh = dx * g_ref[...]
        m1 = jnp.mean(dxh, axis=-1, keepdims=True)
        m2 = jnp.mean(dxh * xh, axis=-1, keepdims=True)
        dz = rstd * (dxh - m1 - xh * m2)
        dz_ref[...] = dz
        dzb_ref[...] = dz.astype(BF16)

    row = pl.BlockSpec((tm, d), lambda i: (i, 0))
    vec = pl.BlockSpec((1, d), lambda i: (0, 0))
    return pl.pallas_call(
        body, name=name, grid=(m // tm,), in_specs=[row, row, vec],
        out_specs=[row, row, vec, vec],
        out_shape=[jax.ShapeDtypeStruct((m, d), F32), jax.ShapeDtypeStruct((m, d), BF16),
                   jax.ShapeDtypeStruct((1, d), F32), jax.ShapeDtypeStruct((1, d), F32)],
        compiler_params=_cp("arbitrary"))(dxn, z, g)


def _loss_bwd(y, target):
    m, d = y.shape
    tm = min(TM_LN, m)

    def body(y_ref, t_ref, dy_ref, ls_ref):
        @pl.when(pl.program_id(0) == 0)
        def _():
            ls_ref[...] = jnp.zeros_like(ls_ref)

        e = y_ref[...] - t_ref[...]
        dy_ref[...] = e * (1.0 / d)
        ls_ref[...] += _colsum(e * e) * (0.5 / d)

    row = pl.BlockSpec((tm, d), lambda i: (i, 0))
    return pl.pallas_call(
        body, name="loss_bwd", grid=(m // tm,), in_specs=[row, row],
        out_specs=[row, pl.BlockSpec((1, d), lambda i: (0, 0))],
        out_shape=[jax.ShapeDtypeStruct((m, d), F32), jax.ShapeDtypeStruct((1, d), F32)],
        compiler_params=_cp("arbitrary"))(y, target)


def _sc_fwd(h, cw, cb, seq, name):
    t, d3 = h.shape
    d = d3 // 3
    tm = min(TM_SC, seq)

    def body(hb_ref, hc_ref, hv_ref, cw_ref, cb_ref, q_ref, carry):
        i = pl.program_id(0)

        @pl.when(lax.rem(i * tm, seq) == 0)
        def _():
            carry[...] = jnp.zeros_like(carry)

        p = hc_ref[...].astype(F32) * hv_ref[...].astype(F32)
        ext = jnp.concatenate([carry[...], p], axis=0)
        u = cb_ref[...] + _row(cw_ref, 0) * _down(ext, 2, SUB) + _row(cw_ref, 1) * _down(ext, 1, SUB) \
            + _row(cw_ref, 2) * p
        q_ref[...] = (hb_ref[...].astype(F32) * u).astype(BF16)
        carry[...] = p[tm - SUB:, :]

    blk = lambda c: pl.BlockSpec((tm, d), lambda i: (i, c))
    return pl.pallas_call(
        body, name=name, grid=(t // tm,),
        in_specs=[blk(0), blk(1), blk(2), pl.BlockSpec((3, d), lambda i: (0, 0)),
                  pl.BlockSpec((1, d), lambda i: (0, 0))],
        out_specs=pl.BlockSpec((tm, d), lambda i: (i, 0)),
        out_shape=jax.ShapeDtypeStruct((t, d), BF16),
        scratch_shapes=[pltpu.VMEM((SUB, d), F32)],
        compiler_params=_cp("arbitrary"))(h, h, h, cw, cb)


def _sc_bwd(h, dq, cw, cb, seq, name):
    t, d3 = h.shape
    d = d3 // 3
    tm = min(TM_SC, seq)
    nt = t // tm
    hpt = tm // SUB16

    def body(hb_ref, hc_ref, hv_ref, hch_ref, hvh_ref, dq_ref, cw_ref, cb_ref,
             dh_ref, dcw_ref, dcb_ref, carry):
        i = pl.program_id(0)
        ri = nt - 1 - i

        @pl.when(i == 0)
        def _():
            dcw_ref[...] = jnp.zeros_like(dcw_ref)
            dcb_ref[...] = jnp.zeros_like(dcb_ref)

        @pl.when(lax.rem((ri + 1) * tm, seq) == 0)
        def _():
            carry[...] = jnp.zeros_like(carry)

        keep = jnp.where(lax.rem(ri * tm, seq) == 0, 0.0, 1.0)
        gb = hb_ref[...].astype(F32)
        gc = hc_ref[...].astype(F32)
        v = hv_ref[...].astype(F32)
        p = gc * v
        p_head = hch_ref[...].astype(F32) * hvh_ref[...].astype(F32) * keep
        ext = jnp.concatenate([p_head, p], axis=0)
        pm2 = _down(ext, 2, SUB16)
        pm1 = _down(ext, 1, SUB16)
        u = cb_ref[...] + _row(cw_ref, 0) * pm2 + _row(cw_ref, 1) * pm1 + _row(cw_ref, 2) * p
        dqf = dq_ref[...].astype(F32)
        du = dqf * gb
        dcb_ref[...] += _colsum(du)
        dcw_ref[0:1, :] += _colsum(du * pm2)
        dcw_ref[1:2, :] += _colsum(du * pm1)
        dcw_ref[2:3, :] += _colsum(du * p)
        ext2 = jnp.concatenate([du, carry[...]], axis=0)
        dp = _row(cw_ref, 2) * du + _row(cw_ref, 1) * _up(ext2, 1, tm) + _row(cw_ref, 0) * _up(ext2, 2, tm)
        carry[...] = du[0:SUB, :]
        dh_ref[0] = (dqf * u).astype(BF16)
        dh_ref[1] = (dp * v).astype(BF16)
        dh_ref[2] = (dp * gc).astype(BF16)

    blk = lambda c: pl.BlockSpec((tm, d), lambda i: (nt - 1 - i, c))
    head = lambda c: pl.BlockSpec((SUB16, d), lambda i: (jnp.maximum((nt - 1 - i) * hpt - 1, 0), c))
    vec = lambda r: pl.BlockSpec((r, d), lambda i: (0, 0))
    return pl.pallas_call(
        body, name=name, grid=(nt,),
        in_specs=[blk(0), blk(1), blk(2), head(1), head(2),
                  pl.BlockSpec((tm, d), lambda i: (nt - 1 - i, 0)), vec(3), vec(1)],
        out_specs=[pl.BlockSpec((3, tm, d), lambda i: (0, nt - 1 - i, 0)), vec(3), vec(1)],
        out_shape=[jax.ShapeDtypeStruct((3, t, d), BF16), jax.ShapeDtypeStruct((3, d), F32),
                   jax.ShapeDtypeStruct((1, d), F32)],
        scratch_shapes=[pltpu.VMEM((SUB, d), F32)],
        compiler_params=_cp("arbitrary"))(h, h, h, h, h, dq, cw, cb)


def _fold(v):
    return jnp.sum(v.reshape(v.shape[0] // SUB, SUB, v.shape[1]), axis=0)


def _ffn_up(xb, w3, l, cw, cb, seq, name, rider=()):
    t, d = xb.shape
    f = w3.shape[2] // 2
    tc = f // FFN_COL_BLOCKS
    tm = min(TM_MM, seq)
    nc = FFN_COL_BLOCKS
    bufs, rkinds, rspecs, rshapes, rsems = _rider_args(rider)
    nr = len(bufs)

    def body(*refs):
        x_ref, wg_ref, wv_ref, cwg_ref, cwv_ref, cbg_ref, cbv_ref = refs[:7]
        h_ref, pre_ref, act_ref = refs[7 + nr:10 + nr]
        rout = refs[10 + nr:10 + 2 * nr]
        eg, ev = refs[10 + 2 * nr:12 + 2 * nr]
        sems = refs[12 + 2 * nr:]
        i = pl.program_id(1)
        if nr:
            @pl.when(jnp.logical_and(pl.program_id(0) == 0, i == 0))
            def _():
                _rider_start(rout, rkinds, *sems)

        @pl.when(lax.rem(i * tm, seq) == 0)
        def _():
            eg[0:SUB, :] = jnp.zeros((SUB, tc), F32)
            ev[0:SUB, :] = jnp.zeros((SUB, tc), F32)

        xx = x_ref[...]

        def matmul(lo, hi):
            eg[SUB:, lo:hi] = jnp.dot(xx, wg_ref[:, lo:hi], preferred_element_type=F32)
            ev[SUB:, lo:hi] = jnp.dot(xx, wv_ref[:, lo:hi], preferred_element_type=F32)

        def epilogue(lo, hi):
            for c0 in range(lo, hi, LANE):
                cols = slice(c0, c0 + LANE)
                taps = [[ref[k:k + 1, cols] for k in range(3)] + [bref[:, cols]]
                        for ref, bref in ((cwg_ref, cbg_ref), (cwv_ref, cbv_ref))]
                for r0 in range(0, tm, RC):
                    rows = slice(r0, r0 + RC)
                    pres = []
                    for half, e_ref in enumerate((eg, ev)):
                        w0, w1, w2, bias = taps[half]
                        e = e_ref[r0:r0 + RC + SUB, cols]
                        cur = e[SUB:]
                        pre = bias + w0 * _down(e, 2, SUB) + w1 * _down(e, 1, SUB) + w2 * cur
                        h_ref[half, rows, cols] = cur.astype(BF16)
                        pre_ref[half, rows, cols] = pre.astype(BF16)
                        pres.append(pre)
                    act_ref[rows, cols] = (pres[0] * _sigmoid(pres[0]) * pres[1]).astype(BF16)

        blocks = [(lo, min(lo + MXU_COLS, tc)) for lo in range(0, tc, MXU_COLS)]
        matmul(*blocks[0])
        for b, blk in enumerate(blocks):
            if b + 1 < len(blocks):
                matmul(*blocks[b + 1])
            epilogue(*blk)
        eg[0:SUB, :] = eg[tm:tm + SUB, :]
        ev[0:SUB, :] = ev[tm:tm + SUB, :]
        if nr:
            @pl.when(jnp.logical_and(pl.program_id(0) == nc - 1, i == t // tm - 1))
            def _():
                _rider_wait(rout, rkinds, *sems)

    wspec = lambda off: pl.BlockSpec((None, d, tc), lambda j, i: (l, 0, j + off))
    vec = lambda r, off: pl.BlockSpec((r, tc), lambda j, i: (0, j + off))
    pair = pl.BlockSpec((2, tm, tc), lambda j, i: (0, i, j))
    return pl.pallas_call(
        body, name=name, grid=(nc, t // tm),
        in_specs=[pl.BlockSpec((tm, d), lambda j, i: (i, 0)), wspec(0), wspec(nc),
                  vec(3, 0), vec(3, nc), vec(1, 0), vec(1, nc)] + rspecs,
        out_specs=[pair, pair, pl.BlockSpec((tm, tc), lambda j, i: (i, j))] + rspecs,
        out_shape=[jax.ShapeDtypeStruct((2, t, f), BF16), jax.ShapeDtypeStruct((2, t, f), BF16),
                   jax.ShapeDtypeStruct((t, f), BF16)] + rshapes,
        input_output_aliases={7 + u: 3 + u for u in range(nr)},
        scratch_shapes=[pltpu.VMEM((tm + SUB, tc), F32), pltpu.VMEM((tm + SUB, tc), F32)] + rsems,
        compiler_params=_cp("arbitrary", "arbitrary"))(xb, w3, w3, cw, cw, cb, cb, *bufs)


def _ffn_down_bwd(dzb, wd3, l, h3, pre3, cw, seq, name, scatter=()):
    t, d = dzb.shape
    f = wd3.shape[1]
    tc = f // FFN_COL_BLOCKS
    tm = min(TM_MM, seq)
    nt = t // tm
    nc = FFN_COL_BLOCKS
    srcs, skinds, sspecs, sshapes, ssems = _scatter_args(scatter)
    ns = len(srcs)

    def body(*refs):
        dz_ref, wd_ref, h_ref, pre_ref, cwg_ref, cwv_ref = refs[:6]
        sin = refs[6:6 + ns]
        dh_ref, dcwg_ref, dcwv_ref, dcbg_ref, dcbv_ref = refs[6 + ns:11 + ns]
        lands = refs[11 + ns:11 + 2 * ns]
        da_s, carry = refs[11 + 2 * ns:13 + 2 * ns]
        sems = refs[13 + 2 * ns:]
        i = pl.program_id(1)
        ri = nt - 1 - i
        if ns:
            @pl.when(jnp.logical_and(pl.program_id(0) == 0, i == 0))
            def _():
                for cp in _scatter_copies(sin, lands, skinds, *sems):
                    cp.start()

        @pl.when(i == 0)
        def _():
            for r in (dcwg_ref, dcwv_ref, dcbg_ref, dcbv_ref):
                r[...] = jnp.zeros_like(r)

        @pl.when(lax.rem((ri + 1) * tm, seq) == 0)
        def _():
            carry[...] = jnp.zeros_like(carry)

        dz = dz_ref[...]

        def matmul(lo, hi):
            da_s[:, lo:hi] = lax.dot_general(dz, wd_ref[lo:hi, :], (((1,), (1,)), ((), ())),
                                             preferred_element_type=F32)

        def epilogue(lo, hi):
            for c0 in range(lo, hi, LANE):
                cols = slice(c0, c0 + LANE)
                taps = [[ref[k:k + 1, cols] for k in range(3)] for ref in (cwg_ref, cwv_ref)]
                acc = [jnp.zeros((SUB, LANE), F32)] * 8
                for r0 in range(tm - RC, -1, -RC):
                    rows = slice(r0, r0 + RC)
                    da = da_s[rows, cols]
                    gp = pre_ref[0, rows, cols].astype(F32)
                    vp = pre_ref[1, rows, cols].astype(F32)
                    sg = _sigmoid(gp)
                    dpres = (da * vp * (sg * (1.0 + gp * (1.0 - sg))), da * (gp * sg))
                    for half in range(2):
                        w0, w1, w2 = taps[half]
                        dpre = dpres[half]
                        ext = jnp.concatenate([dpre, carry[half, :, cols]], axis=0)
                        u1 = _up(ext, 1, RC)
                        u2 = _up(ext, 2, RC)
                        carry[half, :, cols] = dpre[0:SUB]
                        dh_ref[half, rows, cols] = (w2 * dpre + w1 * u1 + w0 * u2).astype(BF16)
                        hh = h_ref[half, rows, cols].astype(F32)
                        for k, term in enumerate((hh * u2, hh * u1, hh * dpre, dpre)):
                            acc[4 * half + k] = acc[4 * half + k] + _fold(term)
                for half, (dcw_ref, dcb_ref) in enumerate(((dcwg_ref, dcbg_ref), (dcwv_ref, dcbv_ref))):
                    for k in range(3):
                        dcw_ref[k:k + 1, cols] += _colsum(acc[4 * half + k])
                    dcb_ref[:, cols] += _colsum(acc[4 * half + 3])

        blocks = [(lo, min(lo + MXU_COLS, tc)) for lo in range(0, tc, MXU_COLS)]
        matmul(*blocks[0])
        for b, blk in enumerate(blocks):
            if b + 1 < len(blocks):
                matmul(*blocks[b + 1])
            epilogue(*blk)
        if ns:
            @pl.when(jnp.logical_and(pl.program_id(0) == nc - 1, i == nt - 1))
            def _():
                for cp in _scatter_copies(sin, lands, skinds, *sems):
                    cp.wait()

    pair = pl.BlockSpec((2, tm, tc), lambda j, i: (0, nt - 1 - i, j))
    vec = lambda off: pl.BlockSpec((3, tc), lambda j, i: (0, j + off))
    acc_spec = lambda r: pl.BlockSpec((r, tc), lambda j, i: (0, j))
    return pl.pallas_call(
        body, name=name, grid=(nc, nt),
        in_specs=[pl.BlockSpec((tm, d), lambda j, i: (nt - 1 - i, 0)),
                  pl.BlockSpec((None, tc, d), lambda j, i: (l, j, 0)), pair, pair, vec(0), vec(nc)] + sspecs,
        out_specs=[pair, acc_spec(3), acc_spec(3), acc_spec(1), acc_spec(1)] + sspecs,
        out_shape=[jax.ShapeDtypeStruct((2, t, f), BF16), jax.ShapeDtypeStruct((3, f), F32),
                   jax.ShapeDtypeStruct((3, f), F32), jax.ShapeDtypeStruct((1, f), F32),
                   jax.ShapeDtypeStruct((1, f), F32)] + sshapes,
        scratch_shapes=[pltpu.VMEM((tm, tc), F32), pltpu.VMEM((2, SUB, tc), F32)] + ssems,
        compiler_params=_cp("arbitrary", "arbitrary"))(dzb, wd3, h3, pre3, cw, cw, *srcs)


def _lru_gates(xr, wg_ref, bg_ref):
    rs, gs = [], []
    for hd in range(LRU_HEADS):
        xh = xr[:, hd * LRU_BLOCK:(hd + 1) * LRU_BLOCK].astype(BF16)
        gt = jnp.dot(xh, wg_ref[hd], preferred_element_type=F32) + _row(bg_ref, hd)
        rs.append(gt[:, :LRU_BLOCK])
        gs.append(gt[:, LRU_BLOCK:])
    return jnp.concatenate(rs, axis=1), jnp.concatenate(gs, axis=1)


def _lru_coeffs(xr, wg_ref, bg_ref, lam_ref):
    gr, gi = _lru_gates(xr, wg_ref, bg_ref)
    r = _sigmoid(gr)
    ig = _sigmoid(gi)
    sp = _softplus(-lam_ref[...])
    log_a = -LRU_C * r * sp
    a = jnp.exp(log_a)
    mult = jnp.sqrt(_one_minus_exp(2.0 * log_a))
    return r, ig, sp, a, mult


def _lru_fwd(h, cw, cb, wg, bg, lam, seq, name):
    t, r2 = h.shape
    rw = r2 // 2
    ts = min(TS_LRU, seq)
    n8 = ts // SUB

    def body(hg_ref, hr_ref, cw_ref, cb_ref, wg_ref, bg_ref, lam_ref, hs_ref, y_ref,
             a_s, b_s, cconv, cstate):
        i = pl.program_id(0)

        @pl.when(lax.rem(i * ts, seq) == 0)
        def _():
            cconv[...] = jnp.zeros_like(cconv)
            cstate[...] = jnp.zeros_like(cstate)

        rin = hr_ref[...].astype(F32)
        ext = jnp.concatenate([cconv[...], rin], axis=0)
        xr = cb_ref[...]
        for k in range(4):
            xr = xr + _row(cw_ref, k) * _down(ext, 3 - k, SUB)
        cconv[...] = rin[ts - SUB:, :]
        _, ig, _, a, mult = _lru_coeffs(xr, wg_ref, bg_ref, lam_ref)
        a_s[...] = a
        b_s[...] = mult * (ig * xr)
        row = lax.broadcasted_iota(jnp.int32, (SUB, rw), 0)

        def step(j, carry):
            off = pl.multiple_of(j * SUB, SUB)
            a8 = a_s[pl.ds(off, SUB), :]
            b8 = b_s[pl.ds(off, SUB), :]
            for d in (1, 2, 4):
                m = row >= d
                b8 = jnp.where(m, a8 * pltpu.roll(b8, d, 0) + b8, b8)
                a8 = jnp.where(m, a8 * pltpu.roll(a8, d, 0), a8)
            h8 = a8 * carry + b8
            hs_ref[pl.ds(off, SUB), :] = h8
            return _colsum(jnp.where(row == SUB - 1, h8, 0.0))

        cstate[...] = lax.fori_loop(0, n8, step, cstate[...])
        gel, _ = _gelu_and_grad(hg_ref[...].astype(F32))
        y_ref[...] = (hs_ref[...] * gel).astype(BF16)

    full = lambda shp: pl.BlockSpec(shp, lambda i: (0,) * len(shp))
    return pl.pallas_call(
        body, name=name, grid=(t // ts,),
        in_specs=[pl.BlockSpec((ts, rw), lambda i: (i, 0)), pl.BlockSpec((ts, rw), lambda i: (i, 1)),
                  full((4, rw)), full((1, rw)), full(wg.shape), full(bg.shape), full((1, rw))],
        out_specs=[pl.BlockSpec((ts, rw), lambda i: (i, 0)), pl.BlockSpec((ts, rw), lambda i: (i, 0))],
        out_shape=[jax.ShapeDtypeStruct((t, rw), F32), jax.ShapeDtypeStruct((t, rw), BF16)],
        scratch_shapes=[pltpu.VMEM((ts, rw), F32), pltpu.VMEM((ts, rw), F32),
                        pltpu.VMEM((SUB, rw), F32), pltpu.VMEM((1, rw), F32)],
        compiler_params=_cp("arbitrary"))(h, h, cw, cb, wg, bg, lam)


def _lru_bwd(h, hs, dy, cw, cb, wg, bg, lam, seq, name):
    t, r2 = h.shape
    rw = r2 // 2
    ts = min(TS_LRU, seq)
    nt = t // ts
    n8 = ts // SUB
    hp16 = ts // SUB16
    hp8 = ts // SUB

    def body(hg_ref, hr_ref, hrh_ref, hs_ref, hsh_ref, dy_ref, cw_ref, cb_ref, wg_ref, bg_ref, lam_ref,
             dh_ref, dbin_ref, dcw_ref, dcb_ref, dwg_ref, dbg_ref, dlam_ref,
             a_s, g_s, l_s, c_lam, c_a, c_dxr):
        i = pl.program_id(0)
        ri = nt - 1 - i

        @pl.when(i == 0)
        def _():
            for r in (dbin_ref, dcw_ref, dcb_ref, dwg_ref, dbg_ref, dlam_ref):
                r[...] = jnp.zeros_like(r)

        @pl.when(lax.rem((ri + 1) * ts, seq) == 0)
        def _():
            c_lam[...] = jnp.zeros_like(c_lam)
            c_a[...] = jnp.zeros_like(c_a)
            c_dxr[...] = jnp.zeros_like(c_dxr)

        keep = jnp.where(lax.rem(ri * ts, seq) == 0, 0.0, 1.0)
        rin = hr_ref[...].astype(F32)
        ext = jnp.concatenate([hrh_ref[...].astype(F32) * keep, rin], axis=0)
        shifted = [_down(ext, 3 - k, SUB16) for k in range(4)]
        xr = cb_ref[...]
        for k in range(4):
            xr = xr + _row(cw_ref, k) * shifted[k]
        r, ig, sp, a, mult = _lru_coeffs(xr, wg_ref, bg_ref, lam_ref)
        gel, dgel = _gelu_and_grad(hg_ref[...].astype(F32))
        dyf = dy_ref[...].astype(F32)
        hsv = hs_ref[...]
        dg = dyf * hsv * dgel

        a_s[...] = _up(jnp.concatenate([a, c_a[...]], axis=0), 1, ts)
        g_s[...] = dyf * gel
        c_a[...] = a[0:SUB, :]
        row = lax.broadcasted_iota(jnp.int32, (SUB, rw), 0)

        def step(j, carry):
            off = pl.multiple_of((n8 - 1 - j) * SUB, SUB)
            a8 = a_s[pl.ds(off, SUB), :]
            b8 = g_s[pl.ds(off, SUB), :]
            for d in (1, 2, 4):
                m = row < SUB - d
                b8 = jnp.where(m, a8 * pltpu.roll(b8, SUB - d, 0) + b8, b8)
                a8 = jnp.where(m, a8 * pltpu.roll(a8, SUB - d, 0), a8)
            l8 = a8 * carry + b8
            l_s[pl.ds(off, SUB), :] = l8
            return _colsum(jnp.where(row == 0, l8, 0.0))

        c_lam[...] = lax.fori_loop(0, n8, step, c_lam[...])
        lamv = l_s[...]
        hs_prev = _down(jnp.concatenate([hsh_ref[...] * keep, hsv], axis=0), 1, SUB)
        da = lamv * hs_prev
        t1 = lamv * xr
        dmult = t1 * ig
        dig = t1 * mult
        dxr = lamv * mult * ig
        dla = da * a - dmult * (a * a) / mult
        dr = dla * (-LRU_C * sp)
        dlam_ref[...] += _colsum(dla * (-LRU_C) * r) * (-_sigmoid(-lam_ref[...]))
        dgr = dr * r * (1.0 - r)
        dgi = dig * ig * (1.0 - ig)
        parts = []
        for hd in range(LRU_HEADS):
            sl = slice(hd * LRU_BLOCK, (hd + 1) * LRU_BLOCK)
            dgt = jnp.concatenate([dgr[:, sl], dgi[:, sl]], axis=1)
            dbg_ref[hd:hd + 1, :] += _colsum(dgt)
            dgt16 = dgt.astype(BF16)
            parts.append(lax.dot_general(dgt16, wg_ref[hd], (((1,), (1,)), ((), ())),
                                         preferred_element_type=F32))
            dwg_ref[hd] += lax.dot_general(xr[:, sl].astype(BF16), dgt16, (((0,), (0,)), ((), ())),
                                           preferred_element_type=F32)
        dxr = dxr + jnp.concatenate(parts, axis=1)

        dcb_ref[...] += _colsum(dxr)
        for k in range(4):
            dcw_ref[k:k + 1, :] += _colsum(dxr * shifted[k])
        ext2 = jnp.concatenate([dxr, c_dxr[...]], axis=0)
        drb = _row(cw_ref, 3) * dxr
        for k in range(3):
            drb = drb + _row(cw_ref, k) * _up(ext2, 3 - k, ts)
        c_dxr[...] = dxr[0:SUB, :]
        dh_ref[0] = dg.astype(BF16)
        dh_ref[1] = drb.astype(BF16)
        dbin_ref[:, 0:rw] += _colsum(dg)
        dbin_ref[:, rw:] += _colsum(drb)

    rev = lambda c: pl.BlockSpec((ts, rw), lambda i: (nt - 1 - i, c))
    full = lambda shp: pl.BlockSpec(shp, lambda i: (0,) * len(shp))
    nh = LRU_HEADS
    return pl.pallas_call(
        body, name=name, grid=(nt,),
        in_specs=[rev(0), rev(1),
                  pl.BlockSpec((SUB16, rw), lambda i: (jnp.maximum((nt - 1 - i) * hp16 - 1, 0), 1)),
                  rev(0),
                  pl.BlockSpec((SUB, rw), lambda i: (jnp.maximum((nt - 1 - i) * hp8 - 1, 0), 0)),
                  rev(0), full((4, rw)), full((1, rw)), full(wg.shape), full(bg.shape), full((1, rw))],
        out_specs=[pl.BlockSpec((2, ts, rw), lambda i: (0, nt - 1 - i, 0)), full((1, r2)), full((4, rw)),
                   full((1, rw)), full((nh, LRU_BLOCK, 2 * LRU_BLOCK)), full((nh, 2 * LRU_BLOCK)), full((1, rw))],
        out_shape=[jax.ShapeDtypeStruct((2, t, rw), BF16), jax.ShapeDtypeStruct((1, r2), F32),
                   jax.ShapeDtypeStruct((4, rw), F32), jax.ShapeDtypeStruct((1, rw), F32),
                   jax.ShapeDtypeStruct((nh, LRU_BLOCK, 2 * LRU_BLOCK), F32),
                   jax.ShapeDtypeStruct((nh, 2 * LRU_BLOCK), F32), jax.ShapeDtypeStruct((1, rw), F32)],
        scratch_shapes=[pltpu.VMEM((ts, rw), F32), pltpu.VMEM((ts, rw), F32), pltpu.VMEM((ts, rw), F32),
                        pltpu.VMEM((1, rw), F32), pltpu.VMEM((SUB, rw), F32), pltpu.VMEM((SUB, rw), F32)],
        compiler_params=_cp("arbitrary"))(h, h, h, hs, hs, dy, cw, cb, wg, bg, lam)


def _row_tile(rows, cols, mult, elems=ELEMS_PER_BLOCK):
    cap = max(mult, elems // cols)
    best = None
    for cand in range(mult, min(rows, cap) + 1, mult):
        if rows % cand == 0:
            best = cand
    return best if best is not None else rows


def _core_index():
    return lax.axis_index("c").astype(jnp.int32).reshape(1)


def _chip_index():
    return (2 * lax.axis_index("x") + lax.axis_index("y")).astype(jnp.int32).reshape(1)


def _add_pair(p4, r3, name):
    s, _, rows, cols = p4.shape
    tr = _row_tile(rows, cols, SUB16)

    def body(c_ref, a_ref, b_ref, o_ref):
        o_ref[...] = (a_ref[...].astype(F32) + b_ref[...].astype(F32)).astype(o_ref.dtype)

    blk = pl.BlockSpec((None, tr, cols), lambda k, i, c_ref: (k, i, 0))
    return pl.pallas_call(
        body, name=name,
        grid_spec=pltpu.PrefetchScalarGridSpec(
            num_scalar_prefetch=1, grid=(s, rows // tr),
            in_specs=[pl.BlockSpec((None, None, tr, cols), lambda k, i, c_ref: (k, c_ref[0], i, 0)), blk],
            out_specs=blk),
        out_shape=jax.ShapeDtypeStruct((s, rows, cols), p4.dtype),
        compiler_params=_cp("parallel", "parallel"))(_core_index(), p4, r3)


def _add_chips(r, name):
    shape = r.shape[1:]
    r3 = r.reshape(N_CHIPS, -1, shape[-1])
    _, rows, cols = r3.shape
    tr = _row_tile(rows, cols, SUB16)

    def body(r_ref, o_ref):
        s = r_ref[0].astype(F32) + r_ref[1].astype(F32)
        s = s + r_ref[2].astype(F32)
        o_ref[...] = s + r_ref[3].astype(F32)

    out = pl.pallas_call(body, name=name, grid=(rows // tr,),
                         in_specs=[pl.BlockSpec((N_CHIPS, tr, cols), lambda i: (0, i, 0))],
                         out_specs=pl.BlockSpec((tr, cols), lambda i: (i, 0)),
                         out_shape=jax.ShapeDtypeStruct((rows, cols), F32),
                         compiler_params=_cp("parallel"))(r3)
    return out.reshape(shape)


def _adamw(w3, g_mine, g_sib, m3, v3, name):
    nl, r, cols = w3.shape
    rows = r // 2
    flat = [arr.reshape(nl, 2, rows, cols) for arr in (w3, m3, v3)]
    tr = _row_tile(rows, cols, SUB)

    def body(c_ref, w_ref, gm_ref, gs_ref, m_ref, v_ref, g_ref, d_ref, mo_ref, vo_ref):
        gg = jnp.where(pl.program_id(1) == c_ref[0], gm_ref[...], gs_ref[...])
        m2 = ADAM_B1 * m_ref[...] + (1.0 - ADAM_B1) * gg
        v2 = ADAM_B2 * v_ref[...] + (1.0 - ADAM_B2) * (gg * gg)
        m_hat = m2 / (1.0 - ADAM_B1 ** ADAM_STEP)
        v_hat = v2 / (1.0 - ADAM_B2 ** ADAM_STEP)
        g_ref[...] = gg
        d_ref[...] = -ADAM_LR * (m_hat / (jnp.sqrt(v_hat) + ADAM_EPS) + ADAM_WD * w_ref[...])
        mo_ref[...] = m2
        vo_ref[...] = v2

    blk = pl.BlockSpec((None, None, tr, cols), lambda l, hh, i, c_ref: (l, hh, i, 0))
    gblk = pl.BlockSpec((None, tr, cols), lambda l, hh, i, c_ref: (l, i, 0))
    outs = pl.pallas_call(
        body, name=name,
        grid_spec=pltpu.PrefetchScalarGridSpec(
            num_scalar_prefetch=1, grid=(nl, 2, rows // tr),
            in_specs=[blk, gblk, gblk, blk, blk], out_specs=[blk] * 4),
        out_shape=[jax.ShapeDtypeStruct((nl, 2, rows, cols), F32)] * 4,
        compiler_params=_cp("parallel", "parallel", "parallel"))(_core_index(), flat[0], g_mine, g_sib, flat[1],
                                                                 flat[2])
    return tuple(o.reshape(nl, r, cols) for o in outs)


def _place(src3, layer, kind, dtype, name):
    _, r, c = src3.shape
    tr = _row_tile(r, c, SUB16)
    in_spec = pl.BlockSpec((None, tr, c), lambda i, my_ref: (layer, i, 0))
    if kind == "col":
        out_spec = pl.BlockSpec((tr, c), lambda i, my_ref: (i, my_ref[0]))
        out_shape = (r, N_CHIPS * c)
    else:
        out_spec = pl.BlockSpec((None, tr, c), lambda i, my_ref: (my_ref[0], i, 0))
        out_shape = (N_CHIPS, r, c)

    def body(my_ref, i_ref, o_ref):
        o_ref[...] = i_ref[...].astype(o_ref.dtype)

    return pl.pallas_call(
        body, name=name,
        grid_spec=pltpu.PrefetchScalarGridSpec(num_scalar_prefetch=1, grid=(r // tr,), in_specs=[in_spec],
                                               out_specs=out_spec),
        out_shape=jax.ShapeDtypeStruct(out_shape, dtype),
        compiler_params=_cp("parallel"))(_chip_index(), src3)


def _half(ref, c, h):
    return ref.at[pl.ds(c * h, h)]


def _position():
    x = lax.axis_index("x")
    y = lax.axis_index("y")
    c = lax.axis_index("c")
    return x, y, c


def _peer_chip(x, y, j):
    tx = 1 - x if j & 2 else x
    ty = 1 - y if j & 1 else y
    return tx, ty


def _remote(src, dst, ssem, rsem, dev):
    return pltpu.make_async_remote_copy(src_ref=src, dst_ref=dst, send_sem=ssem, recv_sem=rsem,
                                        device_id=dev, device_id_type=pl.DeviceIdType.MESH)


_ANY = pl.BlockSpec(memory_space=pl.ANY)


def _unit_view(kind, ref, k):
    if kind == "col":
        n = ref.shape[1] // N_CHIPS
        return ref.at[:, pl.ds(pl.multiple_of(k * n, LANE), n)]
    return ref.at[k]


def _all_gather(placed, kinds):
    nt = len(placed)

    def body(*refs):
        outs = refs[nt:2 * nt]
        ssem, rsem = refs[2 * nt:]
        x, y, c = _position()
        my = 2 * x + y
        sib = (x, y, 1 - c)

        def part(t, k, core):
            view = _unit_view(kinds[t], outs[t], k)
            h = view.shape[0] // 2
            return _half(view, core, h)

        sends, fwds = [], []
        for t in range(nt):
            own = part(t, my, c)
            for j in (1, 2, 3):
                tx, ty = _peer_chip(x, y, j)
                cp = _remote(own, own, ssem.at[6 * t + j - 1], rsem.at[6 * t + j - 1], (tx, ty, c))
                cp.start()
                sends.append(cp)
        for t in range(nt):
            for j in (1, 2, 3):
                tx, ty = _peer_chip(x, y, j)
                got = part(t, 2 * tx + ty, c)
                _remote(got, got, ssem.at[6 * t + j - 1], rsem.at[6 * t + j - 1], sib).wait_recv()
                cp = _remote(got, got, ssem.at[6 * t + 2 + j], rsem.at[6 * t + 2 + j], sib)
                cp.start()
                fwds.append(cp)
        for t in range(nt):
            for j in (1, 2, 3):
                tx, ty = _peer_chip(x, y, j)
                other = part(t, 2 * tx + ty, 1 - c)
                _remote(other, other, ssem.at[6 * t + 2 + j], rsem.at[6 * t + 2 + j], sib).wait_recv()
        for cp in sends + fwds:
            cp.wait_send()

    return pl.pallas_call(
        body, name="all_gather", in_specs=[_ANY] * nt, out_specs=[_ANY] * nt,
        out_shape=[jax.ShapeDtypeStruct(p.shape, p.dtype) for p in placed],
        input_output_aliases={t: t for t in range(nt)},
        scratch_shapes=[pltpu.SemaphoreType.DMA((6 * nt,)), pltpu.SemaphoreType.DMA((6 * nt,))],
    )(*placed)


def _rider_start(refs, kinds, ssem, rsem):
    x, y, c = _position()
    my = 2 * x + y
    for u, (ref, kind) in enumerate(zip(refs, kinds)):
        own = _unit_view(kind, ref, my)
        for j in (1, 2, 3):
            tx, ty = _peer_chip(x, y, j)
            _remote(own, own, ssem.at[3 * u + j - 1], rsem.at[3 * u + j - 1], (tx, ty, c)).start()


def _rider_wait(refs, kinds, ssem, rsem):
    x, y, c = _position()
    for u, (ref, kind) in enumerate(zip(refs, kinds)):
        for j in (1, 2, 3):
            tx, ty = _peer_chip(x, y, j)
            got = _unit_view(kind, ref, 2 * tx + ty)
            _remote(got, got, ssem.at[3 * u + j - 1], rsem.at[3 * u + j - 1], (tx, ty, c)).wait()


def _rider_args(rider):
    bufs = [b for b, _ in rider]
    kinds = [k for _, k in rider]
    n = len(bufs)
    sems = [pltpu.SemaphoreType.DMA((3 * n,)), pltpu.SemaphoreType.DMA((3 * n,))] if n else []
    return bufs, kinds, [_ANY] * n, [jax.ShapeDtypeStruct(b.shape, b.dtype) for b in bufs], sems


def _d2d_stream(src4, other_half, name):
    s, _, rows, cols = src4.shape
    tr = _row_tile(rows, cols, SUB16, STREAM_ELEMS_PER_BLOCK)
    nblk = rows // tr

    nh = src4.shape[1]

    def body(c_ref, src_ref, dst_ref, ssem, rsem):
        k = pl.program_id(0)
        i = pl.program_id(1)
        x, y, c = _position()
        sib = (x, y, 1 - c)
        blk = dst_ref.at[pl.ds(pl.multiple_of((k * nblk + i) * tr, SUB16), tr)]
        cp = _remote(src_ref, blk, ssem, rsem, sib)
        cp.start()
        cp.wait_send()

        @pl.when(jnp.logical_and(k == s - 1, i == nblk - 1))
        def _():
            _remote(dst_ref, dst_ref, ssem, rsem, sib).wait_recv()

    if other_half:
        src_map = lambda k, i, c_ref: ((k * nh + 1 - c_ref[0]) * nblk + i, 0)
    else:
        src_map = lambda k, i, c_ref: (k * nh * nblk + i, 0)
    out = pl.pallas_call(
        body, name=name,
        grid_spec=pltpu.PrefetchScalarGridSpec(
            num_scalar_prefetch=1, grid=(s, nblk),
            in_specs=[pl.BlockSpec((tr, cols), src_map)], out_specs=_ANY,
            scratch_shapes=[pltpu.SemaphoreType.DMA, pltpu.SemaphoreType.DMA]),
        out_shape=jax.ShapeDtypeStruct((s * rows, cols), src4.dtype),
        compiler_params=_cp("arbitrary", "arbitrary"))(_core_index(), src4.reshape(s * nh * rows, cols))
    return out.reshape(s, rows, cols)


def _scatter_copies(srcs, lands, kinds, ssem, rsem, lsem):
    x, y, c = _position()
    my = 2 * x + y
    cps = []
    for u, (src, land, kind) in enumerate(zip(srcs, lands, kinds)):
        cps.append(pltpu.make_async_copy(_unit_view(kind, src, my), land.at[my], lsem.at[u]))
        for j in (1, 2, 3):
            tx, ty = _peer_chip(x, y, j)
            cps.append(_remote(_unit_view(kind, src, 2 * tx + ty), land.at[my], ssem.at[3 * u + j - 1],
                               rsem.at[3 * u + j - 1], (tx, ty, c)))
    return cps


def _scatter_args(scatter):
    srcs = [s for s, _ in scatter]
    kinds = [k for _, k in scatter]
    n = len(srcs)
    shapes = [jax.ShapeDtypeStruct((N_CHIPS, s.shape[0], s.shape[1] // N_CHIPS) if k == "col" else s.shape, s.dtype)
              for s, k in scatter]
    sems = [pltpu.SemaphoreType.DMA((3 * n,)), pltpu.SemaphoreType.DMA((3 * n,)),
            pltpu.SemaphoreType.DMA((n,))] if n else []
    return srcs, kinds, [_ANY] * n, shapes, sems


def _rs_scatter(scatter):
    srcs, kinds, specs, shapes, sems = _scatter_args(scatter)
    n = len(srcs)

    def body(*refs):
        cps = _scatter_copies(refs[:n], refs[n:2 * n], kinds, *refs[2 * n:])
        for cp in cps:
            cp.start()
        for cp in cps:
            cp.wait()

    return pl.pallas_call(body, name="rs_scatter", in_specs=specs, out_specs=specs, out_shape=shapes,
                          scratch_shapes=sems)(*srcs)


SMALL = (("sc_conv_w", True), ("sc_conv_b", False), ("lru_b_in", True), ("lru_conv_w", True),
         ("lru_conv_b", True), ("lru_b_gate", True), ("lru_lambda", True), ("ffn_conv_w", True),
         ("ffn_conv_b", False), ("ln_g", True), ("ln_b", True))
PACK_ROW_MULT = 2 * SUB16


def _pack_rows(shapes):
    n = sum(math.prod(shapes[name]) for name, _ in SMALL)
    rows = -(-n // 128)
    return -(-rows // PACK_ROW_MULT) * PACK_ROW_MULT


def _pack_local(vals, shapes):
    flat = jnp.concatenate([vals[name].reshape(-1) for name, _ in SMALL])
    rows = _pack_rows(shapes)
    return jnp.pad(flat, (0, rows * 128 - flat.shape[0])).reshape(rows, 128)


def _unpack_local(pack, shapes):
    flat = pack.reshape(-1)
    out, off = {}, 0
    for name, _ in SMALL:
        n = math.prod(shapes[name])
        out[name] = flat[off:off + n].reshape(shapes[name])
        off += n
    return out


def _pack_slots(fulls, shapes):
    parts = []
    for name, sharded in SMALL:
        v = fulls[name]
        if sharded:
            ns = shapes[name][-1]
            v = jnp.moveaxis(v.reshape(v.shape[:-1] + (N_CHIPS, ns)), -2, 0).reshape(N_CHIPS, -1)
        else:
            v = jnp.broadcast_to(v.reshape(1, -1), (N_CHIPS, v.size))
        parts.append(v)
    flat = jnp.concatenate(parts, axis=1)
    rows = _pack_rows(shapes)
    return jnp.pad(flat, ((0, 0), (0, rows * 128 - flat.shape[1]))).reshape(N_CHIPS, rows, 128)


def _unpack_slots(packs, shapes):
    flat = packs.reshape(N_CHIPS, -1)
    out, off = {}, 0
    for name, sharded in SMALL:
        n = math.prod(shapes[name])
        if sharded:
            seg = flat[:, off:off + n].reshape((N_CHIPS,) + tuple(shapes[name]))
            seg = jnp.moveaxis(seg, 0, -2)
            out[name] = seg.reshape(seg.shape[:-2] + (N_CHIPS * shapes[name][-1],))
        off += n
    return out


WEIGHTS = ("sc_w_in", "sc_conv_w", "sc_conv_b", "sc_w_out", "lru_w_in", "lru_b_in", "lru_conv_w", "lru_conv_b",
           "lru_w_gate", "lru_b_gate", "lru_lambda", "lru_w_out", "ffn_w_up", "ffn_conv_w", "ffn_conv_b",
           "ffn_w_down", "ln_g", "ln_b")
GATHER_KIND = {"sc_w_in": "col", "sc_w_out": "lead", "lru_w_in": "col", "lru_w_out": "lead", "ffn_w_up": "col",
               "ffn_w_down": "lead"}


def kernel(x, sc_w_in, sc_conv_w, sc_conv_b, sc_w_out, lru_w_in, lru_b_in, lru_conv_w, lru_conv_b, lru_w_gate, lru_b_gate, lru_lambda, lru_w_out, ffn_w_up, ffn_conv_w, ffn_conv_b, ffn_w_down, ln_g, ln_b, loss_target, m_sc_w_in, m_sc_conv_w, m_sc_conv_b, m_sc_w_out, m_lru_w_in, m_lru_b_in, m_lru_conv_w, m_lru_conv_b, m_lru_w_gate, m_lru_b_gate, m_lru_lambda, m_lru_w_out, m_ffn_w_up, m_ffn_conv_w, m_ffn_conv_b, m_ffn_w_down, m_ln_g, m_ln_b, v_sc_w_in, v_sc_conv_w, v_sc_conv_b, v_sc_w_out, v_lru_w_in, v_lru_b_in, v_lru_conv_w, v_lru_conv_b, v_lru_w_gate, v_lru_b_gate, v_lru_lambda, v_lru_w_out, v_ffn_w_up, v_ffn_conv_w, v_ffn_conv_b, v_ffn_w_down, v_ln_g, v_ln_b):
    w = dict(zip(WEIGHTS, (sc_w_in, sc_conv_w, sc_conv_b, sc_w_out, lru_w_in, lru_b_in, lru_conv_w, lru_conv_b,
                           lru_w_gate, lru_b_gate, lru_lambda, lru_w_out, ffn_w_up, ffn_conv_w, ffn_conv_b,
                           ffn_w_down, ln_g, ln_b)))
    mom = dict(zip(WEIGHTS, (m_sc_w_in, m_sc_conv_w, m_sc_conv_b, m_sc_w_out, m_lru_w_in, m_lru_b_in, m_lru_conv_w,
                             m_lru_conv_b, m_lru_w_gate, m_lru_b_gate, m_lru_lambda, m_lru_w_out, m_ffn_w_up,
                             m_ffn_conv_w, m_ffn_conv_b, m_ffn_w_down, m_ln_g, m_ln_b)))
    vel = dict(zip(WEIGHTS, (v_sc_w_in, v_sc_conv_w, v_sc_conv_b, v_sc_w_out, v_lru_w_in, v_lru_b_in, v_lru_conv_w,
                             v_lru_conv_b, v_lru_w_gate, v_lru_b_gate, v_lru_lambda, v_lru_w_out, v_ffn_w_up,
                             v_ffn_conv_w, v_ffn_conv_b, v_ffn_w_down, v_ln_g, v_ln_b)))
    bd, seq, d = x.shape
    t = bd * seq
    small_shapes = {name: w[name].shape for name, _ in SMALL}

    w_pack = _pack_local(w, small_shapes)
    gate_shape = w["lru_w_gate"].shape
    bufs = {(n, l): (_place(w[n], l, k, BF16, "place_w"), k)
            for n, k in GATHER_KIND.items() for l in range(w[n].shape[0])}
    bufs["gate"] = (_place(w["lru_w_gate"].reshape(1, -1, gate_shape[-1]), 0, "lead", BF16, "place_w"), "lead")
    bufs["pack"] = (_place(w_pack[None], 0, "lead", F32, "place_w"), "lead")

    def layer_keys(i):
        mixer = ("sc_w_in", "sc_w_out") if i % 2 == 0 else ("lru_w_in", "lru_w_out")
        return [(mixer[0], i // 2), (mixer[1], i // 2)], [("ffn_w_up", i), ("ffn_w_down", i)]

    def gathered(keys, arrays):
        for key, arr in zip(keys, arrays):
            bufs[key] = (arr, bufs[key][1])

    def wt(name, l):
        arr = bufs[(name, l)][0]
        return arr.reshape(1, -1, arr.shape[-1])

    first = layer_keys(0)[0] + layer_keys(0)[1] + ["gate", "pack"]
    gathered(first, _all_gather([bufs[k][0] for k in first], [bufs[k][1] for k in first]))
    full = _unpack_slots(bufs["pack"][0], small_shapes)
    full["sc_conv_b"] = sc_conv_b
    full["ffn_conv_b"] = ffn_conv_b
    wg_full = jnp.moveaxis(bufs["gate"][0].reshape((N_CHIPS,) + gate_shape), 0, -2)
    wg_full = wg_full.reshape(wg_full.shape[:-2] + (2 * LRU_BLOCK,))
    f = N_CHIPS * w["ffn_w_down"].shape[1]
    rw = N_CHIPS * w["lru_w_out"].shape[1]

    x0 = x.reshape(t, d)
    xb = x0.astype(BF16)
    cur, cur_b = x0, xb
    saved = []

    for i in range(DEPTH):
        j = i // 2
        s = {"xb": cur_b}
        if i % 2 == 0:
            h = _mm_nn(cur_b, wt("sc_w_in", j), 0, None, 3 * d, "sc_in")
            q = _sc_fwd(h, full["sc_conv_w"][j], full["sc_conv_b"][j][None], seq, "sc_fwd")
            z1, x1, x1b = _mm_nn_ln(q, wt("sc_w_out", j), 0, cur, full["ln_g"][i, 0][None], full["ln_b"][i, 0][None],
                                    "sc_out_ln")
        else:
            h = _mm_nn(cur_b, wt("lru_w_in", j), 0, full["lru_b_in"][j][None], 2 * rw, "lru_in")
            hs, q = _lru_fwd(h, full["lru_conv_w"][j], full["lru_conv_b"][j][None], wg_full[j],
                             full["lru_b_gate"][j], full["lru_lambda"][j][None], seq, "lru_fwd")
            s["hs"] = hs
            z1, x1, x1b = _mm_nn_ln(q, wt("lru_w_out", j), 0, cur, full["ln_g"][i, 0][None],
                                    full["ln_b"][i, 0][None], "lru_out_ln")
        s.update(h=h, q=q, z1=z1, x1b=x1b)
        mixer_next, ffn_next = layer_keys(i + 1) if i + 1 < DEPTH else ([], [])
        behind_up, behind_down = mixer_next + ffn_next[:1], ffn_next[1:]
        h3, pre3, act, *arrived = _ffn_up(x1b, wt("ffn_w_up", i), 0, full["ffn_conv_w"][i],
                                          full["ffn_conv_b"][i][None], seq, "ffn_up",
                                          rider=[bufs[k] for k in behind_up])
        gathered(behind_up, arrived)
        z2, x2, x2b, *arrived = _mm_nn_ln(act, wt("ffn_w_down", i), 0, x1, full["ln_g"][i, 1][None],
                                          full["ln_b"][i, 1][None], "ffn_down_ln",
                                          rider=[bufs[k] for k in behind_down])
        gathered(behind_down, arrived)
        s.update(h3=h3, pre3=pre3, act=act, z2=z2)
        saved.append(s)
        cur, cur_b = x2, x2b

    dcur, loss_parts = _loss_bwd(cur, loss_target.reshape(t, d))
    loss = lax.psum(jnp.sum(loss_parts), MESH_AXES)

    def pair_sum(p, kind):
        lead = kind == "lead"
        p4 = p.reshape(N_CHIPS if lead else 1, 2, -1, p.shape[-1])
        chip_sum = _add_pair(p4, _d2d_stream(p4, True, "rs_swap"), "rs_add_pair")
        return chip_sum if lead else chip_sum[0]

    gp = {n: [None] * w[n].shape[0] for n, _ in SMALL}
    gp["lru_w_gate"] = [None] * gate_shape[0]
    landed, pending = {}, []
    for i in reversed(range(DEPTH)):
        j = i // 2
        s = saved[i]
        mixer_keys, ffn_keys = layer_keys(i)
        dz2, dz2b, dg, db = _ln_bwd(dcur, s["z2"], full["ln_g"][i, 1][None], "ln_bwd")
        gp["ln_g"][i] = [None, dg[0]]
        gp["ln_b"][i] = [None, db[0]]
        p_down = _mm_tn(s["act"], dz2b[None], f // 2, d, "ffn_down_dw").reshape(N_CHIPS, -1, d)
        dh3, dcwg, dcwv, dcbg, dcbv, *lands = _ffn_down_bwd(
            dz2b, wt("ffn_w_down", i), 0, s["h3"], s["pre3"], full["ffn_conv_w"][i], seq, "ffn_down_bwd",
            scatter=[(chip_sum, kind) for _, chip_sum, kind in pending])
        landed.update({key: land for (key, _, _), land in zip(pending, lands)})
        gp["ffn_conv_w"][i] = jnp.concatenate([dcwg, dcwv], axis=1)
        gp["ffn_conv_b"][i] = jnp.concatenate([dcbg[0], dcbv[0]])
        dx1 = _mm_nt_res(dh3, wt("ffn_w_up", i), 0, dz2, f // 2, "ffn_up_dx")
        p_up = _mm_tn(s["x1b"], dh3, d, f // 2, "ffn_up_dw")
        ffn_partials = ((ffn_keys[0], p_up, "col"), (ffn_keys[1], p_down, "lead"))
        early = [(key, pair_sum(p, kind), kind) for key, p, kind in ffn_partials] if i == 0 else []
        early_scatter = [(chip_sum, kind) for _, chip_sum, kind in early]
        dz1, dz1b, dg, db = _ln_bwd(dx1, s["z1"], full["ln_g"][i, 0][None], "ln_bwd")
        gp["ln_g"][i][0] = dg[0]
        gp["ln_b"][i][0] = db[0]
        gp["ln_g"][i] = jnp.stack(gp["ln_g"][i])
        gp["ln_b"][i] = jnp.stack(gp["ln_b"][i])
        if i % 2 == 0:
            dq = _mm_nt(dz1b, wt("sc_w_out", j), 0, d, "sc_out_dx")
            p_out = _mm_tn(s["q"], dz1b[None], d, d, "sc_out_dw")
            dh3, dcw, dcb = _sc_bwd(s["h"], dq, full["sc_conv_w"][j], full["sc_conv_b"][j][None], seq, "sc_bwd")
            gp["sc_conv_w"][j] = dcw
            gp["sc_conv_b"][j] = dcb[0]
            res = _mm_nt_res(dh3, wt("sc_w_in", j), 0, dz1, d, "sc_in_dx", scatter=early_scatter)
            dcur, lands = (res[0], res[1:]) if early else (res, [])
            landed.update({key: land for (key, _, _), land in zip(early, lands)})
            p_in = _mm_tn(s["xb"], dh3, d, d, "sc_in_dw")
        else:
            dq = _mm_nt(dz1b, wt("lru_w_out", j), 0, rw, "lru_out_dx")
            p_out = _mm_tn(s["q"], dz1b[None], rw, d, "lru_out_dw")
            dh3, dbin, dcw, dcb, dwg, dbg, dlam = _lru_bwd(
                s["h"], s["hs"], dq, full["lru_conv_w"][j], full["lru_conv_b"][j][None], wg_full[j],
                full["lru_b_gate"][j], full["lru_lambda"][j][None], seq, "lru_bwd")
            gp["lru_b_in"][j] = dbin[0]
            gp["lru_conv_w"][j] = dcw
            gp["lru_conv_b"][j] = dcb[0]
            gp["lru_w_gate"][j] = dwg
            gp["lru_b_gate"][j] = dbg
            gp["lru_lambda"][j] = dlam[0]
            dcur = _mm_nt_res(dh3, wt("lru_w_in", j), 0, dz1, rw, "lru_in_dx")
            p_in = _mm_tn(s["xb"], dh3, d, rw, "lru_in_dw")
        layer_partials = ((mixer_keys[0], p_in, "col"), (mixer_keys[1], p_out.reshape(N_CHIPS, -1, d), "lead"))
        layer_partials += () if early else ffn_partials
        pending = [(key, pair_sum(p, kind), kind) for key, p, kind in layer_partials]
    grad_x = dcur.reshape(bd, seq, d)
    gp = {n: jnp.stack(v) for n, v in gp.items()}

    gate = gp["lru_w_gate"]
    gate = jnp.moveaxis(gate.reshape(gate.shape[:-1] + (N_CHIPS, gate_shape[-1])), -2, 0)
    gate = gate.astype(BF16).reshape(N_CHIPS, -1, gate_shape[-1])
    pending += [("gate", pair_sum(gate, "lead"), "lead"),
                ("pack", pair_sum(_pack_slots(gp, small_shapes), "lead"), "lead")]
    lands = _rs_scatter([(chip_sum, kind) for _, chip_sum, kind in pending])
    landed.update({key: land for (key, _, _), land in zip(pending, lands)})

    def update(keys, w3, m3, v3):
        g_mine = jnp.stack([_add_chips(landed[k], "rs_add_chips") for k in keys])
        g_sib = _d2d_stream(g_mine.reshape(1, 1, -1, g_mine.shape[-1]), False, "rs_share").reshape(g_mine.shape)
        return _adamw(w3, g_mine, g_sib, m3, v3, "adamw")

    g_out, d_out, m_out, v_out = {}, {}, {}, {}
    for n in GATHER_KIND:
        outs = update([(n, l) for l in range(w[n].shape[0])], w[n], mom[n], vel[n])
        g_out[n], d_out[n], m_out[n], v_out[n] = outs
    as_rows = lambda a: a.reshape(1, -1, a.shape[-1])
    outs = update(["gate"], as_rows(w["lru_w_gate"]), as_rows(mom["lru_w_gate"]), as_rows(vel["lru_w_gate"]))
    g_out["lru_w_gate"], d_out["lru_w_gate"], m_out["lru_w_gate"], v_out["lru_w_gate"] = (
        o.reshape(gate_shape) for o in outs)
    packs = update(["pack"], w_pack[None], _pack_local(mom, small_shapes)[None], _pack_local(vel, small_shapes)[None])
    for dst, pack in zip((g_out, d_out, m_out, v_out), packs):
        dst.update(_unpack_local(pack[0], small_shapes))

    return (loss, grad_x, *[g_out[n] for n in WEIGHTS], *[d_out[n] for n in WEIGHTS],
            *[m_out[n] for n in WEIGHTS], *[v_out[n] for n in WEIGHTS])
```

```python
import math

import jax
import jax.numpy as jnp
from jax import lax
from jax.experimental import pallas as pl
from jax.experimental.pallas import tpu as pltpu

F32 = jnp.float32
BF16 = jnp.bfloat16

DEPTH = 4
LRU_HEADS = 10
LRU_BLOCK = 128
LRU_C = 8.0
LN_EPS = 1e-5
ALPHA = (2.0 * DEPTH) ** 0.25
ADAM_LR, ADAM_B1, ADAM_B2, ADAM_EPS, ADAM_WD, ADAM_STEP = 0.001, 0.9, 0.999, 1e-08, 0.01, 10
N_CHIPS = 4
MESH_AXES = ("x", "y", "c")

VMEM_LIMIT_BYTES = 48 * 1024 * 1024
TM_MM = 512
TM_RES = 1024
TT_MM = 2048
TM_SC = 256
FFN_COL_BLOCKS = 2
RC = 128
LANE = 128
MXU_COLS = 256
TS_LRU = 128
TM_LN = 512
ELEMS_PER_BLOCK = 256 * 1024
STREAM_ELEMS_PER_BLOCK = 1024 * 1024
SUB = 8
SUB16 = 16


def _cp(*sem):
    return pltpu.CompilerParams(dimension_semantics=sem, vmem_limit_bytes=VMEM_LIMIT_BYTES)


def _sigmoid(v):
    return 1.0 / (1.0 + jnp.exp(-v))


def _softplus(v):
    e = jnp.exp(-jnp.abs(v))
    log1p = jnp.where(e < 1e-3, e * (1.0 - e * (0.5 - e * (1.0 / 3.0))), jnp.log(1.0 + e))
    return jnp.maximum(v, 0.0) + log1p


def _one_minus_exp(v):
    series = -v * (1.0 + v * (0.5 + v * (1.0 / 6.0 + v * (1.0 / 24.0))))
    return jnp.where(v > -0.02, series, 1.0 - jnp.exp(v))


def _gelu_and_grad(v):
    k = math.sqrt(2.0 / math.pi)
    t = jnp.tanh(k * (v + 0.044715 * v * v * v))
    val = 0.5 * v * (1.0 + t)
    grad = 0.5 * (1.0 + t) + 0.5 * v * (1.0 - t * t) * k * (1.0 + 3.0 * 0.044715 * v * v)
    return val, grad


def _down(ext, k, n_head):
    if k:
        ext = pltpu.roll(ext, k, 0)
    return ext[n_head:]


def _up(ext, k, n):
    if k:
        ext = pltpu.roll(ext, ext.shape[0] - k, 0)
    return ext[:n]


def _row(ref, k):
    return ref[k:k + 1, :]


def _colsum(v):
    return jnp.sum(v, axis=0, keepdims=True)


def _mm_nn(a, w3, l, bias, tn, name):
    m, k = a.shape
    n = w3.shape[2]
    tm = min(TM_MM, m)
    has_bias = bias is not None

    def body(*refs):
        if has_bias:
            a_ref, w_ref, b_ref, o_ref = refs
        else:
            a_ref, w_ref, o_ref = refs
        acc = jnp.dot(a_ref[...], w_ref[...], preferred_element_type=F32)
        if has_bias:
            acc = acc + b_ref[...]
        o_ref[...] = acc.astype(o_ref.dtype)

    in_specs = [pl.BlockSpec((tm, k), lambda i, j: (i, 0)),
                pl.BlockSpec((None, k, tn), lambda i, j: (l, 0, j))]
    args = [a, w3]
    if has_bias:
        in_specs.append(pl.BlockSpec((1, tn), lambda i, j: (0, j)))
        args.append(bias)
    return pl.pallas_call(
        body, name=name, grid=(m // tm, n // tn), in_specs=in_specs,
        out_specs=pl.BlockSpec((tm, tn), lambda i, j: (i, j)),
        out_shape=jax.ShapeDtypeStruct((m, n), BF16),
        compiler_params=_cp("parallel", "arbitrary"))(*args)


def _mm_nn_ln(a, w3, l, xres, g, b, name, rider=()):
    m, k = a.shape
    n = w3.shape[2]
    tm = min(TM_MM, m)
    bufs, rkinds, rspecs, rshapes, rsems = _rider_args(rider)
    nr = len(bufs)

    def body(*refs):
        a_ref, w_ref, x_ref, g_ref, b_ref = refs[:5]
        z_ref, xn_ref, xb_ref = refs[5 + nr:8 + nr]
        rout, sems = refs[8 + nr:8 + 2 * nr], refs[8 + 2 * nr:]
        if nr:
            @pl.when(pl.program_id(0) == 0)
            def _():
                _rider_start(rout, rkinds, *sems)

        y = jnp.dot(a_ref[...], w_ref[...], preferred_element_type=F32)
        z = ALPHA * x_ref[...] + y
        mu = jnp.mean(z, axis=-1, keepdims=True)
        zc = z - mu
        var = jnp.mean(zc * zc, axis=-1, keepdims=True)
        xn = zc * lax.rsqrt(var + LN_EPS) * g_ref[...] + b_ref[...]
        z_ref[...] = z
        xn_ref[...] = xn
        xb_ref[...] = xn.astype(BF16)
        if nr:
            @pl.when(pl.program_id(0) == m // tm - 1)
            def _():
                _rider_wait(rout, rkinds, *sems)

    row = pl.BlockSpec((tm, n), lambda i: (i, 0))
    vec = pl.BlockSpec((1, n), lambda i: (0, 0))
    return pl.pallas_call(
        body, name=name, grid=(m // tm,),
        in_specs=[pl.BlockSpec((tm, k), lambda i: (i, 0)),
                  pl.BlockSpec((None, k, n), lambda i: (l, 0, 0)), row, vec, vec] + rspecs,
        out_specs=[row, row, row] + rspecs,
        out_shape=[jax.ShapeDtypeStruct((m, n), F32), jax.ShapeDtypeStruct((m, n), F32),
                   jax.ShapeDtypeStruct((m, n), BF16)] + rshapes,
        input_output_aliases={5 + u: 3 + u for u in range(nr)},
        scratch_shapes=rsems,
        compiler_params=_cp("arbitrary"))(a, w3, xres, g, b, *bufs)


def _mm_nt(a, w3, l, tk, name):
    m, n = a.shape
    kd = w3.shape[1]
    tm = min(TM_MM, m)

    def body(a_ref, w_ref, o_ref):
        o_ref[...] = lax.dot_general(a_ref[...], w_ref[...], (((1,), (1,)), ((), ())),
                                     preferred_element_type=F32).astype(o_ref.dtype)

    return pl.pallas_call(
        body, name=name, grid=(m // tm, kd // tk),
        in_specs=[pl.BlockSpec((tm, n), lambda i, j: (i, 0)),
                  pl.BlockSpec((None, tk, n), lambda i, j: (l, j, 0))],
        out_specs=pl.BlockSpec((tm, tk), lambda i, j: (i, j)),
        out_shape=jax.ShapeDtypeStruct((m, kd), BF16),
        compiler_params=_cp("parallel", "arbitrary"))(a, w3)


def _mm_nt_res(dh3, w3, l, dz, tc, name, scatter=()):
    g, m, cg = dh3.shape
    kd = w3.shape[1]
    ncg = cg // tc
    nk = g * ncg
    tm = min(TM_RES, m)
    srcs, skinds, sspecs, sshapes, ssems = _scatter_args(scatter)
    ns = len(srcs)

    def body(*refs):
        a_ref, w_ref, dz_ref = refs[:3]
        sin, o_ref, lands = refs[3:3 + ns], refs[3 + ns], refs[4 + ns:4 + 2 * ns]
        acc, sems = refs[4 + 2 * ns], refs[5 + 2 * ns:]
        i = pl.program_id(0)
        k = pl.program_id(1)
        if ns:
            @pl.when(jnp.logical_and(i == 0, k == 0))
            def _():
                for cp in _scatter_copies(sin, lands, skinds, *sems):
                    cp.start()

        @pl.when(k == 0)
        def _():
            acc[...] = ALPHA * dz_ref[...]

        acc[...] += lax.dot_general(a_ref[...], w_ref[...], (((1,), (1,)), ((), ())),
                                    preferred_element_type=F32)

        @pl.when(k == nk - 1)
        def _():
            o_ref[...] = acc[...]

        if ns:
            @pl.when(jnp.logical_and(i == m // tm - 1, k == nk - 1))
            def _():
                for cp in _scatter_copies(sin, lands, skinds, *sems):
                    cp.wait()

    outs = pl.pallas_call(
        body, name=name, grid=(m // tm, nk),
        in_specs=[pl.BlockSpec((None, tm, tc), lambda i, k: (k // ncg, i, k % ncg)),
                  pl.BlockSpec((None, kd, tc), lambda i, k: (l, 0, k)),
                  pl.BlockSpec((tm, kd), lambda i, k: (i, 0))] + sspecs,
        out_specs=[pl.BlockSpec((tm, kd), lambda i, k: (i, 0))] + sspecs,
        out_shape=[jax.ShapeDtypeStruct((m, kd), F32)] + sshapes,
        scratch_shapes=[pltpu.VMEM((tm, kd), F32)] + ssems,
        compiler_params=_cp("arbitrary", "arbitrary"))(dh3, w3, dz, *srcs)
    return outs if ns else outs[0]


def _mm_tn(a, b3, tka, tnb, name):
    t, ka = a.shape
    g, _, cg = b3.shape
    ncg = cg // tnb
    tt = min(TT_MM, t)
    nt = t // tt

    def body(a_ref, b_ref, o_ref, acc):
        s = pl.program_id(2)

        @pl.when(s == 0)
        def _():
            acc[...] = jnp.zeros_like(acc)

        acc[...] += lax.dot_general(a_ref[...], b_ref[...], (((0,), (0,)), ((), ())),
                                    preferred_element_type=F32)

        @pl.when(s == nt - 1)
        def _():
            o_ref[...] = acc[...].astype(o_ref.dtype)

    return pl.pallas_call(
        body, name=name, grid=(ka // tka, g * ncg, nt),
        in_specs=[pl.BlockSpec((tt, tka), lambda i, j, s: (s, i)),
                  pl.BlockSpec((None, tt, tnb), lambda i, j, s: (j // ncg, s, j % ncg))],
        out_specs=pl.BlockSpec((tka, tnb), lambda i, j, s: (i, j)),
        out_shape=jax.ShapeDtypeStruct((ka, g * cg), BF16),
        scratch_shapes=[pltpu.VMEM((tka, tnb), F32)],
        compiler_params=_cp("parallel", "parallel", "arbitrary"))(a, b3)


def _ln_bwd(dxn, z, g, name):
    m, d = z.shape
    tm = min(TM_LN, m)

    def body(dx_ref, z_ref, g_ref, dz_ref, dzb_ref, dg_ref, db_ref):
        @pl.when(pl.program_id(0) == 0)
        def _():
            dg_ref[...] = jnp.zeros_like(dg_ref)
            db_ref[...] = jnp.zeros_like(db_ref)

        zz = z_ref[...]
        dx = dx_ref[...]
        mu = jnp.mean(zz, axis=-1, keepdims=True)
        zc = zz - mu
        var = jnp.mean(zc * zc, axis=-1, keepdims=True)
        rstd = lax.rsqrt(var + LN_EPS)
        xh = zc * rstd
        dg_ref[...] += _colsum(dx * xh)
        db_ref[...] += _colsum(dx)
        dxh = dx * g_ref[...]
        m1 = jnp.mean(dxh, axis=-1, keepdims=True)
        m2 = jnp.mean(dxh * xh, axis=-1, keepdims=True)
        dz = rstd * (dxh - m1 - xh * m2)
        dz_ref[...] = dz
        dzb_ref[...] = dz.astype(BF16)

    row = pl.BlockSpec((tm, d), lambda i: (i, 0))
    vec = pl.BlockSpec((1, d), lambda i: (0, 0))
    return pl.pallas_call(
        body, name=name, grid=(m // tm,), in_specs=[row, row, vec],
        out_specs=[row, row, vec, vec],
        out_shape=[jax.ShapeDtypeStruct((m, d), F32), jax.ShapeDtypeStruct((m, d), BF16),
                   jax.ShapeDtypeStruct((1, d), F32), jax.ShapeDtypeStruct((1, d), F32)],
        compiler_params=_cp("arbitrary"))(dxn, z, g)


def _loss_bwd(y, target):
    m, d = y.shape
    tm = min(TM_LN, m)

    def body(y_ref, t_ref, dy_ref, ls_ref):
        @pl.when(pl.program_id(0) == 0)
        def _():
            ls_ref[...] = jnp.zeros_like(ls_ref)

        e = y_ref[...] - t_ref[...]
        dy_ref[...] = e * (1.0 / d)
        ls_ref[...] += _colsum(e * e) * (0.5 / d)

    row = pl.BlockSpec((tm, d), lambda i: (i, 0))
    return pl.pallas_call(
        body, name="loss_bwd", grid=(m // tm,), in_specs=[row, row],
        out_specs=[row, pl.BlockSpec((1, d), lambda i: (0, 0))],
        out_shape=[jax.ShapeDtypeStruct((m, d), F32), jax.ShapeDtypeStruct((1, d), F32)],
        compiler_params=_cp("arbitrary"))(y, target)


def _sc_fwd(h, cw, cb, seq, name):
    t, d3 = h.shape
    d = d3 // 3
    tm = min(TM_SC, seq)

    def body(hb_ref, hc_ref, hv_ref, cw_ref, cb_ref, q_ref, carry):
        i = pl.program_id(0)

        @pl.when(lax.rem(i * tm, seq) == 0)
        def _():
            carry[...] = jnp.zeros_like(carry)

        p = hc_ref[...].astype(F32) * hv_ref[...].astype(F32)
        ext = jnp.concatenate([carry[...], p], axis=0)
        u = cb_ref[...] + _row(cw_ref, 0) * _down(ext, 2, SUB) + _row(cw_ref, 1) * _down(ext, 1, SUB) \
            + _row(cw_ref, 2) * p
        q_ref[...] = (hb_ref[...].astype(F32) * u).astype(BF16)
        carry[...] = p[tm - SUB:, :]

    blk = lambda c: pl.BlockSpec((tm, d), lambda i: (i, c))
    return pl.pallas_call(
        body, name=name, grid=(t // tm,),
        in_specs=[blk(0), blk(1), blk(2), pl.BlockSpec((3, d), lambda i: (0, 0)),
                  pl.BlockSpec((1, d), lambda i: (0, 0))],
        out_specs=pl.BlockSpec((tm, d), lambda i: (i, 0)),
        out_shape=jax.ShapeDtypeStruct((t, d), BF16),
        scratch_shapes=[pltpu.VMEM((SUB, d), F32)],
        compiler_params=_cp("arbitrary"))(h, h, h, cw, cb)


def _sc_bwd(h, dq, cw, cb, seq, name):
    t, d3 = h.shape
    d = d3 // 3
    tm = min(TM_SC, seq)
    nt = t // tm
    hpt = tm // SUB16

    def body(hb_ref, hc_ref, hv_ref, hch_ref, hvh_ref, dq_ref, cw_ref, cb_ref,
             dh_ref, dcw_ref, dcb_ref, carry):
        i = pl.program_id(0)
        ri = nt - 1 - i

        @pl.when(i == 0)
        def _():
            dcw_ref[...] = jnp.zeros_like(dcw_ref)
            dcb_ref[...] = jnp.zeros_like(dcb_ref)

        @pl.when(lax.rem((ri + 1) * tm, seq) == 0)
        def _():
            carry[...] = jnp.zeros_like(carry)

        keep = jnp.where(lax.rem(ri * tm, seq) == 0, 0.0, 1.0)
        gb = hb_ref[...].astype(F32)
        gc = hc_ref[...].astype(F32)
        v = hv_ref[...].astype(F32)
        p = gc * v
        p_head = hch_ref[...].astype(F32) * hvh_ref[...].astype(F32) * keep
        ext = jnp.concatenate([p_head, p], axis=0)
        pm2 = _down(ext, 2, SUB16)
        pm1 = _down(ext, 1, SUB16)
        u = cb_ref[...] + _row(cw_ref, 0) * pm2 + _row(cw_ref, 1) * pm1 + _row(cw_ref, 2) * p
        dqf = dq_ref[...].astype(F32)
        du = dqf * gb
        dcb_ref[...] += _colsum(du)
        dcw_ref[0:1, :] += _colsum(du * pm2)
        dcw_ref[1:2, :] += _colsum(du * pm1)
        dcw_ref[2:3, :] += _colsum(du * p)
        ext2 = jnp.concatenate([du, carry[...]], axis=0)
        dp = _row(cw_ref, 2) * du + _row(cw_ref, 1) * _up(ext2, 1, tm) + _row(cw_ref, 0) * _up(ext2, 2, tm)
        carry[...] = du[0:SUB, :]
        dh_ref[0] = (dqf * u).astype(BF16)
        dh_ref[1] = (dp * v).astype(BF16)
        dh_ref[2] = (dp * gc).astype(BF16)

    blk = lambda c: pl.BlockSpec((tm, d), lambda i: (nt - 1 - i, c))
    head = lambda c: pl.BlockSpec((SUB16, d), lambda i: (jnp.maximum((nt - 1 - i) * hpt - 1, 0), c))
    vec = lambda r: pl.BlockSpec((r, d), lambda i: (0, 0))
    return pl.pallas_call(
        body, name=name, grid=(nt,),
        in_specs=[blk(0), blk(1), blk(2), head(1), head(2),
                  pl.BlockSpec((tm, d), lambda i: (nt - 1 - i, 0)), vec(3), vec(1)],
        out_specs=[pl.BlockSpec((3, tm, d), lambda i: (0, nt - 1 - i, 0)), vec(3), vec(1)],
        out_shape=[jax.ShapeDtypeStruct((3, t, d), BF16), jax.ShapeDtypeStruct((3, d), F32),
                   jax.ShapeDtypeStruct((1, d), F32)],
        scratch_shapes=[pltpu.VMEM((SUB, d), F32)],
        compiler_params=_cp("arbitrary"))(h, h, h, h, h, dq, cw, cb)


def _fold(v):
    return jnp.sum(v.reshape(v.shape[0] // SUB, SUB, v.shape[1]), axis=0)


def _ffn_up(xb, w3, l, cw, cb, seq, name, rider=()):
    t, d = xb.shape
    f = w3.shape[2] // 2
    tc = f // FFN_COL_BLOCKS
    tm = min(TM_MM, seq)
    nc = FFN_COL_BLOCKS
    bufs, rkinds, rspecs, rshapes, rsems = _rider_args(rider)
    nr = len(bufs)

    def body(*refs):
        x_ref, wg_ref, wv_ref, cwg_ref, cwv_ref, cbg_ref, cbv_ref = refs[:7]
        h_ref, pre_ref, act_ref = refs[7 + nr:10 + nr]
        rout = refs[10 + nr:10 + 2 * nr]
        eg, ev = refs[10 + 2 * nr:12 + 2 * nr]
        sems = refs[12 + 2 * nr:]
        i = pl.program_id(1)
        if nr:
            @pl.when(jnp.logical_and(pl.program_id(0) == 0, i == 0))
            def _():
                _rider_start(rout, rkinds, *sems)

        @pl.when(lax.rem(i * tm, seq) == 0)
        def _():
            eg[0:SUB, :] = jnp.zeros((SUB, tc), F32)
            ev[0:SUB, :] = jnp.zeros((SUB, tc), F32)

        xx = x_ref[...]

        def matmul(lo, hi):
            eg[SUB:, lo:hi] = jnp.dot(xx, wg_ref[:, lo:hi], preferred_element_type=F32)
            ev[SUB:, lo:hi] = jnp.dot(xx, wv_ref[:, lo:hi], preferred_element_type=F32)

        def epilogue(lo, hi):
            for c0 in range(lo, hi, LANE):
                cols = slice(c0, c0 + LANE)
                taps = [[ref[k:k + 1, cols] for k in range(3)] + [bref[:, cols]]
                        for ref, bref in ((cwg_ref, cbg_ref), (cwv_ref, cbv_ref))]
                for r0 in range(0, tm, RC):
                    rows = slice(r0, r0 + RC)
                    pres = []
                    for half, e_ref in enumerate((eg, ev)):
                        w0, w1, w2, bias = taps[half]
                        e = e_ref[r0:r0 + RC + SUB, cols]
                        cur = e[SUB:]
                        pre = bias + w0 * _down(e, 2, SUB) + w1 * _down(e, 1, SUB) + w2 * cur
                        h_ref[half, rows, cols] = cur.astype(BF16)
                        pre_ref[half, rows, cols] = pre.astype(BF16)
                        pres.append(pre)
                    act_ref[rows, cols] = (pres[0] * _sigmoid(pres[0]) * pres[1]).astype(BF16)

        blocks = [(lo, min(lo + MXU_COLS, tc)) for lo in range(0, tc, MXU_COLS)]
        matmul(*blocks[0])
        for b, blk in enumerate(blocks):
            if b + 1 < len(blocks):
                matmul(*blocks[b + 1])
            epilogue(*blk)
        eg[0:SUB, :] = eg[tm:tm + SUB, :]
        ev[0:SUB, :] = ev[tm:tm + SUB, :]
        if nr:
            @pl.when(jnp.logical_and(pl.program_id(0) == nc - 1, i == t // tm - 1))
            def _():
                _rider_wait(rout, rkinds, *sems)

    wspec = lambda off: pl.BlockSpec((None, d, tc), lambda j, i: (l, 0, j + off))
    vec = lambda r, off: pl.BlockSpec((r, tc), lambda j, i: (0, j + off))
    pair = pl.BlockSpec((2, tm, tc), lambda j, i: (0, i, j))
    return pl.pallas_call(
        body, name=name, grid=(nc, t // tm),
        in_specs=[pl.BlockSpec((tm, d), lambda j, i: (i, 0)), wspec(0), wspec(nc),
                  vec(3, 0), vec(3, nc), vec(1, 0), vec(1, nc)] + rspecs,
        out_specs=[pair, pair, pl.BlockSpec((tm, tc), lambda j, i: (i, j))] + rspecs,
        out_shape=[jax.ShapeDtypeStruct((2, t, f), BF16), jax.ShapeDtypeStruct((2, t, f), BF16),
                   jax.ShapeDtypeStruct((t, f), BF16)] + rshapes,
        input_output_aliases={7 + u: 3 + u for u in range(nr)},
        scratch_shapes=[pltpu.VMEM((tm + SUB, tc), F32), pltpu.VMEM((tm + SUB, tc), F32)] + rsems,
        compiler_params=_cp("arbitrary", "arbitrary"))(xb, w3, w3, cw, cw, cb, cb, *bufs)


def _ffn_down_bwd(dzb, wd3, l, h3, pre3, cw, seq, name, scatter=()):
    t, d = dzb.shape
    f = wd3.shape[1]
    tc = f // FFN_COL_BLOCKS
    tm = min(TM_MM, seq)
    nt = t // tm
    nc = FFN_COL_BLOCKS
    srcs, skinds, sspecs, sshapes, ssems = _scatter_args(scatter)
    ns = len(srcs)

    def body(*refs):
        dz_ref, wd_ref, h_ref, pre_ref, cwg_ref, cwv_ref = refs[:6]
        sin = refs[6:6 + ns]
        dh_ref, dcwg_ref, dcwv_ref, dcbg_ref, dcbv_ref = refs[6 + ns:11 + ns]
        lands = refs[11 + ns:11 + 2 * ns]
        da_s, carry = refs[11 + 2 * ns:13 + 2 * ns]
        sems = refs[13 + 2 * ns:]
        i = pl.program_id(1)
        ri = nt - 1 - i
        if ns:
            @pl.when(jnp.logical_and(pl.program_id(0) == 0, i == 0))
            def _():
                for cp in _scatter_copies(sin, lands, skinds, *sems):
                    cp.start()

        @pl.when(i == 0)
        def _():
            for r in (dcwg_ref, dcwv_ref, dcbg_ref, dcbv_ref):
                r[...] = jnp.zeros_like(r)

        @pl.when(lax.rem((ri + 1) * tm, seq) == 0)
        def _():
            carry[...] = jnp.zeros_like(carry)

        dz = dz_ref[...]

        def matmul(lo, hi):
            da_s[:, lo:hi] = lax.dot_general(dz, wd_ref[lo:hi, :], (((1,), (1,)), ((), ())),
                                             preferred_element_type=F32)

        def epilogue(lo, hi):
            for c0 in range(lo, hi, LANE):
                cols = slice(c0, c0 + LANE)
                taps = [[ref[k:k + 1, cols] for k in range(3)] for ref in (cwg_ref, cwv_ref)]
                acc = [jnp.zeros((SUB, LANE), F32)] * 8
                for r0 in range(tm - RC, -1, -RC):
                    rows = slice(r0, r0 + RC)
                    da = da_s[rows, cols]
                    gp = pre_ref[0, rows, cols].astype(F32)
                    vp = pre_ref[1, rows, cols].astype(F32)
                    sg = _sigmoid(gp)
                    dpres = (da * vp * (sg * (1.0 + gp * (1.0 - sg))), da * (gp * sg))
                    for half in range(2):
                        w0, w1, w2 = taps[half]
                        dpre = dpres[half]
                        ext = jnp.concatenate([dpre, carry[half, :, cols]], axis=0)
                        u1 = _up(ext, 1, RC)
                        u2 = _up(ext, 2, RC)
                        carry[half, :, cols] = dpre[0:SUB]
                        dh_ref[half, rows, cols] = (w2 * dpre + w1 * u1 + w0 * u2).astype(BF16)
                        hh = h_ref[half, rows, cols].astype(F32)
                        for k, term in enumerate((hh * u2, hh * u1, hh * dpre, dpre)):
                            acc[4 * half + k] = acc[4 * half + k] + _fold(term)
                for half, (dcw_ref, dcb_ref) in enumerate(((dcwg_ref, dcbg_ref), (dcwv_ref, dcbv_ref))):
                    for k in range(3):
                        dcw_ref[k:k + 1, cols] += _colsum(acc[4 * half + k])
                    dcb_ref[:, cols] += _colsum(acc[4 * half + 3])

        blocks = [(lo, min(lo + MXU_COLS, tc)) for lo in range(0, tc, MXU_COLS)]
        matmul(*blocks[0])
        for b, blk in enumerate(blocks):
            if b + 1 < len(blocks):
                matmul(*blocks[b + 1])
            epilogue(*blk)
        if ns:
            @pl.when(jnp.logical_and(pl.program_id(0) == nc - 1, i == nt - 1))
            def _():
                for cp in _scatter_copies(sin, lands, skinds, *sems):
                    cp.wait()

    pair = pl.BlockSpec((2, tm, tc), lambda j, i: (0, nt - 1 - i, j))
    vec = lambda off: pl.BlockSpec((3, tc), lambda j, i: (0, j + off))
    acc_spec = lambda r: pl.BlockSpec((r, tc), lambda j, i: (0, j))
    return pl.pallas_call(
        body, name=name, grid=(nc, nt),
        in_specs=[pl.BlockSpec((tm, d), lambda j, i: (nt - 1 - i, 0)),
                  pl.BlockSpec((None, tc, d), lambda j, i: (l, j, 0)), pair, pair, vec(0), vec(nc)] + sspecs,
        out_specs=[pair, acc_spec(3), acc_spec(3), acc_spec(1), acc_spec(1)] + sspecs,
        out_shape=[jax.ShapeDtypeStruct((2, t, f), BF16), jax.ShapeDtypeStruct((3, f), F32),
                   jax.ShapeDtypeStruct((3, f), F32), jax.ShapeDtypeStruct((1, f), F32),
                   jax.ShapeDtypeStruct((1, f), F32)] + sshapes,
        scratch_shapes=[pltpu.VMEM((tm, tc), F32), pltpu.VMEM((2, SUB, tc), F32)] + ssems,
        compiler_params=_cp("arbitrary", "arbitrary"))(dzb, wd3, h3, pre3, cw, cw, *srcs)


def _lru_gates(xr, wg_ref, bg_ref):
    rs, gs = [], []
    for hd in range(LRU_HEADS):
        xh = xr[:, hd * LRU_BLOCK:(hd + 1) * LRU_BLOCK].astype(BF16)
        gt = jnp.dot(xh, wg_ref[hd], preferred_element_type=F32) + _row(bg_ref, hd)
        rs.append(gt[:, :LRU_BLOCK])
        gs.append(gt[:, LRU_BLOCK:])
    return jnp.concatenate(rs, axis=1), jnp.concatenate(gs, axis=1)


def _lru_coeffs(xr, wg_ref, bg_ref, lam_ref):
    gr, gi = _lru_gates(xr, wg_ref, bg_ref)
    r = _sigmoid(gr)
    ig = _sigmoid(gi)
    sp = _softplus(-lam_ref[...])
    log_a = -LRU_C * r * sp
    a = jnp.exp(log_a)
    mult = jnp.sqrt(_one_minus_exp(2.0 * log_a))
    return r, ig, sp, a, mult


def _lru_fwd(h, cw, cb, wg, bg, lam, seq, name):
    t, r2 = h.shape
    rw = r2 // 2
    ts = min(TS_LRU, seq)
    n8 = ts // SUB

    def body(hg_ref, hr_ref, cw_ref, cb_ref, wg_ref, bg_ref, lam_ref, hs_ref, y_ref,
             a_s, b_s, cconv, cstate):
        i = pl.program_id(0)

        @pl.when(lax.rem(i * ts, seq) == 0)
        def _():
            cconv[...] = jnp.zeros_like(cconv)
            cstate[...] = jnp.zeros_like(cstate)

        rin = hr_ref[...].astype(F32)
        ext = jnp.concatenate([cconv[...], rin], axis=0)
        xr = cb_ref[...]
        for k in range(4):
            xr = xr + _row(cw_ref, k) * _down(ext, 3 - k, SUB)
        cconv[...] = rin[ts - SUB:, :]
        _, ig, _, a, mult = _lru_coeffs(xr, wg_ref, bg_ref, lam_ref)
        a_s[...] = a
        b_s[...] = mult * (ig * xr)
        row = lax.broadcasted_iota(jnp.int32, (SUB, rw), 0)

        def step(j, carry):
            off = pl.multiple_of(j * SUB, SUB)
            a8 = a_s[pl.ds(off, SUB), :]
            b8 = b_s[pl.ds(off, SUB), :]
            for d in (1, 2, 4):
                m = row >= d
                b8 = jnp.where(m, a8 * pltpu.roll(b8, d, 0) + b8, b8)
                a8 = jnp.where(m, a8 * pltpu.roll(a8, d, 0), a8)
            h8 = a8 * carry + b8
            hs_ref[pl.ds(off, SUB), :] = h8
            return _colsum(jnp.where(row == SUB - 1, h8, 0.0))

        cstate[...] = lax.fori_loop(0, n8, step, cstate[...])
        gel, _ = _gelu_and_grad(hg_ref[...].astype(F32))
        y_ref[...] = (hs_ref[...] * gel).astype(BF16)

    full = lambda shp: pl.BlockSpec(shp, lambda i: (0,) * len(shp))
    return pl.pallas_call(
        body, name=name, grid=(t // ts,),
        in_specs=[pl.BlockSpec((ts, rw), lambda i: (i, 0)), pl.BlockSpec((ts, rw), lambda i: (i, 1)),
                  full((4, rw)), full((1, rw)), full(wg.shape), full(bg.shape), full((1, rw))],
        out_specs=[pl.BlockSpec((ts, rw), lambda i: (i, 0)), pl.BlockSpec((ts, rw), lambda i: (i, 0))],
        out_shape=[jax.ShapeDtypeStruct((t, rw), F32), jax.ShapeDtypeStruct((t, rw), BF16)],
        scratch_shapes=[pltpu.VMEM((ts, rw), F32), pltpu.VMEM((ts, rw), F32),
                        pltpu.VMEM((SUB, rw), F32), pltpu.VMEM((1, rw), F32)],
        compiler_params=_cp("arbitrary"))(h, h, cw, cb, wg, bg, lam)


def _lru_bwd(h, hs, dy, cw, cb, wg, bg, lam, seq, name):
    t, r2 = h.shape
    rw = r2 // 2
    ts = min(TS_LRU, seq)
    nt = t // ts
    n8 = ts // SUB
    hp16 = ts // SUB16
    hp8 = ts // SUB

    def body(hg_ref, hr_ref, hrh_ref, hs_ref, hsh_ref, dy_ref, cw_ref, cb_ref, wg_ref, bg_ref, lam_ref,
             dh_ref, dbin_ref, dcw_ref, dcb_ref, dwg_ref, dbg_ref, dlam_ref,
             a_s, g_s, l_s, c_lam, c_a, c_dxr):
        i = pl.program_id(0)
        ri = nt - 1 - i

        @pl.when(i == 0)
        def _():
            for r in (dbin_ref, dcw_ref, dcb_ref, dwg_ref, dbg_ref, dlam_ref):
                r[...] = jnp.zeros_like(r)

        @pl.when(lax.rem((ri + 1) * ts, seq) == 0)
        def _():
            c_lam[...] = jnp.zeros_like(c_lam)
            c_a[...] = jnp.zeros_like(c_a)
            c_dxr[...] = jnp.zeros_like(c_dxr)

        keep = jnp.where(lax.rem(ri * ts, seq) == 0, 0.0, 1.0)
        rin = hr_ref[...].astype(F32)
        ext = jnp.concatenate([hrh_ref[...].astype(F32) * keep, rin], axis=0)
        shifted = [_down(ext, 3 - k, SUB16) for k in range(4)]
        xr = cb_ref[...]
        for k in range(4):
            xr = xr + _row(cw_ref, k) * shifted[k]
        r, ig, sp, a, mult = _lru_coeffs(xr, wg_ref, bg_ref, lam_ref)
        gel, dgel = _gelu_and_grad(hg_ref[...].astype(F32))
        dyf = dy_ref[...].astype(F32)
        hsv = hs_ref[...]
        dg = dyf * hsv * dgel

        a_s[...] = _up(jnp.concatenate([a, c_a[...]], axis=0), 1, ts)
        g_s[...] = dyf * gel
        c_a[...] = a[0:SUB, :]
        row = lax.broadcasted_iota(jnp.int32, (SUB, rw), 0)

        def step(j, carry):
            off = pl.multiple_of((n8 - 1 - j) * SUB, SUB)
            a8 = a_s[pl.ds(off, SUB), :]
            b8 = g_s[pl.ds(off, SUB), :]
            for d in (1, 2, 4):
                m = row < SUB - d
                b8 = jnp.where(m, a8 * pltpu.roll(b8, SUB - d, 0) + b8, b8)
                a8 = jnp.where(m, a8 * pltpu.roll(a8, SUB - d, 0), a8)
            l8 = a8 * carry + b8
            l_s[pl.ds(off, SUB), :] = l8
            return _colsum(jnp.where(row == 0, l8, 0.0))

        c_lam[...] = lax.fori_loop(0, n8, step, c_lam[...])
        lamv = l_s[...]
        hs_prev = _down(jnp.concatenate([hsh_ref[...] * keep, hsv], axis=0), 1, SUB)
        da = lamv * hs_prev
        t1 = lamv * xr
        dmult = t1 * ig
        dig = t1 * mult
        dxr = lamv * mult * ig
        dla = da * a - dmult * (a * a) / mult
        dr = dla * (-LRU_C * sp)
        dlam_ref[...] += _colsum(dla * (-LRU_C) * r) * (-_sigmoid(-lam_ref[...]))
        dgr = dr * r * (1.0 - r)
        dgi = dig * ig * (1.0 - ig)
        parts = []
        for hd in range(LRU_HEADS):
            sl = slice(hd * LRU_BLOCK, (hd + 1) * LRU_BLOCK)
            dgt = jnp.concatenate([dgr[:, sl], dgi[:, sl]], axis=1)
            dbg_ref[hd:hd + 1, :] += _colsum(dgt)
            dgt16 = dgt.astype(BF16)
            parts.append(lax.dot_general(dgt16, wg_ref[hd], (((1,), (1,)), ((), ())),
                                         preferred_element_type=F32))
            dwg_ref[hd] += lax.dot_general(xr[:, sl].astype(BF16), dgt16, (((0,), (0,)), ((), ())),
                                           preferred_element_type=F32)
        dxr = dxr + jnp.concatenate(parts, axis=1)

        dcb_ref[...] += _colsum(dxr)
        for k in range(4):
            dcw_ref[k:k + 1, :] += _colsum(dxr * shifted[k])
        ext2 = jnp.concatenate([dxr, c_dxr[...]], axis=0)
        drb = _row(cw_ref, 3) * dxr
        for k in range(3):
            drb = drb + _row(cw_ref, k) * _up(ext2, 3 - k, ts)
        c_dxr[...] = dxr[0:SUB, :]
        dh_ref[0] = dg.astype(BF16)
        dh_ref[1] = drb.astype(BF16)
        dbin_ref[:, 0:rw] += _colsum(dg)
        dbin_ref[:, rw:] += _colsum(drb)

    rev = lambda c: pl.BlockSpec((ts, rw), lambda i: (nt - 1 - i, c))
    full = lambda shp: pl.BlockSpec(shp, lambda i: (0,) * len(shp))
    nh = LRU_HEADS
    return pl.pallas_call(
        body, name=name, grid=(nt,),
        in_specs=[rev(0), rev(1),
                  pl.BlockSpec((SUB16, rw), lambda i: (jnp.maximum((nt - 1 - i) * hp16 - 1, 0), 1)),
                  rev(0),
                  pl.BlockSpec((SUB, rw), lambda i: (jnp.maximum((nt - 1 - i) * hp8 - 1, 0), 0)),
                  rev(0), full((4, rw)), full((1, rw)), full(wg.shape), full(bg.shape), full((1, rw))],
        out_specs=[pl.BlockSpec((2, ts, rw), lambda i: (0, nt - 1 - i, 0)), full((1, r2)), full((4, rw)),
                   full((1, rw)), full((nh, LRU_BLOCK, 2 * LRU_BLOCK)), full((nh, 2 * LRU_BLOCK)), full((1, rw))],
        out_shape=[jax.ShapeDtypeStruct((2, t, rw), BF16), jax.ShapeDtypeStruct((1, r2), F32),
                   jax.ShapeDtypeStruct((4, rw), F32), jax.ShapeDtypeStruct((1, rw), F32),
                   jax.ShapeDtypeStruct((nh, LRU_BLOCK, 2 * LRU_BLOCK), F32),
                   jax.ShapeDtypeStruct((nh, 2 * LRU_BLOCK), F32), jax.ShapeDtypeStruct((1, rw), F32)],
        scratch_shapes=[pltpu.VMEM((ts, rw), F32), pltpu.VMEM((ts, rw), F32), pltpu.VMEM((ts, rw), F32),
                        pltpu.VMEM((1, rw), F32), pltpu.VMEM((SUB, rw), F32), pltpu.VMEM((SUB, rw), F32)],
        compiler_params=_cp("arbitrary"))(h, h, h, hs, hs, dy, cw, cb, wg, bg, lam)


def _row_tile(rows, cols, mult, elems=ELEMS_PER_BLOCK):
    cap = max(mult, elems // cols)
    best = None
    for cand in range(mult, min(rows, cap) + 1, mult):
        if rows % cand == 0:
            best = cand
    return best if best is not None else rows


def _core_index():
    return lax.axis_index("c").astype(jnp.int32).reshape(1)


def _chip_index():
    return (2 * lax.axis_index("x") + lax.axis_index("y")).astype(jnp.int32).reshape(1)


def _add_pair(p4, r3, name):
    s, _, rows, cols = p4.shape
    tr = _row_tile(rows, cols, SUB16)

    def body(c_ref, a_ref, b_ref, o_ref):
        o_ref[...] = (a_ref[...].astype(F32) + b_ref[...].astype(F32)).astype(o_ref.dtype)

    blk = pl.BlockSpec((None, tr, cols), lambda k, i, c_ref: (k, i, 0))
    return pl.pallas_call(
        body, name=name,
        grid_spec=pltpu.PrefetchScalarGridSpec(
            num_scalar_prefetch=1, grid=(s, rows // tr),
            in_specs=[pl.BlockSpec((None, None, tr, cols), lambda k, i, c_ref: (k, c_ref[0], i, 0)), blk],
            out_specs=blk),
        out_shape=jax.ShapeDtypeStruct((s, rows, cols), p4.dtype),
        compiler_params=_cp("parallel", "parallel"))(_core_index(), p4, r3)


def _add_chips(r, name):
    shape = r.shape[1:]
    r3 = r.reshape(N_CHIPS, -1, shape[-1])
    _, rows, cols = r3.shape
    tr = _row_tile(rows, cols, SUB16)

    def body(r_ref, o_ref):
        s = r_ref[0].astype(F32) + r_ref[1].astype(F32)
        s = s + r_ref[2].astype(F32)
        o_ref[...] = s + r_ref[3].astype(F32)

    out = pl.pallas_call(body, name=name, grid=(rows // tr,),
                         in_specs=[pl.BlockSpec((N_CHIPS, tr, cols), lambda i: (0, i, 0))],
                         out_specs=pl.BlockSpec((tr, cols), lambda i: (i, 0)),
                         out_shape=jax.ShapeDtypeStruct((rows, cols), F32),
                         compiler_params=_cp("parallel"))(r3)
    return out.reshape(shape)


def _adamw(w3, g_mine, g_sib, m3, v3, name):
    nl, r, cols = w3.shape
    rows = r // 2
    flat = [arr.reshape(nl, 2, rows, cols) for arr in (w3, m3, v3)]
    tr = _row_tile(rows, cols, SUB)

    def body(c_ref, w_ref, gm_ref, gs_ref, m_ref, v_ref, g_ref, d_ref, mo_ref, vo_ref):
        gg = jnp.where(pl.program_id(1) == c_ref[0], gm_ref[...], gs_ref[...])
        m2 = ADAM_B1 * m_ref[...] + (1.0 - ADAM_B1) * gg
        v2 = ADAM_B2 * v_ref[...] + (1.0 - ADAM_B2) * (gg * gg)
        m_hat = m2 / (1.0 - ADAM_B1 ** ADAM_STEP)
        v_hat = v2 / (1.0 - ADAM_B2 ** ADAM_STEP)
        g_ref[...] = gg
        d_ref[...] = -ADAM_LR * (m_hat / (jnp.sqrt(v_hat) + ADAM_EPS) + ADAM_WD * w_ref[...])
        mo_ref[...] = m2
        vo_ref[...] = v2

    blk = pl.BlockSpec((None, None, tr, cols), lambda l, hh, i, c_ref: (l, hh, i, 0))
    gblk = pl.BlockSpec((None, tr, cols), lambda l, hh, i, c_ref: (l, i, 0))
    outs = pl.pallas_call(
        body, name=name,
        grid_spec=pltpu.PrefetchScalarGridSpec(
            num_scalar_prefetch=1, grid=(nl, 2, rows // tr),
            in_specs=[blk, gblk, gblk, blk, blk], out_specs=[blk] * 4),
        out_shape=[jax.ShapeDtypeStruct((nl, 2, rows, cols), F32)] * 4,
        compiler_params=_cp("parallel", "parallel", "parallel"))(_core_index(), flat[0], g_mine, g_sib, flat[1],
                                                                 flat[2])
    return tuple(o.reshape(nl, r, cols) for o in outs)


def _place(src3, layer, kind, dtype, name):
    _, r, c = src3.shape
    tr = _row_tile(r, c, SUB16)
    in_spec = pl.BlockSpec((None, tr, c), lambda i, my_ref: (layer, i, 0))
    if kind == "col":
        out_spec = pl.BlockSpec((tr, c), lambda i, my_ref: (i, my_ref[0]))
        out_shape = (r, N_CHIPS * c)
    else:
        out_spec = pl.BlockSpec((None, tr, c), lambda i, my_ref: (my_ref[0], i, 0))
        out_shape = (N_CHIPS, r, c)

    def body(my_ref, i_ref, o_ref):
        o_ref[...] = i_ref[...].astype(o_ref.dtype)

    return pl.pallas_call(
        body, name=name,
        grid_spec=pltpu.PrefetchScalarGridSpec(num_scalar_prefetch=1, grid=(r // tr,), in_specs=[in_spec],
                                               out_specs=out_spec),
        out_shape=jax.ShapeDtypeStruct(out_shape, dtype),
        compiler_params=_cp("parallel"))(_chip_index(), src3)


def _half(ref, c, h):
    return ref.at[pl.ds(c * h, h)]


def _position():
    x = lax.axis_index("x")
    y = lax.axis_index("y")
    c = lax.axis_index("c")
    return x, y, c


def _peer_chip(x, y, j):
    tx = 1 - x if j & 2 else x
    ty = 1 - y if j & 1 else y
    return tx, ty


def _remote(src, dst, ssem, rsem, dev):
    return pltpu.make_async_remote_copy(src_ref=src, dst_ref=dst, send_sem=ssem, recv_sem=rsem,
                                        device_id=dev, device_id_type=pl.DeviceIdType.MESH)


_ANY = pl.BlockSpec(memory_space=pl.ANY)


def _unit_view(kind, ref, k):
    if kind == "col":
        n = ref.shape[1] // N_CHIPS
        return ref.at[:, pl.ds(pl.multiple_of(k * n, LANE), n)]
    return ref.at[k]


def _all_gather(placed, kinds):
    nt = len(placed)

    def body(*refs):
        outs = refs[nt:2 * nt]
        ssem, rsem = refs[2 * nt:]
        x, y, c = _position()
        my = 2 * x + y
        sib = (x, y, 1 - c)

        def part(t, k, core):
            view = _unit_view(kinds[t], outs[t], k)
            h = view.shape[0] // 2
            return _half(view, core, h)

        sends, fwds = [], []
        for t in range(nt):
            own = part(t, my, c)
            for j in (1, 2, 3):
                tx, ty = _peer_chip(x, y, j)
                cp = _remote(own, own, ssem.at[6 * t + j - 1], rsem.at[6 * t + j - 1], (tx, ty, c))
                cp.start()
                sends.append(cp)
        for t in range(nt):
            for j in (1, 2, 3):
                tx, ty = _peer_chip(x, y, j)
                got = part(t, 2 * tx + ty, c)
                _remote(got, got, ssem.at[6 * t + j - 1], rsem.at[6 * t + j - 1], sib).wait_recv()
                cp = _remote(got, got, ssem.at[6 * t + 2 + j], rsem.at[6 * t + 2 + j], sib)
                cp.start()
                fwds.append(cp)
        for t in range(nt):
            for j in (1, 2, 3):
                tx, ty = _peer_chip(x, y, j)
                other = part(t, 2 * tx + ty, 1 - c)
                _remote(other, other, ssem.at[6 * t + 2 + j], rsem.at[6 * t + 2 + j], sib).wait_recv()
        for cp in sends + fwds:
            cp.wait_send()

    return pl.pallas_call(
        body, name="all_gather", in_specs=[_ANY] * nt, out_specs=[_ANY] * nt,
        out_shape=[jax.ShapeDtypeStruct(p.shape, p.dtype) for p in placed],
        input_output_aliases={t: t for t in range(nt)},
        scratch_shapes=[pltpu.SemaphoreType.DMA((6 * nt,)), pltpu.SemaphoreType.DMA((6 * nt,))],
    )(*placed)


def _rider_start(refs, kinds, ssem, rsem):
    x, y, c = _position()
    my = 2 * x + y
    for u, (ref, kind) in enumerate(zip(refs, kinds)):
        own = _unit_view(kind, ref, my)
        for j in (1, 2, 3):
            tx, ty = _peer_chip(x, y, j)
            _remote(own, own, ssem.at[3 * u + j - 1], rsem.at[3 * u + j - 1], (tx, ty, c)).start()


def _rider_wait(refs, kinds, ssem, rsem):
    x, y, c = _position()
    for u, (ref, kind) in enumerate(zip(refs, kinds)):
        for j in (1, 2, 3):
            tx, ty = _peer_chip(x, y, j)
            got = _unit_view(kind, ref, 2 * tx + ty)
            _remote(got, got, ssem.at[3 * u + j - 1], rsem.at[3 * u + j - 1], (tx, ty, c)).wait()


def _rider_args(rider):
    bufs = [b for b, _ in rider]
    kinds = [k for _, k in rider]
    n = len(bufs)
    sems = [pltpu.SemaphoreType.DMA((3 * n,)), pltpu.SemaphoreType.DMA((3 * n,))] if n else []
    return bufs, kinds, [_ANY] * n, [jax.ShapeDtypeStruct(b.shape, b.dtype) for b in bufs], sems


def _d2d_stream(src4, other_half, name):
    s, _, rows, cols = src4.shape
    tr = _row_tile(rows, cols, SUB16, STREAM_ELEMS_PER_BLOCK)
    nblk = rows // tr

    nh = src4.shape[1]

    def body(c_ref, src_ref, dst_ref, ssem, rsem):
        k = pl.program_id(0)
        i = pl.program_id(1)
        x, y, c = _position()
        sib = (x, y, 1 - c)
        blk = dst_ref.at[pl.ds(pl.multiple_of((k * nblk + i) * tr, SUB16), tr)]
        cp = _remote(src_ref, blk, ssem, rsem, sib)
        cp.start()
        cp.wait_send()

        @pl.when(jnp.logical_and(k == s - 1, i == nblk - 1))
        def _():
            _remote(dst_ref, dst_ref, ssem, rsem, sib).wait_recv()

    if other_half:
        src_map = lambda k, i, c_ref: ((k * nh + 1 - c_ref[0]) * nblk + i, 0)
    else:
        src_map = lambda k, i, c_ref: (k * nh * nblk + i, 0)
    out = pl.pallas_call(
        body, name=name,
        grid_spec=pltpu.PrefetchScalarGridSpec(
            num_scalar_prefetch=1, grid=(s, nblk),
            in_specs=[pl.BlockSpec((tr, cols), src_map)], out_specs=_ANY,
            scratch_shapes=[pltpu.SemaphoreType.DMA, pltpu.SemaphoreType.DMA]),
        out_shape=jax.ShapeDtypeStruct((s * rows, cols), src4.dtype),
        compiler_params=_cp("arbitrary", "arbitrary"))(_core_index(), src4.reshape(s * nh * rows, cols))
    return out.reshape(s, rows, cols)


def _scatter_copies(srcs, lands, kinds, ssem, rsem, lsem):
    x, y, c = _position()
    my = 2 * x + y
    cps = []
    for u, (src, land, kind) in enumerate(zip(srcs, lands, kinds)):
        cps.append(pltpu.make_async_copy(_unit_view(kind, src, my), land.at[my], lsem.at[u]))
        for j in (1, 2, 3):
            tx, ty = _peer_chip(x, y, j)
            cps.append(_remote(_unit_view(kind, src, 2 * tx + ty), land.at[my], ssem.at[3 * u + j - 1],
                               rsem.at[3 * u + j - 1], (tx, ty, c)))
    return cps


def _scatter_args(scatter):
    srcs = [s for s, _ in scatter]
    kinds = [k for _, k in scatter]
    n = len(srcs)
    shapes = [jax.ShapeDtypeStruct((N_CHIPS, s.shape[0], s.shape[1] // N_CHIPS) if k == "col" else s.shape, s.dtype)
              for s, k in scatter]
    sems = [pltpu.SemaphoreType.DMA((3 * n,)), pltpu.SemaphoreType.DMA((3 * n,)),
            pltpu.SemaphoreType.DMA((n,))] if n else []
    return srcs, kinds, [_ANY] * n, shapes, sems


def _rs_scatter(scatter):
    srcs, kinds, specs, shapes, sems = _scatter_args(scatter)
    n = len(srcs)

    def body(*refs):
        cps = _scatter_copies(refs[:n], refs[n:2 * n], kinds, *refs[2 * n:])
        for cp in cps:
            cp.start()
        for cp in cps:
            cp.wait()

    return pl.pallas_call(body, name="rs_scatter", in_specs=specs, out_specs=specs, out_shape=shapes,
                          scratch_shapes=sems)(*srcs)


SMALL = (("sc_conv_w", True), ("sc_conv_b", False), ("lru_b_in", True), ("lru_conv_w", True),
         ("lru_conv_b", True), ("lru_b_gate", True), ("lru_lambda", True), ("ffn_conv_w", True),
         ("ffn_conv_b", False), ("ln_g", True), ("ln_b", True))
PACK_ROW_MULT = 2 * SUB16


def _pack_rows(shapes):
    n = sum(math.prod(shapes[name]) for name, _ in SMALL)
    rows = -(-n // 128)
    return -(-rows // PACK_ROW_MULT) * PACK_ROW_MULT


def _pack_local(vals, shapes):
    flat = jnp.concatenate([vals[name].reshape(-1) for name, _ in SMALL])
    rows = _pack_rows(shapes)
    return jnp.pad(flat, (0, rows * 128 - flat.shape[0])).reshape(rows, 128)


def _unpack_local(pack, shapes):
    flat = pack.reshape(-1)
    out, off = {}, 0
    for name, _ in SMALL:
        n = math.prod(shapes[name])
        out[name] = flat[off:off + n].reshape(shapes[name])
        off += n
    return out


def _pack_slots(fulls, shapes):
    parts = []
    for name, sharded in SMALL:
        v = fulls[name]
        if sharded:
            ns = shapes[name][-1]
            v = jnp.moveaxis(v.reshape(v.shape[:-1] + (N_CHIPS, ns)), -2, 0).reshape(N_CHIPS, -1)
        else:
            v = jnp.broadcast_to(v.reshape(1, -1), (N_CHIPS, v.size))
        parts.append(v)
    flat = jnp.concatenate(parts, axis=1)
    rows = _pack_rows(shapes)
    return jnp.pad(flat, ((0, 0), (0, rows * 128 - flat.shape[1]))).reshape(N_CHIPS, rows, 128)


def _unpack_slots(packs, shapes):
    flat = packs.reshape(N_CHIPS, -1)
    out, off = {}, 0
    for name, sharded in SMALL:
        n = math.prod(shapes[name])
        if sharded:
            seg = flat[:, off:off + n].reshape((N_CHIPS,) + tuple(shapes[name]))
            seg = jnp.moveaxis(seg, 0, -2)
            out[name] = seg.reshape(seg.shape[:-2] + (N_CHIPS * shapes[name][-1],))
        off += n
    return out


WEIGHTS = ("sc_w_in", "sc_conv_w", "sc_conv_b", "sc_w_out", "lru_w_in", "lru_b_in", "lru_conv_w", "lru_conv_b",
           "lru_w_gate", "lru_b_gate", "lru_lambda", "lru_w_out", "ffn_w_up", "ffn_conv_w", "ffn_conv_b",
           "ffn_w_down", "ln_g", "ln_b")
GATHER_KIND = {"sc_w_in": "col", "sc_w_out": "lead", "lru_w_in": "col", "lru_w_out": "lead", "ffn_w_up": "col",
               "ffn_w_down": "lead"}


def kernel(x, sc_w_in, sc_conv_w, sc_conv_b, sc_w_out, lru_w_in, lru_b_in, lru_conv_w, lru_conv_b, lru_w_gate, lru_b_gate, lru_lambda, lru_w_out, ffn_w_up, ffn_conv_w, ffn_conv_b, ffn_w_down, ln_g, ln_b, loss_target, m_sc_w_in, m_sc_conv_w, m_sc_conv_b, m_sc_w_out, m_lru_w_in, m_lru_b_in, m_lru_conv_w, m_lru_conv_b, m_lru_w_gate, m_lru_b_gate, m_lru_lambda, m_lru_w_out, m_ffn_w_up, m_ffn_conv_w, m_ffn_conv_b, m_ffn_w_down, m_ln_g, m_ln_b, v_sc_w_in, v_sc_conv_w, v_sc_conv_b, v_sc_w_out, v_lru_w_in, v_lru_b_in, v_lru_conv_w, v_lru_conv_b, v_lru_w_gate, v_lru_b_gate, v_lru_lambda, v_lru_w_out, v_ffn_w_up, v_ffn_conv_w, v_ffn_conv_b, v_ffn_w_down, v_ln_g, v_ln_b):
    w = dict(zip(WEIGHTS, (sc_w_in, sc_conv_w, sc_conv_b, sc_w_out, lru_w_in, lru_b_in, lru_conv_w, lru_conv_b,
                           lru_w_gate, lru_b_gate, lru_lambda, lru_w_out, ffn_w_up, ffn_conv_w, ffn_conv_b,
                           ffn_w_down, ln_g, ln_b)))
    mom = dict(zip(WEIGHTS, (m_sc_w_in, m_sc_conv_w, m_sc_conv_b, m_sc_w_out, m_lru_w_in, m_lru_b_in, m_lru_conv_w,
                             m_lru_conv_b, m_lru_w_gate, m_lru_b_gate, m_lru_lambda, m_lru_w_out, m_ffn_w_up,
                             m_ffn_conv_w, m_ffn_conv_b, m_ffn_w_down, m_ln_g, m_ln_b)))
    vel = dict(zip(WEIGHTS, (v_sc_w_in, v_sc_conv_w, v_sc_conv_b, v_sc_w_out, v_lru_w_in, v_lru_b_in, v_lru_conv_w,
                             v_lru_conv_b, v_lru_w_gate, v_lru_b_gate, v_lru_lambda, v_lru_w_out, v_ffn_w_up,
                             v_ffn_conv_w, v_ffn_conv_b, v_ffn_w_down, v_ln_g, v_ln_b)))
    bd, seq, d = x.shape
    t = bd * seq
    small_shapes = {name: w[name].shape for name, _ in SMALL}

    w_pack = _pack_local(w, small_shapes)
    gate_shape = w["lru_w_gate"].shape
    bufs = {(n, l): (_place(w[n], l, k, BF16, "place_w"), k)
            for n, k in GATHER_KIND.items() for l in range(w[n].shape[0])}
    bufs["gate"] = (_place(w["lru_w_gate"].reshape(1, -1, gate_shape[-1]), 0, "lead", BF16, "place_w"), "lead")
    bufs["pack"] = (_place(w_pack[None], 0, "lead", F32, "place_w"), "lead")

    def layer_keys(i):
        mixer = ("sc_w_in", "sc_w_out") if i % 2 == 0 else ("lru_w_in", "lru_w_out")
        return [(mixer[0], i // 2), (mixer[1], i // 2)], [("ffn_w_up", i), ("ffn_w_down", i)]

    def gathered(keys, arrays):
        for key, arr in zip(keys, arrays):
            bufs[key] = (arr, bufs[key][1])

    def wt(name, l):
        arr = bufs[(name, l)][0]
        return arr.reshape(1, -1, arr.shape[-1])

    first = layer_keys(0)[0] + layer_keys(0)[1] + ["gate", "pack"]
    gathered(first, _all_gather([bufs[k][0] for k in first], [bufs[k][1] for k in first]))
    full = _unpack_slots(bufs["pack"][0], small_shapes)
    full["sc_conv_b"] = sc_conv_b
    full["ffn_conv_b"] = ffn_conv_b
    wg_full = jnp.moveaxis(bufs["gate"][0].reshape((N_CHIPS,) + gate_shape), 0, -2)
    wg_full = wg_full.reshape(wg_full.shape[:-2] + (2 * LRU_BLOCK,))
    f = N_CHIPS * w["ffn_w_down"].shape[1]
    rw = N_CHIPS * w["lru_w_out"].shape[1]

    x0 = x.reshape(t, d)
    xb = x0.astype(BF16)
    cur, cur_b = x0, xb
    saved = []

    for i in range(DEPTH):
        j = i // 2
        s = {"xb": cur_b}
        if i % 2 == 0:
            h = _mm_nn(cur_b, wt("sc_w_in", j), 0, None, 3 * d, "sc_in")
            q = _sc_fwd(h, full["sc_conv_w"][j], full["sc_conv_b"][j][None], seq, "sc_fwd")
            z1, x1, x1b = _mm_nn_ln(q, wt("sc_w_out", j), 0, cur, full["ln_g"][i, 0][None], full["ln_b"][i, 0][None],
                                    "sc_out_ln")
        else:
            h = _mm_nn(cur_b, wt("lru_w_in", j), 0, full["lru_b_in"][j][None], 2 * rw, "lru_in")
            hs, q = _lru_fwd(h, full["lru_conv_w"][j], full["lru_conv_b"][j][None], wg_full[j],
                             full["lru_b_gate"][j], full["lru_lambda"][j][None], seq, "lru_fwd")
            s["hs"] = hs
            z1, x1, x1b = _mm_nn_ln(q, wt("lru_w_out", j), 0, cur, full["ln_g"][i, 0][None],
                                    full["ln_b"][i, 0][None], "lru_out_ln")
        s.update(h=h, q=q, z1=z1, x1b=x1b)
        mixer_next, ffn_next = layer_keys(i + 1) if i + 1 < DEPTH else ([], [])
        behind_up, behind_down = mixer_next[:1] + ffn_next[:1], mixer_next[1:] + ffn_next[1:]
        h3, pre3, act, *arrived = _ffn_up(x1b, wt("ffn_w_up", i), 0, full["ffn_conv_w"][i],
                                          full["ffn_conv_b"][i][None], seq, "ffn_up",
                                          rider=[bufs[k] for k in behind_up])
        gathered(behind_up, arrived)
        z2, x2, x2b, *arrived = _mm_nn_ln(act, wt("ffn_w_down", i), 0, x1, full["ln_g"][i, 1][None],
                                          full["ln_b"][i, 1][None], "ffn_down_ln",
                                          rider=[bufs[k] for k in behind_down])
        gathered(behind_down, arrived)
        s.update(h3=h3, pre3=pre3, act=act, z2=z2)
        saved.append(s)
        cur, cur_b = x2, x2b

    dcur, loss_parts = _loss_bwd(cur, loss_target.reshape(t, d))
    loss = lax.psum(jnp.sum(loss_parts), MESH_AXES)

    def pair_sum(p, kind):
        lead = kind == "lead"
        p4 = p.reshape(N_CHIPS if lead else 1, 2, -1, p.shape[-1])
        chip_sum = _add_pair(p4, _d2d_stream(p4, True, "rs_swap"), "rs_add_pair")
        return chip_sum if lead else chip_sum[0]

    gp = {n: [None] * w[n].shape[0] for n, _ in SMALL}
    gp["lru_w_gate"] = [None] * gate_shape[0]
    landed, pending = {}, []
    for i in reversed(range(DEPTH)):
        j = i // 2
        s = saved[i]
        mixer_keys, ffn_keys = layer_keys(i)
        dz2, dz2b, dg, db = _ln_bwd(dcur, s["z2"], full["ln_g"][i, 1][None], "ln_bwd")
        gp["ln_g"][i] = [None, dg[0]]
        gp["ln_b"][i] = [None, db[0]]
        p_down = _mm_tn(s["act"], dz2b[None], f // 2, d, "ffn_down_dw").reshape(N_CHIPS, -1, d)
        dh3, dcwg, dcwv, dcbg, dcbv, *lands = _ffn_down_bwd(
            dz2b, wt("ffn_w_down", i), 0, s["h3"], s["pre3"], full["ffn_conv_w"][i], seq, "ffn_down_bwd",
            scatter=[(chip_sum, kind) for _, chip_sum, kind in pending])
        landed.update({key: land for (key, _, _), land in zip(pending, lands)})
        gp["ffn_conv_w"][i] = jnp.concatenate([dcwg, dcwv], axis=1)
        gp["ffn_conv_b"][i] = jnp.concatenate([dcbg[0], dcbv[0]])
        dx1 = _mm_nt_res(dh3, wt("ffn_w_up", i), 0, dz2, f // 2, "ffn_up_dx")
        p_up = _mm_tn(s["x1b"], dh3, d, f // 2, "ffn_up_dw")
        ffn_partials = ((ffn_keys[0], p_up, "col"), (ffn_keys[1], p_down, "lead"))
        early = [(key, pair_sum(p, kind), kind) for key, p, kind in ffn_partials] if i == 0 else []
        early_scatter = [(chip_sum, kind) for _, chip_sum, kind in early]
        dz1, dz1b, dg, db = _ln_bwd(dx1, s["z1"], full["ln_g"][i, 0][None], "ln_bwd")
        gp["ln_g"][i][0] = dg[0]
        gp["ln_b"][i][0] = db[0]
        gp["ln_g"][i] = jnp.stack(gp["ln_g"][i])
        gp["ln_b"][i] = jnp.stack(gp["ln_b"][i])
        if i % 2 == 0:
            dq = _mm_nt(dz1b, wt("sc_w_out", j), 0, d, "sc_out_dx")
            p_out = _mm_tn(s["q"], dz1b[None], d, d, "sc_out_dw")
            dh3, dcw, dcb = _sc_bwd(s["h"], dq, full["sc_conv_w"][j], full["sc_conv_b"][j][None], seq, "sc_bwd")
            gp["sc_conv_w"][j] = dcw
            gp["sc_conv_b"][j] = dcb[0]
            res = _mm_nt_res(dh3, wt("sc_w_in", j), 0, dz1, d, "sc_in_dx", scatter=early_scatter)
            dcur, lands = (res[0], res[1:]) if early else (res, [])
            landed.update({key: land for (key, _, _), land in zip(early, lands)})
            p_in = _mm_tn(s["xb"], dh3, d, d, "sc_in_dw")
        else:
            dq = _mm_nt(dz1b, wt("lru_w_out", j), 0, rw, "lru_out_dx")
            p_out = _mm_tn(s["q"], dz1b[None], rw, d, "lru_out_dw")
            dh3, dbin, dcw, dcb, dwg, dbg, dlam = _lru_bwd(
                s["h"], s["hs"], dq, full["lru_conv_w"][j], full["lru_conv_b"][j][None], wg_full[j],
                full["lru_b_gate"][j], full["lru_lambda"][j][None], seq, "lru_bwd")
            gp["lru_b_in"][j] = dbin[0]
            gp["lru_conv_w"][j] = dcw
            gp["lru_conv_b"][j] = dcb[0]
            gp["lru_w_gate"][j] = dwg
            gp["lru_b_gate"][j] = dbg
            gp["lru_lambda"][j] = dlam[0]
            dcur = _mm_nt_res(dh3, wt("lru_w_in", j), 0, dz1, rw, "lru_in_dx")
            p_in = _mm_tn(s["xb"], dh3, d, rw, "lru_in_dw")
        layer_partials = ((mixer_keys[0], p_in, "col"), (mixer_keys[1], p_out.reshape(N_CHIPS, -1, d), "lead"))
        layer_partials += () if early else ffn_partials
        pending = [(key, pair_sum(p, kind), kind) for key, p, kind in layer_partials]
    grad_x = dcur.reshape(bd, seq, d)
    gp = {n: jnp.stack(v) for n, v in gp.items()}

    gate = gp["lru_w_gate"]
    gate = jnp.moveaxis(gate.reshape(gate.shape[:-1] + (N_CHIPS, gate_shape[-1])), -2, 0)
    gate = gate.astype(BF16).reshape(N_CHIPS, -1, gate_shape[-1])
    pending += [("gate", pair_sum(gate, "lead"), "lead"),
                ("pack", pair_sum(_pack_slots(gp, small_shapes), "lead"), "lead")]
    lands = _rs_scatter([(chip_sum, kind) for _, chip_sum, kind in pending])
    landed.update({key: land for (key, _, _), land in zip(pending, lands)})

    def update(keys, w3, m3, v3):
        g_mine = jnp.stack([_add_chips(landed[k], "rs_add_chips") for k in keys])
        g_sib = _d2d_stream(g_mine.reshape(1, 1, -1, g_mine.shape[-1]), False, "rs_share").reshape(g_mine.shape)
        return _adamw(w3, g_mine, g_sib, m3, v3, "adamw")

    g_out, d_out, m_out, v_out = {}, {}, {}, {}
    for n in GATHER_KIND:
        outs = update([(n, l) for l in range(w[n].shape[0])], w[n], mom[n], vel[n])
        g_out[n], d_out[n], m_out[n], v_out[n] = outs
    as_rows = lambda a: a.reshape(1, -1, a.shape[-1])
    outs = update(["gate"], as_rows(w["lru_w_gate"]), as_rows(mom["lru_w_gate"]), as_rows(vel["lru_w_gate"]))
    g_out["lru_w_gate"], d_out["lru_w_gate"], m_out["lru_w_gate"], v_out["lru_w_gate"] = (
        o.reshape(gate_shape) for o in outs)
    packs = update(["pack"], w_pack[None], _pack_local(mom, small_shapes)[None], _pack_local(vel, small_shapes)[None])
    for dst, pack in zip((g_out, d_out, m_out, v_out), packs):
        dst.update(_unpack_local(pack[0], small_shapes))

    return (loss, grad_x, *[g_out[n] for n in WEIGHTS], *[d_out[n] for n in WEIGHTS],
            *[m_out[n] for n in WEIGHTS], *[v_out[n] for n in WEIGHTS])
```

```python
import math

import jax
import jax.numpy as jnp
from jax import lax
from jax.experimental import pallas as pl
from jax.experimental.pallas import tpu as pltpu

F32 = jnp.float32
BF16 = jnp.bfloat16

DEPTH = 4
LRU_HEADS = 10
LRU_BLOCK = 128
LRU_C = 8.0
LN_EPS = 1e-5
ALPHA = (2.0 * DEPTH) ** 0.25
ADAM_LR, ADAM_B1, ADAM_B2, ADAM_EPS, ADAM_WD, ADAM_STEP = 0.001, 0.9, 0.999, 1e-08, 0.01, 10
N_CHIPS = 4
MESH_AXES = ("x", "y", "c")

VMEM_LIMIT_BYTES = 48 * 1024 * 1024
TM_MM = 512
TM_RES = 1024
TT_MM = 2048
TM_SC = 256
FFN_COL_BLOCKS = 2
RC = 128
LANE = 128
MXU_COLS = 256
TS_LRU = 128
TM_LN = 512
ELEMS_PER_BLOCK = 256 * 1024
STREAM_ELEMS_PER_BLOCK = 1024 * 1024
SUB = 8
SUB16 = 16


def _cp(*sem):
    return pltpu.CompilerParams(dimension_semantics=sem, vmem_limit_bytes=VMEM_LIMIT_BYTES)


def _sigmoid(v):
    return 0.5 + 0.5 * jnp.tanh(0.5 * v)


def _softplus(v):
    e = jnp.exp(-jnp.abs(v))
    log1p = jnp.where(e < 1e-3, e * (1.0 - e * (0.5 - e * (1.0 / 3.0))), jnp.log(1.0 + e))
    return jnp.maximum(v, 0.0) + log1p


def _one_minus_exp(v):
    series = -v * (1.0 + v * (0.5 + v * (1.0 / 6.0 + v * (1.0 / 24.0))))
    return jnp.where(v > -0.02, series, 1.0 - jnp.exp(v))


def _gelu_and_grad(v):
    k = math.sqrt(2.0 / math.pi)
    t = jnp.tanh(k * (v + 0.044715 * v * v * v))
    val = 0.5 * v * (1.0 + t)
    grad = 0.5 * (1.0 + t) + 0.5 * v * (1.0 - t * t) * k * (1.0 + 3.0 * 0.044715 * v * v)
    return val, grad


def _down(ext, k, n_head):
    if k:
        ext = pltpu.roll(ext, k, 0)
    return ext[n_head:]


def _up(ext, k, n):
    if k:
        ext = pltpu.roll(ext, ext.shape[0] - k, 0)
    return ext[:n]


def _row(ref, k):
    return ref[k:k + 1, :]


def _colsum(v):
    return jnp.sum(v, axis=0, keepdims=True)


def _mm_nn(a, w3, l, bias, tn, name):
    m, k = a.shape
    n = w3.shape[2]
    tm = min(TM_MM, m)
    has_bias = bias is not None

    def body(*refs):
        if has_bias:
            a_ref, w_ref, b_ref, o_ref = refs
        else:
            a_ref, w_ref, o_ref = refs
        acc = jnp.dot(a_ref[...], w_ref[...], preferred_element_type=F32)
        if has_bias:
            acc = acc + b_ref[...]
        o_ref[...] = acc.astype(o_ref.dtype)

    in_specs = [pl.BlockSpec((tm, k), lambda i, j: (i, 0)),
                pl.BlockSpec((None, k, tn), lambda i, j: (l, 0, j))]
    args = [a, w3]
    if has_bias:
        in_specs.append(pl.BlockSpec((1, tn), lambda i, j: (0, j)))
        args.append(bias)
    return pl.pallas_call(
        body, name=name, grid=(m // tm, n // tn), in_specs=in_specs,
        out_specs=pl.BlockSpec((tm, tn), lambda i, j: (i, j)),
        out_shape=jax.ShapeDtypeStruct((m, n), BF16),
        compiler_params=_cp("parallel", "arbitrary"))(*args)


def _mm_nn_ln(a, w3, l, xres, g, b, name, rider=()):
    m, k = a.shape
    n = w3.shape[2]
    tm = min(TM_MM, m)
    bufs, rkinds, rspecs, rshapes, rsems = _rider_args(rider)
    nr = len(bufs)

    def body(*refs):
        a_ref, w_ref, x_ref, g_ref, b_ref = refs[:5]
        z_ref, xn_ref, xb_ref = refs[5 + nr:8 + nr]
        rout, sems = refs[8 + nr:8 + 2 * nr], refs[8 + 2 * nr:]
        if nr:
            @pl.when(pl.program_id(0) == 0)
            def _():
                _rider_start(rout, rkinds, *sems)

        y = jnp.dot(a_ref[...], w_ref[...], preferred_element_type=F32)
        z = ALPHA * x_ref[...] + y
        mu = jnp.mean(z, axis=-1, keepdims=True)
        zc = z - mu
        var = jnp.mean(zc * zc, axis=-1, keepdims=True)
        xn = zc * lax.rsqrt(var + LN_EPS) * g_ref[...] + b_ref[...]
        z_ref[...] = z
        xn_ref[...] = xn
        xb_ref[...] = xn.astype(BF16)
        if nr:
            @pl.when(pl.program_id(0) == m // tm - 1)
            def _():
                _rider_wait(rout, rkinds, *sems)

    row = pl.BlockSpec((tm, n), lambda i: (i, 0))
    vec = pl.BlockSpec((1, n), lambda i: (0, 0))
    return pl.pallas_call(
        body, name=name, grid=(m // tm,),
        in_specs=[pl.BlockSpec((tm, k), lambda i: (i, 0)),
                  pl.BlockSpec((None, k, n), lambda i: (l, 0, 0)), row, vec, vec] + rspecs,
        out_specs=[row, row, row] + rspecs,
        out_shape=[jax.ShapeDtypeStruct((m, n), F32), jax.ShapeDtypeStruct((m, n), F32),
                   jax.ShapeDtypeStruct((m, n), BF16)] + rshapes,
        input_output_aliases={5 + u: 3 + u for u in range(nr)},
        scratch_shapes=rsems,
        compiler_params=_cp("arbitrary"))(a, w3, xres, g, b, *bufs)


def _mm_nt(a, w3, l, tk, name):
    m, n = a.shape
    kd = w3.shape[1]
    tm = min(TM_MM, m)

    def body(a_ref, w_ref, o_ref):
        o_ref[...] = lax.dot_general(a_ref[...], w_ref[...], (((1,), (1,)), ((), ())),
                                     preferred_element_type=F32).astype(o_ref.dtype)

    return pl.pallas_call(
        body, name=name, grid=(m // tm, kd // tk),
        in_specs=[pl.BlockSpec((tm, n), lambda i, j: (i, 0)),
                  pl.BlockSpec((None, tk, n), lambda i, j: (l, j, 0))],
        out_specs=pl.BlockSpec((tm, tk), lambda i, j: (i, j)),
        out_shape=jax.ShapeDtypeStruct((m, kd), BF16),
        compiler_params=_cp("parallel", "arbitrary"))(a, w3)


def _mm_nt_res(dh3, w3, l, dz, tc, name, scatter=()):
    g, m, cg = dh3.shape
    kd = w3.shape[1]
    ncg = cg // tc
    nk = g * ncg
    tm = min(TM_RES, m)
    srcs, skinds, sspecs, sshapes, ssems = _scatter_args(scatter)
    ns = len(srcs)

    def body(*refs):
        a_ref, w_ref, dz_ref = refs[:3]
        sin, o_ref, lands = refs[3:3 + ns], refs[3 + ns], refs[4 + ns:4 + 2 * ns]
        acc, sems = refs[4 + 2 * ns], refs[5 + 2 * ns:]
        i = pl.program_id(0)
        k = pl.program_id(1)
        if ns:
            @pl.when(jnp.logical_and(i == 0, k == 0))
            def _():
                for cp in _scatter_copies(sin, lands, skinds, *sems):
                    cp.start()

        @pl.when(k == 0)
        def _():
            acc[...] = ALPHA * dz_ref[...]

        acc[...] += lax.dot_general(a_ref[...], w_ref[...], (((1,), (1,)), ((), ())),
                                    preferred_element_type=F32)

        @pl.when(k == nk - 1)
        def _():
            o_ref[...] = acc[...]

        if ns:
            @pl.when(jnp.logical_and(i == m // tm - 1, k == nk - 1))
            def _():
                for cp in _scatter_copies(sin, lands, skinds, *sems):
                    cp.wait()

    outs = pl.pallas_call(
        body, name=name, grid=(m // tm, nk),
        in_specs=[pl.BlockSpec((None, tm, tc), lambda i, k: (k // ncg, i, k % ncg)),
                  pl.BlockSpec((None, kd, tc), lambda i, k: (l, 0, k)),
                  pl.BlockSpec((tm, kd), lambda i, k: (i, 0))] + sspecs,
        out_specs=[pl.BlockSpec((tm, kd), lambda i, k: (i, 0))] + sspecs,
        out_shape=[jax.ShapeDtypeStruct((m, kd), F32)] + sshapes,
        scratch_shapes=[pltpu.VMEM((tm, kd), F32)] + ssems,
        compiler_params=_cp("arbitrary", "arbitrary"))(dh3, w3, dz, *srcs)
    return outs if ns else outs[0]


def _mm_tn(a, b3, tka, tnb, name):
    t, ka = a.shape
    g, _, cg = b3.shape
    ncg = cg // tnb
    tt = min(TT_MM, t)
    nt = t // tt

    def body(a_ref, b_ref, o_ref, acc):
        s = pl.program_id(2)

        @pl.when(s == 0)
        def _():
            acc[...] = jnp.zeros_like(acc)

        acc[...] += lax.dot_general(a_ref[...], b_ref[...], (((0,), (0,)), ((), ())),
                                    preferred_element_type=F32)

        @pl.when(s == nt - 1)
        def _():
            o_ref[...] = acc[...].astype(o_ref.dtype)

    return pl.pallas_call(
        body, name=name, grid=(ka // tka, g * ncg, nt),
        in_specs=[pl.BlockSpec((tt, tka), lambda i, j, s: (s, i)),
                  pl.BlockSpec((None, tt, tnb), lambda i, j, s: (j // ncg, s, j % ncg))],
        out_specs=pl.BlockSpec((tka, tnb), lambda i, j, s: (i, j)),
        out_shape=jax.ShapeDtypeStruct((ka, g * cg), BF16),
        scratch_shapes=[pltpu.VMEM((tka, tnb), F32)],
        compiler_params=_cp("parallel", "parallel", "arbitrary"))(a, b3)


def _ln_bwd(dxn, z, g, name):
    m, d = z.shape
    tm = min(TM_LN, m)

    def body(dx_ref, z_ref, g_ref, dz_ref, dzb_ref, dg_ref, db_ref):
        @pl.when(pl.program_id(0) == 0)
        def _():
            dg_ref[...] = jnp.zeros_like(dg_ref)
            db_ref[...] = jnp.zeros_like(db_ref)

        zz = z_ref[...]
        dx = dx_ref[...]
        mu = jnp.mean(zz, axis=-1, keepdims=True)
        zc = zz - mu
        var = jnp.mean(zc * zc, axis=-1, keepdims=True)
        rstd = lax.rsqrt(var + LN_EPS)
        xh = zc * rstd
        dg_ref[...] += _colsum(dx * xh)
        db_ref[...] += _colsum(dx)
        dxh = dx * g_ref[...]
        m1 = jnp.mean(dxh, axis=-1, keepdims=True)
        m2 = jnp.mean(dxh * xh, axis=-1, keepdims=True)
        dz = rstd * (dxh - m1 - xh * m2)
        dz_ref[...] = dz
        dzb_ref[...] = dz.astype(BF16)

    row = pl.BlockSpec((tm, d), lambda i: (i, 0))
    vec = pl.BlockSpec((1, d), lambda i: (0, 0))
    return pl.pallas_call(
        body, name=name, grid=(m // tm,), in_specs=[row, row, vec],
        out_specs=[row, row, vec, vec],
        out_shape=[jax.ShapeDtypeStruct((m, d), F32), jax.ShapeDtypeStruct((m, d), BF16),
                   jax.ShapeDtypeStruct((1, d), F32), jax.ShapeDtypeStruct((1, d), F32)],
        compiler_params=_cp("arbitrary"))(dxn, z, g)


def _loss_bwd(y, target):
    m, d = y.shape
    tm = min(TM_LN, m)

    def body(y_ref, t_ref, dy_ref, ls_ref):
        @pl.when(pl.program_id(0) == 0)
        def _():
            ls_ref[...] = jnp.zeros_like(ls_ref)

        e = y_ref[...] - t_ref[...]
        dy_ref[...] = e * (1.0 / d)
        ls_ref[...] += _colsum(e * e) * (0.5 / d)

    row = pl.BlockSpec((tm, d), lambda i: (i, 0))
    return pl.pallas_call(
        body, name="loss_bwd", grid=(m // tm,), in_specs=[row, row],
        out_specs=[row, pl.BlockSpec((1, d), lambda i: (0, 0))],
        out_shape=[jax.ShapeDtypeStruct((m, d), F32), jax.ShapeDtypeStruct((1, d), F32)],
        compiler_params=_cp("arbitrary"))(y, target)


def _sc_fwd(h, cw, cb, seq, name):
    t, d3 = h.shape
    d = d3 // 3
    tm = min(TM_SC, seq)

    def body(hb_ref, hc_ref, hv_ref, cw_ref, cb_ref, q_ref, carry):
        i = pl.program_id(0)

        @pl.when(lax.rem(i * tm, seq) == 0)
        def _():
            carry[...] = jnp.zeros_like(carry)

        p = hc_ref[...].astype(F32) * hv_ref[...].astype(F32)
        ext = jnp.concatenate([carry[...], p], axis=0)
        u = cb_ref[...] + _row(cw_ref, 0) * _down(ext, 2, SUB) + _row(cw_ref, 1) * _down(ext, 1, SUB) \
            + _row(cw_ref, 2) * p
        q_ref[...] = (hb_ref[...].astype(F32) * u).astype(BF16)
        carry[...] = p[tm - SUB:, :]

    blk = lambda c: pl.BlockSpec((tm, d), lambda i: (i, c))
    return pl.pallas_call(
        body, name=name, grid=(t // tm,),
        in_specs=[blk(0), blk(1), blk(2), pl.BlockSpec((3, d), lambda i: (0, 0)),
                  pl.BlockSpec((1, d), lambda i: (0, 0))],
        out_specs=pl.BlockSpec((tm, d), lambda i: (i, 0)),
        out_shape=jax.ShapeDtypeStruct((t, d), BF16),
        scratch_shapes=[pltpu.VMEM((SUB, d), F32)],
        compiler_params=_cp("arbitrary"))(h, h, h, cw, cb)


def _sc_bwd(h, dq, cw, cb, seq, name):
    t, d3 = h.shape
    d = d3 // 3
    tm = min(TM_SC, seq)
    nt = t // tm
    hpt = tm // SUB16

    def body(hb_ref, hc_ref, hv_ref, hch_ref, hvh_ref, dq_ref, cw_ref, cb_ref,
             dh_ref, dcw_ref, dcb_ref, carry):
        i = pl.program_id(0)
        ri = nt - 1 - i

        @pl.when(i == 0)
        def _():
            dcw_ref[...] = jnp.zeros_like(dcw_ref)
            dcb_ref[...] = jnp.zeros_like(dcb_ref)

        @pl.when(lax.rem((ri + 1) * tm, seq) == 0)
        def _():
            carry[...] = jnp.zeros_like(carry)

        keep = jnp.where(lax.rem(ri * tm, seq) == 0, 0.0, 1.0)
        gb = hb_ref[...].astype(F32)
        gc = hc_ref[...].astype(F32)
        v = hv_ref[...].astype(F32)
        p = gc * v
        p_head = hch_ref[...].astype(F32) * hvh_ref[...].astype(F32) * keep
        ext = jnp.concatenate([p_head, p], axis=0)
        pm2 = _down(ext, 2, SUB16)
        pm1 = _down(ext, 1, SUB16)
        u = cb_ref[...] + _row(cw_ref, 0) * pm2 + _row(cw_ref, 1) * pm1 + _row(cw_ref, 2) * p
        dqf = dq_ref[...].astype(F32)
        du = dqf * gb
        dcb_ref[...] += _colsum(du)
        dcw_ref[0:1, :] += _colsum(du * pm2)
        dcw_ref[1:2, :] += _colsum(du * pm1)
        dcw_ref[2:3, :] += _colsum(du * p)
        ext2 = jnp.concatenate([du, carry[...]], axis=0)
        dp = _row(cw_ref, 2) * du + _row(cw_ref, 1) * _up(ext2, 1, tm) + _row(cw_ref, 0) * _up(ext2, 2, tm)
        carry[...] = du[0:SUB, :]
        dh_ref[0] = (dqf * u).astype(BF16)
        dh_ref[1] = (dp * v).astype(BF16)
        dh_ref[2] = (dp * gc).astype(BF16)

    blk = lambda c: pl.BlockSpec((tm, d), lambda i: (nt - 1 - i, c))
    head = lambda c: pl.BlockSpec((SUB16, d), lambda i: (jnp.maximum((nt - 1 - i) * hpt - 1, 0), c))
    vec = lambda r: pl.BlockSpec((r, d), lambda i: (0, 0))
    return pl.pallas_call(
        body, name=name, grid=(nt,),
        in_specs=[blk(0), blk(1), blk(2), head(1), head(2),
                  pl.BlockSpec((tm, d), lambda i: (nt - 1 - i, 0)), vec(3), vec(1)],
        out_specs=[pl.BlockSpec((3, tm, d), lambda i: (0, nt - 1 - i, 0)), vec(3), vec(1)],
        out_shape=[jax.ShapeDtypeStruct((3, t, d), BF16), jax.ShapeDtypeStruct((3, d), F32),
                   jax.ShapeDtypeStruct((1, d), F32)],
        scratch_shapes=[pltpu.VMEM((SUB, d), F32)],
        compiler_params=_cp("arbitrary"))(h, h, h, h, h, dq, cw, cb)


def _fold(v):
    return jnp.sum(v.reshape(v.shape[0] // SUB, SUB, v.shape[1]), axis=0)


def _ffn_up(xb, w3, l, cw, cb, seq, name, rider=()):
    t, d = xb.shape
    f = w3.shape[2] // 2
    tc = f // FFN_COL_BLOCKS
    tm = min(TM_MM, seq)
    nc = FFN_COL_BLOCKS
    bufs, rkinds, rspecs, rshapes, rsems = _rider_args(rider)
    nr = len(bufs)

    def body(*refs):
        x_ref, wg_ref, wv_ref, cwg_ref, cwv_ref, cbg_ref, cbv_ref = refs[:7]
        h_ref, pre_ref, act_ref = refs[7 + nr:10 + nr]
        rout = refs[10 + nr:10 + 2 * nr]
        eg, ev = refs[10 + 2 * nr:12 + 2 * nr]
        sems = refs[12 + 2 * nr:]
        i = pl.program_id(1)
        if nr:
            @pl.when(jnp.logical_and(pl.program_id(0) == 0, i == 0))
            def _():
                _rider_start(rout, rkinds, *sems)

        @pl.when(lax.rem(i * tm, seq) == 0)
        def _():
            eg[0:SUB, :] = jnp.zeros((SUB, tc), F32)
            ev[0:SUB, :] = jnp.zeros((SUB, tc), F32)

        xx = x_ref[...]

        def matmul(lo, hi):
            eg[SUB:, lo:hi] = jnp.dot(xx, wg_ref[:, lo:hi], preferred_element_type=F32)
            ev[SUB:, lo:hi] = jnp.dot(xx, wv_ref[:, lo:hi], preferred_element_type=F32)

        def epilogue(lo, hi):
            for c0 in range(lo, hi, LANE):
                cols = slice(c0, c0 + LANE)
                taps = [[ref[k:k + 1, cols] for k in range(3)] + [bref[:, cols]]
                        for ref, bref in ((cwg_ref, cbg_ref), (cwv_ref, cbv_ref))]
                for r0 in range(0, tm, RC):
                    rows = slice(r0, r0 + RC)
                    pres = []
                    for half, e_ref in enumerate((eg, ev)):
                        w0, w1, w2, bias = taps[half]
                        e = e_ref[r0:r0 + RC + SUB, cols]
                        cur = e[SUB:]
                        pre = bias + w0 * _down(e, 2, SUB) + w1 * _down(e, 1, SUB) + w2 * cur
                        h_ref[half, rows, cols] = cur.astype(BF16)
                        pre_ref[half, rows, cols] = pre.astype(BF16)
                        pres.append(pre)
                    act_ref[rows, cols] = (pres[0] * _sigmoid(pres[0]) * pres[1]).astype(BF16)

        blocks = [(lo, min(lo + MXU_COLS, tc)) for lo in range(0, tc, MXU_COLS)]
        matmul(*blocks[0])
        for b, blk in enumerate(blocks):
            if b + 1 < len(blocks):
                matmul(*blocks[b + 1])
            epilogue(*blk)
        eg[0:SUB, :] = eg[tm:tm + SUB, :]
        ev[0:SUB, :] = ev[tm:tm + SUB, :]
        if nr:
            @pl.when(jnp.logical_and(pl.program_id(0) == nc - 1, i == t // tm - 1))
            def _():
                _rider_wait(rout, rkinds, *sems)

    wspec = lambda off: pl.BlockSpec((None, d, tc), lambda j, i: (l, 0, j + off))
    vec = lambda r, off: pl.BlockSpec((r, tc), lambda j, i: (0, j + off))
    pair = pl.BlockSpec((2, tm, tc), lambda j, i: (0, i, j))
    return pl.pallas_call(
        body, name=name, grid=(nc, t // tm),
        in_specs=[pl.BlockSpec((tm, d), lambda j, i: (i, 0)), wspec(0), wspec(nc),
                  vec(3, 0), vec(3, nc), vec(1, 0), vec(1, nc)] + rspecs,
        out_specs=[pair, pair, pl.BlockSpec((tm, tc), lambda j, i: (i, j))] + rspecs,
        out_shape=[jax.ShapeDtypeStruct((2, t, f), BF16), jax.ShapeDtypeStruct((2, t, f), BF16),
                   jax.ShapeDtypeStruct((t, f), BF16)] + rshapes,
        input_output_aliases={7 + u: 3 + u for u in range(nr)},
        scratch_shapes=[pltpu.VMEM((tm + SUB, tc), F32), pltpu.VMEM((tm + SUB, tc), F32)] + rsems,
        compiler_params=_cp("arbitrary", "arbitrary"))(xb, w3, w3, cw, cw, cb, cb, *bufs)


def _ffn_down_bwd(dzb, wd3, l, h3, pre3, cw, seq, name, scatter=()):
    t, d = dzb.shape
    f = wd3.shape[1]
    tc = f // FFN_COL_BLOCKS
    tm = min(TM_MM, seq)
    nt = t // tm
    nc = FFN_COL_BLOCKS
    srcs, skinds, sspecs, sshapes, ssems = _scatter_args(scatter)
    ns = len(srcs)

    def body(*refs):
        dz_ref, wd_ref, h_ref, pre_ref, cwg_ref, cwv_ref = refs[:6]
        sin = refs[6:6 + ns]
        dh_ref, dcwg_ref, dcwv_ref, dcbg_ref, dcbv_ref = refs[6 + ns:11 + ns]
        lands = refs[11 + ns:11 + 2 * ns]
        da_s, carry = refs[11 + 2 * ns:13 + 2 * ns]
        sems = refs[13 + 2 * ns:]
        i = pl.program_id(1)
        ri = nt - 1 - i
        if ns:
            @pl.when(jnp.logical_and(pl.program_id(0) == 0, i == 0))
            def _():
                for cp in _scatter_copies(sin, lands, skinds, *sems):
                    cp.start()

        @pl.when(i == 0)
        def _():
            for r in (dcwg_ref, dcwv_ref, dcbg_ref, dcbv_ref):
                r[...] = jnp.zeros_like(r)

        @pl.when(lax.rem((ri + 1) * tm, seq) == 0)
        def _():
            carry[...] = jnp.zeros_like(carry)

        dz = dz_ref[...]

        def matmul(lo, hi):
            da_s[:, lo:hi] = lax.dot_general(dz, wd_ref[lo:hi, :], (((1,), (1,)), ((), ())),
                                             preferred_element_type=F32)

        def epilogue(lo, hi):
            for c0 in range(lo, hi, LANE):
                cols = slice(c0, c0 + LANE)
                taps = [[ref[k:k + 1, cols] for k in range(3)] for ref in (cwg_ref, cwv_ref)]
                acc = [jnp.zeros((SUB, LANE), F32)] * 8
                for r0 in range(tm - RC, -1, -RC):
                    rows = slice(r0, r0 + RC)
                    da = da_s[rows, cols]
                    gp = pre_ref[0, rows, cols].astype(F32)
                    vp = pre_ref[1, rows, cols].astype(F32)
                    sg = _sigmoid(gp)
                    dpres = (da * vp * (sg * (1.0 + gp * (1.0 - sg))), da * (gp * sg))
                    for half in range(2):
                        w0, w1, w2 = taps[half]
                        dpre = dpres[half]
                        ext = jnp.concatenate([dpre, carry[half, :, cols]], axis=0)
                        u1 = _up(ext, 1, RC)
                        u2 = _up(ext, 2, RC)
                        carry[half, :, cols] = dpre[0:SUB]
                        dh_ref[half, rows, cols] = (w2 * dpre + w1 * u1 + w0 * u2).astype(BF16)
                        hh = h_ref[half, rows, cols].astype(F32)
                        for k, term in enumerate((hh * u2, hh * u1, hh * dpre, dpre)):
                            acc[4 * half + k] = acc[4 * half + k] + _fold(term)
                for half, (dcw_ref, dcb_ref) in enumerate(((dcwg_ref, dcbg_ref), (dcwv_ref, dcbv_ref))):
                    for k in range(3):
                        dcw_ref[k:k + 1, cols] += _colsum(acc[4 * half + k])
                    dcb_ref[:, cols] += _colsum(acc[4 * half + 3])

        blocks = [(lo, min(lo + MXU_COLS, tc)) for lo in range(0, tc, MXU_COLS)]
        matmul(*blocks[0])
        for b, blk in enumerate(blocks):
            if b + 1 < len(blocks):
                matmul(*blocks[b + 1])
            epilogue(*blk)
        if ns:
            @pl.when(jnp.logical_and(pl.program_id(0) == nc - 1, i == nt - 1))
            def _():
                for cp in _scatter_copies(sin, lands, skinds, *sems):
                    cp.wait()

    pair = pl.BlockSpec((2, tm, tc), lambda j, i: (0, nt - 1 - i, j))
    vec = lambda off: pl.BlockSpec((3, tc), lambda j, i: (0, j + off))
    acc_spec = lambda r: pl.BlockSpec((r, tc), lambda j, i: (0, j))
    return pl.pallas_call(
        body, name=name, grid=(nc, nt),
        in_specs=[pl.BlockSpec((tm, d), lambda j, i: (nt - 1 - i, 0)),
                  pl.BlockSpec((None, tc, d), lambda j, i: (l, j, 0)), pair, pair, vec(0), vec(nc)] + sspecs,
        out_specs=[pair, acc_spec(3), acc_spec(3), acc_spec(1), acc_spec(1)] + sspecs,
        out_shape=[jax.ShapeDtypeStruct((2, t, f), BF16), jax.ShapeDtypeStruct((3, f), F32),
                   jax.ShapeDtypeStruct((3, f), F32), jax.ShapeDtypeStruct((1, f), F32),
                   jax.ShapeDtypeStruct((1, f), F32)] + sshapes,
        scratch_shapes=[pltpu.VMEM((tm, tc), F32), pltpu.VMEM((2, SUB, tc), F32)] + ssems,
        compiler_params=_cp("arbitrary", "arbitrary"))(dzb, wd3, h3, pre3, cw, cw, *srcs)


def _lru_gates(xr, wg_ref, bg_ref):
    rs, gs = [], []
    for hd in range(LRU_HEADS):
        xh = xr[:, hd * LRU_BLOCK:(hd + 1) * LRU_BLOCK].astype(BF16)
        gt = jnp.dot(xh, wg_ref[hd], preferred_element_type=F32) + _row(bg_ref, hd)
        rs.append(gt[:, :LRU_BLOCK])
        gs.append(gt[:, LRU_BLOCK:])
    return jnp.concatenate(rs, axis=1), jnp.concatenate(gs, axis=1)


def _lru_coeffs(xr, wg_ref, bg_ref, lam_ref):
    gr, gi = _lru_gates(xr, wg_ref, bg_ref)
    r = _sigmoid(gr)
    ig = _sigmoid(gi)
    sp = _softplus(-lam_ref[...])
    log_a = -LRU_C * r * sp
    a = jnp.exp(log_a)
    mult = jnp.sqrt(_one_minus_exp(2.0 * log_a))
    return r, ig, sp, a, mult


def _lru_fwd(h, cw, cb, wg, bg, lam, seq, name):
    t, r2 = h.shape
    rw = r2 // 2
    ts = min(TS_LRU, seq)
    n8 = ts // SUB

    def body(hg_ref, hr_ref, cw_ref, cb_ref, wg_ref, bg_ref, lam_ref, hs_ref, y_ref,
             a_s, b_s, cconv, cstate):
        i = pl.program_id(0)

        @pl.when(lax.rem(i * ts, seq) == 0)
        def _():
            cconv[...] = jnp.zeros_like(cconv)
            cstate[...] = jnp.zeros_like(cstate)

        rin = hr_ref[...].astype(F32)
        ext = jnp.concatenate([cconv[...], rin], axis=0)
        xr = cb_ref[...]
        for k in range(4):
            xr = xr + _row(cw_ref, k) * _down(ext, 3 - k, SUB)
        cconv[...] = rin[ts - SUB:, :]
        _, ig, _, a, mult = _lru_coeffs(xr, wg_ref, bg_ref, lam_ref)
        a_s[...] = a
        b_s[...] = mult * (ig * xr)
        row = lax.broadcasted_iota(jnp.int32, (SUB, rw), 0)

        def step(j, carry):
            off = pl.multiple_of(j * SUB, SUB)
            a8 = a_s[pl.ds(off, SUB), :]
            b8 = b_s[pl.ds(off, SUB), :]
            for d in (1, 2, 4):
                m = row >= d
                b8 = jnp.where(m, a8 * pltpu.roll(b8, d, 0) + b8, b8)
                a8 = jnp.where(m, a8 * pltpu.roll(a8, d, 0), a8)
            h8 = a8 * carry + b8
            hs_ref[pl.ds(off, SUB), :] = h8
            return _colsum(jnp.where(row == SUB - 1, h8, 0.0))

        cstate[...] = lax.fori_loop(0, n8, step, cstate[...])
        gel, _ = _gelu_and_grad(hg_ref[...].astype(F32))
        y_ref[...] = (hs_ref[...] * gel).astype(BF16)

    full = lambda shp: pl.BlockSpec(shp, lambda i: (0,) * len(shp))
    return pl.pallas_call(
        body, name=name, grid=(t // ts,),
        in_specs=[pl.BlockSpec((ts, rw), lambda i: (i, 0)), pl.BlockSpec((ts, rw), lambda i: (i, 1)),
                  full((4, rw)), full((1, rw)), full(wg.shape), full(bg.shape), full((1, rw))],
        out_specs=[pl.BlockSpec((ts, rw), lambda i: (i, 0)), pl.BlockSpec((ts, rw), lambda i: (i, 0))],
        out_shape=[jax.ShapeDtypeStruct((t, rw), F32), jax.ShapeDtypeStruct((t, rw), BF16)],
        scratch_shapes=[pltpu.VMEM((ts, rw), F32), pltpu.VMEM((ts, rw), F32),
                        pltpu.VMEM((SUB, rw), F32), pltpu.VMEM((1, rw), F32)],
        compiler_params=_cp("arbitrary"))(h, h, cw, cb, wg, bg, lam)


def _lru_bwd(h, hs, dy, cw, cb, wg, bg, lam, seq, name):
    t, r2 = h.shape
    rw = r2 // 2
    ts = min(TS_LRU, seq)
    nt = t // ts
    n8 = ts // SUB
    hp16 = ts // SUB16
    hp8 = ts // SUB

    def body(hg_ref, hr_ref, hrh_ref, hs_ref, hsh_ref, dy_ref, cw_ref, cb_ref, wg_ref, bg_ref, lam_ref,
             dh_ref, dbin_ref, dcw_ref, dcb_ref, dwg_ref, dbg_ref, dlam_ref,
             a_s, g_s, l_s, c_lam, c_a, c_dxr):
        i = pl.program_id(0)
        ri = nt - 1 - i

        @pl.when(i == 0)
        def _():
            for r in (dbin_ref, dcw_ref, dcb_ref, dwg_ref, dbg_ref, dlam_ref):
                r[...] = jnp.zeros_like(r)

        @pl.when(lax.rem((ri + 1) * ts, seq) == 0)
        def _():
            c_lam[...] = jnp.zeros_like(c_lam)
            c_a[...] = jnp.zeros_like(c_a)
            c_dxr[...] = jnp.zeros_like(c_dxr)

        keep = jnp.where(lax.rem(ri * ts, seq) == 0, 0.0, 1.0)
        rin = hr_ref[...].astype(F32)
        ext = jnp.concatenate([hrh_ref[...].astype(F32) * keep, rin], axis=0)
        shifted = [_down(ext, 3 - k, SUB16) for k in range(4)]
        xr = cb_ref[...]
        for k in range(4):
            xr = xr + _row(cw_ref, k) * shifted[k]
        r, ig, sp, a, mult = _lru_coeffs(xr, wg_ref, bg_ref, lam_ref)
        gel, dgel = _gelu_and_grad(hg_ref[...].astype(F32))
        dyf = dy_ref[...].astype(F32)
        hsv = hs_ref[...]
        dg = dyf * hsv * dgel

        a_s[...] = _up(jnp.concatenate([a, c_a[...]], axis=0), 1, ts)
        g_s[...] = dyf * gel
        c_a[...] = a[0:SUB, :]
        row = lax.broadcasted_iota(jnp.int32, (SUB, rw), 0)

        def step(j, carry):
            off = pl.multiple_of((n8 - 1 - j) * SUB, SUB)
            a8 = a_s[pl.ds(off, SUB), :]
            b8 = g_s[pl.ds(off, SUB), :]
            for d in (1, 2, 4):
                m = row < SUB - d
                b8 = jnp.where(m, a8 * pltpu.roll(b8, SUB - d, 0) + b8, b8)
                a8 = jnp.where(m, a8 * pltpu.roll(a8, SUB - d, 0), a8)
            l8 = a8 * carry + b8
            l_s[pl.ds(off, SUB), :] = l8
            return _colsum(jnp.where(row == 0, l8, 0.0))

        c_lam[...] = lax.fori_loop(0, n8, step, c_lam[...])
        lamv = l_s[...]
        hs_prev = _down(jnp.concatenate([hsh_ref[...] * keep, hsv], axis=0), 1, SUB)
        da = lamv * hs_prev
        t1 = lamv * xr
        dmult = t1 * ig
        dig = t1 * mult
        dxr = lamv * mult * ig
        dla = da * a - dmult * (a * a) / mult
        dr = dla * (-LRU_C * sp)
        dlam_ref[...] += _colsum(dla * (-LRU_C) * r) * (-1.0 / (1.0 + jnp.exp(lam_ref[...])))
        dgr = dr * r * (1.0 - r)
        dgi = dig * ig * (1.0 - ig)
        parts = []
        for hd in range(LRU_HEADS):
            sl = slice(hd * LRU_BLOCK, (hd + 1) * LRU_BLOCK)
            dgt = jnp.concatenate([dgr[:, sl], dgi[:, sl]], axis=1)
            dbg_ref[hd:hd + 1, :] += _colsum(dgt)
            dgt16 = dgt.astype(BF16)
            parts.append(lax.dot_general(dgt16, wg_ref[hd], (((1,), (1,)), ((), ())),
                                         preferred_element_type=F32))
            dwg_ref[hd] += lax.dot_general(xr[:, sl].astype(BF16), dgt16, (((0,), (0,)), ((), ())),
                                           preferred_element_type=F32)
        dxr = dxr + jnp.concatenate(parts, axis=1)

        dcb_ref[...] += _colsum(dxr)
        for k in range(4):
            dcw_ref[k:k + 1, :] += _colsum(dxr * shifted[k])
        ext2 = jnp.concatenate([dxr, c_dxr[...]], axis=0)
        drb = _row(cw_ref, 3) * dxr
        for k in range(3):
            drb = drb + _row(cw_ref, k) * _up(ext2, 3 - k, ts)
        c_dxr[...] = dxr[0:SUB, :]
        dh_ref[0] = dg.astype(BF16)
        dh_ref[1] = drb.astype(BF16)
        dbin_ref[:, 0:rw] += _colsum(dg)
        dbin_ref[:, rw:] += _colsum(drb)

    rev = lambda c: pl.BlockSpec((ts, rw), lambda i: (nt - 1 - i, c))
    full = lambda shp: pl.BlockSpec(shp, lambda i: (0,) * len(shp))
    nh = LRU_HEADS
    return pl.pallas_call(
        body, name=name, grid=(nt,),
        in_specs=[rev(0), rev(1),
                  pl.BlockSpec((SUB16, rw), lambda i: (jnp.maximum((nt - 1 - i) * hp16 - 1, 0), 1)),
                  rev(0),
                  pl.BlockSpec((SUB, rw), lambda i: (jnp.maximum((nt - 1 - i) * hp8 - 1, 0), 0)),
                  rev(0), full((4, rw)), full((1, rw)), full(wg.shape), full(bg.shape), full((1, rw))],
        out_specs=[pl.BlockSpec((2, ts, rw), lambda i: (0, nt - 1 - i, 0)), full((1, r2)), full((4, rw)),
                   full((1, rw)), full((nh, LRU_BLOCK, 2 * LRU_BLOCK)), full((nh, 2 * LRU_BLOCK)), full((1, rw))],
        out_shape=[jax.ShapeDtypeStruct((2, t, rw), BF16), jax.ShapeDtypeStruct((1, r2), F32),
                   jax.ShapeDtypeStruct((4, rw), F32), jax.ShapeDtypeStruct((1, rw), F32),
                   jax.ShapeDtypeStruct((nh, LRU_BLOCK, 2 * LRU_BLOCK), F32),
                   jax.ShapeDtypeStruct((nh, 2 * LRU_BLOCK), F32), jax.ShapeDtypeStruct((1, rw), F32)],
        scratch_shapes=[pltpu.VMEM((ts, rw), F32), pltpu.VMEM((ts, rw), F32), pltpu.VMEM((ts, rw), F32),
                        pltpu.VMEM((1, rw), F32), pltpu.VMEM((SUB, rw), F32), pltpu.VMEM((SUB, rw), F32)],
        compiler_params=_cp("arbitrary"))(h, h, h, hs, hs, dy, cw, cb, wg, bg, lam)


def _row_tile(rows, cols, mult, elems=ELEMS_PER_BLOCK):
    cap = max(mult, elems // cols)
    best = None
    for cand in range(mult, min(rows, cap) + 1, mult):
        if rows % cand == 0:
            best = cand
    return best if best is not None else rows


def _core_index():
    return lax.axis_index("c").astype(jnp.int32).reshape(1)


def _chip_index():
    return (2 * lax.axis_index("x") + lax.axis_index("y")).astype(jnp.int32).reshape(1)


def _add_pair(p4, r3, name):
    s, _, rows, cols = p4.shape
    tr = _row_tile(rows, cols, SUB16)

    def body(c_ref, a_ref, b_ref, o_ref):
        o_ref[...] = (a_ref[...].astype(F32) + b_ref[...].astype(F32)).astype(o_ref.dtype)

    blk = pl.BlockSpec((None, tr, cols), lambda k, i, c_ref: (k, i, 0))
    return pl.pallas_call(
        body, name=name,
        grid_spec=pltpu.PrefetchScalarGridSpec(
            num_scalar_prefetch=1, grid=(s, rows // tr),
            in_specs=[pl.BlockSpec((None, None, tr, cols), lambda k, i, c_ref: (k, c_ref[0], i, 0)), blk],
            out_specs=blk),
        out_shape=jax.ShapeDtypeStruct((s, rows, cols), p4.dtype),
        compiler_params=_cp("parallel", "parallel"))(_core_index(), p4, r3)


def _add_chips(r, name):
    shape = r.shape[1:]
    r3 = r.reshape(N_CHIPS, -1, shape[-1])
    _, rows, cols = r3.shape
    tr = _row_tile(rows, cols, SUB16)

    def body(r_ref, o_ref):
        s = r_ref[0].astype(F32) + r_ref[1].astype(F32)
        s = s + r_ref[2].astype(F32)
        o_ref[...] = s + r_ref[3].astype(F32)

    out = pl.pallas_call(body, name=name, grid=(rows // tr,),
                         in_specs=[pl.BlockSpec((N_CHIPS, tr, cols), lambda i: (0, i, 0))],
                         out_specs=pl.BlockSpec((tr, cols), lambda i: (i, 0)),
                         out_shape=jax.ShapeDtypeStruct((rows, cols), F32),
                         compiler_params=_cp("parallel"))(r3)
    return out.reshape(shape)


def _adamw(w3, g_mine, g_sib, m3, v3, name):
    nl, r, cols = w3.shape
    rows = r // 2
    flat = [arr.reshape(nl, 2, rows, cols) for arr in (w3, m3, v3)]
    tr = _row_tile(rows, cols, SUB)

    def body(c_ref, w_ref, gm_ref, gs_ref, m_ref, v_ref, g_ref, d_ref, mo_ref, vo_ref):
        gg = jnp.where(pl.program_id(1) == c_ref[0], gm_ref[...], gs_ref[...])
        m2 = ADAM_B1 * m_ref[...] + (1.0 - ADAM_B1) * gg
        v2 = ADAM_B2 * v_ref[...] + (1.0 - ADAM_B2) * (gg * gg)
        m_hat = m2 / (1.0 - ADAM_B1 ** ADAM_STEP)
        v_hat = v2 / (1.0 - ADAM_B2 ** ADAM_STEP)
        g_ref[...] = gg
        d_ref[...] = -ADAM_LR * (m_hat / (jnp.sqrt(v_hat) + ADAM_EPS) + ADAM_WD * w_ref[...])
        mo_ref[...] = m2
        vo_ref[...] = v2

    blk = pl.BlockSpec((None, None, tr, cols), lambda l, hh, i, c_ref: (l, hh, i, 0))
    gblk = pl.BlockSpec((None, tr, cols), lambda l, hh, i, c_ref: (l, i, 0))
    outs = pl.pallas_call(
        body, name=name,
        grid_spec=pltpu.PrefetchScalarGridSpec(
            num_scalar_prefetch=1, grid=(nl, 2, rows // tr),
            in_specs=[blk, gblk, gblk, blk, blk], out_specs=[blk] * 4),
        out_shape=[jax.ShapeDtypeStruct((nl, 2, rows, cols), F32)] * 4,
        compiler_params=_cp("parallel", "parallel", "parallel"))(_core_index(), flat[0], g_mine, g_sib, flat[1],
                                                                 flat[2])
    return tuple(o.reshape(nl, r, cols) for o in outs)


def _place(src3, layer, kind, dtype, name):
    _, r, c = src3.shape
    tr = _row_tile(r, c, SUB16)
    in_spec = pl.BlockSpec((None, tr, c), lambda i, my_ref: (layer, i, 0))
    if kind == "col":
        out_spec = pl.BlockSpec((tr, c), lambda i, my_ref: (i, my_ref[0]))
        out_shape = (r, N_CHIPS * c)
    else:
        out_spec = pl.BlockSpec((None, tr, c), lambda i, my_ref: (my_ref[0], i, 0))
        out_shape = (N_CHIPS, r, c)

    def body(my_ref, i_ref, o_ref):
        o_ref[...] = i_ref[...].astype(o_ref.dtype)

    return pl.pallas_call(
        body, name=name,
        grid_spec=pltpu.PrefetchScalarGridSpec(num_scalar_prefetch=1, grid=(r // tr,), in_specs=[in_spec],
                                               out_specs=out_spec),
        out_shape=jax.ShapeDtypeStruct(out_shape, dtype),
        compiler_params=_cp("parallel"))(_chip_index(), src3)


def _half(ref, c, h):
    return ref.at[pl.ds(c * h, h)]


def _position():
    x = lax.axis_index("x")
    y = lax.axis_index("y")
    c = lax.axis_index("c")
    return x, y, c


def _peer_chip(x, y, j):
    tx = 1 - x if j & 2 else x
    ty = 1 - y if j & 1 else y
    return tx, ty


def _remote(src, dst, ssem, rsem, dev):
    return pltpu.make_async_remote_copy(src_ref=src, dst_ref=dst, send_sem=ssem, recv_sem=rsem,
                                        device_id=dev, device_id_type=pl.DeviceIdType.MESH)


_ANY = pl.BlockSpec(memory_space=pl.ANY)


def _unit_view(kind, ref, k):
    if kind == "col":
        n = ref.shape[1] // N_CHIPS
        return ref.at[:, pl.ds(pl.multiple_of(k * n, LANE), n)]
    return ref.at[k]


def _all_gather(placed, kinds):
    nt = len(placed)

    def body(*refs):
        outs = refs[nt:2 * nt]
        ssem, rsem = refs[2 * nt:]
        x, y, c = _position()
        my = 2 * x + y
        sib = (x, y, 1 - c)

        def part(t, k, core):
            view = _unit_view(kinds[t], outs[t], k)
            h = view.shape[0] // 2
            return _half(view, core, h)

        sends, fwds = [], []
        for t in range(nt):
            own = part(t, my, c)
            for j in (1, 2, 3):
                tx, ty = _peer_chip(x, y, j)
                cp = _remote(own, own, ssem.at[6 * t + j - 1], rsem.at[6 * t + j - 1], (tx, ty, c))
                cp.start()
                sends.append(cp)
        for t in range(nt):
            for j in (1, 2, 3):
                tx, ty = _peer_chip(x, y, j)
                got = part(t, 2 * tx + ty, c)
                _remote(got, got, ssem.at[6 * t + j - 1], rsem.at[6 * t + j - 1], sib).wait_recv()
                cp = _remote(got, got, ssem.at[6 * t + 2 + j], rsem.at[6 * t + 2 + j], sib)
                cp.start()
                fwds.append(cp)
        for t in range(nt):
            for j in (1, 2, 3):
                tx, ty = _peer_chip(x, y, j)
                other = part(t, 2 * tx + ty, 1 - c)
                _remote(other, other, ssem.at[6 * t + 2 + j], rsem.at[6 * t + 2 + j], sib).wait_recv()
        for cp in sends + fwds:
            cp.wait_send()

    return pl.pallas_call(
        body, name="all_gather", in_specs=[_ANY] * nt, out_specs=[_ANY] * nt,
        out_shape=[jax.ShapeDtypeStruct(p.shape, p.dtype) for p in placed],
        input_output_aliases={t: t for t in range(nt)},
        scratch_shapes=[pltpu.SemaphoreType.DMA((6 * nt,)), pltpu.SemaphoreType.DMA((6 * nt,))],
    )(*placed)


def _rider_start(refs, kinds, ssem, rsem):
    x, y, c = _position()
    my = 2 * x + y
    for u, (ref, kind) in enumerate(zip(refs, kinds)):
        own = _unit_view(kind, ref, my)
        for j in (1, 2, 3):
            tx, ty = _peer_chip(x, y, j)
            _remote(own, own, ssem.at[3 * u + j - 1], rsem.at[3 * u + j - 1], (tx, ty, c)).start()


def _rider_wait(refs, kinds, ssem, rsem):
    x, y, c = _position()
    for u, (ref, kind) in enumerate(zip(refs, kinds)):
        for j in (1, 2, 3):
            tx, ty = _peer_chip(x, y, j)
            got = _unit_view(kind, ref, 2 * tx + ty)
            _remote(got, got, ssem.at[3 * u + j - 1], rsem.at[3 * u + j - 1], (tx, ty, c)).wait()


def _rider_args(rider):
    bufs = [b for b, _ in rider]
    kinds = [k for _, k in rider]
    n = len(bufs)
    sems = [pltpu.SemaphoreType.DMA((3 * n,)), pltpu.SemaphoreType.DMA((3 * n,))] if n else []
    return bufs, kinds, [_ANY] * n, [jax.ShapeDtypeStruct(b.shape, b.dtype) for b in bufs], sems


def _d2d_stream(src4, other_half, name):
    s, _, rows, cols = src4.shape
    tr = _row_tile(rows, cols, SUB16, STREAM_ELEMS_PER_BLOCK)
    nblk = rows // tr

    nh = src4.shape[1]

    def body(c_ref, src_ref, dst_ref, ssem, rsem):
        k = pl.program_id(0)
        i = pl.program_id(1)
        x, y, c = _position()
        sib = (x, y, 1 - c)
        blk = dst_ref.at[pl.ds(pl.multiple_of((k * nblk + i) * tr, SUB16), tr)]
        cp = _remote(src_ref, blk, ssem, rsem, sib)
        cp.start()
        cp.wait_send()

        @pl.when(jnp.logical_and(k == s - 1, i == nblk - 1))
        def _():
            _remote(dst_ref, dst_ref, ssem, rsem, sib).wait_recv()

    if other_half:
        src_map = lambda k, i, c_ref: ((k * nh + 1 - c_ref[0]) * nblk + i, 0)
    else:
        src_map = lambda k, i, c_ref: (k * nh * nblk + i, 0)
    out = pl.pallas_call(
        body, name=name,
        grid_spec=pltpu.PrefetchScalarGridSpec(
            num_scalar_prefetch=1, grid=(s, nblk),
            in_specs=[pl.BlockSpec((tr, cols), src_map)], out_specs=_ANY,
            scratch_shapes=[pltpu.SemaphoreType.DMA, pltpu.SemaphoreType.DMA]),
        out_shape=jax.ShapeDtypeStruct((s * rows, cols), src4.dtype),
        compiler_params=_cp("arbitrary", "arbitrary"))(_core_index(), src4.reshape(s * nh * rows, cols))
    return out.reshape(s, rows, cols)


def _scatter_copies(srcs, lands, kinds, ssem, rsem, lsem):
    x, y, c = _position()
    my = 2 * x + y
    cps = []
    for u, (src, land, kind) in enumerate(zip(srcs, lands, kinds)):
        cps.append(pltpu.make_async_copy(_unit_view(kind, src, my), land.at[my], lsem.at[u]))
        for j in (1, 2, 3):
            tx, ty = _peer_chip(x, y, j)
            cps.append(_remote(_unit_view(kind, src, 2 * tx + ty), land.at[my], ssem.at[3 * u + j - 1],
                               rsem.at[3 * u + j - 1], (tx, ty, c)))
    return cps


def _scatter_args(scatter):
    srcs = [s for s, _ in scatter]
    kinds = [k for _, k in scatter]
    n = len(srcs)
    shapes = [jax.ShapeDtypeStruct((N_CHIPS, s.shape[0], s.shape[1] // N_CHIPS) if k == "col" else s.shape, s.dtype)
              for s, k in scatter]
    sems = [pltpu.SemaphoreType.DMA((3 * n,)), pltpu.SemaphoreType.DMA((3 * n,)),
            pltpu.SemaphoreType.DMA((n,))] if n else []
    return srcs, kinds, [_ANY] * n, shapes, sems


def _rs_scatter(scatter):
    srcs, kinds, specs, shapes, sems = _scatter_args(scatter)
    n = len(srcs)

    def body(*refs):
        cps = _scatter_copies(refs[:n], refs[n:2 * n], kinds, *refs[2 * n:])
        for cp in cps:
            cp.start()
        for cp in cps:
            cp.wait()

    return pl.pallas_call(body, name="rs_scatter", in_specs=specs, out_specs=specs, out_shape=shapes,
                          scratch_shapes=sems)(*srcs)


SMALL = (("sc_conv_w", True), ("sc_conv_b", False), ("lru_b_in", True), ("lru_conv_w", True),
         ("lru_conv_b", True), ("lru_b_gate", True), ("lru_lambda", True), ("ffn_conv_w", True),
         ("ffn_conv_b", False), ("ln_g", True), ("ln_b", True))
PACK_ROW_MULT = 2 * SUB16


def _pack_rows(shapes):
    n = sum(math.prod(shapes[name]) for name, _ in SMALL)
    rows = -(-n // 128)
    return -(-rows // PACK_ROW_MULT) * PACK_ROW_MULT


def _pack_local(vals, shapes):
    flat = jnp.concatenate([vals[name].reshape(-1) for name, _ in SMALL])
    rows = _pack_rows(shapes)
    return jnp.pad(flat, (0, rows * 128 - flat.shape[0])).reshape(rows, 128)


def _unpack_local(pack, shapes):
    flat = pack.reshape(-1)
    out, off = {}, 0
    for name, _ in SMALL:
        n = math.prod(shapes[name])
        out[name] = flat[off:off + n].reshape(shapes[name])
        off += n
    return out


def _pack_slots(fulls, shapes):
    parts = []
    for name, sharded in SMALL:
        v = fulls[name]
        if sharded:
            ns = shapes[name][-1]
            v = jnp.moveaxis(v.reshape(v.shape[:-1] + (N_CHIPS, ns)), -2, 0).reshape(N_CHIPS, -1)
        else:
            v = jnp.broadcast_to(v.reshape(1, -1), (N_CHIPS, v.size))
        parts.append(v)
    flat = jnp.concatenate(parts, axis=1)
    rows = _pack_rows(shapes)
    return jnp.pad(flat, ((0, 0), (0, rows * 128 - flat.shape[1]))).reshape(N_CHIPS, rows, 128)


def _unpack_slots(packs, shapes):
    flat = packs.reshape(N_CHIPS, -1)
    out, off = {}, 0
    for name, sharded in SMALL:
        n = math.prod(shapes[name])
        if sharded:
            seg = flat[:, off:off + n].reshape((N_CHIPS,) + tuple(shapes[name]))
            seg = jnp.moveaxis(seg, 0, -2)
            out[name] = seg.reshape(seg.shape[:-2] + (N_CHIPS * shapes[name][-1],))
        off += n
    return out


WEIGHTS = ("sc_w_in", "sc_conv_w", "sc_conv_b", "sc_w_out", "lru_w_in", "lru_b_in", "lru_conv_w", "lru_conv_b",
           "lru_w_gate", "lru_b_gate", "lru_lambda", "lru_w_out", "ffn_w_up", "ffn_conv_w", "ffn_conv_b",
           "ffn_w_down", "ln_g", "ln_b")
GATHER_KIND = {"sc_w_in": "col", "sc_w_out": "lead", "lru_w_in": "col", "lru_w_out": "lead", "ffn_w_up": "col",
               "ffn_w_down": "lead"}


def kernel(x, sc_w_in, sc_conv_w, sc_conv_b, sc_w_out, lru_w_in, lru_b_in, lru_conv_w, lru_conv_b, lru_w_gate, lru_b_gate, lru_lambda, lru_w_out, ffn_w_up, ffn_conv_w, ffn_conv_b, ffn_w_down, ln_g, ln_b, loss_target, m_sc_w_in, m_sc_conv_w, m_sc_conv_b, m_sc_w_out, m_lru_w_in, m_lru_b_in, m_lru_conv_w, m_lru_conv_b, m_lru_w_gate, m_lru_b_gate, m_lru_lambda, m_lru_w_out, m_ffn_w_up, m_ffn_conv_w, m_ffn_conv_b, m_ffn_w_down, m_ln_g, m_ln_b, v_sc_w_in, v_sc_conv_w, v_sc_conv_b, v_sc_w_out, v_lru_w_in, v_lru_b_in, v_lru_conv_w, v_lru_conv_b, v_lru_w_gate, v_lru_b_gate, v_lru_lambda, v_lru_w_out, v_ffn_w_up, v_ffn_conv_w, v_ffn_conv_b, v_ffn_w_down, v_ln_g, v_ln_b):
    w = dict(zip(WEIGHTS, (sc_w_in, sc_conv_w, sc_conv_b, sc_w_out, lru_w_in, lru_b_in, lru_conv_w, lru_conv_b,
                           lru_w_gate, lru_b_gate, lru_lambda, lru_w_out, ffn_w_up, ffn_conv_w, ffn_conv_b,
                           ffn_w_down, ln_g, ln_b)))
    mom = dict(zip(WEIGHTS, (m_sc_w_in, m_sc_conv_w, m_sc_conv_b, m_sc_w_out, m_lru_w_in, m_lru_b_in, m_lru_conv_w,
                             m_lru_conv_b, m_lru_w_gate, m_lru_b_gate, m_lru_lambda, m_lru_w_out, m_ffn_w_up,
                             m_ffn_conv_w, m_ffn_conv_b, m_ffn_w_down, m_ln_g, m_ln_b)))
    vel = dict(zip(WEIGHTS, (v_sc_w_in, v_sc_conv_w, v_sc_conv_b, v_sc_w_out, v_lru_w_in, v_lru_b_in, v_lru_conv_w,
                             v_lru_conv_b, v_lru_w_gate, v_lru_b_gate, v_lru_lambda, v_lru_w_out, v_ffn_w_up,
                             v_ffn_conv_w, v_ffn_conv_b, v_ffn_w_down, v_ln_g, v_ln_b)))
    bd, seq, d = x.shape
    t = bd * seq
    small_shapes = {name: w[name].shape for name, _ in SMALL}

    w_pack = _pack_local(w, small_shapes)
    gate_shape = w["lru_w_gate"].shape
    bufs = {(n, l): (_place(w[n], l, k, BF16, "place_w"), k)
            for n, k in GATHER_KIND.items() for l in range(w[n].shape[0])}
    bufs["gate"] = (_place(w["lru_w_gate"].reshape(1, -1, gate_shape[-1]), 0, "lead", BF16, "place_w"), "lead")
    bufs["pack"] = (_place(w_pack[None], 0, "lead", F32, "place_w"), "lead")

    def layer_keys(i):
        mixer = ("sc_w_in", "sc_w_out") if i % 2 == 0 else ("lru_w_in", "lru_w_out")
        return [(mixer[0], i // 2), (mixer[1], i // 2)], [("ffn_w_up", i), ("ffn_w_down", i)]

    def gathered(keys, arrays):
        for key, arr in zip(keys, arrays):
            bufs[key] = (arr, bufs[key][1])

    def wt(name, l):
        arr = bufs[(name, l)][0]
        return arr.reshape(1, -1, arr.shape[-1])

    first = layer_keys(0)[0] + layer_keys(0)[1] + ["gate", "pack"]
    gathered(first, _all_gather([bufs[k][0] for k in first], [bufs[k][1] for k in first]))
    full = _unpack_slots(bufs["pack"][0], small_shapes)
    full["sc_conv_b"] = sc_conv_b
    full["ffn_conv_b"] = ffn_conv_b
    wg_full = jnp.moveaxis(bufs["gate"][0].reshape((N_CHIPS,) + gate_shape), 0, -2)
    wg_full = wg_full.reshape(wg_full.shape[:-2] + (2 * LRU_BLOCK,))
    f = N_CHIPS * w["ffn_w_down"].shape[1]
    rw = N_CHIPS * w["lru_w_out"].shape[1]

    x0 = x.reshape(t, d)
    xb = x0.astype(BF16)
    cur, cur_b = x0, xb
    saved = []

    for i in range(DEPTH):
        j = i // 2
        s = {"xb": cur_b}
        mixer_next, ffn_next = layer_keys(i + 1) if i + 1 < DEPTH else ([], [])
        behind_out, behind_up, behind_down = mixer_next[1:], mixer_next[:1] + ffn_next[:1], ffn_next[1:]
        if i % 2 == 0:
            h = _mm_nn(cur_b, wt("sc_w_in", j), 0, None, 3 * d, "sc_in")
            q = _sc_fwd(h, full["sc_conv_w"][j], full["sc_conv_b"][j][None], seq, "sc_fwd")
            z1, x1, x1b, *arrived = _mm_nn_ln(q, wt("sc_w_out", j), 0, cur, full["ln_g"][i, 0][None],
                                              full["ln_b"][i, 0][None], "sc_out_ln",
                                              rider=[bufs[k] for k in behind_out])
        else:
            h = _mm_nn(cur_b, wt("lru_w_in", j), 0, full["lru_b_in"][j][None], 2 * rw, "lru_in")
            hs, q = _lru_fwd(h, full["lru_conv_w"][j], full["lru_conv_b"][j][None], wg_full[j],
                             full["lru_b_gate"][j], full["lru_lambda"][j][None], seq, "lru_fwd")
            s["hs"] = hs
            z1, x1, x1b, *arrived = _mm_nn_ln(q, wt("lru_w_out", j), 0, cur, full["ln_g"][i, 0][None],
                                              full["ln_b"][i, 0][None], "lru_out_ln",
                                              rider=[bufs[k] for k in behind_out])
        gathered(behind_out, arrived)
        s.update(h=h, q=q, z1=z1, x1b=x1b)
        h3, pre3, act, *arrived = _ffn_up(x1b, wt("ffn_w_up", i), 0, full["ffn_conv_w"][i],
                                          full["ffn_conv_b"][i][None], seq, "ffn_up",
                                          rider=[bufs[k] for k in behind_up])
        gathered(behind_up, arrived)
        z2, x2, x2b, *arrived = _mm_nn_ln(act, wt("ffn_w_down", i), 0, x1, full["ln_g"][i, 1][None],
                                          full["ln_b"][i, 1][None], "ffn_down_ln",
                                          rider=[bufs[k] for k in behind_down])
        gathered(behind_down, arrived)
        s.update(h3=h3, pre3=pre3, act=act, z2=z2)
        saved.append(s)
        cur, cur_b = x2, x2b

    dcur, loss_parts = _loss_bwd(cur, loss_target.reshape(t, d))
    loss = lax.psum(jnp.sum(loss_parts), MESH_AXES)

    def pair_sum(p, kind):
        lead = kind == "lead"
        p4 = p.reshape(N_CHIPS if lead else 1, 2, -1, p.shape[-1])
        chip_sum = _add_pair(p4, _d2d_stream(p4, True, "rs_swap"), "rs_add_pair")
        return chip_sum if lead else chip_sum[0]

    gp = {n: [None] * w[n].shape[0] for n, _ in SMALL}
    gp["lru_w_gate"] = [None] * gate_shape[0]
    landed, pending = {}, []
    for i in reversed(range(DEPTH)):
        j = i // 2
        s = saved[i]
        mixer_keys, ffn_keys = layer_keys(i)
        dz2, dz2b, dg, db = _ln_bwd(dcur, s["z2"], full["ln_g"][i, 1][None], "ln_bwd")
        gp["ln_g"][i] = [None, dg[0]]
        gp["ln_b"][i] = [None, db[0]]
        p_down = _mm_tn(s["act"], dz2b[None], f // 2, d, "ffn_down_dw").reshape(N_CHIPS, -1, d)
        dh3, dcwg, dcwv, dcbg, dcbv, *lands = _ffn_down_bwd(
            dz2b, wt("ffn_w_down", i), 0, s["h3"], s["pre3"], full["ffn_conv_w"][i], seq, "ffn_down_bwd",
            scatter=[(chip_sum, kind) for _, chip_sum, kind in pending])
        landed.update({key: land for (key, _, _), land in zip(pending, lands)})
        gp["ffn_conv_w"][i] = jnp.concatenate([dcwg, dcwv], axis=1)
        gp["ffn_conv_b"][i] = jnp.concatenate([dcbg[0], dcbv[0]])
        dx1 = _mm_nt_res(dh3, wt("ffn_w_up", i), 0, dz2, f // 2, "ffn_up_dx")
        p_up = _mm_tn(s["x1b"], dh3, d, f // 2, "ffn_up_dw")
        ffn_partials = ((ffn_keys[0], p_up, "col"), (ffn_keys[1], p_down, "lead"))
        early = [(key, pair_sum(p, kind), kind) for key, p, kind in ffn_partials] if i == 0 else []
        early_scatter = [(chip_sum, kind) for _, chip_sum, kind in early]
        dz1, dz1b, dg, db = _ln_bwd(dx1, s["z1"], full["ln_g"][i, 0][None], "ln_bwd")
        gp["ln_g"][i][0] = dg[0]
        gp["ln_b"][i][0] = db[0]
        gp["ln_g"][i] = jnp.stack(gp["ln_g"][i])
        gp["ln_b"][i] = jnp.stack(gp["ln_b"][i])
        if i % 2 == 0:
            dq = _mm_nt(dz1b, wt("sc_w_out", j), 0, d, "sc_out_dx")
            p_out = _mm_tn(s["q"], dz1b[None], d, d, "sc_out_dw")
            dh3, dcw, dcb = _sc_bwd(s["h"], dq, full["sc_conv_w"][j], full["sc_conv_b"][j][None], seq, "sc_bwd")
            gp["sc_conv_w"][j] = dcw
            gp["sc_conv_b"][j] = dcb[0]
            res = _mm_nt_res(dh3, wt("sc_w_in", j), 0, dz1, d, "sc_in_dx", scatter=early_scatter)
            dcur, lands = (res[0], res[1:]) if early else (res, [])
            landed.update({key: land for (key, _, _), land in zip(early, lands)})
            p_in = _mm_tn(s["xb"], dh3, d, d, "sc_in_dw")
        else:
            dq = _mm_nt(dz1b, wt("lru_w_out", j), 0, rw, "lru_out_dx")
            p_out = _mm_tn(s["q"], dz1b[None], rw, d, "lru_out_dw")
            dh3, dbin, dcw, dcb, dwg, dbg, dlam = _lru_bwd(
                s["h"], s["hs"], dq, full["lru_conv_w"][j], full["lru_conv_b"][j][None], wg_full[j],
                full["lru_b_gate"][j], full["lru_lambda"][j][None], seq, "lru_bwd")
            gp["lru_b_in"][j] = dbin[0]
            gp["lru_conv_w"][j] = dcw
            gp["lru_conv_b"][j] = dcb[0]
            gp["lru_w_gate"][j] = dwg
            gp["lru_b_gate"][j] = dbg
            gp["lru_lambda"][j] = dlam[0]
            dcur = _mm_nt_res(dh3, wt("lru_w_in", j), 0, dz1, rw, "lru_in_dx")
            p_in = _mm_tn(s["xb"], dh3, d, rw, "lru_in_dw")
        layer_partials = ((mixer_keys[0], p_in, "col"), (mixer_keys[1], p_out.reshape(N_CHIPS, -1, d), "lead"))
        layer_partials += () if early else ffn_partials
        pending = [(key, pair_sum(p, kind), kind) for key, p, kind in layer_partials]
    grad_x = dcur.reshape(bd, seq, d)
    gp = {n: jnp.stack(v) for n, v in gp.items()}

    gate = gp["lru_w_gate"]
    gate = jnp.moveaxis(gate.reshape(gate.shape[:-1] + (N_CHIPS, gate_shape[-1])), -2, 0)
    gate = gate.astype(BF16).reshape(N_CHIPS, -1, gate_shape[-1])
    pending += [("gate", pair_sum(gate, "lead"), "lead"),
                ("pack", pair_sum(_pack_slots(gp, small_shapes), "lead"), "lead")]
    lands = _rs_scatter([(chip_sum, kind) for _, chip_sum, kind in pending])
    landed.update({key: land for (key, _, _), land in zip(pending, lands)})

    def update(keys, w3, m3, v3):
        g_mine = jnp.stack([_add_chips(landed[k], "rs_add_chips") for k in keys])
        g_sib = _d2d_stream(g_mine.reshape(1, 1, -1, g_mine.shape[-1]), False, "rs_share").reshape(g_mine.shape)
        return _adamw(w3, g_mine, g_sib, m3, v3, "adamw")

    g_out, d_out, m_out, v_out = {}, {}, {}, {}
    for n in GATHER_KIND:
        outs = update([(n, l) for l in range(w[n].shape[0])], w[n], mom[n], vel[n])
        g_out[n], d_out[n], m_out[n], v_out[n] = outs
    as_rows = lambda a: a.reshape(1, -1, a.shape[-1])
    outs = update(["gate"], as_rows(w["lru_w_gate"]), as_rows(mom["lru_w_gate"]), as_rows(vel["lru_w_gate"]))
    g_out["lru_w_gate"], d_out["lru_w_gate"], m_out["lru_w_gate"], v_out["lru_w_gate"] = (
        o.reshape(gate_shape) for o in outs)
    packs = update(["pack"], w_pack[None], _pack_local(mom, small_shapes)[None], _pack_local(vel, small_shapes)[None])
    for dst, pack in zip((g_out, d_out, m_out, v_out), packs):
        dst.update(_unpack_local(pack[0], small_shapes))

    return (loss, grad_x, *[g_out[n] for n in WEIGHTS], *[d_out[n] for n in WEIGHTS],
            *[m_out[n] for n in WEIGHTS], *[v_out[n] for n in WEIGHTS])
```

```python
import math

import jax
import jax.numpy as jnp
from jax import lax
from jax.experimental import pallas as pl
from jax.experimental.pallas import tpu as pltpu

F32 = jnp.float32
BF16 = jnp.bfloat16

DEPTH = 4
LRU_HEADS = 10
LRU_BLOCK = 128
LRU_C = 8.0
LN_EPS = 1e-5
ALPHA = (2.0 * DEPTH) ** 0.25
ADAM_LR, ADAM_B1, ADAM_B2, ADAM_EPS, ADAM_WD, ADAM_STEP = 0.001, 0.9, 0.999, 1e-08, 0.01, 10
N_CHIPS = 4
MESH_AXES = ("x", "y", "c")

VMEM_LIMIT_BYTES = 48 * 1024 * 1024
TM_MM = 512
TM_RES = 1024
TT_MM = 2048
TM_SC = 256
FFN_COL_BLOCKS = 2
RC = 128
LANE = 128
MXU_COLS = 256
TS_LRU = 128
TM_LN = 512
ELEMS_PER_BLOCK = 256 * 1024
STREAM_ELEMS_PER_BLOCK = 1024 * 1024
SUB = 8
SUB16 = 16


def _cp(*sem):
    return pltpu.CompilerParams(dimension_semantics=sem, vmem_limit_bytes=VMEM_LIMIT_BYTES)


def _sigmoid(v):
    return 0.5 + 0.5 * jnp.tanh(0.5 * v)


def _softplus(v):
    e = jnp.exp(-jnp.abs(v))
    log1p = jnp.where(e < 1e-3, e * (1.0 - e * (0.5 - e * (1.0 / 3.0))), jnp.log(1.0 + e))
    return jnp.maximum(v, 0.0) + log1p


def _one_minus_exp(v):
    series = -v * (1.0 + v * (0.5 + v * (1.0 / 6.0 + v * (1.0 / 24.0))))
    return jnp.where(v > -0.02, series, 1.0 - jnp.exp(v))


def _gelu_and_grad(v):
    k = math.sqrt(2.0 / math.pi)
    t = jnp.tanh(k * (v + 0.044715 * v * v * v))
    val = 0.5 * v * (1.0 + t)
    grad = 0.5 * (1.0 + t) + 0.5 * v * (1.0 - t * t) * k * (1.0 + 3.0 * 0.044715 * v * v)
    return val, grad


def _down(ext, k, n_head):
    if k:
        ext = pltpu.roll(ext, k, 0)
    return ext[n_head:]


def _up(ext, k, n):
    if k:
        ext = pltpu.roll(ext, ext.shape[0] - k, 0)
    return ext[:n]


def _row(ref, k):
    return ref[k:k + 1, :]


def _colsum(v):
    return jnp.sum(v, axis=0, keepdims=True)


def _mm_nn(a, w3, l, bias, tn, name):
    m, k = a.shape
    n = w3.shape[2]
    tm = min(TM_MM, m)
    has_bias = bias is not None

    def body(*refs):
        if has_bias:
            a_ref, w_ref, b_ref, o_ref = refs
        else:
            a_ref, w_ref, o_ref = refs
        acc = jnp.dot(a_ref[...], w_ref[...], preferred_element_type=F32)
        if has_bias:
            acc = acc + b_ref[...]
        o_ref[...] = acc.astype(o_ref.dtype)

    in_specs = [pl.BlockSpec((tm, k), lambda i, j: (i, 0)),
                pl.BlockSpec((None, k, tn), lambda i, j: (l, 0, j))]
    args = [a, w3]
    if has_bias:
        in_specs.append(pl.BlockSpec((1, tn), lambda i, j: (0, j)))
        args.append(bias)
    return pl.pallas_call(
        body, name=name, grid=(m // tm, n // tn), in_specs=in_specs,
        out_specs=pl.BlockSpec((tm, tn), lambda i, j: (i, j)),
        out_shape=jax.ShapeDtypeStruct((m, n), BF16),
        compiler_params=_cp("parallel", "arbitrary"))(*args)


def _mm_nn_ln(a, w3, l, xres, g, b, name, rider=()):
    m, k = a.shape
    n = w3.shape[2]
    tm = min(TM_MM, m)
    bufs, rkinds, rspecs, rshapes, rsems = _rider_args(rider)
    nr = len(bufs)

    def body(*refs):
        a_ref, w_ref, x_ref, g_ref, b_ref = refs[:5]
        z_ref, xn_ref, xb_ref = refs[5 + nr:8 + nr]
        rout, sems = refs[8 + nr:8 + 2 * nr], refs[8 + 2 * nr:]
        if nr:
            @pl.when(pl.program_id(0) == 0)
            def _():
                _rider_start(rout, rkinds, *sems)

        y = jnp.dot(a_ref[...], w_ref[...], preferred_element_type=F32)
        z = ALPHA * x_ref[...] + y
        mu = jnp.mean(z, axis=-1, keepdims=True)
        zc = z - mu
        var = jnp.mean(zc * zc, axis=-1, keepdims=True)
        xn = zc * lax.rsqrt(var + LN_EPS) * g_ref[...] + b_ref[...]
        z_ref[...] = z
        xn_ref[...] = xn
        xb_ref[...] = xn.astype(BF16)
        if nr:
            @pl.when(pl.program_id(0) == m // tm - 1)
            def _():
                _rider_wait(rout, rkinds, *sems)

    row = pl.BlockSpec((tm, n), lambda i: (i, 0))
    vec = pl.BlockSpec((1, n), lambda i: (0, 0))
    return pl.pallas_call(
        body, name=name, grid=(m // tm,),
        in_specs=[pl.BlockSpec((tm, k), lambda i: (i, 0)),
                  pl.BlockSpec((None, k, n), lambda i: (l, 0, 0)), row, vec, vec] + rspecs,
        out_specs=[row, row, row] + rspecs,
        out_shape=[jax.ShapeDtypeStruct((m, n), F32), jax.ShapeDtypeStruct((m, n), F32),
                   jax.ShapeDtypeStruct((m, n), BF16)] + rshapes,
        input_output_aliases={5 + u: 3 + u for u in range(nr)},
        scratch_shapes=rsems,
        compiler_params=_cp("arbitrary"))(a, w3, xres, g, b, *bufs)


def _mm_nt(a, w3, l, tk, name):
    m, n = a.shape
    kd = w3.shape[1]
    tm = min(TM_MM, m)

    def body(a_ref, w_ref, o_ref):
        o_ref[...] = lax.dot_general(a_ref[...], w_ref[...], (((1,), (1,)), ((), ())),
                                     preferred_element_type=F32).astype(o_ref.dtype)

    return pl.pallas_call(
        body, name=name, grid=(m // tm, kd // tk),
        in_specs=[pl.BlockSpec((tm, n), lambda i, j: (i, 0)),
                  pl.BlockSpec((None, tk, n), lambda i, j: (l, j, 0))],
        out_specs=pl.BlockSpec((tm, tk), lambda i, j: (i, j)),
        out_shape=jax.ShapeDtypeStruct((m, kd), BF16),
        compiler_params=_cp("parallel", "arbitrary"))(a, w3)


def _mm_nt_res(dh3, w3, l, dz, tc, name, scatter=()):
    g, m, cg = dh3.shape
    kd = w3.shape[1]
    ncg = cg // tc
    nk = g * ncg
    tm = min(TM_RES, m)
    srcs, skinds, sspecs, sshapes, ssems = _scatter_args(scatter)
    ns = len(srcs)

    def body(*refs):
        a_ref, w_ref, dz_ref = refs[:3]
        sin, o_ref, lands = refs[3:3 + ns], refs[3 + ns], refs[4 + ns:4 + 2 * ns]
        acc, sems = refs[4 + 2 * ns], refs[5 + 2 * ns:]
        i = pl.program_id(0)
        k = pl.program_id(1)
        if ns:
            @pl.when(jnp.logical_and(i == 0, k == 0))
            def _():
                for cp in _scatter_copies(sin, lands, skinds, *sems):
                    cp.start()

        @pl.when(k == 0)
        def _():
            acc[...] = ALPHA * dz_ref[...]

        acc[...] += lax.dot_general(a_ref[...], w_ref[...], (((1,), (1,)), ((), ())),
                                    preferred_element_type=F32)

        @pl.when(k == nk - 1)
        def _():
            o_ref[...] = acc[...]

        if ns:
            @pl.when(jnp.logical_and(i == m // tm - 1, k == nk - 1))
            def _():
                for cp in _scatter_copies(sin, lands, skinds, *sems):
                    cp.wait()

    outs = pl.pallas_call(
        body, name=name, grid=(m // tm, nk),
        in_specs=[pl.BlockSpec((None, tm, tc), lambda i, k: (k // ncg, i, k % ncg)),
                  pl.BlockSpec((None, kd, tc), lambda i, k: (l, 0, k)),
                  pl.BlockSpec((tm, kd), lambda i, k: (i, 0))] + sspecs,
        out_specs=[pl.BlockSpec((tm, kd), lambda i, k: (i, 0))] + sspecs,
        out_shape=[jax.ShapeDtypeStruct((m, kd), F32)] + sshapes,
        scratch_shapes=[pltpu.VMEM((tm, kd), F32)] + ssems,
        compiler_params=_cp("arbitrary", "arbitrary"))(dh3, w3, dz, *srcs)
    return outs if ns else outs[0]


def _mm_tn(a, b3, tka, tnb, name):
    t, ka = a.shape
    g, _, cg = b3.shape
    ncg = cg // tnb
    tt = min(TT_MM, t)
    nt = t // tt

    def body(a_ref, b_ref, o_ref, acc):
        s = pl.program_id(2)

        @pl.when(s == 0)
        def _():
            acc[...] = jnp.zeros_like(acc)

        acc[...] += lax.dot_general(a_ref[...], b_ref[...], (((0,), (0,)), ((), ())),
                                    preferred_element_type=F32)

        @pl.when(s == nt - 1)
        def _():
            o_ref[...] = acc[...].astype(o_ref.dtype)

    return pl.pallas_call(
        body, name=name, grid=(ka // tka, g * ncg, nt),
        in_specs=[pl.BlockSpec((tt, tka), lambda i, j, s: (s, i)),
                  pl.BlockSpec((None, tt, tnb), lambda i, j, s: (j // ncg, s, j % ncg))],
        out_specs=pl.BlockSpec((tka, tnb), lambda i, j, s: (i, j)),
        out_shape=jax.ShapeDtypeStruct((ka, g * cg), BF16),
        scratch_shapes=[pltpu.VMEM((tka, tnb), F32)],
        compiler_params=_cp("parallel", "parallel", "arbitrary"))(a, b3)


def _ln_bwd(dxn, z, g, name, target=None):
    m, d = z.shape
    tm = min(TM_LN, m)
    with_loss = target is not None

    def body(*refs):
        dx_ref, z_ref, g_ref = refs[:3]
        dz_ref, dzb_ref, dg_ref, db_ref = refs[3 + with_loss:7 + with_loss]

        @pl.when(pl.program_id(0) == 0)
        def _():
            for r in refs[5 + with_loss:]:
                r[...] = jnp.zeros_like(r)

        zz = z_ref[...]
        dx = dx_ref[...]
        if with_loss:
            err = dx - refs[3][...]
            dx = err * (1.0 / d)
            refs[8][...] += _colsum(err * err) * (0.5 / d)
        mu = jnp.mean(zz, axis=-1, keepdims=True)
        zc = zz - mu
        var = jnp.mean(zc * zc, axis=-1, keepdims=True)
        rstd = lax.rsqrt(var + LN_EPS)
        xh = zc * rstd
        dg_ref[...] += _colsum(dx * xh)
        db_ref[...] += _colsum(dx)
        dxh = dx * g_ref[...]
        m1 = jnp.mean(dxh, axis=-1, keepdims=True)
        m2 = jnp.mean(dxh * xh, axis=-1, keepdims=True)
        dz = rstd * (dxh - m1 - xh * m2)
        dz_ref[...] = dz
        dzb_ref[...] = dz.astype(BF16)

    row = pl.BlockSpec((tm, d), lambda i: (i, 0))
    vec = pl.BlockSpec((1, d), lambda i: (0, 0))
    vec_shape = jax.ShapeDtypeStruct((1, d), F32)
    return pl.pallas_call(
        body, name=name, grid=(m // tm,), in_specs=[row, row, vec] + [row] * with_loss,
        out_specs=[row, row, vec, vec] + [vec] * with_loss,
        out_shape=[jax.ShapeDtypeStruct((m, d), F32), jax.ShapeDtypeStruct((m, d), BF16), vec_shape, vec_shape]
        + [vec_shape] * with_loss,
        compiler_params=_cp("arbitrary"))(dxn, z, g, *([target] if with_loss else []))


def _sc_fwd(h, cw, cb, seq, name):
    t, d3 = h.shape
    d = d3 // 3
    tm = min(TM_SC, seq)

    def body(hb_ref, hc_ref, hv_ref, cw_ref, cb_ref, q_ref, carry):
        i = pl.program_id(0)

        @pl.when(lax.rem(i * tm, seq) == 0)
        def _():
            carry[...] = jnp.zeros_like(carry)

        p = hc_ref[...].astype(F32) * hv_ref[...].astype(F32)
        ext = jnp.concatenate([carry[...], p], axis=0)
        u = cb_ref[...] + _row(cw_ref, 0) * _down(ext, 2, SUB) + _row(cw_ref, 1) * _down(ext, 1, SUB) \
            + _row(cw_ref, 2) * p
        q_ref[...] = (hb_ref[...].astype(F32) * u).astype(BF16)
        carry[...] = p[tm - SUB:, :]

    blk = lambda c: pl.BlockSpec((tm, d), lambda i: (i, c))
    return pl.pallas_call(
        body, name=name, grid=(t // tm,),
        in_specs=[blk(0), blk(1), blk(2), pl.BlockSpec((3, d), lambda i: (0, 0)),
                  pl.BlockSpec((1, d), lambda i: (0, 0))],
        out_specs=pl.BlockSpec((tm, d), lambda i: (i, 0)),
        out_shape=jax.ShapeDtypeStruct((t, d), BF16),
        scratch_shapes=[pltpu.VMEM((SUB, d), F32)],
        compiler_params=_cp("arbitrary"))(h, h, h, cw, cb)


def _sc_bwd(h, dq, cw, cb, seq, name):
    t, d3 = h.shape
    d = d3 // 3
    tm = min(TM_SC, seq)
    nt = t // tm
    hpt = tm // SUB16

    def body(hb_ref, hc_ref, hv_ref, hch_ref, hvh_ref, dq_ref, cw_ref, cb_ref,
             dh_ref, dcw_ref, dcb_ref, carry):
        i = pl.program_id(0)
        ri = nt - 1 - i

        @pl.when(i == 0)
        def _():
            dcw_ref[...] = jnp.zeros_like(dcw_ref)
            dcb_ref[...] = jnp.zeros_like(dcb_ref)

        @pl.when(lax.rem((ri + 1) * tm, seq) == 0)
        def _():
            carry[...] = jnp.zeros_like(carry)

        keep = jnp.where(lax.rem(ri * tm, seq) == 0, 0.0, 1.0)
        gb = hb_ref[...].astype(F32)
        gc = hc_ref[...].astype(F32)
        v = hv_ref[...].astype(F32)
        p = gc * v
        p_head = hch_ref[...].astype(F32) * hvh_ref[...].astype(F32) * keep
        ext = jnp.concatenate([p_head, p], axis=0)
        pm2 = _down(ext, 2, SUB16)
        pm1 = _down(ext, 1, SUB16)
        u = cb_ref[...] + _row(cw_ref, 0) * pm2 + _row(cw_ref, 1) * pm1 + _row(cw_ref, 2) * p
        dqf = dq_ref[...].astype(F32)
        du = dqf * gb
        dcb_ref[...] += _colsum(du)
        dcw_ref[0:1, :] += _colsum(du * pm2)
        dcw_ref[1:2, :] += _colsum(du * pm1)
        dcw_ref[2:3, :] += _colsum(du * p)
        ext2 = jnp.concatenate([du, carry[...]], axis=0)
        dp = _row(cw_ref, 2) * du + _row(cw_ref, 1) * _up(ext2, 1, tm) + _row(cw_ref, 0) * _up(ext2, 2, tm)
        carry[...] = du[0:SUB, :]
        dh_ref[0] = (dqf * u).astype(BF16)
        dh_ref[1] = (dp * v).astype(BF16)
        dh_ref[2] = (dp * gc).astype(BF16)

    blk = lambda c: pl.BlockSpec((tm, d), lambda i: (nt - 1 - i, c))
    head = lambda c: pl.BlockSpec((SUB16, d), lambda i: (jnp.maximum((nt - 1 - i) * hpt - 1, 0), c))
    vec = lambda r: pl.BlockSpec((r, d), lambda i: (0, 0))
    return pl.pallas_call(
        body, name=name, grid=(nt,),
        in_specs=[blk(0), blk(1), blk(2), head(1), head(2),
                  pl.BlockSpec((tm, d), lambda i: (nt - 1 - i, 0)), vec(3), vec(1)],
        out_specs=[pl.BlockSpec((3, tm, d), lambda i: (0, nt - 1 - i, 0)), vec(3), vec(1)],
        out_shape=[jax.ShapeDtypeStruct((3, t, d), BF16), jax.ShapeDtypeStruct((3, d), F32),
                   jax.ShapeDtypeStruct((1, d), F32)],
        scratch_shapes=[pltpu.VMEM((SUB, d), F32)],
        compiler_params=_cp("arbitrary"))(h, h, h, h, h, dq, cw, cb)


def _fold(v):
    return jnp.sum(v.reshape(v.shape[0] // SUB, SUB, v.shape[1]), axis=0)


def _ffn_up(xb, w3, l, cw, cb, seq, name, rider=()):
    t, d = xb.shape
    f = w3.shape[2] // 2
    tc = f // FFN_COL_BLOCKS
    tm = min(TM_MM, seq)
    nc = FFN_COL_BLOCKS
    bufs, rkinds, rspecs, rshapes, rsems = _rider_args(rider)
    nr = len(bufs)

    def body(*refs):
        x_ref, wg_ref, wv_ref, cwg_ref, cwv_ref, cbg_ref, cbv_ref = refs[:7]
        h_ref, pre_ref, act_ref = refs[7 + nr:10 + nr]
        rout = refs[10 + nr:10 + 2 * nr]
        eg, ev = refs[10 + 2 * nr:12 + 2 * nr]
        sems = refs[12 + 2 * nr:]
        i = pl.program_id(1)
        if nr:
            @pl.when(jnp.logical_and(pl.program_id(0) == 0, i == 0))
            def _():
                _rider_start(rout, rkinds, *sems)

        @pl.when(lax.rem(i * tm, seq) == 0)
        def _():
            eg[0:SUB, :] = jnp.zeros((SUB, tc), F32)
            ev[0:SUB, :] = jnp.zeros((SUB, tc), F32)

        xx = x_ref[...]

        def matmul(lo, hi):
            eg[SUB:, lo:hi] = jnp.dot(xx, wg_ref[:, lo:hi], preferred_element_type=F32)
            ev[SUB:, lo:hi] = jnp.dot(xx, wv_ref[:, lo:hi], preferred_element_type=F32)

        def epilogue(lo, hi):
            for c0 in range(lo, hi, LANE):
                cols = slice(c0, c0 + LANE)
                taps = [[ref[k:k + 1, cols] for k in range(3)] + [bref[:, cols]]
                        for ref, bref in ((cwg_ref, cbg_ref), (cwv_ref, cbv_ref))]
                for r0 in range(0, tm, RC):
                    rows = slice(r0, r0 + RC)
                    pres = []
                    for half, e_ref in enumerate((eg, ev)):
                        w0, w1, w2, bias = taps[half]
                        e = e_ref[r0:r0 + RC + SUB, cols]
                        cur = e[SUB:]
                        pre = bias + w0 * _down(e, 2, SUB) + w1 * _down(e, 1, SUB) + w2 * cur
                        h_ref[half, rows, cols] = cur.astype(BF16)
                        pre_ref[half, rows, cols] = pre.astype(BF16)
                        pres.append(pre)
                    act_ref[rows, cols] = (pres[0] * _sigmoid(pres[0]) * pres[1]).astype(BF16)

        blocks = [(lo, min(lo + MXU_COLS, tc)) for lo in range(0, tc, MXU_COLS)]
        matmul(*blocks[0])
        for b, blk in enumerate(blocks):
            if b + 1 < len(blocks):
                matmul(*blocks[b + 1])
            epilogue(*blk)
        eg[0:SUB, :] = eg[tm:tm + SUB, :]
        ev[0:SUB, :] = ev[tm:tm + SUB, :]
        if nr:
            @pl.when(jnp.logical_and(pl.program_id(0) == nc - 1, i == t // tm - 1))
            def _():
                _rider_wait(rout, rkinds, *sems)

    wspec = lambda off: pl.BlockSpec((None, d, tc), lambda j, i: (l, 0, j + off))
    vec = lambda r, off: pl.BlockSpec((r, tc), lambda j, i: (0, j + off))
    pair = pl.BlockSpec((2, tm, tc), lambda j, i: (0, i, j))
    return pl.pallas_call(
        body, name=name, grid=(nc, t // tm),
        in_specs=[pl.BlockSpec((tm, d), lambda j, i: (i, 0)), wspec(0), wspec(nc),
                  vec(3, 0), vec(3, nc), vec(1, 0), vec(1, nc)] + rspecs,
        out_specs=[pair, pair, pl.BlockSpec((tm, tc), lambda j, i: (i, j))] + rspecs,
        out_shape=[jax.ShapeDtypeStruct((2, t, f), BF16), jax.ShapeDtypeStruct((2, t, f), BF16),
                   jax.ShapeDtypeStruct((t, f), BF16)] + rshapes,
        input_output_aliases={7 + u: 3 + u for u in range(nr)},
        scratch_shapes=[pltpu.VMEM((tm + SUB, tc), F32), pltpu.VMEM((tm + SUB, tc), F32)] + rsems,
        compiler_params=_cp("arbitrary", "arbitrary"))(xb, w3, w3, cw, cw, cb, cb, *bufs)


def _ffn_down_bwd(dzb, wd3, l, h3, pre3, cw, seq, name, scatter=()):
    t, d = dzb.shape
    f = wd3.shape[1]
    tc = f // FFN_COL_BLOCKS
    tm = min(TM_MM, seq)
    nt = t // tm
    nc = FFN_COL_BLOCKS
    srcs, skinds, sspecs, sshapes, ssems = _scatter_args(scatter)
    ns = len(srcs)

    def body(*refs):
        dz_ref, wd_ref, h_ref, pre_ref, cwg_ref, cwv_ref = refs[:6]
        sin = refs[6:6 + ns]
        dh_ref, dcwg_ref, dcwv_ref, dcbg_ref, dcbv_ref = refs[6 + ns:11 + ns]
        lands = refs[11 + ns:11 + 2 * ns]
        da_s, carry = refs[11 + 2 * ns:13 + 2 * ns]
        sems = refs[13 + 2 * ns:]
        i = pl.program_id(1)
        ri = nt - 1 - i
        if ns:
            @pl.when(jnp.logical_and(pl.program_id(0) == 0, i == 0))
            def _():
                for cp in _scatter_copies(sin, lands, skinds, *sems):
                    cp.start()

        @pl.when(i == 0)
        def _():
            for r in (dcwg_ref, dcwv_ref, dcbg_ref, dcbv_ref):
                r[...] = jnp.zeros_like(r)

        @pl.when(lax.rem((ri + 1) * tm, seq) == 0)
        def _():
            carry[...] = jnp.zeros_like(carry)

        dz = dz_ref[...]

        def matmul(lo, hi):
            da_s[:, lo:hi] = lax.dot_general(dz, wd_ref[lo:hi, :], (((1,), (1,)), ((), ())),
                                             preferred_element_type=F32)

        def epilogue(lo, hi):
            for c0 in range(lo, hi, LANE):
                cols = slice(c0, c0 + LANE)
                taps = [[ref[k:k + 1, cols] for k in range(3)] for ref in (cwg_ref, cwv_ref)]
                acc = [jnp.zeros((SUB, LANE), F32)] * 8
                for r0 in range(tm - RC, -1, -RC):
                    rows = slice(r0, r0 + RC)
                    da = da_s[rows, cols]
                    gp = pre_ref[0, rows, cols].astype(F32)
                    vp = pre_ref[1, rows, cols].astype(F32)
                    sg = _sigmoid(gp)
                    dpres = (da * vp * (sg * (1.0 + gp * (1.0 - sg))), da * (gp * sg))
                    for half in range(2):
                        w0, w1, w2 = taps[half]
                        dpre = dpres[half]
                        ext = jnp.concatenate([dpre, carry[half, :, cols]], axis=0)
                        u1 = _up(ext, 1, RC)
                        u2 = _up(ext, 2, RC)
                        carry[half, :, cols] = dpre[0:SUB]
                        dh_ref[half, rows, cols] = (w2 * dpre + w1 * u1 + w0 * u2).astype(BF16)
                        hh = h_ref[half, rows, cols].astype(F32)
                        for k, term in enumerate((hh * u2, hh * u1, hh * dpre, dpre)):
                            acc[4 * half + k] = acc[4 * half + k] + _fold(term)
                for half, (dcw_ref, dcb_ref) in enumerate(((dcwg_ref, dcbg_ref), (dcwv_ref, dcbv_ref))):
                    for k in range(3):
                        dcw_ref[k:k + 1, cols] += _colsum(acc[4 * half + k])
                    dcb_ref[:, cols] += _colsum(acc[4 * half + 3])

        blocks = [(lo, min(lo + MXU_COLS, tc)) for lo in range(0, tc, MXU_COLS)]
        matmul(*blocks[0])
        for b, blk in enumerate(blocks):
            if b + 1 < len(blocks):
                matmul(*blocks[b + 1])
            epilogue(*blk)
        if ns:
            @pl.when(jnp.logical_and(pl.program_id(0) == nc - 1, i == nt - 1))
            def _():
                for cp in _scatter_copies(sin, lands, skinds, *sems):
                    cp.wait()

    pair = pl.BlockSpec((2, tm, tc), lambda j, i: (0, nt - 1 - i, j))
    vec = lambda off: pl.BlockSpec((3, tc), lambda j, i: (0, j + off))
    acc_spec = lambda r: pl.BlockSpec((r, tc), lambda j, i: (0, j))
    return pl.pallas_call(
        body, name=name, grid=(nc, nt),
        in_specs=[pl.BlockSpec((tm, d), lambda j, i: (nt - 1 - i, 0)),
                  pl.BlockSpec((None, tc, d), lambda j, i: (l, j, 0)), pair, pair, vec(0), vec(nc)] + sspecs,
        out_specs=[pair, acc_spec(3), acc_spec(3), acc_spec(1), acc_spec(1)] + sspecs,
        out_shape=[jax.ShapeDtypeStruct((2, t, f), BF16), jax.ShapeDtypeStruct((3, f), F32),
                   jax.ShapeDtypeStruct((3, f), F32), jax.ShapeDtypeStruct((1, f), F32),
                   jax.ShapeDtypeStruct((1, f), F32)] + sshapes,
        scratch_shapes=[pltpu.VMEM((tm, tc), F32), pltpu.VMEM((2, SUB, tc), F32)] + ssems,
        compiler_params=_cp("arbitrary", "arbitrary"))(dzb, wd3, h3, pre3, cw, cw, *srcs)


def _lru_gates(xr, wg_ref, bg_ref):
    rs, gs = [], []
    for hd in range(LRU_HEADS):
        xh = xr[:, hd * LRU_BLOCK:(hd + 1) * LRU_BLOCK].astype(BF16)
        gt = jnp.dot(xh, wg_ref[hd], preferred_element_type=F32) + _row(bg_ref, hd)
        rs.append(gt[:, :LRU_BLOCK])
        gs.append(gt[:, LRU_BLOCK:])
    return jnp.concatenate(rs, axis=1), jnp.concatenate(gs, axis=1)


def _lru_coeffs(xr, wg_ref, bg_ref, lam_ref):
    gr, gi = _lru_gates(xr, wg_ref, bg_ref)
    r = _sigmoid(gr)
    ig = _sigmoid(gi)
    sp = _softplus(-lam_ref[...])
    log_a = -LRU_C * r * sp
    a = jnp.exp(log_a)
    mult = jnp.sqrt(_one_minus_exp(2.0 * log_a))
    return r, ig, sp, a, mult


def _lru_fwd(h, cw, cb, wg, bg, lam, seq, name):
    t, r2 = h.shape
    rw = r2 // 2
    ts = min(TS_LRU, seq)
    n8 = ts // SUB

    def body(hg_ref, hr_ref, cw_ref, cb_ref, wg_ref, bg_ref, lam_ref, hs_ref, y_ref,
             a_s, b_s, cconv, cstate):
        i = pl.program_id(0)

        @pl.when(lax.rem(i * ts, seq) == 0)
        def _():
            cconv[...] = jnp.zeros_like(cconv)
            cstate[...] = jnp.zeros_like(cstate)

        rin = hr_ref[...].astype(F32)
        ext = jnp.concatenate([cconv[...], rin], axis=0)
        xr = cb_ref[...]
        for k in range(4):
            xr = xr + _row(cw_ref, k) * _down(ext, 3 - k, SUB)
        cconv[...] = rin[ts - SUB:, :]
        _, ig, _, a, mult = _lru_coeffs(xr, wg_ref, bg_ref, lam_ref)
        a_s[...] = a
        b_s[...] = mult * (ig * xr)
        row = lax.broadcasted_iota(jnp.int32, (SUB, rw), 0)

        def step(j, carry):
            off = pl.multiple_of(j * SUB, SUB)
            a8 = a_s[pl.ds(off, SUB), :]
            b8 = b_s[pl.ds(off, SUB), :]
            for d in (1, 2, 4):
                m = row >= d
                b8 = jnp.where(m, a8 * pltpu.roll(b8, d, 0) + b8, b8)
                a8 = jnp.where(m, a8 * pltpu.roll(a8, d, 0), a8)
            h8 = a8 * carry + b8
            hs_ref[pl.ds(off, SUB), :] = h8
            return _colsum(jnp.where(row == SUB - 1, h8, 0.0))

        cstate[...] = lax.fori_loop(0, n8, step, cstate[...])
        gel, _ = _gelu_and_grad(hg_ref[...].astype(F32))
        y_ref[...] = (hs_ref[...] * gel).astype(BF16)

    full = lambda shp: pl.BlockSpec(shp, lambda i: (0,) * len(shp))
    return pl.pallas_call(
        body, name=name, grid=(t // ts,),
        in_specs=[pl.BlockSpec((ts, rw), lambda i: (i, 0)), pl.BlockSpec((ts, rw), lambda i: (i, 1)),
                  full((4, rw)), full((1, rw)), full(wg.shape), full(bg.shape), full((1, rw))],
        out_specs=[pl.BlockSpec((ts, rw), lambda i: (i, 0)), pl.BlockSpec((ts, rw), lambda i: (i, 0))],
        out_shape=[jax.ShapeDtypeStruct((t, rw), F32), jax.ShapeDtypeStruct((t, rw), BF16)],
        scratch_shapes=[pltpu.VMEM((ts, rw), F32), pltpu.VMEM((ts, rw), F32),
                        pltpu.VMEM((SUB, rw), F32), pltpu.VMEM((1, rw), F32)],
        compiler_params=_cp("arbitrary"))(h, h, cw, cb, wg, bg, lam)


def _lru_bwd(h, hs, dy, cw, cb, wg, bg, lam, seq, name):
    t, r2 = h.shape
    rw = r2 // 2
    ts = min(TS_LRU, seq)
    nt = t // ts
    n8 = ts // SUB
    hp16 = ts // SUB16
    hp8 = ts // SUB

    def body(hg_ref, hr_ref, hrh_ref, hs_ref, hsh_ref, dy_ref, cw_ref, cb_ref, wg_ref, bg_ref, lam_ref,
             dh_ref, dbin_ref, dcw_ref, dcb_ref, dwg_ref, dbg_ref, dlam_ref,
             a_s, g_s, l_s, c_lam, c_a, c_dxr):
        i = pl.program_id(0)
        ri = nt - 1 - i

        @pl.when(i == 0)
        def _():
            for r in (dbin_ref, dcw_ref, dcb_ref, dwg_ref, dbg_ref, dlam_ref):
                r[...] = jnp.zeros_like(r)

        @pl.when(lax.rem((ri + 1) * ts, seq) == 0)
        def _():
            c_lam[...] = jnp.zeros_like(c_lam)
            c_a[...] = jnp.zeros_like(c_a)
            c_dxr[...] = jnp.zeros_like(c_dxr)

        keep = jnp.where(lax.rem(ri * ts, seq) == 0, 0.0, 1.0)
        rin = hr_ref[...].astype(F32)
        ext = jnp.concatenate([hrh_ref[...].astype(F32) * keep, rin], axis=0)
        shifted = [_down(ext, 3 - k, SUB16) for k in range(4)]
        xr = cb_ref[...]
        for k in range(4):
            xr = xr + _row(cw_ref, k) * shifted[k]
        r, ig, sp, a, mult = _lru_coeffs(xr, wg_ref, bg_ref, lam_ref)
        gel, dgel = _gelu_and_grad(hg_ref[...].astype(F32))
        dyf = dy_ref[...].astype(F32)
        hsv = hs_ref[...]
        dg = dyf * hsv * dgel

        a_s[...] = _up(jnp.concatenate([a, c_a[...]], axis=0), 1, ts)
        g_s[...] = dyf * gel
        c_a[...] = a[0:SUB, :]
        row = lax.broadcasted_iota(jnp.int32, (SUB, rw), 0)

        def step(j, carry):
            off = pl.multiple_of((n8 - 1 - j) * SUB, SUB)
            a8 = a_s[pl.ds(off, SUB), :]
            b8 = g_s[pl.ds(off, SUB), :]
            for d in (1, 2, 4):
                m = row < SUB - d
                b8 = jnp.where(m, a8 * pltpu.roll(b8, SUB - d, 0) + b8, b8)
                a8 = jnp.where(m, a8 * pltpu.roll(a8, SUB - d, 0), a8)
            l8 = a8 * carry + b8
            l_s[pl.ds(off, SUB), :] = l8
            return _colsum(jnp.where(row == 0, l8, 0.0))

        c_lam[...] = lax.fori_loop(0, n8, step, c_lam[...])
        lamv = l_s[...]
        hs_prev = _down(jnp.concatenate([hsh_ref[...] * keep, hsv], axis=0), 1, SUB)
        da = lamv * hs_prev
        t1 = lamv * xr
        dmult = t1 * ig
        dig = t1 * mult
        dxr = lamv * mult * ig
        dla = da * a - dmult * (a * a) / mult
        dr = dla * (-LRU_C * sp)
        dlam_ref[...] += _colsum(dla * (-LRU_C) * r) * (-1.0 / (1.0 + jnp.exp(lam_ref[...])))
        dgr = dr * r * (1.0 - r)
        dgi = dig * ig * (1.0 - ig)
        parts = []
        for hd in range(LRU_HEADS):
            sl = slice(hd * LRU_BLOCK, (hd + 1) * LRU_BLOCK)
            dgt = jnp.concatenate([dgr[:, sl], dgi[:, sl]], axis=1)
            dbg_ref[hd:hd + 1, :] += _colsum(dgt)
            dgt16 = dgt.astype(BF16)
            parts.append(lax.dot_general(dgt16, wg_ref[hd], (((1,), (1,)), ((), ())),
                                         preferred_element_type=F32))
            dwg_ref[hd] += lax.dot_general(xr[:, sl].astype(BF16), dgt16, (((0,), (0,)), ((), ())),
                                           preferred_element_type=F32)
        dxr = dxr + jnp.concatenate(parts, axis=1)

        dcb_ref[...] += _colsum(dxr)
        for k in range(4):
            dcw_ref[k:k + 1, :] += _colsum(dxr * shifted[k])
        ext2 = jnp.concatenate([dxr, c_dxr[...]], axis=0)
        drb = _row(cw_ref, 3) * dxr
        for k in range(3):
            drb = drb + _row(cw_ref, k) * _up(ext2, 3 - k, ts)
        c_dxr[...] = dxr[0:SUB, :]
        dh_ref[0] = dg.astype(BF16)
        dh_ref[1] = drb.astype(BF16)
        dbin_ref[:, 0:rw] += _colsum(dg)
        dbin_ref[:, rw:] += _colsum(drb)

    rev = lambda c: pl.BlockSpec((ts, rw), lambda i: (nt - 1 - i, c))
    full = lambda shp: pl.BlockSpec(shp, lambda i: (0,) * len(shp))
    nh = LRU_HEADS
    return pl.pallas_call(
        body, name=name, grid=(nt,),
        in_specs=[rev(0), rev(1),
                  pl.BlockSpec((SUB16, rw), lambda i: (jnp.maximum((nt - 1 - i) * hp16 - 1, 0), 1)),
                  rev(0),
                  pl.BlockSpec((SUB, rw), lambda i: (jnp.maximum((nt - 1 - i) * hp8 - 1, 0), 0)),
                  rev(0), full((4, rw)), full((1, rw)), full(wg.shape), full(bg.shape), full((1, rw))],
        out_specs=[pl.BlockSpec((2, ts, rw), lambda i: (0, nt - 1 - i, 0)), full((1, r2)), full((4, rw)),
                   full((1, rw)), full((nh, LRU_BLOCK, 2 * LRU_BLOCK)), full((nh, 2 * LRU_BLOCK)), full((1, rw))],
        out_shape=[jax.ShapeDtypeStruct((2, t, rw), BF16), jax.ShapeDtypeStruct((1, r2), F32),
                   jax.ShapeDtypeStruct((4, rw), F32), jax.ShapeDtypeStruct((1, rw), F32),
                   jax.ShapeDtypeStruct((nh, LRU_BLOCK, 2 * LRU_BLOCK), F32),
                   jax.ShapeDtypeStruct((nh, 2 * LRU_BLOCK), F32), jax.ShapeDtypeStruct((1, rw), F32)],
        scratch_shapes=[pltpu.VMEM((ts, rw), F32), pltpu.VMEM((ts, rw), F32), pltpu.VMEM((ts, rw), F32),
                        pltpu.VMEM((1, rw), F32), pltpu.VMEM((SUB, rw), F32), pltpu.VMEM((SUB, rw), F32)],
        compiler_params=_cp("arbitrary"))(h, h, h, hs, hs, dy, cw, cb, wg, bg, lam)


def _row_tile(rows, cols, mult, elems=ELEMS_PER_BLOCK):
    cap = max(mult, elems // cols)
    best = None
    for cand in range(mult, min(rows, cap) + 1, mult):
        if rows % cand == 0:
            best = cand
    return best if best is not None else rows


def _core_index():
    return lax.axis_index("c").astype(jnp.int32).reshape(1)


def _chip_index():
    return (2 * lax.axis_index("x") + lax.axis_index("y")).astype(jnp.int32).reshape(1)


def _add_pair(p4, r3, name):
    s, _, rows, cols = p4.shape
    tr = _row_tile(rows, cols, SUB16)

    def body(c_ref, a_ref, b_ref, o_ref):
        o_ref[...] = (a_ref[...].astype(F32) + b_ref[...].astype(F32)).astype(o_ref.dtype)

    blk = pl.BlockSpec((None, tr, cols), lambda k, i, c_ref: (k, i, 0))
    return pl.pallas_call(
        body, name=name,
        grid_spec=pltpu.PrefetchScalarGridSpec(
            num_scalar_prefetch=1, grid=(s, rows // tr),
            in_specs=[pl.BlockSpec((None, None, tr, cols), lambda k, i, c_ref: (k, c_ref[0], i, 0)), blk],
            out_specs=blk),
        out_shape=jax.ShapeDtypeStruct((s, rows, cols), p4.dtype),
        compiler_params=_cp("parallel", "parallel"))(_core_index(), p4, r3)


def _add_chips(r, name):
    shape = r.shape[1:]
    r3 = r.reshape(N_CHIPS, -1, shape[-1])
    _, rows, cols = r3.shape
    tr = _row_tile(rows, cols, SUB16)

    def body(r_ref, o_ref):
        s = r_ref[0].astype(F32) + r_ref[1].astype(F32)
        s = s + r_ref[2].astype(F32)
        o_ref[...] = s + r_ref[3].astype(F32)

    out = pl.pallas_call(body, name=name, grid=(rows // tr,),
                         in_specs=[pl.BlockSpec((N_CHIPS, tr, cols), lambda i: (0, i, 0))],
                         out_specs=pl.BlockSpec((tr, cols), lambda i: (i, 0)),
                         out_shape=jax.ShapeDtypeStruct((rows, cols), F32),
                         compiler_params=_cp("parallel"))(r3)
    return out.reshape(shape)


def _adamw(w3, g_mine, g_sib, m3, v3, name):
    nl, r, cols = w3.shape
    rows = r // 2
    flat = [arr.reshape(nl, 2, rows, cols) for arr in (w3, m3, v3)]
    tr = _row_tile(rows, cols, SUB)

    def body(c_ref, w_ref, gm_ref, gs_ref, m_ref, v_ref, g_ref, d_ref, mo_ref, vo_ref):
        gg = jnp.where(pl.program_id(1) == c_ref[0], gm_ref[...], gs_ref[...])
        m2 = ADAM_B1 * m_ref[...] + (1.0 - ADAM_B1) * gg
        v2 = ADAM_B2 * v_ref[...] + (1.0 - ADAM_B2) * (gg * gg)
        m_hat = m2 / (1.0 - ADAM_B1 ** ADAM_STEP)
        v_hat = v2 / (1.0 - ADAM_B2 ** ADAM_STEP)
        g_ref[...] = gg
        d_ref[...] = -ADAM_LR * (m_hat / (jnp.sqrt(v_hat) + ADAM_EPS) + ADAM_WD * w_ref[...])
        mo_ref[...] = m2
        vo_ref[...] = v2

    blk = pl.BlockSpec((None, None, tr, cols), lambda l, hh, i, c_ref: (l, hh, i, 0))
    gblk = pl.BlockSpec((None, tr, cols), lambda l, hh, i, c_ref: (l, i, 0))
    outs = pl.pallas_call(
        body, name=name,
        grid_spec=pltpu.PrefetchScalarGridSpec(
            num_scalar_prefetch=1, grid=(nl, 2, rows // tr),
            in_specs=[blk, gblk, gblk, blk, blk], out_specs=[blk] * 4),
        out_shape=[jax.ShapeDtypeStruct((nl, 2, rows, cols), F32)] * 4,
        compiler_params=_cp("parallel", "parallel", "parallel"))(_core_index(), flat[0], g_mine, g_sib, flat[1],
                                                                 flat[2])
    return tuple(o.reshape(nl, r, cols) for o in outs)


def _place(src3, layer, kind, dtype, name):
    _, r, c = src3.shape
    tr = _row_tile(r, c, SUB16)
    in_spec = pl.BlockSpec((None, tr, c), lambda i, my_ref: (layer, i, 0))
    if kind == "col":
        out_spec = pl.BlockSpec((tr, c), lambda i, my_ref: (i, my_ref[0]))
        out_shape = (r, N_CHIPS * c)
    else:
        out_spec = pl.BlockSpec((None, tr, c), lambda i, my_ref: (my_ref[0], i, 0))
        out_shape = (N_CHIPS, r, c)

    def body(my_ref, i_ref, o_ref):
        o_ref[...] = i_ref[...].astype(o_ref.dtype)

    return pl.pallas_call(
        body, name=name,
        grid_spec=pltpu.PrefetchScalarGridSpec(num_scalar_prefetch=1, grid=(r // tr,), in_specs=[in_spec],
                                               out_specs=out_spec),
        out_shape=jax.ShapeDtypeStruct(out_shape, dtype),
        compiler_params=_cp("parallel"))(_chip_index(), src3)


def _half(ref, c, h):
    return ref.at[pl.ds(c * h, h)]


def _position():
    x = lax.axis_index("x")
    y = lax.axis_index("y")
    c = lax.axis_index("c")
    return x, y, c


def _peer_chip(x, y, j):
    tx = 1 - x if j & 2 else x
    ty = 1 - y if j & 1 else y
    return tx, ty


def _remote(src, dst, ssem, rsem, dev):
    return pltpu.make_async_remote_copy(src_ref=src, dst_ref=dst, send_sem=ssem, recv_sem=rsem,
                                        device_id=dev, device_id_type=pl.DeviceIdType.MESH)


_ANY = pl.BlockSpec(memory_space=pl.ANY)


def _unit_view(kind, ref, k):
    if kind == "col":
        n = ref.shape[1] // N_CHIPS
        return ref.at[:, pl.ds(pl.multiple_of(k * n, LANE), n)]
    return ref.at[k]


def _all_gather(placed, kinds):
    nt = len(placed)

    def body(*refs):
        outs = refs[nt:2 * nt]
        ssem, rsem = refs[2 * nt:]
        x, y, c = _position()
        my = 2 * x + y
        sib = (x, y, 1 - c)

        def part(t, k, core):
            view = _unit_view(kinds[t], outs[t], k)
            h = view.shape[0] // 2
            return _half(view, core, h)

        sends, fwds = [], []
        for t in range(nt):
            own = part(t, my, c)
            for j in (1, 2, 3):
                tx, ty = _peer_chip(x, y, j)
                cp = _remote(own, own, ssem.at[6 * t + j - 1], rsem.at[6 * t + j - 1], (tx, ty, c))
                cp.start()
                sends.append(cp)
        for t in range(nt):
            for j in (1, 2, 3):
                tx, ty = _peer_chip(x, y, j)
                got = part(t, 2 * tx + ty, c)
                _remote(got, got, ssem.at[6 * t + j - 1], rsem.at[6 * t + j - 1], sib).wait_recv()
                cp = _remote(got, got, ssem.at[6 * t + 2 + j], rsem.at[6 * t + 2 + j], sib)
                cp.start()
                fwds.append(cp)
        for t in range(nt):
            for j in (1, 2, 3):
                tx, ty = _peer_chip(x, y, j)
                other = part(t, 2 * tx + ty, 1 - c)
                _remote(other, other, ssem.at[6 * t + 2 + j], rsem.at[6 * t + 2 + j], sib).wait_recv()
        for cp in sends + fwds:
            cp.wait_send()

    return pl.pallas_call(
        body, name="all_gather", in_specs=[_ANY] * nt, out_specs=[_ANY] * nt,
        out_shape=[jax.ShapeDtypeStruct(p.shape, p.dtype) for p in placed],
        input_output_aliases={t: t for t in range(nt)},
        scratch_shapes=[pltpu.SemaphoreType.DMA((6 * nt,)), pltpu.SemaphoreType.DMA((6 * nt,))],
    )(*placed)


def _rider_start(refs, kinds, ssem, rsem):
    x, y, c = _position()
    my = 2 * x + y
    for u, (ref, kind) in enumerate(zip(refs, kinds)):
        own = _unit_view(kind, ref, my)
        for j in (1, 2, 3):
            tx, ty = _peer_chip(x, y, j)
            _remote(own, own, ssem.at[3 * u + j - 1], rsem.at[3 * u + j - 1], (tx, ty, c)).start()


def _rider_wait(refs, kinds, ssem, rsem):
    x, y, c = _position()
    for u, (ref, kind) in enumerate(zip(refs, kinds)):
        for j in (1, 2, 3):
            tx, ty = _peer_chip(x, y, j)
            got = _unit_view(kind, ref, 2 * tx + ty)
            _remote(got, got, ssem.at[3 * u + j - 1], rsem.at[3 * u + j - 1], (tx, ty, c)).wait()


def _rider_args(rider):
    bufs = [b for b, _ in rider]
    kinds = [k for _, k in rider]
    n = len(bufs)
    sems = [pltpu.SemaphoreType.DMA((3 * n,)), pltpu.SemaphoreType.DMA((3 * n,))] if n else []
    return bufs, kinds, [_ANY] * n, [jax.ShapeDtypeStruct(b.shape, b.dtype) for b in bufs], sems


def _d2d_stream(src4, other_half, name):
    s, _, rows, cols = src4.shape
    tr = _row_tile(rows, cols, SUB16, STREAM_ELEMS_PER_BLOCK)
    nblk = rows // tr

    nh = src4.shape[1]

    def body(c_ref, src_ref, dst_ref, ssem, rsem):
        k = pl.program_id(0)
        i = pl.program_id(1)
        x, y, c = _position()
        sib = (x, y, 1 - c)
        blk = dst_ref.at[pl.ds(pl.multiple_of((k * nblk + i) * tr, SUB16), tr)]
        cp = _remote(src_ref, blk, ssem, rsem, sib)
        cp.start()
        cp.wait_send()

        @pl.when(jnp.logical_and(k == s - 1, i == nblk - 1))
        def _():
            _remote(dst_ref, dst_ref, ssem, rsem, sib).wait_recv()

    if other_half:
        src_map = lambda k, i, c_ref: ((k * nh + 1 - c_ref[0]) * nblk + i, 0)
    else:
        src_map = lambda k, i, c_ref: (k * nh * nblk + i, 0)
    out = pl.pallas_call(
        body, name=name,
        grid_spec=pltpu.PrefetchScalarGridSpec(
            num_scalar_prefetch=1, grid=(s, nblk),
            in_specs=[pl.BlockSpec((tr, cols), src_map)], out_specs=_ANY,
            scratch_shapes=[pltpu.SemaphoreType.DMA, pltpu.SemaphoreType.DMA]),
        out_shape=jax.ShapeDtypeStruct((s * rows, cols), src4.dtype),
        compiler_params=_cp("arbitrary", "arbitrary"))(_core_index(), src4.reshape(s * nh * rows, cols))
    return out.reshape(s, rows, cols)


def _scatter_copies(srcs, lands, kinds, ssem, rsem, lsem):
    x, y, c = _position()
    my = 2 * x + y
    cps = []
    for u, (src, land, kind) in enumerate(zip(srcs, lands, kinds)):
        cps.append(pltpu.make_async_copy(_unit_view(kind, src, my), land.at[my], lsem.at[u]))
        for j in (1, 2, 3):
            tx, ty = _peer_chip(x, y, j)
            cps.append(_remote(_unit_view(kind, src, 2 * tx + ty), land.at[my], ssem.at[3 * u + j - 1],
                               rsem.at[3 * u + j - 1], (tx, ty, c)))
    return cps


def _scatter_args(scatter):
    srcs = [s for s, _ in scatter]
    kinds = [k for _, k in scatter]
    n = len(srcs)
    shapes = [jax.ShapeDtypeStruct((N_CHIPS, s.shape[0], s.shape[1] // N_CHIPS) if k == "col" else s.shape, s.dtype)
              for s, k in scatter]
    sems = [pltpu.SemaphoreType.DMA((3 * n,)), pltpu.SemaphoreType.DMA((3 * n,)),
            pltpu.SemaphoreType.DMA((n,))] if n else []
    return srcs, kinds, [_ANY] * n, shapes, sems


def _rs_scatter(scatter):
    srcs, kinds, specs, shapes, sems = _scatter_args(scatter)
    n = len(srcs)

    def body(*refs):
        cps = _scatter_copies(refs[:n], refs[n:2 * n], kinds, *refs[2 * n:])
        for cp in cps:
            cp.start()
        for cp in cps:
            cp.wait()

    return pl.pallas_call(body, name="rs_scatter", in_specs=specs, out_specs=specs, out_shape=shapes,
                          scratch_shapes=sems)(*srcs)


SMALL = (("sc_conv_w", True), ("sc_conv_b", False), ("lru_b_in", True), ("lru_conv_w", True),
         ("lru_conv_b", True), ("lru_b_gate", True), ("lru_lambda", True), ("ffn_conv_w", True),
         ("ffn_conv_b", False), ("ln_g", True), ("ln_b", True))
PACK_ROW_MULT = 2 * SUB16


def _pack_rows(shapes):
    n = sum(math.prod(shapes[name]) for name, _ in SMALL)
    rows = -(-n // 128)
    return -(-rows // PACK_ROW_MULT) * PACK_ROW_MULT


def _pack_local(vals, shapes):
    flat = jnp.concatenate([vals[name].reshape(-1) for name, _ in SMALL])
    rows = _pack_rows(shapes)
    return jnp.pad(flat, (0, rows * 128 - flat.shape[0])).reshape(rows, 128)


def _unpack_local(pack, shapes):
    flat = pack.reshape(-1)
    out, off = {}, 0
    for name, _ in SMALL:
        n = math.prod(shapes[name])
        out[name] = flat[off:off + n].reshape(shapes[name])
        off += n
    return out


def _pack_slots(fulls, shapes):
    parts = []
    for name, sharded in SMALL:
        v = fulls[name]
        if sharded:
            ns = shapes[name][-1]
            v = jnp.moveaxis(v.reshape(v.shape[:-1] + (N_CHIPS, ns)), -2, 0).reshape(N_CHIPS, -1)
        else:
            v = jnp.broadcast_to(v.reshape(1, -1), (N_CHIPS, v.size))
        parts.append(v)
    flat = jnp.concatenate(parts, axis=1)
    rows = _pack_rows(shapes)
    return jnp.pad(flat, ((0, 0), (0, rows * 128 - flat.shape[1]))).reshape(N_CHIPS, rows, 128)


def _unpack_slots(packs, shapes):
    flat = packs.reshape(N_CHIPS, -1)
    out, off = {}, 0
    for name, sharded in SMALL:
        n = math.prod(shapes[name])
        if sharded:
            seg = flat[:, off:off + n].reshape((N_CHIPS,) + tuple(shapes[name]))
            seg = jnp.moveaxis(seg, 0, -2)
            out[name] = seg.reshape(seg.shape[:-2] + (N_CHIPS * shapes[name][-1],))
        off += n
    return out


WEIGHTS = ("sc_w_in", "sc_conv_w", "sc_conv_b", "sc_w_out", "lru_w_in", "lru_b_in", "lru_conv_w", "lru_conv_b",
           "lru_w_gate", "lru_b_gate", "lru_lambda", "lru_w_out", "ffn_w_up", "ffn_conv_w", "ffn_conv_b",
           "ffn_w_down", "ln_g", "ln_b")
GATHER_KIND = {"sc_w_in": "col", "sc_w_out": "lead", "lru_w_in": "col", "lru_w_out": "lead", "ffn_w_up": "col",
               "ffn_w_down": "lead"}


def kernel(x, sc_w_in, sc_conv_w, sc_conv_b, sc_w_out, lru_w_in, lru_b_in, lru_conv_w, lru_conv_b, lru_w_gate, lru_b_gate, lru_lambda, lru_w_out, ffn_w_up, ffn_conv_w, ffn_conv_b, ffn_w_down, ln_g, ln_b, loss_target, m_sc_w_in, m_sc_conv_w, m_sc_conv_b, m_sc_w_out, m_lru_w_in, m_lru_b_in, m_lru_conv_w, m_lru_conv_b, m_lru_w_gate, m_lru_b_gate, m_lru_lambda, m_lru_w_out, m_ffn_w_up, m_ffn_conv_w, m_ffn_conv_b, m_ffn_w_down, m_ln_g, m_ln_b, v_sc_w_in, v_sc_conv_w, v_sc_conv_b, v_sc_w_out, v_lru_w_in, v_lru_b_in, v_lru_conv_w, v_lru_conv_b, v_lru_w_gate, v_lru_b_gate, v_lru_lambda, v_lru_w_out, v_ffn_w_up, v_ffn_conv_w, v_ffn_conv_b, v_ffn_w_down, v_ln_g, v_ln_b):
    w = dict(zip(WEIGHTS, (sc_w_in, sc_conv_w, sc_conv_b, sc_w_out, lru_w_in, lru_b_in, lru_conv_w, lru_conv_b,
                           lru_w_gate, lru_b_gate, lru_lambda, lru_w_out, ffn_w_up, ffn_conv_w, ffn_conv_b,
                           ffn_w_down, ln_g, ln_b)))
    mom = dict(zip(WEIGHTS, (m_sc_w_in, m_sc_conv_w, m_sc_conv_b, m_sc_w_out, m_lru_w_in, m_lru_b_in, m_lru_conv_w,
                             m_lru_conv_b, m_lru_w_gate, m_lru_b_gate, m_lru_lambda, m_lru_w_out, m_ffn_w_up,
                             m_ffn_conv_w, m_ffn_conv_b, m_ffn_w_down, m_ln_g, m_ln_b)))
    vel = dict(zip(WEIGHTS, (v_sc_w_in, v_sc_conv_w, v_sc_conv_b, v_sc_w_out, v_lru_w_in, v_lru_b_in, v_lru_conv_w,
                             v_lru_conv_b, v_lru_w_gate, v_lru_b_gate, v_lru_lambda, v_lru_w_out, v_ffn_w_up,
                             v_ffn_conv_w, v_ffn_conv_b, v_ffn_w_down, v_ln_g, v_ln_b)))
    bd, seq, d = x.shape
    t = bd * seq
    small_shapes = {name: w[name].shape for name, _ in SMALL}

    w_pack = _pack_local(w, small_shapes)
    gate_shape = w["lru_w_gate"].shape
    bufs = {(n, l): (_place(w[n], l, k, BF16, "place_w"), k)
            for n, k in GATHER_KIND.items() for l in range(w[n].shape[0])}
    bufs["gate"] = (_place(w["lru_w_gate"].reshape(1, -1, gate_shape[-1]), 0, "lead", BF16, "place_w"), "lead")
    bufs["pack"] = (_place(w_pack[None], 0, "lead", F32, "place_w"), "lead")

    def layer_keys(i):
        mixer = ("sc_w_in", "sc_w_out") if i % 2 == 0 else ("lru_w_in", "lru_w_out")
        return [(mixer[0], i // 2), (mixer[1], i // 2)], [("ffn_w_up", i), ("ffn_w_down", i)]

    def gathered(keys, arrays):
        for key, arr in zip(keys, arrays):
            bufs[key] = (arr, bufs[key][1])

    def wt(name, l):
        arr = bufs[(name, l)][0]
        return arr.reshape(1, -1, arr.shape[-1])

    first = layer_keys(0)[0] + layer_keys(0)[1] + ["gate", "pack"]
    gathered(first, _all_gather([bufs[k][0] for k in first], [bufs[k][1] for k in first]))
    full = _unpack_slots(bufs["pack"][0], small_shapes)
    full["sc_conv_b"] = sc_conv_b
    full["ffn_conv_b"] = ffn_conv_b
    wg_full = jnp.moveaxis(bufs["gate"][0].reshape((N_CHIPS,) + gate_shape), 0, -2)
    wg_full = wg_full.reshape(wg_full.shape[:-2] + (2 * LRU_BLOCK,))
    f = N_CHIPS * w["ffn_w_down"].shape[1]
    rw = N_CHIPS * w["lru_w_out"].shape[1]

    x0 = x.reshape(t, d)
    xb = x0.astype(BF16)
    cur, cur_b = x0, xb
    saved = []

    for i in range(DEPTH):
        j = i // 2
        s = {"xb": cur_b}
        mixer_next, ffn_next = layer_keys(i + 1) if i + 1 < DEPTH else ([], [])
        behind_out, behind_up, behind_down = mixer_next[1:], mixer_next[:1] + ffn_next[:1], ffn_next[1:]
        if i % 2 == 0:
            h = _mm_nn(cur_b, wt("sc_w_in", j), 0, None, 3 * d, "sc_in")
            q = _sc_fwd(h, full["sc_conv_w"][j], full["sc_conv_b"][j][None], seq, "sc_fwd")
            z1, x1, x1b, *arrived = _mm_nn_ln(q, wt("sc_w_out", j), 0, cur, full["ln_g"][i, 0][None],
                                              full["ln_b"][i, 0][None], "sc_out_ln",
                                              rider=[bufs[k] for k in behind_out])
        else:
            h = _mm_nn(cur_b, wt("lru_w_in", j), 0, full["lru_b_in"][j][None], 2 * rw, "lru_in")
            hs, q = _lru_fwd(h, full["lru_conv_w"][j], full["lru_conv_b"][j][None], wg_full[j],
                             full["lru_b_gate"][j], full["lru_lambda"][j][None], seq, "lru_fwd")
            s["hs"] = hs
            z1, x1, x1b, *arrived = _mm_nn_ln(q, wt("lru_w_out", j), 0, cur, full["ln_g"][i, 0][None],
                                              full["ln_b"][i, 0][None], "lru_out_ln",
                                              rider=[bufs[k] for k in behind_out])
        gathered(behind_out, arrived)
        s.update(h=h, q=q, z1=z1, x1b=x1b)
        h3, pre3, act, *arrived = _ffn_up(x1b, wt("ffn_w_up", i), 0, full["ffn_conv_w"][i],
                                          full["ffn_conv_b"][i][None], seq, "ffn_up",
                                          rider=[bufs[k] for k in behind_up])
        gathered(behind_up, arrived)
        z2, x2, x2b, *arrived = _mm_nn_ln(act, wt("ffn_w_down", i), 0, x1, full["ln_g"][i, 1][None],
                                          full["ln_b"][i, 1][None], "ffn_down_ln",
                                          rider=[bufs[k] for k in behind_down])
        gathered(behind_down, arrived)
        s.update(h3=h3, pre3=pre3, act=act, z2=z2)
        saved.append(s)
        cur, cur_b = x2, x2b

    dcur, loss = cur, None

    def pair_sum(p, kind):
        lead = kind == "lead"
        p4 = p.reshape(N_CHIPS if lead else 1, 2, -1, p.shape[-1])
        chip_sum = _add_pair(p4, _d2d_stream(p4, True, "rs_swap"), "rs_add_pair")
        return chip_sum if lead else chip_sum[0]

    gp = {n: [None] * w[n].shape[0] for n, _ in SMALL}
    gp["lru_w_gate"] = [None] * gate_shape[0]
    landed, pending = {}, []
    for i in reversed(range(DEPTH)):
        j = i // 2
        s = saved[i]
        mixer_keys, ffn_keys = layer_keys(i)
        if i == DEPTH - 1:
            dz2, dz2b, dg, db, loss_parts = _ln_bwd(dcur, s["z2"], full["ln_g"][i, 1][None], "ln_bwd_loss",
                                                    target=loss_target.reshape(t, d))
            loss = lax.psum(jnp.sum(loss_parts), MESH_AXES)
        else:
            dz2, dz2b, dg, db = _ln_bwd(dcur, s["z2"], full["ln_g"][i, 1][None], "ln_bwd")
        gp["ln_g"][i] = [None, dg[0]]
        gp["ln_b"][i] = [None, db[0]]
        p_down = _mm_tn(s["act"], dz2b[None], f // 2, d, "ffn_down_dw").reshape(N_CHIPS, -1, d)
        dh3, dcwg, dcwv, dcbg, dcbv, *lands = _ffn_down_bwd(
            dz2b, wt("ffn_w_down", i), 0, s["h3"], s["pre3"], full["ffn_conv_w"][i], seq, "ffn_down_bwd",
            scatter=[(chip_sum, kind) for _, chip_sum, kind in pending])
        landed.update({key: land for (key, _, _), land in zip(pending, lands)})
        gp["ffn_conv_w"][i] = jnp.concatenate([dcwg, dcwv], axis=1)
        gp["ffn_conv_b"][i] = jnp.concatenate([dcbg[0], dcbv[0]])
        dx1 = _mm_nt_res(dh3, wt("ffn_w_up", i), 0, dz2, f // 2, "ffn_up_dx")
        p_up = _mm_tn(s["x1b"], dh3, d, f // 2, "ffn_up_dw")
        ffn_partials = ((ffn_keys[0], p_up, "col"), (ffn_keys[1], p_down, "lead"))
        early = [(key, pair_sum(p, kind), kind) for key, p, kind in ffn_partials] if i == 0 else []
        early_scatter = [(chip_sum, kind) for _, chip_sum, kind in early]
        dz1, dz1b, dg, db = _ln_bwd(dx1, s["z1"], full["ln_g"][i, 0][None], "ln_bwd")
        gp["ln_g"][i][0] = dg[0]
        gp["ln_b"][i][0] = db[0]
        gp["ln_g"][i] = jnp.stack(gp["ln_g"][i])
        gp["ln_b"][i] = jnp.stack(gp["ln_b"][i])
        if i % 2 == 0:
            dq = _mm_nt(dz1b, wt("sc_w_out", j), 0, d, "sc_out_dx")
            p_out = _mm_tn(s["q"], dz1b[None], d, d, "sc_out_dw")
            dh3, dcw, dcb = _sc_bwd(s["h"], dq, full["sc_conv_w"][j], full["sc_conv_b"][j][None], seq, "sc_bwd")
            gp["sc_conv_w"][j] = dcw
            gp["sc_conv_b"][j] = dcb[0]
            res = _mm_nt_res(dh3, wt("sc_w_in", j), 0, dz1, d, "sc_in_dx", scatter=early_scatter)
            dcur, lands = (res[0], res[1:]) if early else (res, [])
            landed.update({key: land for (key, _, _), land in zip(early, lands)})
            p_in = _mm_tn(s["xb"], dh3, d, d, "sc_in_dw")
        else:
            dq = _mm_nt(dz1b, wt("lru_w_out", j), 0, rw, "lru_out_dx")
            p_out = _mm_tn(s["q"], dz1b[None], rw, d, "lru_out_dw")
            dh3, dbin, dcw, dcb, dwg, dbg, dlam = _lru_bwd(
                s["h"], s["hs"], dq, full["lru_conv_w"][j], full["lru_conv_b"][j][None], wg_full[j],
                full["lru_b_gate"][j], full["lru_lambda"][j][None], seq, "lru_bwd")
            gp["lru_b_in"][j] = dbin[0]
            gp["lru_conv_w"][j] = dcw
            gp["lru_conv_b"][j] = dcb[0]
            gp["lru_w_gate"][j] = dwg
            gp["lru_b_gate"][j] = dbg
            gp["lru_lambda"][j] = dlam[0]
            dcur = _mm_nt_res(dh3, wt("lru_w_in", j), 0, dz1, rw, "lru_in_dx")
            p_in = _mm_tn(s["xb"], dh3, d, rw, "lru_in_dw")
        layer_partials = ((mixer_keys[0], p_in, "col"), (mixer_keys[1], p_out.reshape(N_CHIPS, -1, d), "lead"))
        layer_partials += () if early else ffn_partials
        pending = [(key, pair_sum(p, kind), kind) for key, p, kind in layer_partials]
    grad_x = dcur.reshape(bd, seq, d)
    gp = {n: jnp.stack(v) for n, v in gp.items()}

    gate = gp["lru_w_gate"]
    gate = jnp.moveaxis(gate.reshape(gate.shape[:-1] + (N_CHIPS, gate_shape[-1])), -2, 0)
    gate = gate.astype(BF16).reshape(N_CHIPS, -1, gate_shape[-1])
    pending += [("gate", pair_sum(gate, "lead"), "lead"),
                ("pack", pair_sum(_pack_slots(gp, small_shapes), "lead"), "lead")]
    lands = _rs_scatter([(chip_sum, kind) for _, chip_sum, kind in pending])
    landed.update({key: land for (key, _, _), land in zip(pending, lands)})

    def update(keys, w3, m3, v3):
        g_mine = jnp.stack([_add_chips(landed[k], "rs_add_chips") for k in keys])
        g_sib = _d2d_stream(g_mine.reshape(1, 1, -1, g_mine.shape[-1]), False, "rs_share").reshape(g_mine.shape)
        return _adamw(w3, g_mine, g_sib, m3, v3, "adamw")

    g_out, d_out, m_out, v_out = {}, {}, {}, {}
    for n in GATHER_KIND:
        outs = update([(n, l) for l in range(w[n].shape[0])], w[n], mom[n], vel[n])
        g_out[n], d_out[n], m_out[n], v_out[n] = outs
    as_rows = lambda a: a.reshape(1, -1, a.shape[-1])
    outs = update(["gate"], as_rows(w["lru_w_gate"]), as_rows(mom["lru_w_gate"]), as_rows(vel["lru_w_gate"]))
    g_out["lru_w_gate"], d_out["lru_w_gate"], m_out["lru_w_gate"], v_out["lru_w_gate"] = (
        o.reshape(gate_shape) for o in outs)
    packs = update(["pack"], w_pack[None], _pack_local(mom, small_shapes)[None], _pack_local(vel, small_shapes)[None])
    for dst, pack in zip((g_out, d_out, m_out, v_out), packs):
        dst.update(_unpack_local(pack[0], small_shapes))

    return (loss, grad_x, *[g_out[n] for n in WEIGHTS], *[d_out[n] for n in WEIGHTS],
            *[m_out[n] for n in WEIGHTS], *[v_out[n] for n in WEIGHTS])
```

```python
import math

import jax
import jax.numpy as jnp
from jax import lax
from jax.experimental import pallas as pl
from jax.experimental.pallas import tpu as pltpu

F32 = jnp.float32
BF16 = jnp.bfloat16

DEPTH = 4
LRU_HEADS = 10
LRU_BLOCK = 128
LRU_C = 8.0
LN_EPS = 1e-5
ALPHA = (2.0 * DEPTH) ** 0.25
ADAM_LR, ADAM_B1, ADAM_B2, ADAM_EPS, ADAM_WD, ADAM_STEP = 0.001, 0.9, 0.999, 1e-08, 0.01, 10
N_CHIPS = 4
MESH_AXES = ("x", "y", "c")

VMEM_LIMIT_BYTES = 48 * 1024 * 1024
TM_MM = 512
TM_RES = 1024
TT_MM = 2048
TM_SC = 256
FFN_COL_BLOCKS = 2
RC = 128
LANE = 128
MXU_COLS = 256
TS_LRU = 256
TM_LN = 512
ELEMS_PER_BLOCK = 256 * 1024
STREAM_ELEMS_PER_BLOCK = 1024 * 1024
SUB = 8
SUB16 = 16


def _cp(*sem):
    return pltpu.CompilerParams(dimension_semantics=sem, vmem_limit_bytes=VMEM_LIMIT_BYTES)


def _sigmoid(v):
    return 0.5 + 0.5 * jnp.tanh(0.5 * v)


def _softplus(v):
    e = jnp.exp(-jnp.abs(v))
    log1p = jnp.where(e < 1e-3, e * (1.0 - e * (0.5 - e * (1.0 / 3.0))), jnp.log(1.0 + e))
    return jnp.maximum(v, 0.0) + log1p


def _one_minus_exp(v):
    series = -v * (1.0 + v * (0.5 + v * (1.0 / 6.0 + v * (1.0 / 24.0))))
    return jnp.where(v > -0.02, series, 1.0 - jnp.exp(v))


def _gelu_and_grad(v):
    k = math.sqrt(2.0 / math.pi)
    t = jnp.tanh(k * (v + 0.044715 * v * v * v))
    val = 0.5 * v * (1.0 + t)
    grad = 0.5 * (1.0 + t) + 0.5 * v * (1.0 - t * t) * k * (1.0 + 3.0 * 0.044715 * v * v)
    return val, grad


def _down(ext, k, n_head):
    if k:
        ext = pltpu.roll(ext, k, 0)
    return ext[n_head:]


def _up(ext, k, n):
    if k:
        ext = pltpu.roll(ext, ext.shape[0] - k, 0)
    return ext[:n]


def _row(ref, k):
    return ref[k:k + 1, :]


def _colsum(v):
    return jnp.sum(v, axis=0, keepdims=True)


def _mm_nn(a, w3, l, bias, tn, name):
    m, k = a.shape
    n = w3.shape[2]
    tm = min(TM_MM, m)
    has_bias = bias is not None

    def body(*refs):
        if has_bias:
            a_ref, w_ref, b_ref, o_ref = refs
        else:
            a_ref, w_ref, o_ref = refs
        acc = jnp.dot(a_ref[...], w_ref[...], preferred_element_type=F32)
        if has_bias:
            acc = acc + b_ref[...]
        o_ref[...] = acc.astype(o_ref.dtype)

    in_specs = [pl.BlockSpec((tm, k), lambda i, j: (i, 0)),
                pl.BlockSpec((None, k, tn), lambda i, j: (l, 0, j))]
    args = [a, w3]
    if has_bias:
        in_specs.append(pl.BlockSpec((1, tn), lambda i, j: (0, j)))
        args.append(bias)
    return pl.pallas_call(
        body, name=name, grid=(m // tm, n // tn), in_specs=in_specs,
        out_specs=pl.BlockSpec((tm, tn), lambda i, j: (i, j)),
        out_shape=jax.ShapeDtypeStruct((m, n), BF16),
        compiler_params=_cp("parallel", "arbitrary"))(*args)


def _mm_nn_ln(a, w3, l, xres, g, b, name, rider=()):
    m, k = a.shape
    n = w3.shape[2]
    tm = min(TM_MM, m)
    bufs, rkinds, rspecs, rshapes, rsems = _rider_args(rider)
    nr = len(bufs)

    def body(*refs):
        a_ref, w_ref, x_ref, g_ref, b_ref = refs[:5]
        z_ref, xn_ref, xb_ref = refs[5 + nr:8 + nr]
        rout, sems = refs[8 + nr:8 + 2 * nr], refs[8 + 2 * nr:]
        if nr:
            @pl.when(pl.program_id(0) == 0)
            def _():
                _rider_start(rout, rkinds, *sems)

        y = jnp.dot(a_ref[...], w_ref[...], preferred_element_type=F32)
        z = ALPHA * x_ref[...] + y
        mu = jnp.mean(z, axis=-1, keepdims=True)
        zc = z - mu
        var = jnp.mean(zc * zc, axis=-1, keepdims=True)
        xn = zc * lax.rsqrt(var + LN_EPS) * g_ref[...] + b_ref[...]
        z_ref[...] = z
        xn_ref[...] = xn
        xb_ref[...] = xn.astype(BF16)
        if nr:
            @pl.when(pl.program_id(0) == m // tm - 1)
            def _():
                _rider_wait(rout, rkinds, *sems)

    row = pl.BlockSpec((tm, n), lambda i: (i, 0))
    vec = pl.BlockSpec((1, n), lambda i: (0, 0))
    return pl.pallas_call(
        body, name=name, grid=(m // tm,),
        in_specs=[pl.BlockSpec((tm, k), lambda i: (i, 0)),
                  pl.BlockSpec((None, k, n), lambda i: (l, 0, 0)), row, vec, vec] + rspecs,
        out_specs=[row, row, row] + rspecs,
        out_shape=[jax.ShapeDtypeStruct((m, n), F32), jax.ShapeDtypeStruct((m, n), F32),
                   jax.ShapeDtypeStruct((m, n), BF16)] + rshapes,
        input_output_aliases={5 + u: 3 + u for u in range(nr)},
        scratch_shapes=rsems,
        compiler_params=_cp("arbitrary"))(a, w3, xres, g, b, *bufs)


def _mm_nt(a, w3, l, tk, name):
    m, n = a.shape
    kd = w3.shape[1]
    tm = min(TM_MM, m)

    def body(a_ref, w_ref, o_ref):
        o_ref[...] = lax.dot_general(a_ref[...], w_ref[...], (((1,), (1,)), ((), ())),
                                     preferred_element_type=F32).astype(o_ref.dtype)

    return pl.pallas_call(
        body, name=name, grid=(m // tm, kd // tk),
        in_specs=[pl.BlockSpec((tm, n), lambda i, j: (i, 0)),
                  pl.BlockSpec((None, tk, n), lambda i, j: (l, j, 0))],
        out_specs=pl.BlockSpec((tm, tk), lambda i, j: (i, j)),
        out_shape=jax.ShapeDtypeStruct((m, kd), BF16),
        compiler_params=_cp("parallel", "arbitrary"))(a, w3)


def _mm_nt_res(dh3, w3, l, dz, tc, name, scatter=()):
    g, m, cg = dh3.shape
    kd = w3.shape[1]
    ncg = cg // tc
    nk = g * ncg
    tm = min(TM_RES, m)
    srcs, skinds, sspecs, sshapes, ssems = _scatter_args(scatter)
    ns = len(srcs)

    def body(*refs):
        a_ref, w_ref, dz_ref = refs[:3]
        sin, o_ref, lands = refs[3:3 + ns], refs[3 + ns], refs[4 + ns:4 + 2 * ns]
        acc, sems = refs[4 + 2 * ns], refs[5 + 2 * ns:]
        i = pl.program_id(0)
        k = pl.program_id(1)
        if ns:
            @pl.when(jnp.logical_and(i == 0, k == 0))
            def _():
                for cp in _scatter_copies(sin, lands, skinds, *sems):
                    cp.start()

        @pl.when(k == 0)
        def _():
            acc[...] = ALPHA * dz_ref[...]

        acc[...] += lax.dot_general(a_ref[...], w_ref[...], (((1,), (1,)), ((), ())),
                                    preferred_element_type=F32)

        @pl.when(k == nk - 1)
        def _():
            o_ref[...] = acc[...]

        if ns:
            @pl.when(jnp.logical_and(i == m // tm - 1, k == nk - 1))
            def _():
                for cp in _scatter_copies(sin, lands, skinds, *sems):
                    cp.wait()

    outs = pl.pallas_call(
        body, name=name, grid=(m // tm, nk),
        in_specs=[pl.BlockSpec((None, tm, tc), lambda i, k: (k // ncg, i, k % ncg)),
                  pl.BlockSpec((None, kd, tc), lambda i, k: (l, 0, k)),
                  pl.BlockSpec((tm, kd), lambda i, k: (i, 0))] + sspecs,
        out_specs=[pl.BlockSpec((tm, kd), lambda i, k: (i, 0))] + sspecs,
        out_shape=[jax.ShapeDtypeStruct((m, kd), F32)] + sshapes,
        scratch_shapes=[pltpu.VMEM((tm, kd), F32)] + ssems,
        compiler_params=_cp("arbitrary", "arbitrary"))(dh3, w3, dz, *srcs)
    return outs if ns else outs[0]


def _mm_tn(a, b3, tka, tnb, name):
    t, ka = a.shape
    g, _, cg = b3.shape
    ncg = cg // tnb
    tt = min(TT_MM, t)
    nt = t // tt

    def body(a_ref, b_ref, o_ref, acc):
        s = pl.program_id(2)

        @pl.when(s == 0)
        def _():
            acc[...] = jnp.zeros_like(acc)

        acc[...] += lax.dot_general(a_ref[...], b_ref[...], (((0,), (0,)), ((), ())),
                                    preferred_element_type=F32)

        @pl.when(s == nt - 1)
        def _():
            o_ref[...] = acc[...].astype(o_ref.dtype)

    return pl.pallas_call(
        body, name=name, grid=(ka // tka, g * ncg, nt),
        in_specs=[pl.BlockSpec((tt, tka), lambda i, j, s: (s, i)),
                  pl.BlockSpec((None, tt, tnb), lambda i, j, s: (j // ncg, s, j % ncg))],
        out_specs=pl.BlockSpec((tka, tnb), lambda i, j, s: (i, j)),
        out_shape=jax.ShapeDtypeStruct((ka, g * cg), BF16),
        scratch_shapes=[pltpu.VMEM((tka, tnb), F32)],
        compiler_params=_cp("parallel", "parallel", "arbitrary"))(a, b3)


def _ln_bwd(dxn, z, g, name, target=None):
    m, d = z.shape
    tm = min(TM_LN, m)
    with_loss = target is not None

    def body(*refs):
        dx_ref, z_ref, g_ref = refs[:3]
        dz_ref, dzb_ref, dg_ref, db_ref = refs[3 + with_loss:7 + with_loss]

        @pl.when(pl.program_id(0) == 0)
        def _():
            for r in refs[5 + with_loss:]:
                r[...] = jnp.zeros_like(r)

        zz = z_ref[...]
        dx = dx_ref[...]
        if with_loss:
            err = dx - refs[3][...]
            dx = err * (1.0 / d)
            refs[8][...] += _colsum(err * err) * (0.5 / d)
        mu = jnp.mean(zz, axis=-1, keepdims=True)
        zc = zz - mu
        var = jnp.mean(zc * zc, axis=-1, keepdims=True)
        rstd = lax.rsqrt(var + LN_EPS)
        xh = zc * rstd
        dg_ref[...] += _colsum(dx * xh)
        db_ref[...] += _colsum(dx)
        dxh = dx * g_ref[...]
        m1 = jnp.mean(dxh, axis=-1, keepdims=True)
        m2 = jnp.mean(dxh * xh, axis=-1, keepdims=True)
        dz = rstd * (dxh - m1 - xh * m2)
        dz_ref[...] = dz
        dzb_ref[...] = dz.astype(BF16)

    row = pl.BlockSpec((tm, d), lambda i: (i, 0))
    vec = pl.BlockSpec((1, d), lambda i: (0, 0))
    vec_shape = jax.ShapeDtypeStruct((1, d), F32)
    return pl.pallas_call(
        body, name=name, grid=(m // tm,), in_specs=[row, row, vec] + [row] * with_loss,
        out_specs=[row, row, vec, vec] + [vec] * with_loss,
        out_shape=[jax.ShapeDtypeStruct((m, d), F32), jax.ShapeDtypeStruct((m, d), BF16), vec_shape, vec_shape]
        + [vec_shape] * with_loss,
        compiler_params=_cp("arbitrary"))(dxn, z, g, *([target] if with_loss else []))


def _sc_fwd(h, cw, cb, seq, name):
    t, d3 = h.shape
    d = d3 // 3
    tm = min(TM_SC, seq)

    def body(hb_ref, hc_ref, hv_ref, cw_ref, cb_ref, q_ref, carry):
        i = pl.program_id(0)

        @pl.when(lax.rem(i * tm, seq) == 0)
        def _():
            carry[...] = jnp.zeros_like(carry)

        p = hc_ref[...].astype(F32) * hv_ref[...].astype(F32)
        ext = jnp.concatenate([carry[...], p], axis=0)
        u = cb_ref[...] + _row(cw_ref, 0) * _down(ext, 2, SUB) + _row(cw_ref, 1) * _down(ext, 1, SUB) \
            + _row(cw_ref, 2) * p
        q_ref[...] = (hb_ref[...].astype(F32) * u).astype(BF16)
        carry[...] = p[tm - SUB:, :]

    blk = lambda c: pl.BlockSpec((tm, d), lambda i: (i, c))
    return pl.pallas_call(
        body, name=name, grid=(t // tm,),
        in_specs=[blk(0), blk(1), blk(2), pl.BlockSpec((3, d), lambda i: (0, 0)),
                  pl.BlockSpec((1, d), lambda i: (0, 0))],
        out_specs=pl.BlockSpec((tm, d), lambda i: (i, 0)),
        out_shape=jax.ShapeDtypeStruct((t, d), BF16),
        scratch_shapes=[pltpu.VMEM((SUB, d), F32)],
        compiler_params=_cp("arbitrary"))(h, h, h, cw, cb)


def _sc_bwd(h, dq, cw, cb, seq, name):
    t, d3 = h.shape
    d = d3 // 3
    tm = min(TM_SC, seq)
    nt = t // tm
    hpt = tm // SUB16

    def body(hb_ref, hc_ref, hv_ref, hch_ref, hvh_ref, dq_ref, cw_ref, cb_ref,
             dh_ref, dcw_ref, dcb_ref, carry):
        i = pl.program_id(0)
        ri = nt - 1 - i

        @pl.when(i == 0)
        def _():
            dcw_ref[...] = jnp.zeros_like(dcw_ref)
            dcb_ref[...] = jnp.zeros_like(dcb_ref)

        @pl.when(lax.rem((ri + 1) * tm, seq) == 0)
        def _():
            carry[...] = jnp.zeros_like(carry)

        keep = jnp.where(lax.rem(ri * tm, seq) == 0, 0.0, 1.0)
        gb = hb_ref[...].astype(F32)
        gc = hc_ref[...].astype(F32)
        v = hv_ref[...].astype(F32)
        p = gc * v
        p_head = hch_ref[...].astype(F32) * hvh_ref[...].astype(F32) * keep
        ext = jnp.concatenate([p_head, p], axis=0)
        pm2 = _down(ext, 2, SUB16)
        pm1 = _down(ext, 1, SUB16)
        u = cb_ref[...] + _row(cw_ref, 0) * pm2 + _row(cw_ref, 1) * pm1 + _row(cw_ref, 2) * p
        dqf = dq_ref[...].astype(F32)
        du = dqf * gb
        dcb_ref[...] += _colsum(du)
        dcw_ref[0:1, :] += _colsum(du * pm2)
        dcw_ref[1:2, :] += _colsum(du * pm1)
        dcw_ref[2:3, :] += _colsum(du * p)
        ext2 = jnp.concatenate([du, carry[...]], axis=0)
        dp = _row(cw_ref, 2) * du + _row(cw_ref, 1) * _up(ext2, 1, tm) + _row(cw_ref, 0) * _up(ext2, 2, tm)
        carry[...] = du[0:SUB, :]
        dh_ref[0] = (dqf * u).astype(BF16)
        dh_ref[1] = (dp * v).astype(BF16)
        dh_ref[2] = (dp * gc).astype(BF16)

    blk = lambda c: pl.BlockSpec((tm, d), lambda i: (nt - 1 - i, c))
    head = lambda c: pl.BlockSpec((SUB16, d), lambda i: (jnp.maximum((nt - 1 - i) * hpt - 1, 0), c))
    vec = lambda r: pl.BlockSpec((r, d), lambda i: (0, 0))
    return pl.pallas_call(
        body, name=name, grid=(nt,),
        in_specs=[blk(0), blk(1), blk(2), head(1), head(2),
                  pl.BlockSpec((tm, d), lambda i: (nt - 1 - i, 0)), vec(3), vec(1)],
        out_specs=[pl.BlockSpec((3, tm, d), lambda i: (0, nt - 1 - i, 0)), vec(3), vec(1)],
        out_shape=[jax.ShapeDtypeStruct((3, t, d), BF16), jax.ShapeDtypeStruct((3, d), F32),
                   jax.ShapeDtypeStruct((1, d), F32)],
        scratch_shapes=[pltpu.VMEM((SUB, d), F32)],
        compiler_params=_cp("arbitrary"))(h, h, h, h, h, dq, cw, cb)


def _fold(v):
    return jnp.sum(v.reshape(v.shape[0] // SUB, SUB, v.shape[1]), axis=0)


def _ffn_up(xb, w3, l, cw, cb, seq, name, rider=()):
    t, d = xb.shape
    f = w3.shape[2] // 2
    tc = f // FFN_COL_BLOCKS
    tm = min(TM_MM, seq)
    nc = FFN_COL_BLOCKS
    bufs, rkinds, rspecs, rshapes, rsems = _rider_args(rider)
    nr = len(bufs)

    def body(*refs):
        x_ref, wg_ref, wv_ref, cwg_ref, cwv_ref, cbg_ref, cbv_ref = refs[:7]
        h_ref, pre_ref, act_ref = refs[7 + nr:10 + nr]
        rout = refs[10 + nr:10 + 2 * nr]
        eg, ev = refs[10 + 2 * nr:12 + 2 * nr]
        sems = refs[12 + 2 * nr:]
        i = pl.program_id(1)
        if nr:
            @pl.when(jnp.logical_and(pl.program_id(0) == 0, i == 0))
            def _():
                _rider_start(rout, rkinds, *sems)

        @pl.when(lax.rem(i * tm, seq) == 0)
        def _():
            eg[0:SUB, :] = jnp.zeros((SUB, tc), F32)
            ev[0:SUB, :] = jnp.zeros((SUB, tc), F32)

        xx = x_ref[...]

        def matmul(lo, hi):
            eg[SUB:, lo:hi] = jnp.dot(xx, wg_ref[:, lo:hi], preferred_element_type=F32)
            ev[SUB:, lo:hi] = jnp.dot(xx, wv_ref[:, lo:hi], preferred_element_type=F32)

        def epilogue(lo, hi):
            for c0 in range(lo, hi, LANE):
                cols = slice(c0, c0 + LANE)
                taps = [[ref[k:k + 1, cols] for k in range(3)] + [bref[:, cols]]
                        for ref, bref in ((cwg_ref, cbg_ref), (cwv_ref, cbv_ref))]
                for r0 in range(0, tm, RC):
                    rows = slice(r0, r0 + RC)
                    pres = []
                    for half, e_ref in enumerate((eg, ev)):
                        w0, w1, w2, bias = taps[half]
                        e = e_ref[r0:r0 + RC + SUB, cols]
                        cur = e[SUB:]
                        pre = bias + w0 * _down(e, 2, SUB) + w1 * _down(e, 1, SUB) + w2 * cur
                        h_ref[half, rows, cols] = cur.astype(BF16)
                        pre_ref[half, rows, cols] = pre.astype(BF16)
                        pres.append(pre)
                    act_ref[rows, cols] = (pres[0] * _sigmoid(pres[0]) * pres[1]).astype(BF16)

        blocks = [(lo, min(lo + MXU_COLS, tc)) for lo in range(0, tc, MXU_COLS)]
        matmul(*blocks[0])
        for b, blk in enumerate(blocks):
            if b + 1 < len(blocks):
                matmul(*blocks[b + 1])
            epilogue(*blk)
        eg[0:SUB, :] = eg[tm:tm + SUB, :]
        ev[0:SUB, :] = ev[tm:tm + SUB, :]
        if nr:
            @pl.when(jnp.logical_and(pl.program_id(0) == nc - 1, i == t // tm - 1))
            def _():
                _rider_wait(rout, rkinds, *sems)

    wspec = lambda off: pl.BlockSpec((None, d, tc), lambda j, i: (l, 0, j + off))
    vec = lambda r, off: pl.BlockSpec((r, tc), lambda j, i: (0, j + off))
    pair = pl.BlockSpec((2, tm, tc), lambda j, i: (0, i, j))
    return pl.pallas_call(
        body, name=name, grid=(nc, t // tm),
        in_specs=[pl.BlockSpec((tm, d), lambda j, i: (i, 0)), wspec(0), wspec(nc),
                  vec(3, 0), vec(3, nc), vec(1, 0), vec(1, nc)] + rspecs,
        out_specs=[pair, pair, pl.BlockSpec((tm, tc), lambda j, i: (i, j))] + rspecs,
        out_shape=[jax.ShapeDtypeStruct((2, t, f), BF16), jax.ShapeDtypeStruct((2, t, f), BF16),
                   jax.ShapeDtypeStruct((t, f), BF16)] + rshapes,
        input_output_aliases={7 + u: 3 + u for u in range(nr)},
        scratch_shapes=[pltpu.VMEM((tm + SUB, tc), F32), pltpu.VMEM((tm + SUB, tc), F32)] + rsems,
        compiler_params=_cp("arbitrary", "arbitrary"))(xb, w3, w3, cw, cw, cb, cb, *bufs)


def _ffn_down_bwd(dzb, wd3, l, h3, pre3, cw, seq, name, scatter=()):
    t, d = dzb.shape
    f = wd3.shape[1]
    tc = f // FFN_COL_BLOCKS
    tm = min(TM_MM, seq)
    nt = t // tm
    nc = FFN_COL_BLOCKS
    srcs, skinds, sspecs, sshapes, ssems = _scatter_args(scatter)
    ns = len(srcs)

    def body(*refs):
        dz_ref, wd_ref, h_ref, pre_ref, cwg_ref, cwv_ref = refs[:6]
        sin = refs[6:6 + ns]
        dh_ref, dcwg_ref, dcwv_ref, dcbg_ref, dcbv_ref = refs[6 + ns:11 + ns]
        lands = refs[11 + ns:11 + 2 * ns]
        da_s, carry = refs[11 + 2 * ns:13 + 2 * ns]
        sems = refs[13 + 2 * ns:]
        i = pl.program_id(1)
        ri = nt - 1 - i
        if ns:
            @pl.when(jnp.logical_and(pl.program_id(0) == 0, i == 0))
            def _():
                for cp in _scatter_copies(sin, lands, skinds, *sems):
                    cp.start()

        @pl.when(i == 0)
        def _():
            for r in (dcwg_ref, dcwv_ref, dcbg_ref, dcbv_ref):
                r[...] = jnp.zeros_like(r)

        @pl.when(lax.rem((ri + 1) * tm, seq) == 0)
        def _():
            carry[...] = jnp.zeros_like(carry)

        dz = dz_ref[...]

        def matmul(lo, hi):
            da_s[:, lo:hi] = lax.dot_general(dz, wd_ref[lo:hi, :], (((1,), (1,)), ((), ())),
                                             preferred_element_type=F32)

        def epilogue(lo, hi):
            for c0 in range(lo, hi, LANE):
                cols = slice(c0, c0 + LANE)
                taps = [[ref[k:k + 1, cols] for k in range(3)] for ref in (cwg_ref, cwv_ref)]
                acc = [jnp.zeros((SUB, LANE), F32)] * 8
                for r0 in range(tm - RC, -1, -RC):
                    rows = slice(r0, r0 + RC)
                    da = da_s[rows, cols]
                    gp = pre_ref[0, rows, cols].astype(F32)
                    vp = pre_ref[1, rows, cols].astype(F32)
                    sg = _sigmoid(gp)
                    dpres = (da * vp * (sg * (1.0 + gp * (1.0 - sg))), da * (gp * sg))
                    for half in range(2):
                        w0, w1, w2 = taps[half]
                        dpre = dpres[half]
                        ext = jnp.concatenate([dpre, carry[half, :, cols]], axis=0)
                        u1 = _up(ext, 1, RC)
                        u2 = _up(ext, 2, RC)
                        carry[half, :, cols] = dpre[0:SUB]
                        dh_ref[half, rows, cols] = (w2 * dpre + w1 * u1 + w0 * u2).astype(BF16)
                        hh = h_ref[half, rows, cols].astype(F32)
                        for k, term in enumerate((hh * u2, hh * u1, hh * dpre, dpre)):
                            acc[4 * half + k] = acc[4 * half + k] + _fold(term)
                for half, (dcw_ref, dcb_ref) in enumerate(((dcwg_ref, dcbg_ref), (dcwv_ref, dcbv_ref))):
                    for k in range(3):
                        dcw_ref[k:k + 1, cols] += _colsum(acc[4 * half + k])
                    dcb_ref[:, cols] += _colsum(acc[4 * half + 3])

        blocks = [(lo, min(lo + MXU_COLS, tc)) for lo in range(0, tc, MXU_COLS)]
        matmul(*blocks[0])
        for b, blk in enumerate(blocks):
            if b + 1 < len(blocks):
                matmul(*blocks[b + 1])
            epilogue(*blk)
        if ns:
            @pl.when(jnp.logical_and(pl.program_id(0) == nc - 1, i == nt - 1))
            def _():
                for cp in _scatter_copies(sin, lands, skinds, *sems):
                    cp.wait()

    pair = pl.BlockSpec((2, tm, tc), lambda j, i: (0, nt - 1 - i, j))
    vec = lambda off: pl.BlockSpec((3, tc), lambda j, i: (0, j + off))
    acc_spec = lambda r: pl.BlockSpec((r, tc), lambda j, i: (0, j))
    return pl.pallas_call(
        body, name=name, grid=(nc, nt),
        in_specs=[pl.BlockSpec((tm, d), lambda j, i: (nt - 1 - i, 0)),
                  pl.BlockSpec((None, tc, d), lambda j, i: (l, j, 0)), pair, pair, vec(0), vec(nc)] + sspecs,
        out_specs=[pair, acc_spec(3), acc_spec(3), acc_spec(1), acc_spec(1)] + sspecs,
        out_shape=[jax.ShapeDtypeStruct((2, t, f), BF16), jax.ShapeDtypeStruct((3, f), F32),
                   jax.ShapeDtypeStruct((3, f), F32), jax.ShapeDtypeStruct((1, f), F32),
                   jax.ShapeDtypeStruct((1, f), F32)] + sshapes,
        scratch_shapes=[pltpu.VMEM((tm, tc), F32), pltpu.VMEM((2, SUB, tc), F32)] + ssems,
        compiler_params=_cp("arbitrary", "arbitrary"))(dzb, wd3, h3, pre3, cw, cw, *srcs)


def _lru_gates(xr, wg_ref, bg_ref):
    rs, gs = [], []
    for hd in range(LRU_HEADS):
        xh = xr[:, hd * LRU_BLOCK:(hd + 1) * LRU_BLOCK].astype(BF16)
        gt = jnp.dot(xh, wg_ref[hd], preferred_element_type=F32) + _row(bg_ref, hd)
        rs.append(gt[:, :LRU_BLOCK])
        gs.append(gt[:, LRU_BLOCK:])
    return jnp.concatenate(rs, axis=1), jnp.concatenate(gs, axis=1)


def _lru_coeffs(xr, wg_ref, bg_ref, lam_ref):
    gr, gi = _lru_gates(xr, wg_ref, bg_ref)
    r = _sigmoid(gr)
    ig = _sigmoid(gi)
    sp = _softplus(-lam_ref[...])
    log_a = -LRU_C * r * sp
    a = jnp.exp(log_a)
    mult = jnp.sqrt(_one_minus_exp(2.0 * log_a))
    return r, ig, sp, a, mult


def _lru_fwd(h, cw, cb, wg, bg, lam, seq, name):
    t, r2 = h.shape
    rw = r2 // 2
    ts = min(TS_LRU, seq)
    n8 = ts // SUB

    def body(hg_ref, hr_ref, cw_ref, cb_ref, wg_ref, bg_ref, lam_ref, hs_ref, y_ref,
             a_s, b_s, cconv, cstate):
        i = pl.program_id(0)

        @pl.when(lax.rem(i * ts, seq) == 0)
        def _():
            cconv[...] = jnp.zeros_like(cconv)
            cstate[...] = jnp.zeros_like(cstate)

        rin = hr_ref[...].astype(F32)
        ext = jnp.concatenate([cconv[...], rin], axis=0)
        xr = cb_ref[...]
        for k in range(4):
            xr = xr + _row(cw_ref, k) * _down(ext, 3 - k, SUB)
        cconv[...] = rin[ts - SUB:, :]
        _, ig, _, a, mult = _lru_coeffs(xr, wg_ref, bg_ref, lam_ref)
        a_s[...] = a
        b_s[...] = mult * (ig * xr)
        row = lax.broadcasted_iota(jnp.int32, (SUB, rw), 0)

        def step(j, carry):
            off = pl.multiple_of(j * SUB, SUB)
            a8 = a_s[pl.ds(off, SUB), :]
            b8 = b_s[pl.ds(off, SUB), :]
            for d in (1, 2, 4):
                m = row >= d
                b8 = jnp.where(m, a8 * pltpu.roll(b8, d, 0) + b8, b8)
                a8 = jnp.where(m, a8 * pltpu.roll(a8, d, 0), a8)
            h8 = a8 * carry + b8
            hs_ref[pl.ds(off, SUB), :] = h8
            return _colsum(jnp.where(row == SUB - 1, h8, 0.0))

        cstate[...] = lax.fori_loop(0, n8, step, cstate[...])
        gel, _ = _gelu_and_grad(hg_ref[...].astype(F32))
        y_ref[...] = (hs_ref[...] * gel).astype(BF16)

    full = lambda shp: pl.BlockSpec(shp, lambda i: (0,) * len(shp))
    return pl.pallas_call(
        body, name=name, grid=(t // ts,),
        in_specs=[pl.BlockSpec((ts, rw), lambda i: (i, 0)), pl.BlockSpec((ts, rw), lambda i: (i, 1)),
                  full((4, rw)), full((1, rw)), full(wg.shape), full(bg.shape), full((1, rw))],
        out_specs=[pl.BlockSpec((ts, rw), lambda i: (i, 0)), pl.BlockSpec((ts, rw), lambda i: (i, 0))],
        out_shape=[jax.ShapeDtypeStruct((t, rw), F32), jax.ShapeDtypeStruct((t, rw), BF16)],
        scratch_shapes=[pltpu.VMEM((ts, rw), F32), pltpu.VMEM((ts, rw), F32),
                        pltpu.VMEM((SUB, rw), F32), pltpu.VMEM((1, rw), F32)],
        compiler_params=_cp("arbitrary"))(h, h, cw, cb, wg, bg, lam)


def _lru_bwd(h, hs, dy, cw, cb, wg, bg, lam, seq, name):
    t, r2 = h.shape
    rw = r2 // 2
    ts = min(TS_LRU, seq)
    nt = t // ts
    n8 = ts // SUB
    hp16 = ts // SUB16
    hp8 = ts // SUB

    def body(hg_ref, hr_ref, hrh_ref, hs_ref, hsh_ref, dy_ref, cw_ref, cb_ref, wg_ref, bg_ref, lam_ref,
             dh_ref, dbin_ref, dcw_ref, dcb_ref, dwg_ref, dbg_ref, dlam_ref,
             a_s, g_s, l_s, c_lam, c_a, c_dxr):
        i = pl.program_id(0)
        ri = nt - 1 - i

        @pl.when(i == 0)
        def _():
            for r in (dbin_ref, dcw_ref, dcb_ref, dwg_ref, dbg_ref, dlam_ref):
                r[...] = jnp.zeros_like(r)

        @pl.when(lax.rem((ri + 1) * ts, seq) == 0)
        def _():
            c_lam[...] = jnp.zeros_like(c_lam)
            c_a[...] = jnp.zeros_like(c_a)
            c_dxr[...] = jnp.zeros_like(c_dxr)

        keep = jnp.where(lax.rem(ri * ts, seq) == 0, 0.0, 1.0)
        rin = hr_ref[...].astype(F32)
        ext = jnp.concatenate([hrh_ref[...].astype(F32) * keep, rin], axis=0)
        shifted = [_down(ext, 3 - k, SUB16) for k in range(4)]
        xr = cb_ref[...]
        for k in range(4):
            xr = xr + _row(cw_ref, k) * shifted[k]
        r, ig, sp, a, mult = _lru_coeffs(xr, wg_ref, bg_ref, lam_ref)
        gel, dgel = _gelu_and_grad(hg_ref[...].astype(F32))
        dyf = dy_ref[...].astype(F32)
        hsv = hs_ref[...]
        dg = dyf * hsv * dgel

        a_s[...] = _up(jnp.concatenate([a, c_a[...]], axis=0), 1, ts)
        g_s[...] = dyf * gel
        c_a[...] = a[0:SUB, :]
        row = lax.broadcasted_iota(jnp.int32, (SUB, rw), 0)

        def step(j, carry):
            off = pl.multiple_of((n8 - 1 - j) * SUB, SUB)
            a8 = a_s[pl.ds(off, SUB), :]
            b8 = g_s[pl.ds(off, SUB), :]
            for d in (1, 2, 4):
                m = row < SUB - d
                b8 = jnp.where(m, a8 * pltpu.roll(b8, SUB - d, 0) + b8, b8)
                a8 = jnp.where(m, a8 * pltpu.roll(a8, SUB - d, 0), a8)
            l8 = a8 * carry + b8
            l_s[pl.ds(off, SUB), :] = l8
            return _colsum(jnp.where(row == 0, l8, 0.0))

        c_lam[...] = lax.fori_loop(0, n8, step, c_lam[...])
        lamv = l_s[...]
        hs_prev = _down(jnp.concatenate([hsh_ref[...] * keep, hsv], axis=0), 1, SUB)
        da = lamv * hs_prev
        t1 = lamv * xr
        dmult = t1 * ig
        dig = t1 * mult
        dxr = lamv * mult * ig
        dla = da * a - dmult * (a * a) / mult
        dr = dla * (-LRU_C * sp)
        dlam_ref[...] += _colsum(dla * (-LRU_C) * r) * (-1.0 / (1.0 + jnp.exp(lam_ref[...])))
        dgr = dr * r * (1.0 - r)
        dgi = dig * ig * (1.0 - ig)
        parts = []
        for hd in range(LRU_HEADS):
            sl = slice(hd * LRU_BLOCK, (hd + 1) * LRU_BLOCK)
            dgt = jnp.concatenate([dgr[:, sl], dgi[:, sl]], axis=1)
            dbg_ref[hd:hd + 1, :] += _colsum(dgt)
            dgt16 = dgt.astype(BF16)
            parts.append(lax.dot_general(dgt16, wg_ref[hd], (((1,), (1,)), ((), ())),
                                         preferred_element_type=F32))
            dwg_ref[hd] += lax.dot_general(xr[:, sl].astype(BF16), dgt16, (((0,), (0,)), ((), ())),
                                           preferred_element_type=F32)
        dxr = dxr + jnp.concatenate(parts, axis=1)

        dcb_ref[...] += _colsum(dxr)
        for k in range(4):
            dcw_ref[k:k + 1, :] += _colsum(dxr * shifted[k])
        ext2 = jnp.concatenate([dxr, c_dxr[...]], axis=0)
        drb = _row(cw_ref, 3) * dxr
        for k in range(3):
            drb = drb + _row(cw_ref, k) * _up(ext2, 3 - k, ts)
        c_dxr[...] = dxr[0:SUB, :]
        dh_ref[0] = dg.astype(BF16)
        dh_ref[1] = drb.astype(BF16)
        dbin_ref[:, 0:rw] += _colsum(dg)
        dbin_ref[:, rw:] += _colsum(drb)

    rev = lambda c: pl.BlockSpec((ts, rw), lambda i: (nt - 1 - i, c))
    full = lambda shp: pl.BlockSpec(shp, lambda i: (0,) * len(shp))
    nh = LRU_HEADS
    return pl.pallas_call(
        body, name=name, grid=(nt,),
        in_specs=[rev(0), rev(1),
                  pl.BlockSpec((SUB16, rw), lambda i: (jnp.maximum((nt - 1 - i) * hp16 - 1, 0), 1)),
                  rev(0),
                  pl.BlockSpec((SUB, rw), lambda i: (jnp.maximum((nt - 1 - i) * hp8 - 1, 0), 0)),
                  rev(0), full((4, rw)), full((1, rw)), full(wg.shape), full(bg.shape), full((1, rw))],
        out_specs=[pl.BlockSpec((2, ts, rw), lambda i: (0, nt - 1 - i, 0)), full((1, r2)), full((4, rw)),
                   full((1, rw)), full((nh, LRU_BLOCK, 2 * LRU_BLOCK)), full((nh, 2 * LRU_BLOCK)), full((1, rw))],
        out_shape=[jax.ShapeDtypeStruct((2, t, rw), BF16), jax.ShapeDtypeStruct((1, r2), F32),
                   jax.ShapeDtypeStruct((4, rw), F32), jax.ShapeDtypeStruct((1, rw), F32),
                   jax.ShapeDtypeStruct((nh, LRU_BLOCK, 2 * LRU_BLOCK), F32),
                   jax.ShapeDtypeStruct((nh, 2 * LRU_BLOCK), F32), jax.ShapeDtypeStruct((1, rw), F32)],
        scratch_shapes=[pltpu.VMEM((ts, rw), F32), pltpu.VMEM((ts, rw), F32), pltpu.VMEM((ts, rw), F32),
                        pltpu.VMEM((1, rw), F32), pltpu.VMEM((SUB, rw), F32), pltpu.VMEM((SUB, rw), F32)],
        compiler_params=_cp("arbitrary"))(h, h, h, hs, hs, dy, cw, cb, wg, bg, lam)


def _row_tile(rows, cols, mult, elems=ELEMS_PER_BLOCK):
    cap = max(mult, elems // cols)
    best = None
    for cand in range(mult, min(rows, cap) + 1, mult):
        if rows % cand == 0:
            best = cand
    return best if best is not None else rows


def _core_index():
    return lax.axis_index("c").astype(jnp.int32).reshape(1)


def _chip_index():
    return (2 * lax.axis_index("x") + lax.axis_index("y")).astype(jnp.int32).reshape(1)


def _add_pair(p4, r3, name):
    s, _, rows, cols = p4.shape
    tr = _row_tile(rows, cols, SUB16)

    def body(c_ref, a_ref, b_ref, o_ref):
        o_ref[...] = (a_ref[...].astype(F32) + b_ref[...].astype(F32)).astype(o_ref.dtype)

    blk = pl.BlockSpec((None, tr, cols), lambda k, i, c_ref: (k, i, 0))
    return pl.pallas_call(
        body, name=name,
        grid_spec=pltpu.PrefetchScalarGridSpec(
            num_scalar_prefetch=1, grid=(s, rows // tr),
            in_specs=[pl.BlockSpec((None, None, tr, cols), lambda k, i, c_ref: (k, c_ref[0], i, 0)), blk],
            out_specs=blk),
        out_shape=jax.ShapeDtypeStruct((s, rows, cols), p4.dtype),
        compiler_params=_cp("parallel", "parallel"))(_core_index(), p4, r3)


def _add_chips(r, name):
    shape = r.shape[1:]
    r3 = r.reshape(N_CHIPS, -1, shape[-1])
    _, rows, cols = r3.shape
    tr = _row_tile(rows, cols, SUB16)

    def body(r_ref, o_ref):
        s = r_ref[0].astype(F32) + r_ref[1].astype(F32)
        s = s + r_ref[2].astype(F32)
        o_ref[...] = s + r_ref[3].astype(F32)

    out = pl.pallas_call(body, name=name, grid=(rows // tr,),
                         in_specs=[pl.BlockSpec((N_CHIPS, tr, cols), lambda i: (0, i, 0))],
                         out_specs=pl.BlockSpec((tr, cols), lambda i: (i, 0)),
                         out_shape=jax.ShapeDtypeStruct((rows, cols), F32),
                         compiler_params=_cp("parallel"))(r3)
    return out.reshape(shape)


def _adamw(w3, g_mine, g_sib, m3, v3, name):
    nl, r, cols = w3.shape
    rows = r // 2
    flat = [arr.reshape(nl, 2, rows, cols) for arr in (w3, m3, v3)]
    tr = _row_tile(rows, cols, SUB)

    def body(c_ref, w_ref, gm_ref, gs_ref, m_ref, v_ref, g_ref, d_ref, mo_ref, vo_ref):
        gg = jnp.where(pl.program_id(1) == c_ref[0], gm_ref[...], gs_ref[...])
        m2 = ADAM_B1 * m_ref[...] + (1.0 - ADAM_B1) * gg
        v2 = ADAM_B2 * v_ref[...] + (1.0 - ADAM_B2) * (gg * gg)
        m_hat = m2 / (1.0 - ADAM_B1 ** ADAM_STEP)
        v_hat = v2 / (1.0 - ADAM_B2 ** ADAM_STEP)
        g_ref[...] = gg
        d_ref[...] = -ADAM_LR * (m_hat / (jnp.sqrt(v_hat) + ADAM_EPS) + ADAM_WD * w_ref[...])
        mo_ref[...] = m2
        vo_ref[...] = v2

    blk = pl.BlockSpec((None, None, tr, cols), lambda l, hh, i, c_ref: (l, hh, i, 0))
    gblk = pl.BlockSpec((None, tr, cols), lambda l, hh, i, c_ref: (l, i, 0))
    outs = pl.pallas_call(
        body, name=name,
        grid_spec=pltpu.PrefetchScalarGridSpec(
            num_scalar_prefetch=1, grid=(nl, 2, rows // tr),
            in_specs=[blk, gblk, gblk, blk, blk], out_specs=[blk] * 4),
        out_shape=[jax.ShapeDtypeStruct((nl, 2, rows, cols), F32)] * 4,
        compiler_params=_cp("parallel", "parallel", "parallel"))(_core_index(), flat[0], g_mine, g_sib, flat[1],
                                                                 flat[2])
    return tuple(o.reshape(nl, r, cols) for o in outs)


def _place(src3, layer, kind, dtype, name):
    _, r, c = src3.shape
    tr = _row_tile(r, c, SUB16)
    in_spec = pl.BlockSpec((None, tr, c), lambda i, my_ref: (layer, i, 0))
    if kind == "col":
        out_spec = pl.BlockSpec((tr, c), lambda i, my_ref: (i, my_ref[0]))
        out_shape = (r, N_CHIPS * c)
    else:
        out_spec = pl.BlockSpec((None, tr, c), lambda i, my_ref: (my_ref[0], i, 0))
        out_shape = (N_CHIPS, r, c)

    def body(my_ref, i_ref, o_ref):
        o_ref[...] = i_ref[...].astype(o_ref.dtype)

    return pl.pallas_call(
        body, name=name,
        grid_spec=pltpu.PrefetchScalarGridSpec(num_scalar_prefetch=1, grid=(r // tr,), in_specs=[in_spec],
                                               out_specs=out_spec),
        out_shape=jax.ShapeDtypeStruct(out_shape, dtype),
        compiler_params=_cp("parallel"))(_chip_index(), src3)


def _half(ref, c, h):
    return ref.at[pl.ds(c * h, h)]


def _position():
    x = lax.axis_index("x")
    y = lax.axis_index("y")
    c = lax.axis_index("c")
    return x, y, c


def _peer_chip(x, y, j):
    tx = 1 - x if j & 2 else x
    ty = 1 - y if j & 1 else y
    return tx, ty


def _remote(src, dst, ssem, rsem, dev):
    return pltpu.make_async_remote_copy(src_ref=src, dst_ref=dst, send_sem=ssem, recv_sem=rsem,
                                        device_id=dev, device_id_type=pl.DeviceIdType.MESH)


_ANY = pl.BlockSpec(memory_space=pl.ANY)


def _unit_view(kind, ref, k):
    if kind == "col":
        n = ref.shape[1] // N_CHIPS
        return ref.at[:, pl.ds(pl.multiple_of(k * n, LANE), n)]
    return ref.at[k]


def _all_gather(placed, kinds):
    nt = len(placed)

    def body(*refs):
        outs = refs[nt:2 * nt]
        ssem, rsem = refs[2 * nt:]
        x, y, c = _position()
        my = 2 * x + y
        sib = (x, y, 1 - c)

        def part(t, k, core):
            view = _unit_view(kinds[t], outs[t], k)
            h = view.shape[0] // 2
            return _half(view, core, h)

        sends, fwds = [], []
        for t in range(nt):
            own = part(t, my, c)
            for j in (1, 2, 3):
                tx, ty = _peer_chip(x, y, j)
                cp = _remote(own, own, ssem.at[6 * t + j - 1], rsem.at[6 * t + j - 1], (tx, ty, c))
                cp.start()
                sends.append(cp)
        for t in range(nt):
            for j in (1, 2, 3):
                tx, ty = _peer_chip(x, y, j)
                got = part(t, 2 * tx + ty, c)
                _remote(got, got, ssem.at[6 * t + j - 1], rsem.at[6 * t + j - 1], sib).wait_recv()
                cp = _remote(got, got, ssem.at[6 * t + 2 + j], rsem.at[6 * t + 2 + j], sib)
                cp.start()
                fwds.append(cp)
        for t in range(nt):
            for j in (1, 2, 3):
                tx, ty = _peer_chip(x, y, j)
                other = part(t, 2 * tx + ty, 1 - c)
                _remote(other, other, ssem.at[6 * t + 2 + j], rsem.at[6 * t + 2 + j], sib).wait_recv()
        for cp in sends + fwds:
            cp.wait_send()

    return pl.pallas_call(
        body, name="all_gather", in_specs=[_ANY] * nt, out_specs=[_ANY] * nt,
        out_shape=[jax.ShapeDtypeStruct(p.shape, p.dtype) for p in placed],
        input_output_aliases={t: t for t in range(nt)},
        scratch_shapes=[pltpu.SemaphoreType.DMA((6 * nt,)), pltpu.SemaphoreType.DMA((6 * nt,))],
    )(*placed)


def _rider_start(refs, kinds, ssem, rsem):
    x, y, c = _position()
    my = 2 * x + y
    for u, (ref, kind) in enumerate(zip(refs, kinds)):
        own = _unit_view(kind, ref, my)
        for j in (1, 2, 3):
            tx, ty = _peer_chip(x, y, j)
            _remote(own, own, ssem.at[3 * u + j - 1], rsem.at[3 * u + j - 1], (tx, ty, c)).start()


def _rider_wait(refs, kinds, ssem, rsem):
    x, y, c = _position()
    for u, (ref, kind) in enumerate(zip(refs, kinds)):
        for j in (1, 2, 3):
            tx, ty = _peer_chip(x, y, j)
            got = _unit_view(kind, ref, 2 * tx + ty)
            _remote(got, got, ssem.at[3 * u + j - 1], rsem.at[3 * u + j - 1], (tx, ty, c)).wait()


def _rider_args(rider):
    bufs = [b for b, _ in rider]
    kinds = [k for _, k in rider]
    n = len(bufs)
    sems = [pltpu.SemaphoreType.DMA((3 * n,)), pltpu.SemaphoreType.DMA((3 * n,))] if n else []
    return bufs, kinds, [_ANY] * n, [jax.ShapeDtypeStruct(b.shape, b.dtype) for b in bufs], sems


def _d2d_stream(src4, other_half, name):
    s, _, rows, cols = src4.shape
    tr = _row_tile(rows, cols, SUB16, STREAM_ELEMS_PER_BLOCK)
    nblk = rows // tr

    nh = src4.shape[1]

    def body(c_ref, src_ref, dst_ref, ssem, rsem):
        k = pl.program_id(0)
        i = pl.program_id(1)
        x, y, c = _position()
        sib = (x, y, 1 - c)
        blk = dst_ref.at[pl.ds(pl.multiple_of((k * nblk + i) * tr, SUB16), tr)]
        cp = _remote(src_ref, blk, ssem, rsem, sib)
        cp.start()
        cp.wait_send()

        @pl.when(jnp.logical_and(k == s - 1, i == nblk - 1))
        def _():
            _remote(dst_ref, dst_ref, ssem, rsem, sib).wait_recv()

    if other_half:
        src_map = lambda k, i, c_ref: ((k * nh + 1 - c_ref[0]) * nblk + i, 0)
    else:
        src_map = lambda k, i, c_ref: (k * nh * nblk + i, 0)
    out = pl.pallas_call(
        body, name=name,
        grid_spec=pltpu.PrefetchScalarGridSpec(
            num_scalar_prefetch=1, grid=(s, nblk),
            in_specs=[pl.BlockSpec((tr, cols), src_map)], out_specs=_ANY,
            scratch_shapes=[pltpu.SemaphoreType.DMA, pltpu.SemaphoreType.DMA]),
        out_shape=jax.ShapeDtypeStruct((s * rows, cols), src4.dtype),
        compiler_params=_cp("arbitrary", "arbitrary"))(_core_index(), src4.reshape(s * nh * rows, cols))
    return out.reshape(s, rows, cols)


def _scatter_copies(srcs, lands, kinds, ssem, rsem, lsem):
    x, y, c = _position()
    my = 2 * x + y
    cps = []
    for u, (src, land, kind) in enumerate(zip(srcs, lands, kinds)):
        cps.append(pltpu.make_async_copy(_unit_view(kind, src, my), land.at[my], lsem.at[u]))
        for j in (1, 2, 3):
            tx, ty = _peer_chip(x, y, j)
            cps.append(_remote(_unit_view(kind, src, 2 * tx + ty), land.at[my], ssem.at[3 * u + j - 1],
                               rsem.at[3 * u + j - 1], (tx, ty, c)))
    return cps


def _scatter_args(scatter):
    srcs = [s for s, _ in scatter]
    kinds = [k for _, k in scatter]
    n = len(srcs)
    shapes = [jax.ShapeDtypeStruct((N_CHIPS, s.shape[0], s.shape[1] // N_CHIPS) if k == "col" else s.shape, s.dtype)
              for s, k in scatter]
    sems = [pltpu.SemaphoreType.DMA((3 * n,)), pltpu.SemaphoreType.DMA((3 * n,)),
            pltpu.SemaphoreType.DMA((n,))] if n else []
    return srcs, kinds, [_ANY] * n, shapes, sems


def _rs_scatter(scatter):
    srcs, kinds, specs, shapes, sems = _scatter_args(scatter)
    n = len(srcs)

    def body(*refs):
        cps = _scatter_copies(refs[:n], refs[n:2 * n], kinds, *refs[2 * n:])
        for cp in cps:
            cp.start()
        for cp in cps:
            cp.wait()

    return pl.pallas_call(body, name="rs_scatter", in_specs=specs, out_specs=specs, out_shape=shapes,
                          scratch_shapes=sems)(*srcs)


SMALL = (("sc_conv_w", True), ("sc_conv_b", False), ("lru_b_in", True), ("lru_conv_w", True),
         ("lru_conv_b", True), ("lru_b_gate", True), ("lru_lambda", True), ("ffn_conv_w", True),
         ("ffn_conv_b", False), ("ln_g", True), ("ln_b", True))
PACK_ROW_MULT = 2 * SUB16


def _pack_rows(shapes):
    n = sum(math.prod(shapes[name]) for name, _ in SMALL)
    rows = -(-n // 128)
    return -(-rows // PACK_ROW_MULT) * PACK_ROW_MULT


def _pack_local(vals, shapes):
    flat = jnp.concatenate([vals[name].reshape(-1) for name, _ in SMALL])
    rows = _pack_rows(shapes)
    return jnp.pad(flat, (0, rows * 128 - flat.shape[0])).reshape(rows, 128)


def _unpack_local(pack, shapes):
    flat = pack.reshape(-1)
    out, off = {}, 0
    for name, _ in SMALL:
        n = math.prod(shapes[name])
        out[name] = flat[off:off + n].reshape(shapes[name])
        off += n
    return out


def _pack_slots(fulls, shapes):
    parts = []
    for name, sharded in SMALL:
        v = fulls[name]
        if sharded:
            ns = shapes[name][-1]
            v = jnp.moveaxis(v.reshape(v.shape[:-1] + (N_CHIPS, ns)), -2, 0).reshape(N_CHIPS, -1)
        else:
            v = jnp.broadcast_to(v.reshape(1, -1), (N_CHIPS, v.size))
        parts.append(v)
    flat = jnp.concatenate(parts, axis=1)
    rows = _pack_rows(shapes)
    return jnp.pad(flat, ((0, 0), (0, rows * 128 - flat.shape[1]))).reshape(N_CHIPS, rows, 128)


def _unpack_slots(packs, shapes):
    flat = packs.reshape(N_CHIPS, -1)
    out, off = {}, 0
    for name, sharded in SMALL:
        n = math.prod(shapes[name])
        if sharded:
            seg = flat[:, off:off + n].reshape((N_CHIPS,) + tuple(shapes[name]))
            seg = jnp.moveaxis(seg, 0, -2)
            out[name] = seg.reshape(seg.shape[:-2] + (N_CHIPS * shapes[name][-1],))
        off += n
    return out


WEIGHTS = ("sc_w_in", "sc_conv_w", "sc_conv_b", "sc_w_out", "lru_w_in", "lru_b_in", "lru_conv_w", "lru_conv_b",
           "lru_w_gate", "lru_b_gate", "lru_lambda", "lru_w_out", "ffn_w_up", "ffn_conv_w", "ffn_conv_b",
           "ffn_w_down", "ln_g", "ln_b")
GATHER_KIND = {"sc_w_in": "col", "sc_w_out": "lead", "lru_w_in": "col", "lru_w_out": "lead", "ffn_w_up": "col",
               "ffn_w_down": "lead"}


def kernel(x, sc_w_in, sc_conv_w, sc_conv_b, sc_w_out, lru_w_in, lru_b_in, lru_conv_w, lru_conv_b, lru_w_gate, lru_b_gate, lru_lambda, lru_w_out, ffn_w_up, ffn_conv_w, ffn_conv_b, ffn_w_down, ln_g, ln_b, loss_target, m_sc_w_in, m_sc_conv_w, m_sc_conv_b, m_sc_w_out, m_lru_w_in, m_lru_b_in, m_lru_conv_w, m_lru_conv_b, m_lru_w_gate, m_lru_b_gate, m_lru_lambda, m_lru_w_out, m_ffn_w_up, m_ffn_conv_w, m_ffn_conv_b, m_ffn_w_down, m_ln_g, m_ln_b, v_sc_w_in, v_sc_conv_w, v_sc_conv_b, v_sc_w_out, v_lru_w_in, v_lru_b_in, v_lru_conv_w, v_lru_conv_b, v_lru_w_gate, v_lru_b_gate, v_lru_lambda, v_lru_w_out, v_ffn_w_up, v_ffn_conv_w, v_ffn_conv_b, v_ffn_w_down, v_ln_g, v_ln_b):
    w = dict(zip(WEIGHTS, (sc_w_in, sc_conv_w, sc_conv_b, sc_w_out, lru_w_in, lru_b_in, lru_conv_w, lru_conv_b,
                           lru_w_gate, lru_b_gate, lru_lambda, lru_w_out, ffn_w_up, ffn_conv_w, ffn_conv_b,
                           ffn_w_down, ln_g, ln_b)))
    mom = dict(zip(WEIGHTS, (m_sc_w_in, m_sc_conv_w, m_sc_conv_b, m_sc_w_out, m_lru_w_in, m_lru_b_in, m_lru_conv_w,
                             m_lru_conv_b, m_lru_w_gate, m_lru_b_gate, m_lru_lambda, m_lru_w_out, m_ffn_w_up,
                             m_ffn_conv_w, m_ffn_conv_b, m_ffn_w_down, m_ln_g, m_ln_b)))
    vel = dict(zip(WEIGHTS, (v_sc_w_in, v_sc_conv_w, v_sc_conv_b, v_sc_w_out, v_lru_w_in, v_lru_b_in, v_lru_conv_w,
                             v_lru_conv_b, v_lru_w_gate, v_lru_b_gate, v_lru_lambda, v_lru_w_out, v_ffn_w_up,
                             v_ffn_conv_w, v_ffn_conv_b, v_ffn_w_down, v_ln_g, v_ln_b)))
    bd, seq, d = x.shape
    t = bd * seq
    small_shapes = {name: w[name].shape for name, _ in SMALL}

    w_pack = _pack_local(w, small_shapes)
    gate_shape = w["lru_w_gate"].shape
    bufs = {(n, l): (_place(w[n], l, k, BF16, "place_w"), k)
            for n, k in GATHER_KIND.items() for l in range(w[n].shape[0])}
    bufs["gate"] = (_place(w["lru_w_gate"].reshape(1, -1, gate_shape[-1]), 0, "lead", BF16, "place_w"), "lead")
    bufs["pack"] = (_place(w_pack[None], 0, "lead", F32, "place_w"), "lead")

    def layer_keys(i):
        mixer = ("sc_w_in", "sc_w_out") if i % 2 == 0 else ("lru_w_in", "lru_w_out")
        return [(mixer[0], i // 2), (mixer[1], i // 2)], [("ffn_w_up", i), ("ffn_w_down", i)]

    def gathered(keys, arrays):
        for key, arr in zip(keys, arrays):
            bufs[key] = (arr, bufs[key][1])

    def wt(name, l):
        arr = bufs[(name, l)][0]
        return arr.reshape(1, -1, arr.shape[-1])

    first = layer_keys(0)[0] + layer_keys(0)[1] + ["gate", "pack"]
    gathered(first, _all_gather([bufs[k][0] for k in first], [bufs[k][1] for k in first]))
    full = _unpack_slots(bufs["pack"][0], small_shapes)
    full["sc_conv_b"] = sc_conv_b
    full["ffn_conv_b"] = ffn_conv_b
    wg_full = jnp.moveaxis(bufs["gate"][0].reshape((N_CHIPS,) + gate_shape), 0, -2)
    wg_full = wg_full.reshape(wg_full.shape[:-2] + (2 * LRU_BLOCK,))
    f = N_CHIPS * w["ffn_w_down"].shape[1]
    rw = N_CHIPS * w["lru_w_out"].shape[1]

    x0 = x.reshape(t, d)
    xb = x0.astype(BF16)
    cur, cur_b = x0, xb
    saved = []

    for i in range(DEPTH):
        j = i // 2
        s = {"xb": cur_b}
        mixer_next, ffn_next = layer_keys(i + 1) if i + 1 < DEPTH else ([], [])
        behind_out, behind_up, behind_down = mixer_next[1:], mixer_next[:1] + ffn_next[:1], ffn_next[1:]
        if i % 2 == 0:
            h = _mm_nn(cur_b, wt("sc_w_in", j), 0, None, 3 * d, "sc_in")
            q = _sc_fwd(h, full["sc_conv_w"][j], full["sc_conv_b"][j][None], seq, "sc_fwd")
            z1, x1, x1b, *arrived = _mm_nn_ln(q, wt("sc_w_out", j), 0, cur, full["ln_g"][i, 0][None],
                                              full["ln_b"][i, 0][None], "sc_out_ln",
                                              rider=[bufs[k] for k in behind_out])
        else:
            h = _mm_nn(cur_b, wt("lru_w_in", j), 0, full["lru_b_in"][j][None], 2 * rw, "lru_in")
            hs, q = _lru_fwd(h, full["lru_conv_w"][j], full["lru_conv_b"][j][None], wg_full[j],
                             full["lru_b_gate"][j], full["lru_lambda"][j][None], seq, "lru_fwd")
            s["hs"] = hs
            z1, x1, x1b, *arrived = _mm_nn_ln(q, wt("lru_w_out", j), 0, cur, full["ln_g"][i, 0][None],
                                              full["ln_b"][i, 0][None], "lru_out_ln",
                                              rider=[bufs[k] for k in behind_out])
        gathered(behind_out, arrived)
        s.update(h=h, q=q, z1=z1, x1b=x1b)
        h3, pre3, act, *arrived = _ffn_up(x1b, wt("ffn_w_up", i), 0, full["ffn_conv_w"][i],
                                          full["ffn_conv_b"][i][None], seq, "ffn_up",
                                          rider=[bufs[k] for k in behind_up])
        gathered(behind_up, arrived)
        z2, x2, x2b, *arrived = _mm_nn_ln(act, wt("ffn_w_down", i), 0, x1, full["ln_g"][i, 1][None],
                                          full["ln_b"][i, 1][None], "ffn_down_ln",
                                          rider=[bufs[k] for k in behind_down])
        gathered(behind_down, arrived)
        s.update(h3=h3, pre3=pre3, act=act, z2=z2)
        saved.append(s)
        cur, cur_b = x2, x2b

    dcur, loss = cur, None

    def pair_sum(p, kind):
        lead = kind == "lead"
        p4 = p.reshape(N_CHIPS if lead else 1, 2, -1, p.shape[-1])
        chip_sum = _add_pair(p4, _d2d_stream(p4, True, "rs_swap"), "rs_add_pair")
        return chip_sum if lead else chip_sum[0]

    gp = {n: [None] * w[n].shape[0] for n, _ in SMALL}
    gp["lru_w_gate"] = [None] * gate_shape[0]
    landed, pending = {}, []
    for i in reversed(range(DEPTH)):
        j = i // 2
        s = saved[i]
        mixer_keys, ffn_keys = layer_keys(i)
        if i == DEPTH - 1:
            dz2, dz2b, dg, db, loss_parts = _ln_bwd(dcur, s["z2"], full["ln_g"][i, 1][None], "ln_bwd_loss",
                                                    target=loss_target.reshape(t, d))
            loss = lax.psum(jnp.sum(loss_parts), MESH_AXES)
        else:
            dz2, dz2b, dg, db = _ln_bwd(dcur, s["z2"], full["ln_g"][i, 1][None], "ln_bwd")
        gp["ln_g"][i] = [None, dg[0]]
        gp["ln_b"][i] = [None, db[0]]
        p_down = _mm_tn(s["act"], dz2b[None], f // 2, d, "ffn_down_dw").reshape(N_CHIPS, -1, d)
        dh3, dcwg, dcwv, dcbg, dcbv, *lands = _ffn_down_bwd(
            dz2b, wt("ffn_w_down", i), 0, s["h3"], s["pre3"], full["ffn_conv_w"][i], seq, "ffn_down_bwd",
            scatter=[(chip_sum, kind) for _, chip_sum, kind in pending])
        landed.update({key: land for (key, _, _), land in zip(pending, lands)})
        gp["ffn_conv_w"][i] = jnp.concatenate([dcwg, dcwv], axis=1)
        gp["ffn_conv_b"][i] = jnp.concatenate([dcbg[0], dcbv[0]])
        dx1 = _mm_nt_res(dh3, wt("ffn_w_up", i), 0, dz2, f // 2, "ffn_up_dx")
        p_up = _mm_tn(s["x1b"], dh3, d, f // 2, "ffn_up_dw")
        ffn_partials = ((ffn_keys[0], p_up, "col"), (ffn_keys[1], p_down, "lead"))
        early = [(key, pair_sum(p, kind), kind) for key, p, kind in ffn_partials] if i == 0 else []
        early_scatter = [(chip_sum, kind) for _, chip_sum, kind in early]
        dz1, dz1b, dg, db = _ln_bwd(dx1, s["z1"], full["ln_g"][i, 0][None], "ln_bwd")
        gp["ln_g"][i][0] = dg[0]
        gp["ln_b"][i][0] = db[0]
        gp["ln_g"][i] = jnp.stack(gp["ln_g"][i])
        gp["ln_b"][i] = jnp.stack(gp["ln_b"][i])
        if i % 2 == 0:
            dq = _mm_nt(dz1b, wt("sc_w_out", j), 0, d, "sc_out_dx")
            p_out = _mm_tn(s["q"], dz1b[None], d, d, "sc_out_dw")
            dh3, dcw, dcb = _sc_bwd(s["h"], dq, full["sc_conv_w"][j], full["sc_conv_b"][j][None], seq, "sc_bwd")
            gp["sc_conv_w"][j] = dcw
            gp["sc_conv_b"][j] = dcb[0]
            res = _mm_nt_res(dh3, wt("sc_w_in", j), 0, dz1, d, "sc_in_dx", scatter=early_scatter)
            dcur, lands = (res[0], res[1:]) if early else (res, [])
            landed.update({key: land for (key, _, _), land in zip(early, lands)})
            p_in = _mm_tn(s["xb"], dh3, d, d, "sc_in_dw")
        else:
            dq = _mm_nt(dz1b, wt("lru_w_out", j), 0, rw, "lru_out_dx")
            p_out = _mm_tn(s["q"], dz1b[None], rw, d, "lru_out_dw")
            dh3, dbin, dcw, dcb, dwg, dbg, dlam = _lru_bwd(
                s["h"], s["hs"], dq, full["lru_conv_w"][j], full["lru_conv_b"][j][None], wg_full[j],
                full["lru_b_gate"][j], full["lru_lambda"][j][None], seq, "lru_bwd")
            gp["lru_b_in"][j] = dbin[0]
            gp["lru_conv_w"][j] = dcw
            gp["lru_conv_b"][j] = dcb[0]
            gp["lru_w_gate"][j] = dwg
            gp["lru_b_gate"][j] = dbg
            gp["lru_lambda"][j] = dlam[0]
            dcur = _mm_nt_res(dh3, wt("lru_w_in", j), 0, dz1, rw, "lru_in_dx")
            p_in = _mm_tn(s["xb"], dh3, d, rw, "lru_in_dw")
        layer_partials = ((mixer_keys[0], p_in, "col"), (mixer_keys[1], p_out.reshape(N_CHIPS, -1, d), "lead"))
        layer_partials += () if early else ffn_partials
        pending = [(key, pair_sum(p, kind), kind) for key, p, kind in layer_partials]
    grad_x = dcur.reshape(bd, seq, d)
    gp = {n: jnp.stack(v) for n, v in gp.items()}

    gate = gp["lru_w_gate"]
    gate = jnp.moveaxis(gate.reshape(gate.shape[:-1] + (N_CHIPS, gate_shape[-1])), -2, 0)
    gate = gate.astype(BF16).reshape(N_CHIPS, -1, gate_shape[-1])
    pending += [("gate", pair_sum(gate, "lead"), "lead"),
                ("pack", pair_sum(_pack_slots(gp, small_shapes), "lead"), "lead")]
    lands = _rs_scatter([(chip_sum, kind) for _, chip_sum, kind in pending])
    landed.update({key: land for (key, _, _), land in zip(pending, lands)})

    def update(keys, w3, m3, v3):
        g_mine = jnp.stack([_add_chips(landed[k], "rs_add_chips") for k in keys])
        g_sib = _d2d_stream(g_mine.reshape(1, 1, -1, g_mine.shape[-1]), False, "rs_share").reshape(g_mine.shape)
        return _adamw(w3, g_mine, g_sib, m3, v3, "adamw")

    g_out, d_out, m_out, v_out = {}, {}, {}, {}
    for n in GATHER_KIND:
        outs = update([(n, l) for l in range(w[n].shape[0])], w[n], mom[n], vel[n])
        g_out[n], d_out[n], m_out[n], v_out[n] = outs
    as_rows = lambda a: a.reshape(1, -1, a.shape[-1])
    outs = update(["gate"], as_rows(w["lru_w_gate"]), as_rows(mom["lru_w_gate"]), as_rows(vel["lru_w_gate"]))
    g_out["lru_w_gate"], d_out["lru_w_gate"], m_out["lru_w_gate"], v_out["lru_w_gate"] = (
        o.reshape(gate_shape) for o in outs)
    packs = update(["pack"], w_pack[None], _pack_local(mom, small_shapes)[None], _pack_local(vel, small_shapes)[None])
    for dst, pack in zip((g_out, d_out, m_out, v_out), packs):
        dst.update(_unpack_local(pack[0], small_shapes))

    return (loss, grad_x, *[g_out[n] for n in WEIGHTS], *[d_out[n] for n in WEIGHTS],
            *[m_out[n] for n in WEIGHTS], *[v_out[n] for n in WEIGHTS])
```

```python
import math

import jax
import jax.numpy as jnp
from jax import lax
from jax.experimental import pallas as pl
from jax.experimental.pallas import tpu as pltpu

F32 = jnp.float32
BF16 = jnp.bfloat16

DEPTH = 4
LRU_HEADS = 10
LRU_BLOCK = 128
LRU_C = 8.0
LN_EPS = 1e-5
ALPHA = (2.0 * DEPTH) ** 0.25
ADAM_LR, ADAM_B1, ADAM_B2, ADAM_EPS, ADAM_WD, ADAM_STEP = 0.001, 0.9, 0.999, 1e-08, 0.01, 10
N_CHIPS = 4
MESH_AXES = ("x", "y", "c")

VMEM_LIMIT_BYTES = 48 * 1024 * 1024
TM_MM = 512
TM_RES = 1024
TT_MM = 2048
TM_SC = 256
FFN_COL_BLOCKS = 2
RC = 128
LANE = 128
MXU_COLS = 768
TS_LRU = 256
TM_LN = 512
ELEMS_PER_BLOCK = 256 * 1024
STREAM_ELEMS_PER_BLOCK = 1024 * 1024
SUB = 8
SUB16 = 16


def _cp(*sem):
    return pltpu.CompilerParams(dimension_semantics=sem, vmem_limit_bytes=VMEM_LIMIT_BYTES)


def _sigmoid(v):
    return 0.5 + 0.5 * jnp.tanh(0.5 * v)


def _softplus(v):
    e = jnp.exp(-jnp.abs(v))
    log1p = jnp.where(e < 1e-3, e * (1.0 - e * (0.5 - e * (1.0 / 3.0))), jnp.log(1.0 + e))
    return jnp.maximum(v, 0.0) + log1p


def _one_minus_exp(v):
    series = -v * (1.0 + v * (0.5 + v * (1.0 / 6.0 + v * (1.0 / 24.0))))
    return jnp.where(v > -0.02, series, 1.0 - jnp.exp(v))


def _gelu_and_grad(v):
    k = math.sqrt(2.0 / math.pi)
    t = jnp.tanh(k * (v + 0.044715 * v * v * v))
    val = 0.5 * v * (1.0 + t)
    grad = 0.5 * (1.0 + t) + 0.5 * v * (1.0 - t * t) * k * (1.0 + 3.0 * 0.044715 * v * v)
    return val, grad


def _down(ext, k, n_head):
    if k:
        ext = pltpu.roll(ext, k, 0)
    return ext[n_head:]


def _up(ext, k, n):
    if k:
        ext = pltpu.roll(ext, ext.shape[0] - k, 0)
    return ext[:n]


def _row(ref, k):
    return ref[k:k + 1, :]


def _colsum(v):
    return jnp.sum(v, axis=0, keepdims=True)


def _mm_nn(a, w3, l, bias, tn, name):
    m, k = a.shape
    n = w3.shape[2]
    tm = min(TM_MM, m)
    has_bias = bias is not None

    def body(*refs):
        if has_bias:
            a_ref, w_ref, b_ref, o_ref = refs
        else:
            a_ref, w_ref, o_ref = refs
        acc = jnp.dot(a_ref[...], w_ref[...], preferred_element_type=F32)
        if has_bias:
            acc = acc + b_ref[...]
        o_ref[...] = acc.astype(o_ref.dtype)

    in_specs = [pl.BlockSpec((tm, k), lambda i, j: (i, 0)),
                pl.BlockSpec((None, k, tn), lambda i, j: (l, 0, j))]
    args = [a, w3]
    if has_bias:
        in_specs.append(pl.BlockSpec((1, tn), lambda i, j: (0, j)))
        args.append(bias)
    return pl.pallas_call(
        body, name=name, grid=(m // tm, n // tn), in_specs=in_specs,
        out_specs=pl.BlockSpec((tm, tn), lambda i, j: (i, j)),
        out_shape=jax.ShapeDtypeStruct((m, n), BF16),
        compiler_params=_cp("parallel", "arbitrary"))(*args)


def _mm_nn_ln(a, w3, l, xres, g, b, name, rider=()):
    m, k = a.shape
    n = w3.shape[2]
    tm = min(TM_MM, m)
    bufs, rkinds, rspecs, rshapes, rsems = _rider_args(rider)
    nr = len(bufs)

    def body(*refs):
        a_ref, w_ref, x_ref, g_ref, b_ref = refs[:5]
        z_ref, xn_ref, xb_ref = refs[5 + nr:8 + nr]
        rout, sems = refs[8 + nr:8 + 2 * nr], refs[8 + 2 * nr:]
        if nr:
            @pl.when(pl.program_id(0) == 0)
            def _():
                _rider_start(rout, rkinds, *sems)

        y = jnp.dot(a_ref[...], w_ref[...], preferred_element_type=F32)
        z = ALPHA * x_ref[...] + y
        mu = jnp.mean(z, axis=-1, keepdims=True)
        zc = z - mu
        var = jnp.mean(zc * zc, axis=-1, keepdims=True)
        xn = zc * lax.rsqrt(var + LN_EPS) * g_ref[...] + b_ref[...]
        z_ref[...] = z
        xn_ref[...] = xn
        xb_ref[...] = xn.astype(BF16)
        if nr:
            @pl.when(pl.program_id(0) == m // tm - 1)
            def _():
                _rider_wait(rout, rkinds, *sems)

    row = pl.BlockSpec((tm, n), lambda i: (i, 0))
    vec = pl.BlockSpec((1, n), lambda i: (0, 0))
    return pl.pallas_call(
        body, name=name, grid=(m // tm,),
        in_specs=[pl.BlockSpec((tm, k), lambda i: (i, 0)),
                  pl.BlockSpec((None, k, n), lambda i: (l, 0, 0)), row, vec, vec] + rspecs,
        out_specs=[row, row, row] + rspecs,
        out_shape=[jax.ShapeDtypeStruct((m, n), F32), jax.ShapeDtypeStruct((m, n), F32),
                   jax.ShapeDtypeStruct((m, n), BF16)] + rshapes,
        input_output_aliases={5 + u: 3 + u for u in range(nr)},
        scratch_shapes=rsems,
        compiler_params=_cp("arbitrary"))(a, w3, xres, g, b, *bufs)


def _mm_nt(a, w3, l, tk, name):
    m, n = a.shape
    kd = w3.shape[1]
    tm = min(TM_MM, m)

    def body(a_ref, w_ref, o_ref):
        o_ref[...] = lax.dot_general(a_ref[...], w_ref[...], (((1,), (1,)), ((), ())),
                                     preferred_element_type=F32).astype(o_ref.dtype)

    return pl.pallas_call(
        body, name=name, grid=(m // tm, kd // tk),
        in_specs=[pl.BlockSpec((tm, n), lambda i, j: (i, 0)),
                  pl.BlockSpec((None, tk, n), lambda i, j: (l, j, 0))],
        out_specs=pl.BlockSpec((tm, tk), lambda i, j: (i, j)),
        out_shape=jax.ShapeDtypeStruct((m, kd), BF16),
        compiler_params=_cp("parallel", "arbitrary"))(a, w3)


def _mm_nt_res(dh3, w3, l, dz, tc, name, scatter=()):
    g, m, cg = dh3.shape
    kd = w3.shape[1]
    ncg = cg // tc
    nk = g * ncg
    tm = min(TM_RES, m)
    srcs, skinds, sspecs, sshapes, ssems = _scatter_args(scatter)
    ns = len(srcs)

    def body(*refs):
        a_ref, w_ref, dz_ref = refs[:3]
        sin, o_ref, lands = refs[3:3 + ns], refs[3 + ns], refs[4 + ns:4 + 2 * ns]
        acc, sems = refs[4 + 2 * ns], refs[5 + 2 * ns:]
        i = pl.program_id(0)
        k = pl.program_id(1)
        if ns:
            @pl.when(jnp.logical_and(i == 0, k == 0))
            def _():
                for cp in _scatter_copies(sin, lands, skinds, *sems):
                    cp.start()

        @pl.when(k == 0)
        def _():
            acc[...] = ALPHA * dz_ref[...]

        acc[...] += lax.dot_general(a_ref[...], w_ref[...], (((1,), (1,)), ((), ())),
                                    preferred_element_type=F32)

        @pl.when(k == nk - 1)
        def _():
            o_ref[...] = acc[...]

        if ns:
            @pl.when(jnp.logical_and(i == m // tm - 1, k == nk - 1))
            def _():
                for cp in _scatter_copies(sin, lands, skinds, *sems):
                    cp.wait()

    outs = pl.pallas_call(
        body, name=name, grid=(m // tm, nk),
        in_specs=[pl.BlockSpec((None, tm, tc), lambda i, k: (k // ncg, i, k % ncg)),
                  pl.BlockSpec((None, kd, tc), lambda i, k: (l, 0, k)),
                  pl.BlockSpec((tm, kd), lambda i, k: (i, 0))] + sspecs,
        out_specs=[pl.BlockSpec((tm, kd), lambda i, k: (i, 0))] + sspecs,
        out_shape=[jax.ShapeDtypeStruct((m, kd), F32)] + sshapes,
        scratch_shapes=[pltpu.VMEM((tm, kd), F32)] + ssems,
        compiler_params=_cp("arbitrary", "arbitrary"))(dh3, w3, dz, *srcs)
    return outs if ns else outs[0]


def _mm_tn(a, b3, tka, tnb, name):
    t, ka = a.shape
    g, _, cg = b3.shape
    ncg = cg // tnb
    tt = min(TT_MM, t)
    nt = t // tt

    def body(a_ref, b_ref, o_ref, acc):
        s = pl.program_id(2)

        @pl.when(s == 0)
        def _():
            acc[...] = jnp.zeros_like(acc)

        acc[...] += lax.dot_general(a_ref[...], b_ref[...], (((0,), (0,)), ((), ())),
                                    preferred_element_type=F32)

        @pl.when(s == nt - 1)
        def _():
            o_ref[...] = acc[...].astype(o_ref.dtype)

    return pl.pallas_call(
        body, name=name, grid=(ka // tka, g * ncg, nt),
        in_specs=[pl.BlockSpec((tt, tka), lambda i, j, s: (s, i)),
                  pl.BlockSpec((None, tt, tnb), lambda i, j, s: (j // ncg, s, j % ncg))],
        out_specs=pl.BlockSpec((tka, tnb), lambda i, j, s: (i, j)),
        out_shape=jax.ShapeDtypeStruct((ka, g * cg), BF16),
        scratch_shapes=[pltpu.VMEM((tka, tnb), F32)],
        compiler_params=_cp("parallel", "parallel", "arbitrary"))(a, b3)


def _ln_bwd(dxn, z, g, name, target=None):
    m, d = z.shape
    tm = min(TM_LN, m)
    with_loss = target is not None

    def body(*refs):
        dx_ref, z_ref, g_ref = refs[:3]
        dz_ref, dzb_ref, dg_ref, db_ref = refs[3 + with_loss:7 + with_loss]

        @pl.when(pl.program_id(0) == 0)
        def _():
            for r in refs[5 + with_loss:]:
                r[...] = jnp.zeros_like(r)

        zz = z_ref[...]
        dx = dx_ref[...]
        if with_loss:
            err = dx - refs[3][...]
            dx = err * (1.0 / d)
            refs[8][...] += _colsum(err * err) * (0.5 / d)
        mu = jnp.mean(zz, axis=-1, keepdims=True)
        zc = zz - mu
        var = jnp.mean(zc * zc, axis=-1, keepdims=True)
        rstd = lax.rsqrt(var + LN_EPS)
        xh = zc * rstd
        dg_ref[...] += _colsum(dx * xh)
        db_ref[...] += _colsum(dx)
        dxh = dx * g_ref[...]
        m1 = jnp.mean(dxh, axis=-1, keepdims=True)
        m2 = jnp.mean(dxh * xh, axis=-1, keepdims=True)
        dz = rstd * (dxh - m1 - xh * m2)
        dz_ref[...] = dz
        dzb_ref[...] = dz.astype(BF16)

    row = pl.BlockSpec((tm, d), lambda i: (i, 0))
    vec = pl.BlockSpec((1, d), lambda i: (0, 0))
    vec_shape = jax.ShapeDtypeStruct((1, d), F32)
    return pl.pallas_call(
        body, name=name, grid=(m // tm,), in_specs=[row, row, vec] + [row] * with_loss,
        out_specs=[row, row, vec, vec] + [vec] * with_loss,
        out_shape=[jax.ShapeDtypeStruct((m, d), F32), jax.ShapeDtypeStruct((m, d), BF16), vec_shape, vec_shape]
        + [vec_shape] * with_loss,
        compiler_params=_cp("arbitrary"))(dxn, z, g, *([target] if with_loss else []))


def _sc_fwd(h, cw, cb, seq, name):
    t, d3 = h.shape
    d = d3 // 3
    tm = min(TM_SC, seq)

    def body(hb_ref, hc_ref, hv_ref, cw_ref, cb_ref, q_ref, carry):
        i = pl.program_id(0)

        @pl.when(lax.rem(i * tm, seq) == 0)
        def _():
            carry[...] = jnp.zeros_like(carry)

        p = hc_ref[...].astype(F32) * hv_ref[...].astype(F32)
        ext = jnp.concatenate([carry[...], p], axis=0)
        u = cb_ref[...] + _row(cw_ref, 0) * _down(ext, 2, SUB) + _row(cw_ref, 1) * _down(ext, 1, SUB) \
            + _row(cw_ref, 2) * p
        q_ref[...] = (hb_ref[...].astype(F32) * u).astype(BF16)
        carry[...] = p[tm - SUB:, :]

    blk = lambda c: pl.BlockSpec((tm, d), lambda i: (i, c))
    return pl.pallas_call(
        body, name=name, grid=(t // tm,),
        in_specs=[blk(0), blk(1), blk(2), pl.BlockSpec((3, d), lambda i: (0, 0)),
                  pl.BlockSpec((1, d), lambda i: (0, 0))],
        out_specs=pl.BlockSpec((tm, d), lambda i: (i, 0)),
        out_shape=jax.ShapeDtypeStruct((t, d), BF16),
        scratch_shapes=[pltpu.VMEM((SUB, d), F32)],
        compiler_params=_cp("arbitrary"))(h, h, h, cw, cb)


def _sc_bwd(h, dq, cw, cb, seq, name):
    t, d3 = h.shape
    d = d3 // 3
    tm = min(TM_SC, seq)
    nt = t // tm
    hpt = tm // SUB16

    def body(hb_ref, hc_ref, hv_ref, hch_ref, hvh_ref, dq_ref, cw_ref, cb_ref,
             dh_ref, dcw_ref, dcb_ref, carry):
        i = pl.program_id(0)
        ri = nt - 1 - i

        @pl.when(i == 0)
        def _():
            dcw_ref[...] = jnp.zeros_like(dcw_ref)
            dcb_ref[...] = jnp.zeros_like(dcb_ref)

        @pl.when(lax.rem((ri + 1) * tm, seq) == 0)
        def _():
            carry[...] = jnp.zeros_like(carry)

        keep = jnp.where(lax.rem(ri * tm, seq) == 0, 0.0, 1.0)
        gb = hb_ref[...].astype(F32)
        gc = hc_ref[...].astype(F32)
        v = hv_ref[...].astype(F32)
        p = gc * v
        p_head = hch_ref[...].astype(F32) * hvh_ref[...].astype(F32) * keep
        ext = jnp.concatenate([p_head, p], axis=0)
        pm2 = _down(ext, 2, SUB16)
        pm1 = _down(ext, 1, SUB16)
        u = cb_ref[...] + _row(cw_ref, 0) * pm2 + _row(cw_ref, 1) * pm1 + _row(cw_ref, 2) * p
        dqf = dq_ref[...].astype(F32)
        du = dqf * gb
        dcb_ref[...] += _colsum(du)
        dcw_ref[0:1, :] += _colsum(du * pm2)
        dcw_ref[1:2, :] += _colsum(du * pm1)
        dcw_ref[2:3, :] += _colsum(du * p)
        ext2 = jnp.concatenate([du, carry[...]], axis=0)
        dp = _row(cw_ref, 2) * du + _row(cw_ref, 1) * _up(ext2, 1, tm) + _row(cw_ref, 0) * _up(ext2, 2, tm)
        carry[...] = du[0:SUB, :]
        dh_ref[0] = (dqf * u).astype(BF16)
        dh_ref[1] = (dp * v).astype(BF16)
        dh_ref[2] = (dp * gc).astype(BF16)

    blk = lambda c: pl.BlockSpec((tm, d), lambda i: (nt - 1 - i, c))
    head = lambda c: pl.BlockSpec((SUB16, d), lambda i: (jnp.maximum((nt - 1 - i) * hpt - 1, 0), c))
    vec = lambda r: pl.BlockSpec((r, d), lambda i: (0, 0))
    return pl.pallas_call(
        body, name=name, grid=(nt,),
        in_specs=[blk(0), blk(1), blk(2), head(1), head(2),
                  pl.BlockSpec((tm, d), lambda i: (nt - 1 - i, 0)), vec(3), vec(1)],
        out_specs=[pl.BlockSpec((3, tm, d), lambda i: (0, nt - 1 - i, 0)), vec(3), vec(1)],
        out_shape=[jax.ShapeDtypeStruct((3, t, d), BF16), jax.ShapeDtypeStruct((3, d), F32),
                   jax.ShapeDtypeStruct((1, d), F32)],
        scratch_shapes=[pltpu.VMEM((SUB, d), F32)],
        compiler_params=_cp("arbitrary"))(h, h, h, h, h, dq, cw, cb)


def _fold(v):
    return jnp.sum(v.reshape(v.shape[0] // SUB, SUB, v.shape[1]), axis=0)


def _ffn_up(xb, w3, l, cw, cb, seq, name, rider=()):
    t, d = xb.shape
    f = w3.shape[2] // 2
    tc = f // FFN_COL_BLOCKS
    tm = min(TM_MM, seq)
    nc = FFN_COL_BLOCKS
    bufs, rkinds, rspecs, rshapes, rsems = _rider_args(rider)
    nr = len(bufs)

    def body(*refs):
        x_ref, wg_ref, wv_ref, cwg_ref, cwv_ref, cbg_ref, cbv_ref = refs[:7]
        h_ref, pre_ref, act_ref = refs[7 + nr:10 + nr]
        rout = refs[10 + nr:10 + 2 * nr]
        eg, ev = refs[10 + 2 * nr:12 + 2 * nr]
        sems = refs[12 + 2 * nr:]
        i = pl.program_id(1)
        if nr:
            @pl.when(jnp.logical_and(pl.program_id(0) == 0, i == 0))
            def _():
                _rider_start(rout, rkinds, *sems)

        @pl.when(lax.rem(i * tm, seq) == 0)
        def _():
            eg[0:SUB, :] = jnp.zeros((SUB, tc), F32)
            ev[0:SUB, :] = jnp.zeros((SUB, tc), F32)

        xx = x_ref[...]

        def matmul(lo, hi):
            eg[SUB:, lo:hi] = jnp.dot(xx, wg_ref[:, lo:hi], preferred_element_type=F32)
            ev[SUB:, lo:hi] = jnp.dot(xx, wv_ref[:, lo:hi], preferred_element_type=F32)

        def epilogue(lo, hi):
            for c0 in range(lo, hi, LANE):
                cols = slice(c0, c0 + LANE)
                taps = [[ref[k:k + 1, cols] for k in range(3)] + [bref[:, cols]]
                        for ref, bref in ((cwg_ref, cbg_ref), (cwv_ref, cbv_ref))]
                for r0 in range(0, tm, RC):
                    rows = slice(r0, r0 + RC)
                    pres = []
                    for half, e_ref in enumerate((eg, ev)):
                        w0, w1, w2, bias = taps[half]
                        e = e_ref[r0:r0 + RC + SUB, cols]
                        cur = e[SUB:]
                        pre = bias + w0 * _down(e, 2, SUB) + w1 * _down(e, 1, SUB) + w2 * cur
                        h_ref[half, rows, cols] = cur.astype(BF16)
                        pre_ref[half, rows, cols] = pre.astype(BF16)
                        pres.append(pre)
                    act_ref[rows, cols] = (pres[0] * _sigmoid(pres[0]) * pres[1]).astype(BF16)

        blocks = [(lo, min(lo + MXU_COLS, tc)) for lo in range(0, tc, MXU_COLS)]
        matmul(*blocks[0])
        for b, blk in enumerate(blocks):
            if b + 1 < len(blocks):
                matmul(*blocks[b + 1])
            epilogue(*blk)
        eg[0:SUB, :] = eg[tm:tm + SUB, :]
        ev[0:SUB, :] = ev[tm:tm + SUB, :]
        if nr:
            @pl.when(jnp.logical_and(pl.program_id(0) == nc - 1, i == t // tm - 1))
            def _():
                _rider_wait(rout, rkinds, *sems)

    wspec = lambda off: pl.BlockSpec((None, d, tc), lambda j, i: (l, 0, j + off))
    vec = lambda r, off: pl.BlockSpec((r, tc), lambda j, i: (0, j + off))
    pair = pl.BlockSpec((2, tm, tc), lambda j, i: (0, i, j))
    return pl.pallas_call(
        body, name=name, grid=(nc, t // tm),
        in_specs=[pl.BlockSpec((tm, d), lambda j, i: (i, 0)), wspec(0), wspec(nc),
                  vec(3, 0), vec(3, nc), vec(1, 0), vec(1, nc)] + rspecs,
        out_specs=[pair, pair, pl.BlockSpec((tm, tc), lambda j, i: (i, j))] + rspecs,
        out_shape=[jax.ShapeDtypeStruct((2, t, f), BF16), jax.ShapeDtypeStruct((2, t, f), BF16),
                   jax.ShapeDtypeStruct((t, f), BF16)] + rshapes,
        input_output_aliases={7 + u: 3 + u for u in range(nr)},
        scratch_shapes=[pltpu.VMEM((tm + SUB, tc), F32), pltpu.VMEM((tm + SUB, tc), F32)] + rsems,
        compiler_params=_cp("arbitrary", "arbitrary"))(xb, w3, w3, cw, cw, cb, cb, *bufs)


def _ffn_down_bwd(dzb, wd3, l, h3, pre3, cw, seq, name, scatter=()):
    t, d = dzb.shape
    f = wd3.shape[1]
    tc = f // FFN_COL_BLOCKS
    tm = min(TM_MM, seq)
    nt = t // tm
    nc = FFN_COL_BLOCKS
    srcs, skinds, sspecs, sshapes, ssems = _scatter_args(scatter)
    ns = len(srcs)

    def body(*refs):
        dz_ref, wd_ref, h_ref, pre_ref, cwg_ref, cwv_ref = refs[:6]
        sin = refs[6:6 + ns]
        dh_ref, dcwg_ref, dcwv_ref, dcbg_ref, dcbv_ref = refs[6 + ns:11 + ns]
        lands = refs[11 + ns:11 + 2 * ns]
        da_s, carry = refs[11 + 2 * ns:13 + 2 * ns]
        sems = refs[13 + 2 * ns:]
        i = pl.program_id(1)
        ri = nt - 1 - i
        if ns:
            @pl.when(jnp.logical_and(pl.program_id(0) == 0, i == 0))
            def _():
                for cp in _scatter_copies(sin, lands, skinds, *sems):
                    cp.start()

        @pl.when(i == 0)
        def _():
            for r in (dcwg_ref, dcwv_ref, dcbg_ref, dcbv_ref):
                r[...] = jnp.zeros_like(r)

        @pl.when(lax.rem((ri + 1) * tm, seq) == 0)
        def _():
            carry[...] = jnp.zeros_like(carry)

        dz = dz_ref[...]

        def matmul(lo, hi):
            da_s[:, lo:hi] = lax.dot_general(dz, wd_ref[lo:hi, :], (((1,), (1,)), ((), ())),
                                             preferred_element_type=F32)

        def epilogue(lo, hi):
            for c0 in range(lo, hi, LANE):
                cols = slice(c0, c0 + LANE)
                taps = [[ref[k:k + 1, cols] for k in range(3)] for ref in (cwg_ref, cwv_ref)]
                acc = [jnp.zeros((SUB, LANE), F32)] * 8
                for r0 in range(tm - RC, -1, -RC):
                    rows = slice(r0, r0 + RC)
                    da = da_s[rows, cols]
                    gp = pre_ref[0, rows, cols].astype(F32)
                    vp = pre_ref[1, rows, cols].astype(F32)
                    sg = _sigmoid(gp)
                    dpres = (da * vp * (sg * (1.0 + gp * (1.0 - sg))), da * (gp * sg))
                    for half in range(2):
                        w0, w1, w2 = taps[half]
                        dpre = dpres[half]
                        ext = jnp.concatenate([dpre, carry[half, :, cols]], axis=0)
                        u1 = _up(ext, 1, RC)
                        u2 = _up(ext, 2, RC)
                        carry[half, :, cols] = dpre[0:SUB]
                        dh_ref[half, rows, cols] = (w2 * dpre + w1 * u1 + w0 * u2).astype(BF16)
                        hh = h_ref[half, rows, cols].astype(F32)
                        for k, term in enumerate((hh * u2, hh * u1, hh * dpre, dpre)):
                            acc[4 * half + k] = acc[4 * half + k] + _fold(term)
                for half, (dcw_ref, dcb_ref) in enumerate(((dcwg_ref, dcbg_ref), (dcwv_ref, dcbv_ref))):
                    for k in range(3):
                        dcw_ref[k:k + 1, cols] += _colsum(acc[4 * half + k])
                    dcb_ref[:, cols] += _colsum(acc[4 * half + 3])

        blocks = [(lo, min(lo + MXU_COLS, tc)) for lo in range(0, tc, MXU_COLS)]
        matmul(*blocks[0])
        for b, blk in enumerate(blocks):
            if b + 1 < len(blocks):
                matmul(*blocks[b + 1])
            epilogue(*blk)
        if ns:
            @pl.when(jnp.logical_and(pl.program_id(0) == nc - 1, i == nt - 1))
            def _():
                for cp in _scatter_copies(sin, lands, skinds, *sems):
                    cp.wait()

    pair = pl.BlockSpec((2, tm, tc), lambda j, i: (0, nt - 1 - i, j))
    vec = lambda off: pl.BlockSpec((3, tc), lambda j, i: (0, j + off))
    acc_spec = lambda r: pl.BlockSpec((r, tc), lambda j, i: (0, j))
    return pl.pallas_call(
        body, name=name, grid=(nc, nt),
        in_specs=[pl.BlockSpec((tm, d), lambda j, i: (nt - 1 - i, 0)),
                  pl.BlockSpec((None, tc, d), lambda j, i: (l, j, 0)), pair, pair, vec(0), vec(nc)] + sspecs,
        out_specs=[pair, acc_spec(3), acc_spec(3), acc_spec(1), acc_spec(1)] + sspecs,
        out_shape=[jax.ShapeDtypeStruct((2, t, f), BF16), jax.ShapeDtypeStruct((3, f), F32),
                   jax.ShapeDtypeStruct((3, f), F32), jax.ShapeDtypeStruct((1, f), F32),
                   jax.ShapeDtypeStruct((1, f), F32)] + sshapes,
        scratch_shapes=[pltpu.VMEM((tm, tc), F32), pltpu.VMEM((2, SUB, tc), F32)] + ssems,
        compiler_params=_cp("arbitrary", "arbitrary"))(dzb, wd3, h3, pre3, cw, cw, *srcs)


def _lru_gates(xr, wg_ref, bg_ref):
    rs, gs = [], []
    for hd in range(LRU_HEADS):
        xh = xr[:, hd * LRU_BLOCK:(hd + 1) * LRU_BLOCK].astype(BF16)
        gt = jnp.dot(xh, wg_ref[hd], preferred_element_type=F32) + _row(bg_ref, hd)
        rs.append(gt[:, :LRU_BLOCK])
        gs.append(gt[:, LRU_BLOCK:])
    return jnp.concatenate(rs, axis=1), jnp.concatenate(gs, axis=1)


def _lru_coeffs(xr, wg_ref, bg_ref, lam_ref):
    gr, gi = _lru_gates(xr, wg_ref, bg_ref)
    r = _sigmoid(gr)
    ig = _sigmoid(gi)
    sp = _softplus(-lam_ref[...])
    log_a = -LRU_C * r * sp
    a = jnp.exp(log_a)
    mult = jnp.sqrt(_one_minus_exp(2.0 * log_a))
    return r, ig, sp, a, mult


def _lru_fwd(h, cw, cb, wg, bg, lam, seq, name):
    t, r2 = h.shape
    rw = r2 // 2
    ts = min(TS_LRU, seq)
    n8 = ts // SUB

    def body(hg_ref, hr_ref, cw_ref, cb_ref, wg_ref, bg_ref, lam_ref, hs_ref, y_ref,
             a_s, b_s, cconv, cstate):
        i = pl.program_id(0)

        @pl.when(lax.rem(i * ts, seq) == 0)
        def _():
            cconv[...] = jnp.zeros_like(cconv)
            cstate[...] = jnp.zeros_like(cstate)

        rin = hr_ref[...].astype(F32)
        ext = jnp.concatenate([cconv[...], rin], axis=0)
        xr = cb_ref[...]
        for k in range(4):
            xr = xr + _row(cw_ref, k) * _down(ext, 3 - k, SUB)
        cconv[...] = rin[ts - SUB:, :]
        _, ig, _, a, mult = _lru_coeffs(xr, wg_ref, bg_ref, lam_ref)
        a_s[...] = a
        b_s[...] = mult * (ig * xr)
        row = lax.broadcasted_iota(jnp.int32, (SUB, rw), 0)

        def step(j, carry):
            off = pl.multiple_of(j * SUB, SUB)
            a8 = a_s[pl.ds(off, SUB), :]
            b8 = b_s[pl.ds(off, SUB), :]
            for d in (1, 2, 4):
                m = row >= d
                b8 = jnp.where(m, a8 * pltpu.roll(b8, d, 0) + b8, b8)
                a8 = jnp.where(m, a8 * pltpu.roll(a8, d, 0), a8)
            h8 = a8 * carry + b8
            hs_ref[pl.ds(off, SUB), :] = h8
            return _colsum(jnp.where(row == SUB - 1, h8, 0.0))

        cstate[...] = lax.fori_loop(0, n8, step, cstate[...])
        gel, _ = _gelu_and_grad(hg_ref[...].astype(F32))
        y_ref[...] = (hs_ref[...] * gel).astype(BF16)

    full = lambda shp: pl.BlockSpec(shp, lambda i: (0,) * len(shp))
    return pl.pallas_call(
        body, name=name, grid=(t // ts,),
        in_specs=[pl.BlockSpec((ts, rw), lambda i: (i, 0)), pl.BlockSpec((ts, rw), lambda i: (i, 1)),
                  full((4, rw)), full((1, rw)), full(wg.shape), full(bg.shape), full((1, rw))],
        out_specs=[pl.BlockSpec((ts, rw), lambda i: (i, 0)), pl.BlockSpec((ts, rw), lambda i: (i, 0))],
        out_shape=[jax.ShapeDtypeStruct((t, rw), F32), jax.ShapeDtypeStruct((t, rw), BF16)],
        scratch_shapes=[pltpu.VMEM((ts, rw), F32), pltpu.VMEM((ts, rw), F32),
                        pltpu.VMEM((SUB, rw), F32), pltpu.VMEM((1, rw), F32)],
        compiler_params=_cp("arbitrary"))(h, h, cw, cb, wg, bg, lam)


def _lru_bwd(h, hs, dy, cw, cb, wg, bg, lam, seq, name):
    t, r2 = h.shape
    rw = r2 // 2
    ts = min(TS_LRU, seq)
    nt = t // ts
    n8 = ts // SUB
    hp16 = ts // SUB16
    hp8 = ts // SUB

    def body(hg_ref, hr_ref, hrh_ref, hs_ref, hsh_ref, dy_ref, cw_ref, cb_ref, wg_ref, bg_ref, lam_ref,
             dh_ref, dbin_ref, dcw_ref, dcb_ref, dwg_ref, dbg_ref, dlam_ref,
             a_s, g_s, l_s, c_lam, c_a, c_dxr):
        i = pl.program_id(0)
        ri = nt - 1 - i

        @pl.when(i == 0)
        def _():
            for r in (dbin_ref, dcw_ref, dcb_ref, dwg_ref, dbg_ref, dlam_ref):
                r[...] = jnp.zeros_like(r)

        @pl.when(lax.rem((ri + 1) * ts, seq) == 0)
        def _():
            c_lam[...] = jnp.zeros_like(c_lam)
            c_a[...] = jnp.zeros_like(c_a)
            c_dxr[...] = jnp.zeros_like(c_dxr)

        keep = jnp.where(lax.rem(ri * ts, seq) == 0, 0.0, 1.0)
        rin = hr_ref[...].astype(F32)
        ext = jnp.concatenate([hrh_ref[...].astype(F32) * keep, rin], axis=0)
        shifted = [_down(ext, 3 - k, SUB16) for k in range(4)]
        xr = cb_ref[...]
        for k in range(4):
            xr = xr + _row(cw_ref, k) * shifted[k]
        r, ig, sp, a, mult = _lru_coeffs(xr, wg_ref, bg_ref, lam_ref)
        gel, dgel = _gelu_and_grad(hg_ref[...].astype(F32))
        dyf = dy_ref[...].astype(F32)
        hsv = hs_ref[...]
        dg = dyf * hsv * dgel

        a_s[...] = _up(jnp.concatenate([a, c_a[...]], axis=0), 1, ts)
        g_s[...] = dyf * gel
        c_a[...] = a[0:SUB, :]
        row = lax.broadcasted_iota(jnp.int32, (SUB, rw), 0)

        def step(j, carry):
            off = pl.multiple_of((n8 - 1 - j) * SUB, SUB)
            a8 = a_s[pl.ds(off, SUB), :]
            b8 = g_s[pl.ds(off, SUB), :]
            for d in (1, 2, 4):
                m = row < SUB - d
                b8 = jnp.where(m, a8 * pltpu.roll(b8, SUB - d, 0) + b8, b8)
                a8 = jnp.where(m, a8 * pltpu.roll(a8, SUB - d, 0), a8)
            l8 = a8 * carry + b8
            l_s[pl.ds(off, SUB), :] = l8
            return _colsum(jnp.where(row == 0, l8, 0.0))

        c_lam[...] = lax.fori_loop(0, n8, step, c_lam[...])
        lamv = l_s[...]
        hs_prev = _down(jnp.concatenate([hsh_ref[...] * keep, hsv], axis=0), 1, SUB)
        da = lamv * hs_prev
        t1 = lamv * xr
        dmult = t1 * ig
        dig = t1 * mult
        dxr = lamv * mult * ig
        dla = da * a - dmult * (a * a) / mult
        dr = dla * (-LRU_C * sp)
        dlam_ref[...] += _colsum(dla * (-LRU_C) * r) * (-1.0 / (1.0 + jnp.exp(lam_ref[...])))
        dgr = dr * r * (1.0 - r)
        dgi = dig * ig * (1.0 - ig)
        parts = []
        for hd in range(LRU_HEADS):
            sl = slice(hd * LRU_BLOCK, (hd + 1) * LRU_BLOCK)
            dgt = jnp.concatenate([dgr[:, sl], dgi[:, sl]], axis=1)
            dbg_ref[hd:hd + 1, :] += _colsum(dgt)
            dgt16 = dgt.astype(BF16)
            parts.append(lax.dot_general(dgt16, wg_ref[hd], (((1,), (1,)), ((), ())),
                                         preferred_element_type=F32))
            dwg_ref[hd] += lax.dot_general(xr[:, sl].astype(BF16), dgt16, (((0,), (0,)), ((), ())),
                                           preferred_element_type=F32)
        dxr = dxr + jnp.concatenate(parts, axis=1)

        dcb_ref[...] += _colsum(dxr)
        for k in range(4):
            dcw_ref[k:k + 1, :] += _colsum(dxr * shifted[k])
        ext2 = jnp.concatenate([dxr, c_dxr[...]], axis=0)
        drb = _row(cw_ref, 3) * dxr
        for k in range(3):
            drb = drb + _row(cw_ref, k) * _up(ext2, 3 - k, ts)
        c_dxr[...] = dxr[0:SUB, :]
        dh_ref[0] = dg.astype(BF16)
        dh_ref[1] = drb.astype(BF16)
        dbin_ref[:, 0:rw] += _colsum(dg)
        dbin_ref[:, rw:] += _colsum(drb)

    rev = lambda c: pl.BlockSpec((ts, rw), lambda i: (nt - 1 - i, c))
    full = lambda shp: pl.BlockSpec(shp, lambda i: (0,) * len(shp))
    nh = LRU_HEADS
    return pl.pallas_call(
        body, name=name, grid=(nt,),
        in_specs=[rev(0), rev(1),
                  pl.BlockSpec((SUB16, rw), lambda i: (jnp.maximum((nt - 1 - i) * hp16 - 1, 0), 1)),
                  rev(0),
                  pl.BlockSpec((SUB, rw), lambda i: (jnp.maximum((nt - 1 - i) * hp8 - 1, 0), 0)),
                  rev(0), full((4, rw)), full((1, rw)), full(wg.shape), full(bg.shape), full((1, rw))],
        out_specs=[pl.BlockSpec((2, ts, rw), lambda i: (0, nt - 1 - i, 0)), full((1, r2)), full((4, rw)),
                   full((1, rw)), full((nh, LRU_BLOCK, 2 * LRU_BLOCK)), full((nh, 2 * LRU_BLOCK)), full((1, rw))],
        out_shape=[jax.ShapeDtypeStruct((2, t, rw), BF16), jax.ShapeDtypeStruct((1, r2), F32),
                   jax.ShapeDtypeStruct((4, rw), F32), jax.ShapeDtypeStruct((1, rw), F32),
                   jax.ShapeDtypeStruct((nh, LRU_BLOCK, 2 * LRU_BLOCK), F32),
                   jax.ShapeDtypeStruct((nh, 2 * LRU_BLOCK), F32), jax.ShapeDtypeStruct((1, rw), F32)],
        scratch_shapes=[pltpu.VMEM((ts, rw), F32), pltpu.VMEM((ts, rw), F32), pltpu.VMEM((ts, rw), F32),
                        pltpu.VMEM((1, rw), F32), pltpu.VMEM((SUB, rw), F32), pltpu.VMEM((SUB, rw), F32)],
        compiler_params=_cp("arbitrary"))(h, h, h, hs, hs, dy, cw, cb, wg, bg, lam)


def _row_tile(rows, cols, mult, elems=ELEMS_PER_BLOCK):
    cap = max(mult, elems // cols)
    best = None
    for cand in range(mult, min(rows, cap) + 1, mult):
        if rows % cand == 0:
            best = cand
    return best if best is not None else rows


def _core_index():
    return lax.axis_index("c").astype(jnp.int32).reshape(1)


def _chip_index():
    return (2 * lax.axis_index("x") + lax.axis_index("y")).astype(jnp.int32).reshape(1)


def _add_pair(p4, r3, name):
    s, _, rows, cols = p4.shape
    tr = _row_tile(rows, cols, SUB16)

    def body(c_ref, a_ref, b_ref, o_ref):
        o_ref[...] = (a_ref[...].astype(F32) + b_ref[...].astype(F32)).astype(o_ref.dtype)

    blk = pl.BlockSpec((None, tr, cols), lambda k, i, c_ref: (k, i, 0))
    return pl.pallas_call(
        body, name=name,
        grid_spec=pltpu.PrefetchScalarGridSpec(
            num_scalar_prefetch=1, grid=(s, rows // tr),
            in_specs=[pl.BlockSpec((None, None, tr, cols), lambda k, i, c_ref: (k, c_ref[0], i, 0)), blk],
            out_specs=blk),
        out_shape=jax.ShapeDtypeStruct((s, rows, cols), p4.dtype),
        compiler_params=_cp("parallel", "parallel"))(_core_index(), p4, r3)


def _add_chips(r, name):
    shape = r.shape[1:]
    r3 = r.reshape(N_CHIPS, -1, shape[-1])
    _, rows, cols = r3.shape
    tr = _row_tile(rows, cols, SUB16)

    def body(r_ref, o_ref):
        s = r_ref[0].astype(F32) + r_ref[1].astype(F32)
        s = s + r_ref[2].astype(F32)
        o_ref[...] = s + r_ref[3].astype(F32)

    out = pl.pallas_call(body, name=name, grid=(rows // tr,),
                         in_specs=[pl.BlockSpec((N_CHIPS, tr, cols), lambda i: (0, i, 0))],
                         out_specs=pl.BlockSpec((tr, cols), lambda i: (i, 0)),
                         out_shape=jax.ShapeDtypeStruct((rows, cols), F32),
                         compiler_params=_cp("parallel"))(r3)
    return out.reshape(shape)


def _adamw(w3, g_mine, g_sib, m3, v3, name):
    nl, r, cols = w3.shape
    rows = r // 2
    flat = [arr.reshape(nl, 2, rows, cols) for arr in (w3, m3, v3)]
    tr = _row_tile(rows, cols, SUB)

    def body(c_ref, w_ref, gm_ref, gs_ref, m_ref, v_ref, g_ref, d_ref, mo_ref, vo_ref):
        gg = jnp.where(pl.program_id(1) == c_ref[0], gm_ref[...], gs_ref[...])
        m2 = ADAM_B1 * m_ref[...] + (1.0 - ADAM_B1) * gg
        v2 = ADAM_B2 * v_ref[...] + (1.0 - ADAM_B2) * (gg * gg)
        m_hat = m2 / (1.0 - ADAM_B1 ** ADAM_STEP)
        v_hat = v2 / (1.0 - ADAM_B2 ** ADAM_STEP)
        g_ref[...] = gg
        d_ref[...] = -ADAM_LR * (m_hat / (jnp.sqrt(v_hat) + ADAM_EPS) + ADAM_WD * w_ref[...])
        mo_ref[...] = m2
        vo_ref[...] = v2

    blk = pl.BlockSpec((None, None, tr, cols), lambda l, hh, i, c_ref: (l, hh, i, 0))
    gblk = pl.BlockSpec((None, tr, cols), lambda l, hh, i, c_ref: (l, i, 0))
    outs = pl.pallas_call(
        body, name=name,
        grid_spec=pltpu.PrefetchScalarGridSpec(
            num_scalar_prefetch=1, grid=(nl, 2, rows // tr),
            in_specs=[blk, gblk, gblk, blk, blk], out_specs=[blk] * 4),
        out_shape=[jax.ShapeDtypeStruct((nl, 2, rows, cols), F32)] * 4,
        compiler_params=_cp("parallel", "parallel", "parallel"))(_core_index(), flat[0], g_mine, g_sib, flat[1],
                                                                 flat[2])
    return tuple(o.reshape(nl, r, cols) for o in outs)


def _place(src3, layer, kind, dtype, name):
    _, r, c = src3.shape
    tr = _row_tile(r, c, SUB16)
    in_spec = pl.BlockSpec((None, tr, c), lambda i, my_ref: (layer, i, 0))
    if kind == "col":
        out_spec = pl.BlockSpec((tr, c), lambda i, my_ref: (i, my_ref[0]))
        out_shape = (r, N_CHIPS * c)
    else:
        out_spec = pl.BlockSpec((None, tr, c), lambda i, my_ref: (my_ref[0], i, 0))
        out_shape = (N_CHIPS, r, c)

    def body(my_ref, i_ref, o_ref):
        o_ref[...] = i_ref[...].astype(o_ref.dtype)

    return pl.pallas_call(
        body, name=name,
        grid_spec=pltpu.PrefetchScalarGridSpec(num_scalar_prefetch=1, grid=(r // tr,), in_specs=[in_spec],
                                               out_specs=out_spec),
        out_shape=jax.ShapeDtypeStruct(out_shape, dtype),
        compiler_params=_cp("parallel"))(_chip_index(), src3)


def _half(ref, c, h):
    return ref.at[pl.ds(c * h, h)]


def _position():
    x = lax.axis_index("x")
    y = lax.axis_index("y")
    c = lax.axis_index("c")
    return x, y, c


def _peer_chip(x, y, j):
    tx = 1 - x if j & 2 else x
    ty = 1 - y if j & 1 else y
    return tx, ty


def _remote(src, dst, ssem, rsem, dev):
    return pltpu.make_async_remote_copy(src_ref=src, dst_ref=dst, send_sem=ssem, recv_sem=rsem,
                                        device_id=dev, device_id_type=pl.DeviceIdType.MESH)


_ANY = pl.BlockSpec(memory_space=pl.ANY)


def _unit_view(kind, ref, k):
    if kind == "col":
        n = ref.shape[1] // N_CHIPS
        return ref.at[:, pl.ds(pl.multiple_of(k * n, LANE), n)]
    return ref.at[k]


def _all_gather(placed, kinds):
    nt = len(placed)

    def body(*refs):
        outs = refs[nt:2 * nt]
        ssem, rsem = refs[2 * nt:]
        x, y, c = _position()
        my = 2 * x + y
        sib = (x, y, 1 - c)

        def part(t, k, core):
            view = _unit_view(kinds[t], outs[t], k)
            h = view.shape[0] // 2
            return _half(view, core, h)

        sends, fwds = [], []
        for t in range(nt):
            own = part(t, my, c)
            for j in (1, 2, 3):
                tx, ty = _peer_chip(x, y, j)
                cp = _remote(own, own, ssem.at[6 * t + j - 1], rsem.at[6 * t + j - 1], (tx, ty, c))
                cp.start()
                sends.append(cp)
        for t in range(nt):
            for j in (1, 2, 3):
                tx, ty = _peer_chip(x, y, j)
                got = part(t, 2 * tx + ty, c)
                _remote(got, got, ssem.at[6 * t + j - 1], rsem.at[6 * t + j - 1], sib).wait_recv()
                cp = _remote(got, got, ssem.at[6 * t + 2 + j], rsem.at[6 * t + 2 + j], sib)
                cp.start()
                fwds.append(cp)
        for t in range(nt):
            for j in (1, 2, 3):
                tx, ty = _peer_chip(x, y, j)
                other = part(t, 2 * tx + ty, 1 - c)
                _remote(other, other, ssem.at[6 * t + 2 + j], rsem.at[6 * t + 2 + j], sib).wait_recv()
        for cp in sends + fwds:
            cp.wait_send()

    return pl.pallas_call(
        body, name="all_gather", in_specs=[_ANY] * nt, out_specs=[_ANY] * nt,
        out_shape=[jax.ShapeDtypeStruct(p.shape, p.dtype) for p in placed],
        input_output_aliases={t: t for t in range(nt)},
        scratch_shapes=[pltpu.SemaphoreType.DMA((6 * nt,)), pltpu.SemaphoreType.DMA((6 * nt,))],
    )(*placed)


def _rider_start(refs, kinds, ssem, rsem):
    x, y, c = _position()
    my = 2 * x + y
    for u, (ref, kind) in enumerate(zip(refs, kinds)):
        own = _unit_view(kind, ref, my)
        for j in (1, 2, 3):
            tx, ty = _peer_chip(x, y, j)
            _remote(own, own, ssem.at[3 * u + j - 1], rsem.at[3 * u + j - 1], (tx, ty, c)).start()


def _rider_wait(refs, kinds, ssem, rsem):
    x, y, c = _position()
    for u, (ref, kind) in enumerate(zip(refs, kinds)):
        for j in (1, 2, 3):
            tx, ty = _peer_chip(x, y, j)
            got = _unit_view(kind, ref, 2 * tx + ty)
            _remote(got, got, ssem.at[3 * u + j - 1], rsem.at[3 * u + j - 1], (tx, ty, c)).wait()


def _rider_args(rider):
    bufs = [b for b, _ in rider]
    kinds = [k for _, k in rider]
    n = len(bufs)
    sems = [pltpu.SemaphoreType.DMA((3 * n,)), pltpu.SemaphoreType.DMA((3 * n,))] if n else []
    return bufs, kinds, [_ANY] * n, [jax.ShapeDtypeStruct(b.shape, b.dtype) for b in bufs], sems


def _d2d_stream(src4, other_half, name):
    s, _, rows, cols = src4.shape
    tr = _row_tile(rows, cols, SUB16, STREAM_ELEMS_PER_BLOCK)
    nblk = rows // tr

    nh = src4.shape[1]

    def body(c_ref, src_ref, dst_ref, ssem, rsem):
        k = pl.program_id(0)
        i = pl.program_id(1)
        x, y, c = _position()
        sib = (x, y, 1 - c)
        blk = dst_ref.at[pl.ds(pl.multiple_of((k * nblk + i) * tr, SUB16), tr)]
        cp = _remote(src_ref, blk, ssem, rsem, sib)
        cp.start()
        cp.wait_send()

        @pl.when(jnp.logical_and(k == s - 1, i == nblk - 1))
        def _():
            _remote(dst_ref, dst_ref, ssem, rsem, sib).wait_recv()

    if other_half:
        src_map = lambda k, i, c_ref: ((k * nh + 1 - c_ref[0]) * nblk + i, 0)
    else:
        src_map = lambda k, i, c_ref: (k * nh * nblk + i, 0)
    out = pl.pallas_call(
        body, name=name,
        grid_spec=pltpu.PrefetchScalarGridSpec(
            num_scalar_prefetch=1, grid=(s, nblk),
            in_specs=[pl.BlockSpec((tr, cols), src_map)], out_specs=_ANY,
            scratch_shapes=[pltpu.SemaphoreType.DMA, pltpu.SemaphoreType.DMA]),
        out_shape=jax.ShapeDtypeStruct((s * rows, cols), src4.dtype),
        compiler_params=_cp("arbitrary", "arbitrary"))(_core_index(), src4.reshape(s * nh * rows, cols))
    return out.reshape(s, rows, cols)


def _scatter_copies(srcs, lands, kinds, ssem, rsem, lsem):
    x, y, c = _position()
    my = 2 * x + y
    cps = []
    for u, (src, land, kind) in enumerate(zip(srcs, lands, kinds)):
        cps.append(pltpu.make_async_copy(_unit_view(kind, src, my), land.at[my], lsem.at[u]))
        for j in (1, 2, 3):
            tx, ty = _peer_chip(x, y, j)
            cps.append(_remote(_unit_view(kind, src, 2 * tx + ty), land.at[my], ssem.at[3 * u + j - 1],
                               rsem.at[3 * u + j - 1], (tx, ty, c)))
    return cps


def _scatter_args(scatter):
    srcs = [s for s, _ in scatter]
    kinds = [k for _, k in scatter]
    n = len(srcs)
    shapes = [jax.ShapeDtypeStruct((N_CHIPS, s.shape[0], s.shape[1] // N_CHIPS) if k == "col" else s.shape, s.dtype)
              for s, k in scatter]
    sems = [pltpu.SemaphoreType.DMA((3 * n,)), pltpu.SemaphoreType.DMA((3 * n,)),
            pltpu.SemaphoreType.DMA((n,))] if n else []
    return srcs, kinds, [_ANY] * n, shapes, sems


def _rs_scatter(scatter):
    srcs, kinds, specs, shapes, sems = _scatter_args(scatter)
    n = len(srcs)

    def body(*refs):
        cps = _scatter_copies(refs[:n], refs[n:2 * n], kinds, *refs[2 * n:])
        for cp in cps:
            cp.start()
        for cp in cps:
            cp.wait()

    return pl.pallas_call(body, name="rs_scatter", in_specs=specs, out_specs=specs, out_shape=shapes,
                          scratch_shapes=sems)(*srcs)


SMALL = (("sc_conv_w", True), ("sc_conv_b", False), ("lru_b_in", True), ("lru_conv_w", True),
         ("lru_conv_b", True), ("lru_b_gate", True), ("lru_lambda", True), ("ffn_conv_w", True),
         ("ffn_conv_b", False), ("ln_g", True), ("ln_b", True))
PACK_ROW_MULT = 2 * SUB16


def _pack_rows(shapes):
    n = sum(math.prod(shapes[name]) for name, _ in SMALL)
    rows = -(-n // 128)
    return -(-rows // PACK_ROW_MULT) * PACK_ROW_MULT


def _pack_local(vals, shapes):
    flat = jnp.concatenate([vals[name].reshape(-1) for name, _ in SMALL])
    rows = _pack_rows(shapes)
    return jnp.pad(flat, (0, rows * 128 - flat.shape[0])).reshape(rows, 128)


def _unpack_local(pack, shapes):
    flat = pack.reshape(-1)
    out, off = {}, 0
    for name, _ in SMALL:
        n = math.prod(shapes[name])
        out[name] = flat[off:off + n].reshape(shapes[name])
        off += n
    return out


def _pack_slots(fulls, shapes):
    parts = []
    for name, sharded in SMALL:
        v = fulls[name]
        if sharded:
            ns = shapes[name][-1]
            v = jnp.moveaxis(v.reshape(v.shape[:-1] + (N_CHIPS, ns)), -2, 0).reshape(N_CHIPS, -1)
        else:
            v = jnp.broadcast_to(v.reshape(1, -1), (N_CHIPS, v.size))
        parts.append(v)
    flat = jnp.concatenate(parts, axis=1)
    rows = _pack_rows(shapes)
    return jnp.pad(flat, ((0, 0), (0, rows * 128 - flat.shape[1]))).reshape(N_CHIPS, rows, 128)


def _unpack_slots(packs, shapes):
    flat = packs.reshape(N_CHIPS, -1)
    out, off = {}, 0
    for name, sharded in SMALL:
        n = math.prod(shapes[name])
        if sharded:
            seg = flat[:, off:off + n].reshape((N_CHIPS,) + tuple(shapes[name]))
            seg = jnp.moveaxis(seg, 0, -2)
            out[name] = seg.reshape(seg.shape[:-2] + (N_CHIPS * shapes[name][-1],))
        off += n
    return out


WEIGHTS = ("sc_w_in", "sc_conv_w", "sc_conv_b", "sc_w_out", "lru_w_in", "lru_b_in", "lru_conv_w", "lru_conv_b",
           "lru_w_gate", "lru_b_gate", "lru_lambda", "lru_w_out", "ffn_w_up", "ffn_conv_w", "ffn_conv_b",
           "ffn_w_down", "ln_g", "ln_b")
GATHER_KIND = {"sc_w_in": "col", "sc_w_out": "lead", "lru_w_in": "col", "lru_w_out": "lead", "ffn_w_up": "col",
               "ffn_w_down": "lead"}


def kernel(x, sc_w_in, sc_conv_w, sc_conv_b, sc_w_out, lru_w_in, lru_b_in, lru_conv_w, lru_conv_b, lru_w_gate, lru_b_gate, lru_lambda, lru_w_out, ffn_w_up, ffn_conv_w, ffn_conv_b, ffn_w_down, ln_g, ln_b, loss_target, m_sc_w_in, m_sc_conv_w, m_sc_conv_b, m_sc_w_out, m_lru_w_in, m_lru_b_in, m_lru_conv_w, m_lru_conv_b, m_lru_w_gate, m_lru_b_gate, m_lru_lambda, m_lru_w_out, m_ffn_w_up, m_ffn_conv_w, m_ffn_conv_b, m_ffn_w_down, m_ln_g, m_ln_b, v_sc_w_in, v_sc_conv_w, v_sc_conv_b, v_sc_w_out, v_lru_w_in, v_lru_b_in, v_lru_conv_w, v_lru_conv_b, v_lru_w_gate, v_lru_b_gate, v_lru_lambda, v_lru_w_out, v_ffn_w_up, v_ffn_conv_w, v_ffn_conv_b, v_ffn_w_down, v_ln_g, v_ln_b):
    w = dict(zip(WEIGHTS, (sc_w_in, sc_conv_w, sc_conv_b, sc_w_out, lru_w_in, lru_b_in, lru_conv_w, lru_conv_b,
                           lru_w_gate, lru_b_gate, lru_lambda, lru_w_out, ffn_w_up, ffn_conv_w, ffn_conv_b,
                           ffn_w_down, ln_g, ln_b)))
    mom = dict(zip(WEIGHTS, (m_sc_w_in, m_sc_conv_w, m_sc_conv_b, m_sc_w_out, m_lru_w_in, m_lru_b_in, m_lru_conv_w,
                             m_lru_conv_b, m_lru_w_gate, m_lru_b_gate, m_lru_lambda, m_lru_w_out, m_ffn_w_up,
                             m_ffn_conv_w, m_ffn_conv_b, m_ffn_w_down, m_ln_g, m_ln_b)))
    vel = dict(zip(WEIGHTS, (v_sc_w_in, v_sc_conv_w, v_sc_conv_b, v_sc_w_out, v_lru_w_in, v_lru_b_in, v_lru_conv_w,
                             v_lru_conv_b, v_lru_w_gate, v_lru_b_gate, v_lru_lambda, v_lru_w_out, v_ffn_w_up,
                             v_ffn_conv_w, v_ffn_conv_b, v_ffn_w_down, v_ln_g, v_ln_b)))
    bd, seq, d = x.shape
    t = bd * seq
    small_shapes = {name: w[name].shape for name, _ in SMALL}

    w_pack = _pack_local(w, small_shapes)
    gate_shape = w["lru_w_gate"].shape
    bufs = {(n, l): (_place(w[n], l, k, BF16, "place_w"), k)
            for n, k in GATHER_KIND.items() for l in range(w[n].shape[0])}
    bufs["gate"] = (_place(w["lru_w_gate"].reshape(1, -1, gate_shape[-1]), 0, "lead", BF16, "place_w"), "lead")
    bufs["pack"] = (_place(w_pack[None], 0, "lead", F32, "place_w"), "lead")

    def layer_keys(i):
        mixer = ("sc_w_in", "sc_w_out") if i % 2 == 0 else ("lru_w_in", "lru_w_out")
        return [(mixer[0], i // 2), (mixer[1], i // 2)], [("ffn_w_up", i), ("ffn_w_down", i)]

    def gathered(keys, arrays):
        for key, arr in zip(keys, arrays):
            bufs[key] = (arr, bufs[key][1])

    def wt(name, l):
        arr = bufs[(name, l)][0]
        return arr.reshape(1, -1, arr.shape[-1])

    first = layer_keys(0)[0] + layer_keys(0)[1] + ["gate", "pack"]
    gathered(first, _all_gather([bufs[k][0] for k in first], [bufs[k][1] for k in first]))
    full = _unpack_slots(bufs["pack"][0], small_shapes)
    full["sc_conv_b"] = sc_conv_b
    full["ffn_conv_b"] = ffn_conv_b
    wg_full = jnp.moveaxis(bufs["gate"][0].reshape((N_CHIPS,) + gate_shape), 0, -2)
    wg_full = wg_full.reshape(wg_full.shape[:-2] + (2 * LRU_BLOCK,))
    f = N_CHIPS * w["ffn_w_down"].shape[1]
    rw = N_CHIPS * w["lru_w_out"].shape[1]

    x0 = x.reshape(t, d)
    xb = x0.astype(BF16)
    cur, cur_b = x0, xb
    saved = []

    for i in range(DEPTH):
        j = i // 2
        s = {"xb": cur_b}
        mixer_next, ffn_next = layer_keys(i + 1) if i + 1 < DEPTH else ([], [])
        behind_out, behind_up, behind_down = mixer_next[1:], mixer_next[:1] + ffn_next[:1], ffn_next[1:]
        if i % 2 == 0:
            h = _mm_nn(cur_b, wt("sc_w_in", j), 0, None, 3 * d, "sc_in")
            q = _sc_fwd(h, full["sc_conv_w"][j], full["sc_conv_b"][j][None], seq, "sc_fwd")
            z1, x1, x1b, *arrived = _mm_nn_ln(q, wt("sc_w_out", j), 0, cur, full["ln_g"][i, 0][None],
                                              full["ln_b"][i, 0][None], "sc_out_ln",
                                              rider=[bufs[k] for k in behind_out])
        else:
            h = _mm_nn(cur_b, wt("lru_w_in", j), 0, full["lru_b_in"][j][None], 2 * rw, "lru_in")
            hs, q = _lru_fwd(h, full["lru_conv_w"][j], full["lru_conv_b"][j][None], wg_full[j],
                             full["lru_b_gate"][j], full["lru_lambda"][j][None], seq, "lru_fwd")
            s["hs"] = hs
            z1, x1, x1b, *arrived = _mm_nn_ln(q, wt("lru_w_out", j), 0, cur, full["ln_g"][i, 0][None],
                                              full["ln_b"][i, 0][None], "lru_out_ln",
                                              rider=[bufs[k] for k in behind_out])
        gathered(behind_out, arrived)
        s.update(h=h, q=q, z1=z1, x1b=x1b)
        h3, pre3, act, *arrived = _ffn_up(x1b, wt("ffn_w_up", i), 0, full["ffn_conv_w"][i],
                                          full["ffn_conv_b"][i][None], seq, "ffn_up",
                                          rider=[bufs[k] for k in behind_up])
        gathered(behind_up, arrived)
        z2, x2, x2b, *arrived = _mm_nn_ln(act, wt("ffn_w_down", i), 0, x1, full["ln_g"][i, 1][None],
                                          full["ln_b"][i, 1][None], "ffn_down_ln",
                                          rider=[bufs[k] for k in behind_down])
        gathered(behind_down, arrived)
        s.update(h3=h3, pre3=pre3, act=act, z2=z2)
        saved.append(s)
        cur, cur_b = x2, x2b

    dcur, loss = cur, None

    def pair_sum(p, kind):
        lead = kind == "lead"
        p4 = p.reshape(N_CHIPS if lead else 1, 2, -1, p.shape[-1])
        chip_sum = _add_pair(p4, _d2d_stream(p4, True, "rs_swap"), "rs_add_pair")
        return chip_sum if lead else chip_sum[0]

    gp = {n: [None] * w[n].shape[0] for n, _ in SMALL}
    gp["lru_w_gate"] = [None] * gate_shape[0]
    landed, pending = {}, []
    for i in reversed(range(DEPTH)):
        j = i // 2
        s = saved[i]
        mixer_keys, ffn_keys = layer_keys(i)
        if i == DEPTH - 1:
            dz2, dz2b, dg, db, loss_parts = _ln_bwd(dcur, s["z2"], full["ln_g"][i, 1][None], "ln_bwd_loss",
                                                    target=loss_target.reshape(t, d))
            loss = lax.psum(jnp.sum(loss_parts), MESH_AXES)
        else:
            dz2, dz2b, dg, db = _ln_bwd(dcur, s["z2"], full["ln_g"][i, 1][None], "ln_bwd")
        gp["ln_g"][i] = [None, dg[0]]
        gp["ln_b"][i] = [None, db[0]]
        p_down = _mm_tn(s["act"], dz2b[None], f // 2, d, "ffn_down_dw").reshape(N_CHIPS, -1, d)
        dh3, dcwg, dcwv, dcbg, dcbv, *lands = _ffn_down_bwd(
            dz2b, wt("ffn_w_down", i), 0, s["h3"], s["pre3"], full["ffn_conv_w"][i], seq, "ffn_down_bwd",
            scatter=[(chip_sum, kind) for _, chip_sum, kind in pending])
        landed.update({key: land for (key, _, _), land in zip(pending, lands)})
        gp["ffn_conv_w"][i] = jnp.concatenate([dcwg, dcwv], axis=1)
        gp["ffn_conv_b"][i] = jnp.concatenate([dcbg[0], dcbv[0]])
        dx1 = _mm_nt_res(dh3, wt("ffn_w_up", i), 0, dz2, f // 2, "ffn_up_dx")
        p_up = _mm_tn(s["x1b"], dh3, d, f // 2, "ffn_up_dw")
        ffn_partials = ((ffn_keys[0], p_up, "col"), (ffn_keys[1], p_down, "lead"))
        early = [(key, pair_sum(p, kind), kind) for key, p, kind in ffn_partials] if i == 0 else []
        early_scatter = [(chip_sum, kind) for _, chip_sum, kind in early]
        dz1, dz1b, dg, db = _ln_bwd(dx1, s["z1"], full["ln_g"][i, 0][None], "ln_bwd")
        gp["ln_g"][i][0] = dg[0]
        gp["ln_b"][i][0] = db[0]
        gp["ln_g"][i] = jnp.stack(gp["ln_g"][i])
        gp["ln_b"][i] = jnp.stack(gp["ln_b"][i])
        if i % 2 == 0:
            dq = _mm_nt(dz1b, wt("sc_w_out", j), 0, d, "sc_out_dx")
            p_out = _mm_tn(s["q"], dz1b[None], d, d, "sc_out_dw")
            dh3, dcw, dcb = _sc_bwd(s["h"], dq, full["sc_conv_w"][j], full["sc_conv_b"][j][None], seq, "sc_bwd")
            gp["sc_conv_w"][j] = dcw
            gp["sc_conv_b"][j] = dcb[0]
            res = _mm_nt_res(dh3, wt("sc_w_in", j), 0, dz1, d, "sc_in_dx", scatter=early_scatter)
            dcur, lands = (res[0], res[1:]) if early else (res, [])
            landed.update({key: land for (key, _, _), land in zip(early, lands)})
            p_in = _mm_tn(s["xb"], dh3, d, d, "sc_in_dw")
        else:
            dq = _mm_nt(dz1b, wt("lru_w_out", j), 0, rw, "lru_out_dx")
            p_out = _mm_tn(s["q"], dz1b[None], rw, d, "lru_out_dw")
            dh3, dbin, dcw, dcb, dwg, dbg, dlam = _lru_bwd(
                s["h"], s["hs"], dq, full["lru_conv_w"][j], full["lru_conv_b"][j][None], wg_full[j],
                full["lru_b_gate"][j], full["lru_lambda"][j][None], seq, "lru_bwd")
            gp["lru_b_in"][j] = dbin[0]
            gp["lru_conv_w"][j] = dcw
            gp["lru_conv_b"][j] = dcb[0]
            gp["lru_w_gate"][j] = dwg
            gp["lru_b_gate"][j] = dbg
            gp["lru_lambda"][j] = dlam[0]
            dcur = _mm_nt_res(dh3, wt("lru_w_in", j), 0, dz1, rw, "lru_in_dx")
            p_in = _mm_tn(s["xb"], dh3, d, rw, "lru_in_dw")
        layer_partials = ((mixer_keys[0], p_in, "col"), (mixer_keys[1], p_out.reshape(N_CHIPS, -1, d), "lead"))
        layer_partials += () if early else ffn_partials
        pending = [(key, pair_sum(p, kind), kind) for key, p, kind in layer_partials]
    grad_x = dcur.reshape(bd, seq, d)
    gp = {n: jnp.stack(v) for n, v in gp.items()}

    gate = gp["lru_w_gate"]
    gate = jnp.moveaxis(gate.reshape(gate.shape[:-1] + (N_CHIPS, gate_shape[-1])), -2, 0)
    gate = gate.astype(BF16).reshape(N_CHIPS, -1, gate_shape[-1])
    pending += [("gate", pair_sum(gate, "lead"), "lead"),
                ("pack", pair_sum(_pack_slots(gp, small_shapes), "lead"), "lead")]
    lands = _rs_scatter([(chip_sum, kind) for _, chip_sum, kind in pending])
    landed.update({key: land for (key, _, _), land in zip(pending, lands)})

    def update(keys, w3, m3, v3):
        g_mine = jnp.stack([_add_chips(landed[k], "rs_add_chips") for k in keys])
        g_sib = _d2d_stream(g_mine.reshape(1, 1, -1, g_mine.shape[-1]), False, "rs_share").reshape(g_mine.shape)
        return _adamw(w3, g_mine, g_sib, m3, v3, "adamw")

    g_out, d_out, m_out, v_out = {}, {}, {}, {}
    for n in GATHER_KIND:
        outs = update([(n, l) for l in range(w[n].shape[0])], w[n], mom[n], vel[n])
        g_out[n], d_out[n], m_out[n], v_out[n] = outs
    as_rows = lambda a: a.reshape(1, -1, a.shape[-1])
    outs = update(["gate"], as_rows(w["lru_w_gate"]), as_rows(mom["lru_w_gate"]), as_rows(vel["lru_w_gate"]))
    g_out["lru_w_gate"], d_out["lru_w_gate"], m_out["lru_w_gate"], v_out["lru_w_gate"] = (
        o.reshape(gate_shape) for o in outs)
    packs = update(["pack"], w_pack[None], _pack_local(mom, small_shapes)[None], _pack_local(vel, small_shapes)[None])
    for dst, pack in zip((g_out, d_out, m_out, v_out), packs):
        dst.update(_unpack_local(pack[0], small_shapes))

    return (loss, grad_x, *[g_out[n] for n in WEIGHTS], *[d_out[n] for n in WEIGHTS],
            *[m_out[n] for n in WEIGHTS], *[v_out[n] for n in WEIGHTS])
```

```python
import math

import jax
import jax.numpy as jnp
from jax import lax
from jax.experimental import pallas as pl
from jax.experimental.pallas import tpu as pltpu

F32 = jnp.float32
BF16 = jnp.bfloat16

DEPTH = 4
LRU_HEADS = 10
LRU_BLOCK = 128
LRU_C = 8.0
LN_EPS = 1e-5
ALPHA = (2.0 * DEPTH) ** 0.25
ADAM_LR, ADAM_B1, ADAM_B2, ADAM_EPS, ADAM_WD, ADAM_STEP = 0.001, 0.9, 0.999, 1e-08, 0.01, 10
N_CHIPS = 4
MESH_AXES = ("x", "y", "c")

VMEM_LIMIT_BYTES = 48 * 1024 * 1024
TM_MM = 512
TM_RES = 1024
TT_MM = 2048
TM_SC = 256
FFN_COL_BLOCKS = 2
RC = 128
LANE = 128
MXU_COLS = 256
TS_LRU = 256
TM_LN = 512
ELEMS_PER_BLOCK = 512 * 1024
STREAM_ELEMS_PER_BLOCK = 2048 * 1024
SUB = 8
SUB16 = 16


def _cp(*sem):
    return pltpu.CompilerParams(dimension_semantics=sem, vmem_limit_bytes=VMEM_LIMIT_BYTES)


def _sigmoid(v):
    return 0.5 + 0.5 * jnp.tanh(0.5 * v)


def _softplus(v):
    e = jnp.exp(-jnp.abs(v))
    log1p = jnp.where(e < 1e-3, e * (1.0 - e * (0.5 - e * (1.0 / 3.0))), jnp.log(1.0 + e))
    return jnp.maximum(v, 0.0) + log1p


def _one_minus_exp(v):
    series = -v * (1.0 + v * (0.5 + v * (1.0 / 6.0 + v * (1.0 / 24.0))))
    return jnp.where(v > -0.02, series, 1.0 - jnp.exp(v))


def _gelu_and_grad(v):
    k = math.sqrt(2.0 / math.pi)
    t = jnp.tanh(k * (v + 0.044715 * v * v * v))
    val = 0.5 * v * (1.0 + t)
    grad = 0.5 * (1.0 + t) + 0.5 * v * (1.0 - t * t) * k * (1.0 + 3.0 * 0.044715 * v * v)
    return val, grad


def _down(ext, k, n_head):
    if k:
        ext = pltpu.roll(ext, k, 0)
    return ext[n_head:]


def _up(ext, k, n):
    if k:
        ext = pltpu.roll(ext, ext.shape[0] - k, 0)
    return ext[:n]


def _row(ref, k):
    return ref[k:k + 1, :]


def _colsum(v):
    return jnp.sum(v, axis=0, keepdims=True)


def _mm_nn(a, w3, l, bias, tn, name):
    m, k = a.shape
    n = w3.shape[2]
    tm = min(TM_MM, m)
    has_bias = bias is not None

    def body(*refs):
        if has_bias:
            a_ref, w_ref, b_ref, o_ref = refs
        else:
            a_ref, w_ref, o_ref = refs
        acc = jnp.dot(a_ref[...], w_ref[...], preferred_element_type=F32)
        if has_bias:
            acc = acc + b_ref[...]
        o_ref[...] = acc.astype(o_ref.dtype)

    in_specs = [pl.BlockSpec((tm, k), lambda i, j: (i, 0)),
                pl.BlockSpec((None, k, tn), lambda i, j: (l, 0, j))]
    args = [a, w3]
    if has_bias:
        in_specs.append(pl.BlockSpec((1, tn), lambda i, j: (0, j)))
        args.append(bias)
    return pl.pallas_call(
        body, name=name, grid=(m // tm, n // tn), in_specs=in_specs,
        out_specs=pl.BlockSpec((tm, tn), lambda i, j: (i, j)),
        out_shape=jax.ShapeDtypeStruct((m, n), BF16),
        compiler_params=_cp("parallel", "arbitrary"))(*args)


def _mm_nn_ln(a, w3, l, xres, g, b, name, rider=()):
    m, k = a.shape
    n = w3.shape[2]
    tm = min(TM_MM, m)
    bufs, rkinds, rspecs, rshapes, rsems = _rider_args(rider)
    nr = len(bufs)

    def body(*refs):
        a_ref, w_ref, x_ref, g_ref, b_ref = refs[:5]
        z_ref, xn_ref, xb_ref = refs[5 + nr:8 + nr]
        rout, sems = refs[8 + nr:8 + 2 * nr], refs[8 + 2 * nr:]
        if nr:
            @pl.when(pl.program_id(0) == 0)
            def _():
                _rider_start(rout, rkinds, *sems)

        y = jnp.dot(a_ref[...], w_ref[...], preferred_element_type=F32)
        z = ALPHA * x_ref[...] + y
        mu = jnp.mean(z, axis=-1, keepdims=True)
        zc = z - mu
        var = jnp.mean(zc * zc, axis=-1, keepdims=True)
        xn = zc * lax.rsqrt(var + LN_EPS) * g_ref[...] + b_ref[...]
        z_ref[...] = z
        xn_ref[...] = xn
        xb_ref[...] = xn.astype(BF16)
        if nr:
            @pl.when(pl.program_id(0) == m // tm - 1)
            def _():
                _rider_wait(rout, rkinds, *sems)

    row = pl.BlockSpec((tm, n), lambda i: (i, 0))
    vec = pl.BlockSpec((1, n), lambda i: (0, 0))
    return pl.pallas_call(
        body, name=name, grid=(m // tm,),
        in_specs=[pl.BlockSpec((tm, k), lambda i: (i, 0)),
                  pl.BlockSpec((None, k, n), lambda i: (l, 0, 0)), row, vec, vec] + rspecs,
        out_specs=[row, row, row] + rspecs,
        out_shape=[jax.ShapeDtypeStruct((m, n), F32), jax.ShapeDtypeStruct((m, n), F32),
                   jax.ShapeDtypeStruct((m, n), BF16)] + rshapes,
        input_output_aliases={5 + u: 3 + u for u in range(nr)},
        scratch_shapes=rsems,
        compiler_params=_cp("arbitrary"))(a, w3, xres, g, b, *bufs)


def _mm_nt(a, w3, l, tk, name):
    m, n = a.shape
    kd = w3.shape[1]
    tm = min(TM_MM, m)

    def body(a_ref, w_ref, o_ref):
        o_ref[...] = lax.dot_general(a_ref[...], w_ref[...], (((1,), (1,)), ((), ())),
                                     preferred_element_type=F32).astype(o_ref.dtype)

    return pl.pallas_call(
        body, name=name, grid=(m // tm, kd // tk),
        in_specs=[pl.BlockSpec((tm, n), lambda i, j: (i, 0)),
                  pl.BlockSpec((None, tk, n), lambda i, j: (l, j, 0))],
        out_specs=pl.BlockSpec((tm, tk), lambda i, j: (i, j)),
        out_shape=jax.ShapeDtypeStruct((m, kd), BF16),
        compiler_params=_cp("parallel", "arbitrary"))(a, w3)


def _mm_nt_res(dh3, w3, l, dz, tc, name, scatter=()):
    g, m, cg = dh3.shape
    kd = w3.shape[1]
    ncg = cg // tc
    nk = g * ncg
    tm = min(TM_RES, m)
    srcs, skinds, sspecs, sshapes, ssems = _scatter_args(scatter)
    ns = len(srcs)

    def body(*refs):
        a_ref, w_ref, dz_ref = refs[:3]
        sin, o_ref, lands = refs[3:3 + ns], refs[3 + ns], refs[4 + ns:4 + 2 * ns]
        acc, sems = refs[4 + 2 * ns], refs[5 + 2 * ns:]
        i = pl.program_id(0)
        k = pl.program_id(1)
        if ns:
            @pl.when(jnp.logical_and(i == 0, k == 0))
            def _():
                for cp in _scatter_copies(sin, lands, skinds, *sems):
                    cp.start()

        @pl.when(k == 0)
        def _():
            acc[...] = ALPHA * dz_ref[...]

        acc[...] += lax.dot_general(a_ref[...], w_ref[...], (((1,), (1,)), ((), ())),
                                    preferred_element_type=F32)

        @pl.when(k == nk - 1)
        def _():
            o_ref[...] = acc[...]

        if ns:
            @pl.when(jnp.logical_and(i == m // tm - 1, k == nk - 1))
            def _():
                for cp in _scatter_copies(sin, lands, skinds, *sems):
                    cp.wait()

    outs = pl.pallas_call(
        body, name=name, grid=(m // tm, nk),
        in_specs=[pl.BlockSpec((None, tm, tc), lambda i, k: (k // ncg, i, k % ncg)),
                  pl.BlockSpec((None, kd, tc), lambda i, k: (l, 0, k)),
                  pl.BlockSpec((tm, kd), lambda i, k: (i, 0))] + sspecs,
        out_specs=[pl.BlockSpec((tm, kd), lambda i, k: (i, 0))] + sspecs,
        out_shape=[jax.ShapeDtypeStruct((m, kd), F32)] + sshapes,
        scratch_shapes=[pltpu.VMEM((tm, kd), F32)] + ssems,
        compiler_params=_cp("arbitrary", "arbitrary"))(dh3, w3, dz, *srcs)
    return outs if ns else outs[0]


def _mm_tn(a, b3, tka, tnb, name):
    t, ka = a.shape
    g, _, cg = b3.shape
    ncg = cg // tnb
    tt = min(TT_MM, t)
    nt = t // tt

    def body(a_ref, b_ref, o_ref, acc):
        s = pl.program_id(2)

        @pl.when(s == 0)
        def _():
            acc[...] = jnp.zeros_like(acc)

        acc[...] += lax.dot_general(a_ref[...], b_ref[...], (((0,), (0,)), ((), ())),
                                    preferred_element_type=F32)

        @pl.when(s == nt - 1)
        def _():
            o_ref[...] = acc[...].astype(o_ref.dtype)

    return pl.pallas_call(
        body, name=name, grid=(ka // tka, g * ncg, nt),
        in_specs=[pl.BlockSpec((tt, tka), lambda i, j, s: (s, i)),
                  pl.BlockSpec((None, tt, tnb), lambda i, j, s: (j // ncg, s, j % ncg))],
        out_specs=pl.BlockSpec((tka, tnb), lambda i, j, s: (i, j)),
        out_shape=jax.ShapeDtypeStruct((ka, g * cg), BF16),
        scratch_shapes=[pltpu.VMEM((tka, tnb), F32)],
        compiler_params=_cp("parallel", "parallel", "arbitrary"))(a, b3)


def _ln_bwd(dxn, z, g, name, target=None):
    m, d = z.shape
    tm = min(TM_LN, m)
    with_loss = target is not None

    def body(*refs):
        dx_ref, z_ref, g_ref = refs[:3]
        dz_ref, dzb_ref, dg_ref, db_ref = refs[3 + with_loss:7 + with_loss]

        @pl.when(pl.program_id(0) == 0)
        def _():
            for r in refs[5 + with_loss:]:
                r[...] = jnp.zeros_like(r)

        zz = z_ref[...]
        dx = dx_ref[...]
        if with_loss:
            err = dx - refs[3][...]
            dx = err * (1.0 / d)
            refs[8][...] += _colsum(err * err) * (0.5 / d)
        mu = jnp.mean(zz, axis=-1, keepdims=True)
        zc = zz - mu
        var = jnp.mean(zc * zc, axis=-1, keepdims=True)
        rstd = lax.rsqrt(var + LN_EPS)
        xh = zc * rstd
        dg_ref[...] += _colsum(dx * xh)
        db_ref[...] += _colsum(dx)
        dxh = dx * g_ref[...]
        m1 = jnp.mean(dxh, axis=-1, keepdims=True)
        m2 = jnp.mean(dxh * xh, axis=-1, keepdims=True)
        dz = rstd * (dxh - m1 - xh * m2)
        dz_ref[...] = dz
        dzb_ref[...] = dz.astype(BF16)

    row = pl.BlockSpec((tm, d), lambda i: (i, 0))
    vec = pl.BlockSpec((1, d), lambda i: (0, 0))
    vec_shape = jax.ShapeDtypeStruct((1, d), F32)
    return pl.pallas_call(
        body, name=name, grid=(m // tm,), in_specs=[row, row, vec] + [row] * with_loss,
        out_specs=[row, row, vec, vec] + [vec] * with_loss,
        out_shape=[jax.ShapeDtypeStruct((m, d), F32), jax.ShapeDtypeStruct((m, d), BF16), vec_shape, vec_shape]
        + [vec_shape] * with_loss,
        compiler_params=_cp("arbitrary"))(dxn, z, g, *([target] if with_loss else []))


def _sc_fwd(h, cw, cb, seq, name):
    t, d3 = h.shape
    d = d3 // 3
    tm = min(TM_SC, seq)

    def body(hb_ref, hc_ref, hv_ref, cw_ref, cb_ref, q_ref, carry):
        i = pl.program_id(0)

        @pl.when(lax.rem(i * tm, seq) == 0)
        def _():
            carry[...] = jnp.zeros_like(carry)

        p = hc_ref[...].astype(F32) * hv_ref[...].astype(F32)
        ext = jnp.concatenate([carry[...], p], axis=0)
        u = cb_ref[...] + _row(cw_ref, 0) * _down(ext, 2, SUB) + _row(cw_ref, 1) * _down(ext, 1, SUB) \
            + _row(cw_ref, 2) * p
        q_ref[...] = (hb_ref[...].astype(F32) * u).astype(BF16)
        carry[...] = p[tm - SUB:, :]

    blk = lambda c: pl.BlockSpec((tm, d), lambda i: (i, c))
    return pl.pallas_call(
        body, name=name, grid=(t // tm,),
        in_specs=[blk(0), blk(1), blk(2), pl.BlockSpec((3, d), lambda i: (0, 0)),
                  pl.BlockSpec((1, d), lambda i: (0, 0))],
        out_specs=pl.BlockSpec((tm, d), lambda i: (i, 0)),
        out_shape=jax.ShapeDtypeStruct((t, d), BF16),
        scratch_shapes=[pltpu.VMEM((SUB, d), F32)],
        compiler_params=_cp("arbitrary"))(h, h, h, cw, cb)


def _sc_bwd(h, dq, cw, cb, seq, name):
    t, d3 = h.shape
    d = d3 // 3
    tm = min(TM_SC, seq)
    nt = t // tm
    hpt = tm // SUB16

    def body(hb_ref, hc_ref, hv_ref, hch_ref, hvh_ref, dq_ref, cw_ref, cb_ref,
             dh_ref, dcw_ref, dcb_ref, carry):
        i = pl.program_id(0)
        ri = nt - 1 - i

        @pl.when(i == 0)
        def _():
            dcw_ref[...] = jnp.zeros_like(dcw_ref)
            dcb_ref[...] = jnp.zeros_like(dcb_ref)

        @pl.when(lax.rem((ri + 1) * tm, seq) == 0)
        def _():
            carry[...] = jnp.zeros_like(carry)

        keep = jnp.where(lax.rem(ri * tm, seq) == 0, 0.0, 1.0)
        gb = hb_ref[...].astype(F32)
        gc = hc_ref[...].astype(F32)
        v = hv_ref[...].astype(F32)
        p = gc * v
        p_head = hch_ref[...].astype(F32) * hvh_ref[...].astype(F32) * keep
        ext = jnp.concatenate([p_head, p], axis=0)
        pm2 = _down(ext, 2, SUB16)
        pm1 = _down(ext, 1, SUB16)
        u = cb_ref[...] + _row(cw_ref, 0) * pm2 + _row(cw_ref, 1) * pm1 + _row(cw_ref, 2) * p
        dqf = dq_ref[...].astype(F32)
        du = dqf * gb
        dcb_ref[...] += _colsum(du)
        dcw_ref[0:1, :] += _colsum(du * pm2)
        dcw_ref[1:2, :] += _colsum(du * pm1)
        dcw_ref[2:3, :] += _colsum(du * p)
        ext2 = jnp.concatenate([du, carry[...]], axis=0)
        dp = _row(cw_ref, 2) * du + _row(cw_ref, 1) * _up(ext2, 1, tm) + _row(cw_ref, 0) * _up(ext2, 2, tm)
        carry[...] = du[0:SUB, :]
        dh_ref[0] = (dqf * u).astype(BF16)
        dh_ref[1] = (dp * v).astype(BF16)
        dh_ref[2] = (dp * gc).astype(BF16)

    blk = lambda c: pl.BlockSpec((tm, d), lambda i: (nt - 1 - i, c))
    head = lambda c: pl.BlockSpec((SUB16, d), lambda i: (jnp.maximum((nt - 1 - i) * hpt - 1, 0), c))
    vec = lambda r: pl.BlockSpec((r, d), lambda i: (0, 0))
    return pl.pallas_call(
        body, name=name, grid=(nt,),
        in_specs=[blk(0), blk(1), blk(2), head(1), head(2),
                  pl.BlockSpec((tm, d), lambda i: (nt - 1 - i, 0)), vec(3), vec(1)],
        out_specs=[pl.BlockSpec((3, tm, d), lambda i: (0, nt - 1 - i, 0)), vec(3), vec(1)],
        out_shape=[jax.ShapeDtypeStruct((3, t, d), BF16), jax.ShapeDtypeStruct((3, d), F32),
                   jax.ShapeDtypeStruct((1, d), F32)],
        scratch_shapes=[pltpu.VMEM((SUB, d), F32)],
        compiler_params=_cp("arbitrary"))(h, h, h, h, h, dq, cw, cb)


def _fold(v):
    return jnp.sum(v.reshape(v.shape[0] // SUB, SUB, v.shape[1]), axis=0)


def _ffn_up(xb, w3, l, cw, cb, seq, name, rider=()):
    t, d = xb.shape
    f = w3.shape[2] // 2
    tc = f // FFN_COL_BLOCKS
    tm = min(TM_MM, seq)
    nc = FFN_COL_BLOCKS
    bufs, rkinds, rspecs, rshapes, rsems = _rider_args(rider)
    nr = len(bufs)

    def body(*refs):
        x_ref, wg_ref, wv_ref, cwg_ref, cwv_ref, cbg_ref, cbv_ref = refs[:7]
        h_ref, pre_ref, act_ref = refs[7 + nr:10 + nr]
        rout = refs[10 + nr:10 + 2 * nr]
        eg, ev = refs[10 + 2 * nr:12 + 2 * nr]
        sems = refs[12 + 2 * nr:]
        i = pl.program_id(1)
        if nr:
            @pl.when(jnp.logical_and(pl.program_id(0) == 0, i == 0))
            def _():
                _rider_start(rout, rkinds, *sems)

        @pl.when(lax.rem(i * tm, seq) == 0)
        def _():
            eg[0:SUB, :] = jnp.zeros((SUB, tc), F32)
            ev[0:SUB, :] = jnp.zeros((SUB, tc), F32)

        xx = x_ref[...]

        def matmul(lo, hi):
            eg[SUB:, lo:hi] = jnp.dot(xx, wg_ref[:, lo:hi], preferred_element_type=F32)
            ev[SUB:, lo:hi] = jnp.dot(xx, wv_ref[:, lo:hi], preferred_element_type=F32)

        def epilogue(lo, hi):
            for c0 in range(lo, hi, LANE):
                cols = slice(c0, c0 + LANE)
                taps = [[ref[k:k + 1, cols] for k in range(3)] + [bref[:, cols]]
                        for ref, bref in ((cwg_ref, cbg_ref), (cwv_ref, cbv_ref))]
                for r0 in range(0, tm, RC):
                    rows = slice(r0, r0 + RC)
                    pres = []
                    for half, e_ref in enumerate((eg, ev)):
                        w0, w1, w2, bias = taps[half]
                        e = e_ref[r0:r0 + RC + SUB, cols]
                        cur = e[SUB:]
                        pre = bias + w0 * _down(e, 2, SUB) + w1 * _down(e, 1, SUB) + w2 * cur
                        h_ref[half, rows, cols] = cur.astype(BF16)
                        pre_ref[half, rows, cols] = pre.astype(BF16)
                        pres.append(pre)
                    act_ref[rows, cols] = (pres[0] * _sigmoid(pres[0]) * pres[1]).astype(BF16)

        blocks = [(lo, min(lo + MXU_COLS, tc)) for lo in range(0, tc, MXU_COLS)]
        matmul(*blocks[0])
        for b, blk in enumerate(blocks):
            if b + 1 < len(blocks):
                matmul(*blocks[b + 1])
            epilogue(*blk)
        eg[0:SUB, :] = eg[tm:tm + SUB, :]
        ev[0:SUB, :] = ev[tm:tm + SUB, :]
        if nr:
            @pl.when(jnp.logical_and(pl.program_id(0) == nc - 1, i == t // tm - 1))
            def _():
                _rider_wait(rout, rkinds, *sems)

    wspec = lambda off: pl.BlockSpec((None, d, tc), lambda j, i: (l, 0, j + off))
    vec = lambda r, off: pl.BlockSpec((r, tc), lambda j, i: (0, j + off))
    pair = pl.BlockSpec((2, tm, tc), lambda j, i: (0, i, j))
    return pl.pallas_call(
        body, name=name, grid=(nc, t // tm),
        in_specs=[pl.BlockSpec((tm, d), lambda j, i: (i, 0)), wspec(0), wspec(nc),
                  vec(3, 0), vec(3, nc), vec(1, 0), vec(1, nc)] + rspecs,
        out_specs=[pair, pair, pl.BlockSpec((tm, tc), lambda j, i: (i, j))] + rspecs,
        out_shape=[jax.ShapeDtypeStruct((2, t, f), BF16), jax.ShapeDtypeStruct((2, t, f), BF16),
                   jax.ShapeDtypeStruct((t, f), BF16)] + rshapes,
        input_output_aliases={7 + u: 3 + u for u in range(nr)},
        scratch_shapes=[pltpu.VMEM((tm + SUB, tc), F32), pltpu.VMEM((tm + SUB, tc), F32)] + rsems,
        compiler_params=_cp("arbitrary", "arbitrary"))(xb, w3, w3, cw, cw, cb, cb, *bufs)


def _ffn_down_bwd(dzb, wd3, l, h3, pre3, cw, seq, name, scatter=()):
    t, d = dzb.shape
    f = wd3.shape[1]
    tc = f // FFN_COL_BLOCKS
    tm = min(TM_MM, seq)
    nt = t // tm
    nc = FFN_COL_BLOCKS
    srcs, skinds, sspecs, sshapes, ssems = _scatter_args(scatter)
    ns = len(srcs)

    def body(*refs):
        dz_ref, wd_ref, h_ref, pre_ref, cwg_ref, cwv_ref = refs[:6]
        sin = refs[6:6 + ns]
        dh_ref, dcwg_ref, dcwv_ref, dcbg_ref, dcbv_ref = refs[6 + ns:11 + ns]
        lands = refs[11 + ns:11 + 2 * ns]
        da_s, carry = refs[11 + 2 * ns:13 + 2 * ns]
        sems = refs[13 + 2 * ns:]
        i = pl.program_id(1)
        ri = nt - 1 - i
        if ns:
            @pl.when(jnp.logical_and(pl.program_id(0) == 0, i == 0))
            def _():
                for cp in _scatter_copies(sin, lands, skinds, *sems):
                    cp.start()

        @pl.when(i == 0)
        def _():
            for r in (dcwg_ref, dcwv_ref, dcbg_ref, dcbv_ref):
                r[...] = jnp.zeros_like(r)

        @pl.when(lax.rem((ri + 1) * tm, seq) == 0)
        def _():
            carry[...] = jnp.zeros_like(carry)

        dz = dz_ref[...]

        def matmul(lo, hi):
            da_s[:, lo:hi] = lax.dot_general(dz, wd_ref[lo:hi, :], (((1,), (1,)), ((), ())),
                                             preferred_element_type=F32)

        def epilogue(lo, hi):
            for c0 in range(lo, hi, LANE):
                cols = slice(c0, c0 + LANE)
                taps = [[ref[k:k + 1, cols] for k in range(3)] for ref in (cwg_ref, cwv_ref)]
                acc = [jnp.zeros((SUB, LANE), F32)] * 8
                for r0 in range(tm - RC, -1, -RC):
                    rows = slice(r0, r0 + RC)
                    da = da_s[rows, cols]
                    gp = pre_ref[0, rows, cols].astype(F32)
                    vp = pre_ref[1, rows, cols].astype(F32)
                    sg = _sigmoid(gp)
                    dpres = (da * vp * (sg * (1.0 + gp * (1.0 - sg))), da * (gp * sg))
                    for half in range(2):
                        w0, w1, w2 = taps[half]
                        dpre = dpres[half]
                        ext = jnp.concatenate([dpre, carry[half, :, cols]], axis=0)
                        u1 = _up(ext, 1, RC)
                        u2 = _up(ext, 2, RC)
                        carry[half, :, cols] = dpre[0:SUB]
                        dh_ref[half, rows, cols] = (w2 * dpre + w1 * u1 + w0 * u2).astype(BF16)
                        hh = h_ref[half, rows, cols].astype(F32)
                        for k, term in enumerate((hh * u2, hh * u1, hh * dpre, dpre)):
                            acc[4 * half + k] = acc[4 * half + k] + _fold(term)
                for half, (dcw_ref, dcb_ref) in enumerate(((dcwg_ref, dcbg_ref), (dcwv_ref, dcbv_ref))):
                    for k in range(3):
                        dcw_ref[k:k + 1, cols] += _colsum(acc[4 * half + k])
                    dcb_ref[:, cols] += _colsum(acc[4 * half + 3])

        blocks = [(lo, min(lo + MXU_COLS, tc)) for lo in range(0, tc, MXU_COLS)]
        matmul(*blocks[0])
        for b, blk in enumerate(blocks):
            if b + 1 < len(blocks):
                matmul(*blocks[b + 1])
            epilogue(*blk)
        if ns:
            @pl.when(jnp.logical_and(pl.program_id(0) == nc - 1, i == nt - 1))
            def _():
                for cp in _scatter_copies(sin, lands, skinds, *sems):
                    cp.wait()

    pair = pl.BlockSpec((2, tm, tc), lambda j, i: (0, nt - 1 - i, j))
    vec = lambda off: pl.BlockSpec((3, tc), lambda j, i: (0, j + off))
    acc_spec = lambda r: pl.BlockSpec((r, tc), lambda j, i: (0, j))
    return pl.pallas_call(
        body, name=name, grid=(nc, nt),
        in_specs=[pl.BlockSpec((tm, d), lambda j, i: (nt - 1 - i, 0)),
                  pl.BlockSpec((None, tc, d), lambda j, i: (l, j, 0)), pair, pair, vec(0), vec(nc)] + sspecs,
        out_specs=[pair, acc_spec(3), acc_spec(3), acc_spec(1), acc_spec(1)] + sspecs,
        out_shape=[jax.ShapeDtypeStruct((2, t, f), BF16), jax.ShapeDtypeStruct((3, f), F32),
                   jax.ShapeDtypeStruct((3, f), F32), jax.ShapeDtypeStruct((1, f), F32),
                   jax.ShapeDtypeStruct((1, f), F32)] + sshapes,
        scratch_shapes=[pltpu.VMEM((tm, tc), F32), pltpu.VMEM((2, SUB, tc), F32)] + ssems,
        compiler_params=_cp("arbitrary", "arbitrary"))(dzb, wd3, h3, pre3, cw, cw, *srcs)


def _lru_gates(xr, wg_ref, bg_ref):
    rs, gs = [], []
    for hd in range(LRU_HEADS):
        xh = xr[:, hd * LRU_BLOCK:(hd + 1) * LRU_BLOCK].astype(BF16)
        gt = jnp.dot(xh, wg_ref[hd], preferred_element_type=F32) + _row(bg_ref, hd)
        rs.append(gt[:, :LRU_BLOCK])
        gs.append(gt[:, LRU_BLOCK:])
    return jnp.concatenate(rs, axis=1), jnp.concatenate(gs, axis=1)


def _lru_coeffs(xr, wg_ref, bg_ref, lam_ref):
    gr, gi = _lru_gates(xr, wg_ref, bg_ref)
    r = _sigmoid(gr)
    ig = _sigmoid(gi)
    sp = _softplus(-lam_ref[...])
    log_a = -LRU_C * r * sp
    a = jnp.exp(log_a)
    mult = jnp.sqrt(_one_minus_exp(2.0 * log_a))
    return r, ig, sp, a, mult


def _lru_fwd(h, cw, cb, wg, bg, lam, seq, name):
    t, r2 = h.shape
    rw = r2 // 2
    ts = min(TS_LRU, seq)
    n8 = ts // SUB

    def body(hg_ref, hr_ref, cw_ref, cb_ref, wg_ref, bg_ref, lam_ref, hs_ref, y_ref,
             a_s, b_s, cconv, cstate):
        i = pl.program_id(0)

        @pl.when(lax.rem(i * ts, seq) == 0)
        def _():
            cconv[...] = jnp.zeros_like(cconv)
            cstate[...] = jnp.zeros_like(cstate)

        rin = hr_ref[...].astype(F32)
        ext = jnp.concatenate([cconv[...], rin], axis=0)
        xr = cb_ref[...]
        for k in range(4):
            xr = xr + _row(cw_ref, k) * _down(ext, 3 - k, SUB)
        cconv[...] = rin[ts - SUB:, :]
        _, ig, _, a, mult = _lru_coeffs(xr, wg_ref, bg_ref, lam_ref)
        a_s[...] = a
        b_s[...] = mult * (ig * xr)
        row = lax.broadcasted_iota(jnp.int32, (SUB, rw), 0)

        def step(j, carry):
            off = pl.multiple_of(j * SUB, SUB)
            a8 = a_s[pl.ds(off, SUB), :]
            b8 = b_s[pl.ds(off, SUB), :]
            for d in (1, 2, 4):
                m = row >= d
                b8 = jnp.where(m, a8 * pltpu.roll(b8, d, 0) + b8, b8)
                a8 = jnp.where(m, a8 * pltpu.roll(a8, d, 0), a8)
            h8 = a8 * carry + b8
            hs_ref[pl.ds(off, SUB), :] = h8
            return _colsum(jnp.where(row == SUB - 1, h8, 0.0))

        cstate[...] = lax.fori_loop(0, n8, step, cstate[...])
        gel, _ = _gelu_and_grad(hg_ref[...].astype(F32))
        y_ref[...] = (hs_ref[...] * gel).astype(BF16)

    full = lambda shp: pl.BlockSpec(shp, lambda i: (0,) * len(shp))
    return pl.pallas_call(
        body, name=name, grid=(t // ts,),
        in_specs=[pl.BlockSpec((ts, rw), lambda i: (i, 0)), pl.BlockSpec((ts, rw), lambda i: (i, 1)),
                  full((4, rw)), full((1, rw)), full(wg.shape), full(bg.shape), full((1, rw))],
        out_specs=[pl.BlockSpec((ts, rw), lambda i: (i, 0)), pl.BlockSpec((ts, rw), lambda i: (i, 0))],
        out_shape=[jax.ShapeDtypeStruct((t, rw), F32), jax.ShapeDtypeStruct((t, rw), BF16)],
        scratch_shapes=[pltpu.VMEM((ts, rw), F32), pltpu.VMEM((ts, rw), F32),
                        pltpu.VMEM((SUB, rw), F32), pltpu.VMEM((1, rw), F32)],
        compiler_params=_cp("arbitrary"))(h, h, cw, cb, wg, bg, lam)


def _lru_bwd(h, hs, dy, cw, cb, wg, bg, lam, seq, name):
    t, r2 = h.shape
    rw = r2 // 2
    ts = min(TS_LRU, seq)
    nt = t // ts
    n8 = ts // SUB
    hp16 = ts // SUB16
    hp8 = ts // SUB

    def body(hg_ref, hr_ref, hrh_ref, hs_ref, hsh_ref, dy_ref, cw_ref, cb_ref, wg_ref, bg_ref, lam_ref,
             dh_ref, dbin_ref, dcw_ref, dcb_ref, dwg_ref, dbg_ref, dlam_ref,
             a_s, g_s, l_s, c_lam, c_a, c_dxr):
        i = pl.program_id(0)
        ri = nt - 1 - i

        @pl.when(i == 0)
        def _():
            for r in (dbin_ref, dcw_ref, dcb_ref, dwg_ref, dbg_ref, dlam_ref):
                r[...] = jnp.zeros_like(r)

        @pl.when(lax.rem((ri + 1) * ts, seq) == 0)
        def _():
            c_lam[...] = jnp.zeros_like(c_lam)
            c_a[...] = jnp.zeros_like(c_a)
            c_dxr[...] = jnp.zeros_like(c_dxr)

        keep = jnp.where(lax.rem(ri * ts, seq) == 0, 0.0, 1.0)
        rin = hr_ref[...].astype(F32)
        ext = jnp.concatenate([hrh_ref[...].astype(F32) * keep, rin], axis=0)
        shifted = [_down(ext, 3 - k, SUB16) for k in range(4)]
        xr = cb_ref[...]
        for k in range(4):
            xr = xr + _row(cw_ref, k) * shifted[k]
        r, ig, sp, a, mult = _lru_coeffs(xr, wg_ref, bg_ref, lam_ref)
        gel, dgel = _gelu_and_grad(hg_ref[...].astype(F32))
        dyf = dy_ref[...].astype(F32)
        hsv = hs_ref[...]
        dg = dyf * hsv * dgel

        a_s[...] = _up(jnp.concatenate([a, c_a[...]], axis=0), 1, ts)
        g_s[...] = dyf * gel
        c_a[...] = a[0:SUB, :]
        row = lax.broadcasted_iota(jnp.int32, (SUB, rw), 0)

        def step(j, carry):
            off = pl.multiple_of((n8 - 1 - j) * SUB, SUB)
            a8 = a_s[pl.ds(off, SUB), :]
            b8 = g_s[pl.ds(off, SUB), :]
            for d in (1, 2, 4):
                m = row < SUB - d
                b8 = jnp.where(m, a8 * pltpu.roll(b8, SUB - d, 0) + b8, b8)
                a8 = jnp.where(m, a8 * pltpu.roll(a8, SUB - d, 0), a8)
            l8 = a8 * carry + b8
            l_s[pl.ds(off, SUB), :] = l8
            return _colsum(jnp.where(row == 0, l8, 0.0))

        c_lam[...] = lax.fori_loop(0, n8, step, c_lam[...])
        lamv = l_s[...]
        hs_prev = _down(jnp.concatenate([hsh_ref[...] * keep, hsv], axis=0), 1, SUB)
        da = lamv * hs_prev
        t1 = lamv * xr
        dmult = t1 * ig
        dig = t1 * mult
        dxr = lamv * mult * ig
        dla = da * a - dmult * (a * a) / mult
        dr = dla * (-LRU_C * sp)
        dlam_ref[...] += _colsum(dla * (-LRU_C) * r) * (-1.0 / (1.0 + jnp.exp(lam_ref[...])))
        dgr = dr * r * (1.0 - r)
        dgi = dig * ig * (1.0 - ig)
        parts = []
        for hd in range(LRU_HEADS):
            sl = slice(hd * LRU_BLOCK, (hd + 1) * LRU_BLOCK)
            dgt = jnp.concatenate([dgr[:, sl], dgi[:, sl]], axis=1)
            dbg_ref[hd:hd + 1, :] += _colsum(dgt)
            dgt16 = dgt.astype(BF16)
            parts.append(lax.dot_general(dgt16, wg_ref[hd], (((1,), (1,)), ((), ())),
                                         preferred_element_type=F32))
            dwg_ref[hd] += lax.dot_general(xr[:, sl].astype(BF16), dgt16, (((0,), (0,)), ((), ())),
                                           preferred_element_type=F32)
        dxr = dxr + jnp.concatenate(parts, axis=1)

        dcb_ref[...] += _colsum(dxr)
        for k in range(4):
            dcw_ref[k:k + 1, :] += _colsum(dxr * shifted[k])
        ext2 = jnp.concatenate([dxr, c_dxr[...]], axis=0)
        drb = _row(cw_ref, 3) * dxr
        for k in range(3):
            drb = drb + _row(cw_ref, k) * _up(ext2, 3 - k, ts)
        c_dxr[...] = dxr[0:SUB, :]
        dh_ref[0] = dg.astype(BF16)
        dh_ref[1] = drb.astype(BF16)
        dbin_ref[:, 0:rw] += _colsum(dg)
        dbin_ref[:, rw:] += _colsum(drb)

    rev = lambda c: pl.BlockSpec((ts, rw), lambda i: (nt - 1 - i, c))
    full = lambda shp: pl.BlockSpec(shp, lambda i: (0,) * len(shp))
    nh = LRU_HEADS
    return pl.pallas_call(
        body, name=name, grid=(nt,),
        in_specs=[rev(0), rev(1),
                  pl.BlockSpec((SUB16, rw), lambda i: (jnp.maximum((nt - 1 - i) * hp16 - 1, 0), 1)),
                  rev(0),
                  pl.BlockSpec((SUB, rw), lambda i: (jnp.maximum((nt - 1 - i) * hp8 - 1, 0), 0)),
                  rev(0), full((4, rw)), full((1, rw)), full(wg.shape), full(bg.shape), full((1, rw))],
        out_specs=[pl.BlockSpec((2, ts, rw), lambda i: (0, nt - 1 - i, 0)), full((1, r2)), full((4, rw)),
                   full((1, rw)), full((nh, LRU_BLOCK, 2 * LRU_BLOCK)), full((nh, 2 * LRU_BLOCK)), full((1, rw))],
        out_shape=[jax.ShapeDtypeStruct((2, t, rw), BF16), jax.ShapeDtypeStruct((1, r2), F32),
                   jax.ShapeDtypeStruct((4, rw), F32), jax.ShapeDtypeStruct((1, rw), F32),
                   jax.ShapeDtypeStruct((nh, LRU_BLOCK, 2 * LRU_BLOCK), F32),
                   jax.ShapeDtypeStruct((nh, 2 * LRU_BLOCK), F32), jax.ShapeDtypeStruct((1, rw), F32)],
        scratch_shapes=[pltpu.VMEM((ts, rw), F32), pltpu.VMEM((ts, rw), F32), pltpu.VMEM((ts, rw), F32),
                        pltpu.VMEM((1, rw), F32), pltpu.VMEM((SUB, rw), F32), pltpu.VMEM((SUB, rw), F32)],
        compiler_params=_cp("arbitrary"))(h, h, h, hs, hs, dy, cw, cb, wg, bg, lam)


def _row_tile(rows, cols, mult, elems=ELEMS_PER_BLOCK):
    cap = max(mult, elems // cols)
    best = None
    for cand in range(mult, min(rows, cap) + 1, mult):
        if rows % cand == 0:
            best = cand
    return best if best is not None else rows


def _core_index():
    return lax.axis_index("c").astype(jnp.int32).reshape(1)


def _chip_index():
    return (2 * lax.axis_index("x") + lax.axis_index("y")).astype(jnp.int32).reshape(1)


def _add_pair(p4, r3, name):
    s, _, rows, cols = p4.shape
    tr = _row_tile(rows, cols, SUB16)

    def body(c_ref, a_ref, b_ref, o_ref):
        o_ref[...] = (a_ref[...].astype(F32) + b_ref[...].astype(F32)).astype(o_ref.dtype)

    blk = pl.BlockSpec((None, tr, cols), lambda k, i, c_ref: (k, i, 0))
    return pl.pallas_call(
        body, name=name,
        grid_spec=pltpu.PrefetchScalarGridSpec(
            num_scalar_prefetch=1, grid=(s, rows // tr),
            in_specs=[pl.BlockSpec((None, None, tr, cols), lambda k, i, c_ref: (k, c_ref[0], i, 0)), blk],
            out_specs=blk),
        out_shape=jax.ShapeDtypeStruct((s, rows, cols), p4.dtype),
        compiler_params=_cp("parallel", "parallel"))(_core_index(), p4, r3)


def _add_chips(r, name):
    shape = r.shape[1:]
    r3 = r.reshape(N_CHIPS, -1, shape[-1])
    _, rows, cols = r3.shape
    tr = _row_tile(rows, cols, SUB16)

    def body(r_ref, o_ref):
        s = r_ref[0].astype(F32) + r_ref[1].astype(F32)
        s = s + r_ref[2].astype(F32)
        o_ref[...] = s + r_ref[3].astype(F32)

    out = pl.pallas_call(body, name=name, grid=(rows // tr,),
                         in_specs=[pl.BlockSpec((N_CHIPS, tr, cols), lambda i: (0, i, 0))],
                         out_specs=pl.BlockSpec((tr, cols), lambda i: (i, 0)),
                         out_shape=jax.ShapeDtypeStruct((rows, cols), F32),
                         compiler_params=_cp("parallel"))(r3)
    return out.reshape(shape)


def _adamw(w3, g_mine, g_sib, m3, v3, name):
    nl, r, cols = w3.shape
    rows = r // 2
    flat = [arr.reshape(nl, 2, rows, cols) for arr in (w3, m3, v3)]
    tr = _row_tile(rows, cols, SUB)

    def body(c_ref, w_ref, gm_ref, gs_ref, m_ref, v_ref, g_ref, d_ref, mo_ref, vo_ref):
        gg = jnp.where(pl.program_id(1) == c_ref[0], gm_ref[...], gs_ref[...])
        m2 = ADAM_B1 * m_ref[...] + (1.0 - ADAM_B1) * gg
        v2 = ADAM_B2 * v_ref[...] + (1.0 - ADAM_B2) * (gg * gg)
        m_hat = m2 / (1.0 - ADAM_B1 ** ADAM_STEP)
        v_hat = v2 / (1.0 - ADAM_B2 ** ADAM_STEP)
        g_ref[...] = gg
        d_ref[...] = -ADAM_LR * (m_hat / (jnp.sqrt(v_hat) + ADAM_EPS) + ADAM_WD * w_ref[...])
        mo_ref[...] = m2
        vo_ref[...] = v2

    blk = pl.BlockSpec((None, None, tr, cols), lambda l, hh, i, c_ref: (l, hh, i, 0))
    gblk = pl.BlockSpec((None, tr, cols), lambda l, hh, i, c_ref: (l, i, 0))
    outs = pl.pallas_call(
        body, name=name,
        grid_spec=pltpu.PrefetchScalarGridSpec(
            num_scalar_prefetch=1, grid=(nl, 2, rows // tr),
            in_specs=[blk, gblk, gblk, blk, blk], out_specs=[blk] * 4),
        out_shape=[jax.ShapeDtypeStruct((nl, 2, rows, cols), F32)] * 4,
        compiler_params=_cp("parallel", "parallel", "parallel"))(_core_index(), flat[0], g_mine, g_sib, flat[1],
                                                                 flat[2])
    return tuple(o.reshape(nl, r, cols) for o in outs)


def _place(src3, layer, kind, dtype, name):
    _, r, c = src3.shape
    tr = _row_tile(r, c, SUB16)
    in_spec = pl.BlockSpec((None, tr, c), lambda i, my_ref: (layer, i, 0))
    if kind == "col":
        out_spec = pl.BlockSpec((tr, c), lambda i, my_ref: (i, my_ref[0]))
        out_shape = (r, N_CHIPS * c)
    else:
        out_spec = pl.BlockSpec((None, tr, c), lambda i, my_ref: (my_ref[0], i, 0))
        out_shape = (N_CHIPS, r, c)

    def body(my_ref, i_ref, o_ref):
        o_ref[...] = i_ref[...].astype(o_ref.dtype)

    return pl.pallas_call(
        body, name=name,
        grid_spec=pltpu.PrefetchScalarGridSpec(num_scalar_prefetch=1, grid=(r // tr,), in_specs=[in_spec],
                                               out_specs=out_spec),
        out_shape=jax.ShapeDtypeStruct(out_shape, dtype),
        compiler_params=_cp("parallel"))(_chip_index(), src3)


def _half(ref, c, h):
    return ref.at[pl.ds(c * h, h)]


def _position():
    x = lax.axis_index("x")
    y = lax.axis_index("y")
    c = lax.axis_index("c")
    return x, y, c


def _peer_chip(x, y, j):
    tx = 1 - x if j & 2 else x
    ty = 1 - y if j & 1 else y
    return tx, ty


def _remote(src, dst, ssem, rsem, dev):
    return pltpu.make_async_remote_copy(src_ref=src, dst_ref=dst, send_sem=ssem, recv_sem=rsem,
                                        device_id=dev, device_id_type=pl.DeviceIdType.MESH)


_ANY = pl.BlockSpec(memory_space=pl.ANY)


def _unit_view(kind, ref, k):
    if kind == "col":
        n = ref.shape[1] // N_CHIPS
        return ref.at[:, pl.ds(pl.multiple_of(k * n, LANE), n)]
    return ref.at[k]


def _all_gather(placed, kinds):
    nt = len(placed)

    def body(*refs):
        outs = refs[nt:2 * nt]
        ssem, rsem = refs[2 * nt:]
        x, y, c = _position()
        my = 2 * x + y
        sib = (x, y, 1 - c)

        def part(t, k, core):
            view = _unit_view(kinds[t], outs[t], k)
            h = view.shape[0] // 2
            return _half(view, core, h)

        sends, fwds = [], []
        for t in range(nt):
            own = part(t, my, c)
            for j in (1, 2, 3):
                tx, ty = _peer_chip(x, y, j)
                cp = _remote(own, own, ssem.at[6 * t + j - 1], rsem.at[6 * t + j - 1], (tx, ty, c))
                cp.start()
                sends.append(cp)
        for t in range(nt):
            for j in (1, 2, 3):
                tx, ty = _peer_chip(x, y, j)
                got = part(t, 2 * tx + ty, c)
                _remote(got, got, ssem.at[6 * t + j - 1], rsem.at[6 * t + j - 1], sib).wait_recv()
                cp = _remote(got, got, ssem.at[6 * t + 2 + j], rsem.at[6 * t + 2 + j], sib)
                cp.start()
                fwds.append(cp)
        for t in range(nt):
            for j in (1, 2, 3):
                tx, ty = _peer_chip(x, y, j)
                other = part(t, 2 * tx + ty, 1 - c)
                _remote(other, other, ssem.at[6 * t + 2 + j], rsem.at[6 * t + 2 + j], sib).wait_recv()
        for cp in sends + fwds:
            cp.wait_send()

    return pl.pallas_call(
        body, name="all_gather", in_specs=[_ANY] * nt, out_specs=[_ANY] * nt,
        out_shape=[jax.ShapeDtypeStruct(p.shape, p.dtype) for p in placed],
        input_output_aliases={t: t for t in range(nt)},
        scratch_shapes=[pltpu.SemaphoreType.DMA((6 * nt,)), pltpu.SemaphoreType.DMA((6 * nt,))],
    )(*placed)


def _rider_start(refs, kinds, ssem, rsem):
    x, y, c = _position()
    my = 2 * x + y
    for u, (ref, kind) in enumerate(zip(refs, kinds)):
        own = _unit_view(kind, ref, my)
        for j in (1, 2, 3):
            tx, ty = _peer_chip(x, y, j)
            _remote(own, own, ssem.at[3 * u + j - 1], rsem.at[3 * u + j - 1], (tx, ty, c)).start()


def _rider_wait(refs, kinds, ssem, rsem):
    x, y, c = _position()
    for u, (ref, kind) in enumerate(zip(refs, kinds)):
        for j in (1, 2, 3):
            tx, ty = _peer_chip(x, y, j)
            got = _unit_view(kind, ref, 2 * tx + ty)
            _remote(got, got, ssem.at[3 * u + j - 1], rsem.at[3 * u + j - 1], (tx, ty, c)).wait()


def _rider_args(rider):
    bufs = [b for b, _ in rider]
    kinds = [k for _, k in rider]
    n = len(bufs)
    sems = [pltpu.SemaphoreType.DMA((3 * n,)), pltpu.SemaphoreType.DMA((3 * n,))] if n else []
    return bufs, kinds, [_ANY] * n, [jax.ShapeDtypeStruct(b.shape, b.dtype) for b in bufs], sems


def _d2d_stream(src4, other_half, name):
    s, _, rows, cols = src4.shape
    tr = _row_tile(rows, cols, SUB16, STREAM_ELEMS_PER_BLOCK)
    nblk = rows // tr

    nh = src4.shape[1]

    def body(c_ref, src_ref, dst_ref, ssem, rsem):
        k = pl.program_id(0)
        i = pl.program_id(1)
        x, y, c = _position()
        sib = (x, y, 1 - c)
        blk = dst_ref.at[pl.ds(pl.multiple_of((k * nblk + i) * tr, SUB16), tr)]
        cp = _remote(src_ref, blk, ssem, rsem, sib)
        cp.start()
        cp.wait_send()

        @pl.when(jnp.logical_and(k == s - 1, i == nblk - 1))
        def _():
            _remote(dst_ref, dst_ref, ssem, rsem, sib).wait_recv()

    if other_half:
        src_map = lambda k, i, c_ref: ((k * nh + 1 - c_ref[0]) * nblk + i, 0)
    else:
        src_map = lambda k, i, c_ref: (k * nh * nblk + i, 0)
    out = pl.pallas_call(
        body, name=name,
        grid_spec=pltpu.PrefetchScalarGridSpec(
            num_scalar_prefetch=1, grid=(s, nblk),
            in_specs=[pl.BlockSpec((tr, cols), src_map)], out_specs=_ANY,
            scratch_shapes=[pltpu.SemaphoreType.DMA, pltpu.SemaphoreType.DMA]),
        out_shape=jax.ShapeDtypeStruct((s * rows, cols), src4.dtype),
        compiler_params=_cp("arbitrary", "arbitrary"))(_core_index(), src4.reshape(s * nh * rows, cols))
    return out.reshape(s, rows, cols)


def _scatter_copies(srcs, lands, kinds, ssem, rsem, lsem):
    x, y, c = _position()
    my = 2 * x + y
    cps = []
    for u, (src, land, kind) in enumerate(zip(srcs, lands, kinds)):
        cps.append(pltpu.make_async_copy(_unit_view(kind, src, my), land.at[my], lsem.at[u]))
        for j in (1, 2, 3):
            tx, ty = _peer_chip(x, y, j)
            cps.append(_remote(_unit_view(kind, src, 2 * tx + ty), land.at[my], ssem.at[3 * u + j - 1],
                               rsem.at[3 * u + j - 1], (tx, ty, c)))
    return cps


def _scatter_args(scatter):
    srcs = [s for s, _ in scatter]
    kinds = [k for _, k in scatter]
    n = len(srcs)
    shapes = [jax.ShapeDtypeStruct((N_CHIPS, s.shape[0], s.shape[1] // N_CHIPS) if k == "col" else s.shape, s.dtype)
              for s, k in scatter]
    sems = [pltpu.SemaphoreType.DMA((3 * n,)), pltpu.SemaphoreType.DMA((3 * n,)),
            pltpu.SemaphoreType.DMA((n,))] if n else []
    return srcs, kinds, [_ANY] * n, shapes, sems


def _rs_scatter(scatter):
    srcs, kinds, specs, shapes, sems = _scatter_args(scatter)
    n = len(srcs)

    def body(*refs):
        cps = _scatter_copies(refs[:n], refs[n:2 * n], kinds, *refs[2 * n:])
        for cp in cps:
            cp.start()
        for cp in cps:
            cp.wait()

    return pl.pallas_call(body, name="rs_scatter", in_specs=specs, out_specs=specs, out_shape=shapes,
                          scratch_shapes=sems)(*srcs)


SMALL = (("sc_conv_w", True), ("sc_conv_b", False), ("lru_b_in", True), ("lru_conv_w", True),
         ("lru_conv_b", True), ("lru_b_gate", True), ("lru_lambda", True), ("ffn_conv_w", True),
         ("ffn_conv_b", False), ("ln_g", True), ("ln_b", True))
PACK_ROW_MULT = 2 * SUB16


def _pack_rows(shapes):
    n = sum(math.prod(shapes[name]) for name, _ in SMALL)
    rows = -(-n // 128)
    return -(-rows // PACK_ROW_MULT) * PACK_ROW_MULT


def _pack_local(vals, shapes):
    flat = jnp.concatenate([vals[name].reshape(-1) for name, _ in SMALL])
    rows = _pack_rows(shapes)
    return jnp.pad(flat, (0, rows * 128 - flat.shape[0])).reshape(rows, 128)


def _unpack_local(pack, shapes):
    flat = pack.reshape(-1)
    out, off = {}, 0
    for name, _ in SMALL:
        n = math.prod(shapes[name])
        out[name] = flat[off:off + n].reshape(shapes[name])
        off += n
    return out


def _pack_slots(fulls, shapes):
    parts = []
    for name, sharded in SMALL:
        v = fulls[name]
        if sharded:
            ns = shapes[name][-1]
            v = jnp.moveaxis(v.reshape(v.shape[:-1] + (N_CHIPS, ns)), -2, 0).reshape(N_CHIPS, -1)
        else:
            v = jnp.broadcast_to(v.reshape(1, -1), (N_CHIPS, v.size))
        parts.append(v)
    flat = jnp.concatenate(parts, axis=1)
    rows = _pack_rows(shapes)
    return jnp.pad(flat, ((0, 0), (0, rows * 128 - flat.shape[1]))).reshape(N_CHIPS, rows, 128)


def _unpack_slots(packs, shapes):
    flat = packs.reshape(N_CHIPS, -1)
    out, off = {}, 0
    for name, sharded in SMALL:
        n = math.prod(shapes[name])
        if sharded:
            seg = flat[:, off:off + n].reshape((N_CHIPS,) + tuple(shapes[name]))
            seg = jnp.moveaxis(seg, 0, -2)
            out[name] = seg.reshape(seg.shape[:-2] + (N_CHIPS * shapes[name][-1],))
        off += n
    return out


WEIGHTS = ("sc_w_in", "sc_conv_w", "sc_conv_b", "sc_w_out", "lru_w_in", "lru_b_in", "lru_conv_w", "lru_conv_b",
           "lru_w_gate", "lru_b_gate", "lru_lambda", "lru_w_out", "ffn_w_up", "ffn_conv_w", "ffn_conv_b",
           "ffn_w_down", "ln_g", "ln_b")
GATHER_KIND = {"sc_w_in": "col", "sc_w_out": "lead", "lru_w_in": "col", "lru_w_out": "lead", "ffn_w_up": "col",
               "ffn_w_down": "lead"}


def kernel(x, sc_w_in, sc_conv_w, sc_conv_b, sc_w_out, lru_w_in, lru_b_in, lru_conv_w, lru_conv_b, lru_w_gate, lru_b_gate, lru_lambda, lru_w_out, ffn_w_up, ffn_conv_w, ffn_conv_b, ffn_w_down, ln_g, ln_b, loss_target, m_sc_w_in, m_sc_conv_w, m_sc_conv_b, m_sc_w_out, m_lru_w_in, m_lru_b_in, m_lru_conv_w, m_lru_conv_b, m_lru_w_gate, m_lru_b_gate, m_lru_lambda, m_lru_w_out, m_ffn_w_up, m_ffn_conv_w, m_ffn_conv_b, m_ffn_w_down, m_ln_g, m_ln_b, v_sc_w_in, v_sc_conv_w, v_sc_conv_b, v_sc_w_out, v_lru_w_in, v_lru_b_in, v_lru_conv_w, v_lru_conv_b, v_lru_w_gate, v_lru_b_gate, v_lru_lambda, v_lru_w_out, v_ffn_w_up, v_ffn_conv_w, v_ffn_conv_b, v_ffn_w_down, v_ln_g, v_ln_b):
    w = dict(zip(WEIGHTS, (sc_w_in, sc_conv_w, sc_conv_b, sc_w_out, lru_w_in, lru_b_in, lru_conv_w, lru_conv_b,
                           lru_w_gate, lru_b_gate, lru_lambda, lru_w_out, ffn_w_up, ffn_conv_w, ffn_conv_b,
                           ffn_w_down, ln_g, ln_b)))
    mom = dict(zip(WEIGHTS, (m_sc_w_in, m_sc_conv_w, m_sc_conv_b, m_sc_w_out, m_lru_w_in, m_lru_b_in, m_lru_conv_w,
                             m_lru_conv_b, m_lru_w_gate, m_lru_b_gate, m_lru_lambda, m_lru_w_out, m_ffn_w_up,
                             m_ffn_conv_w, m_ffn_conv_b, m_ffn_w_down, m_ln_g, m_ln_b)))
    vel = dict(zip(WEIGHTS, (v_sc_w_in, v_sc_conv_w, v_sc_conv_b, v_sc_w_out, v_lru_w_in, v_lru_b_in, v_lru_conv_w,
                             v_lru_conv_b, v_lru_w_gate, v_lru_b_gate, v_lru_lambda, v_lru_w_out, v_ffn_w_up,
                             v_ffn_conv_w, v_ffn_conv_b, v_ffn_w_down, v_ln_g, v_ln_b)))
    bd, seq, d = x.shape
    t = bd * seq
    small_shapes = {name: w[name].shape for name, _ in SMALL}

    w_pack = _pack_local(w, small_shapes)
    gate_shape = w["lru_w_gate"].shape
    bufs = {(n, l): (_place(w[n], l, k, BF16, "place_w"), k)
            for n, k in GATHER_KIND.items() for l in range(w[n].shape[0])}
    bufs["gate"] = (_place(w["lru_w_gate"].reshape(1, -1, gate_shape[-1]), 0, "lead", BF16, "place_w"), "lead")
    bufs["pack"] = (_place(w_pack[None], 0, "lead", F32, "place_w"), "lead")

    def layer_keys(i):
        mixer = ("sc_w_in", "sc_w_out") if i % 2 == 0 else ("lru_w_in", "lru_w_out")
        return [(mixer[0], i // 2), (mixer[1], i // 2)], [("ffn_w_up", i), ("ffn_w_down", i)]

    def gathered(keys, arrays):
        for key, arr in zip(keys, arrays):
            bufs[key] = (arr, bufs[key][1])

    def wt(name, l):
        arr = bufs[(name, l)][0]
        return arr.reshape(1, -1, arr.shape[-1])

    first = layer_keys(0)[0] + layer_keys(0)[1] + ["gate", "pack"]
    gathered(first, _all_gather([bufs[k][0] for k in first], [bufs[k][1] for k in first]))
    full = _unpack_slots(bufs["pack"][0], small_shapes)
    full["sc_conv_b"] = sc_conv_b
    full["ffn_conv_b"] = ffn_conv_b
    wg_full = jnp.moveaxis(bufs["gate"][0].reshape((N_CHIPS,) + gate_shape), 0, -2)
    wg_full = wg_full.reshape(wg_full.shape[:-2] + (2 * LRU_BLOCK,))
    f = N_CHIPS * w["ffn_w_down"].shape[1]
    rw = N_CHIPS * w["lru_w_out"].shape[1]

    x0 = x.reshape(t, d)
    xb = x0.astype(BF16)
    cur, cur_b = x0, xb
    saved = []

    for i in range(DEPTH):
        j = i // 2
        s = {"xb": cur_b}
        mixer_next, ffn_next = layer_keys(i + 1) if i + 1 < DEPTH else ([], [])
        behind_out, behind_up, behind_down = mixer_next[1:], mixer_next[:1] + ffn_next[:1], ffn_next[1:]
        if i % 2 == 0:
            h = _mm_nn(cur_b, wt("sc_w_in", j), 0, None, 3 * d, "sc_in")
            q = _sc_fwd(h, full["sc_conv_w"][j], full["sc_conv_b"][j][None], seq, "sc_fwd")
            z1, x1, x1b, *arrived = _mm_nn_ln(q, wt("sc_w_out", j), 0, cur, full["ln_g"][i, 0][None],
                                              full["ln_b"][i, 0][None], "sc_out_ln",
                                              rider=[bufs[k] for k in behind_out])
        else:
            h = _mm_nn(cur_b, wt("lru_w_in", j), 0, full["lru_b_in"][j][None], 2 * rw, "lru_in")
            hs, q = _lru_fwd(h, full["lru_conv_w"][j], full["lru_conv_b"][j][None], wg_full[j],
                             full["lru_b_gate"][j], full["lru_lambda"][j][None], seq, "lru_fwd")
            s["hs"] = hs
            z1, x1, x1b, *arrived = _mm_nn_ln(q, wt("lru_w_out", j), 0, cur, full["ln_g"][i, 0][None],
                                              full["ln_b"][i, 0][None], "lru_out_ln",
                                              rider=[bufs[k] for k in behind_out])
        gathered(behind_out, arrived)
        s.update(h=h, q=q, z1=z1, x1b=x1b)
        h3, pre3, act, *arrived = _ffn_up(x1b, wt("ffn_w_up", i), 0, full["ffn_conv_w"][i],
                                          full["ffn_conv_b"][i][None], seq, "ffn_up",
                                          rider=[bufs[k] for k in behind_up])
        gathered(behind_up, arrived)
        z2, x2, x2b, *arrived = _mm_nn_ln(act, wt("ffn_w_down", i), 0, x1, full["ln_g"][i, 1][None],
                                          full["ln_b"][i, 1][None], "ffn_down_ln",
                                          rider=[bufs[k] for k in behind_down])
        gathered(behind_down, arrived)
        s.update(h3=h3, pre3=pre3, act=act, z2=z2)
        saved.append(s)
        cur, cur_b = x2, x2b

    dcur, loss = cur, None

    def pair_sum(p, kind):
        lead = kind == "lead"
        p4 = p.reshape(N_CHIPS if lead else 1, 2, -1, p.shape[-1])
        chip_sum = _add_pair(p4, _d2d_stream(p4, True, "rs_swap"), "rs_add_pair")
        return chip_sum if lead else chip_sum[0]

    gp = {n: [None] * w[n].shape[0] for n, _ in SMALL}
    gp["lru_w_gate"] = [None] * gate_shape[0]
    landed, pending = {}, []
    for i in reversed(range(DEPTH)):
        j = i // 2
        s = saved[i]
        mixer_keys, ffn_keys = layer_keys(i)
        if i == DEPTH - 1:
            dz2, dz2b, dg, db, loss_parts = _ln_bwd(dcur, s["z2"], full["ln_g"][i, 1][None], "ln_bwd_loss",
                                                    target=loss_target.reshape(t, d))
            loss = lax.psum(jnp.sum(loss_parts), MESH_AXES)
        else:
            dz2, dz2b, dg, db = _ln_bwd(dcur, s["z2"], full["ln_g"][i, 1][None], "ln_bwd")
        gp["ln_g"][i] = [None, dg[0]]
        gp["ln_b"][i] = [None, db[0]]
        p_down = _mm_tn(s["act"], dz2b[None], f // 2, d, "ffn_down_dw").reshape(N_CHIPS, -1, d)
        dh3, dcwg, dcwv, dcbg, dcbv, *lands = _ffn_down_bwd(
            dz2b, wt("ffn_w_down", i), 0, s["h3"], s["pre3"], full["ffn_conv_w"][i], seq, "ffn_down_bwd",
            scatter=[(chip_sum, kind) for _, chip_sum, kind in pending])
        landed.update({key: land for (key, _, _), land in zip(pending, lands)})
        gp["ffn_conv_w"][i] = jnp.concatenate([dcwg, dcwv], axis=1)
        gp["ffn_conv_b"][i] = jnp.concatenate([dcbg[0], dcbv[0]])
        dx1 = _mm_nt_res(dh3, wt("ffn_w_up", i), 0, dz2, f // 2, "ffn_up_dx")
        p_up = _mm_tn(s["x1b"], dh3, d, f // 2, "ffn_up_dw")
        ffn_partials = ((ffn_keys[0], p_up, "col"), (ffn_keys[1], p_down, "lead"))
        early = [(key, pair_sum(p, kind), kind) for key, p, kind in ffn_partials] if i == 0 else []
        early_scatter = [(chip_sum, kind) for _, chip_sum, kind in early]
        dz1, dz1b, dg, db = _ln_bwd(dx1, s["z1"], full["ln_g"][i, 0][None], "ln_bwd")
        gp["ln_g"][i][0] = dg[0]
        gp["ln_b"][i][0] = db[0]
        gp["ln_g"][i] = jnp.stack(gp["ln_g"][i])
        gp["ln_b"][i] = jnp.stack(gp["ln_b"][i])
        if i % 2 == 0:
            dq = _mm_nt(dz1b, wt("sc_w_out", j), 0, d, "sc_out_dx")
            p_out = _mm_tn(s["q"], dz1b[None], d, d, "sc_out_dw")
            dh3, dcw, dcb = _sc_bwd(s["h"], dq, full["sc_conv_w"][j], full["sc_conv_b"][j][None], seq, "sc_bwd")
            gp["sc_conv_w"][j] = dcw
            gp["sc_conv_b"][j] = dcb[0]
            res = _mm_nt_res(dh3, wt("sc_w_in", j), 0, dz1, d, "sc_in_dx", scatter=early_scatter)
            dcur, lands = (res[0], res[1:]) if early else (res, [])
            landed.update({key: land for (key, _, _), land in zip(early, lands)})
            p_in = _mm_tn(s["xb"], dh3, d, d, "sc_in_dw")
        else:
            dq = _mm_nt(dz1b, wt("lru_w_out", j), 0, rw, "lru_out_dx")
            p_out = _mm_tn(s["q"], dz1b[None], rw, d, "lru_out_dw")
            dh3, dbin, dcw, dcb, dwg, dbg, dlam = _lru_bwd(
                s["h"], s["hs"], dq, full["lru_conv_w"][j], full["lru_conv_b"][j][None], wg_full[j],
                full["lru_b_gate"][j], full["lru_lambda"][j][None], seq, "lru_bwd")
            gp["lru_b_in"][j] = dbin[0]
            gp["lru_conv_w"][j] = dcw
            gp["lru_conv_b"][j] = dcb[0]
            gp["lru_w_gate"][j] = dwg
            gp["lru_b_gate"][j] = dbg
            gp["lru_lambda"][j] = dlam[0]
            dcur = _mm_nt_res(dh3, wt("lru_w_in", j), 0, dz1, rw, "lru_in_dx")
            p_in = _mm_tn(s["xb"], dh3, d, rw, "lru_in_dw")
        layer_partials = ((mixer_keys[0], p_in, "col"), (mixer_keys[1], p_out.reshape(N_CHIPS, -1, d), "lead"))
        layer_partials += () if early else ffn_partials
        pending = [(key, pair_sum(p, kind), kind) for key, p, kind in layer_partials]
    grad_x = dcur.reshape(bd, seq, d)
    gp = {n: jnp.stack(v) for n, v in gp.items()}

    gate = gp["lru_w_gate"]
    gate = jnp.moveaxis(gate.reshape(gate.shape[:-1] + (N_CHIPS, gate_shape[-1])), -2, 0)
    gate = gate.astype(BF16).reshape(N_CHIPS, -1, gate_shape[-1])
    pending += [("gate", pair_sum(gate, "lead"), "lead"),
                ("pack", pair_sum(_pack_slots(gp, small_shapes), "lead"), "lead")]
    lands = _rs_scatter([(chip_sum, kind) for _, chip_sum, kind in pending])
    landed.update({key: land for (key, _, _), land in zip(pending, lands)})

    def update(keys, w3, m3, v3):
        g_mine = jnp.stack([_add_chips(landed[k], "rs_add_chips") for k in keys])
        g_sib = _d2d_stream(g_mine.reshape(1, 1, -1, g_mine.shape[-1]), False, "rs_share").reshape(g_mine.shape)
        return _adamw(w3, g_mine, g_sib, m3, v3, "adamw")

    g_out, d_out, m_out, v_out = {}, {}, {}, {}
    for n in GATHER_KIND:
        outs = update([(n, l) for l in range(w[n].shape[0])], w[n], mom[n], vel[n])
        g_out[n], d_out[n], m_out[n], v_out[n] = outs
    as_rows = lambda a: a.reshape(1, -1, a.shape[-1])
    outs = update(["gate"], as_rows(w["lru_w_gate"]), as_rows(mom["lru_w_gate"]), as_rows(vel["lru_w_gate"]))
    g_out["lru_w_gate"], d_out["lru_w_gate"], m_out["lru_w_gate"], v_out["lru_w_gate"] = (
        o.reshape(gate_shape) for o in outs)
    packs = update(["pack"], w_pack[None], _pack_local(mom, small_shapes)[None], _pack_local(vel, small_shapes)[None])
    for dst, pack in zip((g_out, d_out, m_out, v_out), packs):
        dst.update(_unpack_local(pack[0], small_shapes))

    return (loss, grad_x, *[g_out[n] for n in WEIGHTS], *[d_out[n] for n in WEIGHTS],
            *[m_out[n] for n in WEIGHTS], *[v_out[n] for n in WEIGHTS])
```

```python
import math

import jax
import jax.numpy as jnp
from jax import lax
from jax.experimental import pallas as pl
from jax.experimental.pallas import tpu as pltpu

F32 = jnp.float32
BF16 = jnp.bfloat16

DEPTH = 4
LRU_HEADS = 10
LRU_BLOCK = 128
LRU_C = 8.0
LN_EPS = 1e-5
ALPHA = (2.0 * DEPTH) ** 0.25
ADAM_LR, ADAM_B1, ADAM_B2, ADAM_EPS, ADAM_WD, ADAM_STEP = 0.001, 0.9, 0.999, 1e-08, 0.01, 10
N_CHIPS = 4
MESH_AXES = ("x", "y", "c")

VMEM_LIMIT_BYTES = 48 * 1024 * 1024
TM_MM = 512
TM_RES = 1024
TT_MM = 2048
TM_SC = 256
FFN_COL_BLOCKS = 2
RC = 128
LANE = 128
MXU_COLS = 256
TS_LRU = 256
TM_LN = 512
ELEMS_PER_BLOCK = 512 * 1024
COPY_ELEMS_PER_BLOCK = 1024 * 1024
STREAM_ELEMS_PER_BLOCK = 2048 * 1024
SUB = 8
SUB16 = 16


def _cp(*sem):
    return pltpu.CompilerParams(dimension_semantics=sem, vmem_limit_bytes=VMEM_LIMIT_BYTES)


def _sigmoid(v):
    return 0.5 + 0.5 * jnp.tanh(0.5 * v)


def _softplus(v):
    e = jnp.exp(-jnp.abs(v))
    log1p = jnp.where(e < 1e-3, e * (1.0 - e * (0.5 - e * (1.0 / 3.0))), jnp.log(1.0 + e))
    return jnp.maximum(v, 0.0) + log1p


def _one_minus_exp(v):
    series = -v * (1.0 + v * (0.5 + v * (1.0 / 6.0 + v * (1.0 / 24.0))))
    return jnp.where(v > -0.02, series, 1.0 - jnp.exp(v))


def _gelu_and_grad(v):
    k = math.sqrt(2.0 / math.pi)
    t = jnp.tanh(k * (v + 0.044715 * v * v * v))
    val = 0.5 * v * (1.0 + t)
    grad = 0.5 * (1.0 + t) + 0.5 * v * (1.0 - t * t) * k * (1.0 + 3.0 * 0.044715 * v * v)
    return val, grad


def _down(ext, k, n_head):
    if k:
        ext = pltpu.roll(ext, k, 0)
    return ext[n_head:]


def _up(ext, k, n):
    if k:
        ext = pltpu.roll(ext, ext.shape[0] - k, 0)
    return ext[:n]


def _row(ref, k):
    return ref[k:k + 1, :]


def _colsum(v):
    return jnp.sum(v, axis=0, keepdims=True)


def _mm_nn(a, w3, l, bias, tn, name):
    m, k = a.shape
    n = w3.shape[2]
    tm = min(TM_MM, m)
    has_bias = bias is not None

    def body(*refs):
        if has_bias:
            a_ref, w_ref, b_ref, o_ref = refs
        else:
            a_ref, w_ref, o_ref = refs
        acc = jnp.dot(a_ref[...], w_ref[...], preferred_element_type=F32)
        if has_bias:
            acc = acc + b_ref[...]
        o_ref[...] = acc.astype(o_ref.dtype)

    in_specs = [pl.BlockSpec((tm, k), lambda i, j: (i, 0)),
                pl.BlockSpec((None, k, tn), lambda i, j: (l, 0, j))]
    args = [a, w3]
    if has_bias:
        in_specs.append(pl.BlockSpec((1, tn), lambda i, j: (0, j)))
        args.append(bias)
    return pl.pallas_call(
        body, name=name, grid=(m // tm, n // tn), in_specs=in_specs,
        out_specs=pl.BlockSpec((tm, tn), lambda i, j: (i, j)),
        out_shape=jax.ShapeDtypeStruct((m, n), BF16),
        compiler_params=_cp("parallel", "arbitrary"))(*args)


def _mm_nn_ln(a, w3, l, xres, g, b, name, rider=()):
    m, k = a.shape
    n = w3.shape[2]
    tm = min(TM_MM, m)
    bufs, rkinds, rspecs, rshapes, rsems = _rider_args(rider)
    nr = len(bufs)

    def body(*refs):
        a_ref, w_ref, x_ref, g_ref, b_ref = refs[:5]
        z_ref, xn_ref, xb_ref = refs[5 + nr:8 + nr]
        rout, sems = refs[8 + nr:8 + 2 * nr], refs[8 + 2 * nr:]
        if nr:
            @pl.when(pl.program_id(0) == 0)
            def _():
                _rider_start(rout, rkinds, *sems)

        y = jnp.dot(a_ref[...], w_ref[...], preferred_element_type=F32)
        z = ALPHA * x_ref[...] + y
        mu = jnp.mean(z, axis=-1, keepdims=True)
        zc = z - mu
        var = jnp.mean(zc * zc, axis=-1, keepdims=True)
        xn = zc * lax.rsqrt(var + LN_EPS) * g_ref[...] + b_ref[...]
        z_ref[...] = z
        xn_ref[...] = xn
        xb_ref[...] = xn.astype(BF16)
        if nr:
            @pl.when(pl.program_id(0) == m // tm - 1)
            def _():
                _rider_wait(rout, rkinds, *sems)

    row = pl.BlockSpec((tm, n), lambda i: (i, 0))
    vec = pl.BlockSpec((1, n), lambda i: (0, 0))
    return pl.pallas_call(
        body, name=name, grid=(m // tm,),
        in_specs=[pl.BlockSpec((tm, k), lambda i: (i, 0)),
                  pl.BlockSpec((None, k, n), lambda i: (l, 0, 0)), row, vec, vec] + rspecs,
        out_specs=[row, row, row] + rspecs,
        out_shape=[jax.ShapeDtypeStruct((m, n), F32), jax.ShapeDtypeStruct((m, n), F32),
                   jax.ShapeDtypeStruct((m, n), BF16)] + rshapes,
        input_output_aliases={5 + u: 3 + u for u in range(nr)},
        scratch_shapes=rsems,
        compiler_params=_cp("arbitrary"))(a, w3, xres, g, b, *bufs)


def _mm_nt(a, w3, l, tk, name):
    m, n = a.shape
    kd = w3.shape[1]
    tm = min(TM_MM, m)

    def body(a_ref, w_ref, o_ref):
        o_ref[...] = lax.dot_general(a_ref[...], w_ref[...], (((1,), (1,)), ((), ())),
                                     preferred_element_type=F32).astype(o_ref.dtype)

    return pl.pallas_call(
        body, name=name, grid=(m // tm, kd // tk),
        in_specs=[pl.BlockSpec((tm, n), lambda i, j: (i, 0)),
                  pl.BlockSpec((None, tk, n), lambda i, j: (l, j, 0))],
        out_specs=pl.BlockSpec((tm, tk), lambda i, j: (i, j)),
        out_shape=jax.ShapeDtypeStruct((m, kd), BF16),
        compiler_params=_cp("parallel", "arbitrary"))(a, w3)


def _mm_nt_res(dh3, w3, l, dz, tc, name, scatter=()):
    g, m, cg = dh3.shape
    kd = w3.shape[1]
    ncg = cg // tc
    nk = g * ncg
    tm = min(TM_RES, m)
    srcs, skinds, sspecs, sshapes, ssems = _scatter_args(scatter)
    ns = len(srcs)

    def body(*refs):
        a_ref, w_ref, dz_ref = refs[:3]
        sin, o_ref, lands = refs[3:3 + ns], refs[3 + ns], refs[4 + ns:4 + 2 * ns]
        acc, sems = refs[4 + 2 * ns], refs[5 + 2 * ns:]
        i = pl.program_id(0)
        k = pl.program_id(1)
        if ns:
            @pl.when(jnp.logical_and(i == 0, k == 0))
            def _():
                for cp in _scatter_copies(sin, lands, skinds, *sems):
                    cp.start()

        @pl.when(k == 0)
        def _():
            acc[...] = ALPHA * dz_ref[...]

        acc[...] += lax.dot_general(a_ref[...], w_ref[...], (((1,), (1,)), ((), ())),
                                    preferred_element_type=F32)

        @pl.when(k == nk - 1)
        def _():
            o_ref[...] = acc[...]

        if ns:
            @pl.when(jnp.logical_and(i == m // tm - 1, k == nk - 1))
            def _():
                for cp in _scatter_copies(sin, lands, skinds, *sems):
                    cp.wait()

    outs = pl.pallas_call(
        body, name=name, grid=(m // tm, nk),
        in_specs=[pl.BlockSpec((None, tm, tc), lambda i, k: (k // ncg, i, k % ncg)),
                  pl.BlockSpec((None, kd, tc), lambda i, k: (l, 0, k)),
                  pl.BlockSpec((tm, kd), lambda i, k: (i, 0))] + sspecs,
        out_specs=[pl.BlockSpec((tm, kd), lambda i, k: (i, 0))] + sspecs,
        out_shape=[jax.ShapeDtypeStruct((m, kd), F32)] + sshapes,
        scratch_shapes=[pltpu.VMEM((tm, kd), F32)] + ssems,
        compiler_params=_cp("arbitrary", "arbitrary"))(dh3, w3, dz, *srcs)
    return outs if ns else outs[0]


def _mm_tn(a, b3, tka, tnb, name):
    t, ka = a.shape
    g, _, cg = b3.shape
    ncg = cg // tnb
    tt = min(TT_MM, t)
    nt = t // tt

    def body(a_ref, b_ref, o_ref, acc):
        s = pl.program_id(2)

        @pl.when(s == 0)
        def _():
            acc[...] = jnp.zeros_like(acc)

        acc[...] += lax.dot_general(a_ref[...], b_ref[...], (((0,), (0,)), ((), ())),
                                    preferred_element_type=F32)

        @pl.when(s == nt - 1)
        def _():
            o_ref[...] = acc[...].astype(o_ref.dtype)

    return pl.pallas_call(
        body, name=name, grid=(ka // tka, g * ncg, nt),
        in_specs=[pl.BlockSpec((tt, tka), lambda i, j, s: (s, i)),
                  pl.BlockSpec((None, tt, tnb), lambda i, j, s: (j // ncg, s, j % ncg))],
        out_specs=pl.BlockSpec((tka, tnb), lambda i, j, s: (i, j)),
        out_shape=jax.ShapeDtypeStruct((ka, g * cg), BF16),
        scratch_shapes=[pltpu.VMEM((tka, tnb), F32)],
        compiler_params=_cp("parallel", "parallel", "arbitrary"))(a, b3)


def _ln_bwd(dxn, z, g, name, target=None):
    m, d = z.shape
    tm = min(TM_LN, m)
    with_loss = target is not None

    def body(*refs):
        dx_ref, z_ref, g_ref = refs[:3]
        dz_ref, dzb_ref, dg_ref, db_ref = refs[3 + with_loss:7 + with_loss]

        @pl.when(pl.program_id(0) == 0)
        def _():
            for r in refs[5 + with_loss:]:
                r[...] = jnp.zeros_like(r)

        zz = z_ref[...]
        dx = dx_ref[...]
        if with_loss:
            err = dx - refs[3][...]
            dx = err * (1.0 / d)
            refs[8][...] += _colsum(err * err) * (0.5 / d)
        mu = jnp.mean(zz, axis=-1, keepdims=True)
        zc = zz - mu
        var = jnp.mean(zc * zc, axis=-1, keepdims=True)
        rstd = lax.rsqrt(var + LN_EPS)
        xh = zc * rstd
        dg_ref[...] += _colsum(dx * xh)
        db_ref[...] += _colsum(dx)
        dxh = dx * g_ref[...]
        m1 = jnp.mean(dxh, axis=-1, keepdims=True)
        m2 = jnp.mean(dxh * xh, axis=-1, keepdims=True)
        dz = rstd * (dxh - m1 - xh * m2)
        dz_ref[...] = dz
        dzb_ref[...] = dz.astype(BF16)

    row = pl.BlockSpec((tm, d), lambda i: (i, 0))
    vec = pl.BlockSpec((1, d), lambda i: (0, 0))
    vec_shape = jax.ShapeDtypeStruct((1, d), F32)
    return pl.pallas_call(
        body, name=name, grid=(m // tm,), in_specs=[row, row, vec] + [row] * with_loss,
        out_specs=[row, row, vec, vec] + [vec] * with_loss,
        out_shape=[jax.ShapeDtypeStruct((m, d), F32), jax.ShapeDtypeStruct((m, d), BF16), vec_shape, vec_shape]
        + [vec_shape] * with_loss,
        compiler_params=_cp("arbitrary"))(dxn, z, g, *([target] if with_loss else []))


def _sc_fwd(h, cw, cb, seq, name):
    t, d3 = h.shape
    d = d3 // 3
    tm = min(TM_SC, seq)

    def body(hb_ref, hc_ref, hv_ref, cw_ref, cb_ref, q_ref, carry):
        i = pl.program_id(0)

        @pl.when(lax.rem(i * tm, seq) == 0)
        def _():
            carry[...] = jnp.zeros_like(carry)

        p = hc_ref[...].astype(F32) * hv_ref[...].astype(F32)
        ext = jnp.concatenate([carry[...], p], axis=0)
        u = cb_ref[...] + _row(cw_ref, 0) * _down(ext, 2, SUB) + _row(cw_ref, 1) * _down(ext, 1, SUB) \
            + _row(cw_ref, 2) * p
        q_ref[...] = (hb_ref[...].astype(F32) * u).astype(BF16)
        carry[...] = p[tm - SUB:, :]

    blk = lambda c: pl.BlockSpec((tm, d), lambda i: (i, c))
    return pl.pallas_call(
        body, name=name, grid=(t // tm,),
        in_specs=[blk(0), blk(1), blk(2), pl.BlockSpec((3, d), lambda i: (0, 0)),
                  pl.BlockSpec((1, d), lambda i: (0, 0))],
        out_specs=pl.BlockSpec((tm, d), lambda i: (i, 0)),
        out_shape=jax.ShapeDtypeStruct((t, d), BF16),
        scratch_shapes=[pltpu.VMEM((SUB, d), F32)],
        compiler_params=_cp("arbitrary"))(h, h, h, cw, cb)


def _sc_bwd(h, dq, cw, cb, seq, name):
    t, d3 = h.shape
    d = d3 // 3
    tm = min(TM_SC, seq)
    nt = t // tm
    hpt = tm // SUB16

    def body(hb_ref, hc_ref, hv_ref, hch_ref, hvh_ref, dq_ref, cw_ref, cb_ref,
             dh_ref, dcw_ref, dcb_ref, carry):
        i = pl.program_id(0)
        ri = nt - 1 - i

        @pl.when(i == 0)
        def _():
            dcw_ref[...] = jnp.zeros_like(dcw_ref)
            dcb_ref[...] = jnp.zeros_like(dcb_ref)

        @pl.when(lax.rem((ri + 1) * tm, seq) == 0)
        def _():
            carry[...] = jnp.zeros_like(carry)

        keep = jnp.where(lax.rem(ri * tm, seq) == 0, 0.0, 1.0)
        gb = hb_ref[...].astype(F32)
        gc = hc_ref[...].astype(F32)
        v = hv_ref[...].astype(F32)
        p = gc * v
        p_head = hch_ref[...].astype(F32) * hvh_ref[...].astype(F32) * keep
        ext = jnp.concatenate([p_head, p], axis=0)
        pm2 = _down(ext, 2, SUB16)
        pm1 = _down(ext, 1, SUB16)
        u = cb_ref[...] + _row(cw_ref, 0) * pm2 + _row(cw_ref, 1) * pm1 + _row(cw_ref, 2) * p
        dqf = dq_ref[...].astype(F32)
        du = dqf * gb
        dcb_ref[...] += _colsum(du)
        dcw_ref[0:1, :] += _colsum(du * pm2)
        dcw_ref[1:2, :] += _colsum(du * pm1)
        dcw_ref[2:3, :] += _colsum(du * p)
        ext2 = jnp.concatenate([du, carry[...]], axis=0)
        dp = _row(cw_ref, 2) * du + _row(cw_ref, 1) * _up(ext2, 1, tm) + _row(cw_ref, 0) * _up(ext2, 2, tm)
        carry[...] = du[0:SUB, :]
        dh_ref[0] = (dqf * u).astype(BF16)
        dh_ref[1] = (dp * v).astype(BF16)
        dh_ref[2] = (dp * gc).astype(BF16)

    blk = lambda c: pl.BlockSpec((tm, d), lambda i: (nt - 1 - i, c))
    head = lambda c: pl.BlockSpec((SUB16, d), lambda i: (jnp.maximum((nt - 1 - i) * hpt - 1, 0), c))
    vec = lambda r: pl.BlockSpec((r, d), lambda i: (0, 0))
    return pl.pallas_call(
        body, name=name, grid=(nt,),
        in_specs=[blk(0), blk(1), blk(2), head(1), head(2),
                  pl.BlockSpec((tm, d), lambda i: (nt - 1 - i, 0)), vec(3), vec(1)],
        out_specs=[pl.BlockSpec((3, tm, d), lambda i: (0, nt - 1 - i, 0)), vec(3), vec(1)],
        out_shape=[jax.ShapeDtypeStruct((3, t, d), BF16), jax.ShapeDtypeStruct((3, d), F32),
                   jax.ShapeDtypeStruct((1, d), F32)],
        scratch_shapes=[pltpu.VMEM((SUB, d), F32)],
        compiler_params=_cp("arbitrary"))(h, h, h, h, h, dq, cw, cb)


def _fold(v):
    return jnp.sum(v.reshape(v.shape[0] // SUB, SUB, v.shape[1]), axis=0)


def _ffn_up(xb, w3, l, cw, cb, seq, name, rider=()):
    t, d = xb.shape
    f = w3.shape[2] // 2
    tc = f // FFN_COL_BLOCKS
    tm = min(TM_MM, seq)
    nc = FFN_COL_BLOCKS
    bufs, rkinds, rspecs, rshapes, rsems = _rider_args(rider)
    nr = len(bufs)

    def body(*refs):
        x_ref, wg_ref, wv_ref, cwg_ref, cwv_ref, cbg_ref, cbv_ref = refs[:7]
        h_ref, pre_ref, act_ref = refs[7 + nr:10 + nr]
        rout = refs[10 + nr:10 + 2 * nr]
        eg, ev = refs[10 + 2 * nr:12 + 2 * nr]
        sems = refs[12 + 2 * nr:]
        i = pl.program_id(1)
        if nr:
            @pl.when(jnp.logical_and(pl.program_id(0) == 0, i == 0))
            def _():
                _rider_start(rout, rkinds, *sems)

        @pl.when(lax.rem(i * tm, seq) == 0)
        def _():
            eg[0:SUB, :] = jnp.zeros((SUB, tc), F32)
            ev[0:SUB, :] = jnp.zeros((SUB, tc), F32)

        xx = x_ref[...]

        def matmul(lo, hi):
            eg[SUB:, lo:hi] = jnp.dot(xx, wg_ref[:, lo:hi], preferred_element_type=F32)
            ev[SUB:, lo:hi] = jnp.dot(xx, wv_ref[:, lo:hi], preferred_element_type=F32)

        def epilogue(lo, hi):
            for c0 in range(lo, hi, LANE):
                cols = slice(c0, c0 + LANE)
                taps = [[ref[k:k + 1, cols] for k in range(3)] + [bref[:, cols]]
                        for ref, bref in ((cwg_ref, cbg_ref), (cwv_ref, cbv_ref))]
                for r0 in range(0, tm, RC):
                    rows = slice(r0, r0 + RC)
                    pres = []
                    for half, e_ref in enumerate((eg, ev)):
                        w0, w1, w2, bias = taps[half]
                        e = e_ref[r0:r0 + RC + SUB, cols]
                        cur = e[SUB:]
                        pre = bias + w0 * _down(e, 2, SUB) + w1 * _down(e, 1, SUB) + w2 * cur
                        h_ref[half, rows, cols] = cur.astype(BF16)
                        pre_ref[half, rows, cols] = pre.astype(BF16)
                        pres.append(pre)
                    act_ref[rows, cols] = (pres[0] * _sigmoid(pres[0]) * pres[1]).astype(BF16)

        blocks = [(lo, min(lo + MXU_COLS, tc)) for lo in range(0, tc, MXU_COLS)]
        matmul(*blocks[0])
        for b, blk in enumerate(blocks):
            if b + 1 < len(blocks):
                matmul(*blocks[b + 1])
            epilogue(*blk)
        eg[0:SUB, :] = eg[tm:tm + SUB, :]
        ev[0:SUB, :] = ev[tm:tm + SUB, :]
        if nr:
            @pl.when(jnp.logical_and(pl.program_id(0) == nc - 1, i == t // tm - 1))
            def _():
                _rider_wait(rout, rkinds, *sems)

    wspec = lambda off: pl.BlockSpec((None, d, tc), lambda j, i: (l, 0, j + off))
    vec = lambda r, off: pl.BlockSpec((r, tc), lambda j, i: (0, j + off))
    pair = pl.BlockSpec((2, tm, tc), lambda j, i: (0, i, j))
    return pl.pallas_call(
        body, name=name, grid=(nc, t // tm),
        in_specs=[pl.BlockSpec((tm, d), lambda j, i: (i, 0)), wspec(0), wspec(nc),
                  vec(3, 0), vec(3, nc), vec(1, 0), vec(1, nc)] + rspecs,
        out_specs=[pair, pair, pl.BlockSpec((tm, tc), lambda j, i: (i, j))] + rspecs,
        out_shape=[jax.ShapeDtypeStruct((2, t, f), BF16), jax.ShapeDtypeStruct((2, t, f), BF16),
                   jax.ShapeDtypeStruct((t, f), BF16)] + rshapes,
        input_output_aliases={7 + u: 3 + u for u in range(nr)},
        scratch_shapes=[pltpu.VMEM((tm + SUB, tc), F32), pltpu.VMEM((tm + SUB, tc), F32)] + rsems,
        compiler_params=_cp("arbitrary", "arbitrary"))(xb, w3, w3, cw, cw, cb, cb, *bufs)


def _ffn_down_bwd(dzb, wd3, l, h3, pre3, cw, seq, name, scatter=()):
    t, d = dzb.shape
    f = wd3.shape[1]
    tc = f // FFN_COL_BLOCKS
    tm = min(TM_MM, seq)
    nt = t // tm
    nc = FFN_COL_BLOCKS
    srcs, skinds, sspecs, sshapes, ssems = _scatter_args(scatter)
    ns = len(srcs)

    def body(*refs):
        dz_ref, wd_ref, h_ref, pre_ref, cwg_ref, cwv_ref = refs[:6]
        sin = refs[6:6 + ns]
        dh_ref, dcwg_ref, dcwv_ref, dcbg_ref, dcbv_ref = refs[6 + ns:11 + ns]
        lands = refs[11 + ns:11 + 2 * ns]
        da_s, carry = refs[11 + 2 * ns:13 + 2 * ns]
        sems = refs[13 + 2 * ns:]
        i = pl.program_id(1)
        ri = nt - 1 - i
        if ns:
            @pl.when(jnp.logical_and(pl.program_id(0) == 0, i == 0))
            def _():
                for cp in _scatter_copies(sin, lands, skinds, *sems):
                    cp.start()

        @pl.when(i == 0)
        def _():
            for r in (dcwg_ref, dcwv_ref, dcbg_ref, dcbv_ref):
                r[...] = jnp.zeros_like(r)

        @pl.when(lax.rem((ri + 1) * tm, seq) == 0)
        def _():
            carry[...] = jnp.zeros_like(carry)

        dz = dz_ref[...]

        def matmul(lo, hi):
            da_s[:, lo:hi] = lax.dot_general(dz, wd_ref[lo:hi, :], (((1,), (1,)), ((), ())),
                                             preferred_element_type=F32)

        def epilogue(lo, hi):
            for c0 in range(lo, hi, LANE):
                cols = slice(c0, c0 + LANE)
                taps = [[ref[k:k + 1, cols] for k in range(3)] for ref in (cwg_ref, cwv_ref)]
                acc = [jnp.zeros((SUB, LANE), F32)] * 8
                for r0 in range(tm - RC, -1, -RC):
                    rows = slice(r0, r0 + RC)
                    da = da_s[rows, cols]
                    gp = pre_ref[0, rows, cols].astype(F32)
                    vp = pre_ref[1, rows, cols].astype(F32)
                    sg = _sigmoid(gp)
                    dpres = (da * vp * (sg * (1.0 + gp * (1.0 - sg))), da * (gp * sg))
                    for half in range(2):
                        w0, w1, w2 = taps[half]
                        dpre = dpres[half]
                        ext = jnp.concatenate([dpre, carry[half, :, cols]], axis=0)
                        u1 = _up(ext, 1, RC)
                        u2 = _up(ext, 2, RC)
                        carry[half, :, cols] = dpre[0:SUB]
                        dh_ref[half, rows, cols] = (w2 * dpre + w1 * u1 + w0 * u2).astype(BF16)
                        hh = h_ref[half, rows, cols].astype(F32)
                        for k, term in enumerate((hh * u2, hh * u1, hh * dpre, dpre)):
                            acc[4 * half + k] = acc[4 * half + k] + _fold(term)
                for half, (dcw_ref, dcb_ref) in enumerate(((dcwg_ref, dcbg_ref), (dcwv_ref, dcbv_ref))):
                    for k in range(3):
                        dcw_ref[k:k + 1, cols] += _colsum(acc[4 * half + k])
                    dcb_ref[:, cols] += _colsum(acc[4 * half + 3])

        blocks = [(lo, min(lo + MXU_COLS, tc)) for lo in range(0, tc, MXU_COLS)]
        matmul(*blocks[0])
        for b, blk in enumerate(blocks):
            if b + 1 < len(blocks):
                matmul(*blocks[b + 1])
            epilogue(*blk)
        if ns:
            @pl.when(jnp.logical_and(pl.program_id(0) == nc - 1, i == nt - 1))
            def _():
                for cp in _scatter_copies(sin, lands, skinds, *sems):
                    cp.wait()

    pair = pl.BlockSpec((2, tm, tc), lambda j, i: (0, nt - 1 - i, j))
    vec = lambda off: pl.BlockSpec((3, tc), lambda j, i: (0, j + off))
    acc_spec = lambda r: pl.BlockSpec((r, tc), lambda j, i: (0, j))
    return pl.pallas_call(
        body, name=name, grid=(nc, nt),
        in_specs=[pl.BlockSpec((tm, d), lambda j, i: (nt - 1 - i, 0)),
                  pl.BlockSpec((None, tc, d), lambda j, i: (l, j, 0)), pair, pair, vec(0), vec(nc)] + sspecs,
        out_specs=[pair, acc_spec(3), acc_spec(3), acc_spec(1), acc_spec(1)] + sspecs,
        out_shape=[jax.ShapeDtypeStruct((2, t, f), BF16), jax.ShapeDtypeStruct((3, f), F32),
                   jax.ShapeDtypeStruct((3, f), F32), jax.ShapeDtypeStruct((1, f), F32),
                   jax.ShapeDtypeStruct((1, f), F32)] + sshapes,
        scratch_shapes=[pltpu.VMEM((tm, tc), F32), pltpu.VMEM((2, SUB, tc), F32)] + ssems,
        compiler_params=_cp("arbitrary", "arbitrary"))(dzb, wd3, h3, pre3, cw, cw, *srcs)


def _lru_gates(xr, wg_ref, bg_ref):
    rs, gs = [], []
    for hd in range(LRU_HEADS):
        xh = xr[:, hd * LRU_BLOCK:(hd + 1) * LRU_BLOCK].astype(BF16)
        gt = jnp.dot(xh, wg_ref[hd], preferred_element_type=F32) + _row(bg_ref, hd)
        rs.append(gt[:, :LRU_BLOCK])
        gs.append(gt[:, LRU_BLOCK:])
    return jnp.concatenate(rs, axis=1), jnp.concatenate(gs, axis=1)


def _lru_coeffs(xr, wg_ref, bg_ref, lam_ref):
    gr, gi = _lru_gates(xr, wg_ref, bg_ref)
    r = _sigmoid(gr)
    ig = _sigmoid(gi)
    sp = _softplus(-lam_ref[...])
    log_a = -LRU_C * r * sp
    a = jnp.exp(log_a)
    mult = jnp.sqrt(_one_minus_exp(2.0 * log_a))
    return r, ig, sp, a, mult


def _lru_fwd(h, cw, cb, wg, bg, lam, seq, name):
    t, r2 = h.shape
    rw = r2 // 2
    ts = min(TS_LRU, seq)
    n8 = ts // SUB

    def body(hg_ref, hr_ref, cw_ref, cb_ref, wg_ref, bg_ref, lam_ref, hs_ref, y_ref,
             a_s, b_s, cconv, cstate):
        i = pl.program_id(0)

        @pl.when(lax.rem(i * ts, seq) == 0)
        def _():
            cconv[...] = jnp.zeros_like(cconv)
            cstate[...] = jnp.zeros_like(cstate)

        rin = hr_ref[...].astype(F32)
        ext = jnp.concatenate([cconv[...], rin], axis=0)
        xr = cb_ref[...]
        for k in range(4):
            xr = xr + _row(cw_ref, k) * _down(ext, 3 - k, SUB)
        cconv[...] = rin[ts - SUB:, :]
        _, ig, _, a, mult = _lru_coeffs(xr, wg_ref, bg_ref, lam_ref)
        a_s[...] = a
        b_s[...] = mult * (ig * xr)
        row = lax.broadcasted_iota(jnp.int32, (SUB, rw), 0)

        def step(j, carry):
            off = pl.multiple_of(j * SUB, SUB)
            a8 = a_s[pl.ds(off, SUB), :]
            b8 = b_s[pl.ds(off, SUB), :]
            for d in (1, 2, 4):
                m = row >= d
                b8 = jnp.where(m, a8 * pltpu.roll(b8, d, 0) + b8, b8)
                a8 = jnp.where(m, a8 * pltpu.roll(a8, d, 0), a8)
            h8 = a8 * carry + b8
            hs_ref[pl.ds(off, SUB), :] = h8
            return _colsum(jnp.where(row == SUB - 1, h8, 0.0))

        cstate[...] = lax.fori_loop(0, n8, step, cstate[...])
        gel, _ = _gelu_and_grad(hg_ref[...].astype(F32))
        y_ref[...] = (hs_ref[...] * gel).astype(BF16)

    full = lambda shp: pl.BlockSpec(shp, lambda i: (0,) * len(shp))
    return pl.pallas_call(
        body, name=name, grid=(t // ts,),
        in_specs=[pl.BlockSpec((ts, rw), lambda i: (i, 0)), pl.BlockSpec((ts, rw), lambda i: (i, 1)),
                  full((4, rw)), full((1, rw)), full(wg.shape), full(bg.shape), full((1, rw))],
        out_specs=[pl.BlockSpec((ts, rw), lambda i: (i, 0)), pl.BlockSpec((ts, rw), lambda i: (i, 0))],
        out_shape=[jax.ShapeDtypeStruct((t, rw), F32), jax.ShapeDtypeStruct((t, rw), BF16)],
        scratch_shapes=[pltpu.VMEM((ts, rw), F32), pltpu.VMEM((ts, rw), F32),
                        pltpu.VMEM((SUB, rw), F32), pltpu.VMEM((1, rw), F32)],
        compiler_params=_cp("arbitrary"))(h, h, cw, cb, wg, bg, lam)


def _lru_bwd(h, hs, dy, cw, cb, wg, bg, lam, seq, name):
    t, r2 = h.shape
    rw = r2 // 2
    ts = min(TS_LRU, seq)
    nt = t // ts
    n8 = ts // SUB
    hp16 = ts // SUB16
    hp8 = ts // SUB

    def body(hg_ref, hr_ref, hrh_ref, hs_ref, hsh_ref, dy_ref, cw_ref, cb_ref, wg_ref, bg_ref, lam_ref,
             dh_ref, dbin_ref, dcw_ref, dcb_ref, dwg_ref, dbg_ref, dlam_ref,
             a_s, g_s, l_s, c_lam, c_a, c_dxr):
        i = pl.program_id(0)
        ri = nt - 1 - i

        @pl.when(i == 0)
        def _():
            for r in (dbin_ref, dcw_ref, dcb_ref, dwg_ref, dbg_ref, dlam_ref):
                r[...] = jnp.zeros_like(r)

        @pl.when(lax.rem((ri + 1) * ts, seq) == 0)
        def _():
            c_lam[...] = jnp.zeros_like(c_lam)
            c_a[...] = jnp.zeros_like(c_a)
            c_dxr[...] = jnp.zeros_like(c_dxr)

        keep = jnp.where(lax.rem(ri * ts, seq) == 0, 0.0, 1.0)
        rin = hr_ref[...].astype(F32)
        ext = jnp.concatenate([hrh_ref[...].astype(F32) * keep, rin], axis=0)
        shifted = [_down(ext, 3 - k, SUB16) for k in range(4)]
        xr = cb_ref[...]
        for k in range(4):
            xr = xr + _row(cw_ref, k) * shifted[k]
        r, ig, sp, a, mult = _lru_coeffs(xr, wg_ref, bg_ref, lam_ref)
        gel, dgel = _gelu_and_grad(hg_ref[...].astype(F32))
        dyf = dy_ref[...].astype(F32)
        hsv = hs_ref[...]
        dg = dyf * hsv * dgel

        a_s[...] = _up(jnp.concatenate([a, c_a[...]], axis=0), 1, ts)
        g_s[...] = dyf * gel
        c_a[...] = a[0:SUB, :]
        row = lax.broadcasted_iota(jnp.int32, (SUB, rw), 0)

        def step(j, carry):
            off = pl.multiple_of((n8 - 1 - j) * SUB, SUB)
            a8 = a_s[pl.ds(off, SUB), :]
            b8 = g_s[pl.ds(off, SUB), :]
            for d in (1, 2, 4):
                m = row < SUB - d
                b8 = jnp.where(m, a8 * pltpu.roll(b8, SUB - d, 0) + b8, b8)
                a8 = jnp.where(m, a8 * pltpu.roll(a8, SUB - d, 0), a8)
            l8 = a8 * carry + b8
            l_s[pl.ds(off, SUB), :] = l8
            return _colsum(jnp.where(row == 0, l8, 0.0))

        c_lam[...] = lax.fori_loop(0, n8, step, c_lam[...])
        lamv = l_s[...]
        hs_prev = _down(jnp.concatenate([hsh_ref[...] * keep, hsv], axis=0), 1, SUB)
        da = lamv * hs_prev
        t1 = lamv * xr
        dmult = t1 * ig
        dig = t1 * mult
        dxr = lamv * mult * ig
        dla = da * a - dmult * (a * a) / mult
        dr = dla * (-LRU_C * sp)
        dlam_ref[...] += _colsum(dla * (-LRU_C) * r) * (-1.0 / (1.0 + jnp.exp(lam_ref[...])))
        dgr = dr * r * (1.0 - r)
        dgi = dig * ig * (1.0 - ig)
        parts = []
        for hd in range(LRU_HEADS):
            sl = slice(hd * LRU_BLOCK, (hd + 1) * LRU_BLOCK)
            dgt = jnp.concatenate([dgr[:, sl], dgi[:, sl]], axis=1)
            dbg_ref[hd:hd + 1, :] += _colsum(dgt)
            dgt16 = dgt.astype(BF16)
            parts.append(lax.dot_general(dgt16, wg_ref[hd], (((1,), (1,)), ((), ())),
                                         preferred_element_type=F32))
            dwg_ref[hd] += lax.dot_general(xr[:, sl].astype(BF16), dgt16, (((0,), (0,)), ((), ())),
                                           preferred_element_type=F32)
        dxr = dxr + jnp.concatenate(parts, axis=1)

        dcb_ref[...] += _colsum(dxr)
        for k in range(4):
            dcw_ref[k:k + 1, :] += _colsum(dxr * shifted[k])
        ext2 = jnp.concatenate([dxr, c_dxr[...]], axis=0)
        drb = _row(cw_ref, 3) * dxr
        for k in range(3):
            drb = drb + _row(cw_ref, k) * _up(ext2, 3 - k, ts)
        c_dxr[...] = dxr[0:SUB, :]
        dh_ref[0] = dg.astype(BF16)
        dh_ref[1] = drb.astype(BF16)
        dbin_ref[:, 0:rw] += _colsum(dg)
        dbin_ref[:, rw:] += _colsum(drb)

    rev = lambda c: pl.BlockSpec((ts, rw), lambda i: (nt - 1 - i, c))
    full = lambda shp: pl.BlockSpec(shp, lambda i: (0,) * len(shp))
    nh = LRU_HEADS
    return pl.pallas_call(
        body, name=name, grid=(nt,),
        in_specs=[rev(0), rev(1),
                  pl.BlockSpec((SUB16, rw), lambda i: (jnp.maximum((nt - 1 - i) * hp16 - 1, 0), 1)),
                  rev(0),
                  pl.BlockSpec((SUB, rw), lambda i: (jnp.maximum((nt - 1 - i) * hp8 - 1, 0), 0)),
                  rev(0), full((4, rw)), full((1, rw)), full(wg.shape), full(bg.shape), full((1, rw))],
        out_specs=[pl.BlockSpec((2, ts, rw), lambda i: (0, nt - 1 - i, 0)), full((1, r2)), full((4, rw)),
                   full((1, rw)), full((nh, LRU_BLOCK, 2 * LRU_BLOCK)), full((nh, 2 * LRU_BLOCK)), full((1, rw))],
        out_shape=[jax.ShapeDtypeStruct((2, t, rw), BF16), jax.ShapeDtypeStruct((1, r2), F32),
                   jax.ShapeDtypeStruct((4, rw), F32), jax.ShapeDtypeStruct((1, rw), F32),
                   jax.ShapeDtypeStruct((nh, LRU_BLOCK, 2 * LRU_BLOCK), F32),
                   jax.ShapeDtypeStruct((nh, 2 * LRU_BLOCK), F32), jax.ShapeDtypeStruct((1, rw), F32)],
        scratch_shapes=[pltpu.VMEM((ts, rw), F32), pltpu.VMEM((ts, rw), F32), pltpu.VMEM((ts, rw), F32),
                        pltpu.VMEM((1, rw), F32), pltpu.VMEM((SUB, rw), F32), pltpu.VMEM((SUB, rw), F32)],
        compiler_params=_cp("arbitrary"))(h, h, h, hs, hs, dy, cw, cb, wg, bg, lam)


def _row_tile(rows, cols, mult, elems=ELEMS_PER_BLOCK):
    cap = max(mult, elems // cols)
    best = None
    for cand in range(mult, min(rows, cap) + 1, mult):
        if rows % cand == 0:
            best = cand
    return best if best is not None else rows


def _core_index():
    return lax.axis_index("c").astype(jnp.int32).reshape(1)


def _chip_index():
    return (2 * lax.axis_index("x") + lax.axis_index("y")).astype(jnp.int32).reshape(1)


def _add_pair(p4, r3, name):
    s, _, rows, cols = p4.shape
    tr = _row_tile(rows, cols, SUB16, COPY_ELEMS_PER_BLOCK)

    def body(c_ref, a_ref, b_ref, o_ref):
        o_ref[...] = (a_ref[...].astype(F32) + b_ref[...].astype(F32)).astype(o_ref.dtype)

    blk = pl.BlockSpec((None, tr, cols), lambda k, i, c_ref: (k, i, 0))
    return pl.pallas_call(
        body, name=name,
        grid_spec=pltpu.PrefetchScalarGridSpec(
            num_scalar_prefetch=1, grid=(s, rows // tr),
            in_specs=[pl.BlockSpec((None, None, tr, cols), lambda k, i, c_ref: (k, c_ref[0], i, 0)), blk],
            out_specs=blk),
        out_shape=jax.ShapeDtypeStruct((s, rows, cols), p4.dtype),
        compiler_params=_cp("parallel", "parallel"))(_core_index(), p4, r3)


def _add_chips(r, name):
    shape = r.shape[1:]
    r3 = r.reshape(N_CHIPS, -1, shape[-1])
    _, rows, cols = r3.shape
    tr = _row_tile(rows, cols, SUB16, COPY_ELEMS_PER_BLOCK)

    def body(r_ref, o_ref):
        s = r_ref[0].astype(F32) + r_ref[1].astype(F32)
        s = s + r_ref[2].astype(F32)
        o_ref[...] = s + r_ref[3].astype(F32)

    out = pl.pallas_call(body, name=name, grid=(rows // tr,),
                         in_specs=[pl.BlockSpec((N_CHIPS, tr, cols), lambda i: (0, i, 0))],
                         out_specs=pl.BlockSpec((tr, cols), lambda i: (i, 0)),
                         out_shape=jax.ShapeDtypeStruct((rows, cols), F32),
                         compiler_params=_cp("parallel"))(r3)
    return out.reshape(shape)


def _adamw(w3, g_mine, g_sib, m3, v3, name):
    nl, r, cols = w3.shape
    rows = r // 2
    flat = [arr.reshape(nl, 2, rows, cols) for arr in (w3, m3, v3)]
    tr = _row_tile(rows, cols, SUB)

    def body(c_ref, w_ref, gm_ref, gs_ref, m_ref, v_ref, g_ref, d_ref, mo_ref, vo_ref):
        gg = jnp.where(pl.program_id(1) == c_ref[0], gm_ref[...], gs_ref[...])
        m2 = ADAM_B1 * m_ref[...] + (1.0 - ADAM_B1) * gg
        v2 = ADAM_B2 * v_ref[...] + (1.0 - ADAM_B2) * (gg * gg)
        m_hat = m2 / (1.0 - ADAM_B1 ** ADAM_STEP)
        v_hat = v2 / (1.0 - ADAM_B2 ** ADAM_STEP)
        g_ref[...] = gg
        d_ref[...] = -ADAM_LR * (m_hat / (jnp.sqrt(v_hat) + ADAM_EPS) + ADAM_WD * w_ref[...])
        mo_ref[...] = m2
        vo_ref[...] = v2

    blk = pl.BlockSpec((None, None, tr, cols), lambda l, hh, i, c_ref: (l, hh, i, 0))
    gblk = pl.BlockSpec((None, tr, cols), lambda l, hh, i, c_ref: (l, i, 0))
    outs = pl.pallas_call(
        body, name=name,
        grid_spec=pltpu.PrefetchScalarGridSpec(
            num_scalar_prefetch=1, grid=(nl, 2, rows // tr),
            in_specs=[blk, gblk, gblk, blk, blk], out_specs=[blk] * 4),
        out_shape=[jax.ShapeDtypeStruct((nl, 2, rows, cols), F32)] * 4,
        compiler_params=_cp("parallel", "parallel", "parallel"))(_core_index(), flat[0], g_mine, g_sib, flat[1],
                                                                 flat[2])
    return tuple(o.reshape(nl, r, cols) for o in outs)


def _place(src3, layer, kind, dtype, name):
    _, r, c = src3.shape
    tr = _row_tile(r, c, SUB16, COPY_ELEMS_PER_BLOCK)
    in_spec = pl.BlockSpec((None, tr, c), lambda i, my_ref: (layer, i, 0))
    if kind == "col":
        out_spec = pl.BlockSpec((tr, c), lambda i, my_ref: (i, my_ref[0]))
        out_shape = (r, N_CHIPS * c)
    else:
        out_spec = pl.BlockSpec((None, tr, c), lambda i, my_ref: (my_ref[0], i, 0))
        out_shape = (N_CHIPS, r, c)

    def body(my_ref, i_ref, o_ref):
        o_ref[...] = i_ref[...].astype(o_ref.dtype)

    return pl.pallas_call(
        body, name=name,
        grid_spec=pltpu.PrefetchScalarGridSpec(num_scalar_prefetch=1, grid=(r // tr,), in_specs=[in_spec],
                                               out_specs=out_spec),
        out_shape=jax.ShapeDtypeStruct(out_shape, dtype),
        compiler_params=_cp("parallel"))(_chip_index(), src3)


def _half(ref, c, h):
    return ref.at[pl.ds(c * h, h)]


def _position():
    x = lax.axis_index("x")
    y = lax.axis_index("y")
    c = lax.axis_index("c")
    return x, y, c


def _peer_chip(x, y, j):
    tx = 1 - x if j & 2 else x
    ty = 1 - y if j & 1 else y
    return tx, ty


def _remote(src, dst, ssem, rsem, dev):
    return pltpu.make_async_remote_copy(src_ref=src, dst_ref=dst, send_sem=ssem, recv_sem=rsem,
                                        device_id=dev, device_id_type=pl.DeviceIdType.MESH)


_ANY = pl.BlockSpec(memory_space=pl.ANY)


def _unit_view(kind, ref, k):
    if kind == "col":
        n = ref.shape[1] // N_CHIPS
        return ref.at[:, pl.ds(pl.multiple_of(k * n, LANE), n)]
    return ref.at[k]


def _all_gather(placed, kinds):
    nt = len(placed)

    def body(*refs):
        outs = refs[nt:2 * nt]
        ssem, rsem = refs[2 * nt:]
        x, y, c = _position()
        my = 2 * x + y
        sib = (x, y, 1 - c)

        def part(t, k, core):
            view = _unit_view(kinds[t], outs[t], k)
            h = view.shape[0] // 2
            return _half(view, core, h)

        sends, fwds = [], []
        for t in range(nt):
            own = part(t, my, c)
            for j in (1, 2, 3):
                tx, ty = _peer_chip(x, y, j)
                cp = _remote(own, own, ssem.at[6 * t + j - 1], rsem.at[6 * t + j - 1], (tx, ty, c))
                cp.start()
                sends.append(cp)
        for t in range(nt):
            for j in (1, 2, 3):
                tx, ty = _peer_chip(x, y, j)
                got = part(t, 2 * tx + ty, c)
                _remote(got, got, ssem.at[6 * t + j - 1], rsem.at[6 * t + j - 1], sib).wait_recv()
                cp = _remote(got, got, ssem.at[6 * t + 2 + j], rsem.at[6 * t + 2 + j], sib)
                cp.start()
                fwds.append(cp)
        for t in range(nt):
            for j in (1, 2, 3):
                tx, ty = _peer_chip(x, y, j)
                other = part(t, 2 * tx + ty, 1 - c)
                _remote(other, other, ssem.at[6 * t + 2 + j], rsem.at[6 * t + 2 + j], sib).wait_recv()
        for cp in sends + fwds:
            cp.wait_send()

    return pl.pallas_call(
        body, name="all_gather", in_specs=[_ANY] * nt, out_specs=[_ANY] * nt,
        out_shape=[jax.ShapeDtypeStruct(p.shape, p.dtype) for p in placed],
        input_output_aliases={t: t for t in range(nt)},
        scratch_shapes=[pltpu.SemaphoreType.DMA((6 * nt,)), pltpu.SemaphoreType.DMA((6 * nt,))],
    )(*placed)


def _rider_start(refs, kinds, ssem, rsem):
    x, y, c = _position()
    my = 2 * x + y
    for u, (ref, kind) in enumerate(zip(refs, kinds)):
        own = _unit_view(kind, ref, my)
        for j in (1, 2, 3):
            tx, ty = _peer_chip(x, y, j)
            _remote(own, own, ssem.at[3 * u + j - 1], rsem.at[3 * u + j - 1], (tx, ty, c)).start()


def _rider_wait(refs, kinds, ssem, rsem):
    x, y, c = _position()
    for u, (ref, kind) in enumerate(zip(refs, kinds)):
        for j in (1, 2, 3):
            tx, ty = _peer_chip(x, y, j)
            got = _unit_view(kind, ref, 2 * tx + ty)
            _remote(got, got, ssem.at[3 * u + j - 1], rsem.at[3 * u + j - 1], (tx, ty, c)).wait()


def _rider_args(rider):
    bufs = [b for b, _ in rider]
    kinds = [k for _, k in rider]
    n = len(bufs)
    sems = [pltpu.SemaphoreType.DMA((3 * n,)), pltpu.SemaphoreType.DMA((3 * n,))] if n else []
    return bufs, kinds, [_ANY] * n, [jax.ShapeDtypeStruct(b.shape, b.dtype) for b in bufs], sems


def _d2d_stream(src4, other_half, name):
    s, _, rows, cols = src4.shape
    tr = _row_tile(rows, cols, SUB16, STREAM_ELEMS_PER_BLOCK)
    nblk = rows // tr

    nh = src4.shape[1]

    def body(c_ref, src_ref, dst_ref, ssem, rsem):
        k = pl.program_id(0)
        i = pl.program_id(1)
        x, y, c = _position()
        sib = (x, y, 1 - c)
        blk = dst_ref.at[pl.ds(pl.multiple_of((k * nblk + i) * tr, SUB16), tr)]
        cp = _remote(src_ref, blk, ssem, rsem, sib)
        cp.start()
        cp.wait_send()

        @pl.when(jnp.logical_and(k == s - 1, i == nblk - 1))
        def _():
            _remote(dst_ref, dst_ref, ssem, rsem, sib).wait_recv()

    if other_half:
        src_map = lambda k, i, c_ref: ((k * nh + 1 - c_ref[0]) * nblk + i, 0)
    else:
        src_map = lambda k, i, c_ref: (k * nh * nblk + i, 0)
    out = pl.pallas_call(
        body, name=name,
        grid_spec=pltpu.PrefetchScalarGridSpec(
            num_scalar_prefetch=1, grid=(s, nblk),
            in_specs=[pl.BlockSpec((tr, cols), src_map)], out_specs=_ANY,
            scratch_shapes=[pltpu.SemaphoreType.DMA, pltpu.SemaphoreType.DMA]),
        out_shape=jax.ShapeDtypeStruct((s * rows, cols), src4.dtype),
        compiler_params=_cp("arbitrary", "arbitrary"))(_core_index(), src4.reshape(s * nh * rows, cols))
    return out.reshape(s, rows, cols)


def _scatter_copies(srcs, lands, kinds, ssem, rsem, lsem):
    x, y, c = _position()
    my = 2 * x + y
    cps = []
    for u, (src, land, kind) in enumerate(zip(srcs, lands, kinds)):
        cps.append(pltpu.make_async_copy(_unit_view(kind, src, my), land.at[my], lsem.at[u]))
        for j in (1, 2, 3):
            tx, ty = _peer_chip(x, y, j)
            cps.append(_remote(_unit_view(kind, src, 2 * tx + ty), land.at[my], ssem.at[3 * u + j - 1],
                               rsem.at[3 * u + j - 1], (tx, ty, c)))
    return cps


def _scatter_args(scatter):
    srcs = [s for s, _ in scatter]
    kinds = [k for _, k in scatter]
    n = len(srcs)
    shapes = [jax.ShapeDtypeStruct((N_CHIPS, s.shape[0], s.shape[1] // N_CHIPS) if k == "col" else s.shape, s.dtype)
              for s, k in scatter]
    sems = [pltpu.SemaphoreType.DMA((3 * n,)), pltpu.SemaphoreType.DMA((3 * n,)),
            pltpu.SemaphoreType.DMA((n,))] if n else []
    return srcs, kinds, [_ANY] * n, shapes, sems


def _rs_scatter(scatter):
    srcs, kinds, specs, shapes, sems = _scatter_args(scatter)
    n = len(srcs)

    def body(*refs):
        cps = _scatter_copies(refs[:n], refs[n:2 * n], kinds, *refs[2 * n:])
        for cp in cps:
            cp.start()
        for cp in cps:
            cp.wait()

    return pl.pallas_call(body, name="rs_scatter", in_specs=specs, out_specs=specs, out_shape=shapes,
                          scratch_shapes=sems)(*srcs)


SMALL = (("sc_conv_w", True), ("sc_conv_b", False), ("lru_b_in", True), ("lru_conv_w", True),
         ("lru_conv_b", True), ("lru_b_gate", True), ("lru_lambda", True), ("ffn_conv_w", True),
         ("ffn_conv_b", False), ("ln_g", True), ("ln_b", True))
PACK_ROW_MULT = 2 * SUB16


def _pack_rows(shapes):
    n = sum(math.prod(shapes[name]) for name, _ in SMALL)
    rows = -(-n // 128)
    return -(-rows // PACK_ROW_MULT) * PACK_ROW_MULT


def _pack_local(vals, shapes):
    flat = jnp.concatenate([vals[name].reshape(-1) for name, _ in SMALL])
    rows = _pack_rows(shapes)
    return jnp.pad(flat, (0, rows * 128 - flat.shape[0])).reshape(rows, 128)


def _unpack_local(pack, shapes):
    flat = pack.reshape(-1)
    out, off = {}, 0
    for name, _ in SMALL:
        n = math.prod(shapes[name])
        out[name] = flat[off:off + n].reshape(shapes[name])
        off += n
    return out


def _pack_slots(fulls, shapes):
    parts = []
    for name, sharded in SMALL:
        v = fulls[name]
        if sharded:
            ns = shapes[name][-1]
            v = jnp.moveaxis(v.reshape(v.shape[:-1] + (N_CHIPS, ns)), -2, 0).reshape(N_CHIPS, -1)
        else:
            v = jnp.broadcast_to(v.reshape(1, -1), (N_CHIPS, v.size))
        parts.append(v)
    flat = jnp.concatenate(parts, axis=1)
    rows = _pack_rows(shapes)
    return jnp.pad(flat, ((0, 0), (0, rows * 128 - flat.shape[1]))).reshape(N_CHIPS, rows, 128)


def _unpack_slots(packs, shapes):
    flat = packs.reshape(N_CHIPS, -1)
    out, off = {}, 0
    for name, sharded in SMALL:
        n = math.prod(shapes[name])
        if sharded:
            seg = flat[:, off:off + n].reshape((N_CHIPS,) + tuple(shapes[name]))
            seg = jnp.moveaxis(seg, 0, -2)
            out[name] = seg.reshape(seg.shape[:-2] + (N_CHIPS * shapes[name][-1],))
        off += n
    return out


WEIGHTS = ("sc_w_in", "sc_conv_w", "sc_conv_b", "sc_w_out", "lru_w_in", "lru_b_in", "lru_conv_w", "lru_conv_b",
           "lru_w_gate", "lru_b_gate", "lru_lambda", "lru_w_out", "ffn_w_up", "ffn_conv_w", "ffn_conv_b",
           "ffn_w_down", "ln_g", "ln_b")
GATHER_KIND = {"sc_w_in": "col", "sc_w_out": "lead", "lru_w_in": "col", "lru_w_out": "lead", "ffn_w_up": "col",
               "ffn_w_down": "lead"}


def kernel(x, sc_w_in, sc_conv_w, sc_conv_b, sc_w_out, lru_w_in, lru_b_in, lru_conv_w, lru_conv_b, lru_w_gate, lru_b_gate, lru_lambda, lru_w_out, ffn_w_up, ffn_conv_w, ffn_conv_b, ffn_w_down, ln_g, ln_b, loss_target, m_sc_w_in, m_sc_conv_w, m_sc_conv_b, m_sc_w_out, m_lru_w_in, m_lru_b_in, m_lru_conv_w, m_lru_conv_b, m_lru_w_gate, m_lru_b_gate, m_lru_lambda, m_lru_w_out, m_ffn_w_up, m_ffn_conv_w, m_ffn_conv_b, m_ffn_w_down, m_ln_g, m_ln_b, v_sc_w_in, v_sc_conv_w, v_sc_conv_b, v_sc_w_out, v_lru_w_in, v_lru_b_in, v_lru_conv_w, v_lru_conv_b, v_lru_w_gate, v_lru_b_gate, v_lru_lambda, v_lru_w_out, v_ffn_w_up, v_ffn_conv_w, v_ffn_conv_b, v_ffn_w_down, v_ln_g, v_ln_b):
    w = dict(zip(WEIGHTS, (sc_w_in, sc_conv_w, sc_conv_b, sc_w_out, lru_w_in, lru_b_in, lru_conv_w, lru_conv_b,
                           lru_w_gate, lru_b_gate, lru_lambda, lru_w_out, ffn_w_up, ffn_conv_w, ffn_conv_b,
                           ffn_w_down, ln_g, ln_b)))
    mom = dict(zip(WEIGHTS, (m_sc_w_in, m_sc_conv_w, m_sc_conv_b, m_sc_w_out, m_lru_w_in, m_lru_b_in, m_lru_conv_w,
                             m_lru_conv_b, m_lru_w_gate, m_lru_b_gate, m_lru_lambda, m_lru_w_out, m_ffn_w_up,
                             m_ffn_conv_w, m_ffn_conv_b, m_ffn_w_down, m_ln_g, m_ln_b)))
    vel = dict(zip(WEIGHTS, (v_sc_w_in, v_sc_conv_w, v_sc_conv_b, v_sc_w_out, v_lru_w_in, v_lru_b_in, v_lru_conv_w,
                             v_lru_conv_b, v_lru_w_gate, v_lru_b_gate, v_lru_lambda, v_lru_w_out, v_ffn_w_up,
                             v_ffn_conv_w, v_ffn_conv_b, v_ffn_w_down, v_ln_g, v_ln_b)))
    bd, seq, d = x.shape
    t = bd * seq
    small_shapes = {name: w[name].shape for name, _ in SMALL}

    w_pack = _pack_local(w, small_shapes)
    gate_shape = w["lru_w_gate"].shape
    bufs = {(n, l): (_place(w[n], l, k, BF16, "place_w"), k)
            for n, k in GATHER_KIND.items() for l in range(w[n].shape[0])}
    bufs["gate"] = (_place(w["lru_w_gate"].reshape(1, -1, gate_shape[-1]), 0, "lead", BF16, "place_w"), "lead")
    bufs["pack"] = (_place(w_pack[None], 0, "lead", F32, "place_w"), "lead")

    def layer_keys(i):
        mixer = ("sc_w_in", "sc_w_out") if i % 2 == 0 else ("lru_w_in", "lru_w_out")
        return [(mixer[0], i // 2), (mixer[1], i // 2)], [("ffn_w_up", i), ("ffn_w_down", i)]

    def gathered(keys, arrays):
        for key, arr in zip(keys, arrays):
            bufs[key] = (arr, bufs[key][1])

    def wt(name, l):
        arr = bufs[(name, l)][0]
        return arr.reshape(1, -1, arr.shape[-1])

    first = layer_keys(0)[0] + layer_keys(0)[1] + ["gate", "pack"]
    gathered(first, _all_gather([bufs[k][0] for k in first], [bufs[k][1] for k in first]))
    full = _unpack_slots(bufs["pack"][0], small_shapes)
    full["sc_conv_b"] = sc_conv_b
    full["ffn_conv_b"] = ffn_conv_b
    wg_full = jnp.moveaxis(bufs["gate"][0].reshape((N_CHIPS,) + gate_shape), 0, -2)
    wg_full = wg_full.reshape(wg_full.shape[:-2] + (2 * LRU_BLOCK,))
    f = N_CHIPS * w["ffn_w_down"].shape[1]
    rw = N_CHIPS * w["lru_w_out"].shape[1]

    x0 = x.reshape(t, d)
    xb = x0.astype(BF16)
    cur, cur_b = x0, xb
    saved = []

    for i in range(DEPTH):
        j = i // 2
        s = {"xb": cur_b}
        mixer_next, ffn_next = layer_keys(i + 1) if i + 1 < DEPTH else ([], [])
        behind_out, behind_up, behind_down = mixer_next[1:], mixer_next[:1] + ffn_next[:1], ffn_next[1:]
        if i % 2 == 0:
            h = _mm_nn(cur_b, wt("sc_w_in", j), 0, None, 3 * d, "sc_in")
            q = _sc_fwd(h, full["sc_conv_w"][j], full["sc_conv_b"][j][None], seq, "sc_fwd")
            z1, x1, x1b, *arrived = _mm_nn_ln(q, wt("sc_w_out", j), 0, cur, full["ln_g"][i, 0][None],
                                              full["ln_b"][i, 0][None], "sc_out_ln",
                                              rider=[bufs[k] for k in behind_out])
        else:
            h = _mm_nn(cur_b, wt("lru_w_in", j), 0, full["lru_b_in"][j][None], 2 * rw, "lru_in")
            hs, q = _lru_fwd(h, full["lru_conv_w"][j], full["lru_conv_b"][j][None], wg_full[j],
                             full["lru_b_gate"][j], full["lru_lambda"][j][None], seq, "lru_fwd")
            s["hs"] = hs
            z1, x1, x1b, *arrived = _mm_nn_ln(q, wt("lru_w_out", j), 0, cur, full["ln_g"][i, 0][None],
                                              full["ln_b"][i, 0][None], "lru_out_ln",
                                              rider=[bufs[k] for k in behind_out])
        gathered(behind_out, arrived)
        s.update(h=h, q=q, z1=z1, x1b=x1b)
        h3, pre3, act, *arrived = _ffn_up(x1b, wt("ffn_w_up", i), 0, full["ffn_conv_w"][i],
                                          full["ffn_conv_b"][i][None], seq, "ffn_up",
                                          rider=[bufs[k] for k in behind_up])
        gathered(behind_up, arrived)
        z2, x2, x2b, *arrived = _mm_nn_ln(act, wt("ffn_w_down", i), 0, x1, full["ln_g"][i, 1][None],
                                          full["ln_b"][i, 1][None], "ffn_down_ln",
                                          rider=[bufs[k] for k in behind_down])
        gathered(behind_down, arrived)
        s.update(h3=h3, pre3=pre3, act=act, z2=z2)
        saved.append(s)
        cur, cur_b = x2, x2b

    dcur, loss = cur, None

    def pair_sum(p, kind):
        lead = kind == "lead"
        p4 = p.reshape(N_CHIPS if lead else 1, 2, -1, p.shape[-1])
        chip_sum = _add_pair(p4, _d2d_stream(p4, True, "rs_swap"), "rs_add_pair")
        return chip_sum if lead else chip_sum[0]

    gp = {n: [None] * w[n].shape[0] for n, _ in SMALL}
    gp["lru_w_gate"] = [None] * gate_shape[0]
    landed, pending = {}, []
    for i in reversed(range(DEPTH)):
        j = i // 2
        s = saved[i]
        mixer_keys, ffn_keys = layer_keys(i)
        if i == DEPTH - 1:
            dz2, dz2b, dg, db, loss_parts = _ln_bwd(dcur, s["z2"], full["ln_g"][i, 1][None], "ln_bwd_loss",
                                                    target=loss_target.reshape(t, d))
            loss = lax.psum(jnp.sum(loss_parts), MESH_AXES)
        else:
            dz2, dz2b, dg, db = _ln_bwd(dcur, s["z2"], full["ln_g"][i, 1][None], "ln_bwd")
        gp["ln_g"][i] = [None, dg[0]]
        gp["ln_b"][i] = [None, db[0]]
        p_down = _mm_tn(s["act"], dz2b[None], f // 2, d, "ffn_down_dw").reshape(N_CHIPS, -1, d)
        dh3, dcwg, dcwv, dcbg, dcbv, *lands = _ffn_down_bwd(
            dz2b, wt("ffn_w_down", i), 0, s["h3"], s["pre3"], full["ffn_conv_w"][i], seq, "ffn_down_bwd",
            scatter=[(chip_sum, kind) for _, chip_sum, kind in pending])
        landed.update({key: land for (key, _, _), land in zip(pending, lands)})
        gp["ffn_conv_w"][i] = jnp.concatenate([dcwg, dcwv], axis=1)
        gp["ffn_conv_b"][i] = jnp.concatenate([dcbg[0], dcbv[0]])
        dx1 = _mm_nt_res(dh3, wt("ffn_w_up", i), 0, dz2, f // 2, "ffn_up_dx")
        p_up = _mm_tn(s["x1b"], dh3, d, f // 2, "ffn_up_dw")
        ffn_partials = ((ffn_keys[0], p_up, "col"), (ffn_keys[1], p_down, "lead"))
        early = [(key, pair_sum(p, kind), kind) for key, p, kind in ffn_partials] if i == 0 else []
        early_scatter = [(chip_sum, kind) for _, chip_sum, kind in early]
        dz1, dz1b, dg, db = _ln_bwd(dx1, s["z1"], full["ln_g"][i, 0][None], "ln_bwd")
        gp["ln_g"][i][0] = dg[0]
        gp["ln_b"][i][0] = db[0]
        gp["ln_g"][i] = jnp.stack(gp["ln_g"][i])
        gp["ln_b"][i] = jnp.stack(gp["ln_b"][i])
        if i % 2 == 0:
            dq = _mm_nt(dz1b, wt("sc_w_out", j), 0, d, "sc_out_dx")
            p_out = _mm_tn(s["q"], dz1b[None], d, d, "sc_out_dw")
            dh3, dcw, dcb = _sc_bwd(s["h"], dq, full["sc_conv_w"][j], full["sc_conv_b"][j][None], seq, "sc_bwd")
            gp["sc_conv_w"][j] = dcw
            gp["sc_conv_b"][j] = dcb[0]
            res = _mm_nt_res(dh3, wt("sc_w_in", j), 0, dz1, d, "sc_in_dx", scatter=early_scatter)
            dcur, lands = (res[0], res[1:]) if early else (res, [])
            landed.update({key: land for (key, _, _), land in zip(early, lands)})
            p_in = _mm_tn(s["xb"], dh3, d, d, "sc_in_dw")
        else:
            dq = _mm_nt(dz1b, wt("lru_w_out", j), 0, rw, "lru_out_dx")
            p_out = _mm_tn(s["q"], dz1b[None], rw, d, "lru_out_dw")
            dh3, dbin, dcw, dcb, dwg, dbg, dlam = _lru_bwd(
                s["h"], s["hs"], dq, full["lru_conv_w"][j], full["lru_conv_b"][j][None], wg_full[j],
                full["lru_b_gate"][j], full["lru_lambda"][j][None], seq, "lru_bwd")
            gp["lru_b_in"][j] = dbin[0]
            gp["lru_conv_w"][j] = dcw
            gp["lru_conv_b"][j] = dcb[0]
            gp["lru_w_gate"][j] = dwg
            gp["lru_b_gate"][j] = dbg
            gp["lru_lambda"][j] = dlam[0]
            dcur = _mm_nt_res(dh3, wt("lru_w_in", j), 0, dz1, rw, "lru_in_dx")
            p_in = _mm_tn(s["xb"], dh3, d, rw, "lru_in_dw")
        layer_partials = ((mixer_keys[0], p_in, "col"), (mixer_keys[1], p_out.reshape(N_CHIPS, -1, d), "lead"))
        layer_partials += () if early else ffn_partials
        pending = [(key, pair_sum(p, kind), kind) for key, p, kind in layer_partials]
    grad_x = dcur.reshape(bd, seq, d)
    gp = {n: jnp.stack(v) for n, v in gp.items()}

    gate = gp["lru_w_gate"]
    gate = jnp.moveaxis(gate.reshape(gate.shape[:-1] + (N_CHIPS, gate_shape[-1])), -2, 0)
    gate = gate.astype(BF16).reshape(N_CHIPS, -1, gate_shape[-1])
    pending += [("gate", pair_sum(gate, "lead"), "lead"),
                ("pack", pair_sum(_pack_slots(gp, small_shapes), "lead"), "lead")]
    lands = _rs_scatter([(chip_sum, kind) for _, chip_sum, kind in pending])
    landed.update({key: land for (key, _, _), land in zip(pending, lands)})

    def update(keys, w3, m3, v3):
        g_mine = jnp.stack([_add_chips(landed[k], "rs_add_chips") for k in keys])
        g_sib = _d2d_stream(g_mine.reshape(1, 1, -1, g_mine.shape[-1]), False, "rs_share").reshape(g_mine.shape)
        return _adamw(w3, g_mine, g_sib, m3, v3, "adamw")

    g_out, d_out, m_out, v_out = {}, {}, {}, {}
    for n in GATHER_KIND:
        outs = update([(n, l) for l in range(w[n].shape[0])], w[n], mom[n], vel[n])
        g_out[n], d_out[n], m_out[n], v_out[n] = outs
    as_rows = lambda a: a.reshape(1, -1, a.shape[-1])
    outs = update(["gate"], as_rows(w["lru_w_gate"]), as_rows(mom["lru_w_gate"]), as_rows(vel["lru_w_gate"]))
    g_out["lru_w_gate"], d_out["lru_w_gate"], m_out["lru_w_gate"], v_out["lru_w_gate"] = (
        o.reshape(gate_shape) for o in outs)
    packs = update(["pack"], w_pack[None], _pack_local(mom, small_shapes)[None], _pack_local(vel, small_shapes)[None])
    for dst, pack in zip((g_out, d_out, m_out, v_out), packs):
        dst.update(_unpack_local(pack[0], small_shapes))

    return (loss, grad_x, *[g_out[n] for n in WEIGHTS], *[d_out[n] for n in WEIGHTS],
            *[m_out[n] for n in WEIGHTS], *[v_out[n] for n in WEIGHTS])
```
